```python
import math
import jax, jax.numpy as jnp
from jax import lax
import numpy as np

D_MODEL = 1024
BATCH = 8
SEQ = 4096
DEPTH = 2

N_META = 16
D_FF = 2816
NORM_EPS = 1e-6
NEG_INF = -1e30
SWA_Q_HEADS = 8
SWA_KV_HEADS = 2
SWA_HEAD_DIM = 64
SWA_WINDOW = 128
SWA_BLOCK = 128
REL_BUCKETS = 32
REL_MAX_DIST = 128
SWA_Q_DIM = SWA_Q_HEADS * SWA_HEAD_DIM
SWA_KV_DIM = SWA_KV_HEADS * SWA_HEAD_DIM
DN_HEADS = 4
DN_HEAD_DIM = 128
DN_DIM = DN_HEADS * DN_HEAD_DIM
DN_CONV = 4
DN_CHUNK = 64
GLA_HEADS = 4
GLA_KEY_DIM = D_MODEL // 2
GLA_VAL_DIM = D_MODEL
GLA_GATE_RANK = 16
GLA_GATE_NORM = 16.0
GLA_CHUNK = 64
N_EVEN = (DEPTH + 1) // 2
N_ODD = DEPTH // 2
EVEN_IN_SIZES = (SWA_Q_DIM, SWA_KV_DIM, SWA_KV_DIM, DN_DIM, DN_DIM, DN_DIM, DN_DIM, DN_HEADS, DN_HEADS)
EVEN_IN_DIM = sum(EVEN_IN_SIZES)
EVEN_MIX_DIM = SWA_Q_DIM + DN_DIM
ODD_IN_SIZES = (GLA_KEY_DIM, GLA_KEY_DIM, GLA_VAL_DIM, GLA_VAL_DIM, GLA_GATE_RANK)
ODD_IN_DIM = sum(ODD_IN_SIZES)

kernel_name = "hybrid_swa_deltanet_gla_macaron"


def _split(t, sizes):
    idx = np.cumsum(sizes)[:-1].tolist()
    return jnp.split(t, idx, axis=-1)


def rms_norm(x, w):
    x32 = x.astype(jnp.float32)
    y = x32 * lax.rsqrt(jnp.mean(x32 * x32, axis=-1, keepdims=True) + NORM_EPS)
    return (y * w.astype(jnp.float32)).astype(x.dtype)


def l2_norm(x):
    return x * lax.rsqrt(jnp.sum(x * x, axis=-1, keepdims=True) + 1e-6)


def swiglu(x, w_gate, w_up, w_down):
    return (jax.nn.silu(x @ w_gate) * (x @ w_up)) @ w_down


def causal_depthwise_conv(x, w):
    k = w.shape[0]
    return lax.conv_general_dilated(x, w[:, None, :].astype(x.dtype), window_strides=(1,),
                                    padding=[(k - 1, 0)], dimension_numbers=('NWC', 'WIO', 'NWC'),
                                    feature_group_count=x.shape[-1])


def t5_bucket(rel):
    n = jnp.maximum(rel, 0)
    max_exact = REL_BUCKETS // 2
    n_f = jnp.maximum(n, 1).astype(jnp.float32)
    large = max_exact + (jnp.log(n_f / max_exact) / math.log(REL_MAX_DIST / max_exact)
                         * (REL_BUCKETS - max_exact)).astype(jnp.int32)
    large = jnp.minimum(large, REL_BUCKETS - 1)
    return jnp.where(n < max_exact, n, large)


def to_chunks(t, chunk):
    pad = chunk - N_META
    t = jnp.pad(t, [(0, 0), (pad, 0)] + [(0, 0)] * (t.ndim - 2))
    b, lp = t.shape[:2]
    t = t.reshape(b, lp // chunk, chunk, *t.shape[2:])
    return jnp.moveaxis(t, 3, 1)


def from_chunks(t, chunk):
    t = jnp.moveaxis(t, 1, 3)
    b, n, c = t.shape[:3]
    return t.reshape(b, n * c, *t.shape[3:])[:, chunk - N_META:]


def _chunk_first(t):
    return jnp.moveaxis(t, 2, 0)


def sliding_window_attention(q, k, v, sinks, rel_table):
    f32 = jnp.float32
    b, l = q.shape[:2]
    g = SWA_Q_HEADS // SWA_KV_HEADS
    dh = SWA_HEAD_DIM
    pad = SWA_BLOCK - N_META
    lp = l + pad
    nb = lp // SWA_BLOCK
    padw = ((0, 0), (pad, 0), (0, 0), (0, 0))
    qb = jnp.pad(q.astype(f32), padw).reshape(b, nb, SWA_BLOCK, SWA_KV_HEADS, g, dh)
    kb = jnp.pad(k.astype(f32), padw).reshape(b, nb, SWA_BLOCK, SWA_KV_HEADS, dh)
    vb = jnp.pad(v.astype(f32), padw).reshape(b, nb, SWA_BLOCK, SWA_KV_HEADS, dh)
    prev = ((0, 0), (1, 0), (0, 0), (0, 0), (0, 0))
    k_band = jnp.concatenate([jnp.pad(kb[:, :-1], prev), kb], axis=2)
    v_band = jnp.concatenate([jnp.pad(vb[:, :-1], prev), vb], axis=2)
    k_meta = k[:, :N_META].astype(f32)
    v_meta = v[:, :N_META].astype(f32)
    scale = dh ** -0.5
    s_band = jnp.einsum('bnqhgd,bnkhd->bhgnqk', qb, k_band) * scale
    s_meta = jnp.einsum('bnqhgd,bmhd->bhgnqm', qb, k_meta) * scale
    blk = jnp.arange(nb)[:, None]
    pos_q = blk * SWA_BLOCK + jnp.arange(SWA_BLOCK)[None, :] - pad
    pos_kb = blk * SWA_BLOCK + jnp.arange(2 * SWA_BLOCK)[None, :] - SWA_BLOCK - pad
    rel_b = pos_q[:, :, None] - pos_kb[:, None, :]
    mask_b = (pos_kb[:, None, :] >= N_META) & (rel_b >= 0) & (rel_b < SWA_WINDOW)
    rel_m = pos_q[:, :, None] - jnp.arange(N_META)[None, None, :]
    mask_m = rel_m >= 0
    table = rel_table.astype(f32)

    def rel_bias(rel):
        bias = table[t5_bucket(rel)]
        return jnp.moveaxis(bias, -1, 0).reshape(SWA_KV_HEADS, g, *rel.shape)

    s_band = jnp.where(mask_b, s_band + rel_bias(rel_b), NEG_INF)
    s_meta = jnp.where(mask_m, s_meta + rel_bias(rel_m), NEG_INF)
    sink = jnp.broadcast_to(sinks.astype(f32).reshape(SWA_KV_HEADS, g, 1, 1, 1),
                            (b, SWA_KV_HEADS, g, nb, SWA_BLOCK, 1))
    p = jax.nn.softmax(jnp.concatenate([s_band, s_meta, sink], axis=-1), axis=-1)
    nk = 2 * SWA_BLOCK
    o = (jnp.einsum('bhgnqk,bnkhd->bnqhgd', p[..., :nk], v_band)
         + jnp.einsum('bhgnqm,bmhd->bnqhgd', p[..., nk:nk + N_META], v_meta))
    return o.reshape(b, lp, SWA_Q_DIM)[:, pad:]


def chunk_gated_delta_rule(q, k, v, g, beta):
    c = q.shape[-2]
    tri_incl = jnp.tril(jnp.ones((c, c), bool))
    tri_strict = jnp.tril(jnp.ones((c, c), bool), -1)
    gc = jnp.cumsum(g, axis=-1)
    diff = gc[..., :, None] - gc[..., None, :]
    gamma = jnp.where(tri_incl, jnp.exp(jnp.where(tri_incl, diff, 0.0)), 0.0)
    k_beta = k * beta[..., None]
    v_beta = v * beta[..., None]
    a_strict = jnp.where(tri_strict, jnp.einsum('bhnid,bhnjd->bhnij', k_beta, k) * gamma, 0.0)
    m = a_strict + jnp.eye(c, dtype=jnp.float32)
    u = lax.linalg.triangular_solve(m, v_beta, left_side=True, lower=True, unit_diagonal=True)
    w = lax.linalg.triangular_solve(m, k_beta * jnp.exp(gc)[..., None], left_side=True,
                                    lower=True, unit_diagonal=True)
    attn = jnp.einsum('bhnid,bhnjd->bhnij', q, k) * gamma
    q_dec = q * jnp.exp(gc)[..., None]
    k_dec = k * jnp.exp(gc[..., -1:] - gc)[..., None]
    g_last = jnp.exp(gc[..., -1])

    def step(s, xs):
        attn_n, u_n, w_n, qd_n, kd_n, gl_n = xs
        v_new = u_n - jnp.einsum('bhik,bhkv->bhiv', w_n, s)
        o = jnp.einsum('bhik,bhkv->bhiv', qd_n, s) + jnp.einsum('bhij,bhjv->bhiv', attn_n, v_new)
        s = s * gl_n[..., None, None] + jnp.einsum('bhik,bhiv->bhkv', kd_n, v_new)
        return s, o

    bsz, h, _, _, dk = q.shape
    s0 = jnp.zeros((bsz, h, dk, v.shape[-1]), jnp.float32)
    xs = tuple(_chunk_first(t) for t in (attn, u, w, q_dec, k_dec, g_last))
    _, o = lax.scan(step, s0, xs)
    return jnp.moveaxis(o, 0, 2)


def chunk_gla(q, k, v, glog):
    c = q.shape[-2]
    tri = jnp.tril(jnp.ones((c, c), bool))
    bcum = jnp.cumsum(glog, axis=-2)
    q_dec = q * jnp.exp(bcum)
    attn = jnp.where(tri, jnp.einsum('bhnik,bhnjk->bhnij', q_dec, k * jnp.exp(-bcum)), 0.0)
    o_intra = jnp.einsum('bhnij,bhnjv->bhniv', attn, v)
    b_last = bcum[..., -1:, :]
    k_dec = k * jnp.exp(b_last - bcum)
    decay = jnp.exp(b_last[..., 0, :])

    def step(s, xs):
        qd, kd, vv, dl = xs
        o = jnp.einsum('bhik,bhkv->bhiv', qd, s)
        s = s * dl[..., :, None] + jnp.einsum('bhik,bhiv->bhkv', kd, vv)
        return s, o

    bsz, h, _, _, dk = q.shape
    s0 = jnp.zeros((bsz, h, dk, v.shape[-1]), jnp.float32)
    _, o_inter = lax.scan(step, s0, tuple(_chunk_first(t) for t in (q_dec, k_dec, v, decay)))
    return o_intra + jnp.moveaxis(o_inter, 0, 2)


def gated_deltanet(q, k, v, a, bgate, z, conv_w, a_log, dt_bias, norm_w):
    f32 = jnp.float32
    bsz, l, _ = q.shape
    hd = (bsz, l, DN_HEADS, DN_HEAD_DIM)
    qkv = jax.nn.silu(causal_depthwise_conv(jnp.concatenate([q, k, v], axis=-1), conv_w))
    q, k, v = _split(qkv, (DN_DIM, DN_DIM, DN_DIM))
    q = l2_norm(q.reshape(hd).astype(f32)) * DN_HEAD_DIM ** -0.5
    k = l2_norm(k.reshape(hd).astype(f32))
    v = v.reshape(hd).astype(f32)
    beta = jax.nn.sigmoid(bgate.astype(f32))
    g = -jnp.exp(a_log.astype(f32)) * jax.nn.softplus(a.astype(f32) + dt_bias.astype(f32))
    o = chunk_gated_delta_rule(to_chunks(q, DN_CHUNK), to_chunks(k, DN_CHUNK), to_chunks(v, DN_CHUNK),
                               to_chunks(g, DN_CHUNK), to_chunks(beta, DN_CHUNK))
    o = from_chunks(o, DN_CHUNK)
    o = rms_norm(o, norm_w) * jax.nn.silu(z.reshape(hd).astype(f32))
    return o.reshape(bsz, l, DN_DIM).astype(z.dtype)


def even_mixer(h, w_in, conv_w, sinks, a_log, dt_bias, dn_norm_w, w_out, rel_table):
    bsz, l, _ = h.shape
    qa, ka, va, qb, kb, vb, zb, bb, ab = _split(h @ w_in, EVEN_IN_SIZES)
    o_a = sliding_window_attention(qa.reshape(bsz, l, SWA_Q_HEADS, SWA_HEAD_DIM),
                                   ka.reshape(bsz, l, SWA_KV_HEADS, SWA_HEAD_DIM),
                                   va.reshape(bsz, l, SWA_KV_HEADS, SWA_HEAD_DIM), sinks, rel_table)
    o_b = gated_deltanet(qb, kb, vb, ab, bb, zb, conv_w, a_log, dt_bias, dn_norm_w)
    return jnp.concatenate([o_a.astype(h.dtype), o_b.astype(h.dtype)], axis=-1) @ w_out


def odd_mixer(h, w_in, w_gate_up, b_gate, norm_w, w_out):
    f32 = jnp.float32
    bsz, l, _ = h.shape
    dk = GLA_KEY_DIM // GLA_HEADS
    dv = GLA_VAL_DIM // GLA_HEADS
    q, k, v, g, gk_low = _split(h @ w_in, ODD_IN_SIZES)
    glog = jax.nn.log_sigmoid((gk_low @ w_gate_up + b_gate).astype(f32)) / GLA_GATE_NORM
    q = q.reshape(bsz, l, GLA_HEADS, dk).astype(f32) * dk ** -0.5
    k = k.reshape(bsz, l, GLA_HEADS, dk).astype(f32)
    v = v.reshape(bsz, l, GLA_HEADS, dv).astype(f32)
    glog = glog.reshape(bsz, l, GLA_HEADS, dk)
    o = chunk_gla(to_chunks(q, GLA_CHUNK), to_chunks(k, GLA_CHUNK), to_chunks(v, GLA_CHUNK),
                  to_chunks(glog, GLA_CHUNK))
    o = from_chunks(o, GLA_CHUNK)
    o = rms_norm(o, norm_w) * jax.nn.silu(g.reshape(bsz, l, GLA_HEADS, dv).astype(f32))
    return o.reshape(bsz, l, GLA_VAL_DIM).astype(h.dtype) @ w_out


def _fwd_setup_inputs(seed: int = 0) -> dict:
    key = jax.random.key(seed)
    ks = jax.random.split(key, 24)
    f32 = jnp.float32

    def nrm(k, shape, fan_in):
        return jax.random.normal(k, shape, f32) * fan_in ** -0.5

    dt = jnp.exp(jax.random.uniform(ks[10], (N_EVEN, DN_HEADS), f32) * (math.log(0.1) - math.log(0.001))
                 + math.log(0.001))
    return {
        "x": jax.random.normal(ks[0], (BATCH, SEQ, D_MODEL), f32),
        "meta_tokens": jax.random.normal(ks[1], (N_META, D_MODEL), f32),
        "norm_w": 1.0 + 0.02 * jax.random.normal(ks[2], (DEPTH, 6, D_MODEL), f32),
        "ffn_w_gate": nrm(ks[3], (DEPTH, 2, D_MODEL, D_FF), D_MODEL),
        "ffn_w_up": nrm(ks[4], (DEPTH, 2, D_MODEL, D_FF), D_MODEL),
        "ffn_w_down": nrm(ks[5], (DEPTH, 2, D_FF, D_MODEL), D_FF),
        "rel_bias_table": 0.5 * jax.random.normal(ks[6], (REL_BUCKETS, SWA_Q_HEADS), f32),
        "even_w_in": nrm(ks[7], (N_EVEN, D_MODEL, EVEN_IN_DIM), D_MODEL),
        "even_conv_w": nrm(ks[8], (N_EVEN, DN_CONV, 3 * DN_DIM), DN_CONV),
        "swa_sinks": jax.random.normal(ks[9], (N_EVEN, SWA_Q_HEADS), f32),
        "dn_a_log": jnp.log(jax.random.uniform(ks[11], (N_EVEN, DN_HEADS), f32, 1.0, 16.0)),
        "dn_dt_bias": dt + jnp.log(-jnp.expm1(-dt)),
        "dn_norm_w": 1.0 + 0.02 * jax.random.normal(ks[12], (N_EVEN, DN_HEAD_DIM), f32),
        "even_w_out": nrm(ks[13], (N_EVEN, EVEN_MIX_DIM, D_MODEL), EVEN_MIX_DIM),
        "odd_w_in": nrm(ks[14], (N_ODD, D_MODEL, ODD_IN_DIM), D_MODEL),
        "gla_w_gate_up": nrm(ks[15], (N_ODD, GLA_GATE_RANK, GLA_KEY_DIM), GLA_GATE_RANK),
        "gla_b_gate": 0.1 * jax.random.normal(ks[16], (N_ODD, GLA_KEY_DIM), f32),
        "gla_norm_w": 1.0 + 0.02 * jax.random.normal(ks[17], (N_ODD, GLA_VAL_DIM // GLA_HEADS), f32),
        "odd_w_out": nrm(ks[18], (N_ODD, GLA_VAL_DIM, D_MODEL), GLA_VAL_DIM),
    }


def _fwd_reference(x, meta_tokens, norm_w, ffn_w_gate, ffn_w_up, ffn_w_down, rel_bias_table,
              even_w_in, even_conv_w, swa_sinks, dn_a_log, dn_dt_bias, dn_norm_w, even_w_out,
              odd_w_in, gla_w_gate_up, gla_b_gate, gla_norm_w, odd_w_out):
    bsz = x.shape[0]
    meta = jnp.broadcast_to(meta_tokens[None].astype(x.dtype), (bsz, N_META, x.shape[-1]))
    h = jnp.concatenate([meta, x], axis=1)
    for layer in range(DEPTH):
        nw = norm_w[layer]
        f = swiglu(rms_norm(h, nw[0]), ffn_w_gate[layer, 0], ffn_w_up[layer, 0], ffn_w_down[layer, 0])
        h = h + 0.5 * rms_norm(f, nw[1])
        hn = rms_norm(h, nw[2])
        if layer % 2 == 0:
            i = layer // 2
            mix = even_mixer(hn, even_w_in[i], even_conv_w[i], swa_sinks[i], dn_a_log[i], dn_dt_bias[i],
                             dn_norm_w[i], even_w_out[i], rel_bias_table)
        else:
            i = layer // 2
            mix = odd_mixer(hn, odd_w_in[i], gla_w_gate_up[i], gla_b_gate[i], gla_norm_w[i], odd_w_out[i])
        h = h + rms_norm(mix, nw[3])
        f = swiglu(rms_norm(h, nw[4]), ffn_w_gate[layer, 1], ffn_w_up[layer, 1], ffn_w_down[layer, 1])
        h = h + 0.5 * rms_norm(f, nw[5])
    return h[:, N_META:]


import jax as _jax
import jax.numpy as _jnp

TWIN_FORMAT = 'train_step'
FWD_PARAMS = ['x', 'meta_tokens', 'norm_w', 'ffn_w_gate', 'ffn_w_up', 'ffn_w_down', 'rel_bias_table', 'even_w_in', 'even_conv_w', 'swa_sinks', 'dn_a_log', 'dn_dt_bias', 'dn_norm_w', 'even_w_out', 'odd_w_in', 'gla_w_gate_up', 'gla_b_gate', 'gla_norm_w', 'odd_w_out']
TWIN_WEIGHTS = ['meta_tokens', 'norm_w', 'ffn_w_gate', 'ffn_w_up', 'ffn_w_down', 'rel_bias_table', 'even_w_in', 'even_conv_w', 'swa_sinks', 'dn_a_log', 'dn_dt_bias', 'dn_norm_w', 'even_w_out', 'odd_w_in', 'gla_w_gate_up', 'gla_b_gate', 'gla_norm_w', 'odd_w_out']
TWIN_DIFF_INPUT = 'x'
TWIN_INPUTS = ['x', 'meta_tokens', 'norm_w', 'ffn_w_gate', 'ffn_w_up', 'ffn_w_down', 'rel_bias_table', 'even_w_in', 'even_conv_w', 'swa_sinks', 'dn_a_log', 'dn_dt_bias', 'dn_norm_w', 'even_w_out', 'odd_w_in', 'gla_w_gate_up', 'gla_b_gate', 'gla_norm_w', 'odd_w_out', 'loss_target', 'm_meta_tokens', 'm_norm_w', 'm_ffn_w_gate', 'm_ffn_w_up', 'm_ffn_w_down', 'm_rel_bias_table', 'm_even_w_in', 'm_even_conv_w', 'm_swa_sinks', 'm_dn_a_log', 'm_dn_dt_bias', 'm_dn_norm_w', 'm_even_w_out', 'm_odd_w_in', 'm_gla_w_gate_up', 'm_gla_b_gate', 'm_gla_norm_w', 'm_odd_w_out', 'v_meta_tokens', 'v_norm_w', 'v_ffn_w_gate', 'v_ffn_w_up', 'v_ffn_w_down', 'v_rel_bias_table', 'v_even_w_in', 'v_even_conv_w', 'v_swa_sinks', 'v_dn_a_log', 'v_dn_dt_bias', 'v_dn_norm_w', 'v_even_w_out', 'v_odd_w_in', 'v_gla_w_gate_up', 'v_gla_b_gate', 'v_gla_norm_w', 'v_odd_w_out']
TWIN_OUTPUTS = ['loss', 'grad_x', 'grad_meta_tokens', 'grad_norm_w', 'grad_ffn_w_gate', 'grad_ffn_w_up', 'grad_ffn_w_down', 'grad_rel_bias_table', 'grad_even_w_in', 'grad_even_conv_w', 'grad_swa_sinks', 'grad_dn_a_log', 'grad_dn_dt_bias', 'grad_dn_norm_w', 'grad_even_w_out', 'grad_odd_w_in', 'grad_gla_w_gate_up', 'grad_gla_b_gate', 'grad_gla_norm_w', 'grad_odd_w_out', 'delta_meta_tokens', 'delta_norm_w', 'delta_ffn_w_gate', 'delta_ffn_w_up', 'delta_ffn_w_down', 'delta_rel_bias_table', 'delta_even_w_in', 'delta_even_conv_w', 'delta_swa_sinks', 'delta_dn_a_log', 'delta_dn_dt_bias', 'delta_dn_norm_w', 'delta_even_w_out', 'delta_odd_w_in', 'delta_gla_w_gate_up', 'delta_gla_b_gate', 'delta_gla_norm_w', 'delta_odd_w_out', 'new_m_meta_tokens', 'new_m_norm_w', 'new_m_ffn_w_gate', 'new_m_ffn_w_up', 'new_m_ffn_w_down', 'new_m_rel_bias_table', 'new_m_even_w_in', 'new_m_even_conv_w', 'new_m_swa_sinks', 'new_m_dn_a_log', 'new_m_dn_dt_bias', 'new_m_dn_norm_w', 'new_m_even_w_out', 'new_m_odd_w_in', 'new_m_gla_w_gate_up', 'new_m_gla_b_gate', 'new_m_gla_norm_w', 'new_m_odd_w_out', 'new_v_meta_tokens', 'new_v_norm_w', 'new_v_ffn_w_gate', 'new_v_ffn_w_up', 'new_v_ffn_w_down', 'new_v_rel_bias_table', 'new_v_even_w_in', 'new_v_even_conv_w', 'new_v_swa_sinks', 'new_v_dn_a_log', 'new_v_dn_dt_bias', 'new_v_dn_norm_w', 'new_v_even_w_out', 'new_v_odd_w_in', 'new_v_gla_w_gate_up', 'new_v_gla_b_gate', 'new_v_gla_norm_w', 'new_v_odd_w_out']
TWIN_LEAF_KINDS = {'loss': 'loss', 'grad_x': 'grad_x', 'grad_meta_tokens': 'grad_w', 'grad_norm_w': 'grad_w', 'grad_ffn_w_gate': 'grad_w', 'grad_ffn_w_up': 'grad_w', 'grad_ffn_w_down': 'grad_w', 'grad_rel_bias_table': 'grad_w', 'grad_even_w_in': 'grad_w', 'grad_even_conv_w': 'grad_w', 'grad_swa_sinks': 'grad_w', 'grad_dn_a_log': 'grad_w', 'grad_dn_dt_bias': 'grad_w', 'grad_dn_norm_w': 'grad_w', 'grad_even_w_out': 'grad_w', 'grad_odd_w_in': 'grad_w', 'grad_gla_w_gate_up': 'grad_w', 'grad_gla_b_gate': 'grad_w', 'grad_gla_norm_w': 'grad_w', 'grad_odd_w_out': 'grad_w', 'delta_meta_tokens': 'delta_w', 'delta_norm_w': 'delta_w', 'delta_ffn_w_gate': 'delta_w', 'delta_ffn_w_up': 'delta_w', 'delta_ffn_w_down': 'delta_w', 'delta_rel_bias_table': 'delta_w', 'delta_even_w_in': 'delta_w', 'delta_even_conv_w': 'delta_w', 'delta_swa_sinks': 'delta_w', 'delta_dn_a_log': 'delta_w', 'delta_dn_dt_bias': 'delta_w', 'delta_dn_norm_w': 'delta_w', 'delta_even_w_out': 'delta_w', 'delta_odd_w_in': 'delta_w', 'delta_gla_w_gate_up': 'delta_w', 'delta_gla_b_gate': 'delta_w', 'delta_gla_norm_w': 'delta_w', 'delta_odd_w_out': 'delta_w', 'new_m_meta_tokens': 'new_m', 'new_m_norm_w': 'new_m', 'new_m_ffn_w_gate': 'new_m', 'new_m_ffn_w_up': 'new_m', 'new_m_ffn_w_down': 'new_m', 'new_m_rel_bias_table': 'new_m', 'new_m_even_w_in': 'new_m', 'new_m_even_conv_w': 'new_m', 'new_m_swa_sinks': 'new_m', 'new_m_dn_a_log': 'new_m', 'new_m_dn_dt_bias': 'new_m', 'new_m_dn_norm_w': 'new_m', 'new_m_even_w_out': 'new_m', 'new_m_odd_w_in': 'new_m', 'new_m_gla_w_gate_up': 'new_m', 'new_m_gla_b_gate': 'new_m', 'new_m_gla_norm_w': 'new_m', 'new_m_odd_w_out': 'new_m', 'new_v_meta_tokens': 'new_v', 'new_v_norm_w': 'new_v', 'new_v_ffn_w_gate': 'new_v', 'new_v_ffn_w_up': 'new_v', 'new_v_ffn_w_down': 'new_v', 'new_v_rel_bias_table': 'new_v', 'new_v_even_w_in': 'new_v', 'new_v_even_conv_w': 'new_v', 'new_v_swa_sinks': 'new_v', 'new_v_dn_a_log': 'new_v', 'new_v_dn_dt_bias': 'new_v', 'new_v_dn_norm_w': 'new_v', 'new_v_even_w_out': 'new_v', 'new_v_odd_w_in': 'new_v', 'new_v_gla_w_gate_up': 'new_v', 'new_v_gla_b_gate': 'new_v', 'new_v_gla_norm_w': 'new_v', 'new_v_odd_w_out': 'new_v'}


def _forward(args):
    return _fwd_reference(*[args[k] for k in FWD_PARAMS])


def _output_shape():
    def fwd():
        inp = _fwd_setup_inputs(0)
        return _fwd_reference(*[inp[k] for k in FWD_PARAMS])
    out = _jax.eval_shape(fwd)
    return out.shape, out.dtype

N_MICROBATCH = 1
ADAM_LR = 0.001
ADAM_B1 = 0.9
ADAM_B2 = 0.999
ADAM_EPS = 1e-08
ADAM_WD = 0.01
ADAM_STEP = 10
PER_EXAMPLE_BATCH_AXIS = {'x': 0, 'loss_target': 0}
SHARED_INPUTS = []
_WEIGHT_DTYPES = {'meta_tokens': _jnp.float32, 'norm_w': _jnp.float32, 'ffn_w_gate': _jnp.float32, 'ffn_w_up': _jnp.float32, 'ffn_w_down': _jnp.float32, 'rel_bias_table': _jnp.float32, 'even_w_in': _jnp.float32, 'even_conv_w': _jnp.float32, 'swa_sinks': _jnp.float32, 'dn_a_log': _jnp.float32, 'dn_dt_bias': _jnp.float32, 'dn_norm_w': _jnp.float32, 'even_w_out': _jnp.float32, 'odd_w_in': _jnp.float32, 'gla_w_gate_up': _jnp.float32, 'gla_b_gate': _jnp.float32, 'gla_norm_w': _jnp.float32, 'odd_w_out': _jnp.float32}
MOMENT_SCALE = {'meta_tokens': 7.235295e-02, 'norm_w': 1.379313e+01, 'ffn_w_gate': 2.759982e-01, 'ffn_w_up': 3.099709e-01, 'ffn_w_down': 5.150839e-01, 'rel_bias_table': 4.058525e-01, 'even_w_in': 8.010127e-01, 'even_conv_w': 8.958197e-01, 'swa_sinks': 1.230459e-01, 'dn_a_log': 5.807152e+00, 'dn_dt_bias': 5.475788e+00, 'dn_norm_w': 2.832149e+00, 'even_w_out': 1.073951e+00, 'odd_w_in': 4.843111e-01, 'gla_w_gate_up': 7.684928e-02, 'gla_b_gate': 3.204965e-01, 'gla_norm_w': 8.872848e-01, 'odd_w_out': 4.305628e-01}


def _to_microbatches(a, axis):
    t = _jnp.moveaxis(a, axis, 0)
    t = t.reshape((N_MICROBATCH, t.shape[0] // N_MICROBATCH) + t.shape[1:])
    return _jnp.moveaxis(t, 1, axis + 1)


def setup_inputs(seed: int = 0) -> dict:
    inp = _fwd_setup_inputs(seed)
    key = _jax.random.fold_in(_jax.random.key(seed), 7919)
    shape, _ = _output_shape()
    out = dict(inp)
    out["loss_target"] = _jax.random.normal(_jax.random.fold_in(key, 0), shape, _jnp.float32)
    for i, name in enumerate(TWIN_WEIGHTS):
        w = inp[name].astype(_jnp.float32)
        if MOMENT_SCALE is None:
            s = _jnp.sqrt(_jnp.mean(_jnp.square(w)) + 1e-30)
        else:
            s = MOMENT_SCALE[name]
        km, kv = _jax.random.split(_jax.random.fold_in(key, i + 1))
        out[name] = w
        out["m_" + name] = s * _jax.random.normal(km, w.shape, _jnp.float32)
        out["v_" + name] = (s * s) * _jax.random.uniform(kv, w.shape, _jnp.float32, 0.5, 1.5)
    if N_MICROBATCH > 1:
        for name, axis in PER_EXAMPLE_BATCH_AXIS.items():
            out[name] = _to_microbatches(out[name], axis)
    return {'x': out['x'], 'meta_tokens': out['meta_tokens'], 'norm_w': out['norm_w'], 'ffn_w_gate': out['ffn_w_gate'], 'ffn_w_up': out['ffn_w_up'], 'ffn_w_down': out['ffn_w_down'], 'rel_bias_table': out['rel_bias_table'], 'even_w_in': out['even_w_in'], 'even_conv_w': out['even_conv_w'], 'swa_sinks': out['swa_sinks'], 'dn_a_log': out['dn_a_log'], 'dn_dt_bias': out['dn_dt_bias'], 'dn_norm_w': out['dn_norm_w'], 'even_w_out': out['even_w_out'], 'odd_w_in': out['odd_w_in'], 'gla_w_gate_up': out['gla_w_gate_up'], 'gla_b_gate': out['gla_b_gate'], 'gla_norm_w': out['gla_norm_w'], 'odd_w_out': out['odd_w_out'], 'loss_target': out['loss_target'], 'm_meta_tokens': out['m_meta_tokens'], 'm_norm_w': out['m_norm_w'], 'm_ffn_w_gate': out['m_ffn_w_gate'], 'm_ffn_w_up': out['m_ffn_w_up'], 'm_ffn_w_down': out['m_ffn_w_down'], 'm_rel_bias_table': out['m_rel_bias_table'], 'm_even_w_in': out['m_even_w_in'], 'm_even_conv_w': out['m_even_conv_w'], 'm_swa_sinks': out['m_swa_sinks'], 'm_dn_a_log': out['m_dn_a_log'], 'm_dn_dt_bias': out['m_dn_dt_bias'], 'm_dn_norm_w': out['m_dn_norm_w'], 'm_even_w_out': out['m_even_w_out'], 'm_odd_w_in': out['m_odd_w_in'], 'm_gla_w_gate_up': out['m_gla_w_gate_up'], 'm_gla_b_gate': out['m_gla_b_gate'], 'm_gla_norm_w': out['m_gla_norm_w'], 'm_odd_w_out': out['m_odd_w_out'], 'v_meta_tokens': out['v_meta_tokens'], 'v_norm_w': out['v_norm_w'], 'v_ffn_w_gate': out['v_ffn_w_gate'], 'v_ffn_w_up': out['v_ffn_w_up'], 'v_ffn_w_down': out['v_ffn_w_down'], 'v_rel_bias_table': out['v_rel_bias_table'], 'v_even_w_in': out['v_even_w_in'], 'v_even_conv_w': out['v_even_conv_w'], 'v_swa_sinks': out['v_swa_sinks'], 'v_dn_a_log': out['v_dn_a_log'], 'v_dn_dt_bias': out['v_dn_dt_bias'], 'v_dn_norm_w': out['v_dn_norm_w'], 'v_even_w_out': out['v_even_w_out'], 'v_odd_w_in': out['v_odd_w_in'], 'v_gla_w_gate_up': out['v_gla_w_gate_up'], 'v_gla_b_gate': out['v_gla_b_gate'], 'v_gla_norm_w': out['v_gla_norm_w'], 'v_odd_w_out': out['v_odd_w_out']}


def _loss(weights, diff, rest, loss_target):
    with _jax.named_scope("forward"):
        args = {**rest, TWIN_DIFF_INPUT: diff, **{k: w.astype(_WEIGHT_DTYPES[k]) for k, w in weights.items()}}
        y = _forward(args)
    with _jax.named_scope("loss_head"):
        err = _jnp.square(y.astype(_jnp.float32) - loss_target)
        return 0.5 * _jnp.sum(_jnp.mean(err, axis=-1)) if err.ndim else 0.5 * err


def _adamw(w, g, m, v):
    m = ADAM_B1 * m + (1.0 - ADAM_B1) * g
    v = ADAM_B2 * v + (1.0 - ADAM_B2) * _jnp.square(g)
    m_hat = m / (1.0 - ADAM_B1 ** ADAM_STEP)
    v_hat = v / (1.0 - ADAM_B2 ** ADAM_STEP)
    delta = -ADAM_LR * (m_hat / (_jnp.sqrt(v_hat) + ADAM_EPS) + ADAM_WD * w)
    return delta, m, v


def reference(x, meta_tokens, norm_w, ffn_w_gate, ffn_w_up, ffn_w_down, rel_bias_table, even_w_in, even_conv_w, swa_sinks, dn_a_log, dn_dt_bias, dn_norm_w, even_w_out, odd_w_in, gla_w_gate_up, gla_b_gate, gla_norm_w, odd_w_out, loss_target, m_meta_tokens, m_norm_w, m_ffn_w_gate, m_ffn_w_up, m_ffn_w_down, m_rel_bias_table, m_even_w_in, m_even_conv_w, m_swa_sinks, m_dn_a_log, m_dn_dt_bias, m_dn_norm_w, m_even_w_out, m_odd_w_in, m_gla_w_gate_up, m_gla_b_gate, m_gla_norm_w, m_odd_w_out, v_meta_tokens, v_norm_w, v_ffn_w_gate, v_ffn_w_up, v_ffn_w_down, v_rel_bias_table, v_even_w_in, v_even_conv_w, v_swa_sinks, v_dn_a_log, v_dn_dt_bias, v_dn_norm_w, v_even_w_out, v_odd_w_in, v_gla_w_gate_up, v_gla_b_gate, v_gla_norm_w, v_odd_w_out):
    given = dict(x=x, meta_tokens=meta_tokens, norm_w=norm_w, ffn_w_gate=ffn_w_gate, ffn_w_up=ffn_w_up, ffn_w_down=ffn_w_down, rel_bias_table=rel_bias_table, even_w_in=even_w_in, even_conv_w=even_conv_w, swa_sinks=swa_sinks, dn_a_log=dn_a_log, dn_dt_bias=dn_dt_bias, dn_norm_w=dn_norm_w, even_w_out=even_w_out, odd_w_in=odd_w_in, gla_w_gate_up=gla_w_gate_up, gla_b_gate=gla_b_gate, gla_norm_w=gla_norm_w, odd_w_out=odd_w_out, loss_target=loss_target, m_meta_tokens=m_meta_tokens, m_norm_w=m_norm_w, m_ffn_w_gate=m_ffn_w_gate, m_ffn_w_up=m_ffn_w_up, m_ffn_w_down=m_ffn_w_down, m_rel_bias_table=m_rel_bias_table, m_even_w_in=m_even_w_in, m_even_conv_w=m_even_conv_w, m_swa_sinks=m_swa_sinks, m_dn_a_log=m_dn_a_log, m_dn_dt_bias=m_dn_dt_bias, m_dn_norm_w=m_dn_norm_w, m_even_w_out=m_even_w_out, m_odd_w_in=m_odd_w_in, m_gla_w_gate_up=m_gla_w_gate_up, m_gla_b_gate=m_gla_b_gate, m_gla_norm_w=m_gla_norm_w, m_odd_w_out=m_odd_w_out, v_meta_tokens=v_meta_tokens, v_norm_w=v_norm_w, v_ffn_w_gate=v_ffn_w_gate, v_ffn_w_up=v_ffn_w_up, v_ffn_w_down=v_ffn_w_down, v_rel_bias_table=v_rel_bias_table, v_even_w_in=v_even_w_in, v_even_conv_w=v_even_conv_w, v_swa_sinks=v_swa_sinks, v_dn_a_log=v_dn_a_log, v_dn_dt_bias=v_dn_dt_bias, v_dn_norm_w=v_dn_norm_w, v_even_w_out=v_even_w_out, v_odd_w_in=v_odd_w_in, v_gla_w_gate_up=v_gla_w_gate_up, v_gla_b_gate=v_gla_b_gate, v_gla_norm_w=v_gla_norm_w, v_odd_w_out=v_odd_w_out)
    weights = {n: given[n] for n in TWIN_WEIGHTS}
    shared = {n: given[n] for n in SHARED_INPUTS}
    per_example = {n: given[n] for n in ['x']}
    grad_fn = _jax.value_and_grad(_loss, argnums=(0, 1))

    def one_microbatch(ex, loss_target):
        ex = dict(ex)
        diff = ex.pop(TWIN_DIFF_INPUT)
        return grad_fn(weights, diff, {**shared, **ex}, loss_target)

    if N_MICROBATCH == 1:
        loss, (grad_w, grad_x) = one_microbatch(per_example, given["loss_target"])
    else:
        def body(carry, xs):
            loss_sum, grad_sum = carry
            l_k, (gw_k, gx_k) = one_microbatch(xs[0], xs[1])
            with _jax.named_scope("update"):
                return (loss_sum + l_k, _jax.tree.map(_jnp.add, grad_sum, gw_k)), gx_k

        init = (_jnp.zeros((), _jnp.float32), _jax.tree.map(_jnp.zeros_like, weights))
        (loss, grad_w), grad_x = _jax.lax.scan(body, init, (per_example, given["loss_target"]))
    with _jax.named_scope("update"):
        delta_w, new_m, new_v = {}, {}, {}
        for n in TWIN_WEIGHTS:
            delta_w[n], new_m[n], new_v[n] = _adamw(weights[n], grad_w[n], given["m_" + n], given["v_" + n])
    return (loss, grad_x, *[grad_w[n] for n in TWIN_WEIGHTS], *[delta_w[n] for n in TWIN_WEIGHTS],
            *[new_m[n] for n in TWIN_WEIGHTS], *[new_v[n] for n in TWIN_WEIGHTS])
```

```python
import functools
import math

import numpy as np
import jax
import jax.numpy as jnp
from jax import lax
from jax.experimental import pallas as pl
from jax.experimental.pallas import tpu as pltpu

F32 = jnp.float32
BF16 = jnp.bfloat16
HI = lax.Precision.HIGHEST
MESH = pl.DeviceIdType.MESH
AXES = ("x", "y", "c")
N_DEV = 8

D_MODEL = 1024
N_META = 16
D_FF = 2816
NORM_EPS = 1e-6
NEG_INF = -1e30
SWA_Q_HEADS = 8
SWA_HEAD_DIM = 64
SWA_WINDOW = 128
SWA_BLOCK = 128
REL_BUCKETS = 32
REL_MAX_DIST = 128
DN_HEADS = 4
DN_HEAD_DIM = 128
DN_CONV = 4
GLA_HEADS = 4
GLA_DK = 128
GLA_DV = 256
GLA_GATE_RANK = 16
GLA_GATE_NORM = 16.0
CHUNK = 64
PAD = SWA_BLOCK - N_META
LANE = 128
PROJ_DIM = 3200

ADAM_LR = 0.001
ADAM_B1 = 0.9
ADAM_B2 = 0.999
ADAM_EPS = 1e-08
ADAM_WD = 0.01
ADAM_STEP = 10

FLAT_COLS = 1024
BIG = ("ffn_w_gate", "ffn_w_up", "ffn_w_down", "even_w_in", "even_w_out", "odd_w_in", "odd_w_out")
SMALL = ("meta_tokens", "norm_w", "even_conv_w", "gla_w_gate_up", "gla_b_gate", "gla_norm_w")
REPL = ("rel_bias_table", "swa_sinks", "dn_a_log", "dn_dt_bias", "dn_norm_w")
WEIGHTS = ("meta_tokens", "norm_w", "ffn_w_gate", "ffn_w_up", "ffn_w_down", "rel_bias_table", "even_w_in",
           "even_conv_w", "swa_sinks", "dn_a_log", "dn_dt_bias", "dn_norm_w", "even_w_out", "odd_w_in",
           "gla_w_gate_up", "gla_b_gate", "gla_norm_w", "odd_w_out")
SHARD_AXIS = {"ffn_w_gate": 3, "ffn_w_up": 3, "ffn_w_down": 2, "even_w_in": 2, "even_w_out": 1, "odd_w_in": 2,
              "odd_w_out": 1, "meta_tokens": 1, "norm_w": 2, "even_conv_w": 2, "gla_w_gate_up": 2,
              "gla_b_gate": 1, "gla_norm_w": 1}


def _rms(x, w):
    r = lax.rsqrt(jnp.mean(x * x, axis=-1, keepdims=True) + NORM_EPS)
    return x * r * w


def _sigmoid(x):
    return 0.5 * (jnp.tanh(0.5 * x) + 1.0)


def _silu(x):
    return x * _sigmoid(x)


def _softplus(x):
    pos = x > 0
    return jnp.where(pos, x, 0.0) + jnp.log(1.0 + jnp.exp(jnp.where(pos, -x, x)))


def _l2n(x):
    return x * lax.rsqrt(jnp.sum(x * x, axis=-1, keepdims=True) + 1e-6)


def _make_mm(cast, precision):
    def dg(a, b, ca, cb):
        if cast is not None:
            a, b = a.astype(cast), b.astype(cast)
        return lax.dot_general(a, b, (((ca,), (cb,)), ((), ())), precision=precision, preferred_element_type=F32)

    @jax.custom_vjp
    def nn(a, b):
        return dg(a, b, 1, 0)

    @jax.custom_vjp
    def nt(a, b):
        return dg(a, b, 1, 1)

    @jax.custom_vjp
    def tn(a, b):
        return dg(a, b, 0, 0)

    nn.defvjp(lambda a, b: (nn(a, b), (a, b)), lambda r, g: (nt(g, r[1]), tn(r[0], g)))
    nt.defvjp(lambda a, b: (nt(a, b), (a, b)), lambda r, g: (nn(g, r[1]), tn(g, r[0])))
    tn.defvjp(lambda a, b: (tn(a, b), (a, b)), lambda r, g: (nt(r[1], g), nn(r[0], g)))
    return nn, nt, tn


_mm, _mm_nt, _mm_tn = _make_mm(BF16, None)
_mmh, _mmh_nt, _mmh_tn = _make_mm(None, HI)


def _row_tile(n_rows, cap):
    best = LANE
    for t in range(LANE, cap + 1, LANE):
        if n_rows % t == 0:
            best = t
    return best


def _real_rows(tile_index, tm):
    row = tile_index * tm + lax.broadcasted_iota(jnp.int32, (tm, 1), 0)
    return (row >= PAD).astype(F32)


def _full(shape):
    return pl.BlockSpec(shape, lambda *_: (0,) * len(shape))


def _resident(shape):
    return pl.BlockSpec(shape, lambda *_: (0,) * len(shape), pipeline_mode=pl.Buffered(1))


def rms_mm(h, w, wmat, *, swiglu, name):
    tp, d = h.shape
    n = wmat.shape[1]
    tm = _row_tile(tp, 384)
    half = n // 2

    def body(h_ref, w_ref, wm_ref, hn_ref, *outs):
        hn = _rms(h_ref[...], w_ref[...]).astype(BF16)
        hn_ref[...] = hn
        p = jnp.dot(hn, wm_ref[...], preferred_element_type=F32)
        if swiglu:
            g, u = p[:, :half], p[:, half:]
            outs[0][...] = g.astype(BF16)
            outs[1][...] = u.astype(BF16)
            outs[2][...] = (_silu(g) * u).astype(BF16)
        else:
            outs[0][...] = p

    row = lambda width: pl.BlockSpec((tm, width), lambda i: (i, 0))
    if swiglu:
        out_shape = (jax.ShapeDtypeStruct((tp, d), BF16),) + (jax.ShapeDtypeStruct((tp, half), BF16),) * 3
        out_specs = (row(d), row(half), row(half), row(half))
    else:
        out_shape = (jax.ShapeDtypeStruct((tp, d), BF16), jax.ShapeDtypeStruct((tp, n), F32))
        out_specs = (row(d), row(n))
    return pl.pallas_call(
        body, name=name, grid=(tp // tm,),
        in_specs=[row(d), _full((1, d)), _resident((d, n))],
        out_specs=out_specs, out_shape=out_shape,
    )(h, w, wmat)


def mm_rms_res(acts, wmat, h, w, *, scale, name):
    tp, d = h.shape
    tm = _row_tile(tp, 384)
    widths = [a.shape[1] for a in acts]
    offs = [sum(widths[:i]) for i in range(len(acts))]
    na = len(acts)

    def body(*refs):
        a_refs = refs[:na]
        wm_ref, h_ref, w_ref, f_ref, ho_ref = refs[na:]
        f = None
        for a_ref, off, width in zip(a_refs, offs, widths):
            part = jnp.dot(a_ref[...].astype(BF16), wm_ref[off:off + width, :], preferred_element_type=F32)
            f = part if f is None else f + part
        f_ref[...] = f
        ho_ref[...] = h_ref[...] + scale * _rms(f, w_ref[...])

    row = lambda width: pl.BlockSpec((tm, width), lambda i: (i, 0))
    return pl.pallas_call(
        body, name=name, grid=(tp // tm,),
        in_specs=[row(wd) for wd in widths] + [_resident(wmat.shape), row(d), _full((1, d))],
        out_specs=(row(d), row(d)),
        out_shape=(jax.ShapeDtypeStruct((tp, d), F32), jax.ShapeDtypeStruct((tp, d), F32)),
    )(*acts, wmat, h, w)


def mm_rms_res_bwd(dho, f, w, wmat, gu, *, scale, name):
    tp, d = f.shape
    k = wmat.shape[0]
    tm = _row_tile(tp, 384)
    swiglu = gu is not None

    def body(*refs):
        if swiglu:
            dho_ref, f_ref, w_ref, wm_ref, g_ref, u_ref, df_ref, dw_ref, dg_ref, du_ref = refs
        else:
            dho_ref, f_ref, w_ref, wm_ref, df_ref, dw_ref, da_ref = refs
        i = pl.program_id(0)
        _, vjp = jax.vjp(lambda ff, ww: scale * _rms(ff, ww), f_ref[...], w_ref[...])
        df, dw = vjp(dho_ref[...])
        dfb = (df * _real_rows(i, tm)).astype(BF16)
        df_ref[...] = dfb

        @pl.when(i == 0)
        def _():
            dw_ref[...] = jnp.zeros_like(dw_ref)

        dw_ref[...] += dw
        da = lax.dot_general(dfb, wm_ref[...], (((1,), (1,)), ((), ())), preferred_element_type=F32)
        if swiglu:
            g = g_ref[...].astype(F32)
            u = u_ref[...].astype(F32)
            s = _sigmoid(g)
            dg_ref[...] = (da * u * s * (1.0 + g * (1.0 - s))).astype(BF16)
            du_ref[...] = (da * g * s).astype(BF16)
        else:
            da_ref[...] = da

    row = lambda width: pl.BlockSpec((tm, width), lambda i: (i, 0))
    in_specs = [row(d), row(d), _full((1, d)), _resident(wmat.shape)]
    args = [dho, f, w, wmat]
    out_shape = [jax.ShapeDtypeStruct((tp, d), BF16), jax.ShapeDtypeStruct((1, d), F32)]
    out_specs = [row(d), _full((1, d))]
    if swiglu:
        in_specs += [row(k), row(k)]
        args += list(gu)
        out_shape += [jax.ShapeDtypeStruct((tp, k), BF16)] * 2
        out_specs += [row(k), row(k)]
    else:
        out_shape += [jax.ShapeDtypeStruct((tp, k), F32)]
        out_specs += [row(k)]
    return pl.pallas_call(body, name=name, grid=(tp // tm,), in_specs=in_specs, out_specs=tuple(out_specs),
                          out_shape=tuple(out_shape))(*args)


def rms_mm_bwd(dps, wmat, h, w, dho, *, name):
    tp, d = h.shape
    tm = _row_tile(tp, 384)
    widths = [p.shape[1] for p in dps]
    offs = [sum(widths[:i]) for i in range(len(dps))]
    ndp = len(dps)

    def body(*refs):
        dp_refs = refs[:ndp]
        wm_ref, h_ref, w_ref, dho_ref, dh_ref, dw_ref = refs[ndp:]
        i = pl.program_id(0)
        dhn = None
        for dp_ref, off, width in zip(dp_refs, offs, widths):
            part = lax.dot_general(dp_ref[...].astype(BF16), wm_ref[:, off:off + width], (((1,), (1,)), ((), ())),
                                   preferred_element_type=F32)
            dhn = part if dhn is None else dhn + part
        _, vjp = jax.vjp(_rms, h_ref[...], w_ref[...])
        dx, dw = vjp(dhn)
        dh_ref[...] = (dho_ref[...] + dx) * _real_rows(i, tm)

        @pl.when(i == 0)
        def _():
            dw_ref[...] = jnp.zeros_like(dw_ref)

        dw_ref[...] += dw

    row = lambda width: pl.BlockSpec((tm, width), lambda i: (i, 0))
    return pl.pallas_call(
        body, name=name, grid=(tp // tm,),
        in_specs=[row(wd) for wd in widths] + [_resident(wmat.shape), row(d), _full((1, d)), row(d)],
        out_specs=(row(d), _full((1, d))),
        out_shape=(jax.ShapeDtypeStruct((tp, d), F32), jax.ShapeDtypeStruct((1, d), F32)),
    )(*dps, wmat, h, w, dho)


def mm_tn(a, b, *, name):
    t, m = a.shape
    n = b.shape[1]
    bm = _row_tile(m, 512)
    bn = _row_tile(n, 1536)
    bk = _row_tile(t, 1408)

    def body(a_ref, b_ref, o_ref):
        @pl.when(pl.program_id(2) == 0)
        def _():
            o_ref[...] = jnp.zeros_like(o_ref)

        o_ref[...] += lax.dot_general(a_ref[...].astype(BF16), b_ref[...].astype(BF16), (((0,), (0,)), ((), ())),
                                      preferred_element_type=F32)

    return pl.pallas_call(
        body, name=name, grid=(m // bm, n // bn, t // bk),
        in_specs=[pl.BlockSpec((bk, bm), lambda i, j, kk: (kk, i)), pl.BlockSpec((bk, bn), lambda i, j, kk: (kk, j))],
        out_specs=pl.BlockSpec((bm, bn), lambda i, j, kk: (i, j)),
        out_shape=jax.ShapeDtypeStruct((m, n), F32),
    )(a, b)


def loss_and_grad(h, target, *, name):
    tp, d = h.shape
    tm = SWA_BLOCK

    def body(h_ref, t_ref, dh_ref, loss_ref):
        i = pl.program_id(0)

        @pl.when(i == 0)
        def _():
            loss_ref[...] = jnp.zeros_like(loss_ref)
            dh_ref[...] = jnp.zeros_like(dh_ref)

        @pl.when(i > 0)
        def _():
            err = h_ref[...] - t_ref[...]
            dh_ref[...] = err * (1.0 / d)
            loss_ref[...] += 0.5 * jnp.sum(jnp.sum(err * err, axis=1, keepdims=True), axis=0, keepdims=True) * (1.0 / d)

    return pl.pallas_call(
        body, name=name, grid=(tp // tm,),
        in_specs=[pl.BlockSpec((tm, d), lambda i: (i, 0)), pl.BlockSpec((tm, d), lambda i: (jnp.maximum(i - 1, 0), 0))],
        out_specs=(pl.BlockSpec((tm, d), lambda i: (i, 0)), _full((1, 1))),
        out_shape=(jax.ShapeDtypeStruct((tp, d), F32), jax.ShapeDtypeStruct((1, 1), F32)),
    )(h, target)


def _t5_bucket_np(rel):
    n = np.maximum(rel, 0)
    max_exact = REL_BUCKETS // 2
    n_f = np.maximum(n, 1).astype(np.float32)
    large = max_exact + (np.log(n_f / np.float32(max_exact)) / np.float32(math.log(REL_MAX_DIST / max_exact))
                         * np.float32(REL_BUCKETS - max_exact)).astype(np.int32)
    large = np.minimum(large, REL_BUCKETS - 1)
    return np.where(n < max_exact, n, large).astype(np.int32)


def _swa_positions_np(n):
    i = np.arange(SWA_BLOCK)[:, None]
    j = np.arange(3 * SWA_BLOCK)[None, :]
    pos_q = n * SWA_BLOCK + i - PAD
    pos_k = np.where(j < SWA_BLOCK, j - PAD, (n - 1) * SWA_BLOCK + (j - SWA_BLOCK) - PAD)
    return pos_q, pos_k


def _swa_buckets():
    out = []
    for n in range(3):
        pos_q, pos_k = _swa_positions_np(n)
        out.append(_t5_bucket_np(pos_q - pos_k))
    return jnp.asarray(np.stack(out))


def swa_bias(table, buckets, *, name):
    nc, nq, nk = buckets.shape

    def body(tab_ref, bkt_ref, out_ref):
        for c in range(nc):
            bkt = bkt_ref[c]
            for h in range(SWA_Q_HEADS):
                acc = jnp.zeros((nq, nk), F32)
                for b in range(REL_BUCKETS):
                    acc = jnp.where(bkt == b, tab_ref[b, h], acc)
                out_ref[c, h] = acc

    return pl.pallas_call(
        body, name=name,
        in_specs=[pl.BlockSpec(memory_space=pltpu.SMEM), pl.BlockSpec(memory_space=pltpu.VMEM)],
        out_specs=pl.BlockSpec(memory_space=pltpu.VMEM),
        out_shape=jax.ShapeDtypeStruct((nc, SWA_Q_HEADS, nq, nk), F32),
    )(table, buckets)


def swa_bias_bwd(dbias, buckets, *, name):
    nc = buckets.shape[0]

    def body(db_ref, bkt_ref, out_ref):
        lane = lax.broadcasted_iota(jnp.int32, (1, LANE), 1)
        for b in range(REL_BUCKETS):
            row = jnp.zeros((1, LANE), F32)
            for c in range(nc):
                hit = bkt_ref[c] == b
                for h in range(SWA_Q_HEADS):
                    part = jnp.where(hit, db_ref[c, h], 0.0)
                    tot = jnp.sum(jnp.sum(part, axis=1, keepdims=True), axis=0, keepdims=True)
                    row = row + jnp.where(lane == h, tot, 0.0)
            out_ref[b:b + 1, :] = row

    return pl.pallas_call(
        body, name=name,
        in_specs=[pl.BlockSpec(memory_space=pltpu.VMEM), pl.BlockSpec(memory_space=pltpu.VMEM)],
        out_specs=pl.BlockSpec(memory_space=pltpu.VMEM),
        out_shape=jax.ShapeDtypeStruct((REL_BUCKETS, LANE), F32),
    )(dbias, buckets)


def _swa_block(q, kvm, kvp, kvc, bias, sinks, n):
    blk = SWA_BLOCK
    i = lax.broadcasted_iota(jnp.int32, (blk, 3 * blk), 0)
    j = lax.broadcasted_iota(jnp.int32, (blk, 3 * blk), 1)
    pos_q = n * blk + i - PAD
    is_meta = j < blk
    pos_k = jnp.where(is_meta, j - PAD, (n - 1) * blk + (j - blk) - PAD)
    rel = pos_q - pos_k
    valid = ((is_meta & (pos_k >= 0) & (pos_k < N_META) & (rel >= 0))
             | (jnp.logical_not(is_meta) & (pos_k >= N_META) & (rel >= 0) & (rel < SWA_WINDOW)))
    kv = jnp.concatenate([kvm, kvp, kvc], axis=0)
    lane = lax.broadcasted_iota(jnp.int32, (1, LANE), 1)
    halves = ((lane < SWA_HEAD_DIM).astype(F32), (lane >= SWA_HEAD_DIM).astype(F32))
    scale = SWA_HEAD_DIM ** -0.5
    outs = []
    for pair in range(SWA_Q_HEADS // 2):
        qp = q[:, pair * LANE:(pair + 1) * LANE]
        grp = pair // 2
        kg = kv[:, grp * LANE:(grp + 1) * LANE]
        vg = kv[:, (2 + grp) * LANE:(3 + grp) * LANE]
        op = None
        for hh in range(2):
            h = 2 * pair + hh
            s = _mm_nt(qp * halves[hh], kg) * scale + bias[h]
            s = jnp.where(valid, s, NEG_INF)
            sink = jnp.sum(jnp.where(lane == h, sinks, 0.0), axis=1, keepdims=True)
            m = lax.stop_gradient(jnp.maximum(jnp.max(s, axis=1, keepdims=True), sink))
            e = jnp.exp(s - m)
            den = jnp.sum(e, axis=1, keepdims=True) + jnp.exp(sink - m)
            part = _mm(e / den, vg) * halves[hh]
            op = part if op is None else op + part
        outs.append(op)
    return jnp.concatenate(outs, axis=1)


def _swa_in_specs(nb, rev):
    blk = SWA_BLOCK
    step = (lambda i: nb - 1 - i) if rev else (lambda i: i)
    return [
        pl.BlockSpec((blk, 4 * LANE), lambda i: (step(i), 0)),
        pl.BlockSpec((blk, 4 * LANE), lambda i: (0, 1)),
        pl.BlockSpec((blk, 4 * LANE), lambda i: (jnp.maximum(step(i) - 1, 0), 1)),
        pl.BlockSpec((blk, 4 * LANE), lambda i: (step(i), 1)),
        pl.BlockSpec((1, SWA_Q_HEADS, blk, 3 * blk), lambda i: (jnp.minimum(step(i), 2), 0, 0, 0)),
        _full((1, LANE)),
    ]


def swa_fwd(proj, bias, sinks, *, name):
    tp = proj.shape[0]
    nb = tp // SWA_BLOCK

    def body(q_ref, kvm_ref, kvp_ref, kvc_ref, bias_ref, sinks_ref, o_ref):
        n = pl.program_id(0)
        o_ref[...] = _swa_block(q_ref[...], kvm_ref[...], kvp_ref[...], kvc_ref[...], bias_ref[0], sinks_ref[...], n)

    return pl.pallas_call(
        body, name=name, grid=(nb,),
        in_specs=_swa_in_specs(nb, False),
        out_specs=pl.BlockSpec((SWA_BLOCK, 4 * LANE), lambda i: (i, 0)),
        out_shape=jax.ShapeDtypeStruct((tp, 4 * LANE), F32),
    )(proj, proj, proj, proj, bias, sinks)


def swa_bwd(proj, bias, sinks, do, *, name):
    tp = proj.shape[0]
    nb = tp // SWA_BLOCK
    blk = SWA_BLOCK

    def body(q_ref, kvm_ref, kvp_ref, kvc_ref, bias_ref, sinks_ref, do_ref, dq_ref, dkv_ref, dbias_ref, dsinks_ref,
             carry, meta_acc):
        i = pl.program_id(0)
        n = nb - 1 - i

        @pl.when(i == 0)
        def _():
            carry[...] = jnp.zeros_like(carry)
            meta_acc[...] = jnp.zeros_like(meta_acc)
            dsinks_ref[...] = jnp.zeros_like(dsinks_ref)

        fn = lambda q, kvm, kvp, kvc, b, s: _swa_block(q, kvm, kvp, kvc, b, s, n)
        _, vjp = jax.vjp(fn, q_ref[...], kvm_ref[...], kvp_ref[...], kvc_ref[...], bias_ref[0], sinks_ref[...])
        dq, dkvm, dkvp, dkvc, dbias, dsinks = vjp(do_ref[...])
        dq_ref[...] = dq
        meta_acc[...] += dkvm
        dkv_ref[...] = dkvc + carry[...] + jnp.where(n == 0, meta_acc[...], 0.0)
        carry[...] = dkvp
        first_visit = (n == nb - 1) | (n < 2)

        @pl.when(first_visit)
        def _():
            dbias_ref[0] = dbias

        @pl.when(jnp.logical_not(first_visit))
        def _():
            dbias_ref[0] += dbias

        dsinks_ref[...] += dsinks

    rev = lambda i: nb - 1 - i
    return pl.pallas_call(
        body, name=name, grid=(nb,),
        in_specs=_swa_in_specs(nb, True) + [pl.BlockSpec((blk, 4 * LANE), lambda i: (rev(i), 0))],
        out_specs=(pl.BlockSpec((blk, 4 * LANE), lambda i: (rev(i), 0)),
                   pl.BlockSpec((blk, 4 * LANE), lambda i: (rev(i), 0)),
                   pl.BlockSpec((1, SWA_Q_HEADS, blk, 3 * blk), lambda i: (jnp.minimum(rev(i), 2), 0, 0, 0)),
                   _full((1, LANE))),
        out_shape=(jax.ShapeDtypeStruct((tp, 4 * LANE), F32), jax.ShapeDtypeStruct((tp, 4 * LANE), F32),
                   jax.ShapeDtypeStruct((3, SWA_Q_HEADS, blk, 3 * blk), F32), jax.ShapeDtypeStruct((1, LANE), F32)),
        scratch_shapes=[pltpu.VMEM((blk, 4 * LANE), F32), pltpu.VMEM((blk, 4 * LANE), F32)],
    )(proj, proj, proj, proj, bias, sinks, do)


CONV_COL0 = 2
HALO = 8


def conv_fwd(proj, conv_w, *, name):
    tp = proj.shape[0]
    tm = _row_tile(tp, 384)
    cw = 4 * LANE
    ncol = conv_w.shape[1] // cw

    def body(x_ref, halo_ref, w_ref, y_ref, buf):
        i = pl.program_id(1)
        buf[0:HALO, :] = jnp.where(i > 0, halo_ref[...], 0.0)
        buf[HALO:, :] = x_ref[...]
        acc = None
        for j in range(DN_CONV):
            term = w_ref[j:j + 1, :] * buf[pl.ds(HALO - (DN_CONV - 1) + j, tm), :]
            acc = term if acc is None else acc + term
        y_ref[...] = acc

    return pl.pallas_call(
        body, name=name, grid=(ncol, tp // tm),
        in_specs=[pl.BlockSpec((tm, cw), lambda c, i: (i, CONV_COL0 + c)),
                  pl.BlockSpec((HALO, cw), lambda c, i: (jnp.maximum(i * (tm // HALO) - 1, 0), CONV_COL0 + c)),
                  pl.BlockSpec((DN_CONV, cw), lambda c, i: (0, c))],
        out_specs=pl.BlockSpec((tm, cw), lambda c, i: (i, c)),
        out_shape=jax.ShapeDtypeStruct((tp, ncol * cw), F32),
        scratch_shapes=[pltpu.VMEM((tm + HALO, cw), F32)],
    )(proj, proj, conv_w)


def conv_bwd(proj, conv_w, dy, *, name):
    tp = proj.shape[0]
    tm = _row_tile(tp, 384)
    cw = 4 * LANE
    ncol = conv_w.shape[1] // cw
    nt = tp // tm

    def body(x_ref, xhalo_ref, w_ref, dy_ref, dyhalo_ref, dx_ref, dw_ref, xbuf, dbuf):
        i = pl.program_id(1)
        xbuf[0:HALO, :] = jnp.where(i > 0, xhalo_ref[...], 0.0)
        xbuf[HALO:, :] = x_ref[...]
        dbuf[0:tm, :] = dy_ref[...]
        dbuf[tm:, :] = jnp.where(i < nt - 1, dyhalo_ref[...], 0.0)
        dy_t = dy_ref[...]
        acc = None
        rows = []
        for j in range(DN_CONV):
            term = w_ref[j:j + 1, :] * dbuf[pl.ds(DN_CONV - 1 - j, tm), :]
            acc = term if acc is None else acc + term
            rows.append(jnp.sum(dy_t * xbuf[pl.ds(HALO - (DN_CONV - 1) + j, tm), :], axis=0, keepdims=True))
        dx_ref[...] = acc

        @pl.when(i == 0)
        def _():
            dw_ref[...] = jnp.zeros_like(dw_ref)

        for j in range(DN_CONV):
            dw_ref[j:j + 1, :] += rows[j]

    return pl.pallas_call(
        body, name=name, grid=(ncol, nt),
        in_specs=[pl.BlockSpec((tm, cw), lambda c, i: (i, CONV_COL0 + c)),
                  pl.BlockSpec((HALO, cw), lambda c, i: (jnp.maximum(i * (tm // HALO) - 1, 0), CONV_COL0 + c)),
                  pl.BlockSpec((DN_CONV, cw), lambda c, i: (0, c)),
                  pl.BlockSpec((tm, cw), lambda c, i: (i, c)),
                  pl.BlockSpec((HALO, cw), lambda c, i: (jnp.minimum((i + 1) * (tm // HALO), tp // HALO - 1), c))],
        out_specs=(pl.BlockSpec((tm, cw), lambda c, i: (i, c)), pl.BlockSpec((DN_CONV, cw), lambda c, i: (0, c))),
        out_shape=(jax.ShapeDtypeStruct((tp, ncol * cw), F32), jax.ShapeDtypeStruct((DN_CONV, ncol * cw), F32)),
        scratch_shapes=[pltpu.VMEM((tm + HALO, cw), F32), pltpu.VMEM((tm + HALO, cw), F32)],
    )(proj, proj, conv_w, dy, dy)


def _chunk_masks():
    r = lax.broadcasted_iota(jnp.int32, (CHUNK, CHUNK), 0)
    c = lax.broadcasted_iota(jnp.int32, (CHUNK, CHUNK), 1)
    return r >= c, r > c, (r == c).astype(F32)


def _dn_chunk(y, z, small, s0, s1, s2, s3, a_log, dt_bias, norm_w, rows):
    tri_incl, tri_strict, eye = _chunk_masks()
    ltri = tri_incl.astype(F32)
    ones = jnp.ones((CHUNK, CHUNK), F32)
    lane = lax.broadcasted_iota(jnp.int32, (1, LANE), 1)
    dk = DN_HEAD_DIM
    nh = DN_HEADS
    outs, states = [], []
    for h, s_prev in enumerate((s0, s1, s2, s3)):
        q = _l2n(_silu(y[:, h * dk:(h + 1) * dk])) * dk ** -0.5
        k = _l2n(_silu(y[:, (nh + h) * dk:(nh + h + 1) * dk]))
        v = _silu(y[:, (2 * nh + h) * dk:(2 * nh + h + 1) * dk])
        b_col = jnp.sum(jnp.where(lane == h, small, 0.0), axis=1, keepdims=True)
        a_col = jnp.sum(jnp.where(lane == nh + h, small, 0.0), axis=1, keepdims=True)
        al = jnp.sum(jnp.where(lane == h, a_log, 0.0), axis=1, keepdims=True)
        dt = jnp.sum(jnp.where(lane == h, dt_bias, 0.0), axis=1, keepdims=True)
        beta = _sigmoid(b_col)
        g = -jnp.exp(al) * _softplus(a_col + dt) * rows
        gc_sq = _mmh(ltri, jnp.broadcast_to(g, (CHUNK, CHUNK)))
        gc = _mmh(ltri, jnp.broadcast_to(g, (CHUNK, dk)))
        gc_row = _mmh(ones, eye * gc_sq)
        gamma = jnp.where(tri_incl, jnp.exp(jnp.where(tri_incl, gc_sq - gc_row, 0.0)), 0.0)
        g_last = jnp.sum(g, axis=0, keepdims=True)
        k_beta = k * beta
        v_beta = v * beta
        a = jnp.where(tri_strict, _mm_nt(k_beta, k) * gamma, 0.0)
        inv = eye - a
        power = a
        for _ in range(5):
            power = _mmh(power, power)
            inv = inv + _mmh(inv, power)
        e_gc = jnp.exp(gc)
        u = _mmh(inv, v_beta)
        w = _mmh(inv, k_beta * e_gc)
        attn = _mm_nt(q, k) * gamma
        q_dec = q * e_gc
        k_dec = k * jnp.exp(g_last - gc)
        v_new = u - _mm(w, s_prev)
        o = _mm(q_dec, s_prev) + _mm(attn, v_new)
        states.append(s_prev * jnp.exp(g_last) + _mm_tn(k_dec, v_new))
        outs.append(_rms(o, norm_w) * _silu(z[:, h * dk:(h + 1) * dk]))
    return (jnp.concatenate(outs, axis=1), *states)


Z_COL = 5
SMALL_COL = 24


def _chunk_rows(n):
    row = n * CHUNK + lax.broadcasted_iota(jnp.int32, (CHUNK, 1), 0)
    return (row >= PAD).astype(F32)


def dn_fwd(y, proj, a_log, dt_bias, norm_w, *, name):
    tp = y.shape[0]
    nc = tp // CHUNK
    dk = DN_HEAD_DIM

    def body(y_ref, z_ref, small_ref, al_ref, dt_ref, nw_ref, o_ref, ssave_ref, state):
        n = pl.program_id(0)

        @pl.when(n == 0)
        def _():
            state[...] = jnp.zeros_like(state)

        ssave_ref[0] = state[...]
        out = _dn_chunk(y_ref[...], z_ref[...], small_ref[...], state[0], state[1], state[2], state[3],
                        al_ref[...], dt_ref[...], nw_ref[...], _chunk_rows(n))
        o_ref[...] = out[0]
        for h in range(DN_HEADS):
            state[h] = out[1 + h]

    return pl.pallas_call(
        body, name=name, grid=(nc,),
        in_specs=[pl.BlockSpec((CHUNK, y.shape[1]), lambda n: (n, 0)),
                  pl.BlockSpec((CHUNK, 4 * LANE), lambda n: (n, Z_COL)),
                  pl.BlockSpec((CHUNK, LANE), lambda n: (n, SMALL_COL)),
                  _full((1, LANE)), _full((1, LANE)), _full((1, LANE))],
        out_specs=(pl.BlockSpec((CHUNK, 4 * LANE), lambda n: (n, 0)),
                   pl.BlockSpec((1, DN_HEADS, dk, dk), lambda n: (n, 0, 0, 0))),
        out_shape=(jax.ShapeDtypeStruct((tp, 4 * LANE), F32), jax.ShapeDtypeStruct((nc, DN_HEADS, dk, dk), F32)),
        scratch_shapes=[pltpu.VMEM((DN_HEADS, dk, dk), F32)],
    )(y, proj, proj, a_log, dt_bias, norm_w)


def dn_bwd(y, proj, a_log, dt_bias, norm_w, ssave, do, *, name):
    tp = y.shape[0]
    nc = tp // CHUNK
    dk = DN_HEAD_DIM
    rev = lambda i: nc - 1 - i

    def body(y_ref, z_ref, small_ref, al_ref, dt_ref, nw_ref, ss_ref, do_ref,
             dy_ref, dz_ref, dsmall_ref, dal_ref, ddt_ref, dnw_ref, dstate):
        i = pl.program_id(0)
        n = nc - 1 - i

        @pl.when(i == 0)
        def _():
            dstate[...] = jnp.zeros_like(dstate)
            dal_ref[...] = jnp.zeros_like(dal_ref)
            ddt_ref[...] = jnp.zeros_like(ddt_ref)
            dnw_ref[...] = jnp.zeros_like(dnw_ref)

        rows = _chunk_rows(n)
        fn = lambda *a: _dn_chunk(*a, rows)
        _, vjp = jax.vjp(fn, y_ref[...], z_ref[...], small_ref[...], ss_ref[0, 0], ss_ref[0, 1], ss_ref[0, 2],
                         ss_ref[0, 3], al_ref[...], dt_ref[...], nw_ref[...])
        cts = vjp((do_ref[...], dstate[0], dstate[1], dstate[2], dstate[3]))
        dy_ref[...] = cts[0]
        dz_ref[...] = cts[1]
        dsmall_ref[...] = cts[2]
        for h in range(DN_HEADS):
            dstate[h] = cts[3 + h]
        dal_ref[...] += cts[7]
        ddt_ref[...] += cts[8]
        dnw_ref[...] += cts[9]

    return pl.pallas_call(
        body, name=name, grid=(nc,),
        in_specs=[pl.BlockSpec((CHUNK, y.shape[1]), lambda i: (rev(i), 0)),
                  pl.BlockSpec((CHUNK, 4 * LANE), lambda i: (rev(i), Z_COL)),
                  pl.BlockSpec((CHUNK, LANE), lambda i: (rev(i), SMALL_COL)),
                  _full((1, LANE)), _full((1, LANE)), _full((1, LANE)),
                  pl.BlockSpec((1, DN_HEADS, dk, dk), lambda i: (rev(i), 0, 0, 0)),
                  pl.BlockSpec((CHUNK, 4 * LANE), lambda i: (rev(i), 1))],
        out_specs=(pl.BlockSpec((CHUNK, y.shape[1]), lambda i: (rev(i), 0)),
                   pl.BlockSpec((CHUNK, 4 * LANE), lambda i: (rev(i), 0)),
                   pl.BlockSpec((CHUNK, LANE), lambda i: (rev(i), 0)),
                   _full((1, LANE)), _full((1, LANE)), _full((1, LANE))),
        out_shape=(jax.ShapeDtypeStruct((tp, y.shape[1]), F32), jax.ShapeDtypeStruct((tp, 4 * LANE), F32),
                   jax.ShapeDtypeStruct((tp, LANE), F32), jax.ShapeDtypeStruct((1, LANE), F32),
                   jax.ShapeDtypeStruct((1, LANE), F32), jax.ShapeDtypeStruct((1, LANE), F32)),
        scratch_shapes=[pltpu.VMEM((DN_HEADS, dk, dk), F32)],
    )(y, proj, proj, a_log, dt_bias, norm_w, ssave, do)


def _gla_chunk(q, k, v, gate, low, s0, s1, s2, s3, w_gate_up, b_gate, norm_w, rows):
    tri_incl, _, _ = _chunk_masks()
    ltri = tri_incl.astype(F32)
    dk, dv = GLA_DK, GLA_DV
    logit = _mmh(low, w_gate_up) + b_gate
    glog_all = -_softplus(-logit) * (1.0 / GLA_GATE_NORM) * rows
    ones = jnp.ones((CHUNK, dk), F32)
    outs, states = [], []
    for h, s_prev in enumerate((s0, s1, s2, s3)):
        qh = q[:, h * dk:(h + 1) * dk] * dk ** -0.5
        kh = k[:, h * dk:(h + 1) * dk]
        vh = v[:, h * dv:(h + 1) * dv]
        glog = glog_all[:, h * dk:(h + 1) * dk]
        bcum = _mmh(ltri, glog)
        q_dec = qh * jnp.exp(bcum)
        attn = jnp.where(tri_incl, _mm_nt(q_dec, kh * jnp.exp(-bcum)), 0.0)
        b_last = jnp.sum(glog, axis=0, keepdims=True)
        k_dec = kh * jnp.exp(b_last - bcum)
        decay_col = jnp.exp(_mmh_tn(glog, ones))
        decay = jnp.concatenate([decay_col, decay_col], axis=1)
        o = _mm(attn, vh) + _mm(q_dec, s_prev)
        states.append(s_prev * decay + _mm_tn(k_dec, vh))
        outs.append(_rms(o, norm_w) * _silu(gate[:, h * dv:(h + 1) * dv]))
    return (jnp.concatenate(outs, axis=1), *states)


LOW_COL = 24


def _gla_in_specs(step):
    return [pl.BlockSpec((CHUNK, 4 * LANE), lambda i: (step(i), 0)),
            pl.BlockSpec((CHUNK, 4 * LANE), lambda i: (step(i), 1)),
            pl.BlockSpec((CHUNK, 8 * LANE), lambda i: (step(i), 1)),
            pl.BlockSpec((CHUNK, 8 * LANE), lambda i: (step(i), 2)),
            pl.BlockSpec((CHUNK, LANE), lambda i: (step(i), LOW_COL)),
            _full((LANE, 4 * LANE)), _full((1, 4 * LANE)), _full((1, GLA_DV))]


def gla_fwd(proj, w_gate_up, b_gate, norm_w, *, name):
    tp = proj.shape[0]
    nc = tp // CHUNK

    def body(q_ref, k_ref, v_ref, g_ref, low_ref, wgu_ref, bg_ref, nw_ref, o_ref, ssave_ref, state):
        n = pl.program_id(0)

        @pl.when(n == 0)
        def _():
            state[...] = jnp.zeros_like(state)

        ssave_ref[0] = state[...]
        out = _gla_chunk(q_ref[...], k_ref[...], v_ref[...], g_ref[...], low_ref[...], state[0], state[1], state[2],
                         state[3], wgu_ref[...], bg_ref[...], nw_ref[...], _chunk_rows(n))
        o_ref[...] = out[0]
        for h in range(GLA_HEADS):
            state[h] = out[1 + h]

    return pl.pallas_call(
        body, name=name, grid=(nc,),
        in_specs=_gla_in_specs(lambda i: i),
        out_specs=(pl.BlockSpec((CHUNK, 8 * LANE), lambda n: (n, 0)),
                   pl.BlockSpec((1, GLA_HEADS, GLA_DK, GLA_DV), lambda n: (n, 0, 0, 0))),
        out_shape=(jax.ShapeDtypeStruct((tp, 8 * LANE), F32),
                   jax.ShapeDtypeStruct((nc, GLA_HEADS, GLA_DK, GLA_DV), F32)),
        scratch_shapes=[pltpu.VMEM((GLA_HEADS, GLA_DK, GLA_DV), F32)],
    )(proj, proj, proj, proj, proj, w_gate_up, b_gate, norm_w)


def gla_bwd(proj, w_gate_up, b_gate, norm_w, ssave, do, *, name):
    tp = proj.shape[0]
    nc = tp // CHUNK
    rev = lambda i: nc - 1 - i

    def body(q_ref, k_ref, v_ref, g_ref, low_ref, wgu_ref, bg_ref, nw_ref, ss_ref, do_ref,
             dq_ref, dk_ref, dv_ref, dg_ref, dlow_ref, dwgu_ref, dbg_ref, dnw_ref, dstate):
        i = pl.program_id(0)
        n = nc - 1 - i

        @pl.when(i == 0)
        def _():
            dstate[...] = jnp.zeros_like(dstate)
            dwgu_ref[...] = jnp.zeros_like(dwgu_ref)
            dbg_ref[...] = jnp.zeros_like(dbg_ref)
            dnw_ref[...] = jnp.zeros_like(dnw_ref)

        rows = _chunk_rows(n)
        fn = lambda *a: _gla_chunk(*a, rows)
        _, vjp = jax.vjp(fn, q_ref[...], k_ref[...], v_ref[...], g_ref[...], low_ref[...], ss_ref[0, 0], ss_ref[0, 1],
                         ss_ref[0, 2], ss_ref[0, 3], wgu_ref[...], bg_ref[...], nw_ref[...])
        cts = vjp((do_ref[...], dstate[0], dstate[1], dstate[2], dstate[3]))
        dq_ref[...] = cts[0]
        dk_ref[...] = cts[1]
        dv_ref[...] = cts[2]
        dg_ref[...] = cts[3]
        dlow_ref[...] = cts[4]
        for h in range(GLA_HEADS):
            dstate[h] = cts[5 + h]
        dwgu_ref[...] += cts[9]
        dbg_ref[...] += cts[10]
        dnw_ref[...] += cts[11]

    chunk = lambda width: pl.BlockSpec((CHUNK, width), lambda i: (rev(i), 0))
    return pl.pallas_call(
        body, name=name, grid=(nc,),
        in_specs=_gla_in_specs(rev) + [pl.BlockSpec((1, GLA_HEADS, GLA_DK, GLA_DV), lambda i: (rev(i), 0, 0, 0)),
                                       chunk(8 * LANE)],
        out_specs=(chunk(4 * LANE), chunk(4 * LANE), chunk(8 * LANE), chunk(8 * LANE), chunk(LANE),
                   _full((LANE, 4 * LANE)), _full((1, 4 * LANE)), _full((1, GLA_DV))),
        out_shape=(jax.ShapeDtypeStruct((tp, 4 * LANE), F32), jax.ShapeDtypeStruct((tp, 4 * LANE), F32),
                   jax.ShapeDtypeStruct((tp, 8 * LANE), F32), jax.ShapeDtypeStruct((tp, 8 * LANE), F32),
                   jax.ShapeDtypeStruct((tp, LANE), F32), jax.ShapeDtypeStruct((LANE, 4 * LANE), F32),
                   jax.ShapeDtypeStruct((1, 4 * LANE), F32), jax.ShapeDtypeStruct((1, GLA_DV), F32)),
        scratch_shapes=[pltpu.VMEM((GLA_HEADS, GLA_DK, GLA_DV), F32)],
    )(proj, proj, proj, proj, proj, w_gate_up, b_gate, norm_w, ssave, do)


def _even_proj_weight(w_in):
    hd = SWA_HEAD_DIM
    k0, k1 = w_in[:, 512:512 + hd], w_in[:, 512 + hd:640]
    v0, v1 = w_in[:, 640:640 + hd], w_in[:, 640 + hd:768]
    zeros = jnp.zeros((w_in.shape[0], LANE - 2 * DN_HEADS), w_in.dtype)
    return jnp.concatenate([w_in[:, :512], k0, k0, k1, k1, v0, v0, v1, v1, w_in[:, 768:2816], w_in[:, 2816:2824], zeros],
                           axis=1)


def _even_proj_weight_grad(dw):
    hd = SWA_HEAD_DIM
    c = lambda i: dw[:, 512 + i * hd:512 + (i + 1) * hd]
    return jnp.concatenate([dw[:, :512], c(0) + c(1), c(2) + c(3), c(4) + c(5), c(6) + c(7), dw[:, 1024:3072],
                            dw[:, 3072:3080]], axis=1)


def _ffn_fwd(h, nw_in, nw_out, w_gu, w_down, tag):
    hn, g, u, a = rms_mm(h, nw_in, w_gu, swiglu=True, name=f"ffn_up_{tag}")
    f, h_out = mm_rms_res([a], w_down, h, nw_out, scale=0.5, name=f"ffn_down_{tag}")
    return h_out, (h, hn, g, u, a, f)


def _ffn_bwd(dho, saved, nw_in, nw_out, w_gu, w_down, tag):
    h, hn, g, u, a, f = saved
    df, dnw_out, dg, du = mm_rms_res_bwd(dho, f, nw_out, w_down, (g, u), scale=0.5, name=f"ffn_down_bwd_{tag}")
    dh, dnw_in = rms_mm_bwd([dg, du], w_gu, h, nw_in, dho, name=f"ffn_up_bwd_{tag}")
    dw_gate = mm_tn(hn, dg, name=f"ffn_dwg_{tag}")
    dw_up = mm_tn(hn, du, name=f"ffn_dwu_{tag}")
    dw_down = mm_tn(a, df, name=f"ffn_dwd_{tag}")
    return dh, dnw_in, dnw_out, dw_gate, dw_up, dw_down


def local_step(x, target, wts):
    seq, d = x.shape
    row = lambda v: v.reshape(1, -1)
    lane_row = lambda v: jnp.pad(v.reshape(1, -1), ((0, 0), (0, LANE - v.size)))
    nw = wts["norm_w"]
    h = jnp.concatenate([jnp.zeros((PAD, d), F32), wts["meta_tokens"], x], axis=0)
    w_gu = [[jnp.concatenate([wts["ffn_w_gate"][l, j], wts["ffn_w_up"][l, j]], axis=1) for j in range(2)]
            for l in range(2)]
    w_down = wts["ffn_w_down"]
    w_in = [_even_proj_weight(wts["even_w_in"][0]),
            jnp.pad(wts["odd_w_in"][0], ((0, 0), (0, PROJ_DIM - wts["odd_w_in"].shape[2])))]
    w_out = [wts["even_w_out"][0], wts["odd_w_out"][0]]
    buckets = _swa_buckets()
    bias = swa_bias(wts["rel_bias_table"], buckets, name="swa_bias")
    sinks = lane_row(wts["swa_sinks"])
    a_log, dt_bias = lane_row(wts["dn_a_log"]), lane_row(wts["dn_dt_bias"])
    dn_norm_w = row(wts["dn_norm_w"])
    conv_w = wts["even_conv_w"][0]
    w_gate_up = jnp.pad(wts["gla_w_gate_up"][0], ((0, LANE - GLA_GATE_RANK), (0, 0)))
    b_gate, gla_norm_w = row(wts["gla_b_gate"]), row(wts["gla_norm_w"])

    saved = []
    for l in range(2):
        h, s_a = _ffn_fwd(h, row(nw[l, 0]), row(nw[l, 1]), w_gu[l][0], w_down[l, 0], f"{l}a")
        h_mix = h
        hn, proj = rms_mm(h, row(nw[l, 2]), w_in[l], swiglu=False, name=f"mix_in_{l}")
        if l == 0:
            o_a = swa_fwd(proj, bias, sinks, name="swa_fwd")
            y = conv_fwd(proj, conv_w, name="conv_fwd")
            o_b, ssave = dn_fwd(y, proj, a_log, dt_bias, dn_norm_w, name="dn_fwd")
            acts, extra = [o_a, o_b], (y, ssave)
        else:
            o, ssave = gla_fwd(proj, w_gate_up, b_gate, gla_norm_w, name="gla_fwd")
            acts, extra = [o], (ssave,)
        mix, h = mm_rms_res(acts, w_out[l], h, row(nw[l, 3]), scale=1.0, name=f"mix_out_{l}")
        s_m = (h_mix, hn, proj, acts, extra, mix)
        h, s_b = _ffn_fwd(h, row(nw[l, 4]), row(nw[l, 5]), w_gu[l][1], w_down[l, 1], f"{l}b")
        saved.append((s_a, s_m, s_b))

    dh, loss = loss_and_grad(h, target, name="loss")

    grads = {}
    dnw = [[None] * 6 for _ in range(2)]
    dwg = [[None] * 2 for _ in range(2)]
    dwu = [[None] * 2 for _ in range(2)]
    dwd = [[None] * 2 for _ in range(2)]
    for l in (1, 0):
        s_a, s_m, s_b = saved[l]
        dh, dnw[l][4], dnw[l][5], dwg[l][1], dwu[l][1], dwd[l][1] = _ffn_bwd(
            dh, s_b, row(nw[l, 4]), row(nw[l, 5]), w_gu[l][1], w_down[l, 1], f"{l}b")
        h_mix, hn, proj, acts, extra, mix = s_m
        dmix, dnw[l][3], do = mm_rms_res_bwd(dh, mix, row(nw[l, 3]), w_out[l], None, scale=1.0, name=f"mix_out_bwd_{l}")
        dw_out = jnp.concatenate([mm_tn(a, dmix, name=f"mix_dwo_{l}_{i}") for i, a in enumerate(acts)], axis=0)
        if l == 0:
            y, ssave = extra
            dq, dkv, dbias, dsinks = swa_bwd(proj, bias, sinks, do, name="swa_bwd")
            dy, dz, dsmall, da_log, ddt_bias, ddn_norm_w = dn_bwd(y, proj, a_log, dt_bias, dn_norm_w, ssave, do,
                                                                    name="dn_bwd")
            dxc, dconv_w = conv_bwd(proj, conv_w, dy, name="conv_bwd")
            dps = [dq, dkv, dxc, dz, dsmall]
            grads["rel_bias_table"] = swa_bias_bwd(dbias, buckets, name="swa_bias_bwd")[:, :SWA_Q_HEADS]
            grads["swa_sinks"] = dsinks[:, :SWA_Q_HEADS]
            grads["dn_a_log"] = da_log[:, :DN_HEADS]
            grads["dn_dt_bias"] = ddt_bias[:, :DN_HEADS]
            grads["dn_norm_w"] = ddn_norm_w
            grads["even_conv_w"] = dconv_w[None]
            grads["even_w_out"] = dw_out[None]
        else:
            (ssave,) = extra
            dq, dk, dv, dgate, dlow, dwgu, dbg, dgnw = gla_bwd(proj, w_gate_up, b_gate, gla_norm_w, ssave, do,
                                                               name="gla_bwd")
            dps = [dq, dk, dv, dgate, dlow]
            grads["gla_w_gate_up"] = dwgu[None, :GLA_GATE_RANK]
            grads["gla_b_gate"] = dbg
            grads["gla_norm_w"] = dgnw
            grads["odd_w_out"] = dw_out[None]
        dw_in = jnp.concatenate([mm_tn(hn, dp, name=f"mix_dwi_{l}_{i}") for i, dp in enumerate(dps)], axis=1)
        if l == 0:
            grads["even_w_in"] = _even_proj_weight_grad(dw_in)[None]
        else:
            grads["odd_w_in"] = dw_in[None, :, :wts["odd_w_in"].shape[2]]
        dh, dnw[l][2] = rms_mm_bwd(dps, w_in[l], h_mix, row(nw[l, 2]), dh, name=f"mix_in_bwd_{l}")
        dh, dnw[l][0], dnw[l][1], dwg[l][0], dwu[l][0], dwd[l][0] = _ffn_bwd(
            dh, s_a, row(nw[l, 0]), row(nw[l, 1]), w_gu[l][0], w_down[l, 0], f"{l}a")

    stack2 = lambda t: jnp.stack([jnp.stack(r) for r in t])
    grads["ffn_w_gate"], grads["ffn_w_up"], grads["ffn_w_down"] = stack2(dwg), stack2(dwu), stack2(dwd)
    grads["norm_w"] = jnp.stack([jnp.concatenate(r, axis=0) for r in dnw])
    grads["meta_tokens"] = dh[PAD:PAD + N_META]
    return loss[0, 0], dh[PAD + N_META:], grads


def _peer(k):
    x, y, c = (lax.axis_index(a) for a in AXES)
    flip = lambda v, bit: 1 - v if bit else v
    return (flip(x, k & 4), flip(y, k & 2), flip(c, k & 1))


def _my_index():
    x, y, c = (lax.axis_index(a) for a in AXES)
    return 4 * x + 2 * y + c


def all_gather(shard, *, name):
    def body(x_ref, out_ref, send_sems, recv_sems, local_sem):
        me = _my_index()
        mine = pltpu.make_async_copy(x_ref, out_ref.at[me], local_sem)
        mine.start()
        copies = []
        for k in range(1, N_DEV):
            cp = pltpu.make_async_remote_copy(src_ref=x_ref, dst_ref=out_ref.at[me], send_sem=send_sems.at[k - 1],
                                              recv_sem=recv_sems.at[k - 1], device_id=_peer(k), device_id_type=MESH)
            cp.start()
            copies.append(cp)
        for cp in copies:
            cp.wait_recv()
        for cp in copies:
            cp.wait_send()
        mine.wait()

    return pl.pallas_call(
        body, name=name,
        in_specs=[pl.BlockSpec(memory_space=pl.ANY)],
        out_specs=pl.BlockSpec(memory_space=pl.ANY),
        out_shape=jax.ShapeDtypeStruct((N_DEV,) + shard.shape, shard.dtype),
        scratch_shapes=[pltpu.SemaphoreType.DMA((N_DEV - 1,)), pltpu.SemaphoreType.DMA((N_DEV - 1,)),
                        pltpu.SemaphoreType.DMA],
    )(shard)


def exchange_pieces(pieces, *, name):
    def body(p_ref, out_ref, send_sems, recv_sems, local_sem):
        me = _my_index()
        mine = pltpu.make_async_copy(p_ref.at[me], out_ref.at[me], local_sem)
        mine.start()
        copies = []
        for k in range(1, N_DEV):
            px, py, pc = _peer(k)
            cp = pltpu.make_async_remote_copy(src_ref=p_ref.at[4 * px + 2 * py + pc], dst_ref=out_ref.at[me],
                                              send_sem=send_sems.at[k - 1], recv_sem=recv_sems.at[k - 1],
                                              device_id=(px, py, pc), device_id_type=MESH)
            cp.start()
            copies.append(cp)
        for cp in copies:
            cp.wait_recv()
        for cp in copies:
            cp.wait_send()
        mine.wait()

    return pl.pallas_call(
        body, name=name,
        in_specs=[pl.BlockSpec(memory_space=pl.ANY)],
        out_specs=pl.BlockSpec(memory_space=pl.ANY),
        out_shape=jax.ShapeDtypeStruct(pieces.shape, pieces.dtype),
        scratch_shapes=[pltpu.SemaphoreType.DMA((N_DEV - 1,)), pltpu.SemaphoreType.DMA((N_DEV - 1,)),
                        pltpu.SemaphoreType.DMA],
    )(pieces)


def sum_adamw(recv, w, m, v, *, name):
    r, c = w.shape
    tm = LANE
    c1 = 1.0 / (1.0 - ADAM_B1 ** ADAM_STEP)
    c2 = 1.0 / (1.0 - ADAM_B2 ** ADAM_STEP)

    def body(recv_ref, w_ref, m_ref, v_ref, g_ref, d_ref, nm_ref, nv_ref):
        g = recv_ref[0]
        for i in range(1, N_DEV):
            g = g + recv_ref[i]
        nm = ADAM_B1 * m_ref[...] + (1.0 - ADAM_B1) * g
        nv = ADAM_B2 * v_ref[...] + (1.0 - ADAM_B2) * (g * g)
        g_ref[...] = g
        nm_ref[...] = nm
        nv_ref[...] = nv
        d_ref[...] = -ADAM_LR * ((nm * c1) / (jnp.sqrt(nv * c2) + ADAM_EPS) + ADAM_WD * w_ref[...])

    row = pl.BlockSpec((tm, c), lambda i: (i, 0))
    return pl.pallas_call(
        body, name=name, grid=(r // tm,),
        in_specs=[pl.BlockSpec((N_DEV, tm, c), lambda i: (0, i, 0)), row, row, row],
        out_specs=(row,) * 4, out_shape=(jax.ShapeDtypeStruct((r, c), F32),) * 4,
    )(recv, w, m, v)


def _flat_rows(n_elems, row_multiple):
    rows = -(-n_elems // FLAT_COLS)
    return -(-rows // row_multiple) * row_multiple


def _pack(arrays, row_multiple, dtype):
    flat = jnp.concatenate([a.reshape(-1).astype(dtype) for a in arrays])
    rows = _flat_rows(flat.size, row_multiple)
    return jnp.pad(flat, (0, rows * FLAT_COLS - flat.size)).reshape(rows, FLAT_COLS)


def _unpack(flat2d, shapes):
    lead = flat2d.shape[:-2]
    flat = flat2d.reshape(lead + (-1,))
    out, off = [], 0
    for shp in shapes:
        n = int(np.prod(shp))
        out.append(flat[..., off:off + n].reshape(lead + tuple(shp)))
        off += n
    return out


def _join_shards(stacked, axis):
    moved = jnp.moveaxis(stacked, 0, axis)
    shp = list(moved.shape)
    shp[axis:axis + 2] = [shp[axis] * shp[axis + 1]]
    return moved.reshape(shp)


def _split_shards(full, axis):
    shp = list(full.shape)
    shp[axis:axis + 1] = [N_DEV, shp[axis] // N_DEV]
    return jnp.moveaxis(full.reshape(shp), axis, 0)


def kernel(x, meta_tokens, norm_w, ffn_w_gate, ffn_w_up, ffn_w_down, rel_bias_table, even_w_in, even_conv_w, swa_sinks, dn_a_log, dn_dt_bias, dn_norm_w, even_w_out, odd_w_in, gla_w_gate_up, gla_b_gate, gla_norm_w, odd_w_out, loss_target, m_meta_tokens, m_norm_w, m_ffn_w_gate, m_ffn_w_up, m_ffn_w_down, m_rel_bias_table, m_even_w_in, m_even_conv_w, m_swa_sinks, m_dn_a_log, m_dn_dt_bias, m_dn_norm_w, m_even_w_out, m_odd_w_in, m_gla_w_gate_up, m_gla_b_gate, m_gla_norm_w, m_odd_w_out, v_meta_tokens, v_norm_w, v_ffn_w_gate, v_ffn_w_up, v_ffn_w_down, v_rel_bias_table, v_even_w_in, v_even_conv_w, v_swa_sinks, v_dn_a_log, v_dn_dt_bias, v_dn_norm_w, v_even_w_out, v_odd_w_in, v_gla_w_gate_up, v_gla_b_gate, v_gla_norm_w, v_odd_w_out):
    args = locals()
    w = {n: args[n] for n in WEIGHTS}
    m = {n: args["m_" + n] for n in WEIGHTS}
    v = {n: args["v_" + n] for n in WEIGHTS}

    big = all_gather(_pack([w[n] for n in BIG], 16, BF16), name="gather_big")
    small = all_gather(_pack([w[n] for n in SMALL], 8, F32), name="gather_small")
    full = dict(w)
    for names, gathered in ((BIG, big), (SMALL, small)):
        for n, stacked in zip(names, _unpack(gathered, [w[n].shape for n in names])):
            full[n] = _join_shards(stacked, SHARD_AXIS[n])

    loss, grad_x, grads = local_step(x[0], loss_target[0], full)
    loss = lax.psum(loss, AXES)

    order = BIG + SMALL + REPL
    pieces = [_split_shards(grads[n].reshape(full[n].shape), SHARD_AXIS[n]) if n in SHARD_AXIS
              else jnp.broadcast_to(grads[n].reshape(w[n].shape)[None], (N_DEV,) + w[n].shape) for n in order]
    flat = jnp.concatenate([p.reshape(N_DEV, -1) for p in pieces], axis=1)
    rows = _flat_rows(flat.shape[1], LANE)
    flat = jnp.pad(flat, ((0, 0), (0, rows * FLAT_COLS - flat.shape[1]))).reshape(N_DEV, rows, FLAT_COLS)
    recv = exchange_pieces(flat, name="exchange_grads")
    pack_local = lambda t: _pack([t[n] for n in order], LANE, F32)
    outs = sum_adamw(recv, pack_local(w), pack_local(m), pack_local(v), name="sum_adamw")
    shapes = [w[n].shape for n in order]
    result = [dict(zip(order, _unpack(o, shapes))) for o in outs]
    return (loss, grad_x[None], *[r[n] for r in result for n in WEIGHTS])
```

```python
import functools
import math

import numpy as np
import jax
import jax.numpy as jnp
from jax import lax
from jax.experimental import pallas as pl
from jax.experimental.pallas import tpu as pltpu

F32 = jnp.float32
BF16 = jnp.bfloat16
HI = lax.Precision.HIGHEST
MESH = pl.DeviceIdType.MESH
AXES = ("x", "y", "c")
N_DEV = 8

D_MODEL = 1024
N_META = 16
D_FF = 2816
NORM_EPS = 1e-6
NEG_INF = -1e30
SWA_Q_HEADS = 8
SWA_HEAD_DIM = 64
SWA_WINDOW = 128
SWA_BLOCK = 128
REL_BUCKETS = 32
REL_MAX_DIST = 128
DN_HEADS = 4
DN_HEAD_DIM = 128
DN_CONV = 4
GLA_HEADS = 4
GLA_DK = 128
GLA_DV = 256
GLA_GATE_RANK = 16
GLA_GATE_NORM = 16.0
CHUNK = 64
PAD = SWA_BLOCK - N_META
LANE = 128
PROJ_DIM = 3200

ADAM_LR = 0.001
ADAM_B1 = 0.9
ADAM_B2 = 0.999
ADAM_EPS = 1e-08
ADAM_WD = 0.01
ADAM_STEP = 10

FF_SHARD = D_FF // N_DEV
FF_SHARD_PAD = 384
FF_PAD = N_DEV * FF_SHARD_PAD
N_FFN = 4
EVEN_IN_SHARD, EVEN_IN_SHARD_PAD = 353, 384
ODD_IN_SHARD, ODD_IN_SHARD_PAD = 386, 512
OUT_SHARD = D_MODEL // N_DEV

FLAT_COLS = 128
BIG = ("ffn_w_gate", "ffn_w_up", "ffn_w_down", "even_w_in", "even_w_out", "odd_w_in", "odd_w_out")
SMALL = ("meta_tokens", "norm_w", "even_conv_w", "gla_w_gate_up", "gla_b_gate", "gla_norm_w")
REPL = ("rel_bias_table", "swa_sinks", "dn_a_log", "dn_dt_bias", "dn_norm_w")
WEIGHTS = ("meta_tokens", "norm_w", "ffn_w_gate", "ffn_w_up", "ffn_w_down", "rel_bias_table", "even_w_in",
           "even_conv_w", "swa_sinks", "dn_a_log", "dn_dt_bias", "dn_norm_w", "even_w_out", "odd_w_in",
           "gla_w_gate_up", "gla_b_gate", "gla_norm_w", "odd_w_out")
SHARD_AXIS = {"ffn_w_gate": 3, "ffn_w_up": 3, "ffn_w_down": 2, "even_w_in": 2, "even_w_out": 1, "odd_w_in": 2,
              "odd_w_out": 1, "meta_tokens": 1, "norm_w": 2, "even_conv_w": 2, "gla_w_gate_up": 2,
              "gla_b_gate": 1, "gla_norm_w": 1}


def _rms(x, w):
    r = lax.rsqrt(jnp.mean(x * x, axis=-1, keepdims=True) + NORM_EPS)
    return x * r * w


def _sigmoid(x):
    return 0.5 * (jnp.tanh(0.5 * x) + 1.0)


def _silu(x):
    return x * _sigmoid(x)


def _softplus(x):
    pos = x > 0
    return jnp.where(pos, x, 0.0) + jnp.log(1.0 + jnp.exp(jnp.where(pos, -x, x)))


def _l2n(x):
    return x * lax.rsqrt(jnp.sum(x * x, axis=-1, keepdims=True) + 1e-6)


def _make_mm(cast, precision):
    def dg(a, b, ca, cb):
        if cast is not None:
            a, b = a.astype(cast), b.astype(cast)
        return lax.dot_general(a, b, (((ca,), (cb,)), ((), ())), precision=precision, preferred_element_type=F32)

    @jax.custom_vjp
    def nn(a, b):
        return dg(a, b, 1, 0)

    @jax.custom_vjp
    def nt(a, b):
        return dg(a, b, 1, 1)

    @jax.custom_vjp
    def tn(a, b):
        return dg(a, b, 0, 0)

    nn.defvjp(lambda a, b: (nn(a, b), (a, b)), lambda r, g: (nt(g, r[1]), tn(r[0], g)))
    nt.defvjp(lambda a, b: (nt(a, b), (a, b)), lambda r, g: (nn(g, r[1]), tn(g, r[0])))
    tn.defvjp(lambda a, b: (tn(a, b), (a, b)), lambda r, g: (nt(r[1], g), nn(r[0], g)))
    return nn, nt, tn


_mm, _mm_nt, _mm_tn = _make_mm(BF16, None)
_mmh, _mmh_nt, _mmh_tn = _make_mm(None, HI)


def _row_tile(n_rows, cap):
    best = LANE
    for t in range(LANE, cap + 1, LANE):
        if n_rows % t == 0:
            best = t
    return best


def _real_rows(tile_index, tm):
    row = tile_index * tm + lax.broadcasted_iota(jnp.int32, (tm, 1), 0)
    return (row >= PAD).astype(F32)


def _full(shape):
    return pl.BlockSpec(shape, lambda *_: (0,) * len(shape))


def _resident(shape):
    return pl.BlockSpec(shape, lambda *_: (0,) * len(shape), pipeline_mode=pl.Buffered(1))


def _resident_w(wmat, widx):
    if wmat.ndim == 2:
        return _resident(wmat.shape)
    return pl.BlockSpec((None,) + wmat.shape[1:], lambda *_: (widx, 0, 0), pipeline_mode=pl.Buffered(1))


def rms_mm(h, w, wmat, *, swiglu, name, widx=None):
    tp, d = h.shape
    n = wmat.shape[-1]
    tm = _row_tile(tp, 384)
    half = n // 2

    def body(h_ref, w_ref, wm_ref, hn_ref, *outs):
        hn = _rms(h_ref[...], w_ref[...]).astype(BF16)
        hn_ref[...] = hn
        p = jnp.dot(hn, wm_ref[...], preferred_element_type=F32)
        if swiglu:
            g, u = p[:, :half], p[:, half:]
            outs[0][...] = g.astype(BF16)
            outs[1][...] = u.astype(BF16)
            outs[2][...] = (_silu(g) * u).astype(BF16)
        else:
            outs[0][...] = p

    row = lambda width: pl.BlockSpec((tm, width), lambda i: (i, 0))
    if swiglu:
        out_shape = (jax.ShapeDtypeStruct((tp, d), BF16),) + (jax.ShapeDtypeStruct((tp, half), BF16),) * 3
        out_specs = (row(d), row(half), row(half), row(half))
    else:
        out_shape = (jax.ShapeDtypeStruct((tp, d), BF16), jax.ShapeDtypeStruct((tp, n), F32))
        out_specs = (row(d), row(n))
    return pl.pallas_call(
        body, name=name, grid=(tp // tm,),
        in_specs=[row(d), _full((1, d)), _resident_w(wmat, widx)],
        out_specs=out_specs, out_shape=out_shape,
    )(h, w, wmat)


def mm_rms_res(acts, wmat, h, w, *, scale, name, widx=None):
    tp, d = h.shape
    tm = _row_tile(tp, 384)
    widths = [a.shape[1] for a in acts]
    offs = [sum(widths[:i]) for i in range(len(acts))]
    na = len(acts)

    def body(*refs):
        a_refs = refs[:na]
        wm_ref, h_ref, w_ref, f_ref, ho_ref = refs[na:]
        f = None
        for a_ref, off, width in zip(a_refs, offs, widths):
            part = jnp.dot(a_ref[...].astype(BF16), wm_ref[off:off + width, :], preferred_element_type=F32)
            f = part if f is None else f + part
        f_ref[...] = f
        ho_ref[...] = h_ref[...] + scale * _rms(f, w_ref[...])

    row = lambda width: pl.BlockSpec((tm, width), lambda i: (i, 0))
    return pl.pallas_call(
        body, name=name, grid=(tp // tm,),
        in_specs=[row(wd) for wd in widths] + [_resident_w(wmat, widx), row(d), _full((1, d))],
        out_specs=(row(d), row(d)),
        out_shape=(jax.ShapeDtypeStruct((tp, d), F32), jax.ShapeDtypeStruct((tp, d), F32)),
    )(*acts, wmat, h, w)


def mm_rms_res_bwd(dho, f, w, wmat, gu, *, scale, name, widx=None):
    tp, d = f.shape
    k = wmat.shape[-2]
    tm = _row_tile(tp, 384)
    swiglu = gu is not None

    def body(*refs):
        if swiglu:
            dho_ref, f_ref, w_ref, wm_ref, g_ref, u_ref, df_ref, dw_ref, dgu_ref = refs
        else:
            dho_ref, f_ref, w_ref, wm_ref, df_ref, dw_ref, da_ref = refs
        i = pl.program_id(0)
        _, vjp = jax.vjp(lambda ff, ww: scale * _rms(ff, ww), f_ref[...], w_ref[...])
        df, dw = vjp(dho_ref[...])
        dfb = (df * _real_rows(i, tm)).astype(BF16)
        df_ref[...] = dfb

        @pl.when(i == 0)
        def _():
            dw_ref[...] = jnp.zeros_like(dw_ref)

        dw_ref[...] += dw
        da = lax.dot_general(dfb, wm_ref[...], (((1,), (1,)), ((), ())), preferred_element_type=F32)
        if swiglu:
            g = g_ref[...].astype(F32)
            u = u_ref[...].astype(F32)
            s = _sigmoid(g)
            dgu_ref[:, :k] = (da * u * s * (1.0 + g * (1.0 - s))).astype(BF16)
            dgu_ref[:, k:] = (da * g * s).astype(BF16)
        else:
            da_ref[...] = da

    row = lambda width: pl.BlockSpec((tm, width), lambda i: (i, 0))
    in_specs = [row(d), row(d), _full((1, d)), _resident_w(wmat, widx)]
    args = [dho, f, w, wmat]
    out_shape = [jax.ShapeDtypeStruct((tp, d), BF16), jax.ShapeDtypeStruct((1, d), F32)]
    out_specs = [row(d), _full((1, d))]
    if swiglu:
        in_specs += [row(k), row(k)]
        args += list(gu)
        out_shape += [jax.ShapeDtypeStruct((tp, 2 * k), BF16)]
        out_specs += [row(2 * k)]
    else:
        out_shape += [jax.ShapeDtypeStruct((tp, k), F32)]
        out_specs += [row(k)]
    return pl.pallas_call(body, name=name, grid=(tp // tm,), in_specs=in_specs, out_specs=tuple(out_specs),
                          out_shape=tuple(out_shape))(*args)


def rms_mm_bwd(dps, wmat, h, w, dho, *, name, widx=None):
    tp, d = h.shape
    tm = _row_tile(tp, 384)
    widths = [p.shape[1] for p in dps]
    offs = [sum(widths[:i]) for i in range(len(dps))]
    ndp = len(dps)

    def body(*refs):
        dp_refs = refs[:ndp]
        wm_ref, h_ref, w_ref, dho_ref, dh_ref, dw_ref = refs[ndp:]
        i = pl.program_id(0)
        dhn = None
        for dp_ref, off, width in zip(dp_refs, offs, widths):
            part = lax.dot_general(dp_ref[...].astype(BF16), wm_ref[:, off:off + width], (((1,), (1,)), ((), ())),
                                   preferred_element_type=F32)
            dhn = part if dhn is None else dhn + part
        _, vjp = jax.vjp(_rms, h_ref[...], w_ref[...])
        dx, dw = vjp(dhn)
        dh_ref[...] = (dho_ref[...] + dx) * _real_rows(i, tm)

        @pl.when(i == 0)
        def _():
            dw_ref[...] = jnp.zeros_like(dw_ref)

        dw_ref[...] += dw

    row = lambda width: pl.BlockSpec((tm, width), lambda i: (i, 0))
    return pl.pallas_call(
        body, name=name, grid=(tp // tm,),
        in_specs=[row(wd) for wd in widths] + [_resident_w(wmat, widx), row(d), _full((1, d)), row(d)],
        out_specs=(row(d), _full((1, d))),
        out_shape=(jax.ShapeDtypeStruct((tp, d), F32), jax.ShapeDtypeStruct((1, d), F32)),
    )(*dps, wmat, h, w, dho)


def mm_tn(a, b, *, name, into=None, slab=None):
    t, m = a.shape
    n = b.shape[1]
    bm = _row_tile(m, 512)
    bn = _row_tile(n, 1536)
    bk = _row_tile(t, 1408)
    nk = t // bk

    def body(a_ref, b_ref, *rest):
        o_ref, acc = rest[-2], rest[-1]

        @pl.when(pl.program_id(2) == 0)
        def _():
            acc[...] = jnp.zeros_like(acc)

        acc[...] += lax.dot_general(a_ref[...].astype(BF16), b_ref[...].astype(BF16), (((0,), (0,)), ((), ())),
                                    preferred_element_type=F32)

        @pl.when(pl.program_id(2) == nk - 1)
        def _():
            o_ref[...] = acc[...].astype(o_ref.dtype)

    in_specs = [pl.BlockSpec((bk, bm), lambda i, j, kk: (kk, i)), pl.BlockSpec((bk, bn), lambda i, j, kk: (kk, j))]
    scratch = [pltpu.VMEM((bm, bn), F32)]
    if into is None:
        return pl.pallas_call(
            body, name=name, grid=(m // bm, n // bn, nk), in_specs=in_specs,
            out_specs=pl.BlockSpec((bm, bn), lambda i, j, kk: (i, j)),
            out_shape=jax.ShapeDtypeStruct((m, n), F32), scratch_shapes=scratch,
        )(a, b)
    return pl.pallas_call(
        body, name=name, grid=(m // bm, n // bn, nk),
        in_specs=in_specs + [pl.BlockSpec(memory_space=pl.ANY)],
        out_specs=pl.BlockSpec((None, bm, bn), lambda i, j, kk: (slab, i, j)),
        out_shape=jax.ShapeDtypeStruct(into.shape, into.dtype), scratch_shapes=scratch,
        input_output_aliases={2: 0},
    )(a, b, into)


def loss_and_grad(h, target, *, name):
    tp, d = h.shape
    tm = SWA_BLOCK

    def body(h_ref, t_ref, dh_ref, loss_ref):
        i = pl.program_id(0)

        @pl.when(i == 0)
        def _():
            loss_ref[...] = jnp.zeros_like(loss_ref)
            dh_ref[...] = jnp.zeros_like(dh_ref)

        @pl.when(i > 0)
        def _():
            err = h_ref[...] - t_ref[...]
            dh_ref[...] = err * (1.0 / d)
            loss_ref[...] += 0.5 * jnp.sum(jnp.sum(err * err, axis=1, keepdims=True), axis=0, keepdims=True) * (1.0 / d)

    return pl.pallas_call(
        body, name=name, grid=(tp // tm,),
        in_specs=[pl.BlockSpec((tm, d), lambda i: (i, 0)), pl.BlockSpec((tm, d), lambda i: (jnp.maximum(i - 1, 0), 0))],
        out_specs=(pl.BlockSpec((tm, d), lambda i: (i, 0)), _full((1, 1))),
        out_shape=(jax.ShapeDtypeStruct((tp, d), F32), jax.ShapeDtypeStruct((1, 1), F32)),
    )(h, target)


def _t5_bucket_np(rel):
    n = np.maximum(rel, 0)
    max_exact = REL_BUCKETS // 2
    n_f = np.maximum(n, 1).astype(np.float32)
    large = max_exact + (np.log(n_f / np.float32(max_exact)) / np.float32(math.log(REL_MAX_DIST / max_exact))
                         * np.float32(REL_BUCKETS - max_exact)).astype(np.int32)
    large = np.minimum(large, REL_BUCKETS - 1)
    return np.where(n < max_exact, n, large).astype(np.int32)


def _swa_positions_np(n):
    i = np.arange(SWA_BLOCK)[:, None]
    j = np.arange(3 * SWA_BLOCK)[None, :]
    pos_q = n * SWA_BLOCK + i - PAD
    pos_k = np.where(j < SWA_BLOCK, j - PAD, (n - 1) * SWA_BLOCK + (j - SWA_BLOCK) - PAD)
    return pos_q, pos_k


def _swa_buckets():
    out = []
    for n in range(3):
        pos_q, pos_k = _swa_positions_np(n)
        out.append(_t5_bucket_np(pos_q - pos_k))
    return jnp.asarray(np.stack(out))


def swa_bias(table, buckets, *, name):
    nc, nq, nk = buckets.shape

    def body(tab_ref, bkt_ref, out_ref):
        for c in range(nc):
            bkt = bkt_ref[c]
            for h in range(SWA_Q_HEADS):
                acc = jnp.zeros((nq, nk), F32)
                for b in range(REL_BUCKETS):
                    acc = jnp.where(bkt == b, tab_ref[b, h], acc)
                out_ref[c, h] = acc

    return pl.pallas_call(
        body, name=name,
        in_specs=[pl.BlockSpec(memory_space=pltpu.SMEM), pl.BlockSpec(memory_space=pltpu.VMEM)],
        out_specs=pl.BlockSpec(memory_space=pltpu.VMEM),
        out_shape=jax.ShapeDtypeStruct((nc, SWA_Q_HEADS, nq, nk), F32),
    )(table, buckets)


def swa_bias_bwd(dbias, buckets, *, name):
    nc = buckets.shape[0]

    def body(db_ref, bkt_ref, out_ref):
        lane = lax.broadcasted_iota(jnp.int32, (1, LANE), 1)
        for b in range(REL_BUCKETS):
            row = jnp.zeros((1, LANE), F32)
            for c in range(nc):
                hit = bkt_ref[c] == b
                for h in range(SWA_Q_HEADS):
                    part = jnp.where(hit, db_ref[c, h], 0.0)
                    tot = jnp.sum(jnp.sum(part, axis=1, keepdims=True), axis=0, keepdims=True)
                    row = row + jnp.where(lane == h, tot, 0.0)
            out_ref[b:b + 1, :] = row

    return pl.pallas_call(
        body, name=name,
        in_specs=[pl.BlockSpec(memory_space=pltpu.VMEM), pl.BlockSpec(memory_space=pltpu.VMEM)],
        out_specs=pl.BlockSpec(memory_space=pltpu.VMEM),
        out_shape=jax.ShapeDtypeStruct((REL_BUCKETS, LANE), F32),
    )(dbias, buckets)


def _swa_block(q, kvm, kvp, kvc, bias, sinks, n):
    blk = SWA_BLOCK
    i = lax.broadcasted_iota(jnp.int32, (blk, 3 * blk), 0)
    j = lax.broadcasted_iota(jnp.int32, (blk, 3 * blk), 1)
    pos_q = n * blk + i - PAD
    is_meta = j < blk
    pos_k = jnp.where(is_meta, j - PAD, (n - 1) * blk + (j - blk) - PAD)
    rel = pos_q - pos_k
    valid = ((is_meta & (pos_k >= 0) & (pos_k < N_META) & (rel >= 0))
             | (jnp.logical_not(is_meta) & (pos_k >= N_META) & (rel >= 0) & (rel < SWA_WINDOW)))
    kv = jnp.concatenate([kvm, kvp, kvc], axis=0)
    lane = lax.broadcasted_iota(jnp.int32, (1, LANE), 1)
    halves = ((lane < SWA_HEAD_DIM).astype(F32), (lane >= SWA_HEAD_DIM).astype(F32))
    scale = SWA_HEAD_DIM ** -0.5
    outs = []
    for pair in range(SWA_Q_HEADS // 2):
        qp = q[:, pair * LANE:(pair + 1) * LANE]
        grp = pair // 2
        kg = kv[:, grp * LANE:(grp + 1) * LANE]
        vg = kv[:, (2 + grp) * LANE:(3 + grp) * LANE]
        op = None
        for hh in range(2):
            h = 2 * pair + hh
            s = _mm_nt(qp * halves[hh], kg) * scale + bias[h]
            s = jnp.where(valid, s, NEG_INF)
            sink = jnp.sum(jnp.where(lane == h, sinks, 0.0), axis=1, keepdims=True)
            m = lax.stop_gradient(jnp.maximum(jnp.max(s, axis=1, keepdims=True), sink))
            e = jnp.exp(s - m)
            den = jnp.sum(e, axis=1, keepdims=True) + jnp.exp(sink - m)
            part = _mm(e / den, vg) * halves[hh]
            op = part if op is None else op + part
        outs.append(op)
    return jnp.concatenate(outs, axis=1)


def _swa_in_specs(nb, rev):
    blk = SWA_BLOCK
    step = (lambda i: nb - 1 - i) if rev else (lambda i: i)
    return [
        pl.BlockSpec((blk, 4 * LANE), lambda i: (step(i), 0)),
        pl.BlockSpec((blk, 4 * LANE), lambda i: (0, 1)),
        pl.BlockSpec((blk, 4 * LANE), lambda i: (jnp.maximum(step(i) - 1, 0), 1)),
        pl.BlockSpec((blk, 4 * LANE), lambda i: (step(i), 1)),
        pl.BlockSpec((1, SWA_Q_HEADS, blk, 3 * blk), lambda i: (jnp.minimum(step(i), 2), 0, 0, 0)),
        _full((1, LANE)),
    ]


def swa_fwd(proj, bias, sinks, *, name):
    tp = proj.shape[0]
    nb = tp // SWA_BLOCK

    def body(q_ref, kvm_ref, kvp_ref, kvc_ref, bias_ref, sinks_ref, o_ref):
        n = pl.program_id(0)
        o_ref[...] = _swa_block(q_ref[...], kvm_ref[...], kvp_ref[...], kvc_ref[...], bias_ref[0], sinks_ref[...], n)

    return pl.pallas_call(
        body, name=name, grid=(nb,),
        in_specs=_swa_in_specs(nb, False),
        out_specs=pl.BlockSpec((SWA_BLOCK, 4 * LANE), lambda i: (i, 0)),
        out_shape=jax.ShapeDtypeStruct((tp, 4 * LANE), F32),
    )(proj, proj, proj, proj, bias, sinks)


def swa_bwd(proj, bias, sinks, do, *, name):
    tp = proj.shape[0]
    nb = tp // SWA_BLOCK
    blk = SWA_BLOCK

    def body(q_ref, kvm_ref, kvp_ref, kvc_ref, bias_ref, sinks_ref, do_ref, dq_ref, dkv_ref, dbias_ref, dsinks_ref,
             carry, meta_acc):
        i = pl.program_id(0)
        n = nb - 1 - i

        @pl.when(i == 0)
        def _():
            carry[...] = jnp.zeros_like(carry)
            meta_acc[...] = jnp.zeros_like(meta_acc)
            dsinks_ref[...] = jnp.zeros_like(dsinks_ref)

        fn = lambda q, kvm, kvp, kvc, b, s: _swa_block(q, kvm, kvp, kvc, b, s, n)
        _, vjp = jax.vjp(fn, q_ref[...], kvm_ref[...], kvp_ref[...], kvc_ref[...], bias_ref[0], sinks_ref[...])
        dq, dkvm, dkvp, dkvc, dbias, dsinks = vjp(do_ref[...])
        dq_ref[...] = dq
        meta_acc[...] += dkvm
        dkv_ref[...] = dkvc + carry[...] + jnp.where(n == 0, meta_acc[...], 0.0)
        carry[...] = dkvp
        first_visit = (n == nb - 1) | (n < 2)

        @pl.when(first_visit)
        def _():
            dbias_ref[0] = dbias

        @pl.when(jnp.logical_not(first_visit))
        def _():
            dbias_ref[0] += dbias

        dsinks_ref[...] += dsinks

    rev = lambda i: nb - 1 - i
    return pl.pallas_call(
        body, name=name, grid=(nb,),
        in_specs=_swa_in_specs(nb, True) + [pl.BlockSpec((blk, 4 * LANE), lambda i: (rev(i), 0))],
        out_specs=(pl.BlockSpec((blk, 4 * LANE), lambda i: (rev(i), 0)),
                   pl.BlockSpec((blk, 4 * LANE), lambda i: (rev(i), 0)),
                   pl.BlockSpec((1, SWA_Q_HEADS, blk, 3 * blk), lambda i: (jnp.minimum(rev(i), 2), 0, 0, 0)),
                   _full((1, LANE))),
        out_shape=(jax.ShapeDtypeStruct((tp, 4 * LANE), F32), jax.ShapeDtypeStruct((tp, 4 * LANE), F32),
                   jax.ShapeDtypeStruct((3, SWA_Q_HEADS, blk, 3 * blk), F32), jax.ShapeDtypeStruct((1, LANE), F32)),
        scratch_shapes=[pltpu.VMEM((blk, 4 * LANE), F32), pltpu.VMEM((blk, 4 * LANE), F32)],
    )(proj, proj, proj, proj, bias, sinks, do)


CONV_COL0 = 2
HALO = 8


def conv_fwd(proj, conv_w, *, name):
    tp = proj.shape[0]
    tm = _row_tile(tp, 384)
    cw = 4 * LANE
    ncol = conv_w.shape[1] // cw

    def body(x_ref, halo_ref, w_ref, y_ref, buf):
        i = pl.program_id(1)
        buf[0:HALO, :] = jnp.where(i > 0, halo_ref[...], 0.0)
        buf[HALO:, :] = x_ref[...]
        acc = None
        for j in range(DN_CONV):
            term = w_ref[j:j + 1, :] * buf[pl.ds(HALO - (DN_CONV - 1) + j, tm), :]
            acc = term if acc is None else acc + term
        y_ref[...] = acc

    return pl.pallas_call(
        body, name=name, grid=(ncol, tp // tm),
        in_specs=[pl.BlockSpec((tm, cw), lambda c, i: (i, CONV_COL0 + c)),
                  pl.BlockSpec((HALO, cw), lambda c, i: (jnp.maximum(i * (tm // HALO) - 1, 0), CONV_COL0 + c)),
                  pl.BlockSpec((DN_CONV, cw), lambda c, i: (0, c))],
        out_specs=pl.BlockSpec((tm, cw), lambda c, i: (i, c)),
        out_shape=jax.ShapeDtypeStruct((tp, ncol * cw), F32),
        scratch_shapes=[pltpu.VMEM((tm + HALO, cw), F32)],
    )(proj, proj, conv_w)


def conv_bwd(proj, conv_w, dy, *, name):
    tp = proj.shape[0]
    tm = _row_tile(tp, 384)
    cw = 4 * LANE
    ncol = conv_w.shape[1] // cw
    nt = tp // tm

    def body(x_ref, xhalo_ref, w_ref, dy_ref, dyhalo_ref, dx_ref, dw_ref, xbuf, dbuf):
        i = pl.program_id(1)
        xbuf[0:HALO, :] = jnp.where(i > 0, xhalo_ref[...], 0.0)
        xbuf[HALO:, :] = x_ref[...]
        dbuf[0:tm, :] = dy_ref[...]
        dbuf[tm:, :] = jnp.where(i < nt - 1, dyhalo_ref[...], 0.0)
        dy_t = dy_ref[...]
        acc = None
        rows = []
        for j in range(DN_CONV):
            term = w_ref[j:j + 1, :] * dbuf[pl.ds(DN_CONV - 1 - j, tm), :]
            acc = term if acc is None else acc + term
            rows.append(jnp.sum(dy_t * xbuf[pl.ds(HALO - (DN_CONV - 1) + j, tm), :], axis=0, keepdims=True))
        dx_ref[...] = acc

        @pl.when(i == 0)
        def _():
            dw_ref[...] = jnp.zeros_like(dw_ref)

        for j in range(DN_CONV):
            dw_ref[j:j + 1, :] += rows[j]

    return pl.pallas_call(
        body, name=name, grid=(ncol, nt),
        in_specs=[pl.BlockSpec((tm, cw), lambda c, i: (i, CONV_COL0 + c)),
                  pl.BlockSpec((HALO, cw), lambda c, i: (jnp.maximum(i * (tm // HALO) - 1, 0), CONV_COL0 + c)),
                  pl.BlockSpec((DN_CONV, cw), lambda c, i: (0, c)),
                  pl.BlockSpec((tm, cw), lambda c, i: (i, c)),
                  pl.BlockSpec((HALO, cw), lambda c, i: (jnp.minimum((i + 1) * (tm // HALO), tp // HALO - 1), c))],
        out_specs=(pl.BlockSpec((tm, cw), lambda c, i: (i, c)), pl.BlockSpec((DN_CONV, cw), lambda c, i: (0, c))),
        out_shape=(jax.ShapeDtypeStruct((tp, ncol * cw), F32), jax.ShapeDtypeStruct((DN_CONV, ncol * cw), F32)),
        scratch_shapes=[pltpu.VMEM((tm + HALO, cw), F32), pltpu.VMEM((tm + HALO, cw), F32)],
    )(proj, proj, conv_w, dy, dy)


def _chunk_masks():
    r = lax.broadcasted_iota(jnp.int32, (CHUNK, CHUNK), 0)
    c = lax.broadcasted_iota(jnp.int32, (CHUNK, CHUNK), 1)
    return r >= c, r > c, (r == c).astype(F32)


def _dn_chunk(y, z, small, s0, s1, s2, s3, a_log, dt_bias, norm_w, rows):
    tri_incl, tri_strict, eye = _chunk_masks()
    ltri = tri_incl.astype(F32)
    ones = jnp.ones((CHUNK, CHUNK), F32)
    lane = lax.broadcasted_iota(jnp.int32, (1, LANE), 1)
    dk = DN_HEAD_DIM
    nh = DN_HEADS
    outs, states = [], []
    for h, s_prev in enumerate((s0, s1, s2, s3)):
        q = _l2n(_silu(y[:, h * dk:(h + 1) * dk])) * dk ** -0.5
        k = _l2n(_silu(y[:, (nh + h) * dk:(nh + h + 1) * dk]))
        v = _silu(y[:, (2 * nh + h) * dk:(2 * nh + h + 1) * dk])
        b_col = jnp.sum(jnp.where(lane == h, small, 0.0), axis=1, keepdims=True)
        a_col = jnp.sum(jnp.where(lane == nh + h, small, 0.0), axis=1, keepdims=True)
        al = jnp.sum(jnp.where(lane == h, a_log, 0.0), axis=1, keepdims=True)
        dt = jnp.sum(jnp.where(lane == h, dt_bias, 0.0), axis=1, keepdims=True)
        beta = _sigmoid(b_col)
        g = -jnp.exp(al) * _softplus(a_col + dt) * rows
        gc_sq = _mmh(ltri, jnp.broadcast_to(g, (CHUNK, CHUNK)))
        gc = _mmh(ltri, jnp.broadcast_to(g, (CHUNK, dk)))
        gc_row = _mmh(ones, eye * gc_sq)
        gamma = jnp.where(tri_incl, jnp.exp(jnp.where(tri_incl, gc_sq - gc_row, 0.0)), 0.0)
        g_last = jnp.sum(g, axis=0, keepdims=True)
        k_beta = k * beta
        v_beta = v * beta
        a = jnp.where(tri_strict, _mm_nt(k_beta, k) * gamma, 0.0)
        inv = eye - a
        power = a
        for _ in range(5):
            power = _mmh(power, power)
            inv = inv + _mmh(inv, power)
        e_gc = jnp.exp(gc)
        u = _mmh(inv, v_beta)
        w = _mmh(inv, k_beta * e_gc)
        attn = _mm_nt(q, k) * gamma
        q_dec = q * e_gc
        k_dec = k * jnp.exp(g_last - gc)
        v_new = u - _mm(w, s_prev)
        o = _mm(q_dec, s_prev) + _mm(attn, v_new)
        states.append(s_prev * jnp.exp(g_last) + _mm_tn(k_dec, v_new))
        outs.append(_rms(o, norm_w) * _silu(z[:, h * dk:(h + 1) * dk]))
    return (jnp.concatenate(outs, axis=1), *states)


Z_COL = 5
SMALL_COL = 24


def _chunk_rows(n):
    row = n * CHUNK + lax.broadcasted_iota(jnp.int32, (CHUNK, 1), 0)
    return (row >= PAD).astype(F32)


def dn_fwd(y, proj, a_log, dt_bias, norm_w, *, name):
    tp = y.shape[0]
    nc = tp // CHUNK
    dk = DN_HEAD_DIM

    def body(y_ref, z_ref, small_ref, al_ref, dt_ref, nw_ref, o_ref, ssave_ref, state):
        n = pl.program_id(0)

        @pl.when(n == 0)
        def _():
            state[...] = jnp.zeros_like(state)

        ssave_ref[0] = state[...]
        out = _dn_chunk(y_ref[...], z_ref[...], small_ref[...], state[0], state[1], state[2], state[3],
                        al_ref[...], dt_ref[...], nw_ref[...], _chunk_rows(n))
        o_ref[...] = out[0]
        for h in range(DN_HEADS):
            state[h] = out[1 + h]

    return pl.pallas_call(
        body, name=name, grid=(nc,),
        in_specs=[pl.BlockSpec((CHUNK, y.shape[1]), lambda n: (n, 0)),
                  pl.BlockSpec((CHUNK, 4 * LANE), lambda n: (n, Z_COL)),
                  pl.BlockSpec((CHUNK, LANE), lambda n: (n, SMALL_COL)),
                  _full((1, LANE)), _full((1, LANE)), _full((1, LANE))],
        out_specs=(pl.BlockSpec((CHUNK, 4 * LANE), lambda n: (n, 0)),
                   pl.BlockSpec((1, DN_HEADS, dk, dk), lambda n: (n, 0, 0, 0))),
        out_shape=(jax.ShapeDtypeStruct((tp, 4 * LANE), F32), jax.ShapeDtypeStruct((nc, DN_HEADS, dk, dk), F32)),
        scratch_shapes=[pltpu.VMEM((DN_HEADS, dk, dk), F32)],
    )(y, proj, proj, a_log, dt_bias, norm_w)


def dn_bwd(y, proj, a_log, dt_bias, norm_w, ssave, do, *, name):
    tp = y.shape[0]
    nc = tp // CHUNK
    dk = DN_HEAD_DIM
    rev = lambda i: nc - 1 - i

    def body(y_ref, z_ref, small_ref, al_ref, dt_ref, nw_ref, ss_ref, do_ref,
             dy_ref, dz_ref, dsmall_ref, dal_ref, ddt_ref, dnw_ref, dstate):
        i = pl.program_id(0)
        n = nc - 1 - i

        @pl.when(i == 0)
        def _():
            dstate[...] = jnp.zeros_like(dstate)
            dal_ref[...] = jnp.zeros_like(dal_ref)
            ddt_ref[...] = jnp.zeros_like(ddt_ref)
            dnw_ref[...] = jnp.zeros_like(dnw_ref)

        rows = _chunk_rows(n)
        fn = lambda *a: _dn_chunk(*a, rows)
        _, vjp = jax.vjp(fn, y_ref[...], z_ref[...], small_ref[...], ss_ref[0, 0], ss_ref[0, 1], ss_ref[0, 2],
                         ss_ref[0, 3], al_ref[...], dt_ref[...], nw_ref[...])
        cts = vjp((do_ref[...], dstate[0], dstate[1], dstate[2], dstate[3]))
        dy_ref[...] = cts[0]
        dz_ref[...] = cts[1]
        dsmall_ref[...] = cts[2]
        for h in range(DN_HEADS):
            dstate[h] = cts[3 + h]
        dal_ref[...] += cts[7]
        ddt_ref[...] += cts[8]
        dnw_ref[...] += cts[9]

    return pl.pallas_call(
        body, name=name, grid=(nc,),
        in_specs=[pl.BlockSpec((CHUNK, y.shape[1]), lambda i: (rev(i), 0)),
                  pl.BlockSpec((CHUNK, 4 * LANE), lambda i: (rev(i), Z_COL)),
                  pl.BlockSpec((CHUNK, LANE), lambda i: (rev(i), SMALL_COL)),
                  _full((1, LANE)), _full((1, LANE)), _full((1, LANE)),
                  pl.BlockSpec((1, DN_HEADS, dk, dk), lambda i: (rev(i), 0, 0, 0)),
                  pl.BlockSpec((CHUNK, 4 * LANE), lambda i: (rev(i), 1))],
        out_specs=(pl.BlockSpec((CHUNK, y.shape[1]), lambda i: (rev(i), 0)),
                   pl.BlockSpec((CHUNK, 4 * LANE), lambda i: (rev(i), 0)),
                   pl.BlockSpec((CHUNK, LANE), lambda i: (rev(i), 0)),
                   _full((1, LANE)), _full((1, LANE)), _full((1, LANE))),
        out_shape=(jax.ShapeDtypeStruct((tp, y.shape[1]), F32), jax.ShapeDtypeStruct((tp, 4 * LANE), F32),
                   jax.ShapeDtypeStruct((tp, LANE), F32), jax.ShapeDtypeStruct((1, LANE), F32),
                   jax.ShapeDtypeStruct((1, LANE), F32), jax.ShapeDtypeStruct((1, LANE), F32)),
        scratch_shapes=[pltpu.VMEM((DN_HEADS, dk, dk), F32)],
    )(y, proj, proj, a_log, dt_bias, norm_w, ssave, do)


def _gla_chunk(q, k, v, gate, low, s0, s1, s2, s3, w_gate_up, b_gate, norm_w, rows):
    tri_incl, _, _ = _chunk_masks()
    ltri = tri_incl.astype(F32)
    dk, dv = GLA_DK, GLA_DV
    logit = _mmh(low, w_gate_up) + b_gate
    glog_all = -_softplus(-logit) * (1.0 / GLA_GATE_NORM) * rows
    ones = jnp.ones((CHUNK, dk), F32)
    outs, states = [], []
    for h, s_prev in enumerate((s0, s1, s2, s3)):
        qh = q[:, h * dk:(h + 1) * dk] * dk ** -0.5
        kh = k[:, h * dk:(h + 1) * dk]
        vh = v[:, h * dv:(h + 1) * dv]
        glog = glog_all[:, h * dk:(h + 1) * dk]
        bcum = _mmh(ltri, glog)
        q_dec = qh * jnp.exp(bcum)
        attn = jnp.where(tri_incl, _mm_nt(q_dec, kh * jnp.exp(-bcum)), 0.0)
        b_last = jnp.sum(glog, axis=0, keepdims=True)
        k_dec = kh * jnp.exp(b_last - bcum)
        decay_col = jnp.exp(_mmh_tn(glog, ones))
        decay = jnp.concatenate([decay_col, decay_col], axis=1)
        o = _mm(attn, vh) + _mm(q_dec, s_prev)
        states.append(s_prev * decay + _mm_tn(k_dec, vh))
        outs.append(_rms(o, norm_w) * _silu(gate[:, h * dv:(h + 1) * dv]))
    return (jnp.concatenate(outs, axis=1), *states)


LOW_COL = 24


def _gla_in_specs(step):
    return [pl.BlockSpec((CHUNK, 4 * LANE), lambda i: (step(i), 0)),
            pl.BlockSpec((CHUNK, 4 * LANE), lambda i: (step(i), 1)),
            pl.BlockSpec((CHUNK, 8 * LANE), lambda i: (step(i), 1)),
            pl.BlockSpec((CHUNK, 8 * LANE), lambda i: (step(i), 2)),
            pl.BlockSpec((CHUNK, LANE), lambda i: (step(i), LOW_COL)),
            _full((LANE, 4 * LANE)), _full((1, 4 * LANE)), _full((1, GLA_DV))]


def gla_fwd(proj, w_gate_up, b_gate, norm_w, *, name):
    tp = proj.shape[0]
    nc = tp // CHUNK

    def body(q_ref, k_ref, v_ref, g_ref, low_ref, wgu_ref, bg_ref, nw_ref, o_ref, ssave_ref, state):
        n = pl.program_id(0)

        @pl.when(n == 0)
        def _():
            state[...] = jnp.zeros_like(state)

        ssave_ref[0] = state[...]
        out = _gla_chunk(q_ref[...], k_ref[...], v_ref[...], g_ref[...], low_ref[...], state[0], state[1], state[2],
                         state[3], wgu_ref[...], bg_ref[...], nw_ref[...], _chunk_rows(n))
        o_ref[...] = out[0]
        for h in range(GLA_HEADS):
            state[h] = out[1 + h]

    return pl.pallas_call(
        body, name=name, grid=(nc,),
        in_specs=_gla_in_specs(lambda i: i),
        out_specs=(pl.BlockSpec((CHUNK, 8 * LANE), lambda n: (n, 0)),
                   pl.BlockSpec((1, GLA_HEADS, GLA_DK, GLA_DV), lambda n: (n, 0, 0, 0))),
        out_shape=(jax.ShapeDtypeStruct((tp, 8 * LANE), F32),
                   jax.ShapeDtypeStruct((nc, GLA_HEADS, GLA_DK, GLA_DV), F32)),
        scratch_shapes=[pltpu.VMEM((GLA_HEADS, GLA_DK, GLA_DV), F32)],
    )(proj, proj, proj, proj, proj, w_gate_up, b_gate, norm_w)


def gla_bwd(proj, w_gate_up, b_gate, norm_w, ssave, do, *, name):
    tp = proj.shape[0]
    nc = tp // CHUNK
    rev = lambda i: nc - 1 - i

    def body(q_ref, k_ref, v_ref, g_ref, low_ref, wgu_ref, bg_ref, nw_ref, ss_ref, do_ref,
             dq_ref, dk_ref, dv_ref, dg_ref, dlow_ref, dwgu_ref, dbg_ref, dnw_ref, dstate):
        i = pl.program_id(0)
        n = nc - 1 - i

        @pl.when(i == 0)
        def _():
            dstate[...] = jnp.zeros_like(dstate)
            dwgu_ref[...] = jnp.zeros_like(dwgu_ref)
            dbg_ref[...] = jnp.zeros_like(dbg_ref)
            dnw_ref[...] = jnp.zeros_like(dnw_ref)

        rows = _chunk_rows(n)
        fn = lambda *a: _gla_chunk(*a, rows)
        _, vjp = jax.vjp(fn, q_ref[...], k_ref[...], v_ref[...], g_ref[...], low_ref[...], ss_ref[0, 0], ss_ref[0, 1],
                         ss_ref[0, 2], ss_ref[0, 3], wgu_ref[...], bg_ref[...], nw_ref[...])
        cts = vjp((do_ref[...], dstate[0], dstate[1], dstate[2], dstate[3]))
        dq_ref[...] = cts[0]
        dk_ref[...] = cts[1]
        dv_ref[...] = cts[2]
        dg_ref[...] = cts[3]
        dlow_ref[...] = cts[4]
        for h in range(GLA_HEADS):
            dstate[h] = cts[5 + h]
        dwgu_ref[...] += cts[9]
        dbg_ref[...] += cts[10]
        dnw_ref[...] += cts[11]

    chunk = lambda width: pl.BlockSpec((CHUNK, width), lambda i: (rev(i), 0))
    return pl.pallas_call(
        body, name=name, grid=(nc,),
        in_specs=_gla_in_specs(rev) + [pl.BlockSpec((1, GLA_HEADS, GLA_DK, GLA_DV), lambda i: (rev(i), 0, 0, 0)),
                                       chunk(8 * LANE)],
        out_specs=(chunk(4 * LANE), chunk(4 * LANE), chunk(8 * LANE), chunk(8 * LANE), chunk(LANE),
                   _full((LANE, 4 * LANE)), _full((1, 4 * LANE)), _full((1, GLA_DV))),
        out_shape=(jax.ShapeDtypeStruct((tp, 4 * LANE), F32), jax.ShapeDtypeStruct((tp, 4 * LANE), F32),
                   jax.ShapeDtypeStruct((tp, 8 * LANE), F32), jax.ShapeDtypeStruct((tp, 8 * LANE), F32),
                   jax.ShapeDtypeStruct((tp, LANE), F32), jax.ShapeDtypeStruct((LANE, 4 * LANE), F32),
                   jax.ShapeDtypeStruct((1, 4 * LANE), F32), jax.ShapeDtypeStruct((1, GLA_DV), F32)),
        scratch_shapes=[pltpu.VMEM((GLA_HEADS, GLA_DK, GLA_DV), F32)],
    )(proj, proj, proj, proj, proj, w_gate_up, b_gate, norm_w, ssave, do)


def _even_proj_weight(w_in):
    hd = SWA_HEAD_DIM
    k0, k1 = w_in[:, 512:512 + hd], w_in[:, 512 + hd:640]
    v0, v1 = w_in[:, 640:640 + hd], w_in[:, 640 + hd:768]
    zeros = jnp.zeros((w_in.shape[0], LANE - 2 * DN_HEADS), w_in.dtype)
    return jnp.concatenate([w_in[:, :512], k0, k0, k1, k1, v0, v0, v1, v1, w_in[:, 768:2816], w_in[:, 2816:2824], zeros],
                           axis=1)


def _even_proj_weight_grad(dw):
    hd = SWA_HEAD_DIM
    c = lambda i: dw[:, 512 + i * hd:512 + (i + 1) * hd]
    return jnp.concatenate([dw[:, :512], c(0) + c(1), c(2) + c(3), c(4) + c(5), c(6) + c(7), dw[:, 1024:3072],
                            dw[:, 3072:3080]], axis=1)


def _ffn_fwd(h, nw_in, nw_out, w_gu, w_down, idx):
    hn, g, u, a = rms_mm(h, nw_in, w_gu, swiglu=True, name=f"ffn_up_{idx}", widx=idx)
    f, h_out = mm_rms_res([a], w_down, h, nw_out, scale=0.5, name=f"ffn_down_{idx}", widx=idx)
    return h_out, (h, hn, g, u, a, f)


def _ffn_bwd(dho, saved, nw_in, nw_out, w_gu, w_down, idx, g_gu, g_down):
    h, hn, g, u, a, f = saved
    df, dnw_out, dgu = mm_rms_res_bwd(dho, f, nw_out, w_down, (g, u), scale=0.5, name=f"ffn_down_bwd_{idx}", widx=idx)
    dh, dnw_in = rms_mm_bwd([dgu], w_gu, h, nw_in, dho, name=f"ffn_up_bwd_{idx}", widx=idx)
    g_gu = mm_tn(hn, dgu, name=f"ffn_dwgu_{idx}", into=g_gu, slab=idx)
    g_down = mm_tn(a, df, name=f"ffn_dwd_{idx}", into=g_down, slab=idx)
    return dh, dnw_in, dnw_out, g_gu, g_down


def local_step(x, target, wts):
    seq, d = x.shape
    row = lambda v: v.reshape(1, -1)
    lane_row = lambda v: jnp.pad(v.reshape(1, -1), ((0, 0), (0, LANE - v.size)))
    nw = wts["norm_w"]
    h = jnp.concatenate([jnp.zeros((PAD, d), F32), wts["meta_tokens"], x], axis=0)
    w_gu, w_down = wts["w_gu"], wts["w_down"]
    w_in = [_even_proj_weight(wts["even_w_in"]),
            jnp.pad(wts["odd_w_in"], ((0, 0), (0, PROJ_DIM - wts["odd_w_in"].shape[1])))]
    w_out = [wts["even_w_out"], wts["odd_w_out"]]
    buckets = _swa_buckets()
    bias = swa_bias(wts["rel_bias_table"], buckets, name="swa_bias")
    sinks = lane_row(wts["swa_sinks"])
    a_log, dt_bias = lane_row(wts["dn_a_log"]), lane_row(wts["dn_dt_bias"])
    dn_norm_w = row(wts["dn_norm_w"])
    conv_w = wts["even_conv_w"][0]
    w_gate_up = jnp.pad(wts["gla_w_gate_up"][0], ((0, LANE - GLA_GATE_RANK), (0, 0)))
    b_gate, gla_norm_w = row(wts["gla_b_gate"]), row(wts["gla_norm_w"])

    saved = []
    for l in range(2):
        h, s_a = _ffn_fwd(h, row(nw[l, 0]), row(nw[l, 1]), w_gu, w_down, 2 * l)
        h_mix = h
        hn, proj = rms_mm(h, row(nw[l, 2]), w_in[l], swiglu=False, name=f"mix_in_{l}")
        if l == 0:
            o_a = swa_fwd(proj, bias, sinks, name="swa_fwd")
            y = conv_fwd(proj, conv_w, name="conv_fwd")
            o_b, ssave = dn_fwd(y, proj, a_log, dt_bias, dn_norm_w, name="dn_fwd")
            acts, extra = [o_a, o_b], (y, ssave)
        else:
            o, ssave = gla_fwd(proj, w_gate_up, b_gate, gla_norm_w, name="gla_fwd")
            acts, extra = [o], (ssave,)
        mix, h = mm_rms_res(acts, w_out[l], h, row(nw[l, 3]), scale=1.0, name=f"mix_out_{l}")
        s_m = (h_mix, hn, proj, acts, extra, mix)
        h, s_b = _ffn_fwd(h, row(nw[l, 4]), row(nw[l, 5]), w_gu, w_down, 2 * l + 1)
        saved.append((s_a, s_m, s_b))

    dh, loss = loss_and_grad(h, target, name="loss")

    grads = {}
    dnw = [[None] * 6 for _ in range(2)]
    g_gu = lax.empty(w_gu.shape, BF16)
    g_down = lax.empty(w_down.shape, BF16)
    for l in (1, 0):
        s_a, s_m, s_b = saved[l]
        dh, dnw[l][4], dnw[l][5], g_gu, g_down = _ffn_bwd(
            dh, s_b, row(nw[l, 4]), row(nw[l, 5]), w_gu, w_down, 2 * l + 1, g_gu, g_down)
        h_mix, hn, proj, acts, extra, mix = s_m
        dmix, dnw[l][3], do = mm_rms_res_bwd(dh, mix, row(nw[l, 3]), w_out[l], None, scale=1.0, name=f"mix_out_bwd_{l}")
        dw_out = jnp.concatenate([mm_tn(a, dmix, name=f"mix_dwo_{l}_{i}") for i, a in enumerate(acts)], axis=0)
        if l == 0:
            y, ssave = extra
            dq, dkv, dbias, dsinks = swa_bwd(proj, bias, sinks, do, name="swa_bwd")
            dy, dz, dsmall, da_log, ddt_bias, ddn_norm_w = dn_bwd(y, proj, a_log, dt_bias, dn_norm_w, ssave, do,
                                                                    name="dn_bwd")
            dxc, dconv_w = conv_bwd(proj, conv_w, dy, name="conv_bwd")
            dps = [dq, dkv, dxc, dz, dsmall]
            grads["rel_bias_table"] = swa_bias_bwd(dbias, buckets, name="swa_bias_bwd")[:, :SWA_Q_HEADS]
            grads["swa_sinks"] = dsinks[:, :SWA_Q_HEADS]
            grads["dn_a_log"] = da_log[:, :DN_HEADS]
            grads["dn_dt_bias"] = ddt_bias[:, :DN_HEADS]
            grads["dn_norm_w"] = ddn_norm_w
            grads["even_conv_w"] = dconv_w[None]
            grads["even_w_out"] = dw_out
        else:
            (ssave,) = extra
            dq, dk, dv, dgate, dlow, dwgu, dbg, dgnw = gla_bwd(proj, w_gate_up, b_gate, gla_norm_w, ssave, do,
                                                               name="gla_bwd")
            dps = [dq, dk, dv, dgate, dlow]
            grads["gla_w_gate_up"] = dwgu[None, :GLA_GATE_RANK]
            grads["gla_b_gate"] = dbg
            grads["gla_norm_w"] = dgnw
            grads["odd_w_out"] = dw_out
        dw_in = jnp.concatenate([mm_tn(hn, dp, name=f"mix_dwi_{l}_{i}") for i, dp in enumerate(dps)], axis=1)
        if l == 0:
            grads["even_w_in"] = _even_proj_weight_grad(dw_in)
        else:
            grads["odd_w_in"] = dw_in[:, :wts["odd_w_in"].shape[1]]
        dh, dnw[l][2] = rms_mm_bwd(dps, w_in[l], h_mix, row(nw[l, 2]), dh, name=f"mix_in_bwd_{l}")
        dh, dnw[l][0], dnw[l][1], g_gu, g_down = _ffn_bwd(
            dh, s_a, row(nw[l, 0]), row(nw[l, 1]), w_gu, w_down, 2 * l, g_gu, g_down)

    grads["g_gu"], grads["g_down"] = g_gu, g_down
    grads["norm_w"] = jnp.stack([jnp.concatenate(r, axis=0) for r in dnw])
    grads["meta_tokens"] = dh[PAD:PAD + N_META]
    return loss[0, 0], dh[PAD + N_META:], grads


def _peer(k):
    x, y, c = (lax.axis_index(a) for a in AXES)
    flip = lambda v, bit: 1 - v if bit else v
    return (flip(x, k & 4), flip(y, k & 2), flip(c, k & 1))


def _my_index():
    x, y, c = (lax.axis_index(a) for a in AXES)
    return 4 * x + 2 * y + c


def exchange(srcs, outs, items, *, name):
    ns, no, ni = len(srcs), len(outs), len(items)

    def body(*refs):
        src_refs, out_refs = refs[:ns], refs[ns:ns + no]
        send_sems, recv_sems, local_sems = refs[ns + no:]
        me = _my_index()
        local = [pltpu.make_async_copy(send(src_refs[si], me), land(out_refs[oi], me), local_sems.at[a])
                 for a, (si, send, oi, land) in enumerate(items)]
        for cp in local:
            cp.start()
        copies = []
        for k in range(1, N_DEV):
            px, py, pc = _peer(k)
            pj = 4 * px + 2 * py + pc
            for a, (si, send, oi, land) in enumerate(items):
                sem = (k - 1) * ni + a
                cp = pltpu.make_async_remote_copy(src_ref=send(src_refs[si], pj), dst_ref=land(out_refs[oi], me),
                                                  send_sem=send_sems.at[sem], recv_sem=recv_sems.at[sem],
                                                  device_id=(px, py, pc), device_id_type=MESH)
                cp.start()
                copies.append(cp)
        for cp in copies:
            cp.wait_recv()
        for cp in copies:
            cp.wait_send()
        for cp in local:
            cp.wait()

    n_remote = (N_DEV - 1) * ni
    return pl.pallas_call(
        body, name=name,
        in_specs=[pl.BlockSpec(memory_space=pl.ANY)] * ns,
        out_specs=tuple(pl.BlockSpec(memory_space=pl.ANY) for _ in outs),
        out_shape=tuple(outs),
        scratch_shapes=[pltpu.SemaphoreType.DMA((n_remote,)), pltpu.SemaphoreType.DMA((n_remote,)),
                        pltpu.SemaphoreType.DMA((ni,))],
    )(*srcs)


def _block(index, size):
    return pl.ds(pl.multiple_of(index * size, LANE), size)


def _adam_tile(rows):
    for t in (256, 176, 128):
        if rows % t == 0:
            return t
    return rows


def sum_adamw(recv, col0, w, m, v, *, name):
    b, r, c = w.shape
    cp = recv.shape[-1]
    tr = _adam_tile(r)
    c1 = 1.0 / (1.0 - ADAM_B1 ** ADAM_STEP)
    c2 = 1.0 / (1.0 - ADAM_B2 ** ADAM_STEP)

    def body(recv_ref, w_ref, m_ref, v_ref, g_ref, d_ref, nm_ref, nv_ref):
        g = recv_ref[0, 0, :, col0:col0 + c].astype(F32)
        for i in range(1, N_DEV):
            g = g + recv_ref[i, 0, :, col0:col0 + c].astype(F32)
        nm = ADAM_B1 * m_ref[0] + (1.0 - ADAM_B1) * g
        nv = ADAM_B2 * v_ref[0] + (1.0 - ADAM_B2) * (g * g)
        g_ref[0] = g
        nm_ref[0] = nm
        nv_ref[0] = nv
        d_ref[0] = -ADAM_LR * ((nm * c1) / (jnp.sqrt(nv * c2) + ADAM_EPS) + ADAM_WD * w_ref[0])

    tile = pl.BlockSpec((1, tr, c), lambda bi, i: (bi, i, 0))
    return pl.pallas_call(
        body, name=name, grid=(b, r // tr),
        in_specs=[pl.BlockSpec((N_DEV, 1, tr, cp), lambda bi, i: (0, bi, i, 0)), tile, tile, tile],
        out_specs=(tile,) * 4, out_shape=(jax.ShapeDtypeStruct((b, r, c), F32),) * 4,
    )(recv, w, m, v)


def _flat_rows(n_elems, row_multiple):
    rows = -(-n_elems // FLAT_COLS)
    return -(-rows // row_multiple) * row_multiple


def _pack(arrays, row_multiple, dtype):
    flat = jnp.concatenate([a.reshape(-1).astype(dtype) for a in arrays])
    rows = _flat_rows(flat.size, row_multiple)
    return jnp.pad(flat, (0, rows * FLAT_COLS - flat.size)).reshape(rows, FLAT_COLS)


def _unpack(flat2d, shapes):
    lead = flat2d.shape[:-2]
    flat = flat2d.reshape(lead + (-1,))
    out, off = [], 0
    for shp in shapes:
        n = int(np.prod(shp))
        out.append(flat[..., off:off + n].reshape(lead + tuple(shp)))
        off += n
    return out


def _join_shards(stacked, axis):
    moved = jnp.moveaxis(stacked, 0, axis)
    shp = list(moved.shape)
    shp[axis:axis + 2] = [shp[axis] * shp[axis + 1]]
    return moved.reshape(shp)


def _split_shards(full, axis):
    shp = list(full.shape)
    shp[axis:axis + 1] = [N_DEV, shp[axis] // N_DEV]
    return jnp.moveaxis(full.reshape(shp), axis, 0)


def kernel(x, meta_tokens, norm_w, ffn_w_gate, ffn_w_up, ffn_w_down, rel_bias_table, even_w_in, even_conv_w, swa_sinks, dn_a_log, dn_dt_bias, dn_norm_w, even_w_out, odd_w_in, gla_w_gate_up, gla_b_gate, gla_norm_w, odd_w_out, loss_target, m_meta_tokens, m_norm_w, m_ffn_w_gate, m_ffn_w_up, m_ffn_w_down, m_rel_bias_table, m_even_w_in, m_even_conv_w, m_swa_sinks, m_dn_a_log, m_dn_dt_bias, m_dn_norm_w, m_even_w_out, m_odd_w_in, m_gla_w_gate_up, m_gla_b_gate, m_gla_norm_w, m_odd_w_out, v_meta_tokens, v_norm_w, v_ffn_w_gate, v_ffn_w_up, v_ffn_w_down, v_rel_bias_table, v_even_w_in, v_even_conv_w, v_swa_sinks, v_dn_a_log, v_dn_dt_bias, v_dn_norm_w, v_even_w_out, v_odd_w_in, v_gla_w_gate_up, v_gla_b_gate, v_gla_norm_w, v_odd_w_out):
    args = locals()
    w = {n: args[n] for n in WEIGHTS}
    m = {n: args["m_" + n] for n in WEIGHTS}
    v = {n: args["v_" + n] for n in WEIGHTS}

    d = D_MODEL
    sds = jax.ShapeDtypeStruct
    whole = lambda ref, j: ref
    cols = lambda size, base=0: (lambda ref, i: ref.at[(slice(None),) * (len(ref.shape) - 1)
                                                       + (pl.ds(pl.multiple_of(base + i * size, LANE), size),)])
    rows3 = lambda size: (lambda ref, i: ref.at[:, _block(i, size), :])
    rows2 = lambda size: (lambda ref, i: ref.at[_block(i, size), :])
    lead = lambda ref, i: ref.at[i]

    pad_cols = lambda a, to: jnp.pad(a, [(0, 0)] * (a.ndim - 1) + [(0, to - a.shape[-1])])
    gate_s = pad_cols(w["ffn_w_gate"].reshape(N_FFN, d, FF_SHARD), FF_SHARD_PAD).astype(BF16)
    up_s = pad_cols(w["ffn_w_up"].reshape(N_FFN, d, FF_SHARD), FF_SHARD_PAD).astype(BF16)
    down_s = jnp.pad(w["ffn_w_down"].reshape(N_FFN, FF_SHARD, d),
                     ((0, 0), (0, FF_SHARD_PAD - FF_SHARD), (0, 0))).astype(BF16)
    ein_s = pad_cols(w["even_w_in"][0], EVEN_IN_SHARD_PAD).astype(BF16)
    oin_s = pad_cols(w["odd_w_in"][0], ODD_IN_SHARD_PAD).astype(BF16)
    small_s = _pack([w[n] for n in SMALL], 8, F32)
    w_gu, w_down, ein_p, oin_p, eout, oout, small_all = exchange(
        [gate_s, up_s, down_s, ein_s, oin_s, w["even_w_out"][0].astype(BF16), w["odd_w_out"][0].astype(BF16), small_s],
        [sds((N_FFN, d, 2 * FF_PAD), BF16), sds((N_FFN, FF_PAD, d), BF16), sds((d, N_DEV * EVEN_IN_SHARD_PAD), BF16),
         sds((d, N_DEV * ODD_IN_SHARD_PAD), BF16), sds((d, d), BF16), sds((d, d), BF16),
         sds((N_DEV,) + small_s.shape, F32)],
        [(0, whole, 0, cols(FF_SHARD_PAD)), (1, whole, 0, cols(FF_SHARD_PAD, FF_PAD)), (2, whole, 1, rows3(FF_SHARD_PAD)),
         (3, whole, 2, cols(EVEN_IN_SHARD_PAD)), (4, whole, 3, cols(ODD_IN_SHARD_PAD)), (5, whole, 4, rows2(OUT_SHARD)),
         (6, whole, 5, rows2(OUT_SHARD)), (7, whole, 6, lead)],
        name="gather_weights")
    unpad = lambda p, shard, shard_pad: p.reshape(d, N_DEV, shard_pad)[:, :, :shard].reshape(d, N_DEV * shard)
    full = {n: w[n] for n in REPL}
    for n, stacked in zip(SMALL, _unpack(small_all, [w[n].shape for n in SMALL])):
        full[n] = _join_shards(stacked, SHARD_AXIS[n])
    full.update(w_gu=w_gu, w_down=w_down, even_w_out=eout, odd_w_out=oout,
                even_w_in=unpad(ein_p, EVEN_IN_SHARD, EVEN_IN_SHARD_PAD),
                odd_w_in=unpad(oin_p, ODD_IN_SHARD, ODD_IN_SHARD_PAD))

    loss, grad_x, grads = local_step(x[0], loss_target[0], full)
    loss = lax.psum(loss, AXES)

    repad = lambda g, shard, shard_pad: pad_cols(g.reshape(d, N_DEV, shard), shard_pad).reshape(d, N_DEV * shard_pad)
    order = SMALL + REPL
    pieces = [_split_shards(grads[n].reshape(full[n].shape), SHARD_AXIS[n]) if n in SHARD_AXIS
              else jnp.broadcast_to(grads[n].reshape(w[n].shape)[None], (N_DEV,) + w[n].shape) for n in order]
    flat = jnp.concatenate([p.reshape(N_DEV, -1) for p in pieces], axis=1)
    srows = _flat_rows(flat.shape[1], 8)
    flat = jnp.pad(flat, ((0, 0), (0, srows * FLAT_COLS - flat.shape[1]))).reshape(N_DEV, srows, FLAT_COLS)
    r_gu, r_down, r_ein, r_oin, r_eout, r_oout, r_small = exchange(
        [grads["g_gu"], grads["g_down"], repad(grads["even_w_in"], EVEN_IN_SHARD, EVEN_IN_SHARD_PAD).astype(BF16),
         repad(grads["odd_w_in"], ODD_IN_SHARD, ODD_IN_SHARD_PAD).astype(BF16), grads["even_w_out"].astype(BF16),
         grads["odd_w_out"].astype(BF16), flat],
        [sds((N_DEV, N_FFN, d, 2 * FF_SHARD_PAD), BF16), sds((N_DEV, N_FFN, FF_SHARD_PAD, d), BF16),
         sds((N_DEV, 1, d, EVEN_IN_SHARD_PAD), BF16), sds((N_DEV, 1, d, ODD_IN_SHARD_PAD), BF16),
         sds((N_DEV, 1, OUT_SHARD, d), BF16), sds((N_DEV, 1, OUT_SHARD, d), BF16), sds((N_DEV, 1, srows, FLAT_COLS), F32)],
        [(0, cols(FF_SHARD_PAD), 0, lambda ref, i: ref.at[i, :, :, pl.ds(0, FF_SHARD_PAD)]),
         (0, cols(FF_SHARD_PAD, FF_PAD), 0, lambda ref, i: ref.at[i, :, :, pl.ds(FF_SHARD_PAD, FF_SHARD_PAD)]),
         (1, rows3(FF_SHARD_PAD), 1, lead), (2, cols(EVEN_IN_SHARD_PAD), 2, lambda ref, i: ref.at[i, 0]),
         (3, cols(ODD_IN_SHARD_PAD), 3, lambda ref, i: ref.at[i, 0]), (4, rows2(OUT_SHARD), 4, lambda ref, i: ref.at[i, 0]),
         (5, rows2(OUT_SHARD), 5, lambda ref, i: ref.at[i, 0]), (6, lead, 6, lambda ref, i: ref.at[i, 0])],
        name="exchange_grads")

    result = [{} for _ in range(4)]

    def update(names, recv, col0, view, back):
        for n in names:
            outs = sum_adamw(recv, col0, view(w[n]), view(m[n]), view(v[n]), name=f"adamw_{n}")
            for r, o in zip(result, outs):
                r[n] = back(o, n)

    as_given = lambda o, n: o.reshape(w[n].shape)
    update(["ffn_w_gate"], r_gu, 0, lambda a: a.reshape(N_FFN, d, FF_SHARD), as_given)
    update(["ffn_w_up"], r_gu, FF_SHARD_PAD, lambda a: a.reshape(N_FFN, d, FF_SHARD), as_given)
    update(["ffn_w_down"], r_down, 0, lambda a: a.reshape(N_FFN, FF_SHARD, d), as_given)
    update(["even_w_in"], r_ein, 0, lambda a: a, as_given)
    update(["odd_w_in"], r_oin, 0, lambda a: a, as_given)
    update(["even_w_out"], r_eout, 0, lambda a: a, as_given)
    update(["odd_w_out"], r_oout, 0, lambda a: a, as_given)
    pack_local = lambda t: _pack([t[n] for n in order], 8, F32)[None]
    small_outs = sum_adamw(r_small, 0, pack_local(w), pack_local(m), pack_local(v), name="adamw_small")
    for r, o in zip(result, small_outs):
        r.update(zip(order, _unpack(o[0], [w[n].shape for n in order])))
    return (loss, grad_x[None], *[r[n] for r in result for n in WEIGHTS])
```

```python
import functools
import math

import numpy as np
import jax
import jax.numpy as jnp
from jax import lax
from jax.experimental import pallas as pl
from jax.experimental.pallas import tpu as pltpu

F32 = jnp.float32
BF16 = jnp.bfloat16
MESH = pl.DeviceIdType.MESH
AXES = ("x", "y", "c")
N_DEV = 8

D_MODEL = 1024
N_META = 16
D_FF = 2816
NORM_EPS = 1e-6
NEG_INF = -1e30
SWA_Q_HEADS = 8
SWA_HEAD_DIM = 64
SWA_WINDOW = 128
SWA_BLOCK = 128
REL_BUCKETS = 32
REL_MAX_DIST = 128
DN_HEADS = 4
DN_HEAD_DIM = 128
DN_CONV = 4
GLA_HEADS = 4
GLA_DK = 128
GLA_DV = 256
GLA_GATE_RANK = 16
GLA_GATE_NORM = 16.0
CHUNK = 64
PAD = SWA_BLOCK - N_META
LANE = 128
PROJ_DIM = 3200

ADAM_LR = 0.001
ADAM_B1 = 0.9
ADAM_B2 = 0.999
ADAM_EPS = 1e-08
ADAM_WD = 0.01
ADAM_STEP = 10

FF_SHARD = D_FF // N_DEV
FF_SHARD_PAD = 384
FF_PAD = N_DEV * FF_SHARD_PAD
N_FFN = 4
EVEN_IN_SHARD, EVEN_IN_SHARD_PAD = 353, 384
ODD_IN_SHARD, ODD_IN_SHARD_PAD = 386, 512
OUT_SHARD = D_MODEL // N_DEV

FLAT_COLS = 128
BIG = ("ffn_w_gate", "ffn_w_up", "ffn_w_down", "even_w_in", "even_w_out", "odd_w_in", "odd_w_out")
SMALL = ("meta_tokens", "norm_w", "even_conv_w", "gla_w_gate_up", "gla_b_gate", "gla_norm_w")
REPL = ("rel_bias_table", "swa_sinks", "dn_a_log", "dn_dt_bias", "dn_norm_w")
WEIGHTS = ("meta_tokens", "norm_w", "ffn_w_gate", "ffn_w_up", "ffn_w_down", "rel_bias_table", "even_w_in",
           "even_conv_w", "swa_sinks", "dn_a_log", "dn_dt_bias", "dn_norm_w", "even_w_out", "odd_w_in",
           "gla_w_gate_up", "gla_b_gate", "gla_norm_w", "odd_w_out")
SHARD_AXIS = {"ffn_w_gate": 3, "ffn_w_up": 3, "ffn_w_down": 2, "even_w_in": 2, "even_w_out": 1, "odd_w_in": 2,
              "odd_w_out": 1, "meta_tokens": 1, "norm_w": 2, "even_conv_w": 2, "gla_w_gate_up": 2,
              "gla_b_gate": 1, "gla_norm_w": 1}


def _rms(x, w):
    r = lax.rsqrt(jnp.mean(x * x, axis=-1, keepdims=True) + NORM_EPS)
    return x * r * w


def _sigmoid(x):
    return 0.5 * (jnp.tanh(0.5 * x) + 1.0)


def _silu(x):
    return x * _sigmoid(x)


def _softplus(x):
    pos = x > 0
    return jnp.where(pos, x, 0.0) + jnp.log(1.0 + jnp.exp(jnp.where(pos, -x, x)))


def _l2n(x):
    return x * lax.rsqrt(jnp.sum(x * x, axis=-1, keepdims=True) + 1e-6)


def _split_bf16(x):
    hi = x.astype(BF16)
    return hi, (x - hi.astype(F32)).astype(BF16)


def _make_mm(terms, batched):
    off = 1 if batched else 0
    bdims = ((0,), (0,)) if batched else ((), ())

    def dg(a, b, ca, cb):
        dot = lambda p, q: lax.dot_general(p, q, (((ca + off,), (cb + off,)), bdims), preferred_element_type=F32)
        a_hi, a_lo = _split_bf16(a)
        b_hi, b_lo = _split_bf16(b)
        if terms == 1:
            return dot(a_hi, b_hi)
        return dot(a_hi, b_hi) + (dot(a_hi, b_lo) + dot(a_lo, b_hi))

    @jax.custom_vjp
    def nn(a, b):
        return dg(a, b, 1, 0)

    @jax.custom_vjp
    def nt(a, b):
        return dg(a, b, 1, 1)

    @jax.custom_vjp
    def tn(a, b):
        return dg(a, b, 0, 0)

    nn.defvjp(lambda a, b: (nn(a, b), (a, b)), lambda r, g: (nt(g, r[1]), tn(r[0], g)))
    nt.defvjp(lambda a, b: (nt(a, b), (a, b)), lambda r, g: (nn(g, r[1]), tn(g, r[0])))
    tn.defvjp(lambda a, b: (tn(a, b), (a, b)), lambda r, g: (nt(r[1], g), nn(r[0], g)))
    return nn, nt, tn


_mm, _mm_nt, _mm_tn = _make_mm(1, False)
_mm3, _, _ = _make_mm(3, False)
_bmm, _bmm_nt, _bmm_tn = _make_mm(1, True)
_bmm3, _, _ = _make_mm(3, True)


def _tri_ones_dot(x, lower):
    n = x.shape[0]
    r = lax.broadcasted_iota(jnp.int32, (n, n), 0)
    c = lax.broadcasted_iota(jnp.int32, (n, n), 1)
    t = ((r >= c) if lower else (r <= c)).astype(BF16)
    hi, lo = _split_bf16(x)
    return jnp.dot(t, hi, preferred_element_type=F32) + jnp.dot(t, lo, preferred_element_type=F32)


@jax.custom_vjp
def _cumsum_rows(x):
    return _tri_ones_dot(x, True)


_cumsum_rows.defvjp(lambda x: (_tri_ones_dot(x, True), None), lambda _, g: (_tri_ones_dot(g, False),))


def _row_tile(n_rows, cap):
    best = LANE
    for t in range(LANE, cap + 1, LANE):
        if n_rows % t == 0:
            best = t
    return best


def _real_rows(tile_index, tm):
    row = tile_index * tm + lax.broadcasted_iota(jnp.int32, (tm, 1), 0)
    return (row >= PAD).astype(F32)


def _full(shape):
    return pl.BlockSpec(shape, lambda *_: (0,) * len(shape))


def _resident(shape):
    return pl.BlockSpec(shape, lambda *_: (0,) * len(shape), pipeline_mode=pl.Buffered(1))


def _resident_w(wmat, widx):
    if wmat.ndim == 2:
        return _resident(wmat.shape)
    return pl.BlockSpec((None,) + wmat.shape[1:], lambda *_: (widx, 0, 0), pipeline_mode=pl.Buffered(1))


def rms_mm(h, w, wmat, *, swiglu, name, widx=None):
    tp, d = h.shape
    n = wmat.shape[-1]
    tm = _row_tile(tp, 384)
    half = n // 2

    def body(h_ref, w_ref, wm_ref, hn_ref, *outs):
        hn = _rms(h_ref[...], w_ref[...]).astype(BF16)
        hn_ref[...] = hn
        p = jnp.dot(hn, wm_ref[...], preferred_element_type=F32)
        if swiglu:
            g, u = p[:, :half], p[:, half:]
            outs[0][...] = g.astype(BF16)
            outs[1][...] = u.astype(BF16)
            outs[2][...] = (_silu(g) * u).astype(BF16)
        else:
            outs[0][...] = p

    row = lambda width: pl.BlockSpec((tm, width), lambda i: (i, 0))
    if swiglu:
        out_shape = (jax.ShapeDtypeStruct((tp, d), BF16),) + (jax.ShapeDtypeStruct((tp, half), BF16),) * 3
        out_specs = (row(d), row(half), row(half), row(half))
    else:
        out_shape = (jax.ShapeDtypeStruct((tp, d), BF16), jax.ShapeDtypeStruct((tp, n), F32))
        out_specs = (row(d), row(n))
    return pl.pallas_call(
        body, name=name, grid=(tp // tm,),
        in_specs=[row(d), _full((1, d)), _resident_w(wmat, widx)],
        out_specs=out_specs, out_shape=out_shape,
    )(h, w, wmat)


def mm_rms_res(acts, wmat, h, w, *, scale, name, widx=None):
    tp, d = h.shape
    tm = _row_tile(tp, 384)
    widths = [a.shape[1] for a in acts]
    offs = [sum(widths[:i]) for i in range(len(acts))]
    na = len(acts)

    def body(*refs):
        a_refs = refs[:na]
        wm_ref, h_ref, w_ref, f_ref, ho_ref = refs[na:]
        f = None
        for a_ref, off, width in zip(a_refs, offs, widths):
            part = jnp.dot(a_ref[...].astype(BF16), wm_ref[off:off + width, :], preferred_element_type=F32)
            f = part if f is None else f + part
        f_ref[...] = f
        ho_ref[...] = h_ref[...] + scale * _rms(f, w_ref[...])

    row = lambda width: pl.BlockSpec((tm, width), lambda i: (i, 0))
    return pl.pallas_call(
        body, name=name, grid=(tp // tm,),
        in_specs=[row(wd) for wd in widths] + [_resident_w(wmat, widx), row(d), _full((1, d))],
        out_specs=(row(d), row(d)),
        out_shape=(jax.ShapeDtypeStruct((tp, d), F32), jax.ShapeDtypeStruct((tp, d), F32)),
    )(*acts, wmat, h, w)


def mm_rms_res_bwd(dho, f, w, wmat, gu, *, scale, name, widx=None):
    tp, d = f.shape
    k = wmat.shape[-2]
    tm = _row_tile(tp, 384)
    swiglu = gu is not None

    def body(*refs):
        if swiglu:
            dho_ref, f_ref, w_ref, wm_ref, g_ref, u_ref, df_ref, dw_ref, dgu_ref = refs
        else:
            dho_ref, f_ref, w_ref, wm_ref, df_ref, dw_ref, da_ref = refs
        i = pl.program_id(0)
        _, vjp = jax.vjp(lambda ff, ww: scale * _rms(ff, ww), f_ref[...], w_ref[...])
        df, dw = vjp(dho_ref[...])
        dfb = (df * _real_rows(i, tm)).astype(BF16)
        df_ref[...] = dfb

        @pl.when(i == 0)
        def _():
            dw_ref[...] = jnp.zeros_like(dw_ref)

        dw_ref[...] += dw
        da = lax.dot_general(dfb, wm_ref[...], (((1,), (1,)), ((), ())), preferred_element_type=F32)
        if swiglu:
            g = g_ref[...].astype(F32)
            u = u_ref[...].astype(F32)
            s = _sigmoid(g)
            dgu_ref[:, :k] = (da * u * s * (1.0 + g * (1.0 - s))).astype(BF16)
            dgu_ref[:, k:] = (da * g * s).astype(BF16)
        else:
            da_ref[...] = da

    row = lambda width: pl.BlockSpec((tm, width), lambda i: (i, 0))
    in_specs = [row(d), row(d), _full((1, d)), _resident_w(wmat, widx)]
    args = [dho, f, w, wmat]
    out_shape = [jax.ShapeDtypeStruct((tp, d), BF16), jax.ShapeDtypeStruct((1, d), F32)]
    out_specs = [row(d), _full((1, d))]
    if swiglu:
        in_specs += [row(k), row(k)]
        args += list(gu)
        out_shape += [jax.ShapeDtypeStruct((tp, 2 * k), BF16)]
        out_specs += [row(2 * k)]
    else:
        out_shape += [jax.ShapeDtypeStruct((tp, k), F32)]
        out_specs += [row(k)]
    return pl.pallas_call(body, name=name, grid=(tp // tm,), in_specs=in_specs, out_specs=tuple(out_specs),
                          out_shape=tuple(out_shape))(*args)


def rms_mm_bwd(dps, wmat, h, w, dho, *, name, widx=None):
    tp, d = h.shape
    tm = _row_tile(tp, 384)
    widths = [p.shape[1] for p in dps]
    offs = [sum(widths[:i]) for i in range(len(dps))]
    ndp = len(dps)

    def body(*refs):
        dp_refs = refs[:ndp]
        wm_ref, h_ref, w_ref, dho_ref, dh_ref, dw_ref = refs[ndp:]
        i = pl.program_id(0)
        dhn = None
        for dp_ref, off, width in zip(dp_refs, offs, widths):
            part = lax.dot_general(dp_ref[...].astype(BF16), wm_ref[:, off:off + width], (((1,), (1,)), ((), ())),
                                   preferred_element_type=F32)
            dhn = part if dhn is None else dhn + part
        _, vjp = jax.vjp(_rms, h_ref[...], w_ref[...])
        dx, dw = vjp(dhn)
        dh_ref[...] = (dho_ref[...] + dx) * _real_rows(i, tm)

        @pl.when(i == 0)
        def _():
            dw_ref[...] = jnp.zeros_like(dw_ref)

        dw_ref[...] += dw

    row = lambda width: pl.BlockSpec((tm, width), lambda i: (i, 0))
    return pl.pallas_call(
        body, name=name, grid=(tp // tm,),
        in_specs=[row(wd) for wd in widths] + [_resident_w(wmat, widx), row(d), _full((1, d)), row(d)],
        out_specs=(row(d), _full((1, d))),
        out_shape=(jax.ShapeDtypeStruct((tp, d), F32), jax.ShapeDtypeStruct((1, d), F32)),
    )(*dps, wmat, h, w, dho)


def mm_tn(a, b, *, name, into=None, slab=None):
    t, m = a.shape
    n = b.shape[1]
    bm = _row_tile(m, 512)
    bn = _row_tile(n, 1536)
    bk = _row_tile(t, 1408)
    nk = t // bk

    def body(a_ref, b_ref, *rest):
        o_ref, acc = rest[-2], rest[-1]

        @pl.when(pl.program_id(2) == 0)
        def _():
            acc[...] = jnp.zeros_like(acc)

        acc[...] += lax.dot_general(a_ref[...].astype(BF16), b_ref[...].astype(BF16), (((0,), (0,)), ((), ())),
                                    preferred_element_type=F32)

        @pl.when(pl.program_id(2) == nk - 1)
        def _():
            o_ref[...] = acc[...].astype(o_ref.dtype)

    in_specs = [pl.BlockSpec((bk, bm), lambda i, j, kk: (kk, i)), pl.BlockSpec((bk, bn), lambda i, j, kk: (kk, j))]
    scratch = [pltpu.VMEM((bm, bn), F32)]
    if into is None:
        return pl.pallas_call(
            body, name=name, grid=(m // bm, n // bn, nk), in_specs=in_specs,
            out_specs=pl.BlockSpec((bm, bn), lambda i, j, kk: (i, j)),
            out_shape=jax.ShapeDtypeStruct((m, n), F32), scratch_shapes=scratch,
        )(a, b)
    return pl.pallas_call(
        body, name=name, grid=(m // bm, n // bn, nk),
        in_specs=in_specs + [pl.BlockSpec(memory_space=pl.ANY)],
        out_specs=pl.BlockSpec((None, bm, bn), lambda i, j, kk: (slab, i, j)),
        out_shape=jax.ShapeDtypeStruct(into.shape, into.dtype), scratch_shapes=scratch,
        input_output_aliases={2: 0},
    )(a, b, into)


def loss_and_grad(h, target, *, name):
    tp, d = h.shape
    tm = SWA_BLOCK

    def body(h_ref, t_ref, dh_ref, loss_ref):
        i = pl.program_id(0)

        @pl.when(i == 0)
        def _():
            loss_ref[...] = jnp.zeros_like(loss_ref)
            dh_ref[...] = jnp.zeros_like(dh_ref)

        @pl.when(i > 0)
        def _():
            err = h_ref[...] - t_ref[...]
            dh_ref[...] = err * (1.0 / d)
            loss_ref[...] += 0.5 * jnp.sum(jnp.sum(err * err, axis=1, keepdims=True), axis=0, keepdims=True) * (1.0 / d)

    return pl.pallas_call(
        body, name=name, grid=(tp // tm,),
        in_specs=[pl.BlockSpec((tm, d), lambda i: (i, 0)), pl.BlockSpec((tm, d), lambda i: (jnp.maximum(i - 1, 0), 0))],
        out_specs=(pl.BlockSpec((tm, d), lambda i: (i, 0)), _full((1, 1))),
        out_shape=(jax.ShapeDtypeStruct((tp, d), F32), jax.ShapeDtypeStruct((1, 1), F32)),
    )(h, target)


def _t5_bucket_np(rel):
    n = np.maximum(rel, 0)
    max_exact = REL_BUCKETS // 2
    n_f = np.maximum(n, 1).astype(np.float32)
    large = max_exact + (np.log(n_f / np.float32(max_exact)) / np.float32(math.log(REL_MAX_DIST / max_exact))
                         * np.float32(REL_BUCKETS - max_exact)).astype(np.int32)
    large = np.minimum(large, REL_BUCKETS - 1)
    return np.where(n < max_exact, n, large).astype(np.int32)


def _swa_positions_np(n):
    i = np.arange(SWA_BLOCK)[:, None]
    j = np.arange(3 * SWA_BLOCK)[None, :]
    pos_q = n * SWA_BLOCK + i - PAD
    pos_k = np.where(j < SWA_BLOCK, j - PAD, (n - 1) * SWA_BLOCK + (j - SWA_BLOCK) - PAD)
    return pos_q, pos_k


def _swa_buckets():
    out = []
    for n in range(3):
        pos_q, pos_k = _swa_positions_np(n)
        out.append(_t5_bucket_np(pos_q - pos_k))
    return jnp.asarray(np.stack(out))


def swa_bias(table, buckets, *, name):
    nc, nq, nk = buckets.shape

    def body(tab_ref, bkt_ref, out_ref):
        for c in range(nc):
            bkt = bkt_ref[c]
            for h in range(SWA_Q_HEADS):
                acc = jnp.zeros((nq, nk), F32)
                for b in range(REL_BUCKETS):
                    acc = jnp.where(bkt == b, tab_ref[b, h], acc)
                out_ref[c, h] = acc

    return pl.pallas_call(
        body, name=name,
        in_specs=[pl.BlockSpec(memory_space=pltpu.SMEM), pl.BlockSpec(memory_space=pltpu.VMEM)],
        out_specs=pl.BlockSpec(memory_space=pltpu.VMEM),
        out_shape=jax.ShapeDtypeStruct((nc, SWA_Q_HEADS, nq, nk), F32),
    )(table, buckets)


def swa_bias_bwd(dbias, buckets, *, name):
    nc = buckets.shape[0]

    def body(db_ref, bkt_ref, out_ref):
        lane = lax.broadcasted_iota(jnp.int32, (1, LANE), 1)
        for b in range(REL_BUCKETS):
            row = jnp.zeros((1, LANE), F32)
            for c in range(nc):
                hit = bkt_ref[c] == b
                for h in range(SWA_Q_HEADS):
                    part = jnp.where(hit, db_ref[c, h], 0.0)
                    tot = jnp.sum(jnp.sum(part, axis=1, keepdims=True), axis=0, keepdims=True)
                    row = row + jnp.where(lane == h, tot, 0.0)
            out_ref[b:b + 1, :] = row

    return pl.pallas_call(
        body, name=name,
        in_specs=[pl.BlockSpec(memory_space=pltpu.VMEM), pl.BlockSpec(memory_space=pltpu.VMEM)],
        out_specs=pl.BlockSpec(memory_space=pltpu.VMEM),
        out_shape=jax.ShapeDtypeStruct((REL_BUCKETS, LANE), F32),
    )(dbias, buckets)


def _swa_block(q, kvm, kvp, kvc, bias, sinks, n):
    blk = SWA_BLOCK
    i = lax.broadcasted_iota(jnp.int32, (blk, 3 * blk), 0)
    j = lax.broadcasted_iota(jnp.int32, (blk, 3 * blk), 1)
    pos_q = n * blk + i - PAD
    is_meta = j < blk
    pos_k = jnp.where(is_meta, j - PAD, (n - 1) * blk + (j - blk) - PAD)
    rel = pos_q - pos_k
    valid = ((is_meta & (pos_k >= 0) & (pos_k < N_META) & (rel >= 0))
             | (jnp.logical_not(is_meta) & (pos_k >= N_META) & (rel >= 0) & (rel < SWA_WINDOW)))
    kv = jnp.concatenate([kvm, kvp, kvc], axis=0)
    lane = lax.broadcasted_iota(jnp.int32, (1, LANE), 1)
    halves = ((lane < SWA_HEAD_DIM).astype(F32), (lane >= SWA_HEAD_DIM).astype(F32))
    scale = SWA_HEAD_DIM ** -0.5
    outs = []
    for pair in range(SWA_Q_HEADS // 2):
        qp = q[:, pair * LANE:(pair + 1) * LANE]
        grp = pair // 2
        kg = kv[:, grp * LANE:(grp + 1) * LANE]
        vg = kv[:, (2 + grp) * LANE:(3 + grp) * LANE]
        op = None
        for hh in range(2):
            h = 2 * pair + hh
            s = _mm_nt(qp * halves[hh], kg) * scale + bias[h]
            s = jnp.where(valid, s, NEG_INF)
            sink = jnp.sum(jnp.where(lane == h, sinks, 0.0), axis=1, keepdims=True)
            m = lax.stop_gradient(jnp.maximum(jnp.max(s, axis=1, keepdims=True), sink))
            e = jnp.exp(s - m)
            den = jnp.sum(e, axis=1, keepdims=True) + jnp.exp(sink - m)
            part = _mm(e / den, vg) * halves[hh]
            op = part if op is None else op + part
        outs.append(op)
    return jnp.concatenate(outs, axis=1)


def _swa_in_specs(nb, rev):
    blk = SWA_BLOCK
    step = (lambda i: nb - 1 - i) if rev else (lambda i: i)
    return [
        pl.BlockSpec((blk, 4 * LANE), lambda i: (step(i), 0)),
        pl.BlockSpec((blk, 4 * LANE), lambda i: (0, 1)),
        pl.BlockSpec((blk, 4 * LANE), lambda i: (jnp.maximum(step(i) - 1, 0), 1)),
        pl.BlockSpec((blk, 4 * LANE), lambda i: (step(i), 1)),
        pl.BlockSpec((1, SWA_Q_HEADS, blk, 3 * blk), lambda i: (jnp.minimum(step(i), 2), 0, 0, 0)),
        _full((1, LANE)),
    ]


def swa_fwd(proj, bias, sinks, *, name):
    tp = proj.shape[0]
    nb = tp // SWA_BLOCK

    def body(q_ref, kvm_ref, kvp_ref, kvc_ref, bias_ref, sinks_ref, o_ref):
        n = pl.program_id(0)
        o_ref[...] = _swa_block(q_ref[...], kvm_ref[...], kvp_ref[...], kvc_ref[...], bias_ref[0], sinks_ref[...], n)

    return pl.pallas_call(
        body, name=name, grid=(nb,),
        in_specs=_swa_in_specs(nb, False),
        out_specs=pl.BlockSpec((SWA_BLOCK, 4 * LANE), lambda i: (i, 0)),
        out_shape=jax.ShapeDtypeStruct((tp, 4 * LANE), F32),
    )(proj, proj, proj, proj, bias, sinks)


def swa_bwd(proj, bias, sinks, do, *, name):
    tp = proj.shape[0]
    nb = tp // SWA_BLOCK
    blk = SWA_BLOCK

    def body(q_ref, kvm_ref, kvp_ref, kvc_ref, bias_ref, sinks_ref, do_ref, dq_ref, dkv_ref, dbias_ref, dsinks_ref,
             carry, meta_acc):
        i = pl.program_id(0)
        n = nb - 1 - i

        @pl.when(i == 0)
        def _():
            carry[...] = jnp.zeros_like(carry)
            meta_acc[...] = jnp.zeros_like(meta_acc)
            dsinks_ref[...] = jnp.zeros_like(dsinks_ref)

        fn = lambda q, kvm, kvp, kvc, b, s: _swa_block(q, kvm, kvp, kvc, b, s, n)
        _, vjp = jax.vjp(fn, q_ref[...], kvm_ref[...], kvp_ref[...], kvc_ref[...], bias_ref[0], sinks_ref[...])
        dq, dkvm, dkvp, dkvc, dbias, dsinks = vjp(do_ref[...])
        dq_ref[...] = dq
        meta_acc[...] += dkvm
        dkv_ref[...] = dkvc + carry[...] + jnp.where(n == 0, meta_acc[...], 0.0)
        carry[...] = dkvp
        first_visit = (n == nb - 1) | (n < 2)

        @pl.when(first_visit)
        def _():
            dbias_ref[0] = dbias

        @pl.when(jnp.logical_not(first_visit))
        def _():
            dbias_ref[0] += dbias

        dsinks_ref[...] += dsinks

    rev = lambda i: nb - 1 - i
    return pl.pallas_call(
        body, name=name, grid=(nb,),
        in_specs=_swa_in_specs(nb, True) + [pl.BlockSpec((blk, 4 * LANE), lambda i: (rev(i), 0))],
        out_specs=(pl.BlockSpec((blk, 4 * LANE), lambda i: (rev(i), 0)),
                   pl.BlockSpec((blk, 4 * LANE), lambda i: (rev(i), 0)),
                   pl.BlockSpec((1, SWA_Q_HEADS, blk, 3 * blk), lambda i: (jnp.minimum(rev(i), 2), 0, 0, 0)),
                   _full((1, LANE))),
        out_shape=(jax.ShapeDtypeStruct((tp, 4 * LANE), F32), jax.ShapeDtypeStruct((tp, 4 * LANE), F32),
                   jax.ShapeDtypeStruct((3, SWA_Q_HEADS, blk, 3 * blk), F32), jax.ShapeDtypeStruct((1, LANE), F32)),
        scratch_shapes=[pltpu.VMEM((blk, 4 * LANE), F32), pltpu.VMEM((blk, 4 * LANE), F32)],
    )(proj, proj, proj, proj, bias, sinks, do)


CONV_COL0 = 2
HALO = 8


def conv_fwd(proj, conv_w, *, name):
    tp = proj.shape[0]
    tm = _row_tile(tp, 384)
    cw = 4 * LANE
    ncol = conv_w.shape[1] // cw

    def body(x_ref, halo_ref, w_ref, y_ref, buf):
        i = pl.program_id(1)
        buf[0:HALO, :] = jnp.where(i > 0, halo_ref[...], 0.0)
        buf[HALO:, :] = x_ref[...]
        acc = None
        for j in range(DN_CONV):
            term = w_ref[j:j + 1, :] * buf[pl.ds(HALO - (DN_CONV - 1) + j, tm), :]
            acc = term if acc is None else acc + term
        y_ref[...] = acc

    return pl.pallas_call(
        body, name=name, grid=(ncol, tp // tm),
        in_specs=[pl.BlockSpec((tm, cw), lambda c, i: (i, CONV_COL0 + c)),
                  pl.BlockSpec((HALO, cw), lambda c, i: (jnp.maximum(i * (tm // HALO) - 1, 0), CONV_COL0 + c)),
                  pl.BlockSpec((DN_CONV, cw), lambda c, i: (0, c))],
        out_specs=pl.BlockSpec((tm, cw), lambda c, i: (i, c)),
        out_shape=jax.ShapeDtypeStruct((tp, ncol * cw), F32),
        scratch_shapes=[pltpu.VMEM((tm + HALO, cw), F32)],
    )(proj, proj, conv_w)


def conv_bwd(proj, conv_w, dy, *, name):
    tp = proj.shape[0]
    tm = _row_tile(tp, 384)
    cw = 4 * LANE
    ncol = conv_w.shape[1] // cw
    nt = tp // tm

    def body(x_ref, xhalo_ref, w_ref, dy_ref, dyhalo_ref, dx_ref, dw_ref, xbuf, dbuf):
        i = pl.program_id(1)
        xbuf[0:HALO, :] = jnp.where(i > 0, xhalo_ref[...], 0.0)
        xbuf[HALO:, :] = x_ref[...]
        dbuf[0:tm, :] = dy_ref[...]
        dbuf[tm:, :] = jnp.where(i < nt - 1, dyhalo_ref[...], 0.0)
        dy_t = dy_ref[...]
        acc = None
        rows = []
        for j in range(DN_CONV):
            term = w_ref[j:j + 1, :] * dbuf[pl.ds(DN_CONV - 1 - j, tm), :]
            acc = term if acc is None else acc + term
            rows.append(jnp.sum(dy_t * xbuf[pl.ds(HALO - (DN_CONV - 1) + j, tm), :], axis=0, keepdims=True))
        dx_ref[...] = acc

        @pl.when(i == 0)
        def _():
            dw_ref[...] = jnp.zeros_like(dw_ref)

        for j in range(DN_CONV):
            dw_ref[j:j + 1, :] += rows[j]

    return pl.pallas_call(
        body, name=name, grid=(ncol, nt),
        in_specs=[pl.BlockSpec((tm, cw), lambda c, i: (i, CONV_COL0 + c)),
                  pl.BlockSpec((HALO, cw), lambda c, i: (jnp.maximum(i * (tm // HALO) - 1, 0), CONV_COL0 + c)),
                  pl.BlockSpec((DN_CONV, cw), lambda c, i: (0, c)),
                  pl.BlockSpec((tm, cw), lambda c, i: (i, c)),
                  pl.BlockSpec((HALO, cw), lambda c, i: (jnp.minimum((i + 1) * (tm // HALO), tp // HALO - 1), c))],
        out_specs=(pl.BlockSpec((tm, cw), lambda c, i: (i, c)), pl.BlockSpec((DN_CONV, cw), lambda c, i: (0, c))),
        out_shape=(jax.ShapeDtypeStruct((tp, ncol * cw), F32), jax.ShapeDtypeStruct((DN_CONV, ncol * cw), F32)),
        scratch_shapes=[pltpu.VMEM((tm + HALO, cw), F32), pltpu.VMEM((tm + HALO, cw), F32)],
    )(proj, proj, conv_w, dy, dy)


def _chunk_masks():
    r = lax.broadcasted_iota(jnp.int32, (CHUNK, CHUNK), 0)
    c = lax.broadcasted_iota(jnp.int32, (CHUNK, CHUNK), 1)
    return (r >= c).astype(F32), (r > c).astype(F32), (r == c).astype(F32)


def _dn_chunk(y, z, small, s, a_log, dt_bias, norm_w, rows):
    tri_incl, tri_strict, eye = _chunk_masks()
    lane = lax.broadcasted_iota(jnp.int32, (1, LANE), 1)
    dk = DN_HEAD_DIM
    nh = DN_HEADS
    heads = lambda t, first: jnp.stack([t[:, (first + h) * dk:(first + h + 1) * dk] for h in range(nh)])
    pick = lambda t, l: jnp.sum(jnp.where(lane == l, t, 0.0), axis=1, keepdims=True)
    q = _l2n(_silu(heads(y, 0))) * dk ** -0.5
    k = _l2n(_silu(heads(y, nh)))
    v = _silu(heads(y, 2 * nh))
    g_all = jnp.where(lane < nh, -jnp.exp(a_log) * _softplus(small + dt_bias), 0.0) * rows
    beta_all = _sigmoid(small)
    gc_all = _cumsum_rows(g_all)
    g_sum = jnp.sum(g_all, axis=0, keepdims=True)
    gc = jnp.stack([pick(gc_all, h) for h in range(nh)])
    beta = jnp.stack([pick(beta_all, nh + h) for h in range(nh)])
    g_last = jnp.stack([pick(g_sum, h) for h in range(nh)])
    gc_row = jnp.sum(eye * gc, axis=1, keepdims=True)
    gamma = jnp.exp((gc - gc_row) * tri_incl) * tri_incl
    k_beta = k * beta
    v_beta = v * beta
    a = _bmm_nt(k_beta, k) * gamma * tri_strict
    inv = eye - a
    power = a
    for _ in range(5):
        power = _bmm3(power, power)
        inv = inv + _bmm3(inv, power)
    e_gc = jnp.exp(gc)
    uw = _bmm3(inv, jnp.concatenate([v_beta, k_beta * e_gc], axis=2))
    u, w = uw[:, :, :dk], uw[:, :, dk:]
    attn = _bmm_nt(q, k) * gamma
    q_dec = q * e_gc
    k_dec = k * jnp.exp(g_last - gc)
    v_new = u - _bmm(w, s)
    o = _bmm(q_dec, s) + _bmm(attn, v_new)
    s_new = s * jnp.exp(g_last) + _bmm_tn(k_dec, v_new)
    out = _rms(o, norm_w) * _silu(heads(z, 0))
    return jnp.concatenate([out[h] for h in range(nh)], axis=1), s_new


Z_COL = 5
SMALL_COL = 24


def _chunk_rows(n):
    row = n * CHUNK + lax.broadcasted_iota(jnp.int32, (CHUNK, 1), 0)
    return (row >= PAD).astype(F32)


def dn_fwd(y, proj, a_log, dt_bias, norm_w, *, name):
    tp = y.shape[0]
    nc = tp // CHUNK
    dk = DN_HEAD_DIM

    def body(y_ref, z_ref, small_ref, al_ref, dt_ref, nw_ref, o_ref, ssave_ref, state):
        n = pl.program_id(0)

        @pl.when(n == 0)
        def _():
            state[...] = jnp.zeros_like(state)

        ssave_ref[0] = state[...]
        out, s_new = _dn_chunk(y_ref[...], z_ref[...], small_ref[...], state[...], al_ref[...], dt_ref[...],
                               nw_ref[...], _chunk_rows(n))
        o_ref[...] = out
        state[...] = s_new

    return pl.pallas_call(
        body, name=name, grid=(nc,),
        in_specs=[pl.BlockSpec((CHUNK, y.shape[1]), lambda n: (n, 0)),
                  pl.BlockSpec((CHUNK, 4 * LANE), lambda n: (n, Z_COL)),
                  pl.BlockSpec((CHUNK, LANE), lambda n: (n, SMALL_COL)),
                  _full((1, LANE)), _full((1, LANE)), _full((1, LANE))],
        out_specs=(pl.BlockSpec((CHUNK, 4 * LANE), lambda n: (n, 0)),
                   pl.BlockSpec((1, DN_HEADS, dk, dk), lambda n: (n, 0, 0, 0))),
        out_shape=(jax.ShapeDtypeStruct((tp, 4 * LANE), F32), jax.ShapeDtypeStruct((nc, DN_HEADS, dk, dk), F32)),
        scratch_shapes=[pltpu.VMEM((DN_HEADS, dk, dk), F32)],
    )(y, proj, proj, a_log, dt_bias, norm_w)


def dn_bwd(y, proj, a_log, dt_bias, norm_w, ssave, do, *, name):
    tp = y.shape[0]
    nc = tp // CHUNK
    dk = DN_HEAD_DIM
    rev = lambda i: nc - 1 - i

    def body(y_ref, z_ref, small_ref, al_ref, dt_ref, nw_ref, ss_ref, do_ref,
             dy_ref, dz_ref, dsmall_ref, dal_ref, ddt_ref, dnw_ref, dstate):
        i = pl.program_id(0)
        n = nc - 1 - i

        @pl.when(i == 0)
        def _():
            dstate[...] = jnp.zeros_like(dstate)
            dal_ref[...] = jnp.zeros_like(dal_ref)
            ddt_ref[...] = jnp.zeros_like(ddt_ref)
            dnw_ref[...] = jnp.zeros_like(dnw_ref)

        rows = _chunk_rows(n)
        fn = lambda *a: _dn_chunk(*a, rows)
        _, vjp = jax.vjp(fn, y_ref[...], z_ref[...], small_ref[...], ss_ref[0], al_ref[...], dt_ref[...], nw_ref[...])
        dy, dz, dsmall, ds, dal, ddt, dnw = vjp((do_ref[...], dstate[...]))
        dy_ref[...] = dy
        dz_ref[...] = dz
        dsmall_ref[...] = dsmall
        dstate[...] = ds
        dal_ref[...] += dal
        ddt_ref[...] += ddt
        dnw_ref[...] += dnw

    return pl.pallas_call(
        body, name=name, grid=(nc,),
        in_specs=[pl.BlockSpec((CHUNK, y.shape[1]), lambda i: (rev(i), 0)),
                  pl.BlockSpec((CHUNK, 4 * LANE), lambda i: (rev(i), Z_COL)),
                  pl.BlockSpec((CHUNK, LANE), lambda i: (rev(i), SMALL_COL)),
                  _full((1, LANE)), _full((1, LANE)), _full((1, LANE)),
                  pl.BlockSpec((1, DN_HEADS, dk, dk), lambda i: (rev(i), 0, 0, 0)),
                  pl.BlockSpec((CHUNK, 4 * LANE), lambda i: (rev(i), 1))],
        out_specs=(pl.BlockSpec((CHUNK, y.shape[1]), lambda i: (rev(i), 0)),
                   pl.BlockSpec((CHUNK, 4 * LANE), lambda i: (rev(i), 0)),
                   pl.BlockSpec((CHUNK, LANE), lambda i: (rev(i), 0)),
                   _full((1, LANE)), _full((1, LANE)), _full((1, LANE))),
        out_shape=(jax.ShapeDtypeStruct((tp, y.shape[1]), F32), jax.ShapeDtypeStruct((tp, 4 * LANE), F32),
                   jax.ShapeDtypeStruct((tp, LANE), F32), jax.ShapeDtypeStruct((1, LANE), F32),
                   jax.ShapeDtypeStruct((1, LANE), F32), jax.ShapeDtypeStruct((1, LANE), F32)),
        scratch_shapes=[pltpu.VMEM((DN_HEADS, dk, dk), F32)],
    )(y, proj, proj, a_log, dt_bias, norm_w, ssave, do)


def _gla_chunk(q, k, v, gate, low, s, w_gate_up, b_gate, norm_w, rows):
    tri_incl, _, _ = _chunk_masks()
    dk, dv, nh = GLA_DK, GLA_DV, GLA_HEADS
    heads = lambda t, width: jnp.stack([t[:, h * width:(h + 1) * width] for h in range(nh)])
    logit = _mm3(low, w_gate_up) + b_gate
    glog_all = -_softplus(-logit) * (1.0 / GLA_GATE_NORM) * rows
    glog = heads(glog_all, dk)
    bcum = heads(_cumsum_rows(glog_all), dk)
    qh = heads(q, dk) * dk ** -0.5
    kh = heads(k, dk)
    vh = heads(v, dv)
    q_dec = qh * jnp.exp(bcum)
    attn = _bmm_nt(q_dec, kh * jnp.exp(-bcum)) * tri_incl
    b_last = jnp.sum(glog, axis=1, keepdims=True)
    k_dec = kh * jnp.exp(b_last - bcum)
    r = lax.broadcasted_iota(jnp.int32, (dk, dk), 0)
    c = lax.broadcasted_iota(jnp.int32, (dk, dk), 1)
    b_last_col = jnp.sum((r == c).astype(F32) * b_last, axis=2, keepdims=True)
    o = _bmm(attn, vh) + _bmm(q_dec, s)
    s_new = s * jnp.exp(b_last_col) + _bmm_tn(k_dec, vh)
    out = _rms(o, norm_w) * _silu(heads(gate, dv))
    return jnp.concatenate([out[h] for h in range(nh)], axis=1), s_new


LOW_COL = 24


def _gla_in_specs(step):
    return [pl.BlockSpec((CHUNK, 4 * LANE), lambda i: (step(i), 0)),
            pl.BlockSpec((CHUNK, 4 * LANE), lambda i: (step(i), 1)),
            pl.BlockSpec((CHUNK, 8 * LANE), lambda i: (step(i), 1)),
            pl.BlockSpec((CHUNK, 8 * LANE), lambda i: (step(i), 2)),
            pl.BlockSpec((CHUNK, LANE), lambda i: (step(i), LOW_COL)),
            _full((LANE, 4 * LANE)), _full((1, 4 * LANE)), _full((1, GLA_DV))]


def gla_fwd(proj, w_gate_up, b_gate, norm_w, *, name):
    tp = proj.shape[0]
    nc = tp // CHUNK

    def body(q_ref, k_ref, v_ref, g_ref, low_ref, wgu_ref, bg_ref, nw_ref, o_ref, ssave_ref, state):
        n = pl.program_id(0)

        @pl.when(n == 0)
        def _():
            state[...] = jnp.zeros_like(state)

        ssave_ref[0] = state[...]
        out, s_new = _gla_chunk(q_ref[...], k_ref[...], v_ref[...], g_ref[...], low_ref[...], state[...], wgu_ref[...],
                                bg_ref[...], nw_ref[...], _chunk_rows(n))
        o_ref[...] = out
        state[...] = s_new

    return pl.pallas_call(
        body, name=name, grid=(nc,),
        in_specs=_gla_in_specs(lambda i: i),
        out_specs=(pl.BlockSpec((CHUNK, 8 * LANE), lambda n: (n, 0)),
                   pl.BlockSpec((1, GLA_HEADS, GLA_DK, GLA_DV), lambda n: (n, 0, 0, 0))),
        out_shape=(jax.ShapeDtypeStruct((tp, 8 * LANE), F32),
                   jax.ShapeDtypeStruct((nc, GLA_HEADS, GLA_DK, GLA_DV), F32)),
        scratch_shapes=[pltpu.VMEM((GLA_HEADS, GLA_DK, GLA_DV), F32)],
    )(proj, proj, proj, proj, proj, w_gate_up, b_gate, norm_w)


def gla_bwd(proj, w_gate_up, b_gate, norm_w, ssave, do, *, name):
    tp = proj.shape[0]
    nc = tp // CHUNK
    rev = lambda i: nc - 1 - i

    def body(q_ref, k_ref, v_ref, g_ref, low_ref, wgu_ref, bg_ref, nw_ref, ss_ref, do_ref,
             dq_ref, dk_ref, dv_ref, dg_ref, dlow_ref, dwgu_ref, dbg_ref, dnw_ref, dstate):
        i = pl.program_id(0)
        n = nc - 1 - i

        @pl.when(i == 0)
        def _():
            dstate[...] = jnp.zeros_like(dstate)
            dwgu_ref[...] = jnp.zeros_like(dwgu_ref)
            dbg_ref[...] = jnp.zeros_like(dbg_ref)
            dnw_ref[...] = jnp.zeros_like(dnw_ref)

        rows = _chunk_rows(n)
        fn = lambda *a: _gla_chunk(*a, rows)
        _, vjp = jax.vjp(fn, q_ref[...], k_ref[...], v_ref[...], g_ref[...], low_ref[...], ss_ref[0], wgu_ref[...],
                         bg_ref[...], nw_ref[...])
        dq, dk, dv, dg, dlow, ds, dwgu, dbg, dnw = vjp((do_ref[...], dstate[...]))
        dq_ref[...] = dq
        dk_ref[...] = dk
        dv_ref[...] = dv
        dg_ref[...] = dg
        dlow_ref[...] = dlow
        dstate[...] = ds
        dwgu_ref[...] += dwgu
        dbg_ref[...] += dbg
        dnw_ref[...] += dnw

    chunk = lambda width: pl.BlockSpec((CHUNK, width), lambda i: (rev(i), 0))
    return pl.pallas_call(
        body, name=name, grid=(nc,),
        in_specs=_gla_in_specs(rev) + [pl.BlockSpec((1, GLA_HEADS, GLA_DK, GLA_DV), lambda i: (rev(i), 0, 0, 0)),
                                       chunk(8 * LANE)],
        out_specs=(chunk(4 * LANE), chunk(4 * LANE), chunk(8 * LANE), chunk(8 * LANE), chunk(LANE),
                   _full((LANE, 4 * LANE)), _full((1, 4 * LANE)), _full((1, GLA_DV))),
        out_shape=(jax.ShapeDtypeStruct((tp, 4 * LANE), F32), jax.ShapeDtypeStruct((tp, 4 * LANE), F32),
                   jax.ShapeDtypeStruct((tp, 8 * LANE), F32), jax.ShapeDtypeStruct((tp, 8 * LANE), F32),
                   jax.ShapeDtypeStruct((tp, LANE), F32), jax.ShapeDtypeStruct((LANE, 4 * LANE), F32),
                   jax.ShapeDtypeStruct((1, 4 * LANE), F32), jax.ShapeDtypeStruct((1, GLA_DV), F32)),
        scratch_shapes=[pltpu.VMEM((GLA_HEADS, GLA_DK, GLA_DV), F32)],
    )(proj, proj, proj, proj, proj, w_gate_up, b_gate, norm_w, ssave, do)


def _even_proj_weight(w_in):
    hd = SWA_HEAD_DIM
    k0, k1 = w_in[:, 512:512 + hd], w_in[:, 512 + hd:640]
    v0, v1 = w_in[:, 640:640 + hd], w_in[:, 640 + hd:768]
    zeros = jnp.zeros((w_in.shape[0], LANE - 2 * DN_HEADS), w_in.dtype)
    return jnp.concatenate([w_in[:, :512], k0, k0, k1, k1, v0, v0, v1, v1, w_in[:, 768:2816], w_in[:, 2820:2824],
                            w_in[:, 2816:2820], zeros], axis=1)


def _even_proj_weight_grad(dw):
    hd = SWA_HEAD_DIM
    c = lambda i: dw[:, 512 + i * hd:512 + (i + 1) * hd]
    return jnp.concatenate([dw[:, :512], c(0) + c(1), c(2) + c(3), c(4) + c(5), c(6) + c(7), dw[:, 1024:3072],
                            dw[:, 3076:3080], dw[:, 3072:3076]], axis=1)


def _ffn_fwd(h, nw_in, nw_out, w_gu, w_down, idx):
    hn, g, u, a = rms_mm(h, nw_in, w_gu, swiglu=True, name=f"ffn_up_{idx}", widx=idx)
    f, h_out = mm_rms_res([a], w_down, h, nw_out, scale=0.5, name=f"ffn_down_{idx}", widx=idx)
    return h_out, (h, hn, g, u, a, f)


def _ffn_bwd(dho, saved, nw_in, nw_out, w_gu, w_down, idx, g_gu, g_down):
    h, hn, g, u, a, f = saved
    df, dnw_out, dgu = mm_rms_res_bwd(dho, f, nw_out, w_down, (g, u), scale=0.5, name=f"ffn_down_bwd_{idx}", widx=idx)
    dh, dnw_in = rms_mm_bwd([dgu], w_gu, h, nw_in, dho, name=f"ffn_up_bwd_{idx}", widx=idx)
    g_gu = mm_tn(hn, dgu, name=f"ffn_dwgu_{idx}", into=g_gu, slab=idx)
    g_down = mm_tn(a, df, name=f"ffn_dwd_{idx}", into=g_down, slab=idx)
    return dh, dnw_in, dnw_out, g_gu, g_down


def local_step(x, target, wts):
    seq, d = x.shape
    row = lambda v: v.reshape(1, -1)
    lane_row = lambda v: jnp.pad(v.reshape(1, -1), ((0, 0), (0, LANE - v.size)))
    nw = wts["norm_w"]
    h = jnp.concatenate([jnp.zeros((PAD, d), F32), wts["meta_tokens"], x], axis=0)
    w_gu, w_down = wts["w_gu"], wts["w_down"]
    w_in = [_even_proj_weight(wts["even_w_in"]),
            jnp.pad(wts["odd_w_in"], ((0, 0), (0, PROJ_DIM - wts["odd_w_in"].shape[1])))]
    w_out = [wts["even_w_out"], wts["odd_w_out"]]
    buckets = _swa_buckets()
    bias = swa_bias(wts["rel_bias_table"], buckets, name="swa_bias")
    sinks = lane_row(wts["swa_sinks"])
    a_log, dt_bias = lane_row(wts["dn_a_log"]), lane_row(wts["dn_dt_bias"])
    dn_norm_w = row(wts["dn_norm_w"])
    conv_w = wts["even_conv_w"][0]
    w_gate_up = jnp.pad(wts["gla_w_gate_up"][0], ((0, LANE - GLA_GATE_RANK), (0, 0)))
    b_gate, gla_norm_w = row(wts["gla_b_gate"]), row(wts["gla_norm_w"])

    saved = []
    for l in range(2):
        h, s_a = _ffn_fwd(h, row(nw[l, 0]), row(nw[l, 1]), w_gu, w_down, 2 * l)
        h_mix = h
        hn, proj = rms_mm(h, row(nw[l, 2]), w_in[l], swiglu=False, name=f"mix_in_{l}")
        if l == 0:
            o_a = swa_fwd(proj, bias, sinks, name="swa_fwd")
            y = conv_fwd(proj, conv_w, name="conv_fwd")
            o_b, ssave = dn_fwd(y, proj, a_log, dt_bias, dn_norm_w, name="dn_fwd")
            acts, extra = [o_a, o_b], (y, ssave)
        else:
            o, ssave = gla_fwd(proj, w_gate_up, b_gate, gla_norm_w, name="gla_fwd")
            acts, extra = [o], (ssave,)
        mix, h = mm_rms_res(acts, w_out[l], h, row(nw[l, 3]), scale=1.0, name=f"mix_out_{l}")
        s_m = (h_mix, hn, proj, acts, extra, mix)
        h, s_b = _ffn_fwd(h, row(nw[l, 4]), row(nw[l, 5]), w_gu, w_down, 2 * l + 1)
        saved.append((s_a, s_m, s_b))

    dh, loss = loss_and_grad(h, target, name="loss")

    grads = {}
    dnw = [[None] * 6 for _ in range(2)]
    g_gu = lax.empty(w_gu.shape, BF16)
    g_down = lax.empty(w_down.shape, BF16)
    for l in (1, 0):
        s_a, s_m, s_b = saved[l]
        dh, dnw[l][4], dnw[l][5], g_gu, g_down = _ffn_bwd(
            dh, s_b, row(nw[l, 4]), row(nw[l, 5]), w_gu, w_down, 2 * l + 1, g_gu, g_down)
        h_mix, hn, proj, acts, extra, mix = s_m
        dmix, dnw[l][3], do = mm_rms_res_bwd(dh, mix, row(nw[l, 3]), w_out[l], None, scale=1.0, name=f"mix_out_bwd_{l}")
        dw_out = jnp.concatenate([mm_tn(a, dmix, name=f"mix_dwo_{l}_{i}") for i, a in enumerate(acts)], axis=0)
        if l == 0:
            y, ssave = extra
            dq, dkv, dbias, dsinks = swa_bwd(proj, bias, sinks, do, name="swa_bwd")
            dy, dz, dsmall, da_log, ddt_bias, ddn_norm_w = dn_bwd(y, proj, a_log, dt_bias, dn_norm_w, ssave, do,
                                                                    name="dn_bwd")
            dxc, dconv_w = conv_bwd(proj, conv_w, dy, name="conv_bwd")
            dps = [dq, dkv, dxc, dz, dsmall]
            grads["rel_bias_table"] = swa_bias_bwd(dbias, buckets, name="swa_bias_bwd")[:, :SWA_Q_HEADS]
            grads["swa_sinks"] = dsinks[:, :SWA_Q_HEADS]
            grads["dn_a_log"] = da_log[:, :DN_HEADS]
            grads["dn_dt_bias"] = ddt_bias[:, :DN_HEADS]
            grads["dn_norm_w"] = ddn_norm_w
            grads["even_conv_w"] = dconv_w[None]
            grads["even_w_out"] = dw_out
        else:
            (ssave,) = extra
            dq, dk, dv, dgate, dlow, dwgu, dbg, dgnw = gla_bwd(proj, w_gate_up, b_gate, gla_norm_w, ssave, do,
                                                               name="gla_bwd")
            dps = [dq, dk, dv, dgate, dlow]
            grads["gla_w_gate_up"] = dwgu[None, :GLA_GATE_RANK]
            grads["gla_b_gate"] = dbg
            grads["gla_norm_w"] = dgnw
            grads["odd_w_out"] = dw_out
        dw_in = jnp.concatenate([mm_tn(hn, dp, name=f"mix_dwi_{l}_{i}") for i, dp in enumerate(dps)], axis=1)
        if l == 0:
            grads["even_w_in"] = _even_proj_weight_grad(dw_in)
        else:
            grads["odd_w_in"] = dw_in[:, :wts["odd_w_in"].shape[1]]
        dh, dnw[l][2] = rms_mm_bwd(dps, w_in[l], h_mix, row(nw[l, 2]), dh, name=f"mix_in_bwd_{l}")
        dh, dnw[l][0], dnw[l][1], g_gu, g_down = _ffn_bwd(
            dh, s_a, row(nw[l, 0]), row(nw[l, 1]), w_gu, w_down, 2 * l, g_gu, g_down)

    grads["g_gu"], grads["g_down"] = g_gu, g_down
    grads["norm_w"] = jnp.stack([jnp.concatenate(r, axis=0) for r in dnw])
    grads["meta_tokens"] = dh[PAD:PAD + N_META]
    return loss[0, 0], dh[PAD + N_META:], grads


def _peer(k):
    x, y, c = (lax.axis_index(a) for a in AXES)
    flip = lambda v, bit: 1 - v if bit else v
    return (flip(x, k & 4), flip(y, k & 2), flip(c, k & 1))


def _my_index():
    x, y, c = (lax.axis_index(a) for a in AXES)
    return 4 * x + 2 * y + c


def exchange(srcs, outs, items, *, name):
    ns, no, ni = len(srcs), len(outs), len(items)

    def body(*refs):
        src_refs, out_refs = refs[:ns], refs[ns:ns + no]
        send_sems, recv_sems, local_sems = refs[ns + no:]
        me = _my_index()
        local = [pltpu.make_async_copy(send(src_refs[si], me), land(out_refs[oi], me), local_sems.at[a])
                 for a, (si, send, oi, land) in enumerate(items)]
        for cp in local:
            cp.start()
        copies = []
        for k in range(1, N_DEV):
            px, py, pc = _peer(k)
            pj = 4 * px + 2 * py + pc
            for a, (si, send, oi, land) in enumerate(items):
                sem = (k - 1) * ni + a
                cp = pltpu.make_async_remote_copy(src_ref=send(src_refs[si], pj), dst_ref=land(out_refs[oi], me),
                                                  send_sem=send_sems.at[sem], recv_sem=recv_sems.at[sem],
                                                  device_id=(px, py, pc), device_id_type=MESH)
                cp.start()
                copies.append(cp)
        for cp in copies:
            cp.wait_recv()
        for cp in copies:
            cp.wait_send()
        for cp in local:
            cp.wait()

    n_remote = (N_DEV - 1) * ni
    return pl.pallas_call(
        body, name=name,
        in_specs=[pl.BlockSpec(memory_space=pl.ANY)] * ns,
        out_specs=tuple(pl.BlockSpec(memory_space=pl.ANY) for _ in outs),
        out_shape=tuple(outs),
        scratch_shapes=[pltpu.SemaphoreType.DMA((n_remote,)), pltpu.SemaphoreType.DMA((n_remote,)),
                        pltpu.SemaphoreType.DMA((ni,))],
    )(*srcs)


def _block(index, size):
    return pl.ds(pl.multiple_of(index * size, LANE), size)


def _adam_tile(rows):
    for t in (256, 176, 128):
        if rows % t == 0:
            return t
    return rows


def sum_adamw(recv, col0, w, m, v, *, name):
    b, r, c = w.shape
    cp = recv.shape[-1]
    tr = _adam_tile(r)
    c1 = 1.0 / (1.0 - ADAM_B1 ** ADAM_STEP)
    c2 = 1.0 / (1.0 - ADAM_B2 ** ADAM_STEP)

    def body(recv_ref, w_ref, m_ref, v_ref, g_ref, d_ref, nm_ref, nv_ref):
        g = recv_ref[0, 0, :, col0:col0 + c].astype(F32)
        for i in range(1, N_DEV):
            g = g + recv_ref[i, 0, :, col0:col0 + c].astype(F32)
        nm = ADAM_B1 * m_ref[0] + (1.0 - ADAM_B1) * g
        nv = ADAM_B2 * v_ref[0] + (1.0 - ADAM_B2) * (g * g)
        g_ref[0] = g
        nm_ref[0] = nm
        nv_ref[0] = nv
        d_ref[0] = -ADAM_LR * ((nm * c1) / (jnp.sqrt(nv * c2) + ADAM_EPS) + ADAM_WD * w_ref[0])

    tile = pl.BlockSpec((1, tr, c), lambda bi, i: (bi, i, 0))
    return pl.pallas_call(
        body, name=name, grid=(b, r // tr),
        in_specs=[pl.BlockSpec((N_DEV, 1, tr, cp), lambda bi, i: (0, bi, i, 0)), tile, tile, tile],
        out_specs=(tile,) * 4, out_shape=(jax.ShapeDtypeStruct((b, r, c), F32),) * 4,
    )(recv, w, m, v)


def _flat_rows(n_elems, row_multiple):
    rows = -(-n_elems // FLAT_COLS)
    return -(-rows // row_multiple) * row_multiple


def _pack(arrays, row_multiple, dtype):
    flat = jnp.concatenate([a.reshape(-1).astype(dtype) for a in arrays])
    rows = _flat_rows(flat.size, row_multiple)
    return jnp.pad(flat, (0, rows * FLAT_COLS - flat.size)).reshape(rows, FLAT_COLS)


def _unpack(flat2d, shapes):
    lead = flat2d.shape[:-2]
    flat = flat2d.reshape(lead + (-1,))
    out, off = [], 0
    for shp in shapes:
        n = int(np.prod(shp))
        out.append(flat[..., off:off + n].reshape(lead + tuple(shp)))
        off += n
    return out


def _join_shards(stacked, axis):
    moved = jnp.moveaxis(stacked, 0, axis)
    shp = list(moved.shape)
    shp[axis:axis + 2] = [shp[axis] * shp[axis + 1]]
    return moved.reshape(shp)


def _split_shards(full, axis):
    shp = list(full.shape)
    shp[axis:axis + 1] = [N_DEV, shp[axis] // N_DEV]
    return jnp.moveaxis(full.reshape(shp), axis, 0)


def kernel(x, meta_tokens, norm_w, ffn_w_gate, ffn_w_up, ffn_w_down, rel_bias_table, even_w_in, even_conv_w, swa_sinks, dn_a_log, dn_dt_bias, dn_norm_w, even_w_out, odd_w_in, gla_w_gate_up, gla_b_gate, gla_norm_w, odd_w_out, loss_target, m_meta_tokens, m_norm_w, m_ffn_w_gate, m_ffn_w_up, m_ffn_w_down, m_rel_bias_table, m_even_w_in, m_even_conv_w, m_swa_sinks, m_dn_a_log, m_dn_dt_bias, m_dn_norm_w, m_even_w_out, m_odd_w_in, m_gla_w_gate_up, m_gla_b_gate, m_gla_norm_w, m_odd_w_out, v_meta_tokens, v_norm_w, v_ffn_w_gate, v_ffn_w_up, v_ffn_w_down, v_rel_bias_table, v_even_w_in, v_even_conv_w, v_swa_sinks, v_dn_a_log, v_dn_dt_bias, v_dn_norm_w, v_even_w_out, v_odd_w_in, v_gla_w_gate_up, v_gla_b_gate, v_gla_norm_w, v_odd_w_out):
    args = locals()
    w = {n: args[n] for n in WEIGHTS}
    m = {n: args["m_" + n] for n in WEIGHTS}
    v = {n: args["v_" + n] for n in WEIGHTS}

    d = D_MODEL
    sds = jax.ShapeDtypeStruct
    whole = lambda ref, j: ref
    cols = lambda size, base=0: (lambda ref, i: ref.at[(slice(None),) * (len(ref.shape) - 1)
                                                       + (pl.ds(pl.multiple_of(base + i * size, LANE), size),)])
    rows3 = lambda size: (lambda ref, i: ref.at[:, _block(i, size), :])
    rows2 = lambda size: (lambda ref, i: ref.at[_block(i, size), :])
    lead = lambda ref, i: ref.at[i]

    pad_cols = lambda a, to: jnp.pad(a, [(0, 0)] * (a.ndim - 1) + [(0, to - a.shape[-1])])
    gate_s = pad_cols(w["ffn_w_gate"].reshape(N_FFN, d, FF_SHARD), FF_SHARD_PAD).astype(BF16)
    up_s = pad_cols(w["ffn_w_up"].reshape(N_FFN, d, FF_SHARD), FF_SHARD_PAD).astype(BF16)
    down_s = jnp.pad(w["ffn_w_down"].reshape(N_FFN, FF_SHARD, d),
                     ((0, 0), (0, FF_SHARD_PAD - FF_SHARD), (0, 0))).astype(BF16)
    ein_s = pad_cols(w["even_w_in"][0], EVEN_IN_SHARD_PAD).astype(BF16)
    oin_s = pad_cols(w["odd_w_in"][0], ODD_IN_SHARD_PAD).astype(BF16)
    small_s = _pack([w[n] for n in SMALL], 8, F32)
    w_gu, w_down, ein_p, oin_p, eout, oout, small_all = exchange(
        [gate_s, up_s, down_s, ein_s, oin_s, w["even_w_out"][0].astype(BF16), w["odd_w_out"][0].astype(BF16), small_s],
        [sds((N_FFN, d, 2 * FF_PAD), BF16), sds((N_FFN, FF_PAD, d), BF16), sds((d, N_DEV * EVEN_IN_SHARD_PAD), BF16),
         sds((d, N_DEV * ODD_IN_SHARD_PAD), BF16), sds((d, d), BF16), sds((d, d), BF16),
         sds((N_DEV,) + small_s.shape, F32)],
        [(0, whole, 0, cols(FF_SHARD_PAD)), (1, whole, 0, cols(FF_SHARD_PAD, FF_PAD)), (2, whole, 1, rows3(FF_SHARD_PAD)),
         (3, whole, 2, cols(EVEN_IN_SHARD_PAD)), (4, whole, 3, cols(ODD_IN_SHARD_PAD)), (5, whole, 4, rows2(OUT_SHARD)),
         (6, whole, 5, rows2(OUT_SHARD)), (7, whole, 6, lead)],
        name="gather_weights")
    unpad = lambda p, shard, shard_pad: p.reshape(d, N_DEV, shard_pad)[:, :, :shard].reshape(d, N_DEV * shard)
    full = {n: w[n] for n in REPL}
    for n, stacked in zip(SMALL, _unpack(small_all, [w[n].shape for n in SMALL])):
        full[n] = _join_shards(stacked, SHARD_AXIS[n])
    full.update(w_gu=w_gu, w_down=w_down, even_w_out=eout, odd_w_out=oout,
                even_w_in=unpad(ein_p, EVEN_IN_SHARD, EVEN_IN_SHARD_PAD),
                odd_w_in=unpad(oin_p, ODD_IN_SHARD, ODD_IN_SHARD_PAD))

    loss, grad_x, grads = local_step(x[0], loss_target[0], full)
    loss = lax.psum(loss, AXES)

    repad = lambda g, shard, shard_pad: pad_cols(g.reshape(d, N_DEV, shard), shard_pad).reshape(d, N_DEV * shard_pad)
    order = SMALL + REPL
    pieces = [_split_shards(grads[n].reshape(full[n].shape), SHARD_AXIS[n]) if n in SHARD_AXIS
              else jnp.broadcast_to(grads[n].reshape(w[n].shape)[None], (N_DEV,) + w[n].shape) for n in order]
    flat = jnp.concatenate([p.reshape(N_DEV, -1) for p in pieces], axis=1)
    srows = _flat_rows(flat.shape[1], 8)
    flat = jnp.pad(flat, ((0, 0), (0, srows * FLAT_COLS - flat.shape[1]))).reshape(N_DEV, srows, FLAT_COLS)
    r_gu, r_down, r_ein, r_oin, r_eout, r_oout, r_small = exchange(
        [grads["g_gu"], grads["g_down"], repad(grads["even_w_in"], EVEN_IN_SHARD, EVEN_IN_SHARD_PAD).astype(BF16),
         repad(grads["odd_w_in"], ODD_IN_SHARD, ODD_IN_SHARD_PAD).astype(BF16), grads["even_w_out"].astype(BF16),
         grads["odd_w_out"].astype(BF16), flat],
        [sds((N_DEV, N_FFN, d, 2 * FF_SHARD_PAD), BF16), sds((N_DEV, N_FFN, FF_SHARD_PAD, d), BF16),
         sds((N_DEV, 1, d, EVEN_IN_SHARD_PAD), BF16), sds((N_DEV, 1, d, ODD_IN_SHARD_PAD), BF16),
         sds((N_DEV, 1, OUT_SHARD, d), BF16), sds((N_DEV, 1, OUT_SHARD, d), BF16), sds((N_DEV, 1, srows, FLAT_COLS), F32)],
        [(0, cols(FF_SHARD_PAD), 0, lambda ref, i: ref.at[i, :, :, pl.ds(0, FF_SHARD_PAD)]),
         (0, cols(FF_SHARD_PAD, FF_PAD), 0, lambda ref, i: ref.at[i, :, :, pl.ds(FF_SHARD_PAD, FF_SHARD_PAD)]),
         (1, rows3(FF_SHARD_PAD), 1, lead), (2, cols(EVEN_IN_SHARD_PAD), 2, lambda ref, i: ref.at[i, 0]),
         (3, cols(ODD_IN_SHARD_PAD), 3, lambda ref, i: ref.at[i, 0]), (4, rows2(OUT_SHARD), 4, lambda ref, i: ref.at[i, 0]),
         (5, rows2(OUT_SHARD), 5, lambda ref, i: ref.at[i, 0]), (6, lead, 6, lambda ref, i: ref.at[i, 0])],
        name="exchange_grads")

    result = [{} for _ in range(4)]

    def update(names, recv, col0, view, back):
        for n in names:
            outs = sum_adamw(recv, col0, view(w[n]), view(m[n]), view(v[n]), name=f"adamw_{n}")
            for r, o in zip(result, outs):
                r[n] = back(o, n)

    as_given = lambda o, n: o.reshape(w[n].shape)
    update(["ffn_w_gate"], r_gu, 0, lambda a: a.reshape(N_FFN, d, FF_SHARD), as_given)
    update(["ffn_w_up"], r_gu, FF_SHARD_PAD, lambda a: a.reshape(N_FFN, d, FF_SHARD), as_given)
    update(["ffn_w_down"], r_down, 0, lambda a: a.reshape(N_FFN, FF_SHARD, d), as_given)
    update(["even_w_in"], r_ein, 0, lambda a: a, as_given)
    update(["odd_w_in"], r_oin, 0, lambda a: a, as_given)
    update(["even_w_out"], r_eout, 0, lambda a: a, as_given)
    update(["odd_w_out"], r_oout, 0, lambda a: a, as_given)
    pack_local = lambda t: _pack([t[n] for n in order], 8, F32)[None]
    small_outs = sum_adamw(r_small, 0, pack_local(w), pack_local(m), pack_local(v), name="adamw_small")
    for r, o in zip(result, small_outs):
        r.update(zip(order, _unpack(o[0], [w[n].shape for n in order])))
    return (loss, grad_x[None], *[r[n] for r in result for n in WEIGHTS])
```

```python
import functools
import math

import numpy as np
import jax
import jax.numpy as jnp
from jax import lax
from jax.experimental import pallas as pl
from jax.experimental.pallas import tpu as pltpu

F32 = jnp.float32
BF16 = jnp.bfloat16
MESH = pl.DeviceIdType.MESH
AXES = ("x", "y", "c")
N_DEV = 8

D_MODEL = 1024
N_META = 16
D_FF = 2816
NORM_EPS = 1e-6
NEG_INF = -1e30
SWA_Q_HEADS = 8
SWA_HEAD_DIM = 64
SWA_WINDOW = 128
SWA_BLOCK = 128
REL_BUCKETS = 32
REL_MAX_DIST = 128
DN_HEADS = 4
DN_HEAD_DIM = 128
DN_CONV = 4
GLA_HEADS = 4
GLA_DK = 128
GLA_DV = 256
GLA_GATE_RANK = 16
GLA_GATE_NORM = 16.0
CHUNK = 64
PAD = SWA_BLOCK - N_META
LANE = 128
PROJ_DIM = 3200

ADAM_LR = 0.001
ADAM_B1 = 0.9
ADAM_B2 = 0.999
ADAM_EPS = 1e-08
ADAM_WD = 0.01
ADAM_STEP = 10

FF_SHARD = D_FF // N_DEV
FF_SHARD_PAD = 384
FF_PAD = N_DEV * FF_SHARD_PAD
N_FFN = 4
EVEN_IN_SHARD, EVEN_IN_SHARD_PAD = 353, 384
ODD_IN_SHARD, ODD_IN_SHARD_PAD = 386, 512
OUT_SHARD = D_MODEL // N_DEV

FLAT_COLS = 128
BIG = ("ffn_w_gate", "ffn_w_up", "ffn_w_down", "even_w_in", "even_w_out", "odd_w_in", "odd_w_out")
SMALL = ("meta_tokens", "norm_w", "even_conv_w", "gla_w_gate_up", "gla_b_gate", "gla_norm_w")
REPL = ("rel_bias_table", "swa_sinks", "dn_a_log", "dn_dt_bias", "dn_norm_w")
WEIGHTS = ("meta_tokens", "norm_w", "ffn_w_gate", "ffn_w_up", "ffn_w_down", "rel_bias_table", "even_w_in",
           "even_conv_w", "swa_sinks", "dn_a_log", "dn_dt_bias", "dn_norm_w", "even_w_out", "odd_w_in",
           "gla_w_gate_up", "gla_b_gate", "gla_norm_w", "odd_w_out")
SHARD_AXIS = {"ffn_w_gate": 3, "ffn_w_up": 3, "ffn_w_down": 2, "even_w_in": 2, "even_w_out": 1, "odd_w_in": 2,
              "odd_w_out": 1, "meta_tokens": 1, "norm_w": 2, "even_conv_w": 2, "gla_w_gate_up": 2,
              "gla_b_gate": 1, "gla_norm_w": 1}


def _rms(x, w):
    r = lax.rsqrt(jnp.mean(x * x, axis=-1, keepdims=True) + NORM_EPS)
    return x * r * w


def _sigmoid(x):
    return 0.5 * (jnp.tanh(0.5 * x) + 1.0)


def _silu(x):
    return x * _sigmoid(x)


def _softplus(x):
    pos = x > 0
    return jnp.where(pos, x, 0.0) + jnp.log(1.0 + jnp.exp(jnp.where(pos, -x, x)))


def _l2n(x):
    return x * lax.rsqrt(jnp.sum(x * x, axis=-1, keepdims=True) + 1e-6)


def _split_bf16(x):
    hi = x.astype(BF16)
    return hi, (x - hi.astype(F32)).astype(BF16)


def _make_mm(terms, batched):
    off = 1 if batched else 0
    bdims = ((0,), (0,)) if batched else ((), ())

    def dg(a, b, ca, cb):
        dot = lambda p, q: lax.dot_general(p, q, (((ca + off,), (cb + off,)), bdims), preferred_element_type=F32)
        a_hi, a_lo = _split_bf16(a)
        b_hi, b_lo = _split_bf16(b)
        if terms == 1:
            return dot(a_hi, b_hi)
        return dot(a_hi, b_hi) + (dot(a_hi, b_lo) + dot(a_lo, b_hi))

    @jax.custom_vjp
    def nn(a, b):
        return dg(a, b, 1, 0)

    @jax.custom_vjp
    def nt(a, b):
        return dg(a, b, 1, 1)

    @jax.custom_vjp
    def tn(a, b):
        return dg(a, b, 0, 0)

    nn.defvjp(lambda a, b: (nn(a, b), (a, b)), lambda r, g: (nt(g, r[1]), tn(r[0], g)))
    nt.defvjp(lambda a, b: (nt(a, b), (a, b)), lambda r, g: (nn(g, r[1]), tn(g, r[0])))
    tn.defvjp(lambda a, b: (tn(a, b), (a, b)), lambda r, g: (nt(r[1], g), nn(r[0], g)))
    return nn, nt, tn


_mm, _mm_nt, _mm_tn = _make_mm(1, False)
_mm3, _, _ = _make_mm(3, False)
_bmm, _bmm_nt, _bmm_tn = _make_mm(1, True)
_bmm3, _, _ = _make_mm(3, True)


def _tri_ones_dot(x, lower):
    n = x.shape[0]
    r = lax.broadcasted_iota(jnp.int32, (n, n), 0)
    c = lax.broadcasted_iota(jnp.int32, (n, n), 1)
    t = ((r >= c) if lower else (r <= c)).astype(BF16)
    hi, lo = _split_bf16(x)
    return jnp.dot(t, hi, preferred_element_type=F32) + jnp.dot(t, lo, preferred_element_type=F32)


@jax.custom_vjp
def _cumsum_rows(x):
    return _tri_ones_dot(x, True)


_cumsum_rows.defvjp(lambda x: (_tri_ones_dot(x, True), None), lambda _, g: (_tri_ones_dot(g, False),))


def _row_tile(n_rows, cap):
    best = LANE
    for t in range(LANE, cap + 1, LANE):
        if n_rows % t == 0:
            best = t
    return best


def _real_rows(tile_index, tm):
    row = tile_index * tm + lax.broadcasted_iota(jnp.int32, (tm, 1), 0)
    return (row >= PAD).astype(F32)


def _full(shape):
    return pl.BlockSpec(shape, lambda *_: (0,) * len(shape))


def _resident(shape):
    return pl.BlockSpec(shape, lambda *_: (0,) * len(shape), pipeline_mode=pl.Buffered(1))


def _resident_w(wmat, widx):
    if wmat.ndim == 2:
        return _resident(wmat.shape)
    return pl.BlockSpec((None,) + wmat.shape[1:], lambda *_: (widx, 0, 0), pipeline_mode=pl.Buffered(1))


def rms_mm(h, w, wmat, *, swiglu, name, widx=None):
    tp, d = h.shape
    n = wmat.shape[-1]
    tm = _row_tile(tp, 384)
    half = n // 2

    def body(h_ref, w_ref, wm_ref, hn_ref, *outs):
        hn = _rms(h_ref[...], w_ref[...]).astype(BF16)
        hn_ref[...] = hn
        p = jnp.dot(hn, wm_ref[...], preferred_element_type=F32)
        if swiglu:
            g, u = p[:, :half], p[:, half:]
            outs[0][...] = g.astype(BF16)
            outs[1][...] = u.astype(BF16)
            outs[2][...] = (_silu(g) * u).astype(BF16)
        else:
            outs[0][...] = p

    row = lambda width: pl.BlockSpec((tm, width), lambda i: (i, 0))
    if swiglu:
        out_shape = (jax.ShapeDtypeStruct((tp, d), BF16),) + (jax.ShapeDtypeStruct((tp, half), BF16),) * 3
        out_specs = (row(d), row(half), row(half), row(half))
    else:
        out_shape = (jax.ShapeDtypeStruct((tp, d), BF16), jax.ShapeDtypeStruct((tp, n), F32))
        out_specs = (row(d), row(n))
    return pl.pallas_call(
        body, name=name, grid=(tp // tm,),
        in_specs=[row(d), _full((1, d)), _resident_w(wmat, widx)],
        out_specs=out_specs, out_shape=out_shape,
    )(h, w, wmat)


def mm_rms_res(acts, wmat, h, w, *, scale, name, widx=None):
    tp, d = h.shape
    tm = _row_tile(tp, 384)
    widths = [a.shape[1] for a in acts]
    offs = [sum(widths[:i]) for i in range(len(acts))]
    na = len(acts)

    def body(*refs):
        a_refs = refs[:na]
        wm_ref, h_ref, w_ref, f_ref, ho_ref = refs[na:]
        f = None
        for a_ref, off, width in zip(a_refs, offs, widths):
            part = jnp.dot(a_ref[...].astype(BF16), wm_ref[off:off + width, :], preferred_element_type=F32)
            f = part if f is None else f + part
        f_ref[...] = f
        ho_ref[...] = h_ref[...] + scale * _rms(f, w_ref[...])

    row = lambda width: pl.BlockSpec((tm, width), lambda i: (i, 0))
    return pl.pallas_call(
        body, name=name, grid=(tp // tm,),
        in_specs=[row(wd) for wd in widths] + [_resident_w(wmat, widx), row(d), _full((1, d))],
        out_specs=(row(d), row(d)),
        out_shape=(jax.ShapeDtypeStruct((tp, d), F32), jax.ShapeDtypeStruct((tp, d), F32)),
    )(*acts, wmat, h, w)


def mm_rms_res_bwd(dho, f, w, wmat, gu, *, scale, name, widx=None):
    tp, d = f.shape
    k = wmat.shape[-2]
    tm = _row_tile(tp, 384)
    swiglu = gu is not None

    def body(*refs):
        if swiglu:
            dho_ref, f_ref, w_ref, wm_ref, g_ref, u_ref, df_ref, dw_ref, dgu_ref = refs
        else:
            dho_ref, f_ref, w_ref, wm_ref, df_ref, dw_ref, da_ref = refs
        i = pl.program_id(0)
        _, vjp = jax.vjp(lambda ff, ww: scale * _rms(ff, ww), f_ref[...], w_ref[...])
        df, dw = vjp(dho_ref[...])
        dfb = (df * _real_rows(i, tm)).astype(BF16)
        df_ref[...] = dfb

        @pl.when(i == 0)
        def _():
            dw_ref[...] = jnp.zeros_like(dw_ref)

        dw_ref[...] += dw
        da = lax.dot_general(dfb, wm_ref[...], (((1,), (1,)), ((), ())), preferred_element_type=F32)
        if swiglu:
            g = g_ref[...].astype(F32)
            u = u_ref[...].astype(F32)
            s = _sigmoid(g)
            dgu_ref[:, :k] = (da * u * s * (1.0 + g * (1.0 - s))).astype(BF16)
            dgu_ref[:, k:] = (da * g * s).astype(BF16)
        else:
            da_ref[...] = da

    row = lambda width: pl.BlockSpec((tm, width), lambda i: (i, 0))
    in_specs = [row(d), row(d), _full((1, d)), _resident_w(wmat, widx)]
    args = [dho, f, w, wmat]
    out_shape = [jax.ShapeDtypeStruct((tp, d), BF16), jax.ShapeDtypeStruct((1, d), F32)]
    out_specs = [row(d), _full((1, d))]
    if swiglu:
        in_specs += [row(k), row(k)]
        args += list(gu)
        out_shape += [jax.ShapeDtypeStruct((tp, 2 * k), BF16)]
        out_specs += [row(2 * k)]
    else:
        out_shape += [jax.ShapeDtypeStruct((tp, k), F32)]
        out_specs += [row(k)]
    return pl.pallas_call(body, name=name, grid=(tp // tm,), in_specs=in_specs, out_specs=tuple(out_specs),
                          out_shape=tuple(out_shape))(*args)


def rms_mm_bwd(dps, wmat, h, w, dho, *, name, widx=None):
    tp, d = h.shape
    tm = _row_tile(tp, 384)
    widths = [p.shape[1] for p in dps]
    offs = [sum(widths[:i]) for i in range(len(dps))]
    ndp = len(dps)

    def body(*refs):
        dp_refs = refs[:ndp]
        wm_ref, h_ref, w_ref, dho_ref, dh_ref, dw_ref = refs[ndp:]
        i = pl.program_id(0)
        dhn = None
        for dp_ref, off, width in zip(dp_refs, offs, widths):
            part = lax.dot_general(dp_ref[...].astype(BF16), wm_ref[:, off:off + width], (((1,), (1,)), ((), ())),
                                   preferred_element_type=F32)
            dhn = part if dhn is None else dhn + part
        _, vjp = jax.vjp(_rms, h_ref[...], w_ref[...])
        dx, dw = vjp(dhn)
        dh_ref[...] = (dho_ref[...] + dx) * _real_rows(i, tm)

        @pl.when(i == 0)
        def _():
            dw_ref[...] = jnp.zeros_like(dw_ref)

        dw_ref[...] += dw

    row = lambda width: pl.BlockSpec((tm, width), lambda i: (i, 0))
    return pl.pallas_call(
        body, name=name, grid=(tp // tm,),
        in_specs=[row(wd) for wd in widths] + [_resident_w(wmat, widx), row(d), _full((1, d)), row(d)],
        out_specs=(row(d), _full((1, d))),
        out_shape=(jax.ShapeDtypeStruct((tp, d), F32), jax.ShapeDtypeStruct((1, d), F32)),
    )(*dps, wmat, h, w, dho)


def mm_tn(a, b, *, name, out_dtype=F32):
    t, m = a.shape
    n = b.shape[1]
    bm = _row_tile(m, 512)
    bn = _row_tile(n, 1536)
    bk = _row_tile(t, 1408)
    nk = t // bk

    def body(a_ref, b_ref, o_ref, acc):
        @pl.when(pl.program_id(2) == 0)
        def _():
            acc[...] = jnp.zeros_like(acc)

        acc[...] += lax.dot_general(a_ref[...].astype(BF16), b_ref[...].astype(BF16), (((0,), (0,)), ((), ())),
                                    preferred_element_type=F32)

        @pl.when(pl.program_id(2) == nk - 1)
        def _():
            o_ref[...] = acc[...].astype(o_ref.dtype)

    return pl.pallas_call(
        body, name=name, grid=(m // bm, n // bn, nk),
        in_specs=[pl.BlockSpec((bk, bm), lambda i, j, kk: (kk, i)), pl.BlockSpec((bk, bn), lambda i, j, kk: (kk, j))],
        out_specs=pl.BlockSpec((bm, bn), lambda i, j, kk: (i, j)),
        out_shape=jax.ShapeDtypeStruct((m, n), out_dtype), scratch_shapes=[pltpu.VMEM((bm, bn), F32)],
    )(a, b)


def loss_and_grad(h, target, *, name):
    tp, d = h.shape
    tm = SWA_BLOCK

    def body(h_ref, t_ref, dh_ref, loss_ref):
        i = pl.program_id(0)

        @pl.when(i == 0)
        def _():
            loss_ref[...] = jnp.zeros_like(loss_ref)
            dh_ref[...] = jnp.zeros_like(dh_ref)

        @pl.when(i > 0)
        def _():
            err = h_ref[...] - t_ref[...]
            dh_ref[...] = err * (1.0 / d)
            loss_ref[...] += 0.5 * jnp.sum(jnp.sum(err * err, axis=1, keepdims=True), axis=0, keepdims=True) * (1.0 / d)

    return pl.pallas_call(
        body, name=name, grid=(tp // tm,),
        in_specs=[pl.BlockSpec((tm, d), lambda i: (i, 0)), pl.BlockSpec((tm, d), lambda i: (jnp.maximum(i - 1, 0), 0))],
        out_specs=(pl.BlockSpec((tm, d), lambda i: (i, 0)), _full((1, 1))),
        out_shape=(jax.ShapeDtypeStruct((tp, d), F32), jax.ShapeDtypeStruct((1, 1), F32)),
    )(h, target)


def _t5_bucket_np(rel):
    n = np.maximum(rel, 0)
    max_exact = REL_BUCKETS // 2
    n_f = np.maximum(n, 1).astype(np.float32)
    large = max_exact + (np.log(n_f / np.float32(max_exact)) / np.float32(math.log(REL_MAX_DIST / max_exact))
                         * np.float32(REL_BUCKETS - max_exact)).astype(np.int32)
    large = np.minimum(large, REL_BUCKETS - 1)
    return np.where(n < max_exact, n, large).astype(np.int32)


def _swa_positions_np(n):
    i = np.arange(SWA_BLOCK)[:, None]
    j = np.arange(3 * SWA_BLOCK)[None, :]
    pos_q = n * SWA_BLOCK + i - PAD
    pos_k = np.where(j < SWA_BLOCK, j - PAD, (n - 1) * SWA_BLOCK + (j - SWA_BLOCK) - PAD)
    return pos_q, pos_k


def _swa_buckets():
    out = []
    for n in range(3):
        pos_q, pos_k = _swa_positions_np(n)
        out.append(_t5_bucket_np(pos_q - pos_k))
    return jnp.asarray(np.stack(out))


def swa_bias(table, buckets, *, name):
    nc, nq, nk = buckets.shape

    def body(tab_ref, bkt_ref, out_ref):
        for c in range(nc):
            bkt = bkt_ref[c]
            for h in range(SWA_Q_HEADS):
                acc = jnp.zeros((nq, nk), F32)
                for b in range(REL_BUCKETS):
                    acc = jnp.where(bkt == b, tab_ref[b, h], acc)
                out_ref[c, h] = acc

    return pl.pallas_call(
        body, name=name,
        in_specs=[pl.BlockSpec(memory_space=pltpu.SMEM), pl.BlockSpec(memory_space=pltpu.VMEM)],
        out_specs=pl.BlockSpec(memory_space=pltpu.VMEM),
        out_shape=jax.ShapeDtypeStruct((nc, SWA_Q_HEADS, nq, nk), F32),
    )(table, buckets)


def swa_bias_bwd(dbias, buckets, *, name):
    nc = buckets.shape[0]

    def body(db_ref, bkt_ref, out_ref):
        lane = lax.broadcasted_iota(jnp.int32, (1, LANE), 1)
        for b in range(REL_BUCKETS):
            row = jnp.zeros((1, LANE), F32)
            for c in range(nc):
                hit = bkt_ref[c] == b
                for h in range(SWA_Q_HEADS):
                    part = jnp.where(hit, db_ref[c, h], 0.0)
                    tot = jnp.sum(jnp.sum(part, axis=1, keepdims=True), axis=0, keepdims=True)
                    row = row + jnp.where(lane == h, tot, 0.0)
            out_ref[b:b + 1, :] = row

    return pl.pallas_call(
        body, name=name,
        in_specs=[pl.BlockSpec(memory_space=pltpu.VMEM), pl.BlockSpec(memory_space=pltpu.VMEM)],
        out_specs=pl.BlockSpec(memory_space=pltpu.VMEM),
        out_shape=jax.ShapeDtypeStruct((REL_BUCKETS, LANE), F32),
    )(dbias, buckets)


def _swa_block(q, kvm, kvp, kvc, bias, sinks, n):
    blk = SWA_BLOCK
    i = lax.broadcasted_iota(jnp.int32, (blk, 3 * blk), 0)
    j = lax.broadcasted_iota(jnp.int32, (blk, 3 * blk), 1)
    pos_q = n * blk + i - PAD
    is_meta = j < blk
    pos_k = jnp.where(is_meta, j - PAD, (n - 1) * blk + (j - blk) - PAD)
    rel = pos_q - pos_k
    valid = ((is_meta & (pos_k >= 0) & (pos_k < N_META) & (rel >= 0))
             | (jnp.logical_not(is_meta) & (pos_k >= N_META) & (rel >= 0) & (rel < SWA_WINDOW)))
    kv = jnp.concatenate([kvm, kvp, kvc], axis=0)
    lane = lax.broadcasted_iota(jnp.int32, (1, LANE), 1)
    halves = ((lane < SWA_HEAD_DIM).astype(F32), (lane >= SWA_HEAD_DIM).astype(F32))
    scale = SWA_HEAD_DIM ** -0.5
    outs = []
    for pair in range(SWA_Q_HEADS // 2):
        qp = q[:, pair * LANE:(pair + 1) * LANE]
        grp = pair // 2
        kg = kv[:, grp * LANE:(grp + 1) * LANE]
        vg = kv[:, (2 + grp) * LANE:(3 + grp) * LANE]
        op = None
        for hh in range(2):
            h = 2 * pair + hh
            s = _mm_nt(qp * halves[hh], kg) * scale + bias[h]
            s = jnp.where(valid, s, NEG_INF)
            sink = jnp.sum(jnp.where(lane == h, sinks, 0.0), axis=1, keepdims=True)
            m = lax.stop_gradient(jnp.maximum(jnp.max(s, axis=1, keepdims=True), sink))
            e = jnp.exp(s - m)
            den = jnp.sum(e, axis=1, keepdims=True) + jnp.exp(sink - m)
            part = _mm(e / den, vg) * halves[hh]
            op = part if op is None else op + part
        outs.append(op)
    return jnp.concatenate(outs, axis=1)


def _swa_in_specs(nb, rev):
    blk = SWA_BLOCK
    step = (lambda i: nb - 1 - i) if rev else (lambda i: i)
    return [
        pl.BlockSpec((blk, 4 * LANE), lambda i: (step(i), 0)),
        pl.BlockSpec((blk, 4 * LANE), lambda i: (0, 1)),
        pl.BlockSpec((blk, 4 * LANE), lambda i: (jnp.maximum(step(i) - 1, 0), 1)),
        pl.BlockSpec((blk, 4 * LANE), lambda i: (step(i), 1)),
        pl.BlockSpec((1, SWA_Q_HEADS, blk, 3 * blk), lambda i: (jnp.minimum(step(i), 2), 0, 0, 0)),
        _full((1, LANE)),
    ]


def swa_fwd(proj, bias, sinks, *, name):
    tp = proj.shape[0]
    nb = tp // SWA_BLOCK

    def body(q_ref, kvm_ref, kvp_ref, kvc_ref, bias_ref, sinks_ref, o_ref):
        n = pl.program_id(0)
        o_ref[...] = _swa_block(q_ref[...], kvm_ref[...], kvp_ref[...], kvc_ref[...], bias_ref[0], sinks_ref[...], n)

    return pl.pallas_call(
        body, name=name, grid=(nb,),
        in_specs=_swa_in_specs(nb, False),
        out_specs=pl.BlockSpec((SWA_BLOCK, 4 * LANE), lambda i: (i, 0)),
        out_shape=jax.ShapeDtypeStruct((tp, 4 * LANE), F32),
    )(proj, proj, proj, proj, bias, sinks)


def swa_bwd(proj, bias, sinks, do, *, name):
    tp = proj.shape[0]
    nb = tp // SWA_BLOCK
    blk = SWA_BLOCK

    def body(q_ref, kvm_ref, kvp_ref, kvc_ref, bias_ref, sinks_ref, do_ref, dq_ref, dkv_ref, dbias_ref, dsinks_ref,
             carry, meta_acc):
        i = pl.program_id(0)
        n = nb - 1 - i

        @pl.when(i == 0)
        def _():
            carry[...] = jnp.zeros_like(carry)
            meta_acc[...] = jnp.zeros_like(meta_acc)
            dsinks_ref[...] = jnp.zeros_like(dsinks_ref)

        fn = lambda q, kvm, kvp, kvc, b, s: _swa_block(q, kvm, kvp, kvc, b, s, n)
        _, vjp = jax.vjp(fn, q_ref[...], kvm_ref[...], kvp_ref[...], kvc_ref[...], bias_ref[0], sinks_ref[...])
        dq, dkvm, dkvp, dkvc, dbias, dsinks = vjp(do_ref[...])
        dq_ref[...] = dq
        meta_acc[...] += dkvm
        dkv_ref[...] = dkvc + carry[...] + jnp.where(n == 0, meta_acc[...], 0.0)
        carry[...] = dkvp
        first_visit = (n == nb - 1) | (n < 2)

        @pl.when(first_visit)
        def _():
            dbias_ref[0] = dbias

        @pl.when(jnp.logical_not(first_visit))
        def _():
            dbias_ref[0] += dbias

        dsinks_ref[...] += dsinks

    rev = lambda i: nb - 1 - i
    return pl.pallas_call(
        body, name=name, grid=(nb,),
        in_specs=_swa_in_specs(nb, True) + [pl.BlockSpec((blk, 4 * LANE), lambda i: (rev(i), 0))],
        out_specs=(pl.BlockSpec((blk, 4 * LANE), lambda i: (rev(i), 0)),
                   pl.BlockSpec((blk, 4 * LANE), lambda i: (rev(i), 0)),
                   pl.BlockSpec((1, SWA_Q_HEADS, blk, 3 * blk), lambda i: (jnp.minimum(rev(i), 2), 0, 0, 0)),
                   _full((1, LANE))),
        out_shape=(jax.ShapeDtypeStruct((tp, 4 * LANE), F32), jax.ShapeDtypeStruct((tp, 4 * LANE), F32),
                   jax.ShapeDtypeStruct((3, SWA_Q_HEADS, blk, 3 * blk), F32), jax.ShapeDtypeStruct((1, LANE), F32)),
        scratch_shapes=[pltpu.VMEM((blk, 4 * LANE), F32), pltpu.VMEM((blk, 4 * LANE), F32)],
    )(proj, proj, proj, proj, bias, sinks, do)


CONV_COL0 = 2
HALO = 8


def conv_fwd(proj, conv_w, *, name):
    tp = proj.shape[0]
    tm = _row_tile(tp, 384)
    cw = 4 * LANE
    ncol = conv_w.shape[1] // cw

    def body(x_ref, halo_ref, w_ref, y_ref, buf):
        i = pl.program_id(1)
        buf[0:HALO, :] = jnp.where(i > 0, halo_ref[...], 0.0)
        buf[HALO:, :] = x_ref[...]
        acc = None
        for j in range(DN_CONV):
            term = w_ref[j:j + 1, :] * buf[pl.ds(HALO - (DN_CONV - 1) + j, tm), :]
            acc = term if acc is None else acc + term
        y_ref[...] = acc

    return pl.pallas_call(
        body, name=name, grid=(ncol, tp // tm),
        in_specs=[pl.BlockSpec((tm, cw), lambda c, i: (i, CONV_COL0 + c)),
                  pl.BlockSpec((HALO, cw), lambda c, i: (jnp.maximum(i * (tm // HALO) - 1, 0), CONV_COL0 + c)),
                  pl.BlockSpec((DN_CONV, cw), lambda c, i: (0, c))],
        out_specs=pl.BlockSpec((tm, cw), lambda c, i: (i, c)),
        out_shape=jax.ShapeDtypeStruct((tp, ncol * cw), F32),
        scratch_shapes=[pltpu.VMEM((tm + HALO, cw), F32)],
    )(proj, proj, conv_w)


def conv_bwd(proj, conv_w, dy, *, name):
    tp = proj.shape[0]
    tm = _row_tile(tp, 384)
    cw = 4 * LANE
    ncol = conv_w.shape[1] // cw
    nt = tp // tm

    def body(x_ref, xhalo_ref, w_ref, dy_ref, dyhalo_ref, dx_ref, dw_ref, xbuf, dbuf):
        i = pl.program_id(1)
        xbuf[0:HALO, :] = jnp.where(i > 0, xhalo_ref[...], 0.0)
        xbuf[HALO:, :] = x_ref[...]
        dbuf[0:tm, :] = dy_ref[...]
        dbuf[tm:, :] = jnp.where(i < nt - 1, dyhalo_ref[...], 0.0)
        dy_t = dy_ref[...]
        acc = None
        rows = []
        for j in range(DN_CONV):
            term = w_ref[j:j + 1, :] * dbuf[pl.ds(DN_CONV - 1 - j, tm), :]
            acc = term if acc is None else acc + term
            rows.append(jnp.sum(dy_t * xbuf[pl.ds(HALO - (DN_CONV - 1) + j, tm), :], axis=0, keepdims=True))
        dx_ref[...] = acc

        @pl.when(i == 0)
        def _():
            dw_ref[...] = jnp.zeros_like(dw_ref)

        for j in range(DN_CONV):
            dw_ref[j:j + 1, :] += rows[j]

    return pl.pallas_call(
        body, name=name, grid=(ncol, nt),
        in_specs=[pl.BlockSpec((tm, cw), lambda c, i: (i, CONV_COL0 + c)),
                  pl.BlockSpec((HALO, cw), lambda c, i: (jnp.maximum(i * (tm // HALO) - 1, 0), CONV_COL0 + c)),
                  pl.BlockSpec((DN_CONV, cw), lambda c, i: (0, c)),
                  pl.BlockSpec((tm, cw), lambda c, i: (i, c)),
                  pl.BlockSpec((HALO, cw), lambda c, i: (jnp.minimum((i + 1) * (tm // HALO), tp // HALO - 1), c))],
        out_specs=(pl.BlockSpec((tm, cw), lambda c, i: (i, c)), pl.BlockSpec((DN_CONV, cw), lambda c, i: (0, c))),
        out_shape=(jax.ShapeDtypeStruct((tp, ncol * cw), F32), jax.ShapeDtypeStruct((DN_CONV, ncol * cw), F32)),
        scratch_shapes=[pltpu.VMEM((tm + HALO, cw), F32), pltpu.VMEM((tm + HALO, cw), F32)],
    )(proj, proj, conv_w, dy, dy)


def _stack(parts):
    return jnp.concatenate([p[None] for p in parts], axis=0)


def _chunk_masks():
    r = lax.broadcasted_iota(jnp.int32, (CHUNK, CHUNK), 0)
    c = lax.broadcasted_iota(jnp.int32, (CHUNK, CHUNK), 1)
    return (r >= c).astype(F32), (r > c).astype(F32), (r == c).astype(F32)


def _dn_chunk(y, z, small, s, a_log, dt_bias, norm_w, rows):
    tri_incl, tri_strict, eye = _chunk_masks()
    lane = lax.broadcasted_iota(jnp.int32, (1, LANE), 1)
    dk = DN_HEAD_DIM
    nh = DN_HEADS
    heads = lambda t, first: _stack([t[:, (first + h) * dk:(first + h + 1) * dk] for h in range(nh)])
    pick = lambda t, l: jnp.sum(jnp.where(lane == l, t, 0.0), axis=1, keepdims=True)
    q = _l2n(_silu(heads(y, 0))) * dk ** -0.5
    k = _l2n(_silu(heads(y, nh)))
    v = _silu(heads(y, 2 * nh))
    g_all = jnp.where(lane < nh, -jnp.exp(a_log) * _softplus(small + dt_bias), 0.0) * rows
    beta_all = _sigmoid(small)
    gc_all = _cumsum_rows(g_all)
    g_sum = jnp.sum(g_all, axis=0, keepdims=True)
    gc = _stack([pick(gc_all, h) for h in range(nh)])
    beta = _stack([pick(beta_all, nh + h) for h in range(nh)])
    g_last = _stack([pick(g_sum, h) for h in range(nh)])
    gc_row = jnp.sum(eye * gc, axis=1, keepdims=True)
    gamma = jnp.exp((gc - gc_row) * tri_incl) * tri_incl
    k_beta = k * beta
    v_beta = v * beta
    a = _bmm_nt(k_beta, k) * gamma * tri_strict
    inv = eye - a
    power = a
    for _ in range(5):
        power = _bmm3(power, power)
        inv = inv + _bmm3(inv, power)
    e_gc = jnp.exp(gc)
    uw = _bmm3(inv, jnp.concatenate([v_beta, k_beta * e_gc], axis=2))
    u, w = uw[:, :, :dk], uw[:, :, dk:]
    attn = _bmm_nt(q, k) * gamma
    q_dec = q * e_gc
    k_dec = k * jnp.exp(g_last - gc)
    v_new = u - _bmm(w, s)
    o = _bmm(q_dec, s) + _bmm(attn, v_new)
    s_new = s * jnp.exp(g_last) + _bmm_tn(k_dec, v_new)
    out = _rms(o, norm_w) * _silu(heads(z, 0))
    return jnp.concatenate([out[h] for h in range(nh)], axis=1), s_new


Z_COL = 5
SMALL_COL = 24


def _chunk_rows(n):
    row = n * CHUNK + lax.broadcasted_iota(jnp.int32, (CHUNK, 1), 0)
    return (row >= PAD).astype(F32)


def dn_fwd(y, proj, a_log, dt_bias, norm_w, *, name):
    tp = y.shape[0]
    nc = tp // CHUNK
    dk = DN_HEAD_DIM

    def body(y_ref, z_ref, small_ref, al_ref, dt_ref, nw_ref, o_ref, ssave_ref, state):
        n = pl.program_id(0)

        @pl.when(n == 0)
        def _():
            state[...] = jnp.zeros_like(state)

        ssave_ref[0] = state[...]
        out, s_new = _dn_chunk(y_ref[...], z_ref[...], small_ref[...], state[...], al_ref[...], dt_ref[...],
                               nw_ref[...], _chunk_rows(n))
        o_ref[...] = out
        state[...] = s_new

    return pl.pallas_call(
        body, name=name, grid=(nc,),
        in_specs=[pl.BlockSpec((CHUNK, y.shape[1]), lambda n: (n, 0)),
                  pl.BlockSpec((CHUNK, 4 * LANE), lambda n: (n, Z_COL)),
                  pl.BlockSpec((CHUNK, LANE), lambda n: (n, SMALL_COL)),
                  _full((1, LANE)), _full((1, LANE)), _full((1, LANE))],
        out_specs=(pl.BlockSpec((CHUNK, 4 * LANE), lambda n: (n, 0)),
                   pl.BlockSpec((1, DN_HEADS, dk, dk), lambda n: (n, 0, 0, 0))),
        out_shape=(jax.ShapeDtypeStruct((tp, 4 * LANE), F32), jax.ShapeDtypeStruct((nc, DN_HEADS, dk, dk), F32)),
        scratch_shapes=[pltpu.VMEM((DN_HEADS, dk, dk), F32)],
    )(y, proj, proj, a_log, dt_bias, norm_w)


def dn_bwd(y, proj, a_log, dt_bias, norm_w, ssave, do, *, name):
    tp = y.shape[0]
    nc = tp // CHUNK
    dk = DN_HEAD_DIM
    rev = lambda i: nc - 1 - i

    def body(y_ref, z_ref, small_ref, al_ref, dt_ref, nw_ref, ss_ref, do_ref,
             dy_ref, dz_ref, dsmall_ref, dal_ref, ddt_ref, dnw_ref, dstate):
        i = pl.program_id(0)
        n = nc - 1 - i

        @pl.when(i == 0)
        def _():
            dstate[...] = jnp.zeros_like(dstate)
            dal_ref[...] = jnp.zeros_like(dal_ref)
            ddt_ref[...] = jnp.zeros_like(ddt_ref)
            dnw_ref[...] = jnp.zeros_like(dnw_ref)

        rows = _chunk_rows(n)
        fn = lambda *a: _dn_chunk(*a, rows)
        _, vjp = jax.vjp(fn, y_ref[...], z_ref[...], small_ref[...], ss_ref[0], al_ref[...], dt_ref[...], nw_ref[...])
        dy, dz, dsmall, ds, dal, ddt, dnw = vjp((do_ref[...], dstate[...]))
        dy_ref[...] = dy
        dz_ref[...] = dz
        dsmall_ref[...] = dsmall
        dstate[...] = ds
        dal_ref[...] += dal
        ddt_ref[...] += ddt
        dnw_ref[...] += dnw

    return pl.pallas_call(
        body, name=name, grid=(nc,),
        in_specs=[pl.BlockSpec((CHUNK, y.shape[1]), lambda i: (rev(i), 0)),
                  pl.BlockSpec((CHUNK, 4 * LANE), lambda i: (rev(i), Z_COL)),
                  pl.BlockSpec((CHUNK, LANE), lambda i: (rev(i), SMALL_COL)),
                  _full((1, LANE)), _full((1, LANE)), _full((1, LANE)),
                  pl.BlockSpec((1, DN_HEADS, dk, dk), lambda i: (rev(i), 0, 0, 0)),
                  pl.BlockSpec((CHUNK, 4 * LANE), lambda i: (rev(i), 1))],
        out_specs=(pl.BlockSpec((CHUNK, y.shape[1]), lambda i: (rev(i), 0)),
                   pl.BlockSpec((CHUNK, 4 * LANE), lambda i: (rev(i), 0)),
                   pl.BlockSpec((CHUNK, LANE), lambda i: (rev(i), 0)),
                   _full((1, LANE)), _full((1, LANE)), _full((1, LANE))),
        out_shape=(jax.ShapeDtypeStruct((tp, y.shape[1]), F32), jax.ShapeDtypeStruct((tp, 4 * LANE), F32),
                   jax.ShapeDtypeStruct((tp, LANE), F32), jax.ShapeDtypeStruct((1, LANE), F32),
                   jax.ShapeDtypeStruct((1, LANE), F32), jax.ShapeDtypeStruct((1, LANE), F32)),
        scratch_shapes=[pltpu.VMEM((DN_HEADS, dk, dk), F32)],
    )(y, proj, proj, a_log, dt_bias, norm_w, ssave, do)


def _gla_chunk(q, k, v, gate, low, s, w_gate_up, b_gate, norm_w, rows):
    tri_incl, _, _ = _chunk_masks()
    dk, dv, nh = GLA_DK, GLA_DV, GLA_HEADS
    heads = lambda t, width: _stack([t[:, h * width:(h + 1) * width] for h in range(nh)])
    logit = _mm3(low, w_gate_up) + b_gate
    glog_all = -_softplus(-logit) * (1.0 / GLA_GATE_NORM) * rows
    glog = heads(glog_all, dk)
    bcum = heads(_cumsum_rows(glog_all), dk)
    qh = heads(q, dk) * dk ** -0.5
    kh = heads(k, dk)
    vh = heads(v, dv)
    q_dec = qh * jnp.exp(bcum)
    attn = _bmm_nt(q_dec, kh * jnp.exp(-bcum)) * tri_incl
    b_last = jnp.sum(glog, axis=1, keepdims=True)
    k_dec = kh * jnp.exp(b_last - bcum)
    r = lax.broadcasted_iota(jnp.int32, (dk, dk), 0)
    c = lax.broadcasted_iota(jnp.int32, (dk, dk), 1)
    b_last_col = jnp.sum((r == c).astype(F32) * b_last, axis=2, keepdims=True)
    o = _bmm(attn, vh) + _bmm(q_dec, s)
    s_new = s * jnp.exp(b_last_col) + _bmm_tn(k_dec, vh)
    out = _rms(o, norm_w) * _silu(heads(gate, dv))
    return jnp.concatenate([out[h] for h in range(nh)], axis=1), s_new


LOW_COL = 24


def _gla_in_specs(step):
    return [pl.BlockSpec((CHUNK, 4 * LANE), lambda i: (step(i), 0)),
            pl.BlockSpec((CHUNK, 4 * LANE), lambda i: (step(i), 1)),
            pl.BlockSpec((CHUNK, 8 * LANE), lambda i: (step(i), 1)),
            pl.BlockSpec((CHUNK, 8 * LANE), lambda i: (step(i), 2)),
            pl.BlockSpec((CHUNK, LANE), lambda i: (step(i), LOW_COL)),
            _full((LANE, 4 * LANE)), _full((1, 4 * LANE)), _full((1, GLA_DV))]


def gla_fwd(proj, w_gate_up, b_gate, norm_w, *, name):
    tp = proj.shape[0]
    nc = tp // CHUNK

    def body(q_ref, k_ref, v_ref, g_ref, low_ref, wgu_ref, bg_ref, nw_ref, o_ref, ssave_ref, state):
        n = pl.program_id(0)

        @pl.when(n == 0)
        def _():
            state[...] = jnp.zeros_like(state)

        ssave_ref[0] = state[...]
        out, s_new = _gla_chunk(q_ref[...], k_ref[...], v_ref[...], g_ref[...], low_ref[...], state[...], wgu_ref[...],
                                bg_ref[...], nw_ref[...], _chunk_rows(n))
        o_ref[...] = out
        state[...] = s_new

    return pl.pallas_call(
        body, name=name, grid=(nc,),
        in_specs=_gla_in_specs(lambda i: i),
        out_specs=(pl.BlockSpec((CHUNK, 8 * LANE), lambda n: (n, 0)),
                   pl.BlockSpec((1, GLA_HEADS, GLA_DK, GLA_DV), lambda n: (n, 0, 0, 0))),
        out_shape=(jax.ShapeDtypeStruct((tp, 8 * LANE), F32),
                   jax.ShapeDtypeStruct((nc, GLA_HEADS, GLA_DK, GLA_DV), F32)),
        scratch_shapes=[pltpu.VMEM((GLA_HEADS, GLA_DK, GLA_DV), F32)],
    )(proj, proj, proj, proj, proj, w_gate_up, b_gate, norm_w)


def gla_bwd(proj, w_gate_up, b_gate, norm_w, ssave, do, *, name):
    tp = proj.shape[0]
    nc = tp // CHUNK
    rev = lambda i: nc - 1 - i

    def body(q_ref, k_ref, v_ref, g_ref, low_ref, wgu_ref, bg_ref, nw_ref, ss_ref, do_ref,
             dq_ref, dk_ref, dv_ref, dg_ref, dlow_ref, dwgu_ref, dbg_ref, dnw_ref, dstate):
        i = pl.program_id(0)
        n = nc - 1 - i

        @pl.when(i == 0)
        def _():
            dstate[...] = jnp.zeros_like(dstate)
            dwgu_ref[...] = jnp.zeros_like(dwgu_ref)
            dbg_ref[...] = jnp.zeros_like(dbg_ref)
            dnw_ref[...] = jnp.zeros_like(dnw_ref)

        rows = _chunk_rows(n)
        fn = lambda *a: _gla_chunk(*a, rows)
        _, vjp = jax.vjp(fn, q_ref[...], k_ref[...], v_ref[...], g_ref[...], low_ref[...], ss_ref[0], wgu_ref[...],
                         bg_ref[...], nw_ref[...])
        dq, dk, dv, dg, dlow, ds, dwgu, dbg, dnw = vjp((do_ref[...], dstate[...]))
        dq_ref[...] = dq
        dk_ref[...] = dk
        dv_ref[...] = dv
        dg_ref[...] = dg
        dlow_ref[...] = dlow
        dstate[...] = ds
        dwgu_ref[...] += dwgu
        dbg_ref[...] += dbg
        dnw_ref[...] += dnw

    chunk = lambda width: pl.BlockSpec((CHUNK, width), lambda i: (rev(i), 0))
    return pl.pallas_call(
        body, name=name, grid=(nc,),
        in_specs=_gla_in_specs(rev) + [pl.BlockSpec((1, GLA_HEADS, GLA_DK, GLA_DV), lambda i: (rev(i), 0, 0, 0)),
                                       chunk(8 * LANE)],
        out_specs=(chunk(4 * LANE), chunk(4 * LANE), chunk(8 * LANE), chunk(8 * LANE), chunk(LANE),
                   _full((LANE, 4 * LANE)), _full((1, 4 * LANE)), _full((1, GLA_DV))),
        out_shape=(jax.ShapeDtypeStruct((tp, 4 * LANE), F32), jax.ShapeDtypeStruct((tp, 4 * LANE), F32),
                   jax.ShapeDtypeStruct((tp, 8 * LANE), F32), jax.ShapeDtypeStruct((tp, 8 * LANE), F32),
                   jax.ShapeDtypeStruct((tp, LANE), F32), jax.ShapeDtypeStruct((LANE, 4 * LANE), F32),
                   jax.ShapeDtypeStruct((1, 4 * LANE), F32), jax.ShapeDtypeStruct((1, GLA_DV), F32)),
        scratch_shapes=[pltpu.VMEM((GLA_HEADS, GLA_DK, GLA_DV), F32)],
    )(proj, proj, proj, proj, proj, w_gate_up, b_gate, norm_w, ssave, do)


def _even_proj_weight(w_in):
    hd = SWA_HEAD_DIM
    k0, k1 = w_in[:, 512:512 + hd], w_in[:, 512 + hd:640]
    v0, v1 = w_in[:, 640:640 + hd], w_in[:, 640 + hd:768]
    zeros = jnp.zeros((w_in.shape[0], LANE - 2 * DN_HEADS), w_in.dtype)
    return jnp.concatenate([w_in[:, :512], k0, k0, k1, k1, v0, v0, v1, v1, w_in[:, 768:2816], w_in[:, 2820:2824],
                            w_in[:, 2816:2820], zeros], axis=1)


def _even_proj_weight_grad(dw):
    hd = SWA_HEAD_DIM
    c = lambda i: dw[:, 512 + i * hd:512 + (i + 1) * hd]
    return jnp.concatenate([dw[:, :512], c(0) + c(1), c(2) + c(3), c(4) + c(5), c(6) + c(7), dw[:, 1024:3072],
                            dw[:, 3076:3080], dw[:, 3072:3076]], axis=1)


def _ffn_fwd(h, nw_in, nw_out, w_gu, w_down, idx):
    hn, g, u, a = rms_mm(h, nw_in, w_gu[0], swiglu=True, name=f"ffn_up_{idx}", widx=w_gu[1])
    f, h_out = mm_rms_res([a], w_down[0], h, nw_out, scale=0.5, name=f"ffn_down_{idx}", widx=w_down[1])
    return h_out, (h, hn, g, u, a, f)


def _ffn_bwd(dho, saved, nw_in, nw_out, w_gu, w_down, idx):
    h, hn, g, u, a, f = saved
    df, dnw_out, dgu = mm_rms_res_bwd(dho, f, nw_out, w_down[0], (g, u), scale=0.5, name=f"ffn_down_bwd_{idx}",
                                      widx=w_down[1])
    dh, dnw_in = rms_mm_bwd([dgu], w_gu[0], h, nw_in, dho, name=f"ffn_up_bwd_{idx}", widx=w_gu[1])
    g_gu = mm_tn(hn, dgu, name=f"ffn_dwgu_{idx}", out_dtype=BF16)
    g_down = mm_tn(a, df, name=f"ffn_dwd_{idx}", out_dtype=BF16)
    return dh, dnw_in, dnw_out, g_gu, g_down


def local_step(x, target, wts, get_w=None, put_g=None):
    seq, d = x.shape
    wts = dict(wts)
    get_w = get_w or (lambda stage, after: {})
    put_g = put_g or (lambda stage, grads: jnp.zeros((1, 1), F32))
    row = lambda v: v.reshape(1, -1)
    lane_row = lambda v: jnp.pad(v.reshape(1, -1), ((0, 0), (0, LANE - v.size)))
    nw = wts["norm_w"]
    h = jnp.concatenate([jnp.zeros((PAD, d), F32), wts["meta_tokens"], x], axis=0)
    buckets = _swa_buckets()
    bias = swa_bias(wts["rel_bias_table"], buckets, name="swa_bias")
    sinks = lane_row(wts["swa_sinks"])
    a_log, dt_bias = lane_row(wts["dn_a_log"]), lane_row(wts["dn_dt_bias"])
    dn_norm_w = row(wts["dn_norm_w"])
    conv_w = wts["even_conv_w"][0]
    w_gate_up = jnp.pad(wts["gla_w_gate_up"][0], ((0, LANE - GLA_GATE_RANK), (0, 0)))
    b_gate, gla_norm_w = row(wts["gla_b_gate"]), row(wts["gla_norm_w"])

    saved = []
    w_in, w_out = [None, None], [None, None]
    for l in range(2):
        if l == 0:
            w_in[0], w_out[0] = _even_proj_weight(wts["even_w_in"]), wts["even_w_out"]
        else:
            wts.update(get_w("layer1", h))
            w_in[1] = jnp.pad(wts["odd_w_in"], ((0, 0), (0, PROJ_DIM - wts["odd_w_in"].shape[1])))
            w_out[1] = wts["odd_w_out"]
        h, s_a = _ffn_fwd(h, row(nw[l, 0]), row(nw[l, 1]), wts[f"w_gu{2 * l}"], wts[f"w_down{2 * l}"], 2 * l)
        h_mix = h
        hn, proj = rms_mm(h, row(nw[l, 2]), w_in[l], swiglu=False, name=f"mix_in_{l}")
        if l == 0:
            o_a = swa_fwd(proj, bias, sinks, name="swa_fwd")
            y = conv_fwd(proj, conv_w, name="conv_fwd")
            o_b, ssave = dn_fwd(y, proj, a_log, dt_bias, dn_norm_w, name="dn_fwd")
            acts, extra = [o_a, o_b], (y, ssave)
        else:
            o, ssave = gla_fwd(proj, w_gate_up, b_gate, gla_norm_w, name="gla_fwd")
            acts, extra = [o], (ssave,)
        mix, h = mm_rms_res(acts, w_out[l], h, row(nw[l, 3]), scale=1.0, name=f"mix_out_{l}")
        s_m = (h_mix, hn, proj, acts, extra, mix)
        if l == 0:
            wts.update(get_w("ffn1", h))
        h, s_b = _ffn_fwd(h, row(nw[l, 4]), row(nw[l, 5]), wts[f"w_gu{2 * l + 1}"], wts[f"w_down{2 * l + 1}"], 2 * l + 1)
        saved.append((s_a, s_m, s_b))

    dh, loss = loss_and_grad(h, target, name="loss")

    grads = {}
    dnw = [[None] * 6 for _ in range(2)]
    sent = jnp.zeros((1, 1), F32)
    for l in (1, 0):
        s_a, s_m, s_b = saved[l]
        i = 2 * l + 1
        dh, dnw[l][4], dnw[l][5], grads[f"g_gu{i}"], grads[f"g_down{i}"] = _ffn_bwd(
            dh, s_b, row(nw[l, 4]), row(nw[l, 5]) + sent, wts[f"w_gu{i}"], wts[f"w_down{i}"], i)
        sent = put_g("ffn3" if l == 1 else "ffn1", grads)
        h_mix, hn, proj, acts, extra, mix = s_m
        dmix, dnw[l][3], do = mm_rms_res_bwd(dh, mix, row(nw[l, 3]) + sent, w_out[l], None, scale=1.0,
                                             name=f"mix_out_bwd_{l}")
        dw_out = jnp.concatenate([mm_tn(a, dmix, name=f"mix_dwo_{l}_{i}") for i, a in enumerate(acts)], axis=0)
        if l == 0:
            y, ssave = extra
            dq, dkv, dbias, dsinks = swa_bwd(proj, bias, sinks, do, name="swa_bwd")
            dy, dz, dsmall, da_log, ddt_bias, ddn_norm_w = dn_bwd(y, proj, a_log, dt_bias, dn_norm_w, ssave, do,
                                                                    name="dn_bwd")
            dxc, dconv_w = conv_bwd(proj, conv_w, dy, name="conv_bwd")
            dps = [dq, dkv, dxc, dz, dsmall]
            grads["rel_bias_table"] = swa_bias_bwd(dbias, buckets, name="swa_bias_bwd")[:, :SWA_Q_HEADS]
            grads["swa_sinks"] = dsinks[:, :SWA_Q_HEADS]
            grads["dn_a_log"] = da_log[:, :DN_HEADS]
            grads["dn_dt_bias"] = ddt_bias[:, :DN_HEADS]
            grads["dn_norm_w"] = ddn_norm_w
            grads["even_conv_w"] = dconv_w[None]
            grads["even_w_out"] = dw_out
        else:
            (ssave,) = extra
            dq, dk, dv, dgate, dlow, dwgu, dbg, dgnw = gla_bwd(proj, w_gate_up, b_gate, gla_norm_w, ssave, do,
                                                               name="gla_bwd")
            dps = [dq, dk, dv, dgate, dlow]
            grads["gla_w_gate_up"] = dwgu[None, :GLA_GATE_RANK]
            grads["gla_b_gate"] = dbg
            grads["gla_norm_w"] = dgnw
            grads["odd_w_out"] = dw_out
        dw_in = jnp.concatenate([mm_tn(hn, dp, name=f"mix_dwi_{l}_{i}") for i, dp in enumerate(dps)], axis=1)
        if l == 0:
            grads["even_w_in"] = _even_proj_weight_grad(dw_in)
        else:
            grads["odd_w_in"] = dw_in[:, :wts["odd_w_in"].shape[1]]
        dh, dnw[l][2] = rms_mm_bwd(dps, w_in[l], h_mix, row(nw[l, 2]), dh, name=f"mix_in_bwd_{l}")
        i = 2 * l
        dh, dnw[l][0], dnw[l][1], grads[f"g_gu{i}"], grads[f"g_down{i}"] = _ffn_bwd(
            dh, s_a, row(nw[l, 0]), row(nw[l, 1]), wts[f"w_gu{i}"], wts[f"w_down{i}"], i)
        if l == 1:
            sent = put_g("layer1", grads)

    grads["norm_w"] = jnp.stack([jnp.concatenate(r, axis=0) for r in dnw])
    grads["meta_tokens"] = dh[PAD:PAD + N_META]
    return loss[0, 0], dh[PAD + N_META:], grads


def _peer(k):
    x, y, c = (lax.axis_index(a) for a in AXES)
    flip = lambda v, bit: 1 - v if bit else v
    return (flip(x, k & 4), flip(y, k & 2), flip(c, k & 1))


def _my_index():
    x, y, c = (lax.axis_index(a) for a in AXES)
    return 4 * x + 2 * y + c


_HBM = pl.BlockSpec(memory_space=pltpu.HBM)
_SEM = pl.BlockSpec(memory_space=pltpu.SEMAPHORE)
_EFFECT = pltpu.SideEffectType.DATAFLOW_SIDE_EFFECTING


def _remote_copies(items, src_refs, land_refs, send_sems, recv_sems):
    me = _my_index()
    copies = []
    for k in range(1, N_DEV):
        px, py, pc = _peer(k)
        pj = 4 * px + 2 * py + pc
        for a, (sn, send, ln, land, _) in enumerate(items):
            sem = (k - 1) * len(items) + a
            copies.append(pltpu.make_async_remote_copy(
                src_ref=send(src_refs[sn], pj), dst_ref=land(land_refs[ln], me), send_sem=send_sems.at[sem],
                recv_sem=recv_sems.at[sem], device_id=(px, py, pc), device_id_type=MESH))
    return copies


def exchange(srcs, lands, items, remote_groups, *, name):
    sn, ln = list(srcs), list(lands)
    given = [n for n in ln if not isinstance(lands[n], jax.ShapeDtypeStruct)]
    remote = [it for it in items if it[4] in remote_groups]

    def body(*refs):
        src_refs = dict(zip(sn, refs[:len(sn)]))
        land_refs = dict(zip(ln, refs[len(sn) + len(given):len(sn) + len(given) + len(ln)]))
        send_sems, recv_sems, local_sems = refs[len(sn) + len(given) + len(ln):]
        me = _my_index()
        local = [pltpu.make_async_copy(send(src_refs[s], me), land(land_refs[l], me), local_sems.at[a])
                 for a, (s, send, l, land, _) in enumerate(items)]
        for cp in local:
            cp.start()
        copies = _remote_copies(remote, src_refs, land_refs, send_sems, recv_sems)
        for cp in copies:
            cp.start()
        for cp in copies:
            cp.wait_recv()
        for cp in copies:
            cp.wait_send()
        for cp in local:
            cp.wait()

    n_remote = (N_DEV - 1) * len(remote)
    out_shape = tuple(jax.ShapeDtypeStruct(lands[n].shape, lands[n].dtype) for n in ln)
    outs = pl.pallas_call(
        body, name=name,
        in_specs=[pl.BlockSpec(memory_space=pl.ANY)] * (len(sn) + len(given)),
        out_specs=tuple(pl.BlockSpec(memory_space=pl.ANY) for _ in ln),
        out_shape=out_shape,
        input_output_aliases={len(sn) + i: ln.index(n) for i, n in enumerate(given)},
        scratch_shapes=[pltpu.SemaphoreType.DMA((n_remote,)), pltpu.SemaphoreType.DMA((n_remote,)),
                        pltpu.SemaphoreType.DMA((len(items),))],
    )(*[srcs[n] for n in sn], *[lands[n] for n in given])
    return dict(zip(ln, outs))


def start_copies(srcs, lands, items, *, name):
    sn, ln = list(srcs), list(lands)
    n_remote = (N_DEV - 1) * len(items)

    def body(*refs):
        src_refs = dict(zip(sn, refs[:len(sn)]))
        land_refs = dict(zip(ln, refs[len(sn):len(sn) + len(ln)]))
        send_sems, recv_sems = refs[len(sn) + len(ln):len(sn) + len(ln) + 2]
        token = refs[-1]
        for cp in _remote_copies(items, src_refs, land_refs, send_sems, recv_sems):
            cp.start()
        token[...] = jnp.zeros_like(token)

    hbm = lambda a: pltpu.with_memory_space_constraint(a, pltpu.HBM)
    outs = pl.pallas_call(
        body, name=name,
        in_specs=[_HBM] * (len(sn) + len(ln)),
        out_specs=(_SEM, _SEM) + (_HBM,) * len(ln) + (pl.BlockSpec(memory_space=pltpu.VMEM),),
        out_shape=(pltpu.SemaphoreType.DMA((n_remote,)), pltpu.SemaphoreType.DMA((n_remote,)))
        + tuple(pltpu.HBM(lands[n].shape, lands[n].dtype) for n in ln) + (jax.ShapeDtypeStruct((8, LANE), F32),),
        input_output_aliases={len(sn) + i: 2 + i for i in range(len(ln))},
        compiler_params=pltpu.CompilerParams(has_side_effects=_EFFECT),
    )(*[hbm(srcs[n]) for n in sn], *[hbm(lands[n]) for n in ln])
    return (outs[0], outs[1]), dict(zip(ln, outs[2:2 + len(ln)])), outs[-1][0:1, 0:1]


def wait_copies(sems, srcs, lands, items, after, *, name):
    sn, ln = list(srcs), list(lands)

    def body(*refs):
        src_refs = dict(zip(sn, refs[:len(sn)]))
        land_refs = dict(zip(ln, refs[len(sn):len(sn) + len(ln)]))
        send_sems, recv_sems = refs[len(sn) + len(ln):len(sn) + len(ln) + 2]
        copies = _remote_copies(items, src_refs, land_refs, send_sems, recv_sems)
        for cp in copies:
            cp.wait_send()
        for cp in copies:
            cp.wait_recv()

    outs = pl.pallas_call(
        body, name=name,
        in_specs=[_HBM] * (len(sn) + len(ln)) + [_SEM, _SEM, pl.BlockSpec(memory_space=pl.ANY)],
        out_specs=(_HBM,) * len(ln),
        out_shape=tuple(pltpu.HBM(lands[n].shape, lands[n].dtype) for n in ln),
        input_output_aliases={len(sn) + i: i for i in range(len(ln))},
        compiler_params=pltpu.CompilerParams(has_side_effects=_EFFECT),
    )(*[srcs[n] for n in sn], *[lands[n] for n in ln], sems[0], sems[1], after)
    return dict(zip(ln, outs))


def _block(index, size):
    return pl.ds(pl.multiple_of(index * size, LANE), size)


def _adam_tile(rows):
    for t in (256, 176, 128):
        if rows % t == 0:
            return t
    return rows


def sum_adamw(recv, col0, w, m, v, *, name):
    b, r, c = w.shape
    cp = recv.shape[-1]
    tr = _adam_tile(r)
    c1 = 1.0 / (1.0 - ADAM_B1 ** ADAM_STEP)
    c2 = 1.0 / (1.0 - ADAM_B2 ** ADAM_STEP)

    def body(recv_ref, w_ref, m_ref, v_ref, g_ref, d_ref, nm_ref, nv_ref):
        g = recv_ref[0, 0, :, col0:col0 + c].astype(F32)
        for i in range(1, N_DEV):
            g = g + recv_ref[i, 0, :, col0:col0 + c].astype(F32)
        nm = ADAM_B1 * m_ref[0] + (1.0 - ADAM_B1) * g
        nv = ADAM_B2 * v_ref[0] + (1.0 - ADAM_B2) * (g * g)
        g_ref[0] = g
        nm_ref[0] = nm
        nv_ref[0] = nv
        d_ref[0] = -ADAM_LR * ((nm * c1) / (jnp.sqrt(nv * c2) + ADAM_EPS) + ADAM_WD * w_ref[0])

    tile = pl.BlockSpec((1, tr, c), lambda bi, i: (bi, i, 0))
    return pl.pallas_call(
        body, name=name, grid=(b, r // tr),
        in_specs=[pl.BlockSpec((N_DEV, 1, tr, cp), lambda bi, i: (0, bi, i, 0)), tile, tile, tile],
        out_specs=(tile,) * 4, out_shape=(jax.ShapeDtypeStruct((b, r, c), F32),) * 4,
    )(recv, w, m, v)


def _flat_rows(n_elems, row_multiple):
    rows = -(-n_elems // FLAT_COLS)
    return -(-rows // row_multiple) * row_multiple


def _pack(arrays, row_multiple, dtype):
    flat = jnp.concatenate([a.reshape(-1).astype(dtype) for a in arrays])
    rows = _flat_rows(flat.size, row_multiple)
    return jnp.pad(flat, (0, rows * FLAT_COLS - flat.size)).reshape(rows, FLAT_COLS)


def _unpack(flat2d, shapes):
    lead = flat2d.shape[:-2]
    flat = flat2d.reshape(lead + (-1,))
    out, off = [], 0
    for shp in shapes:
        n = int(np.prod(shp))
        out.append(flat[..., off:off + n].reshape(lead + tuple(shp)))
        off += n
    return out


def _join_shards(stacked, axis):
    moved = jnp.moveaxis(stacked, 0, axis)
    shp = list(moved.shape)
    shp[axis:axis + 2] = [shp[axis] * shp[axis + 1]]
    return moved.reshape(shp)


def _split_shards(full, axis):
    shp = list(full.shape)
    shp[axis:axis + 1] = [N_DEV, shp[axis] // N_DEV]
    return jnp.moveaxis(full.reshape(shp), axis, 0)


def kernel(x, meta_tokens, norm_w, ffn_w_gate, ffn_w_up, ffn_w_down, rel_bias_table, even_w_in, even_conv_w, swa_sinks, dn_a_log, dn_dt_bias, dn_norm_w, even_w_out, odd_w_in, gla_w_gate_up, gla_b_gate, gla_norm_w, odd_w_out, loss_target, m_meta_tokens, m_norm_w, m_ffn_w_gate, m_ffn_w_up, m_ffn_w_down, m_rel_bias_table, m_even_w_in, m_even_conv_w, m_swa_sinks, m_dn_a_log, m_dn_dt_bias, m_dn_norm_w, m_even_w_out, m_odd_w_in, m_gla_w_gate_up, m_gla_b_gate, m_gla_norm_w, m_odd_w_out, v_meta_tokens, v_norm_w, v_ffn_w_gate, v_ffn_w_up, v_ffn_w_down, v_rel_bias_table, v_even_w_in, v_even_conv_w, v_swa_sinks, v_dn_a_log, v_dn_dt_bias, v_dn_norm_w, v_even_w_out, v_odd_w_in, v_gla_w_gate_up, v_gla_b_gate, v_gla_norm_w, v_odd_w_out):
    args = locals()
    w = {n: args[n] for n in WEIGHTS}
    m = {n: args["m_" + n] for n in WEIGHTS}
    v = {n: args["v_" + n] for n in WEIGHTS}

    d = D_MODEL
    sds = jax.ShapeDtypeStruct
    whole = lambda ref, j: ref
    cols = lambda size, base=0: (lambda ref, i: ref.at[(slice(None),) * (len(ref.shape) - 1)
                                                       + (pl.ds(pl.multiple_of(base + i * size, LANE), size),)])
    rows3 = lambda size: (lambda ref, i: ref.at[:, _block(i, size), :])
    rows2 = lambda size: (lambda ref, i: ref.at[_block(i, size), :])
    lead = lambda ref, i: ref.at[i]

    pad_cols = lambda a, to: jnp.pad(a, [(0, 0)] * (a.ndim - 1) + [(0, to - a.shape[-1])])
    gate_s = pad_cols(w["ffn_w_gate"].reshape(N_FFN, d, FF_SHARD), FF_SHARD_PAD).astype(BF16)
    up_s = pad_cols(w["ffn_w_up"].reshape(N_FFN, d, FF_SHARD), FF_SHARD_PAD).astype(BF16)
    down_s = jnp.pad(w["ffn_w_down"].reshape(N_FFN, FF_SHARD, d),
                     ((0, 0), (0, FF_SHARD_PAD - FF_SHARD), (0, 0))).astype(BF16)
    small_s = _pack([w[n] for n in SMALL], 8, F32)
    srcs_w = {"ein": pad_cols(w["even_w_in"][0], EVEN_IN_SHARD_PAD).astype(BF16), "eout": w["even_w_out"][0].astype(BF16),
              "small": small_s, "oin": pad_cols(w["odd_w_in"][0], ODD_IN_SHARD_PAD).astype(BF16),
              "oout": w["odd_w_out"][0].astype(BF16)}
    lands_w = {"ein": sds((d, N_DEV * EVEN_IN_SHARD_PAD), BF16), "eout": sds((d, d), BF16),
               "small": sds((N_DEV,) + small_s.shape, F32), "oin": sds((d, N_DEV * ODD_IN_SHARD_PAD), BF16),
               "oout": sds((d, d), BF16)}
    items_w = [("ein", whole, "ein", cols(EVEN_IN_SHARD_PAD), 0), ("eout", whole, "eout", rows2(OUT_SHARD), 0),
               ("small", whole, "small", lead, 0), ("oin", whole, "oin", cols(ODD_IN_SHARD_PAD), 2),
               ("oout", whole, "oout", rows2(OUT_SHARD), 2)]
    for tag, pick, group in (("0", 0, 0), ("1", 1, 1), ("23", slice(2, 4), 2)):
        lead_dims = (2,) if tag == "23" else ()
        srcs_w.update({f"gate{tag}": gate_s[pick], f"up{tag}": up_s[pick], f"down{tag}": down_s[pick]})
        lands_w.update({f"w_gu{tag}": sds(lead_dims + (d, 2 * FF_PAD), BF16), f"w_down{tag}": sds(lead_dims + (FF_PAD, d), BF16)})
        items_w += [(f"gate{tag}", whole, f"w_gu{tag}", cols(FF_SHARD_PAD), group),
                    (f"up{tag}", whole, f"w_gu{tag}", cols(FF_SHARD_PAD, FF_PAD), group),
                    (f"down{tag}", whole, f"w_down{tag}", (rows3 if lead_dims else rows2)(FF_SHARD_PAD), group)]
    got = exchange(srcs_w, lands_w, items_w, {0}, name="gather_first")
    of_group = lambda items, g: [it for it in items if it[4] == g]
    names = lambda items, k: list(dict.fromkeys(it[k] for it in items))
    pending, started = {}, []
    for g in (1, 2):
        its = of_group(items_w, g)
        srcs = {n: srcs_w[n] for n in names(its, 0)}
        sems, lands, token = start_copies(srcs, {n: got[n] for n in names(its, 2)}, its, name=f"gather_start_{g}")
        pending[g] = (sems, srcs, lands, its)
        started.append(token)

    unpad = lambda p, shard, shard_pad: p.reshape(d, N_DEV, shard_pad)[:, :, :shard].reshape(d, N_DEV * shard)

    def get_w(stage, after):
        g = {"ffn1": 1, "layer1": 2}[stage]
        sems, srcs, lands, its = pending[g]
        landed = wait_copies(sems, srcs, lands, its, after, name=f"gather_wait_{g}")
        if g == 1:
            return {"w_gu1": (landed["w_gu1"], None), "w_down1": (landed["w_down1"], None)}
        return {"w_gu2": (landed["w_gu23"], 0), "w_gu3": (landed["w_gu23"], 1), "w_down2": (landed["w_down23"], 0),
                "w_down3": (landed["w_down23"], 1), "odd_w_out": landed["oout"],
                "odd_w_in": unpad(landed["oin"], ODD_IN_SHARD, ODD_IN_SHARD_PAD)}

    full = {n: w[n] for n in REPL}
    for n, stacked in zip(SMALL, _unpack(got["small"], [w[n].shape for n in SMALL])):
        full[n] = _join_shards(stacked, SHARD_AXIS[n])
    full.update(w_gu0=(got["w_gu0"], None), w_down0=(got["w_down0"], None), even_w_out=got["eout"],
                even_w_in=unpad(got["ein"], EVEN_IN_SHARD, EVEN_IN_SHARD_PAD))

    repad = lambda g, shard, shard_pad: pad_cols(g.reshape(d, N_DEV, shard), shard_pad).reshape(d, N_DEV * shard_pad)
    first = lambda ref, i: ref.at[i, 0]
    items_g = [("oin", cols(ODD_IN_SHARD_PAD), "r_oin", first, "layer1"), ("oout", rows2(OUT_SHARD), "r_oout", first, "layer1"),
               ("ein", cols(EVEN_IN_SHARD_PAD), "r_ein", first, "last"), ("eout", rows2(OUT_SHARD), "r_eout", first, "last"),
               ("small", lead, "r_small", first, "last")]
    for i, group in ((3, "ffn3"), (2, "layer1"), (1, "ffn1"), (0, "last")):
        items_g += [(f"g_gu{i}", cols(FF_SHARD_PAD), "r_gu", lambda ref, j, i=i: ref.at[j, i, :, pl.ds(0, FF_SHARD_PAD)], group),
                    (f"g_gu{i}", cols(FF_SHARD_PAD, FF_PAD), "r_gu",
                     lambda ref, j, i=i: ref.at[j, i, :, pl.ds(FF_SHARD_PAD, FF_SHARD_PAD)], group),
                    (f"g_down{i}", rows2(FF_SHARD_PAD), "r_down", lambda ref, j, i=i: ref.at[j, i], group)]
    r_lands = {"r_gu": lax.empty((N_DEV, N_FFN, d, 2 * FF_SHARD_PAD), BF16), "r_down": lax.empty((N_DEV, N_FFN, FF_SHARD_PAD, d), BF16),
               "r_oin": lax.empty((N_DEV, 1, d, ODD_IN_SHARD_PAD), BF16), "r_oout": lax.empty((N_DEV, 1, OUT_SHARD, d), BF16)}
    srcs_g, sent = {}, {}

    def grad_src(n, grads):
        if n == "oin":
            return repad(grads["odd_w_in"], ODD_IN_SHARD, ODD_IN_SHARD_PAD).astype(BF16)
        if n == "ein":
            return repad(grads["even_w_in"], EVEN_IN_SHARD, EVEN_IN_SHARD_PAD).astype(BF16)
        if n in ("oout", "eout"):
            return grads["odd_w_out" if n == "oout" else "even_w_out"].astype(BF16)
        return grads[n]

    def put_g(stage, grads):
        its = of_group(items_g, stage)
        srcs = {n: grad_src(n, grads) for n in names(its, 0)}
        sems, lands, token = start_copies(srcs, {n: r_lands[n] for n in names(its, 2)}, its, name=f"grads_start_{stage}")
        r_lands.update(lands)
        srcs_g.update(srcs)
        sent[stage] = (sems, srcs, its)
        return token

    tied = started[0][0, 0] + started[1][0, 0]
    loss, grad_x, grads = local_step(x[0] + tied, loss_target[0], full, get_w, put_g)
    loss = lax.psum(loss, AXES)

    order = SMALL + REPL
    pieces = [_split_shards(grads[n].reshape(full[n].shape), SHARD_AXIS[n]) if n in SHARD_AXIS
              else jnp.broadcast_to(grads[n].reshape(w[n].shape)[None], (N_DEV,) + w[n].shape) for n in order]
    flat = jnp.concatenate([p.reshape(N_DEV, -1) for p in pieces], axis=1)
    srows = _flat_rows(flat.shape[1], 8)
    grads["small"] = jnp.pad(flat, ((0, 0), (0, srows * FLAT_COLS - flat.shape[1]))).reshape(N_DEV, srows, FLAT_COLS)
    for stage, (sems, srcs, its) in sent.items():
        r_lands.update(wait_copies(sems, srcs, {n: r_lands[n] for n in names(its, 2)}, its, grads["g_down0"],
                                   name=f"grads_wait_{stage}"))
    srcs_g.update({n: grad_src(n, grads) for n in names(of_group(items_g, "last"), 0)})
    r_lands.update(r_ein=sds((N_DEV, 1, d, EVEN_IN_SHARD_PAD), BF16), r_eout=sds((N_DEV, 1, OUT_SHARD, d), BF16),
                   r_small=sds((N_DEV, 1, srows, FLAT_COLS), F32))
    recv = exchange(srcs_g, r_lands, items_g, {"last"}, name="exchange_last")
    r_gu, r_down, r_ein, r_oin, r_eout, r_oout, r_small = (
        recv[n] for n in ("r_gu", "r_down", "r_ein", "r_oin", "r_eout", "r_oout", "r_small"))

    result = [{} for _ in range(4)]

    def update(names, recv, col0, view, back):
        for n in names:
            outs = sum_adamw(recv, col0, view(w[n]), view(m[n]), view(v[n]), name=f"adamw_{n}")
            for r, o in zip(result, outs):
                r[n] = back(o, n)

    as_given = lambda o, n: o.reshape(w[n].shape)
    update(["ffn_w_gate"], r_gu, 0, lambda a: a.reshape(N_FFN, d, FF_SHARD), as_given)
    update(["ffn_w_up"], r_gu, FF_SHARD_PAD, lambda a: a.reshape(N_FFN, d, FF_SHARD), as_given)
    update(["ffn_w_down"], r_down, 0, lambda a: a.reshape(N_FFN, FF_SHARD, d), as_given)
    update(["even_w_in"], r_ein, 0, lambda a: a, as_given)
    update(["odd_w_in"], r_oin, 0, lambda a: a, as_given)
    update(["even_w_out"], r_eout, 0, lambda a: a, as_given)
    update(["odd_w_out"], r_oout, 0, lambda a: a, as_given)
    pack_local = lambda t: _pack([t[n] for n in order], 8, F32)[None]
    small_outs = sum_adamw(r_small, 0, pack_local(w), pack_local(m), pack_local(v), name="adamw_small")
    for r, o in zip(result, small_outs):
        r.update(zip(order, _unpack(o[0], [w[n].shape for n in order])))
    return (loss, grad_x[None], *[r[n] for r in result for n in WEIGHTS])
```

```python
import functools
import math

import numpy as np
import jax
import jax.numpy as jnp
from jax import lax
from jax.experimental import pallas as pl
from jax.experimental.pallas import tpu as pltpu

F32 = jnp.float32
BF16 = jnp.bfloat16
MESH = pl.DeviceIdType.MESH
AXES = ("x", "y", "c")
N_DEV = 8

D_MODEL = 1024
N_META = 16
D_FF = 2816
NORM_EPS = 1e-6
NEG_INF = -1e30
SWA_Q_HEADS = 8
SWA_HEAD_DIM = 64
SWA_WINDOW = 128
SWA_BLOCK = 128
REL_BUCKETS = 32
REL_MAX_DIST = 128
DN_HEADS = 4
DN_HEAD_DIM = 128
DN_CONV = 4
GLA_HEADS = 4
GLA_DK = 128
GLA_DV = 256
GLA_GATE_RANK = 16
GLA_GATE_NORM = 16.0
CHUNK = 64
PAD = SWA_BLOCK - N_META
LANE = 128
PROJ_DIM = 3200

ADAM_LR = 0.001
ADAM_B1 = 0.9
ADAM_B2 = 0.999
ADAM_EPS = 1e-08
ADAM_WD = 0.01
ADAM_STEP = 10

FF_SHARD = D_FF // N_DEV
FF_SHARD_PAD = 384
FF_PAD = N_DEV * FF_SHARD_PAD
N_FFN = 4
EVEN_IN_SHARD, EVEN_IN_SHARD_PAD = 353, 384
ODD_IN_SHARD, ODD_IN_SHARD_PAD = 386, 512
OUT_SHARD = D_MODEL // N_DEV

FLAT_COLS = 128
BIG = ("ffn_w_gate", "ffn_w_up", "ffn_w_down", "even_w_in", "even_w_out", "odd_w_in", "odd_w_out")
SMALL = ("meta_tokens", "norm_w", "even_conv_w", "gla_w_gate_up", "gla_b_gate", "gla_norm_w")
REPL = ("rel_bias_table", "swa_sinks", "dn_a_log", "dn_dt_bias", "dn_norm_w")
WEIGHTS = ("meta_tokens", "norm_w", "ffn_w_gate", "ffn_w_up", "ffn_w_down", "rel_bias_table", "even_w_in",
           "even_conv_w", "swa_sinks", "dn_a_log", "dn_dt_bias", "dn_norm_w", "even_w_out", "odd_w_in",
           "gla_w_gate_up", "gla_b_gate", "gla_norm_w", "odd_w_out")
SHARD_AXIS = {"ffn_w_gate": 3, "ffn_w_up": 3, "ffn_w_down": 2, "even_w_in": 2, "even_w_out": 1, "odd_w_in": 2,
              "odd_w_out": 1, "meta_tokens": 1, "norm_w": 2, "even_conv_w": 2, "gla_w_gate_up": 2,
              "gla_b_gate": 1, "gla_norm_w": 1}


def _rms(x, w):
    r = lax.rsqrt(jnp.mean(x * x, axis=-1, keepdims=True) + NORM_EPS)
    return x * r * w


def _sigmoid(x):
    return 0.5 * (jnp.tanh(0.5 * x) + 1.0)


def _silu(x):
    return x * _sigmoid(x)


def _softplus(x):
    pos = x > 0
    return jnp.where(pos, x, 0.0) + jnp.log(1.0 + jnp.exp(jnp.where(pos, -x, x)))


def _l2n(x):
    return x * lax.rsqrt(jnp.sum(x * x, axis=-1, keepdims=True) + 1e-6)


def _split_bf16(x):
    hi = x.astype(BF16)
    return hi, (x - hi.astype(F32)).astype(BF16)


def _make_mm(terms, batched):
    off = 1 if batched else 0
    bdims = ((0,), (0,)) if batched else ((), ())

    def dg(a, b, ca, cb):
        dot = lambda p, q: lax.dot_general(p, q, (((ca + off,), (cb + off,)), bdims), preferred_element_type=F32)
        a_hi, a_lo = _split_bf16(a)
        b_hi, b_lo = _split_bf16(b)
        if terms == 1:
            return dot(a_hi, b_hi)
        return dot(a_hi, b_hi) + (dot(a_hi, b_lo) + dot(a_lo, b_hi))

    @jax.custom_vjp
    def nn(a, b):
        return dg(a, b, 1, 0)

    @jax.custom_vjp
    def nt(a, b):
        return dg(a, b, 1, 1)

    @jax.custom_vjp
    def tn(a, b):
        return dg(a, b, 0, 0)

    nn.defvjp(lambda a, b: (nn(a, b), (a, b)), lambda r, g: (nt(g, r[1]), tn(r[0], g)))
    nt.defvjp(lambda a, b: (nt(a, b), (a, b)), lambda r, g: (nn(g, r[1]), tn(g, r[0])))
    tn.defvjp(lambda a, b: (tn(a, b), (a, b)), lambda r, g: (nt(r[1], g), nn(r[0], g)))
    return nn, nt, tn


_mm, _mm_nt, _mm_tn = _make_mm(1, False)
_mm3, _, _ = _make_mm(3, False)
_bmm, _bmm_nt, _bmm_tn = _make_mm(1, True)
_bmm3, _bmm3_nt, _bmm3_tn = _make_mm(3, True)


@jax.custom_vjp
def _known_inverse(a, inv):
    return inv


_known_inverse.defvjp(lambda a, inv: (inv, inv),
                      lambda inv, g: (-_bmm3_tn(inv, _bmm3_nt(g, inv)), jnp.zeros_like(inv)))


def _tri_ones_dot(x, lower):
    n = x.shape[0]
    r = lax.broadcasted_iota(jnp.int32, (n, n), 0)
    c = lax.broadcasted_iota(jnp.int32, (n, n), 1)
    t = ((r >= c) if lower else (r <= c)).astype(BF16)
    hi, lo = _split_bf16(x)
    return jnp.dot(t, hi, preferred_element_type=F32) + jnp.dot(t, lo, preferred_element_type=F32)


@jax.custom_vjp
def _cumsum_rows(x):
    return _tri_ones_dot(x, True)


_cumsum_rows.defvjp(lambda x: (_tri_ones_dot(x, True), None), lambda _, g: (_tri_ones_dot(g, False),))


def _row_tile(n_rows, cap):
    best = LANE
    for t in range(LANE, cap + 1, LANE):
        if n_rows % t == 0:
            best = t
    return best


def _real_rows(tile_index, tm):
    row = tile_index * tm + lax.broadcasted_iota(jnp.int32, (tm, 1), 0)
    return (row >= PAD).astype(F32)


def _full(shape):
    return pl.BlockSpec(shape, lambda *_: (0,) * len(shape))


def _resident(shape):
    return pl.BlockSpec(shape, lambda *_: (0,) * len(shape), pipeline_mode=pl.Buffered(1))


def _resident_w(wmat, widx):
    if wmat.ndim == 2:
        return _resident(wmat.shape)
    return pl.BlockSpec((None,) + wmat.shape[1:], lambda *_: (widx, 0, 0), pipeline_mode=pl.Buffered(1))


def rms_mm(h, w, wmat, *, swiglu, name, widx=None):
    tp, d = h.shape
    n = wmat.shape[-1]
    tm = _row_tile(tp, 384)
    half = n // 2

    def body(h_ref, w_ref, wm_ref, hn_ref, *outs):
        hn = _rms(h_ref[...], w_ref[...]).astype(BF16)
        hn_ref[...] = hn
        p = jnp.dot(hn, wm_ref[...], preferred_element_type=F32)
        if swiglu:
            g, u = p[:, :half], p[:, half:]
            outs[0][...] = g.astype(BF16)
            outs[1][...] = u.astype(BF16)
            outs[2][...] = (_silu(g) * u).astype(BF16)
        else:
            outs[0][...] = p

    row = lambda width: pl.BlockSpec((tm, width), lambda i: (i, 0))
    if swiglu:
        out_shape = (jax.ShapeDtypeStruct((tp, d), BF16),) + (jax.ShapeDtypeStruct((tp, half), BF16),) * 3
        out_specs = (row(d), row(half), row(half), row(half))
    else:
        out_shape = (jax.ShapeDtypeStruct((tp, d), BF16), jax.ShapeDtypeStruct((tp, n), F32))
        out_specs = (row(d), row(n))
    return pl.pallas_call(
        body, name=name, grid=(tp // tm,),
        in_specs=[row(d), _full((1, d)), _resident_w(wmat, widx)],
        out_specs=out_specs, out_shape=out_shape,
    )(h, w, wmat)


def mm_rms_res(acts, wmat, h, w, *, scale, name, widx=None):
    tp, d = h.shape
    tm = _row_tile(tp, 384)
    widths = [a.shape[1] for a in acts]
    offs = [sum(widths[:i]) for i in range(len(acts))]
    na = len(acts)

    def body(*refs):
        a_refs = refs[:na]
        wm_ref, h_ref, w_ref, f_ref, ho_ref = refs[na:]
        f = None
        for a_ref, off, width in zip(a_refs, offs, widths):
            part = jnp.dot(a_ref[...].astype(BF16), wm_ref[off:off + width, :], preferred_element_type=F32)
            f = part if f is None else f + part
        f_ref[...] = f
        ho_ref[...] = h_ref[...] + scale * _rms(f, w_ref[...])

    row = lambda width: pl.BlockSpec((tm, width), lambda i: (i, 0))
    return pl.pallas_call(
        body, name=name, grid=(tp // tm,),
        in_specs=[row(wd) for wd in widths] + [_resident_w(wmat, widx), row(d), _full((1, d))],
        out_specs=(row(d), row(d)),
        out_shape=(jax.ShapeDtypeStruct((tp, d), F32), jax.ShapeDtypeStruct((tp, d), F32)),
    )(*acts, wmat, h, w)


def mm_rms_res_bwd(dho, f, w, wmat, gu, *, scale, name, widx=None):
    tp, d = f.shape
    k = wmat.shape[-2]
    tm = _row_tile(tp, 384)
    swiglu = gu is not None

    def body(*refs):
        if swiglu:
            dho_ref, f_ref, w_ref, wm_ref, g_ref, u_ref, df_ref, dw_ref, dgu_ref = refs
        else:
            dho_ref, f_ref, w_ref, wm_ref, df_ref, dw_ref, da_ref = refs
        i = pl.program_id(0)
        _, vjp = jax.vjp(lambda ff, ww: scale * _rms(ff, ww), f_ref[...], w_ref[...])
        df, dw = vjp(dho_ref[...])
        dfb = (df * _real_rows(i, tm)).astype(BF16)
        df_ref[...] = dfb

        @pl.when(i == 0)
        def _():
            dw_ref[...] = jnp.zeros_like(dw_ref)

        dw_ref[...] += dw
        da = lax.dot_general(dfb, wm_ref[...], (((1,), (1,)), ((), ())), preferred_element_type=F32)
        if swiglu:
            g = g_ref[...].astype(F32)
            u = u_ref[...].astype(F32)
            s = _sigmoid(g)
            dgu_ref[:, :k] = (da * u * s * (1.0 + g * (1.0 - s))).astype(BF16)
            dgu_ref[:, k:] = (da * g * s).astype(BF16)
        else:
            da_ref[...] = da

    row = lambda width: pl.BlockSpec((tm, width), lambda i: (i, 0))
    in_specs = [row(d), row(d), _full((1, d)), _resident_w(wmat, widx)]
    args = [dho, f, w, wmat]
    out_shape = [jax.ShapeDtypeStruct((tp, d), BF16), jax.ShapeDtypeStruct((1, d), F32)]
    out_specs = [row(d), _full((1, d))]
    if swiglu:
        in_specs += [row(k), row(k)]
        args += list(gu)
        out_shape += [jax.ShapeDtypeStruct((tp, 2 * k), BF16)]
        out_specs += [row(2 * k)]
    else:
        out_shape += [jax.ShapeDtypeStruct((tp, k), F32)]
        out_specs += [row(k)]
    return pl.pallas_call(body, name=name, grid=(tp // tm,), in_specs=in_specs, out_specs=tuple(out_specs),
                          out_shape=tuple(out_shape))(*args)


def rms_mm_bwd(dps, wmat, h, w, dho, *, name, widx=None):
    tp, d = h.shape
    tm = _row_tile(tp, 384)
    widths = [p.shape[1] for p in dps]
    offs = [sum(widths[:i]) for i in range(len(dps))]
    ndp = len(dps)

    def body(*refs):
        dp_refs = refs[:ndp]
        wm_ref, h_ref, w_ref, dho_ref, dh_ref, dw_ref = refs[ndp:]
        i = pl.program_id(0)
        dhn = None
        for dp_ref, off, width in zip(dp_refs, offs, widths):
            part = lax.dot_general(dp_ref[...].astype(BF16), wm_ref[:, off:off + width], (((1,), (1,)), ((), ())),
                                   preferred_element_type=F32)
            dhn = part if dhn is None else dhn + part
        _, vjp = jax.vjp(_rms, h_ref[...], w_ref[...])
        dx, dw = vjp(dhn)
        dh_ref[...] = (dho_ref[...] + dx) * _real_rows(i, tm)

        @pl.when(i == 0)
        def _():
            dw_ref[...] = jnp.zeros_like(dw_ref)

        dw_ref[...] += dw

    row = lambda width: pl.BlockSpec((tm, width), lambda i: (i, 0))
    return pl.pallas_call(
        body, name=name, grid=(tp // tm,),
        in_specs=[row(wd) for wd in widths] + [_resident_w(wmat, widx), row(d), _full((1, d)), row(d)],
        out_specs=(row(d), _full((1, d))),
        out_shape=(jax.ShapeDtypeStruct((tp, d), F32), jax.ShapeDtypeStruct((1, d), F32)),
    )(*dps, wmat, h, w, dho)


def mm_tn(a, b, *, name, out_dtype=F32):
    t, m = a.shape
    n = b.shape[1]
    bm = _row_tile(m, 512)
    bn = _row_tile(n, 1536)
    bk = _row_tile(t, 1408)
    nk = t // bk

    def body(a_ref, b_ref, o_ref, acc):
        @pl.when(pl.program_id(2) == 0)
        def _():
            acc[...] = jnp.zeros_like(acc)

        acc[...] += lax.dot_general(a_ref[...].astype(BF16), b_ref[...].astype(BF16), (((0,), (0,)), ((), ())),
                                    preferred_element_type=F32)

        @pl.when(pl.program_id(2) == nk - 1)
        def _():
            o_ref[...] = acc[...].astype(o_ref.dtype)

    return pl.pallas_call(
        body, name=name, grid=(m // bm, n // bn, nk),
        in_specs=[pl.BlockSpec((bk, bm), lambda i, j, kk: (kk, i)), pl.BlockSpec((bk, bn), lambda i, j, kk: (kk, j))],
        out_specs=pl.BlockSpec((bm, bn), lambda i, j, kk: (i, j)),
        out_shape=jax.ShapeDtypeStruct((m, n), out_dtype), scratch_shapes=[pltpu.VMEM((bm, bn), F32)],
    )(a, b)


def loss_and_grad(h, target, *, name):
    tp, d = h.shape
    tm = SWA_BLOCK

    def body(h_ref, t_ref, dh_ref, loss_ref):
        i = pl.program_id(0)

        @pl.when(i == 0)
        def _():
            loss_ref[...] = jnp.zeros_like(loss_ref)
            dh_ref[...] = jnp.zeros_like(dh_ref)

        @pl.when(i > 0)
        def _():
            err = h_ref[...] - t_ref[...]
            dh_ref[...] = err * (1.0 / d)
            loss_ref[...] += 0.5 * jnp.sum(jnp.sum(err * err, axis=1, keepdims=True), axis=0, keepdims=True) * (1.0 / d)

    return pl.pallas_call(
        body, name=name, grid=(tp // tm,),
        in_specs=[pl.BlockSpec((tm, d), lambda i: (i, 0)), pl.BlockSpec((tm, d), lambda i: (jnp.maximum(i - 1, 0), 0))],
        out_specs=(pl.BlockSpec((tm, d), lambda i: (i, 0)), _full((1, 1))),
        out_shape=(jax.ShapeDtypeStruct((tp, d), F32), jax.ShapeDtypeStruct((1, 1), F32)),
    )(h, target)


def _t5_bucket_np(rel):
    n = np.maximum(rel, 0)
    max_exact = REL_BUCKETS // 2
    n_f = np.maximum(n, 1).astype(np.float32)
    large = max_exact + (np.log(n_f / np.float32(max_exact)) / np.float32(math.log(REL_MAX_DIST / max_exact))
                         * np.float32(REL_BUCKETS - max_exact)).astype(np.int32)
    large = np.minimum(large, REL_BUCKETS - 1)
    return np.where(n < max_exact, n, large).astype(np.int32)


def _swa_positions_np(n):
    i = np.arange(SWA_BLOCK)[:, None]
    j = np.arange(3 * SWA_BLOCK)[None, :]
    pos_q = n * SWA_BLOCK + i - PAD
    pos_k = np.where(j < SWA_BLOCK, j - PAD, (n - 1) * SWA_BLOCK + (j - SWA_BLOCK) - PAD)
    return pos_q, pos_k


def _swa_buckets():
    out = []
    for n in range(3):
        pos_q, pos_k = _swa_positions_np(n)
        out.append(_t5_bucket_np(pos_q - pos_k))
    return jnp.asarray(np.stack(out))


def swa_bias(table, buckets, *, name):
    nc, nq, nk = buckets.shape

    def body(tab_ref, bkt_ref, out_ref):
        for c in range(nc):
            bkt = bkt_ref[c]
            for h in range(SWA_Q_HEADS):
                acc = jnp.zeros((nq, nk), F32)
                for b in range(REL_BUCKETS):
                    acc = jnp.where(bkt == b, tab_ref[b, h], acc)
                out_ref[c, h] = acc

    return pl.pallas_call(
        body, name=name,
        in_specs=[pl.BlockSpec(memory_space=pltpu.SMEM), pl.BlockSpec(memory_space=pltpu.VMEM)],
        out_specs=pl.BlockSpec(memory_space=pltpu.VMEM),
        out_shape=jax.ShapeDtypeStruct((nc, SWA_Q_HEADS, nq, nk), F32),
    )(table, buckets)


def swa_bias_bwd(dbias, buckets, *, name):
    nc = buckets.shape[0]

    def body(db_ref, bkt_ref, out_ref):
        lane = lax.broadcasted_iota(jnp.int32, (1, LANE), 1)
        for b in range(REL_BUCKETS):
            row = jnp.zeros((1, LANE), F32)
            for c in range(nc):
                hit = bkt_ref[c] == b
                for h in range(SWA_Q_HEADS):
                    part = jnp.where(hit, db_ref[c, h], 0.0)
                    tot = jnp.sum(jnp.sum(part, axis=1, keepdims=True), axis=0, keepdims=True)
                    row = row + jnp.where(lane == h, tot, 0.0)
            out_ref[b:b + 1, :] = row

    return pl.pallas_call(
        body, name=name,
        in_specs=[pl.BlockSpec(memory_space=pltpu.VMEM), pl.BlockSpec(memory_space=pltpu.VMEM)],
        out_specs=pl.BlockSpec(memory_space=pltpu.VMEM),
        out_shape=jax.ShapeDtypeStruct((REL_BUCKETS, LANE), F32),
    )(dbias, buckets)


def _swa_block(q, kvm, kvp, kvc, bias, sinks, n):
    blk = SWA_BLOCK
    i = lax.broadcasted_iota(jnp.int32, (blk, 3 * blk), 0)
    j = lax.broadcasted_iota(jnp.int32, (blk, 3 * blk), 1)
    pos_q = n * blk + i - PAD
    is_meta = j < blk
    pos_k = jnp.where(is_meta, j - PAD, (n - 1) * blk + (j - blk) - PAD)
    rel = pos_q - pos_k
    valid = ((is_meta & (pos_k >= 0) & (pos_k < N_META) & (rel >= 0))
             | (jnp.logical_not(is_meta) & (pos_k >= N_META) & (rel >= 0) & (rel < SWA_WINDOW)))
    kv = jnp.concatenate([kvm, kvp, kvc], axis=0)
    lane = lax.broadcasted_iota(jnp.int32, (1, LANE), 1)
    halves = ((lane < SWA_HEAD_DIM).astype(F32), (lane >= SWA_HEAD_DIM).astype(F32))
    scale = SWA_HEAD_DIM ** -0.5
    outs = []
    for pair in range(SWA_Q_HEADS // 2):
        qp = q[:, pair * LANE:(pair + 1) * LANE]
        grp = pair // 2
        kg = kv[:, grp * LANE:(grp + 1) * LANE]
        vg = kv[:, (2 + grp) * LANE:(3 + grp) * LANE]
        op = None
        for hh in range(2):
            h = 2 * pair + hh
            s = _mm_nt(qp * halves[hh], kg) * scale + bias[h]
            s = jnp.where(valid, s, NEG_INF)
            sink = jnp.sum(jnp.where(lane == h, sinks, 0.0), axis=1, keepdims=True)
            m = lax.stop_gradient(jnp.maximum(jnp.max(s, axis=1, keepdims=True), sink))
            e = jnp.exp(s - m)
            den = jnp.sum(e, axis=1, keepdims=True) + jnp.exp(sink - m)
            part = _mm(e / den, vg) * halves[hh]
            op = part if op is None else op + part
        outs.append(op)
    return jnp.concatenate(outs, axis=1)


def _swa_in_specs(nb, rev):
    blk = SWA_BLOCK
    step = (lambda i: nb - 1 - i) if rev else (lambda i: i)
    return [
        pl.BlockSpec((blk, 4 * LANE), lambda i: (step(i), 0)),
        pl.BlockSpec((blk, 4 * LANE), lambda i: (0, 1)),
        pl.BlockSpec((blk, 4 * LANE), lambda i: (jnp.maximum(step(i) - 1, 0), 1)),
        pl.BlockSpec((blk, 4 * LANE), lambda i: (step(i), 1)),
        pl.BlockSpec((1, SWA_Q_HEADS, blk, 3 * blk), lambda i: (jnp.minimum(step(i), 2), 0, 0, 0)),
        _full((1, LANE)),
    ]


def swa_fwd(proj, bias, sinks, *, name):
    tp = proj.shape[0]
    nb = tp // SWA_BLOCK

    def body(q_ref, kvm_ref, kvp_ref, kvc_ref, bias_ref, sinks_ref, o_ref):
        n = pl.program_id(0)
        o_ref[...] = _swa_block(q_ref[...], kvm_ref[...], kvp_ref[...], kvc_ref[...], bias_ref[0], sinks_ref[...], n)

    return pl.pallas_call(
        body, name=name, grid=(nb,),
        in_specs=_swa_in_specs(nb, False),
        out_specs=pl.BlockSpec((SWA_BLOCK, 4 * LANE), lambda i: (i, 0)),
        out_shape=jax.ShapeDtypeStruct((tp, 4 * LANE), F32),
    )(proj, proj, proj, proj, bias, sinks)


def swa_bwd(proj, bias, sinks, do, *, name):
    tp = proj.shape[0]
    nb = tp // SWA_BLOCK
    blk = SWA_BLOCK

    def body(q_ref, kvm_ref, kvp_ref, kvc_ref, bias_ref, sinks_ref, do_ref, dq_ref, dkv_ref, dbias_ref, dsinks_ref,
             carry, meta_acc):
        i = pl.program_id(0)
        n = nb - 1 - i

        @pl.when(i == 0)
        def _():
            carry[...] = jnp.zeros_like(carry)
            meta_acc[...] = jnp.zeros_like(meta_acc)
            dsinks_ref[...] = jnp.zeros_like(dsinks_ref)

        fn = lambda q, kvm, kvp, kvc, b, s: _swa_block(q, kvm, kvp, kvc, b, s, n)
        _, vjp = jax.vjp(fn, q_ref[...], kvm_ref[...], kvp_ref[...], kvc_ref[...], bias_ref[0], sinks_ref[...])
        dq, dkvm, dkvp, dkvc, dbias, dsinks = vjp(do_ref[...])
        dq_ref[...] = dq
        meta_acc[...] += dkvm
        dkv_ref[...] = dkvc + carry[...] + jnp.where(n == 0, meta_acc[...], 0.0)
        carry[...] = dkvp
        first_visit = (n == nb - 1) | (n < 2)

        @pl.when(first_visit)
        def _():
            dbias_ref[0] = dbias

        @pl.when(jnp.logical_not(first_visit))
        def _():
            dbias_ref[0] += dbias

        dsinks_ref[...] += dsinks

    rev = lambda i: nb - 1 - i
    return pl.pallas_call(
        body, name=name, grid=(nb,),
        in_specs=_swa_in_specs(nb, True) + [pl.BlockSpec((blk, 4 * LANE), lambda i: (rev(i), 0))],
        out_specs=(pl.BlockSpec((blk, 4 * LANE), lambda i: (rev(i), 0)),
                   pl.BlockSpec((blk, 4 * LANE), lambda i: (rev(i), 0)),
                   pl.BlockSpec((1, SWA_Q_HEADS, blk, 3 * blk), lambda i: (jnp.minimum(rev(i), 2), 0, 0, 0)),
                   _full((1, LANE))),
        out_shape=(jax.ShapeDtypeStruct((tp, 4 * LANE), F32), jax.ShapeDtypeStruct((tp, 4 * LANE), F32),
                   jax.ShapeDtypeStruct((3, SWA_Q_HEADS, blk, 3 * blk), F32), jax.ShapeDtypeStruct((1, LANE), F32)),
        scratch_shapes=[pltpu.VMEM((blk, 4 * LANE), F32), pltpu.VMEM((blk, 4 * LANE), F32)],
    )(proj, proj, proj, proj, bias, sinks, do)


CONV_COL0 = 2
HALO = 8


def conv_fwd(proj, conv_w, *, name):
    tp = proj.shape[0]
    tm = _row_tile(tp, 384)
    cw = 4 * LANE
    ncol = conv_w.shape[1] // cw

    def body(x_ref, halo_ref, w_ref, y_ref, buf):
        i = pl.program_id(1)
        buf[0:HALO, :] = jnp.where(i > 0, halo_ref[...], 0.0)
        buf[HALO:, :] = x_ref[...]
        acc = None
        for j in range(DN_CONV):
            term = w_ref[j:j + 1, :] * buf[pl.ds(HALO - (DN_CONV - 1) + j, tm), :]
            acc = term if acc is None else acc + term
        y_ref[...] = acc

    return pl.pallas_call(
        body, name=name, grid=(ncol, tp // tm),
        in_specs=[pl.BlockSpec((tm, cw), lambda c, i: (i, CONV_COL0 + c)),
                  pl.BlockSpec((HALO, cw), lambda c, i: (jnp.maximum(i * (tm // HALO) - 1, 0), CONV_COL0 + c)),
                  pl.BlockSpec((DN_CONV, cw), lambda c, i: (0, c))],
        out_specs=pl.BlockSpec((tm, cw), lambda c, i: (i, c)),
        out_shape=jax.ShapeDtypeStruct((tp, ncol * cw), F32),
        scratch_shapes=[pltpu.VMEM((tm + HALO, cw), F32)],
    )(proj, proj, conv_w)


def conv_bwd(proj, conv_w, dy, *, name):
    tp = proj.shape[0]
    tm = _row_tile(tp, 384)
    cw = 4 * LANE
    ncol = conv_w.shape[1] // cw
    nt = tp // tm

    def body(x_ref, xhalo_ref, w_ref, dy_ref, dyhalo_ref, dx_ref, dw_ref, xbuf, dbuf):
        i = pl.program_id(1)
        xbuf[0:HALO, :] = jnp.where(i > 0, xhalo_ref[...], 0.0)
        xbuf[HALO:, :] = x_ref[...]
        dbuf[0:tm, :] = dy_ref[...]
        dbuf[tm:, :] = jnp.where(i < nt - 1, dyhalo_ref[...], 0.0)
        dy_t = dy_ref[...]
        acc = None
        rows = []
        for j in range(DN_CONV):
            term = w_ref[j:j + 1, :] * dbuf[pl.ds(DN_CONV - 1 - j, tm), :]
            acc = term if acc is None else acc + term
            rows.append(jnp.sum(dy_t * xbuf[pl.ds(HALO - (DN_CONV - 1) + j, tm), :], axis=0, keepdims=True))
        dx_ref[...] = acc

        @pl.when(i == 0)
        def _():
            dw_ref[...] = jnp.zeros_like(dw_ref)

        for j in range(DN_CONV):
            dw_ref[j:j + 1, :] += rows[j]

    return pl.pallas_call(
        body, name=name, grid=(ncol, nt),
        in_specs=[pl.BlockSpec((tm, cw), lambda c, i: (i, CONV_COL0 + c)),
                  pl.BlockSpec((HALO, cw), lambda c, i: (jnp.maximum(i * (tm // HALO) - 1, 0), CONV_COL0 + c)),
                  pl.BlockSpec((DN_CONV, cw), lambda c, i: (0, c)),
                  pl.BlockSpec((tm, cw), lambda c, i: (i, c)),
                  pl.BlockSpec((HALO, cw), lambda c, i: (jnp.minimum((i + 1) * (tm // HALO), tp // HALO - 1), c))],
        out_specs=(pl.BlockSpec((tm, cw), lambda c, i: (i, c)), pl.BlockSpec((DN_CONV, cw), lambda c, i: (0, c))),
        out_shape=(jax.ShapeDtypeStruct((tp, ncol * cw), F32), jax.ShapeDtypeStruct((DN_CONV, ncol * cw), F32)),
        scratch_shapes=[pltpu.VMEM((tm + HALO, cw), F32), pltpu.VMEM((tm + HALO, cw), F32)],
    )(proj, proj, conv_w, dy, dy)


def _stack(parts):
    return jnp.concatenate([p[None] for p in parts], axis=0)


def _chunk_masks():
    r = lax.broadcasted_iota(jnp.int32, (CHUNK, CHUNK), 0)
    c = lax.broadcasted_iota(jnp.int32, (CHUNK, CHUNK), 1)
    return (r >= c).astype(F32), (r > c).astype(F32), (r == c).astype(F32)


def _dn_chunk(y, z, small, s, a_log, dt_bias, norm_w, rows, known_inv=None):
    tri_incl, tri_strict, eye = _chunk_masks()
    lane = lax.broadcasted_iota(jnp.int32, (1, LANE), 1)
    dk = DN_HEAD_DIM
    nh = DN_HEADS
    heads = lambda t, first: _stack([t[:, (first + h) * dk:(first + h + 1) * dk] for h in range(nh)])
    pick = lambda t, l: jnp.sum(jnp.where(lane == l, t, 0.0), axis=1, keepdims=True)
    q = _l2n(_silu(heads(y, 0))) * dk ** -0.5
    k = _l2n(_silu(heads(y, nh)))
    v = _silu(heads(y, 2 * nh))
    g_all = jnp.where(lane < nh, -jnp.exp(a_log) * _softplus(small + dt_bias), 0.0) * rows
    beta_all = _sigmoid(small)
    gc_all = _cumsum_rows(g_all)
    g_sum = jnp.sum(g_all, axis=0, keepdims=True)
    gc = _stack([pick(gc_all, h) for h in range(nh)])
    beta = _stack([pick(beta_all, nh + h) for h in range(nh)])
    g_last = _stack([pick(g_sum, h) for h in range(nh)])
    gc_row = jnp.sum(eye * gc, axis=1, keepdims=True)
    gamma = jnp.exp((gc - gc_row) * tri_incl) * tri_incl
    k_beta = k * beta
    v_beta = v * beta
    a = _bmm_nt(k_beta, k) * gamma * tri_strict
    if known_inv is None:
        inv = eye - a
        power = a
        for _ in range(5):
            power = _bmm3(power, power)
            inv = inv + _bmm3(inv, power)
    else:
        inv = _known_inverse(a, known_inv)
    e_gc = jnp.exp(gc)
    uw = _bmm3(inv, jnp.concatenate([v_beta, k_beta * e_gc], axis=2))
    u, w = uw[:, :, :dk], uw[:, :, dk:]
    attn = _bmm_nt(q, k) * gamma
    q_dec = q * e_gc
    k_dec = k * jnp.exp(g_last - gc)
    v_new = u - _bmm(w, s)
    o = _bmm(q_dec, s) + _bmm(attn, v_new)
    s_new = s * jnp.exp(g_last) + _bmm_tn(k_dec, v_new)
    out = _rms(o, norm_w) * _silu(heads(z, 0))
    return jnp.concatenate([out[h] for h in range(nh)], axis=1), s_new, inv


Z_COL = 5
SMALL_COL = 24


def _chunk_rows(n):
    row = n * CHUNK + lax.broadcasted_iota(jnp.int32, (CHUNK, 1), 0)
    return (row >= PAD).astype(F32)


def dn_fwd(y, proj, a_log, dt_bias, norm_w, *, name):
    tp = y.shape[0]
    nc = tp // CHUNK
    dk = DN_HEAD_DIM

    def body(y_ref, z_ref, small_ref, al_ref, dt_ref, nw_ref, o_ref, ssave_ref, isave_ref, state):
        n = pl.program_id(0)

        @pl.when(n == 0)
        def _():
            state[...] = jnp.zeros_like(state)

        ssave_ref[0] = state[...]
        out, s_new, inv = _dn_chunk(y_ref[...], z_ref[...], small_ref[...], state[...], al_ref[...], dt_ref[...],
                                    nw_ref[...], _chunk_rows(n))
        o_ref[...] = out
        isave_ref[0] = inv
        state[...] = s_new

    return pl.pallas_call(
        body, name=name, grid=(nc,),
        in_specs=[pl.BlockSpec((CHUNK, y.shape[1]), lambda n: (n, 0)),
                  pl.BlockSpec((CHUNK, 4 * LANE), lambda n: (n, Z_COL)),
                  pl.BlockSpec((CHUNK, LANE), lambda n: (n, SMALL_COL)),
                  _full((1, LANE)), _full((1, LANE)), _full((1, LANE))],
        out_specs=(pl.BlockSpec((CHUNK, 4 * LANE), lambda n: (n, 0)),
                   pl.BlockSpec((1, DN_HEADS, dk, dk), lambda n: (n, 0, 0, 0)),
                   pl.BlockSpec((1, DN_HEADS, CHUNK, CHUNK), lambda n: (n, 0, 0, 0))),
        out_shape=(jax.ShapeDtypeStruct((tp, 4 * LANE), F32), jax.ShapeDtypeStruct((nc, DN_HEADS, dk, dk), F32),
                   jax.ShapeDtypeStruct((nc, DN_HEADS, CHUNK, CHUNK), F32)),
        scratch_shapes=[pltpu.VMEM((DN_HEADS, dk, dk), F32)],
    )(y, proj, proj, a_log, dt_bias, norm_w)


def dn_bwd(y, proj, a_log, dt_bias, norm_w, ssave, isave, do, *, name):
    tp = y.shape[0]
    nc = tp // CHUNK
    dk = DN_HEAD_DIM
    rev = lambda i: nc - 1 - i

    def body(y_ref, z_ref, small_ref, al_ref, dt_ref, nw_ref, ss_ref, is_ref, do_ref,
             dy_ref, dz_ref, dsmall_ref, dal_ref, ddt_ref, dnw_ref, dstate):
        i = pl.program_id(0)
        n = nc - 1 - i

        @pl.when(i == 0)
        def _():
            dstate[...] = jnp.zeros_like(dstate)
            dal_ref[...] = jnp.zeros_like(dal_ref)
            ddt_ref[...] = jnp.zeros_like(ddt_ref)
            dnw_ref[...] = jnp.zeros_like(dnw_ref)

        rows = _chunk_rows(n)
        known_inv = is_ref[0]
        fn = lambda *a: _dn_chunk(*a, rows, known_inv)[:2]
        _, vjp = jax.vjp(fn, y_ref[...], z_ref[...], small_ref[...], ss_ref[0], al_ref[...], dt_ref[...], nw_ref[...])
        dy, dz, dsmall, ds, dal, ddt, dnw = vjp((do_ref[...], dstate[...]))
        dy_ref[...] = dy
        dz_ref[...] = dz
        dsmall_ref[...] = dsmall
        dstate[...] = ds
        dal_ref[...] += dal
        ddt_ref[...] += ddt
        dnw_ref[...] += dnw

    return pl.pallas_call(
        body, name=name, grid=(nc,),
        in_specs=[pl.BlockSpec((CHUNK, y.shape[1]), lambda i: (rev(i), 0)),
                  pl.BlockSpec((CHUNK, 4 * LANE), lambda i: (rev(i), Z_COL)),
                  pl.BlockSpec((CHUNK, LANE), lambda i: (rev(i), SMALL_COL)),
                  _full((1, LANE)), _full((1, LANE)), _full((1, LANE)),
                  pl.BlockSpec((1, DN_HEADS, dk, dk), lambda i: (rev(i), 0, 0, 0)),
                  pl.BlockSpec((1, DN_HEADS, CHUNK, CHUNK), lambda i: (rev(i), 0, 0, 0)),
                  pl.BlockSpec((CHUNK, 4 * LANE), lambda i: (rev(i), 1))],
        out_specs=(pl.BlockSpec((CHUNK, y.shape[1]), lambda i: (rev(i), 0)),
                   pl.BlockSpec((CHUNK, 4 * LANE), lambda i: (rev(i), 0)),
                   pl.BlockSpec((CHUNK, LANE), lambda i: (rev(i), 0)),
                   _full((1, LANE)), _full((1, LANE)), _full((1, LANE))),
        out_shape=(jax.ShapeDtypeStruct((tp, y.shape[1]), F32), jax.ShapeDtypeStruct((tp, 4 * LANE), F32),
                   jax.ShapeDtypeStruct((tp, LANE), F32), jax.ShapeDtypeStruct((1, LANE), F32),
                   jax.ShapeDtypeStruct((1, LANE), F32), jax.ShapeDtypeStruct((1, LANE), F32)),
        scratch_shapes=[pltpu.VMEM((DN_HEADS, dk, dk), F32)],
    )(y, proj, proj, a_log, dt_bias, norm_w, ssave, isave, do)


def _gla_chunk(q, k, v, gate, low, s, w_gate_up, b_gate, norm_w, rows):
    tri_incl, _, _ = _chunk_masks()
    dk, dv, nh = GLA_DK, GLA_DV, GLA_HEADS
    heads = lambda t, width: _stack([t[:, h * width:(h + 1) * width] for h in range(nh)])
    logit = _mm3(low, w_gate_up) + b_gate
    glog_all = -_softplus(-logit) * (1.0 / GLA_GATE_NORM) * rows
    glog = heads(glog_all, dk)
    bcum = heads(_cumsum_rows(glog_all), dk)
    qh = heads(q, dk) * dk ** -0.5
    kh = heads(k, dk)
    vh = heads(v, dv)
    q_dec = qh * jnp.exp(bcum)
    attn = _bmm_nt(q_dec, kh * jnp.exp(-bcum)) * tri_incl
    b_last = jnp.sum(glog, axis=1, keepdims=True)
    k_dec = kh * jnp.exp(b_last - bcum)
    r = lax.broadcasted_iota(jnp.int32, (dk, dk), 0)
    c = lax.broadcasted_iota(jnp.int32, (dk, dk), 1)
    b_last_col = jnp.sum((r == c).astype(F32) * b_last, axis=2, keepdims=True)
    o = _bmm(attn, vh) + _bmm(q_dec, s)
    s_new = s * jnp.exp(b_last_col) + _bmm_tn(k_dec, vh)
    out = _rms(o, norm_w) * _silu(heads(gate, dv))
    return jnp.concatenate([out[h] for h in range(nh)], axis=1), s_new


LOW_COL = 24


def _gla_in_specs(step):
    return [pl.BlockSpec((CHUNK, 4 * LANE), lambda i: (step(i), 0)),
            pl.BlockSpec((CHUNK, 4 * LANE), lambda i: (step(i), 1)),
            pl.BlockSpec((CHUNK, 8 * LANE), lambda i: (step(i), 1)),
            pl.BlockSpec((CHUNK, 8 * LANE), lambda i: (step(i), 2)),
            pl.BlockSpec((CHUNK, LANE), lambda i: (step(i), LOW_COL)),
            _full((LANE, 4 * LANE)), _full((1, 4 * LANE)), _full((1, GLA_DV))]


def gla_fwd(proj, w_gate_up, b_gate, norm_w, *, name):
    tp = proj.shape[0]
    nc = tp // CHUNK

    def body(q_ref, k_ref, v_ref, g_ref, low_ref, wgu_ref, bg_ref, nw_ref, o_ref, ssave_ref, state):
        n = pl.program_id(0)

        @pl.when(n == 0)
        def _():
            state[...] = jnp.zeros_like(state)

        ssave_ref[0] = state[...]
        out, s_new = _gla_chunk(q_ref[...], k_ref[...], v_ref[...], g_ref[...], low_ref[...], state[...], wgu_ref[...],
                                bg_ref[...], nw_ref[...], _chunk_rows(n))
        o_ref[...] = out
        state[...] = s_new

    return pl.pallas_call(
        body, name=name, grid=(nc,),
        in_specs=_gla_in_specs(lambda i: i),
        out_specs=(pl.BlockSpec((CHUNK, 8 * LANE), lambda n: (n, 0)),
                   pl.BlockSpec((1, GLA_HEADS, GLA_DK, GLA_DV), lambda n: (n, 0, 0, 0))),
        out_shape=(jax.ShapeDtypeStruct((tp, 8 * LANE), F32),
                   jax.ShapeDtypeStruct((nc, GLA_HEADS, GLA_DK, GLA_DV), F32)),
        scratch_shapes=[pltpu.VMEM((GLA_HEADS, GLA_DK, GLA_DV), F32)],
    )(proj, proj, proj, proj, proj, w_gate_up, b_gate, norm_w)


def gla_bwd(proj, w_gate_up, b_gate, norm_w, ssave, do, *, name):
    tp = proj.shape[0]
    nc = tp // CHUNK
    rev = lambda i: nc - 1 - i

    def body(q_ref, k_ref, v_ref, g_ref, low_ref, wgu_ref, bg_ref, nw_ref, ss_ref, do_ref,
             dq_ref, dk_ref, dv_ref, dg_ref, dlow_ref, dwgu_ref, dbg_ref, dnw_ref, dstate):
        i = pl.program_id(0)
        n = nc - 1 - i

        @pl.when(i == 0)
        def _():
            dstate[...] = jnp.zeros_like(dstate)
            dwgu_ref[...] = jnp.zeros_like(dwgu_ref)
            dbg_ref[...] = jnp.zeros_like(dbg_ref)
            dnw_ref[...] = jnp.zeros_like(dnw_ref)

        rows = _chunk_rows(n)
        fn = lambda *a: _gla_chunk(*a, rows)
        _, vjp = jax.vjp(fn, q_ref[...], k_ref[...], v_ref[...], g_ref[...], low_ref[...], ss_ref[0], wgu_ref[...],
                         bg_ref[...], nw_ref[...])
        dq, dk, dv, dg, dlow, ds, dwgu, dbg, dnw = vjp((do_ref[...], dstate[...]))
        dq_ref[...] = dq
        dk_ref[...] = dk
        dv_ref[...] = dv
        dg_ref[...] = dg
        dlow_ref[...] = dlow
        dstate[...] = ds
        dwgu_ref[...] += dwgu
        dbg_ref[...] += dbg
        dnw_ref[...] += dnw

    chunk = lambda width: pl.BlockSpec((CHUNK, width), lambda i: (rev(i), 0))
    return pl.pallas_call(
        body, name=name, grid=(nc,),
        in_specs=_gla_in_specs(rev) + [pl.BlockSpec((1, GLA_HEADS, GLA_DK, GLA_DV), lambda i: (rev(i), 0, 0, 0)),
                                       chunk(8 * LANE)],
        out_specs=(chunk(4 * LANE), chunk(4 * LANE), chunk(8 * LANE), chunk(8 * LANE), chunk(LANE),
                   _full((LANE, 4 * LANE)), _full((1, 4 * LANE)), _full((1, GLA_DV))),
        out_shape=(jax.ShapeDtypeStruct((tp, 4 * LANE), F32), jax.ShapeDtypeStruct((tp, 4 * LANE), F32),
                   jax.ShapeDtypeStruct((tp, 8 * LANE), F32), jax.ShapeDtypeStruct((tp, 8 * LANE), F32),
                   jax.ShapeDtypeStruct((tp, LANE), F32), jax.ShapeDtypeStruct((LANE, 4 * LANE), F32),
                   jax.ShapeDtypeStruct((1, 4 * LANE), F32), jax.ShapeDtypeStruct((1, GLA_DV), F32)),
        scratch_shapes=[pltpu.VMEM((GLA_HEADS, GLA_DK, GLA_DV), F32)],
    )(proj, proj, proj, proj, proj, w_gate_up, b_gate, norm_w, ssave, do)


def _even_proj_weight(w_in):
    hd = SWA_HEAD_DIM
    k0, k1 = w_in[:, 512:512 + hd], w_in[:, 512 + hd:640]
    v0, v1 = w_in[:, 640:640 + hd], w_in[:, 640 + hd:768]
    zeros = jnp.zeros((w_in.shape[0], LANE - 2 * DN_HEADS), w_in.dtype)
    return jnp.concatenate([w_in[:, :512], k0, k0, k1, k1, v0, v0, v1, v1, w_in[:, 768:2816], w_in[:, 2820:2824],
                            w_in[:, 2816:2820], zeros], axis=1)


def _even_proj_weight_grad(dw):
    hd = SWA_HEAD_DIM
    c = lambda i: dw[:, 512 + i * hd:512 + (i + 1) * hd]
    return jnp.concatenate([dw[:, :512], c(0) + c(1), c(2) + c(3), c(4) + c(5), c(6) + c(7), dw[:, 1024:3072],
                            dw[:, 3076:3080], dw[:, 3072:3076]], axis=1)


def _ffn_fwd(h, nw_in, nw_out, wts, idx, get_w):
    w_gu = wts[f"w_gu{idx}"]
    hn, g, u, a = rms_mm(h, nw_in, w_gu[0], swiglu=True, name=f"ffn_up_{idx}", widx=w_gu[1])
    wts.update(get_w(f"down{idx}", a))
    w_down = wts[f"w_down{idx}"]
    f, h_out = mm_rms_res([a], w_down[0], h, nw_out, scale=0.5, name=f"ffn_down_{idx}", widx=w_down[1])
    return h_out, (h, hn, g, u, a, f)


def _ffn_bwd(dho, saved, nw_in, nw_out, w_gu, w_down, idx, on_grads):
    h, hn, g, u, a, f = saved
    df, dnw_out, dgu = mm_rms_res_bwd(dho, f, nw_out, w_down[0], (g, u), scale=0.5, name=f"ffn_down_bwd_{idx}",
                                      widx=w_down[1])
    g_gu = mm_tn(hn, dgu, name=f"ffn_dwgu_{idx}", out_dtype=BF16)
    g_down = mm_tn(a, df, name=f"ffn_dwd_{idx}", out_dtype=BF16)
    sent = on_grads(g_gu, g_down)
    dh, dnw_in = rms_mm_bwd([dgu], w_gu[0], h, nw_in + sent, dho, name=f"ffn_up_bwd_{idx}", widx=w_gu[1])
    return dh, dnw_in, dnw_out


def local_step(x, target, wts, get_w=None, put_g=None):
    seq, d = x.shape
    wts = dict(wts)
    get_w = get_w or (lambda stage, after: {})
    put_g = put_g or (lambda stage, grads: jnp.zeros((1, 1), F32))
    row = lambda v: v.reshape(1, -1)
    lane_row = lambda v: jnp.pad(v.reshape(1, -1), ((0, 0), (0, LANE - v.size)))
    nw = wts["norm_w"]
    h = jnp.concatenate([jnp.zeros((PAD, d), F32), wts["meta_tokens"], x], axis=0)
    buckets = _swa_buckets()
    bias = swa_bias(wts["rel_bias_table"], buckets, name="swa_bias")
    sinks = lane_row(wts["swa_sinks"])
    a_log, dt_bias = lane_row(wts["dn_a_log"]), lane_row(wts["dn_dt_bias"])
    dn_norm_w = row(wts["dn_norm_w"])
    conv_w = wts["even_conv_w"][0]
    w_gate_up = jnp.pad(wts["gla_w_gate_up"][0], ((0, LANE - GLA_GATE_RANK), (0, 0)))
    b_gate, gla_norm_w = row(wts["gla_b_gate"]), row(wts["gla_norm_w"])

    saved = []
    w_in, w_out = [None, None], [None, None]
    for l in range(2):
        if l == 1:
            wts.update(get_w("layer1", h))
            w_in[1] = jnp.pad(wts["odd_w_in"], ((0, 0), (0, PROJ_DIM - wts["odd_w_in"].shape[1])))
            w_out[1] = wts["odd_w_out"]
        h, s_a = _ffn_fwd(h, row(nw[l, 0]), row(nw[l, 1]), wts, 2 * l, get_w)
        if l == 0:
            wts.update(get_w("even", h))
            w_in[0], w_out[0] = _even_proj_weight(wts["even_w_in"]), wts["even_w_out"]
        h_mix = h
        hn, proj = rms_mm(h, row(nw[l, 2]), w_in[l], swiglu=False, name=f"mix_in_{l}")
        if l == 0:
            o_a = swa_fwd(proj, bias, sinks, name="swa_fwd")
            y = conv_fwd(proj, conv_w, name="conv_fwd")
            o_b, ssave, isave = dn_fwd(y, proj, a_log, dt_bias, dn_norm_w, name="dn_fwd")
            acts, extra = [o_a, o_b], (y, ssave, isave)
        else:
            o, ssave = gla_fwd(proj, w_gate_up, b_gate, gla_norm_w, name="gla_fwd")
            acts, extra = [o], (ssave,)
        mix, h = mm_rms_res(acts, w_out[l], h, row(nw[l, 3]), scale=1.0, name=f"mix_out_{l}")
        s_m = (h_mix, hn, proj, acts, extra, mix)
        if l == 0:
            wts.update(get_w("ffn1", h))
        h, s_b = _ffn_fwd(h, row(nw[l, 4]), row(nw[l, 5]), wts, 2 * l + 1, get_w)
        saved.append((s_a, s_m, s_b))

    dh, loss = loss_and_grad(h, target, name="loss")

    grads = {}
    dnw = [[None] * 6 for _ in range(2)]
    stage_of = {3: "ffn3", 2: "layer1", 1: "ffn1", 0: "ffn0"}

    def on_grads(i):
        def put(g_gu, g_down):
            grads[f"g_gu{i}"], grads[f"g_down{i}"] = g_gu, g_down
            return put_g(stage_of[i], grads)
        return put

    for l in (1, 0):
        s_a, s_m, s_b = saved[l]
        i = 2 * l + 1
        dh, dnw[l][4], dnw[l][5] = _ffn_bwd(dh, s_b, row(nw[l, 4]), row(nw[l, 5]), wts[f"w_gu{i}"], wts[f"w_down{i}"],
                                            i, on_grads(i))
        h_mix, hn, proj, acts, extra, mix = s_m
        dmix, dnw[l][3], do = mm_rms_res_bwd(dh, mix, row(nw[l, 3]), w_out[l], None, scale=1.0, name=f"mix_out_bwd_{l}")
        dw_out = jnp.concatenate([mm_tn(a, dmix, name=f"mix_dwo_{l}_{i}") for i, a in enumerate(acts)], axis=0)
        sent = jnp.zeros((1, 1), F32)
        if l == 0:
            y, ssave, isave = extra
            dq, dkv, dbias, dsinks = swa_bwd(proj, bias, sinks, do, name="swa_bwd")
            dy, dz, dsmall, da_log, ddt_bias, ddn_norm_w = dn_bwd(y, proj, a_log, dt_bias, dn_norm_w, ssave, isave, do,
                                                                    name="dn_bwd")
            dxc, dconv_w = conv_bwd(proj, conv_w, dy, name="conv_bwd")
            dps = [dq, dkv, dxc, dz, dsmall]
            grads["rel_bias_table"] = swa_bias_bwd(dbias, buckets, name="swa_bias_bwd")[:, :SWA_Q_HEADS]
            grads["swa_sinks"] = dsinks[:, :SWA_Q_HEADS]
            grads["dn_a_log"] = da_log[:, :DN_HEADS]
            grads["dn_dt_bias"] = ddt_bias[:, :DN_HEADS]
            grads["dn_norm_w"] = ddn_norm_w
            grads["even_conv_w"] = dconv_w[None]
            grads["even_w_out"] = dw_out
        else:
            (ssave,) = extra
            dq, dk, dv, dgate, dlow, dwgu, dbg, dgnw = gla_bwd(proj, w_gate_up, b_gate, gla_norm_w, ssave, do,
                                                               name="gla_bwd")
            dps = [dq, dk, dv, dgate, dlow]
            grads["gla_w_gate_up"] = dwgu[None, :GLA_GATE_RANK]
            grads["gla_b_gate"] = dbg
            grads["gla_norm_w"] = dgnw
            grads["odd_w_out"] = dw_out
        dw_in = jnp.concatenate([mm_tn(hn, dp, name=f"mix_dwi_{l}_{i}") for i, dp in enumerate(dps)], axis=1)
        if l == 0:
            grads["even_w_in"] = _even_proj_weight_grad(dw_in)
            sent = put_g("even", grads)
        else:
            grads["odd_w_in"] = dw_in[:, :wts["odd_w_in"].shape[1]]
        dh, dnw[l][2] = rms_mm_bwd(dps, w_in[l], h_mix, row(nw[l, 2]) + sent, dh, name=f"mix_in_bwd_{l}")
        i = 2 * l
        dh, dnw[l][0], dnw[l][1] = _ffn_bwd(dh, s_a, row(nw[l, 0]), row(nw[l, 1]), wts[f"w_gu{i}"], wts[f"w_down{i}"],
                                            i, on_grads(i))

    grads["norm_w"] = jnp.stack([jnp.concatenate(r, axis=0) for r in dnw])
    grads["meta_tokens"] = dh[PAD:PAD + N_META]
    return loss[0, 0], dh[PAD + N_META:], grads


def _peer(k):
    x, y, c = (lax.axis_index(a) for a in AXES)
    flip = lambda v, bit: 1 - v if bit else v
    return (flip(x, k & 4), flip(y, k & 2), flip(c, k & 1))


def _my_index():
    x, y, c = (lax.axis_index(a) for a in AXES)
    return 4 * x + 2 * y + c


_HBM = pl.BlockSpec(memory_space=pltpu.HBM)
_SEM = pl.BlockSpec(memory_space=pltpu.SEMAPHORE)
_EFFECT = pltpu.SideEffectType.DATAFLOW_SIDE_EFFECTING


def _remote_copies(items, src_refs, land_refs, send_sems, recv_sems):
    me = _my_index()
    copies = []
    for k in range(1, N_DEV):
        px, py, pc = _peer(k)
        pj = 4 * px + 2 * py + pc
        for a, (sn, send, ln, land, _) in enumerate(items):
            sem = (k - 1) * len(items) + a
            copies.append(pltpu.make_async_remote_copy(
                src_ref=send(src_refs[sn], pj), dst_ref=land(land_refs[ln], me), send_sem=send_sems.at[sem],
                recv_sem=recv_sems.at[sem], device_id=(px, py, pc), device_id_type=MESH))
    return copies


def exchange(srcs, lands, items, remote_groups, *, name):
    sn, ln = list(srcs), list(lands)
    given = [n for n in ln if not isinstance(lands[n], jax.ShapeDtypeStruct)]
    remote = [it for it in items if it[4] in remote_groups]

    def body(*refs):
        src_refs = dict(zip(sn, refs[:len(sn)]))
        land_refs = dict(zip(ln, refs[len(sn) + len(given):len(sn) + len(given) + len(ln)]))
        send_sems, recv_sems, local_sems = refs[len(sn) + len(given) + len(ln):]
        me = _my_index()
        local = [pltpu.make_async_copy(send(src_refs[s], me), land(land_refs[l], me), local_sems.at[a])
                 for a, (s, send, l, land, _) in enumerate(items)]
        for cp in local:
            cp.start()
        copies = _remote_copies(remote, src_refs, land_refs, send_sems, recv_sems)
        for cp in copies:
            cp.start()
        for cp in copies:
            cp.wait_recv()
        for cp in copies:
            cp.wait_send()
        for cp in local:
            cp.wait()

    n_remote = (N_DEV - 1) * len(remote)
    out_shape = tuple(jax.ShapeDtypeStruct(lands[n].shape, lands[n].dtype) for n in ln)
    outs = pl.pallas_call(
        body, name=name,
        in_specs=[pl.BlockSpec(memory_space=pl.ANY)] * (len(sn) + len(given)),
        out_specs=tuple(pl.BlockSpec(memory_space=pl.ANY) for _ in ln),
        out_shape=out_shape,
        input_output_aliases={len(sn) + i: ln.index(n) for i, n in enumerate(given)},
        scratch_shapes=[pltpu.SemaphoreType.DMA((n_remote,)), pltpu.SemaphoreType.DMA((n_remote,)),
                        pltpu.SemaphoreType.DMA((len(items),))],
    )(*[srcs[n] for n in sn], *[lands[n] for n in given])
    return dict(zip(ln, outs))


def start_copies(srcs, lands, items, *, name):
    sn, ln = list(srcs), list(lands)
    n_remote = (N_DEV - 1) * len(items)

    def body(*refs):
        src_refs = dict(zip(sn, refs[:len(sn)]))
        land_refs = dict(zip(ln, refs[len(sn):len(sn) + len(ln)]))
        send_sems, recv_sems = refs[len(sn) + len(ln):len(sn) + len(ln) + 2]
        token = refs[-1]
        for cp in _remote_copies(items, src_refs, land_refs, send_sems, recv_sems):
            cp.start()
        token[...] = jnp.zeros_like(token)

    hbm = lambda a: pltpu.with_memory_space_constraint(a, pltpu.HBM)
    outs = pl.pallas_call(
        body, name=name,
        in_specs=[_HBM] * (len(sn) + len(ln)),
        out_specs=(_SEM, _SEM) + (_HBM,) * len(ln) + (pl.BlockSpec(memory_space=pltpu.VMEM),),
        out_shape=(pltpu.SemaphoreType.DMA((n_remote,)), pltpu.SemaphoreType.DMA((n_remote,)))
        + tuple(pltpu.HBM(lands[n].shape, lands[n].dtype) for n in ln) + (jax.ShapeDtypeStruct((8, LANE), F32),),
        input_output_aliases={len(sn) + i: 2 + i for i in range(len(ln))},
        compiler_params=pltpu.CompilerParams(has_side_effects=_EFFECT),
    )(*[hbm(srcs[n]) for n in sn], *[hbm(lands[n]) for n in ln])
    return (outs[0], outs[1]), dict(zip(ln, outs[2:2 + len(ln)])), outs[-1][0:1, 0:1]


def wait_copies(sems, srcs, lands, items, after, *, name):
    sn, ln = list(srcs), list(lands)

    def body(*refs):
        src_refs = dict(zip(sn, refs[:len(sn)]))
        land_refs = dict(zip(ln, refs[len(sn):len(sn) + len(ln)]))
        send_sems, recv_sems = refs[len(sn) + len(ln):len(sn) + len(ln) + 2]
        copies = _remote_copies(items, src_refs, land_refs, send_sems, recv_sems)
        for cp in copies:
            cp.wait_send()
        for cp in copies:
            cp.wait_recv()

    outs = pl.pallas_call(
        body, name=name,
        in_specs=[_HBM] * (len(sn) + len(ln)) + [_SEM, _SEM, pl.BlockSpec(memory_space=pl.ANY)],
        out_specs=(_HBM,) * len(ln),
        out_shape=tuple(pltpu.HBM(lands[n].shape, lands[n].dtype) for n in ln),
        input_output_aliases={len(sn) + i: i for i in range(len(ln))},
        compiler_params=pltpu.CompilerParams(has_side_effects=_EFFECT),
    )(*[srcs[n] for n in sn], *[lands[n] for n in ln], sems[0], sems[1], after)
    return dict(zip(ln, outs))


def _block(index, size):
    return pl.ds(pl.multiple_of(index * size, LANE), size)


def _adam_tile(rows):
    for t in (256, 176, 128):
        if rows % t == 0:
            return t
    return rows


def sum_adamw(recv, col0, w, m, v, *, name):
    b, r, c = w.shape
    cp = recv.shape[-1]
    tr = _adam_tile(r)
    c1 = 1.0 / (1.0 - ADAM_B1 ** ADAM_STEP)
    c2 = 1.0 / (1.0 - ADAM_B2 ** ADAM_STEP)

    def body(recv_ref, w_ref, m_ref, v_ref, g_ref, d_ref, nm_ref, nv_ref):
        g = recv_ref[0, 0, :, col0:col0 + c].astype(F32)
        for i in range(1, N_DEV):
            g = g + recv_ref[i, 0, :, col0:col0 + c].astype(F32)
        nm = ADAM_B1 * m_ref[0] + (1.0 - ADAM_B1) * g
        nv = ADAM_B2 * v_ref[0] + (1.0 - ADAM_B2) * (g * g)
        g_ref[0] = g
        nm_ref[0] = nm
        nv_ref[0] = nv
        d_ref[0] = -ADAM_LR * ((nm * c1) / (jnp.sqrt(nv * c2) + ADAM_EPS) + ADAM_WD * w_ref[0])

    tile = pl.BlockSpec((1, tr, c), lambda bi, i: (bi, i, 0))
    return pl.pallas_call(
        body, name=name, grid=(b, r // tr),
        in_specs=[pl.BlockSpec((N_DEV, 1, tr, cp), lambda bi, i: (0, bi, i, 0)), tile, tile, tile],
        out_specs=(tile,) * 4, out_shape=(jax.ShapeDtypeStruct((b, r, c), F32),) * 4,
    )(recv, w, m, v)


def _flat_rows(n_elems, row_multiple):
    rows = -(-n_elems // FLAT_COLS)
    return -(-rows // row_multiple) * row_multiple


def _pack(arrays, row_multiple, dtype):
    flat = jnp.concatenate([a.reshape(-1).astype(dtype) for a in arrays])
    rows = _flat_rows(flat.size, row_multiple)
    return jnp.pad(flat, (0, rows * FLAT_COLS - flat.size)).reshape(rows, FLAT_COLS)


def _unpack(flat2d, shapes):
    lead = flat2d.shape[:-2]
    flat = flat2d.reshape(lead + (-1,))
    out, off = [], 0
    for shp in shapes:
        n = int(np.prod(shp))
        out.append(flat[..., off:off + n].reshape(lead + tuple(shp)))
        off += n
    return out


def _join_shards(stacked, axis):
    moved = jnp.moveaxis(stacked, 0, axis)
    shp = list(moved.shape)
    shp[axis:axis + 2] = [shp[axis] * shp[axis + 1]]
    return moved.reshape(shp)


def _split_shards(full, axis):
    shp = list(full.shape)
    shp[axis:axis + 1] = [N_DEV, shp[axis] // N_DEV]
    return jnp.moveaxis(full.reshape(shp), axis, 0)


def kernel(x, meta_tokens, norm_w, ffn_w_gate, ffn_w_up, ffn_w_down, rel_bias_table, even_w_in, even_conv_w, swa_sinks, dn_a_log, dn_dt_bias, dn_norm_w, even_w_out, odd_w_in, gla_w_gate_up, gla_b_gate, gla_norm_w, odd_w_out, loss_target, m_meta_tokens, m_norm_w, m_ffn_w_gate, m_ffn_w_up, m_ffn_w_down, m_rel_bias_table, m_even_w_in, m_even_conv_w, m_swa_sinks, m_dn_a_log, m_dn_dt_bias, m_dn_norm_w, m_even_w_out, m_odd_w_in, m_gla_w_gate_up, m_gla_b_gate, m_gla_norm_w, m_odd_w_out, v_meta_tokens, v_norm_w, v_ffn_w_gate, v_ffn_w_up, v_ffn_w_down, v_rel_bias_table, v_even_w_in, v_even_conv_w, v_swa_sinks, v_dn_a_log, v_dn_dt_bias, v_dn_norm_w, v_even_w_out, v_odd_w_in, v_gla_w_gate_up, v_gla_b_gate, v_gla_norm_w, v_odd_w_out):
    args = locals()
    w = {n: args[n] for n in WEIGHTS}
    m = {n: args["m_" + n] for n in WEIGHTS}
    v = {n: args["v_" + n] for n in WEIGHTS}

    d = D_MODEL
    sds = jax.ShapeDtypeStruct
    whole = lambda ref, j: ref
    cols = lambda size, base=0: (lambda ref, i: ref.at[(slice(None),) * (len(ref.shape) - 1)
                                                       + (pl.ds(pl.multiple_of(base + i * size, LANE), size),)])
    rows3 = lambda size: (lambda ref, i: ref.at[:, _block(i, size), :])
    rows2 = lambda size: (lambda ref, i: ref.at[_block(i, size), :])
    lead = lambda ref, i: ref.at[i]

    pad_cols = lambda a, to: jnp.pad(a, [(0, 0)] * (a.ndim - 1) + [(0, to - a.shape[-1])])
    gate_s = pad_cols(w["ffn_w_gate"].reshape(N_FFN, d, FF_SHARD), FF_SHARD_PAD).astype(BF16)
    up_s = pad_cols(w["ffn_w_up"].reshape(N_FFN, d, FF_SHARD), FF_SHARD_PAD).astype(BF16)
    down_s = jnp.pad(w["ffn_w_down"].reshape(N_FFN, FF_SHARD, d),
                     ((0, 0), (0, FF_SHARD_PAD - FF_SHARD), (0, 0))).astype(BF16)
    small_s = _pack([w[n] for n in SMALL], 8, F32)
    srcs_w = {"ein": pad_cols(w["even_w_in"][0], EVEN_IN_SHARD_PAD).astype(BF16), "eout": w["even_w_out"][0].astype(BF16),
              "small": small_s, "oin": pad_cols(w["odd_w_in"][0], ODD_IN_SHARD_PAD).astype(BF16),
              "oout": w["odd_w_out"][0].astype(BF16)}
    lands_w = {"ein": sds((d, N_DEV * EVEN_IN_SHARD_PAD), BF16), "eout": sds((d, d), BF16),
               "small": sds((N_DEV,) + small_s.shape, F32), "oin": sds((d, N_DEV * ODD_IN_SHARD_PAD), BF16),
               "oout": sds((d, d), BF16)}
    items_w = [("small", whole, "small", lead, "first"), ("ein", whole, "ein", cols(EVEN_IN_SHARD_PAD), "even"),
               ("eout", whole, "eout", rows2(OUT_SHARD), "even"), ("oin", whole, "oin", cols(ODD_IN_SHARD_PAD), "layer1"),
               ("oout", whole, "oout", rows2(OUT_SHARD), "layer1")]
    for tag, pick, group, down_group in (("0", 0, "first", "down0"), ("1", 1, "ffn1", "ffn1"),
                                         ("23", slice(2, 4), "layer1", "layer1")):
        lead_dims = (2,) if tag == "23" else ()
        srcs_w.update({f"gate{tag}": gate_s[pick], f"up{tag}": up_s[pick], f"down{tag}": down_s[pick]})
        lands_w.update({f"w_gu{tag}": sds(lead_dims + (d, 2 * FF_PAD), BF16), f"w_down{tag}": sds(lead_dims + (FF_PAD, d), BF16)})
        items_w += [(f"gate{tag}", whole, f"w_gu{tag}", cols(FF_SHARD_PAD), group),
                    (f"up{tag}", whole, f"w_gu{tag}", cols(FF_SHARD_PAD, FF_PAD), group),
                    (f"down{tag}", whole, f"w_down{tag}", (rows3 if lead_dims else rows2)(FF_SHARD_PAD), down_group)]
    got = exchange(srcs_w, lands_w, items_w, {"first"}, name="gather_first")
    of_group = lambda items, g: [it for it in items if it[4] == g]
    names = lambda items, k: list(dict.fromkeys(it[k] for it in items))
    pending, started = {}, []
    for g in ("down0", "even", "ffn1", "layer1"):
        its = of_group(items_w, g)
        srcs = {n: srcs_w[n] for n in names(its, 0)}
        sems, lands, token = start_copies(srcs, {n: got[n] for n in names(its, 2)}, its, name=f"gather_start_{g}")
        pending[g] = (sems, srcs, lands, its)
        started.append(token)

    unpad = lambda p, shard, shard_pad: p.reshape(d, N_DEV, shard_pad)[:, :, :shard].reshape(d, N_DEV * shard)

    def get_w(stage, after):
        if stage not in pending:
            return {}
        sems, srcs, lands, its = pending[stage]
        landed = wait_copies(sems, srcs, lands, its, after, name=f"gather_wait_{stage}")
        if stage == "down0":
            return {"w_down0": (landed["w_down0"], None)}
        if stage == "even":
            return {"even_w_out": landed["eout"], "even_w_in": unpad(landed["ein"], EVEN_IN_SHARD, EVEN_IN_SHARD_PAD)}
        if stage == "ffn1":
            return {"w_gu1": (landed["w_gu1"], None), "w_down1": (landed["w_down1"], None)}
        return {"w_gu2": (landed["w_gu23"], 0), "w_gu3": (landed["w_gu23"], 1), "w_down2": (landed["w_down23"], 0),
                "w_down3": (landed["w_down23"], 1), "odd_w_out": landed["oout"],
                "odd_w_in": unpad(landed["oin"], ODD_IN_SHARD, ODD_IN_SHARD_PAD)}

    full = {n: w[n] for n in REPL}
    for n, stacked in zip(SMALL, _unpack(got["small"], [w[n].shape for n in SMALL])):
        full[n] = _join_shards(stacked, SHARD_AXIS[n])
    full["w_gu0"] = (got["w_gu0"], None)

    repad = lambda g, shard, shard_pad: pad_cols(g.reshape(d, N_DEV, shard), shard_pad).reshape(d, N_DEV * shard_pad)
    first = lambda ref, i: ref.at[i, 0]
    items_g = [("oin", cols(ODD_IN_SHARD_PAD), "r_oin", first, "layer1"), ("oout", rows2(OUT_SHARD), "r_oout", first, "layer1"),
               ("ein", cols(EVEN_IN_SHARD_PAD), "r_ein", first, "even"), ("eout", rows2(OUT_SHARD), "r_eout", first, "even"),
               ("small", lead, "r_small", first, "last")]
    for i, group in ((3, "ffn3"), (2, "layer1"), (1, "ffn1"), (0, "ffn0")):
        items_g += [(f"g_gu{i}", cols(FF_SHARD_PAD), "r_gu", lambda ref, j, i=i: ref.at[j, i, :, pl.ds(0, FF_SHARD_PAD)], group),
                    (f"g_gu{i}", cols(FF_SHARD_PAD, FF_PAD), "r_gu",
                     lambda ref, j, i=i: ref.at[j, i, :, pl.ds(FF_SHARD_PAD, FF_SHARD_PAD)], group),
                    (f"g_down{i}", rows2(FF_SHARD_PAD), "r_down", lambda ref, j, i=i: ref.at[j, i], group)]
    r_lands = {"r_gu": lax.empty((N_DEV, N_FFN, d, 2 * FF_SHARD_PAD), BF16), "r_down": lax.empty((N_DEV, N_FFN, FF_SHARD_PAD, d), BF16),
               "r_oin": lax.empty((N_DEV, 1, d, ODD_IN_SHARD_PAD), BF16), "r_oout": lax.empty((N_DEV, 1, OUT_SHARD, d), BF16),
               "r_ein": lax.empty((N_DEV, 1, d, EVEN_IN_SHARD_PAD), BF16), "r_eout": lax.empty((N_DEV, 1, OUT_SHARD, d), BF16)}
    srcs_g, sent = {}, {}

    def grad_src(n, grads):
        if n == "oin":
            return repad(grads["odd_w_in"], ODD_IN_SHARD, ODD_IN_SHARD_PAD).astype(BF16)
        if n == "ein":
            return repad(grads["even_w_in"], EVEN_IN_SHARD, EVEN_IN_SHARD_PAD).astype(BF16)
        if n in ("oout", "eout"):
            return grads["odd_w_out" if n == "oout" else "even_w_out"].astype(BF16)
        return grads[n]

    def put_g(stage, grads):
        its = of_group(items_g, stage)
        srcs = {n: grad_src(n, grads) for n in names(its, 0)}
        sems, lands, token = start_copies(srcs, {n: r_lands[n] for n in names(its, 2)}, its, name=f"grads_start_{stage}")
        r_lands.update(lands)
        srcs_g.update(srcs)
        sent[stage] = (sems, srcs, its)
        return token

    tied = sum(t[0, 0] for t in started)
    loss, grad_x, grads = local_step(x[0] + tied, loss_target[0], full, get_w, put_g)
    loss = lax.psum(loss, AXES)

    order = SMALL + REPL
    pieces = [_split_shards(grads[n].reshape(full[n].shape), SHARD_AXIS[n]) if n in SHARD_AXIS
              else jnp.broadcast_to(grads[n].reshape(w[n].shape)[None], (N_DEV,) + w[n].shape) for n in order]
    flat = jnp.concatenate([p.reshape(N_DEV, -1) for p in pieces], axis=1)
    srows = _flat_rows(flat.shape[1], 8)
    grads["small"] = jnp.pad(flat, ((0, 0), (0, srows * FLAT_COLS - flat.shape[1]))).reshape(N_DEV, srows, FLAT_COLS)
    for stage, (sems, srcs, its) in sent.items():
        r_lands.update(wait_copies(sems, srcs, {n: r_lands[n] for n in names(its, 2)}, its, grad_x,
                                   name=f"grads_wait_{stage}"))
    srcs_g.update({n: grad_src(n, grads) for n in names(of_group(items_g, "last"), 0)})
    r_lands["r_small"] = sds((N_DEV, 1, srows, FLAT_COLS), F32)
    recv = exchange(srcs_g, r_lands, items_g, {"last"}, name="exchange_last")
    r_gu, r_down, r_ein, r_oin, r_eout, r_oout, r_small = (
        recv[n] for n in ("r_gu", "r_down", "r_ein", "r_oin", "r_eout", "r_oout", "r_small"))

    result = [{} for _ in range(4)]

    def update(names, recv, col0, view, back):
        for n in names:
            outs = sum_adamw(recv, col0, view(w[n]), view(m[n]), view(v[n]), name=f"adamw_{n}")
            for r, o in zip(result, outs):
                r[n] = back(o, n)

    as_given = lambda o, n: o.reshape(w[n].shape)
    update(["ffn_w_gate"], r_gu, 0, lambda a: a.reshape(N_FFN, d, FF_SHARD), as_given)
    update(["ffn_w_up"], r_gu, FF_SHARD_PAD, lambda a: a.reshape(N_FFN, d, FF_SHARD), as_given)
    update(["ffn_w_down"], r_down, 0, lambda a: a.reshape(N_FFN, FF_SHARD, d), as_given)
    update(["even_w_in"], r_ein, 0, lambda a: a, as_given)
    update(["odd_w_in"], r_oin, 0, lambda a: a, as_given)
    update(["even_w_out"], r_eout, 0, lambda a: a, as_given)
    update(["odd_w_out"], r_oout, 0, lambda a: a, as_given)
    pack_local = lambda t: _pack([t[n] for n in order], 8, F32)[None]
    small_outs = sum_adamw(r_small, 0, pack_local(w), pack_local(m), pack_local(v), name="adamw_small")
    for r, o in zip(result, small_outs):
        r.update(zip(order, _unpack(o[0], [w[n].shape for n in order])))
    return (loss, grad_x[None], *[r[n] for r in result for n in WEIGHTS])
```

```python
import functools
import math

import numpy as np
import jax
import jax.numpy as jnp
from jax import lax
from jax.experimental import pallas as pl
from jax.experimental.pallas import tpu as pltpu

F32 = jnp.float32
BF16 = jnp.bfloat16
MESH = pl.DeviceIdType.MESH
AXES = ("x", "y", "c")
N_DEV = 8

D_MODEL = 1024
N_META = 16
D_FF = 2816
NORM_EPS = 1e-6
NEG_INF = -1e30
SWA_Q_HEADS = 8
SWA_HEAD_DIM = 64
SWA_WINDOW = 128
SWA_BLOCK = 128
REL_BUCKETS = 32
REL_MAX_DIST = 128
DN_HEADS = 4
DN_HEAD_DIM = 128
DN_CONV = 4
GLA_HEADS = 4
GLA_DK = 128
GLA_DV = 256
GLA_GATE_RANK = 16
GLA_GATE_NORM = 16.0
CHUNK = 64
PAD = SWA_BLOCK - N_META
LANE = 128
PROJ_DIM = 3200

ADAM_LR = 0.001
ADAM_B1 = 0.9
ADAM_B2 = 0.999
ADAM_EPS = 1e-08
ADAM_WD = 0.01
ADAM_STEP = 10

FF_SHARD = D_FF // N_DEV
FF_SHARD_PAD = 384
FF_PAD = N_DEV * FF_SHARD_PAD
N_FFN = 4
EVEN_IN_SHARD, EVEN_IN_SHARD_PAD = 353, 384
ODD_IN_SHARD, ODD_IN_SHARD_PAD = 386, 512
OUT_SHARD = D_MODEL // N_DEV

FLAT_COLS = 128
BIG = ("ffn_w_gate", "ffn_w_up", "ffn_w_down", "even_w_in", "even_w_out", "odd_w_in", "odd_w_out")
SMALL = ("meta_tokens", "norm_w", "even_conv_w", "gla_w_gate_up", "gla_b_gate", "gla_norm_w")
REPL = ("rel_bias_table", "swa_sinks", "dn_a_log", "dn_dt_bias", "dn_norm_w")
WEIGHTS = ("meta_tokens", "norm_w", "ffn_w_gate", "ffn_w_up", "ffn_w_down", "rel_bias_table", "even_w_in",
           "even_conv_w", "swa_sinks", "dn_a_log", "dn_dt_bias", "dn_norm_w", "even_w_out", "odd_w_in",
           "gla_w_gate_up", "gla_b_gate", "gla_norm_w", "odd_w_out")
SHARD_AXIS = {"ffn_w_gate": 3, "ffn_w_up": 3, "ffn_w_down": 2, "even_w_in": 2, "even_w_out": 1, "odd_w_in": 2,
              "odd_w_out": 1, "meta_tokens": 1, "norm_w": 2, "even_conv_w": 2, "gla_w_gate_up": 2,
              "gla_b_gate": 1, "gla_norm_w": 1}


def _rms(x, w):
    r = lax.rsqrt(jnp.mean(x * x, axis=-1, keepdims=True) + NORM_EPS)
    return x * r * w


def _sigmoid(x):
    return 0.5 * (jnp.tanh(0.5 * x) + 1.0)


def _silu(x):
    return x * _sigmoid(x)


def _softplus(x):
    pos = x > 0
    return jnp.where(pos, x, 0.0) + jnp.log(1.0 + jnp.exp(jnp.where(pos, -x, x)))


def _l2n(x):
    return x * lax.rsqrt(jnp.sum(x * x, axis=-1, keepdims=True) + 1e-6)


def _split_bf16(x):
    hi = x.astype(BF16)
    return hi, (x - hi.astype(F32)).astype(BF16)


def _make_mm(terms, batched):
    off = 1 if batched else 0
    bdims = ((0,), (0,)) if batched else ((), ())

    def dg(a, b, ca, cb):
        dot = lambda p, q: lax.dot_general(p, q, (((ca + off,), (cb + off,)), bdims), preferred_element_type=F32)
        a_hi, a_lo = _split_bf16(a)
        b_hi, b_lo = _split_bf16(b)
        if terms == 1:
            return dot(a_hi, b_hi)
        return dot(a_hi, b_hi) + (dot(a_hi, b_lo) + dot(a_lo, b_hi))

    @jax.custom_vjp
    def nn(a, b):
        return dg(a, b, 1, 0)

    @jax.custom_vjp
    def nt(a, b):
        return dg(a, b, 1, 1)

    @jax.custom_vjp
    def tn(a, b):
        return dg(a, b, 0, 0)

    nn.defvjp(lambda a, b: (nn(a, b), (a, b)), lambda r, g: (nt(g, r[1]), tn(r[0], g)))
    nt.defvjp(lambda a, b: (nt(a, b), (a, b)), lambda r, g: (nn(g, r[1]), tn(g, r[0])))
    tn.defvjp(lambda a, b: (tn(a, b), (a, b)), lambda r, g: (nt(r[1], g), nn(r[0], g)))
    return nn, nt, tn


_mm, _mm_nt, _mm_tn = _make_mm(1, False)
_mm3, _, _ = _make_mm(3, False)
_bmm, _bmm_nt, _bmm_tn = _make_mm(1, True)
_bmm3, _bmm3_nt, _bmm3_tn = _make_mm(3, True)


@jax.custom_vjp
def _known_inverse(a, inv):
    return inv


_known_inverse.defvjp(lambda a, inv: (inv, inv),
                      lambda inv, g: (-_bmm3_tn(inv, _bmm3_nt(g, inv)), jnp.zeros_like(inv)))


def _tri_ones_dot(x, lower):
    n = x.shape[0]
    r = lax.broadcasted_iota(jnp.int32, (n, n), 0)
    c = lax.broadcasted_iota(jnp.int32, (n, n), 1)
    t = ((r >= c) if lower else (r <= c)).astype(BF16)
    hi, lo = _split_bf16(x)
    return jnp.dot(t, hi, preferred_element_type=F32) + jnp.dot(t, lo, preferred_element_type=F32)


@jax.custom_vjp
def _cumsum_rows(x):
    return _tri_ones_dot(x, True)


_cumsum_rows.defvjp(lambda x: (_tri_ones_dot(x, True), None), lambda _, g: (_tri_ones_dot(g, False),))


def _row_tile(n_rows, cap):
    best = LANE
    for t in range(LANE, cap + 1, LANE):
        if n_rows % t == 0:
            best = t
    return best


def _real_rows(tile_index, tm):
    row = tile_index * tm + lax.broadcasted_iota(jnp.int32, (tm, 1), 0)
    return (row >= PAD).astype(F32)


def _full(shape):
    return pl.BlockSpec(shape, lambda *_: (0,) * len(shape))


def _resident(shape):
    return pl.BlockSpec(shape, lambda *_: (0,) * len(shape), pipeline_mode=pl.Buffered(1))


def _resident_w(wmat, widx):
    if wmat.ndim == 2:
        return _resident(wmat.shape)
    return pl.BlockSpec((None,) + wmat.shape[1:], lambda *_: (widx, 0, 0), pipeline_mode=pl.Buffered(1))


def rms_mm(h, w, wmat, *, swiglu, name, widx=None):
    tp, d = h.shape
    n = wmat.shape[-1]
    tm = _row_tile(tp, 384)
    half = n // 2

    def body(h_ref, w_ref, wm_ref, hn_ref, *outs):
        hn = _rms(h_ref[...], w_ref[...]).astype(BF16)
        hn_ref[...] = hn
        p = jnp.dot(hn, wm_ref[...], preferred_element_type=F32)
        if swiglu:
            g, u = p[:, :half], p[:, half:]
            outs[0][...] = g.astype(BF16)
            outs[1][...] = u.astype(BF16)
            outs[2][...] = (_silu(g) * u).astype(BF16)
        else:
            outs[0][...] = p

    row = lambda width: pl.BlockSpec((tm, width), lambda i: (i, 0))
    if swiglu:
        out_shape = (jax.ShapeDtypeStruct((tp, d), BF16),) + (jax.ShapeDtypeStruct((tp, half), BF16),) * 3
        out_specs = (row(d), row(half), row(half), row(half))
    else:
        out_shape = (jax.ShapeDtypeStruct((tp, d), BF16), jax.ShapeDtypeStruct((tp, n), F32))
        out_specs = (row(d), row(n))
    return pl.pallas_call(
        body, name=name, grid=(tp // tm,),
        in_specs=[row(d), _full((1, d)), _resident_w(wmat, widx)],
        out_specs=out_specs, out_shape=out_shape,
    )(h, w, wmat)


def mm_rms_res(acts, wmat, h, w, *, scale, name, widx=None):
    tp, d = h.shape
    tm = _row_tile(tp, 384)
    widths = [a.shape[1] for a in acts]
    offs = [sum(widths[:i]) for i in range(len(acts))]
    na = len(acts)

    def body(*refs):
        a_refs = refs[:na]
        wm_ref, h_ref, w_ref, f_ref, ho_ref = refs[na:]
        f = None
        for a_ref, off, width in zip(a_refs, offs, widths):
            part = jnp.dot(a_ref[...].astype(BF16), wm_ref[off:off + width, :], preferred_element_type=F32)
            f = part if f is None else f + part
        f_ref[...] = f
        ho_ref[...] = h_ref[...] + scale * _rms(f, w_ref[...])

    row = lambda width: pl.BlockSpec((tm, width), lambda i: (i, 0))
    return pl.pallas_call(
        body, name=name, grid=(tp // tm,),
        in_specs=[row(wd) for wd in widths] + [_resident_w(wmat, widx), row(d), _full((1, d))],
        out_specs=(row(d), row(d)),
        out_shape=(jax.ShapeDtypeStruct((tp, d), F32), jax.ShapeDtypeStruct((tp, d), F32)),
    )(*acts, wmat, h, w)


def mm_rms_res_bwd(dho, f, w, wmat, gu, *, scale, name, widx=None):
    tp, d = f.shape
    k = wmat.shape[-2]
    tm = _row_tile(tp, 384)
    swiglu = gu is not None

    def body(*refs):
        if swiglu:
            dho_ref, f_ref, w_ref, wm_ref, g_ref, u_ref, df_ref, dw_ref, dgu_ref = refs
        else:
            dho_ref, f_ref, w_ref, wm_ref, df_ref, dw_ref, da_ref = refs
        i = pl.program_id(0)
        _, vjp = jax.vjp(lambda ff, ww: scale * _rms(ff, ww), f_ref[...], w_ref[...])
        df, dw = vjp(dho_ref[...])
        dfb = (df * _real_rows(i, tm)).astype(BF16)
        df_ref[...] = dfb

        @pl.when(i == 0)
        def _():
            dw_ref[...] = jnp.zeros_like(dw_ref)

        dw_ref[...] += dw
        da = lax.dot_general(dfb, wm_ref[...], (((1,), (1,)), ((), ())), preferred_element_type=F32)
        if swiglu:
            g = g_ref[...].astype(F32)
            u = u_ref[...].astype(F32)
            s = _sigmoid(g)
            dgu_ref[:, :k] = (da * u * s * (1.0 + g * (1.0 - s))).astype(BF16)
            dgu_ref[:, k:] = (da * g * s).astype(BF16)
        else:
            da_ref[...] = da

    row = lambda width: pl.BlockSpec((tm, width), lambda i: (i, 0))
    in_specs = [row(d), row(d), _full((1, d)), _resident_w(wmat, widx)]
    args = [dho, f, w, wmat]
    out_shape = [jax.ShapeDtypeStruct((tp, d), BF16), jax.ShapeDtypeStruct((1, d), F32)]
    out_specs = [row(d), _full((1, d))]
    if swiglu:
        in_specs += [row(k), row(k)]
        args += list(gu)
        out_shape += [jax.ShapeDtypeStruct((tp, 2 * k), BF16)]
        out_specs += [row(2 * k)]
    else:
        out_shape += [jax.ShapeDtypeStruct((tp, k), F32)]
        out_specs += [row(k)]
    return pl.pallas_call(body, name=name, grid=(tp // tm,), in_specs=in_specs, out_specs=tuple(out_specs),
                          out_shape=tuple(out_shape))(*args)


def rms_mm_bwd(dps, wmat, h, w, dho, *, name, widx=None):
    tp, d = h.shape
    tm = _row_tile(tp, 384)
    widths = [p.shape[1] for p in dps]
    offs = [sum(widths[:i]) for i in range(len(dps))]
    ndp = len(dps)

    def body(*refs):
        dp_refs = refs[:ndp]
        wm_ref, h_ref, w_ref, dho_ref, dh_ref, dw_ref = refs[ndp:]
        i = pl.program_id(0)
        dhn = None
        for dp_ref, off, width in zip(dp_refs, offs, widths):
            part = lax.dot_general(dp_ref[...].astype(BF16), wm_ref[:, off:off + width], (((1,), (1,)), ((), ())),
                                   preferred_element_type=F32)
            dhn = part if dhn is None else dhn + part
        _, vjp = jax.vjp(_rms, h_ref[...], w_ref[...])
        dx, dw = vjp(dhn)
        dh_ref[...] = (dho_ref[...] + dx) * _real_rows(i, tm)

        @pl.when(i == 0)
        def _():
            dw_ref[...] = jnp.zeros_like(dw_ref)

        dw_ref[...] += dw

    row = lambda width: pl.BlockSpec((tm, width), lambda i: (i, 0))
    return pl.pallas_call(
        body, name=name, grid=(tp // tm,),
        in_specs=[row(wd) for wd in widths] + [_resident_w(wmat, widx), row(d), _full((1, d)), row(d)],
        out_specs=(row(d), _full((1, d))),
        out_shape=(jax.ShapeDtypeStruct((tp, d), F32), jax.ShapeDtypeStruct((1, d), F32)),
    )(*dps, wmat, h, w, dho)


def mm_tn(a, b, *, name, out_dtype=F32):
    t, m = a.shape
    n = b.shape[1]
    bm = _row_tile(m, 512)
    bn = _row_tile(n, 1536)
    bk = _row_tile(t, 1408)
    nk = t // bk

    def body(a_ref, b_ref, o_ref, acc):
        @pl.when(pl.program_id(2) == 0)
        def _():
            acc[...] = jnp.zeros_like(acc)

        acc[...] += lax.dot_general(a_ref[...].astype(BF16), b_ref[...].astype(BF16), (((0,), (0,)), ((), ())),
                                    preferred_element_type=F32)

        @pl.when(pl.program_id(2) == nk - 1)
        def _():
            o_ref[...] = acc[...].astype(o_ref.dtype)

    return pl.pallas_call(
        body, name=name, grid=(m // bm, n // bn, nk),
        in_specs=[pl.BlockSpec((bk, bm), lambda i, j, kk: (kk, i)), pl.BlockSpec((bk, bn), lambda i, j, kk: (kk, j))],
        out_specs=pl.BlockSpec((bm, bn), lambda i, j, kk: (i, j)),
        out_shape=jax.ShapeDtypeStruct((m, n), out_dtype), scratch_shapes=[pltpu.VMEM((bm, bn), F32)],
    )(a, b)


def loss_and_grad(h, target, *, name):
    tp, d = h.shape
    tm = SWA_BLOCK

    def body(h_ref, t_ref, dh_ref, loss_ref):
        i = pl.program_id(0)

        @pl.when(i == 0)
        def _():
            loss_ref[...] = jnp.zeros_like(loss_ref)
            dh_ref[...] = jnp.zeros_like(dh_ref)

        @pl.when(i > 0)
        def _():
            err = h_ref[...] - t_ref[...]
            dh_ref[...] = err * (1.0 / d)
            loss_ref[...] += 0.5 * jnp.sum(jnp.sum(err * err, axis=1, keepdims=True), axis=0, keepdims=True) * (1.0 / d)

    return pl.pallas_call(
        body, name=name, grid=(tp // tm,),
        in_specs=[pl.BlockSpec((tm, d), lambda i: (i, 0)), pl.BlockSpec((tm, d), lambda i: (jnp.maximum(i - 1, 0), 0))],
        out_specs=(pl.BlockSpec((tm, d), lambda i: (i, 0)), _full((1, 1))),
        out_shape=(jax.ShapeDtypeStruct((tp, d), F32), jax.ShapeDtypeStruct((1, 1), F32)),
    )(h, target)


def _t5_bucket_np(rel):
    n = np.maximum(rel, 0)
    max_exact = REL_BUCKETS // 2
    n_f = np.maximum(n, 1).astype(np.float32)
    large = max_exact + (np.log(n_f / np.float32(max_exact)) / np.float32(math.log(REL_MAX_DIST / max_exact))
                         * np.float32(REL_BUCKETS - max_exact)).astype(np.int32)
    large = np.minimum(large, REL_BUCKETS - 1)
    return np.where(n < max_exact, n, large).astype(np.int32)


def _swa_positions_np(n):
    i = np.arange(SWA_BLOCK)[:, None]
    j = np.arange(3 * SWA_BLOCK)[None, :]
    pos_q = n * SWA_BLOCK + i - PAD
    pos_k = np.where(j < SWA_BLOCK, j - PAD, (n - 1) * SWA_BLOCK + (j - SWA_BLOCK) - PAD)
    return pos_q, pos_k


def _swa_buckets():
    out = []
    for n in range(3):
        pos_q, pos_k = _swa_positions_np(n)
        out.append(_t5_bucket_np(pos_q - pos_k))
    return jnp.asarray(np.stack(out))


def swa_bias(table, buckets, *, name):
    nc, nq, nk = buckets.shape

    def body(tab_ref, bkt_ref, out_ref):
        for c in range(nc):
            bkt = bkt_ref[c]
            for h in range(SWA_Q_HEADS):
                acc = jnp.zeros((nq, nk), F32)
                for b in range(REL_BUCKETS):
                    acc = jnp.where(bkt == b, tab_ref[b, h], acc)
                out_ref[c, h] = acc

    return pl.pallas_call(
        body, name=name,
        in_specs=[pl.BlockSpec(memory_space=pltpu.SMEM), pl.BlockSpec(memory_space=pltpu.VMEM)],
        out_specs=pl.BlockSpec(memory_space=pltpu.VMEM),
        out_shape=jax.ShapeDtypeStruct((nc, SWA_Q_HEADS, nq, nk), F32),
    )(table, buckets)


def swa_bias_bwd(dbias, buckets, *, name):
    nc = buckets.shape[0]

    def body(db_ref, bkt_ref, out_ref):
        lane = lax.broadcasted_iota(jnp.int32, (1, LANE), 1)
        for b in range(REL_BUCKETS):
            row = jnp.zeros((1, LANE), F32)
            for c in range(nc):
                hit = bkt_ref[c] == b
                for h in range(SWA_Q_HEADS):
                    part = jnp.where(hit, db_ref[c, h], 0.0)
                    tot = jnp.sum(jnp.sum(part, axis=1, keepdims=True), axis=0, keepdims=True)
                    row = row + jnp.where(lane == h, tot, 0.0)
            out_ref[b:b + 1, :] = row

    return pl.pallas_call(
        body, name=name,
        in_specs=[pl.BlockSpec(memory_space=pltpu.VMEM), pl.BlockSpec(memory_space=pltpu.VMEM)],
        out_specs=pl.BlockSpec(memory_space=pltpu.VMEM),
        out_shape=jax.ShapeDtypeStruct((REL_BUCKETS, LANE), F32),
    )(dbias, buckets)


def _swa_block(q, kvm, kvp, kvc, bias, sinks, n):
    blk = SWA_BLOCK
    i = lax.broadcasted_iota(jnp.int32, (blk, 3 * blk), 0)
    j = lax.broadcasted_iota(jnp.int32, (blk, 3 * blk), 1)
    pos_q = n * blk + i - PAD
    is_meta = j < blk
    pos_k = jnp.where(is_meta, j - PAD, (n - 1) * blk + (j - blk) - PAD)
    rel = pos_q - pos_k
    valid = ((is_meta & (pos_k >= 0) & (pos_k < N_META) & (rel >= 0))
             | (jnp.logical_not(is_meta) & (pos_k >= N_META) & (rel >= 0) & (rel < SWA_WINDOW)))
    kv = jnp.concatenate([kvm, kvp, kvc], axis=0)
    lane = lax.broadcasted_iota(jnp.int32, (1, LANE), 1)
    halves = ((lane < SWA_HEAD_DIM).astype(F32), (lane >= SWA_HEAD_DIM).astype(F32))
    scale = SWA_HEAD_DIM ** -0.5
    outs = []
    for pair in range(SWA_Q_HEADS // 2):
        qp = q[:, pair * LANE:(pair + 1) * LANE]
        grp = pair // 2
        kg = kv[:, grp * LANE:(grp + 1) * LANE]
        vg = kv[:, (2 + grp) * LANE:(3 + grp) * LANE]
        op = None
        for hh in range(2):
            h = 2 * pair + hh
            s = _mm_nt(qp * halves[hh], kg) * scale + bias[h]
            s = jnp.where(valid, s, NEG_INF)
            sink = jnp.sum(jnp.where(lane == h, sinks, 0.0), axis=1, keepdims=True)
            m = lax.stop_gradient(jnp.maximum(jnp.max(s, axis=1, keepdims=True), sink))
            e = jnp.exp(s - m)
            den = jnp.sum(e, axis=1, keepdims=True) + jnp.exp(sink - m)
            part = _mm(e / den, vg) * halves[hh]
            op = part if op is None else op + part
        outs.append(op)
    return jnp.concatenate(outs, axis=1)


def _swa_in_specs(nb, rev):
    blk = SWA_BLOCK
    step = (lambda i: nb - 1 - i) if rev else (lambda i: i)
    return [
        pl.BlockSpec((blk, 4 * LANE), lambda i: (step(i), 0)),
        pl.BlockSpec((blk, 4 * LANE), lambda i: (0, 1)),
        pl.BlockSpec((blk, 4 * LANE), lambda i: (jnp.maximum(step(i) - 1, 0), 1)),
        pl.BlockSpec((blk, 4 * LANE), lambda i: (step(i), 1)),
        pl.BlockSpec((1, SWA_Q_HEADS, blk, 3 * blk), lambda i: (jnp.minimum(step(i), 2), 0, 0, 0)),
        _full((1, LANE)),
    ]


def swa_fwd(proj, bias, sinks, *, name):
    tp = proj.shape[0]
    nb = tp // SWA_BLOCK

    def body(q_ref, kvm_ref, kvp_ref, kvc_ref, bias_ref, sinks_ref, o_ref):
        n = pl.program_id(0)
        o_ref[...] = _swa_block(q_ref[...], kvm_ref[...], kvp_ref[...], kvc_ref[...], bias_ref[0], sinks_ref[...], n)

    return pl.pallas_call(
        body, name=name, grid=(nb,),
        in_specs=_swa_in_specs(nb, False),
        out_specs=pl.BlockSpec((SWA_BLOCK, 4 * LANE), lambda i: (i, 0)),
        out_shape=jax.ShapeDtypeStruct((tp, 4 * LANE), F32),
    )(proj, proj, proj, proj, bias, sinks)


def swa_bwd(proj, bias, sinks, do, *, name):
    tp = proj.shape[0]
    nb = tp // SWA_BLOCK
    blk = SWA_BLOCK

    def body(q_ref, kvm_ref, kvp_ref, kvc_ref, bias_ref, sinks_ref, do_ref, dq_ref, dkv_ref, dbias_ref, dsinks_ref,
             carry, meta_acc):
        i = pl.program_id(0)
        n = nb - 1 - i

        @pl.when(i == 0)
        def _():
            carry[...] = jnp.zeros_like(carry)
            meta_acc[...] = jnp.zeros_like(meta_acc)
            dsinks_ref[...] = jnp.zeros_like(dsinks_ref)

        fn = lambda q, kvm, kvp, kvc, b, s: _swa_block(q, kvm, kvp, kvc, b, s, n)
        _, vjp = jax.vjp(fn, q_ref[...], kvm_ref[...], kvp_ref[...], kvc_ref[...], bias_ref[0], sinks_ref[...])
        dq, dkvm, dkvp, dkvc, dbias, dsinks = vjp(do_ref[...])
        dq_ref[...] = dq
        meta_acc[...] += dkvm
        dkv_ref[...] = dkvc + carry[...] + jnp.where(n == 0, meta_acc[...], 0.0)
        carry[...] = dkvp
        first_visit = (n == nb - 1) | (n < 2)

        @pl.when(first_visit)
        def _():
            dbias_ref[0] = dbias

        @pl.when(jnp.logical_not(first_visit))
        def _():
            dbias_ref[0] += dbias

        dsinks_ref[...] += dsinks

    rev = lambda i: nb - 1 - i
    return pl.pallas_call(
        body, name=name, grid=(nb,),
        in_specs=_swa_in_specs(nb, True) + [pl.BlockSpec((blk, 4 * LANE), lambda i: (rev(i), 0))],
        out_specs=(pl.BlockSpec((blk, 4 * LANE), lambda i: (rev(i), 0)),
                   pl.BlockSpec((blk, 4 * LANE), lambda i: (rev(i), 0)),
                   pl.BlockSpec((1, SWA_Q_HEADS, blk, 3 * blk), lambda i: (jnp.minimum(rev(i), 2), 0, 0, 0)),
                   _full((1, LANE))),
        out_shape=(jax.ShapeDtypeStruct((tp, 4 * LANE), F32), jax.ShapeDtypeStruct((tp, 4 * LANE), F32),
                   jax.ShapeDtypeStruct((3, SWA_Q_HEADS, blk, 3 * blk), F32), jax.ShapeDtypeStruct((1, LANE), F32)),
        scratch_shapes=[pltpu.VMEM((blk, 4 * LANE), F32), pltpu.VMEM((blk, 4 * LANE), F32)],
    )(proj, proj, proj, proj, bias, sinks, do)


CONV_COL0 = 2
HALO = 8


def conv_fwd(proj, conv_w, *, name):
    tp = proj.shape[0]
    tm = _row_tile(tp, 384)
    cw = 4 * LANE
    ncol = conv_w.shape[1] // cw

    def body(x_ref, halo_ref, w_ref, y_ref, buf):
        i = pl.program_id(1)
        buf[0:HALO, :] = jnp.where(i > 0, halo_ref[...], 0.0)
        buf[HALO:, :] = x_ref[...]
        acc = None
        for j in range(DN_CONV):
            term = w_ref[j:j + 1, :] * buf[pl.ds(HALO - (DN_CONV - 1) + j, tm), :]
            acc = term if acc is None else acc + term
        y_ref[...] = acc

    return pl.pallas_call(
        body, name=name, grid=(ncol, tp // tm),
        in_specs=[pl.BlockSpec((tm, cw), lambda c, i: (i, CONV_COL0 + c)),
                  pl.BlockSpec((HALO, cw), lambda c, i: (jnp.maximum(i * (tm // HALO) - 1, 0), CONV_COL0 + c)),
                  pl.BlockSpec((DN_CONV, cw), lambda c, i: (0, c))],
        out_specs=pl.BlockSpec((tm, cw), lambda c, i: (i, c)),
        out_shape=jax.ShapeDtypeStruct((tp, ncol * cw), F32),
        scratch_shapes=[pltpu.VMEM((tm + HALO, cw), F32)],
    )(proj, proj, conv_w)


def conv_bwd(proj, conv_w, dy, *, name):
    tp = proj.shape[0]
    tm = _row_tile(tp, 384)
    cw = 4 * LANE
    ncol = conv_w.shape[1] // cw
    nt = tp // tm

    def body(x_ref, xhalo_ref, w_ref, dy_ref, dyhalo_ref, dx_ref, dw_ref, xbuf, dbuf):
        i = pl.program_id(1)
        xbuf[0:HALO, :] = jnp.where(i > 0, xhalo_ref[...], 0.0)
        xbuf[HALO:, :] = x_ref[...]
        dbuf[0:tm, :] = dy_ref[...]
        dbuf[tm:, :] = jnp.where(i < nt - 1, dyhalo_ref[...], 0.0)
        dy_t = dy_ref[...]
        acc = None
        rows = []
        for j in range(DN_CONV):
            term = w_ref[j:j + 1, :] * dbuf[pl.ds(DN_CONV - 1 - j, tm), :]
            acc = term if acc is None else acc + term
            rows.append(jnp.sum(dy_t * xbuf[pl.ds(HALO - (DN_CONV - 1) + j, tm), :], axis=0, keepdims=True))
        dx_ref[...] = acc

        @pl.when(i == 0)
        def _():
            dw_ref[...] = jnp.zeros_like(dw_ref)

        for j in range(DN_CONV):
            dw_ref[j:j + 1, :] += rows[j]

    return pl.pallas_call(
        body, name=name, grid=(ncol, nt),
        in_specs=[pl.BlockSpec((tm, cw), lambda c, i: (i, CONV_COL0 + c)),
                  pl.BlockSpec((HALO, cw), lambda c, i: (jnp.maximum(i * (tm // HALO) - 1, 0), CONV_COL0 + c)),
                  pl.BlockSpec((DN_CONV, cw), lambda c, i: (0, c)),
                  pl.BlockSpec((tm, cw), lambda c, i: (i, c)),
                  pl.BlockSpec((HALO, cw), lambda c, i: (jnp.minimum((i + 1) * (tm // HALO), tp // HALO - 1), c))],
        out_specs=(pl.BlockSpec((tm, cw), lambda c, i: (i, c)), pl.BlockSpec((DN_CONV, cw), lambda c, i: (0, c))),
        out_shape=(jax.ShapeDtypeStruct((tp, ncol * cw), F32), jax.ShapeDtypeStruct((DN_CONV, ncol * cw), F32)),
        scratch_shapes=[pltpu.VMEM((tm + HALO, cw), F32), pltpu.VMEM((tm + HALO, cw), F32)],
    )(proj, proj, conv_w, dy, dy)


def _stack(parts):
    return jnp.concatenate([p[None] for p in parts], axis=0)


def _chunk_masks():
    r = lax.broadcasted_iota(jnp.int32, (CHUNK, CHUNK), 0)
    c = lax.broadcasted_iota(jnp.int32, (CHUNK, CHUNK), 1)
    return (r >= c).astype(F32), (r > c).astype(F32), (r == c).astype(F32)


def _dn_chunk(y, z, small, s, a_log, dt_bias, norm_w, rows, known_inv=None):
    tri_incl, tri_strict, eye = _chunk_masks()
    lane = lax.broadcasted_iota(jnp.int32, (1, LANE), 1)
    dk = DN_HEAD_DIM
    nh = DN_HEADS
    heads = lambda t, first: _stack([t[:, (first + h) * dk:(first + h + 1) * dk] for h in range(nh)])
    pick = lambda t, l: jnp.sum(jnp.where(lane == l, t, 0.0), axis=1, keepdims=True)
    q = _l2n(_silu(heads(y, 0))) * dk ** -0.5
    k = _l2n(_silu(heads(y, nh)))
    v = _silu(heads(y, 2 * nh))
    g_all = jnp.where(lane < nh, -jnp.exp(a_log) * _softplus(small + dt_bias), 0.0) * rows
    beta_all = _sigmoid(small)
    gc_all = _cumsum_rows(g_all)
    g_sum = jnp.sum(g_all, axis=0, keepdims=True)
    gc = _stack([pick(gc_all, h) for h in range(nh)])
    beta = _stack([pick(beta_all, nh + h) for h in range(nh)])
    g_last = _stack([pick(g_sum, h) for h in range(nh)])
    gc_row = jnp.sum(eye * gc, axis=1, keepdims=True)
    gamma = jnp.exp((gc - gc_row) * tri_incl) * tri_incl
    k_beta = k * beta
    v_beta = v * beta
    a = _bmm_nt(k_beta, k) * gamma * tri_strict
    if known_inv is None:
        inv = eye - a
        power = a
        for _ in range(5):
            power = _bmm3(power, power)
            inv = inv + _bmm3(inv, power)
    else:
        inv = _known_inverse(a, known_inv)
    e_gc = jnp.exp(gc)
    uw = _bmm3(inv, jnp.concatenate([v_beta, k_beta * e_gc], axis=2))
    u, w = uw[:, :, :dk], uw[:, :, dk:]
    attn = _bmm_nt(q, k) * gamma
    q_dec = q * e_gc
    k_dec = k * jnp.exp(g_last - gc)
    v_new = u - _bmm(w, s)
    o = _bmm(q_dec, s) + _bmm(attn, v_new)
    s_new = s * jnp.exp(g_last) + _bmm_tn(k_dec, v_new)
    out = _rms(o, norm_w) * _silu(heads(z, 0))
    return jnp.concatenate([out[h] for h in range(nh)], axis=1), s_new, inv


Z_COL = 5
SMALL_COL = 24


def _chunk_rows(n):
    row = n * CHUNK + lax.broadcasted_iota(jnp.int32, (CHUNK, 1), 0)
    return (row >= PAD).astype(F32)


def dn_fwd(y, proj, a_log, dt_bias, norm_w, *, name):
    tp = y.shape[0]
    nc = tp // CHUNK
    dk = DN_HEAD_DIM

    def body(y_ref, z_ref, small_ref, al_ref, dt_ref, nw_ref, o_ref, ssave_ref, isave_ref, state):
        n = pl.program_id(0)

        @pl.when(n == 0)
        def _():
            state[...] = jnp.zeros_like(state)

        ssave_ref[0] = state[...]
        out, s_new, inv = _dn_chunk(y_ref[...], z_ref[...], small_ref[...], state[...], al_ref[...], dt_ref[...],
                                    nw_ref[...], _chunk_rows(n))
        o_ref[...] = out
        isave_ref[0] = inv
        state[...] = s_new

    return pl.pallas_call(
        body, name=name, grid=(nc,),
        in_specs=[pl.BlockSpec((CHUNK, y.shape[1]), lambda n: (n, 0)),
                  pl.BlockSpec((CHUNK, 4 * LANE), lambda n: (n, Z_COL)),
                  pl.BlockSpec((CHUNK, LANE), lambda n: (n, SMALL_COL)),
                  _full((1, LANE)), _full((1, LANE)), _full((1, LANE))],
        out_specs=(pl.BlockSpec((CHUNK, 4 * LANE), lambda n: (n, 0)),
                   pl.BlockSpec((1, DN_HEADS, dk, dk), lambda n: (n, 0, 0, 0)),
                   pl.BlockSpec((1, DN_HEADS, CHUNK, CHUNK), lambda n: (n, 0, 0, 0))),
        out_shape=(jax.ShapeDtypeStruct((tp, 4 * LANE), F32), jax.ShapeDtypeStruct((nc, DN_HEADS, dk, dk), F32),
                   jax.ShapeDtypeStruct((nc, DN_HEADS, CHUNK, CHUNK), F32)),
        scratch_shapes=[pltpu.VMEM((DN_HEADS, dk, dk), F32)],
    )(y, proj, proj, a_log, dt_bias, norm_w)


def dn_bwd(y, proj, a_log, dt_bias, norm_w, ssave, isave, do, *, name):
    tp = y.shape[0]
    nc = tp // CHUNK
    dk = DN_HEAD_DIM
    rev = lambda i: nc - 1 - i

    def body(y_ref, z_ref, small_ref, al_ref, dt_ref, nw_ref, ss_ref, is_ref, do_ref,
             dy_ref, dz_ref, dsmall_ref, dal_ref, ddt_ref, dnw_ref, dstate):
        i = pl.program_id(0)
        n = nc - 1 - i

        @pl.when(i == 0)
        def _():
            dstate[...] = jnp.zeros_like(dstate)
            dal_ref[...] = jnp.zeros_like(dal_ref)
            ddt_ref[...] = jnp.zeros_like(ddt_ref)
            dnw_ref[...] = jnp.zeros_like(dnw_ref)

        rows = _chunk_rows(n)
        known_inv = is_ref[0]
        fn = lambda *a: _dn_chunk(*a, rows, known_inv)[:2]
        _, vjp = jax.vjp(fn, y_ref[...], z_ref[...], small_ref[...], ss_ref[0], al_ref[...], dt_ref[...], nw_ref[...])
        dy, dz, dsmall, ds, dal, ddt, dnw = vjp((do_ref[...], dstate[...]))
        dy_ref[...] = dy
        dz_ref[...] = dz
        dsmall_ref[...] = dsmall
        dstate[...] = ds
        dal_ref[...] += dal
        ddt_ref[...] += ddt
        dnw_ref[...] += dnw

    return pl.pallas_call(
        body, name=name, grid=(nc,),
        in_specs=[pl.BlockSpec((CHUNK, y.shape[1]), lambda i: (rev(i), 0)),
                  pl.BlockSpec((CHUNK, 4 * LANE), lambda i: (rev(i), Z_COL)),
                  pl.BlockSpec((CHUNK, LANE), lambda i: (rev(i), SMALL_COL)),
                  _full((1, LANE)), _full((1, LANE)), _full((1, LANE)),
                  pl.BlockSpec((1, DN_HEADS, dk, dk), lambda i: (rev(i), 0, 0, 0)),
                  pl.BlockSpec((1, DN_HEADS, CHUNK, CHUNK), lambda i: (rev(i), 0, 0, 0)),
                  pl.BlockSpec((CHUNK, 4 * LANE), lambda i: (rev(i), 1))],
        out_specs=(pl.BlockSpec((CHUNK, y.shape[1]), lambda i: (rev(i), 0)),
                   pl.BlockSpec((CHUNK, 4 * LANE), lambda i: (rev(i), 0)),
                   pl.BlockSpec((CHUNK, LANE), lambda i: (rev(i), 0)),
                   _full((1, LANE)), _full((1, LANE)), _full((1, LANE))),
        out_shape=(jax.ShapeDtypeStruct((tp, y.shape[1]), F32), jax.ShapeDtypeStruct((tp, 4 * LANE), F32),
                   jax.ShapeDtypeStruct((tp, LANE), F32), jax.ShapeDtypeStruct((1, LANE), F32),
                   jax.ShapeDtypeStruct((1, LANE), F32), jax.ShapeDtypeStruct((1, LANE), F32)),
        scratch_shapes=[pltpu.VMEM((DN_HEADS, dk, dk), F32)],
    )(y, proj, proj, a_log, dt_bias, norm_w, ssave, isave, do)


def _gla_chunk(q, k, v, gate, low, s, w_gate_up, b_gate, norm_w, rows):
    tri_incl, _, _ = _chunk_masks()
    dk, dv, nh = GLA_DK, GLA_DV, GLA_HEADS
    heads = lambda t, width: _stack([t[:, h * width:(h + 1) * width] for h in range(nh)])
    logit = _mm3(low, w_gate_up) + b_gate
    glog_all = -_softplus(-logit) * (1.0 / GLA_GATE_NORM) * rows
    glog = heads(glog_all, dk)
    bcum = heads(_cumsum_rows(glog_all), dk)
    qh = heads(q, dk) * dk ** -0.5
    kh = heads(k, dk)
    vh = heads(v, dv)
    q_dec = qh * jnp.exp(bcum)
    attn = _bmm_nt(q_dec, kh * jnp.exp(-bcum)) * tri_incl
    b_last = jnp.sum(glog, axis=1, keepdims=True)
    k_dec = kh * jnp.exp(b_last - bcum)
    r = lax.broadcasted_iota(jnp.int32, (dk, dk), 0)
    c = lax.broadcasted_iota(jnp.int32, (dk, dk), 1)
    b_last_col = jnp.sum((r == c).astype(F32) * b_last, axis=2, keepdims=True)
    o = _bmm(attn, vh) + _bmm(q_dec, s)
    s_new = s * jnp.exp(b_last_col) + _bmm_tn(k_dec, vh)
    out = _rms(o, norm_w) * _silu(heads(gate, dv))
    return jnp.concatenate([out[h] for h in range(nh)], axis=1), s_new


LOW_COL = 24


def _gla_in_specs(step):
    return [pl.BlockSpec((CHUNK, 4 * LANE), lambda i: (step(i), 0)),
            pl.BlockSpec((CHUNK, 4 * LANE), lambda i: (step(i), 1)),
            pl.BlockSpec((CHUNK, 8 * LANE), lambda i: (step(i), 1)),
            pl.BlockSpec((CHUNK, 8 * LANE), lambda i: (step(i), 2)),
            pl.BlockSpec((CHUNK, LANE), lambda i: (step(i), LOW_COL)),
            _full((LANE, 4 * LANE)), _full((1, 4 * LANE)), _full((1, GLA_DV))]


def gla_fwd(proj, w_gate_up, b_gate, norm_w, *, name):
    tp = proj.shape[0]
    nc = tp // CHUNK

    def body(q_ref, k_ref, v_ref, g_ref, low_ref, wgu_ref, bg_ref, nw_ref, o_ref, ssave_ref, state):
        n = pl.program_id(0)

        @pl.when(n == 0)
        def _():
            state[...] = jnp.zeros_like(state)

        ssave_ref[0] = state[...]
        out, s_new = _gla_chunk(q_ref[...], k_ref[...], v_ref[...], g_ref[...], low_ref[...], state[...], wgu_ref[...],
                                bg_ref[...], nw_ref[...], _chunk_rows(n))
        o_ref[...] = out
        state[...] = s_new

    return pl.pallas_call(
        body, name=name, grid=(nc,),
        in_specs=_gla_in_specs(lambda i: i),
        out_specs=(pl.BlockSpec((CHUNK, 8 * LANE), lambda n: (n, 0)),
                   pl.BlockSpec((1, GLA_HEADS, GLA_DK, GLA_DV), lambda n: (n, 0, 0, 0))),
        out_shape=(jax.ShapeDtypeStruct((tp, 8 * LANE), F32),
                   jax.ShapeDtypeStruct((nc, GLA_HEADS, GLA_DK, GLA_DV), F32)),
        scratch_shapes=[pltpu.VMEM((GLA_HEADS, GLA_DK, GLA_DV), F32)],
    )(proj, proj, proj, proj, proj, w_gate_up, b_gate, norm_w)


def gla_bwd(proj, w_gate_up, b_gate, norm_w, ssave, do, *, name):
    tp = proj.shape[0]
    nc = tp // CHUNK
    rev = lambda i: nc - 1 - i

    def body(q_ref, k_ref, v_ref, g_ref, low_ref, wgu_ref, bg_ref, nw_ref, ss_ref, do_ref,
             dq_ref, dk_ref, dv_ref, dg_ref, dlow_ref, dwgu_ref, dbg_ref, dnw_ref, dstate):
        i = pl.program_id(0)
        n = nc - 1 - i

        @pl.when(i == 0)
        def _():
            dstate[...] = jnp.zeros_like(dstate)
            dwgu_ref[...] = jnp.zeros_like(dwgu_ref)
            dbg_ref[...] = jnp.zeros_like(dbg_ref)
            dnw_ref[...] = jnp.zeros_like(dnw_ref)

        rows = _chunk_rows(n)
        fn = lambda *a: _gla_chunk(*a, rows)
        _, vjp = jax.vjp(fn, q_ref[...], k_ref[...], v_ref[...], g_ref[...], low_ref[...], ss_ref[0], wgu_ref[...],
                         bg_ref[...], nw_ref[...])
        dq, dk, dv, dg, dlow, ds, dwgu, dbg, dnw = vjp((do_ref[...], dstate[...]))
        dq_ref[...] = dq
        dk_ref[...] = dk
        dv_ref[...] = dv
        dg_ref[...] = dg
        dlow_ref[...] = dlow
        dstate[...] = ds
        dwgu_ref[...] += dwgu
        dbg_ref[...] += dbg
        dnw_ref[...] += dnw

    chunk = lambda width: pl.BlockSpec((CHUNK, width), lambda i: (rev(i), 0))
    return pl.pallas_call(
        body, name=name, grid=(nc,),
        in_specs=_gla_in_specs(rev) + [pl.BlockSpec((1, GLA_HEADS, GLA_DK, GLA_DV), lambda i: (rev(i), 0, 0, 0)),
                                       chunk(8 * LANE)],
        out_specs=(chunk(4 * LANE), chunk(4 * LANE), chunk(8 * LANE), chunk(8 * LANE), chunk(LANE),
                   _full((LANE, 4 * LANE)), _full((1, 4 * LANE)), _full((1, GLA_DV))),
        out_shape=(jax.ShapeDtypeStruct((tp, 4 * LANE), F32), jax.ShapeDtypeStruct((tp, 4 * LANE), F32),
                   jax.ShapeDtypeStruct((tp, 8 * LANE), F32), jax.ShapeDtypeStruct((tp, 8 * LANE), F32),
                   jax.ShapeDtypeStruct((tp, LANE), F32), jax.ShapeDtypeStruct((LANE, 4 * LANE), F32),
                   jax.ShapeDtypeStruct((1, 4 * LANE), F32), jax.ShapeDtypeStruct((1, GLA_DV), F32)),
        scratch_shapes=[pltpu.VMEM((GLA_HEADS, GLA_DK, GLA_DV), F32)],
    )(proj, proj, proj, proj, proj, w_gate_up, b_gate, norm_w, ssave, do)


def _even_proj_weight(w_in):
    hd = SWA_HEAD_DIM
    k0, k1 = w_in[:, 512:512 + hd], w_in[:, 512 + hd:640]
    v0, v1 = w_in[:, 640:640 + hd], w_in[:, 640 + hd:768]
    zeros = jnp.zeros((w_in.shape[0], LANE - 2 * DN_HEADS), w_in.dtype)
    return jnp.concatenate([w_in[:, :512], k0, k0, k1, k1, v0, v0, v1, v1, w_in[:, 768:2816], w_in[:, 2820:2824],
                            w_in[:, 2816:2820], zeros], axis=1)


def _even_proj_weight_grad(dw):
    hd = SWA_HEAD_DIM
    c = lambda i: dw[:, 512 + i * hd:512 + (i + 1) * hd]
    return jnp.concatenate([dw[:, :512], c(0) + c(1), c(2) + c(3), c(4) + c(5), c(6) + c(7), dw[:, 1024:3072],
                            dw[:, 3076:3080], dw[:, 3072:3076]], axis=1)


def _ffn_fwd(h, nw_in, nw_out, wts, idx, get_w):
    w_gu = wts[f"w_gu{idx}"]
    hn, g, u, a = rms_mm(h, nw_in, w_gu[0], swiglu=True, name=f"ffn_up_{idx}", widx=w_gu[1])
    wts.update(get_w(f"down{idx}", a))
    w_down = wts[f"w_down{idx}"]
    f, h_out = mm_rms_res([a], w_down[0], h, nw_out, scale=0.5, name=f"ffn_down_{idx}", widx=w_down[1])
    return h_out, (h, hn, g, u, a, f)


def _ffn_bwd(dho, saved, nw_in, nw_out, w_gu, w_down, idx, on_grads):
    h, hn, g, u, a, f = saved
    df, dnw_out, dgu = mm_rms_res_bwd(dho, f, nw_out, w_down[0], (g, u), scale=0.5, name=f"ffn_down_bwd_{idx}",
                                      widx=w_down[1])
    g_gu = mm_tn(hn, dgu, name=f"ffn_dwgu_{idx}", out_dtype=BF16)
    g_down = mm_tn(a, df, name=f"ffn_dwd_{idx}", out_dtype=BF16)
    sent = on_grads(g_gu, g_down)
    dh, dnw_in = rms_mm_bwd([dgu], w_gu[0], h, nw_in + sent, dho, name=f"ffn_up_bwd_{idx}", widx=w_gu[1])
    return dh, dnw_in, dnw_out


def local_step(x, target, wts, get_w=None, put_g=None):
    seq, d = x.shape
    wts = dict(wts)
    get_w = get_w or (lambda stage, after: {})
    put_g = put_g or (lambda stage, grads: jnp.zeros((1, 1), F32))
    row = lambda v: v.reshape(1, -1)
    lane_row = lambda v: jnp.pad(v.reshape(1, -1), ((0, 0), (0, LANE - v.size)))
    nw = wts["norm_w"]
    h = jnp.concatenate([jnp.zeros((PAD, d), F32), wts["meta_tokens"], x], axis=0)
    buckets = _swa_buckets()
    bias = swa_bias(wts["rel_bias_table"], buckets, name="swa_bias")
    sinks = lane_row(wts["swa_sinks"])
    a_log, dt_bias = lane_row(wts["dn_a_log"]), lane_row(wts["dn_dt_bias"])
    dn_norm_w = row(wts["dn_norm_w"])
    conv_w = wts["even_conv_w"][0]
    w_gate_up = jnp.pad(wts["gla_w_gate_up"][0], ((0, LANE - GLA_GATE_RANK), (0, 0)))
    b_gate, gla_norm_w = row(wts["gla_b_gate"]), row(wts["gla_norm_w"])

    saved = []
    w_in, w_out = [None, None], [None, None]
    for l in range(2):
        if l == 1:
            wts.update(get_w("layer1", h))
            w_in[1] = jnp.pad(wts["odd_w_in"], ((0, 0), (0, PROJ_DIM - wts["odd_w_in"].shape[1])))
            w_out[1] = wts["odd_w_out"]
        h, s_a = _ffn_fwd(h, row(nw[l, 0]), row(nw[l, 1]), wts, 2 * l, get_w)
        if l == 0:
            wts.update(get_w("even", h))
            w_in[0], w_out[0] = _even_proj_weight(wts["even_w_in"]), wts["even_w_out"]
        h_mix = h
        hn, proj = rms_mm(h, row(nw[l, 2]), w_in[l], swiglu=False, name=f"mix_in_{l}")
        if l == 0:
            o_a = swa_fwd(proj, bias, sinks, name="swa_fwd")
            y = conv_fwd(proj, conv_w, name="conv_fwd")
            o_b, ssave, isave = dn_fwd(y, proj, a_log, dt_bias, dn_norm_w, name="dn_fwd")
            acts, extra = [o_a, o_b], (y, ssave, isave)
        else:
            o, ssave = gla_fwd(proj, w_gate_up, b_gate, gla_norm_w, name="gla_fwd")
            acts, extra = [o], (ssave,)
        mix, h = mm_rms_res(acts, w_out[l], h, row(nw[l, 3]), scale=1.0, name=f"mix_out_{l}")
        s_m = (h_mix, hn, proj, acts, extra, mix)
        if l == 0:
            wts.update(get_w("ffn1", h))
        h, s_b = _ffn_fwd(h, row(nw[l, 4]), row(nw[l, 5]), wts, 2 * l + 1, get_w)
        saved.append((s_a, s_m, s_b))

    dh, loss = loss_and_grad(h, target, name="loss")

    grads = {}
    dnw = [[None] * 6 for _ in range(2)]
    stage_of = {3: "ffn3", 2: "layer1", 1: "ffn1", 0: "ffn0"}

    def on_grads(i):
        def put(g_gu, g_down):
            grads[f"g_gu{i}"], grads[f"g_down{i}"] = g_gu, g_down
            return put_g(stage_of[i], grads)
        return put

    for l in (1, 0):
        s_a, s_m, s_b = saved[l]
        i = 2 * l + 1
        dh, dnw[l][4], dnw[l][5] = _ffn_bwd(dh, s_b, row(nw[l, 4]), row(nw[l, 5]), wts[f"w_gu{i}"], wts[f"w_down{i}"],
                                            i, on_grads(i))
        h_mix, hn, proj, acts, extra, mix = s_m
        dmix, dnw[l][3], do = mm_rms_res_bwd(dh, mix, row(nw[l, 3]), w_out[l], None, scale=1.0, name=f"mix_out_bwd_{l}")
        dw_out = jnp.concatenate([mm_tn(a, dmix, name=f"mix_dwo_{l}_{i}") for i, a in enumerate(acts)], axis=0)
        sent = jnp.zeros((1, 1), F32)
        if l == 0:
            y, ssave, isave = extra
            dq, dkv, dbias, dsinks = swa_bwd(proj, bias, sinks, do, name="swa_bwd")
            dy, dz, dsmall, da_log, ddt_bias, ddn_norm_w = dn_bwd(y, proj, a_log, dt_bias, dn_norm_w, ssave, isave, do,
                                                                    name="dn_bwd")
            dxc, dconv_w = conv_bwd(proj, conv_w, dy, name="conv_bwd")
            dps = [dq, dkv, dxc, dz, dsmall]
            grads["rel_bias_table"] = swa_bias_bwd(dbias, buckets, name="swa_bias_bwd")[:, :SWA_Q_HEADS]
            grads["swa_sinks"] = dsinks[:, :SWA_Q_HEADS]
            grads["dn_a_log"] = da_log[:, :DN_HEADS]
            grads["dn_dt_bias"] = ddt_bias[:, :DN_HEADS]
            grads["dn_norm_w"] = ddn_norm_w
            grads["even_conv_w"] = dconv_w[None]
            grads["even_w_out"] = dw_out
        else:
            (ssave,) = extra
            dq, dk, dv, dgate, dlow, dwgu, dbg, dgnw = gla_bwd(proj, w_gate_up, b_gate, gla_norm_w, ssave, do,
                                                               name="gla_bwd")
            dps = [dq, dk, dv, dgate, dlow]
            grads["gla_w_gate_up"] = dwgu[None, :GLA_GATE_RANK]
            grads["gla_b_gate"] = dbg
            grads["gla_norm_w"] = dgnw
            grads["odd_w_out"] = dw_out
        dw_in = jnp.concatenate([mm_tn(hn, dp, name=f"mix_dwi_{l}_{i}") for i, dp in enumerate(dps)], axis=1)
        if l == 0:
            grads["even_w_in"] = _even_proj_weight_grad(dw_in)
            sent = put_g("even", grads)
        else:
            grads["odd_w_in"] = dw_in[:, :wts["odd_w_in"].shape[1]]
        dh, dnw[l][2] = rms_mm_bwd(dps, w_in[l], h_mix, row(nw[l, 2]) + sent, dh, name=f"mix_in_bwd_{l}")
        i = 2 * l
        dh, dnw[l][0], dnw[l][1] = _ffn_bwd(dh, s_a, row(nw[l, 0]), row(nw[l, 1]), wts[f"w_gu{i}"], wts[f"w_down{i}"],
                                            i, on_grads(i))

    grads["norm_w"] = jnp.stack([jnp.concatenate(r, axis=0) for r in dnw])
    grads["meta_tokens"] = dh[PAD:PAD + N_META]
    return loss[0, 0], dh[PAD + N_META:], grads


def _peer(k):
    x, y, c = (lax.axis_index(a) for a in AXES)
    flip = lambda v, bit: 1 - v if bit else v
    return (flip(x, k & 4), flip(y, k & 2), flip(c, k & 1))


def _my_index():
    x, y, c = (lax.axis_index(a) for a in AXES)
    return 4 * x + 2 * y + c


_HBM = pl.BlockSpec(memory_space=pltpu.HBM)
_SEM = pl.BlockSpec(memory_space=pltpu.SEMAPHORE)
_EFFECT = pltpu.SideEffectType.DATAFLOW_SIDE_EFFECTING


def _remote_copies(items, src_refs, land_refs, send_sems, recv_sems):
    me = _my_index()
    copies = []
    for k in range(1, N_DEV):
        px, py, pc = _peer(k)
        pj = 4 * px + 2 * py + pc
        for a, (sn, send, ln, land, _) in enumerate(items):
            sem = (k - 1) * len(items) + a
            copies.append(pltpu.make_async_remote_copy(
                src_ref=send(src_refs[sn], pj), dst_ref=land(land_refs[ln], me), send_sem=send_sems.at[sem],
                recv_sem=recv_sems.at[sem], device_id=(px, py, pc), device_id_type=MESH))
    return copies


def exchange(srcs, lands, items, *, name):
    sn, ln = list(srcs), list(lands)

    def body(*refs):
        src_refs = dict(zip(sn, refs[:len(sn)]))
        land_refs = dict(zip(ln, refs[len(sn) + len(ln):len(sn) + 2 * len(ln)]))
        send_sems, recv_sems = refs[len(sn) + 2 * len(ln):]
        copies = _remote_copies(items, src_refs, land_refs, send_sems, recv_sems)
        for cp in copies:
            cp.start()
        for cp in copies:
            cp.wait_recv()
        for cp in copies:
            cp.wait_send()

    n_remote = (N_DEV - 1) * len(items)
    outs = pl.pallas_call(
        body, name=name,
        in_specs=[pl.BlockSpec(memory_space=pl.ANY)] * (len(sn) + len(ln)),
        out_specs=tuple(pl.BlockSpec(memory_space=pl.ANY) for _ in ln),
        out_shape=tuple(jax.ShapeDtypeStruct(lands[n].shape, lands[n].dtype) for n in ln),
        input_output_aliases={len(sn) + i: i for i in range(len(ln))},
        scratch_shapes=[pltpu.SemaphoreType.DMA((n_remote,)), pltpu.SemaphoreType.DMA((n_remote,))],
    )(*[srcs[n] for n in sn], *[lands[n] for n in ln])
    return dict(zip(ln, outs))


def start_copies(srcs, lands, items, *, name):
    sn, ln = list(srcs), list(lands)
    n_remote = (N_DEV - 1) * len(items)

    def body(*refs):
        src_refs = dict(zip(sn, refs[:len(sn)]))
        land_refs = dict(zip(ln, refs[len(sn):len(sn) + len(ln)]))
        send_sems, recv_sems = refs[len(sn) + len(ln):len(sn) + len(ln) + 2]
        token = refs[-1]
        for cp in _remote_copies(items, src_refs, land_refs, send_sems, recv_sems):
            cp.start()
        token[...] = jnp.zeros_like(token)

    hbm = lambda a: pltpu.with_memory_space_constraint(a, pltpu.HBM)
    outs = pl.pallas_call(
        body, name=name,
        in_specs=[_HBM] * (len(sn) + len(ln)),
        out_specs=(_SEM, _SEM) + (_HBM,) * len(ln) + (pl.BlockSpec(memory_space=pltpu.VMEM),),
        out_shape=(pltpu.SemaphoreType.DMA((n_remote,)), pltpu.SemaphoreType.DMA((n_remote,)))
        + tuple(pltpu.HBM(lands[n].shape, lands[n].dtype) for n in ln) + (jax.ShapeDtypeStruct((8, LANE), F32),),
        input_output_aliases={len(sn) + i: 2 + i for i in range(len(ln))},
        compiler_params=pltpu.CompilerParams(has_side_effects=_EFFECT),
    )(*[hbm(srcs[n]) for n in sn], *[hbm(lands[n]) for n in ln])
    return (outs[0], outs[1]), dict(zip(ln, outs[2:2 + len(ln)])), outs[-1][0:1, 0:1]


def wait_copies(sems, srcs, lands, items, after, *, name):
    sn, ln = list(srcs), list(lands)

    def body(*refs):
        src_refs = dict(zip(sn, refs[:len(sn)]))
        land_refs = dict(zip(ln, refs[len(sn):len(sn) + len(ln)]))
        send_sems, recv_sems = refs[len(sn) + len(ln):len(sn) + len(ln) + 2]
        copies = _remote_copies(items, src_refs, land_refs, send_sems, recv_sems)
        for cp in copies:
            cp.wait_send()
        for cp in copies:
            cp.wait_recv()

    outs = pl.pallas_call(
        body, name=name,
        in_specs=[_HBM] * (len(sn) + len(ln)) + [_SEM, _SEM, pl.BlockSpec(memory_space=pl.ANY)],
        out_specs=(_HBM,) * len(ln),
        out_shape=tuple(pltpu.HBM(lands[n].shape, lands[n].dtype) for n in ln),
        input_output_aliases={len(sn) + i: i for i in range(len(ln))},
        compiler_params=pltpu.CompilerParams(has_side_effects=_EFFECT),
    )(*[srcs[n] for n in sn], *[lands[n] for n in ln], sems[0], sems[1], after)
    return dict(zip(ln, outs))


def _block(index, size):
    return pl.ds(pl.multiple_of(index * size, LANE), size)


def _adam_tile(rows):
    for t in (256, 176, 128):
        if rows % t == 0:
            return t
    return rows


def sum_adamw(recvs, col0, w, m, v, *, name):
    b, r, c = w.shape
    cp = recvs[0].shape[-1]
    tr = _adam_tile(r)
    c1 = 1.0 / (1.0 - ADAM_B1 ** ADAM_STEP)
    c2 = 1.0 / (1.0 - ADAM_B2 ** ADAM_STEP)

    def body(*refs):
        recv_refs = refs[:b]
        w_ref, m_ref, v_ref, g_ref, d_ref, nm_ref, nv_ref = refs[b:]
        for slab, recv_ref in enumerate(recv_refs):
            @pl.when(pl.program_id(0) == slab)
            def _():
                g = recv_ref[0, :, col0:col0 + c].astype(F32)
                for i in range(1, N_DEV):
                    g = g + recv_ref[i, :, col0:col0 + c].astype(F32)
                nm = ADAM_B1 * m_ref[0] + (1.0 - ADAM_B1) * g
                nv = ADAM_B2 * v_ref[0] + (1.0 - ADAM_B2) * (g * g)
                g_ref[0] = g
                nm_ref[0] = nm
                nv_ref[0] = nv
                d_ref[0] = -ADAM_LR * ((nm * c1) / (jnp.sqrt(nv * c2) + ADAM_EPS) + ADAM_WD * w_ref[0])

    tile = pl.BlockSpec((1, tr, c), lambda bi, i: (bi, i, 0))
    piece = lambda slab: pl.BlockSpec((N_DEV, tr, cp), lambda bi, i: (0, jnp.where(bi == slab, i, 0), 0))
    return pl.pallas_call(
        body, name=name, grid=(b, r // tr),
        in_specs=[piece(slab) for slab in range(b)] + [tile, tile, tile],
        out_specs=(tile,) * 4, out_shape=(jax.ShapeDtypeStruct((b, r, c), F32),) * 4,
    )(*recvs, w, m, v)


def _flat_rows(n_elems, row_multiple):
    rows = -(-n_elems // FLAT_COLS)
    return -(-rows // row_multiple) * row_multiple


def _pack(arrays, row_multiple, dtype):
    flat = jnp.concatenate([a.reshape(-1).astype(dtype) for a in arrays])
    rows = _flat_rows(flat.size, row_multiple)
    return jnp.pad(flat, (0, rows * FLAT_COLS - flat.size)).reshape(rows, FLAT_COLS)


def _unpack(flat2d, shapes):
    lead = flat2d.shape[:-2]
    flat = flat2d.reshape(lead + (-1,))
    out, off = [], 0
    for shp in shapes:
        n = int(np.prod(shp))
        out.append(flat[..., off:off + n].reshape(lead + tuple(shp)))
        off += n
    return out


def _join_shards(stacked, axis):
    moved = jnp.moveaxis(stacked, 0, axis)
    shp = list(moved.shape)
    shp[axis:axis + 2] = [shp[axis] * shp[axis + 1]]
    return moved.reshape(shp)


def _split_shards(full, axis):
    shp = list(full.shape)
    shp[axis:axis + 1] = [N_DEV, shp[axis] // N_DEV]
    return jnp.moveaxis(full.reshape(shp), axis, 0)


def kernel(x, meta_tokens, norm_w, ffn_w_gate, ffn_w_up, ffn_w_down, rel_bias_table, even_w_in, even_conv_w, swa_sinks, dn_a_log, dn_dt_bias, dn_norm_w, even_w_out, odd_w_in, gla_w_gate_up, gla_b_gate, gla_norm_w, odd_w_out, loss_target, m_meta_tokens, m_norm_w, m_ffn_w_gate, m_ffn_w_up, m_ffn_w_down, m_rel_bias_table, m_even_w_in, m_even_conv_w, m_swa_sinks, m_dn_a_log, m_dn_dt_bias, m_dn_norm_w, m_even_w_out, m_odd_w_in, m_gla_w_gate_up, m_gla_b_gate, m_gla_norm_w, m_odd_w_out, v_meta_tokens, v_norm_w, v_ffn_w_gate, v_ffn_w_up, v_ffn_w_down, v_rel_bias_table, v_even_w_in, v_even_conv_w, v_swa_sinks, v_dn_a_log, v_dn_dt_bias, v_dn_norm_w, v_even_w_out, v_odd_w_in, v_gla_w_gate_up, v_gla_b_gate, v_gla_norm_w, v_odd_w_out):
    args = locals()
    w = {n: args[n] for n in WEIGHTS}
    m = {n: args["m_" + n] for n in WEIGHTS}
    v = {n: args["v_" + n] for n in WEIGHTS}

    d = D_MODEL
    me = _my_index()
    whole = lambda ref, j: ref
    cols = lambda size, base=0: (lambda ref, i: ref.at[(slice(None),) * (len(ref.shape) - 1)
                                                       + (pl.ds(pl.multiple_of(base + i * size, LANE), size),)])
    rows3 = lambda size: (lambda ref, i: ref.at[:, _block(i, size), :])
    rows2 = lambda size: (lambda ref, i: ref.at[_block(i, size), :])
    lead = lambda ref, i: ref.at[i]
    of_group = lambda items, g: [it for it in items if it[4] == g]
    names = lambda items, k: list(dict.fromkeys(it[k] for it in items))

    def placed(shape, dtype, parts):
        land = lax.empty(shape, dtype)
        for part, axis, start in parts:
            land = lax.dynamic_update_slice(land, part, tuple(start if a == axis else 0 for a in range(land.ndim)))
        return land

    pad_cols = lambda a, to: jnp.pad(a, [(0, 0)] * (a.ndim - 1) + [(0, to - a.shape[-1])])
    gate_s = pad_cols(w["ffn_w_gate"].reshape(N_FFN, d, FF_SHARD), FF_SHARD_PAD).astype(BF16)
    up_s = pad_cols(w["ffn_w_up"].reshape(N_FFN, d, FF_SHARD), FF_SHARD_PAD).astype(BF16)
    down_s = jnp.pad(w["ffn_w_down"].reshape(N_FFN, FF_SHARD, d),
                     ((0, 0), (0, FF_SHARD_PAD - FF_SHARD), (0, 0))).astype(BF16)
    small_s = _pack([w[n] for n in SMALL], 8, F32)
    srcs_w = {"ein": pad_cols(w["even_w_in"][0], EVEN_IN_SHARD_PAD).astype(BF16), "eout": w["even_w_out"][0].astype(BF16),
              "small": small_s, "oin": pad_cols(w["odd_w_in"][0], ODD_IN_SHARD_PAD).astype(BF16),
              "oout": w["odd_w_out"][0].astype(BF16)}
    lands_w = {"ein": placed((d, N_DEV * EVEN_IN_SHARD_PAD), BF16, [(srcs_w["ein"], 1, me * EVEN_IN_SHARD_PAD)]),
               "oin": placed((d, N_DEV * ODD_IN_SHARD_PAD), BF16, [(srcs_w["oin"], 1, me * ODD_IN_SHARD_PAD)]),
               "eout": placed((d, d), BF16, [(srcs_w["eout"], 0, me * OUT_SHARD)]),
               "oout": placed((d, d), BF16, [(srcs_w["oout"], 0, me * OUT_SHARD)]),
               "small": placed((N_DEV,) + small_s.shape, F32, [(small_s[None], 0, me)])}
    items_w = [("small", whole, "small", lead, "first"), ("ein", whole, "ein", cols(EVEN_IN_SHARD_PAD), "even"),
               ("eout", whole, "eout", rows2(OUT_SHARD), "even"), ("oin", whole, "oin", cols(ODD_IN_SHARD_PAD), "layer1"),
               ("oout", whole, "oout", rows2(OUT_SHARD), "layer1")]
    for tag, pick, group, down_group in (("0", 0, "first", "down0"), ("1", 1, "ffn1", "ffn1"),
                                         ("23", slice(2, 4), "layer1", "layer1")):
        lead_dims = (2,) if tag == "23" else ()
        nl = len(lead_dims)
        srcs_w.update({f"gate{tag}": gate_s[pick], f"up{tag}": up_s[pick], f"down{tag}": down_s[pick]})
        lands_w[f"w_gu{tag}"] = placed(lead_dims + (d, 2 * FF_PAD), BF16,
                                       [(srcs_w[f"gate{tag}"], nl + 1, me * FF_SHARD_PAD),
                                        (srcs_w[f"up{tag}"], nl + 1, FF_PAD + me * FF_SHARD_PAD)])
        lands_w[f"w_down{tag}"] = placed(lead_dims + (FF_PAD, d), BF16, [(srcs_w[f"down{tag}"], nl, me * FF_SHARD_PAD)])
        items_w += [(f"gate{tag}", whole, f"w_gu{tag}", cols(FF_SHARD_PAD), group),
                    (f"up{tag}", whole, f"w_gu{tag}", cols(FF_SHARD_PAD, FF_PAD), group),
                    (f"down{tag}", whole, f"w_down{tag}", (rows3 if lead_dims else rows2)(FF_SHARD_PAD), down_group)]
    pending = {}
    for g in ("first", "down0", "even", "ffn1", "layer1"):
        its = of_group(items_w, g)
        srcs = {n: srcs_w[n] for n in names(its, 0)}
        sems, lands, _ = start_copies(srcs, {n: lands_w[n] for n in names(its, 2)}, its, name=f"gather_start_{g}")
        pending[g] = (sems, srcs, lands, its)

    unpad = lambda p, shard, shard_pad: p.reshape(d, N_DEV, shard_pad)[:, :, :shard].reshape(d, N_DEV * shard)

    def get_w(stage, after):
        if stage not in pending:
            return {}
        sems, srcs, lands, its = pending[stage]
        landed = wait_copies(sems, srcs, lands, its, after, name=f"gather_wait_{stage}")
        if stage == "first":
            got = {"w_gu0": (landed["w_gu0"], None)}
            for n, stacked in zip(SMALL, _unpack(landed["small"], [w[n].shape for n in SMALL])):
                got[n] = _join_shards(stacked, SHARD_AXIS[n])
            return got
        if stage == "down0":
            return {"w_down0": (landed["w_down0"], None)}
        if stage == "even":
            return {"even_w_out": landed["eout"], "even_w_in": unpad(landed["ein"], EVEN_IN_SHARD, EVEN_IN_SHARD_PAD)}
        if stage == "ffn1":
            return {"w_gu1": (landed["w_gu1"], None), "w_down1": (landed["w_down1"], None)}
        return {"w_gu2": (landed["w_gu23"], 0), "w_gu3": (landed["w_gu23"], 1), "w_down2": (landed["w_down23"], 0),
                "w_down3": (landed["w_down23"], 1), "odd_w_out": landed["oout"],
                "odd_w_in": unpad(landed["oin"], ODD_IN_SHARD, ODD_IN_SHARD_PAD)}

    full = {n: w[n] for n in REPL}
    full.update(get_w("first", x))

    repad = lambda g, shard, shard_pad: pad_cols(g.reshape(d, N_DEV, shard), shard_pad).reshape(d, N_DEV * shard_pad)
    mine = lambda g, axis, size, base=0: lax.dynamic_slice_in_dim(g, base + me * size, size, axis)
    half = lambda h: (lambda ref, i: ref.at[i, :, pl.ds(h * FF_SHARD_PAD, FF_SHARD_PAD)])
    items_g = [("oin", cols(ODD_IN_SHARD_PAD), "r_oin", lead, "layer1"), ("oout", rows2(OUT_SHARD), "r_oout", lead, "layer1"),
               ("ein", cols(EVEN_IN_SHARD_PAD), "r_ein", lead, "even"), ("eout", rows2(OUT_SHARD), "r_eout", lead, "even"),
               ("small", lead, "r_small", lead, "last")]
    for i, group in ((3, "ffn3"), (2, "layer1"), (1, "ffn1"), (0, "ffn0")):
        items_g += [(f"g_gu{i}", cols(FF_SHARD_PAD), f"r_gu{i}", half(0), group),
                    (f"g_gu{i}", cols(FF_SHARD_PAD, FF_PAD), f"r_gu{i}", half(1), group),
                    (f"g_down{i}", rows2(FF_SHARD_PAD), f"r_down{i}", lead, group)]

    def grad_src(n, grads):
        if n == "oin":
            return repad(grads["odd_w_in"], ODD_IN_SHARD, ODD_IN_SHARD_PAD).astype(BF16)
        if n == "ein":
            return repad(grads["even_w_in"], EVEN_IN_SHARD, EVEN_IN_SHARD_PAD).astype(BF16)
        if n in ("oout", "eout"):
            return grads["odd_w_out" if n == "oout" else "even_w_out"].astype(BF16)
        return grads[n]

    def grad_land(n, srcs):
        if n.startswith("r_gu"):
            g = srcs["g_gu" + n[4:]]
            own = jnp.concatenate([mine(g, 1, FF_SHARD_PAD), mine(g, 1, FF_SHARD_PAD, FF_PAD)], axis=1)
        elif n.startswith("r_down"):
            own = mine(srcs["g_down" + n[6:]], 0, FF_SHARD_PAD)
        elif n == "r_small":
            own = lax.dynamic_index_in_dim(srcs["small"], me, 0, keepdims=False)
        else:
            axis, size = {"r_oin": (1, ODD_IN_SHARD_PAD), "r_ein": (1, EVEN_IN_SHARD_PAD), "r_oout": (0, OUT_SHARD),
                          "r_eout": (0, OUT_SHARD)}[n]
            own = mine(srcs[n[2:]], axis, size)
        return placed((N_DEV,) + own.shape, own.dtype, [(own[None], 0, me)])

    sent = {}

    def put_g(stage, grads):
        its = of_group(items_g, stage)
        srcs = {n: grad_src(n, grads) for n in names(its, 0)}
        lands = {n: grad_land(n, srcs) for n in names(its, 2)}
        sems, lands, token = start_copies(srcs, lands, its, name=f"grads_start_{stage}")
        sent[stage] = (sems, srcs, lands, its)
        return token

    loss, grad_x, grads = local_step(x[0], loss_target[0], full, get_w, put_g)
    loss = lax.psum(loss, AXES)

    order = SMALL + REPL
    pieces = [_split_shards(grads[n].reshape(full[n].shape), SHARD_AXIS[n]) if n in SHARD_AXIS
              else jnp.broadcast_to(grads[n].reshape(w[n].shape)[None], (N_DEV,) + w[n].shape) for n in order]
    flat = jnp.concatenate([p.reshape(N_DEV, -1) for p in pieces], axis=1)
    srows = _flat_rows(flat.shape[1], 8)
    grads["small"] = jnp.pad(flat, ((0, 0), (0, srows * FLAT_COLS - flat.shape[1]))).reshape(N_DEV, srows, FLAT_COLS)
    recv = {}
    for stage, (sems, srcs, lands, its) in sent.items():
        recv.update(wait_copies(sems, srcs, lands, its, grad_x, name=f"grads_wait_{stage}"))
    its = of_group(items_g, "last")
    srcs = {"small": grads["small"]}
    recv.update(exchange(srcs, {"r_small": grad_land("r_small", srcs)}, its, name="exchange_small"))

    result = [{} for _ in range(4)]

    def update(n, recvs, col0, view):
        outs = sum_adamw(recvs, col0, view(w[n]), view(m[n]), view(v[n]), name=f"adamw_{n}")
        for r, o in zip(result, outs):
            r[n] = o.reshape(w[n].shape)

    per_ffn = lambda name: [recv[f"{name}{i}"] for i in range(N_FFN)]
    update("ffn_w_gate", per_ffn("r_gu"), 0, lambda a: a.reshape(N_FFN, d, FF_SHARD))
    update("ffn_w_up", per_ffn("r_gu"), FF_SHARD_PAD, lambda a: a.reshape(N_FFN, d, FF_SHARD))
    update("ffn_w_down", per_ffn("r_down"), 0, lambda a: a.reshape(N_FFN, FF_SHARD, d))
    for n, r in (("even_w_in", "r_ein"), ("odd_w_in", "r_oin"), ("even_w_out", "r_eout"), ("odd_w_out", "r_oout")):
        update(n, [recv[r]], 0, lambda a: a)
    pack_local = lambda t: _pack([t[n] for n in order], 8, F32)[None]
    small_outs = sum_adamw([recv["r_small"]], 0, pack_local(w), pack_local(m), pack_local(v), name="adamw_small")
    for r, o in zip(result, small_outs):
        r.update(zip(order, _unpack(o[0], [w[n].shape for n in order])))
    return (loss, grad_x[None], *[r[n] for r in result for n in WEIGHTS])
```

```python
import functools
import math

import numpy as np
import jax
import jax.numpy as jnp
from jax import lax
from jax.experimental import pallas as pl
from jax.experimental.pallas import tpu as pltpu

F32 = jnp.float32
BF16 = jnp.bfloat16
MESH = pl.DeviceIdType.MESH
AXES = ("x", "y", "c")
N_DEV = 8

D_MODEL = 1024
N_META = 16
D_FF = 2816
NORM_EPS = 1e-6
NEG_INF = -1e30
SWA_Q_HEADS = 8
SWA_HEAD_DIM = 64
SWA_WINDOW = 128
SWA_BLOCK = 128
REL_BUCKETS = 32
REL_MAX_DIST = 128
DN_HEADS = 4
DN_HEAD_DIM = 128
DN_CONV = 4
GLA_HEADS = 4
GLA_DK = 128
GLA_DV = 256
GLA_GATE_RANK = 16
GLA_GATE_NORM = 16.0
CHUNK = 64
PAD = SWA_BLOCK - N_META
LANE = 128
PROJ_DIM = 3200

ADAM_LR = 0.001
ADAM_B1 = 0.9
ADAM_B2 = 0.999
ADAM_EPS = 1e-08
ADAM_WD = 0.01
ADAM_STEP = 10

FF_SHARD = D_FF // N_DEV
FF_SHARD_PAD = 384
FF_PAD = N_DEV * FF_SHARD_PAD
N_FFN = 4
EVEN_IN_SHARD, EVEN_IN_SHARD_PAD = 353, 384
ODD_IN_SHARD, ODD_IN_SHARD_PAD = 386, 512
OUT_SHARD = D_MODEL // N_DEV

FLAT_COLS = 128
BIG = ("ffn_w_gate", "ffn_w_up", "ffn_w_down", "even_w_in", "even_w_out", "odd_w_in", "odd_w_out")
SMALL = ("meta_tokens", "norm_w", "even_conv_w", "gla_w_gate_up", "gla_b_gate", "gla_norm_w")
REPL = ("rel_bias_table", "swa_sinks", "dn_a_log", "dn_dt_bias", "dn_norm_w")
WEIGHTS = ("meta_tokens", "norm_w", "ffn_w_gate", "ffn_w_up", "ffn_w_down", "rel_bias_table", "even_w_in",
           "even_conv_w", "swa_sinks", "dn_a_log", "dn_dt_bias", "dn_norm_w", "even_w_out", "odd_w_in",
           "gla_w_gate_up", "gla_b_gate", "gla_norm_w", "odd_w_out")
SHARD_AXIS = {"ffn_w_gate": 3, "ffn_w_up": 3, "ffn_w_down": 2, "even_w_in": 2, "even_w_out": 1, "odd_w_in": 2,
              "odd_w_out": 1, "meta_tokens": 1, "norm_w": 2, "even_conv_w": 2, "gla_w_gate_up": 2,
              "gla_b_gate": 1, "gla_norm_w": 1}


def _rms(x, w):
    r = lax.rsqrt(jnp.mean(x * x, axis=-1, keepdims=True) + NORM_EPS)
    return x * r * w


def _sigmoid(x):
    return 0.5 * (jnp.tanh(0.5 * x) + 1.0)


def _silu(x):
    return x * _sigmoid(x)


def _softplus(x):
    pos = x > 0
    return jnp.where(pos, x, 0.0) + jnp.log(1.0 + jnp.exp(jnp.where(pos, -x, x)))


def _l2n(x):
    return x * lax.rsqrt(jnp.sum(x * x, axis=-1, keepdims=True) + 1e-6)


def _split_bf16(x):
    hi = x.astype(BF16)
    return hi, (x - hi.astype(F32)).astype(BF16)


def _make_mm(terms, batched):
    off = 1 if batched else 0
    bdims = ((0,), (0,)) if batched else ((), ())

    def dg(a, b, ca, cb):
        dot = lambda p, q: lax.dot_general(p, q, (((ca + off,), (cb + off,)), bdims), preferred_element_type=F32)
        a_hi, a_lo = _split_bf16(a)
        b_hi, b_lo = _split_bf16(b)
        if terms == 1:
            return dot(a_hi, b_hi)
        return dot(a_hi, b_hi) + (dot(a_hi, b_lo) + dot(a_lo, b_hi))

    @jax.custom_vjp
    def nn(a, b):
        return dg(a, b, 1, 0)

    @jax.custom_vjp
    def nt(a, b):
        return dg(a, b, 1, 1)

    @jax.custom_vjp
    def tn(a, b):
        return dg(a, b, 0, 0)

    nn.defvjp(lambda a, b: (nn(a, b), (a, b)), lambda r, g: (nt(g, r[1]), tn(r[0], g)))
    nt.defvjp(lambda a, b: (nt(a, b), (a, b)), lambda r, g: (nn(g, r[1]), tn(g, r[0])))
    tn.defvjp(lambda a, b: (tn(a, b), (a, b)), lambda r, g: (nt(r[1], g), nn(r[0], g)))
    return nn, nt, tn


_mm, _mm_nt, _mm_tn = _make_mm(1, False)
_mm3, _, _ = _make_mm(3, False)
_bmm, _bmm_nt, _bmm_tn = _make_mm(1, True)
_bmm3, _bmm3_nt, _bmm3_tn = _make_mm(3, True)


@jax.custom_vjp
def _known_inverse(a, inv):
    return inv


_known_inverse.defvjp(lambda a, inv: (inv, inv),
                      lambda inv, g: (-_bmm3_tn(inv, _bmm3_nt(g, inv)), jnp.zeros_like(inv)))


def _tri_ones_dot(x, lower):
    n = x.shape[0]
    r = lax.broadcasted_iota(jnp.int32, (n, n), 0)
    c = lax.broadcasted_iota(jnp.int32, (n, n), 1)
    t = ((r >= c) if lower else (r <= c)).astype(BF16)
    hi, lo = _split_bf16(x)
    return jnp.dot(t, hi, preferred_element_type=F32) + jnp.dot(t, lo, preferred_element_type=F32)


@jax.custom_vjp
def _cumsum_rows(x):
    return _tri_ones_dot(x, True)


_cumsum_rows.defvjp(lambda x: (_tri_ones_dot(x, True), None), lambda _, g: (_tri_ones_dot(g, False),))


def _row_tile(n_rows, cap):
    best = LANE
    for t in range(LANE, cap + 1, LANE):
        if n_rows % t == 0:
            best = t
    return best


def _real_rows(tile_index, tm):
    row = tile_index * tm + lax.broadcasted_iota(jnp.int32, (tm, 1), 0)
    return (row >= PAD).astype(F32)


def _full(shape):
    return pl.BlockSpec(shape, lambda *_: (0,) * len(shape))


def _resident(shape):
    return pl.BlockSpec(shape, lambda *_: (0,) * len(shape), pipeline_mode=pl.Buffered(1))


def _resident_w(wmat, widx):
    if wmat.ndim == 2:
        return _resident(wmat.shape)
    return pl.BlockSpec((None,) + wmat.shape[1:], lambda *_: (widx, 0, 0), pipeline_mode=pl.Buffered(1))


def rms_mm(h, w, wmat, *, swiglu, name, widx=None):
    tp, d = h.shape
    n = wmat.shape[-1]
    tm = _row_tile(tp, 384)
    half = n // 2

    def body(h_ref, w_ref, wm_ref, hn_ref, *outs):
        hn = _rms(h_ref[...], w_ref[...]).astype(BF16)
        hn_ref[...] = hn
        p = jnp.dot(hn, wm_ref[...], preferred_element_type=F32)
        if swiglu:
            g, u = p[:, :half], p[:, half:]
            outs[0][...] = g.astype(BF16)
            outs[1][...] = u.astype(BF16)
            outs[2][...] = (_silu(g) * u).astype(BF16)
        else:
            outs[0][...] = p

    row = lambda width: pl.BlockSpec((tm, width), lambda i: (i, 0))
    if swiglu:
        out_shape = (jax.ShapeDtypeStruct((tp, d), BF16),) + (jax.ShapeDtypeStruct((tp, half), BF16),) * 3
        out_specs = (row(d), row(half), row(half), row(half))
    else:
        out_shape = (jax.ShapeDtypeStruct((tp, d), BF16), jax.ShapeDtypeStruct((tp, n), F32))
        out_specs = (row(d), row(n))
    return pl.pallas_call(
        body, name=name, grid=(tp // tm,),
        in_specs=[row(d), _full((1, d)), _resident_w(wmat, widx)],
        out_specs=out_specs, out_shape=out_shape,
    )(h, w, wmat)


def mm_rms_res(acts, wmat, h, w, *, scale, name, widx=None):
    tp, d = h.shape
    tm = _row_tile(tp, 384)
    widths = [a.shape[1] for a in acts]
    offs = [sum(widths[:i]) for i in range(len(acts))]
    na = len(acts)

    def body(*refs):
        a_refs = refs[:na]
        wm_ref, h_ref, w_ref, f_ref, ho_ref = refs[na:]
        f = None
        for a_ref, off, width in zip(a_refs, offs, widths):
            part = jnp.dot(a_ref[...].astype(BF16), wm_ref[off:off + width, :], preferred_element_type=F32)
            f = part if f is None else f + part
        f_ref[...] = f
        ho_ref[...] = h_ref[...] + scale * _rms(f, w_ref[...])

    row = lambda width: pl.BlockSpec((tm, width), lambda i: (i, 0))
    return pl.pallas_call(
        body, name=name, grid=(tp // tm,),
        in_specs=[row(wd) for wd in widths] + [_resident_w(wmat, widx), row(d), _full((1, d))],
        out_specs=(row(d), row(d)),
        out_shape=(jax.ShapeDtypeStruct((tp, d), F32), jax.ShapeDtypeStruct((tp, d), F32)),
    )(*acts, wmat, h, w)


def mm_rms_res_bwd(dho, f, w, wmat, gu, *, scale, name, widx=None):
    tp, d = f.shape
    k = wmat.shape[-2]
    tm = _row_tile(tp, 384)
    swiglu = gu is not None

    def body(*refs):
        if swiglu:
            dho_ref, f_ref, w_ref, wm_ref, g_ref, u_ref, df_ref, dw_ref, dgu_ref = refs
        else:
            dho_ref, f_ref, w_ref, wm_ref, df_ref, dw_ref, da_ref = refs
        i = pl.program_id(0)
        _, vjp = jax.vjp(lambda ff, ww: scale * _rms(ff, ww), f_ref[...], w_ref[...])
        df, dw = vjp(dho_ref[...])
        dfb = (df * _real_rows(i, tm)).astype(BF16)
        df_ref[...] = dfb

        @pl.when(i == 0)
        def _():
            dw_ref[...] = jnp.zeros_like(dw_ref)

        dw_ref[...] += dw
        da = lax.dot_general(dfb, wm_ref[...], (((1,), (1,)), ((), ())), preferred_element_type=F32)
        if swiglu:
            g = g_ref[...].astype(F32)
            u = u_ref[...].astype(F32)
            s = _sigmoid(g)
            dgu_ref[:, :k] = (da * u * s * (1.0 + g * (1.0 - s))).astype(BF16)
            dgu_ref[:, k:] = (da * g * s).astype(BF16)
        else:
            da_ref[...] = da

    row = lambda width: pl.BlockSpec((tm, width), lambda i: (i, 0))
    in_specs = [row(d), row(d), _full((1, d)), _resident_w(wmat, widx)]
    args = [dho, f, w, wmat]
    out_shape = [jax.ShapeDtypeStruct((tp, d), BF16), jax.ShapeDtypeStruct((1, d), F32)]
    out_specs = [row(d), _full((1, d))]
    if swiglu:
        in_specs += [row(k), row(k)]
        args += list(gu)
        out_shape += [jax.ShapeDtypeStruct((tp, 2 * k), BF16)]
        out_specs += [row(2 * k)]
    else:
        out_shape += [jax.ShapeDtypeStruct((tp, k), F32)]
        out_specs += [row(k)]
    return pl.pallas_call(body, name=name, grid=(tp // tm,), in_specs=in_specs, out_specs=tuple(out_specs),
                          out_shape=tuple(out_shape))(*args)


def rms_mm_bwd(dps, wmat, h, w, dho, *, name, widx=None):
    tp, d = h.shape
    tm = _row_tile(tp, 384)
    widths = [p.shape[1] for p in dps]
    offs = [sum(widths[:i]) for i in range(len(dps))]
    ndp = len(dps)

    def body(*refs):
        dp_refs = refs[:ndp]
        wm_ref, h_ref, w_ref, dho_ref, dh_ref, dw_ref = refs[ndp:]
        i = pl.program_id(0)
        dhn = None
        for dp_ref, off, width in zip(dp_refs, offs, widths):
            part = lax.dot_general(dp_ref[...].astype(BF16), wm_ref[:, off:off + width], (((1,), (1,)), ((), ())),
                                   preferred_element_type=F32)
            dhn = part if dhn is None else dhn + part
        _, vjp = jax.vjp(_rms, h_ref[...], w_ref[...])
        dx, dw = vjp(dhn)
        dh_ref[...] = (dho_ref[...] + dx) * _real_rows(i, tm)

        @pl.when(i == 0)
        def _():
            dw_ref[...] = jnp.zeros_like(dw_ref)

        dw_ref[...] += dw

    row = lambda width: pl.BlockSpec((tm, width), lambda i: (i, 0))
    return pl.pallas_call(
        body, name=name, grid=(tp // tm,),
        in_specs=[row(wd) for wd in widths] + [_resident_w(wmat, widx), row(d), _full((1, d)), row(d)],
        out_specs=(row(d), _full((1, d))),
        out_shape=(jax.ShapeDtypeStruct((tp, d), F32), jax.ShapeDtypeStruct((1, d), F32)),
    )(*dps, wmat, h, w, dho)


def mm_tn(a, b, *, name, out_dtype=F32):
    t, m = a.shape
    n = b.shape[1]
    bm = _row_tile(m, 512)
    bn = _row_tile(n, 1536)
    bk = _row_tile(t, 1408)
    nk = t // bk

    def body(a_ref, b_ref, o_ref, acc):
        @pl.when(pl.program_id(2) == 0)
        def _():
            acc[...] = jnp.zeros_like(acc)

        acc[...] += lax.dot_general(a_ref[...].astype(BF16), b_ref[...].astype(BF16), (((0,), (0,)), ((), ())),
                                    preferred_element_type=F32)

        @pl.when(pl.program_id(2) == nk - 1)
        def _():
            o_ref[...] = acc[...].astype(o_ref.dtype)

    return pl.pallas_call(
        body, name=name, grid=(m // bm, n // bn, nk),
        in_specs=[pl.BlockSpec((bk, bm), lambda i, j, kk: (kk, i)), pl.BlockSpec((bk, bn), lambda i, j, kk: (kk, j))],
        out_specs=pl.BlockSpec((bm, bn), lambda i, j, kk: (i, j)),
        out_shape=jax.ShapeDtypeStruct((m, n), out_dtype), scratch_shapes=[pltpu.VMEM((bm, bn), F32)],
    )(a, b)


def loss_and_grad(h, target, *, name):
    tp, d = h.shape
    tm = SWA_BLOCK

    def body(h_ref, t_ref, dh_ref, loss_ref):
        i = pl.program_id(0)

        @pl.when(i == 0)
        def _():
            loss_ref[...] = jnp.zeros_like(loss_ref)
            dh_ref[...] = jnp.zeros_like(dh_ref)

        @pl.when(i > 0)
        def _():
            err = h_ref[...] - t_ref[...]
            dh_ref[...] = err * (1.0 / d)
            loss_ref[...] += 0.5 * jnp.sum(jnp.sum(err * err, axis=1, keepdims=True), axis=0, keepdims=True) * (1.0 / d)

    return pl.pallas_call(
        body, name=name, grid=(tp // tm,),
        in_specs=[pl.BlockSpec((tm, d), lambda i: (i, 0)), pl.BlockSpec((tm, d), lambda i: (jnp.maximum(i - 1, 0), 0))],
        out_specs=(pl.BlockSpec((tm, d), lambda i: (i, 0)), _full((1, 1))),
        out_shape=(jax.ShapeDtypeStruct((tp, d), F32), jax.ShapeDtypeStruct((1, 1), F32)),
    )(h, target)


def _t5_bucket_np(rel):
    n = np.maximum(rel, 0)
    max_exact = REL_BUCKETS // 2
    n_f = np.maximum(n, 1).astype(np.float32)
    large = max_exact + (np.log(n_f / np.float32(max_exact)) / np.float32(math.log(REL_MAX_DIST / max_exact))
                         * np.float32(REL_BUCKETS - max_exact)).astype(np.int32)
    large = np.minimum(large, REL_BUCKETS - 1)
    return np.where(n < max_exact, n, large).astype(np.int32)


def _swa_positions_np(n):
    i = np.arange(SWA_BLOCK)[:, None]
    j = np.arange(3 * SWA_BLOCK)[None, :]
    pos_q = n * SWA_BLOCK + i - PAD
    pos_k = np.where(j < SWA_BLOCK, j - PAD, (n - 1) * SWA_BLOCK + (j - SWA_BLOCK) - PAD)
    return pos_q, pos_k


def _swa_buckets():
    out = []
    for n in range(3):
        pos_q, pos_k = _swa_positions_np(n)
        out.append(_t5_bucket_np(pos_q - pos_k))
    return jnp.asarray(np.stack(out))


def swa_bias(table, buckets, *, name):
    nc, nq, nk = buckets.shape

    def body(tab_ref, bkt_ref, out_ref):
        for c in range(nc):
            bkt = bkt_ref[c]
            for h in range(SWA_Q_HEADS):
                acc = jnp.zeros((nq, nk), F32)
                for b in range(REL_BUCKETS):
                    acc = jnp.where(bkt == b, tab_ref[b, h], acc)
                out_ref[c, h] = acc

    return pl.pallas_call(
        body, name=name,
        in_specs=[pl.BlockSpec(memory_space=pltpu.SMEM), pl.BlockSpec(memory_space=pltpu.VMEM)],
        out_specs=pl.BlockSpec(memory_space=pltpu.VMEM),
        out_shape=jax.ShapeDtypeStruct((nc, SWA_Q_HEADS, nq, nk), F32),
    )(table, buckets)


def swa_bias_bwd(dbias, buckets, *, name):
    nc = buckets.shape[0]

    def body(db_ref, bkt_ref, out_ref):
        lane = lax.broadcasted_iota(jnp.int32, (1, LANE), 1)
        for b in range(REL_BUCKETS):
            row = jnp.zeros((1, LANE), F32)
            for c in range(nc):
                hit = bkt_ref[c] == b
                for h in range(SWA_Q_HEADS):
                    part = jnp.where(hit, db_ref[c, h], 0.0)
                    tot = jnp.sum(jnp.sum(part, axis=1, keepdims=True), axis=0, keepdims=True)
                    row = row + jnp.where(lane == h, tot, 0.0)
            out_ref[b:b + 1, :] = row

    return pl.pallas_call(
        body, name=name,
        in_specs=[pl.BlockSpec(memory_space=pltpu.VMEM), pl.BlockSpec(memory_space=pltpu.VMEM)],
        out_specs=pl.BlockSpec(memory_space=pltpu.VMEM),
        out_shape=jax.ShapeDtypeStruct((REL_BUCKETS, LANE), F32),
    )(dbias, buckets)


def _swa_block(q, kvm, kvp, kvc, bias, sinks, n):
    blk = SWA_BLOCK
    i = lax.broadcasted_iota(jnp.int32, (blk, 3 * blk), 0)
    j = lax.broadcasted_iota(jnp.int32, (blk, 3 * blk), 1)
    pos_q = n * blk + i - PAD
    is_meta = j < blk
    pos_k = jnp.where(is_meta, j - PAD, (n - 1) * blk + (j - blk) - PAD)
    rel = pos_q - pos_k
    valid = ((is_meta & (pos_k >= 0) & (pos_k < N_META) & (rel >= 0))
             | (jnp.logical_not(is_meta) & (pos_k >= N_META) & (rel >= 0) & (rel < SWA_WINDOW)))
    kv = jnp.concatenate([kvm, kvp, kvc], axis=0)
    lane = lax.broadcasted_iota(jnp.int32, (1, LANE), 1)
    halves = ((lane < SWA_HEAD_DIM).astype(F32), (lane >= SWA_HEAD_DIM).astype(F32))
    scale = SWA_HEAD_DIM ** -0.5
    outs = []
    for pair in range(SWA_Q_HEADS // 2):
        qp = q[:, pair * LANE:(pair + 1) * LANE]
        grp = pair // 2
        kg = kv[:, grp * LANE:(grp + 1) * LANE]
        vg = kv[:, (2 + grp) * LANE:(3 + grp) * LANE]
        op = None
        for hh in range(2):
            h = 2 * pair + hh
            s = _mm_nt(qp * halves[hh], kg) * scale + bias[h]
            s = jnp.where(valid, s, NEG_INF)
            sink = jnp.sum(jnp.where(lane == h, sinks, 0.0), axis=1, keepdims=True)
            m = lax.stop_gradient(jnp.maximum(jnp.max(s, axis=1, keepdims=True), sink))
            e = jnp.exp(s - m)
            den = jnp.sum(e, axis=1, keepdims=True) + jnp.exp(sink - m)
            part = _mm(e / den, vg) * halves[hh]
            op = part if op is None else op + part
        outs.append(op)
    return jnp.concatenate(outs, axis=1)


def _swa_in_specs(nb, rev):
    blk = SWA_BLOCK
    step = (lambda i: nb - 1 - i) if rev else (lambda i: i)
    return [
        pl.BlockSpec((blk, 4 * LANE), lambda i: (step(i), 0)),
        pl.BlockSpec((blk, 4 * LANE), lambda i: (0, 1)),
        pl.BlockSpec((blk, 4 * LANE), lambda i: (jnp.maximum(step(i) - 1, 0), 1)),
        pl.BlockSpec((blk, 4 * LANE), lambda i: (step(i), 1)),
        pl.BlockSpec((1, SWA_Q_HEADS, blk, 3 * blk), lambda i: (jnp.minimum(step(i), 2), 0, 0, 0)),
        _full((1, LANE)),
    ]


def swa_fwd(proj, bias, sinks, *, name):
    tp = proj.shape[0]
    nb = tp // SWA_BLOCK

    def body(q_ref, kvm_ref, kvp_ref, kvc_ref, bias_ref, sinks_ref, o_ref):
        n = pl.program_id(0)
        o_ref[...] = _swa_block(q_ref[...], kvm_ref[...], kvp_ref[...], kvc_ref[...], bias_ref[0], sinks_ref[...], n)

    return pl.pallas_call(
        body, name=name, grid=(nb,),
        in_specs=_swa_in_specs(nb, False),
        out_specs=pl.BlockSpec((SWA_BLOCK, 4 * LANE), lambda i: (i, 0)),
        out_shape=jax.ShapeDtypeStruct((tp, 4 * LANE), F32),
    )(proj, proj, proj, proj, bias, sinks)


def swa_bwd(proj, bias, sinks, do, *, name):
    tp = proj.shape[0]
    nb = tp // SWA_BLOCK
    blk = SWA_BLOCK

    def body(q_ref, kvm_ref, kvp_ref, kvc_ref, bias_ref, sinks_ref, do_ref, dq_ref, dkv_ref, dbias_ref, dsinks_ref,
             carry, meta_acc):
        i = pl.program_id(0)
        n = nb - 1 - i

        @pl.when(i == 0)
        def _():
            carry[...] = jnp.zeros_like(carry)
            meta_acc[...] = jnp.zeros_like(meta_acc)
            dsinks_ref[...] = jnp.zeros_like(dsinks_ref)

        fn = lambda q, kvm, kvp, kvc, b, s: _swa_block(q, kvm, kvp, kvc, b, s, n)
        _, vjp = jax.vjp(fn, q_ref[...], kvm_ref[...], kvp_ref[...], kvc_ref[...], bias_ref[0], sinks_ref[...])
        dq, dkvm, dkvp, dkvc, dbias, dsinks = vjp(do_ref[...])
        dq_ref[...] = dq
        meta_acc[...] += dkvm
        dkv_ref[...] = dkvc + carry[...] + jnp.where(n == 0, meta_acc[...], 0.0)
        carry[...] = dkvp
        first_visit = (n == nb - 1) | (n < 2)

        @pl.when(first_visit)
        def _():
            dbias_ref[0] = dbias

        @pl.when(jnp.logical_not(first_visit))
        def _():
            dbias_ref[0] += dbias

        dsinks_ref[...] += dsinks

    rev = lambda i: nb - 1 - i
    return pl.pallas_call(
        body, name=name, grid=(nb,),
        in_specs=_swa_in_specs(nb, True) + [pl.BlockSpec((blk, 4 * LANE), lambda i: (rev(i), 0))],
        out_specs=(pl.BlockSpec((blk, 4 * LANE), lambda i: (rev(i), 0)),
                   pl.BlockSpec((blk, 4 * LANE), lambda i: (rev(i), 0)),
                   pl.BlockSpec((1, SWA_Q_HEADS, blk, 3 * blk), lambda i: (jnp.minimum(rev(i), 2), 0, 0, 0)),
                   _full((1, LANE))),
        out_shape=(jax.ShapeDtypeStruct((tp, 4 * LANE), F32), jax.ShapeDtypeStruct((tp, 4 * LANE), F32),
                   jax.ShapeDtypeStruct((3, SWA_Q_HEADS, blk, 3 * blk), F32), jax.ShapeDtypeStruct((1, LANE), F32)),
        scratch_shapes=[pltpu.VMEM((blk, 4 * LANE), F32), pltpu.VMEM((blk, 4 * LANE), F32)],
    )(proj, proj, proj, proj, bias, sinks, do)


CONV_COL0 = 2
HALO = 8


def conv_fwd(proj, conv_w, *, name):
    tp = proj.shape[0]
    tm = _row_tile(tp, 384)
    cw = 4 * LANE
    ncol = conv_w.shape[1] // cw

    def body(x_ref, halo_ref, w_ref, y_ref, buf):
        i = pl.program_id(1)
        buf[0:HALO, :] = jnp.where(i > 0, halo_ref[...], 0.0)
        buf[HALO:, :] = x_ref[...]
        acc = None
        for j in range(DN_CONV):
            term = w_ref[j:j + 1, :] * buf[pl.ds(HALO - (DN_CONV - 1) + j, tm), :]
            acc = term if acc is None else acc + term
        y_ref[...] = acc

    return pl.pallas_call(
        body, name=name, grid=(ncol, tp // tm),
        in_specs=[pl.BlockSpec((tm, cw), lambda c, i: (i, CONV_COL0 + c)),
                  pl.BlockSpec((HALO, cw), lambda c, i: (jnp.maximum(i * (tm // HALO) - 1, 0), CONV_COL0 + c)),
                  pl.BlockSpec((DN_CONV, cw), lambda c, i: (0, c))],
        out_specs=pl.BlockSpec((tm, cw), lambda c, i: (i, c)),
        out_shape=jax.ShapeDtypeStruct((tp, ncol * cw), F32),
        scratch_shapes=[pltpu.VMEM((tm + HALO, cw), F32)],
    )(proj, proj, conv_w)


def conv_bwd(proj, conv_w, dy, *, name):
    tp = proj.shape[0]
    tm = _row_tile(tp, 384)
    cw = 4 * LANE
    ncol = conv_w.shape[1] // cw
    nt = tp // tm

    def body(x_ref, xhalo_ref, w_ref, dy_ref, dyhalo_ref, dx_ref, dw_ref, xbuf, dbuf):
        i = pl.program_id(1)
        xbuf[0:HALO, :] = jnp.where(i > 0, xhalo_ref[...], 0.0)
        xbuf[HALO:, :] = x_ref[...]
        dbuf[0:tm, :] = dy_ref[...]
        dbuf[tm:, :] = jnp.where(i < nt - 1, dyhalo_ref[...], 0.0)
        dy_t = dy_ref[...]
        acc = None
        rows = []
        for j in range(DN_CONV):
            term = w_ref[j:j + 1, :] * dbuf[pl.ds(DN_CONV - 1 - j, tm), :]
            acc = term if acc is None else acc + term
            rows.append(jnp.sum(dy_t * xbuf[pl.ds(HALO - (DN_CONV - 1) + j, tm), :], axis=0, keepdims=True))
        dx_ref[...] = acc

        @pl.when(i == 0)
        def _():
            dw_ref[...] = jnp.zeros_like(dw_ref)

        for j in range(DN_CONV):
            dw_ref[j:j + 1, :] += rows[j]

    return pl.pallas_call(
        body, name=name, grid=(ncol, nt),
        in_specs=[pl.BlockSpec((tm, cw), lambda c, i: (i, CONV_COL0 + c)),
                  pl.BlockSpec((HALO, cw), lambda c, i: (jnp.maximum(i * (tm // HALO) - 1, 0), CONV_COL0 + c)),
                  pl.BlockSpec((DN_CONV, cw), lambda c, i: (0, c)),
                  pl.BlockSpec((tm, cw), lambda c, i: (i, c)),
                  pl.BlockSpec((HALO, cw), lambda c, i: (jnp.minimum((i + 1) * (tm // HALO), tp // HALO - 1), c))],
        out_specs=(pl.BlockSpec((tm, cw), lambda c, i: (i, c)), pl.BlockSpec((DN_CONV, cw), lambda c, i: (0, c))),
        out_shape=(jax.ShapeDtypeStruct((tp, ncol * cw), F32), jax.ShapeDtypeStruct((DN_CONV, ncol * cw), F32)),
        scratch_shapes=[pltpu.VMEM((tm + HALO, cw), F32), pltpu.VMEM((tm + HALO, cw), F32)],
    )(proj, proj, conv_w, dy, dy)


def _stack(parts):
    return jnp.concatenate([p[None] for p in parts], axis=0)


def _chunk_masks():
    r = lax.broadcasted_iota(jnp.int32, (CHUNK, CHUNK), 0)
    c = lax.broadcasted_iota(jnp.int32, (CHUNK, CHUNK), 1)
    return (r >= c).astype(F32), (r > c).astype(F32), (r == c).astype(F32)


def _dn_chunk(y, z, small, s, a_log, dt_bias, norm_w, rows, known_inv=None):
    tri_incl, tri_strict, eye = _chunk_masks()
    lane = lax.broadcasted_iota(jnp.int32, (1, LANE), 1)
    dk = DN_HEAD_DIM
    nh = DN_HEADS
    heads = lambda t, first: _stack([t[:, (first + h) * dk:(first + h + 1) * dk] for h in range(nh)])
    pick = lambda t, l: jnp.sum(jnp.where(lane == l, t, 0.0), axis=1, keepdims=True)
    q = _l2n(_silu(heads(y, 0))) * dk ** -0.5
    k = _l2n(_silu(heads(y, nh)))
    v = _silu(heads(y, 2 * nh))
    g_all = jnp.where(lane < nh, -jnp.exp(a_log) * _softplus(small + dt_bias), 0.0) * rows
    beta_all = _sigmoid(small)
    gc_all = _cumsum_rows(g_all)
    g_sum = jnp.sum(g_all, axis=0, keepdims=True)
    gc = _stack([pick(gc_all, h) for h in range(nh)])
    beta = _stack([pick(beta_all, nh + h) for h in range(nh)])
    g_last = _stack([pick(g_sum, h) for h in range(nh)])
    gc_row = jnp.sum(eye * gc, axis=1, keepdims=True)
    gamma = jnp.exp((gc - gc_row) * tri_incl) * tri_incl
    k_beta = k * beta
    v_beta = v * beta
    a = _bmm_nt(k_beta, k) * gamma * tri_strict
    if known_inv is None:
        inv = eye - a
        power = a
        for _ in range(5):
            power = _bmm3(power, power)
            inv = inv + _bmm3(inv, power)
    else:
        inv = _known_inverse(a, known_inv)
    e_gc = jnp.exp(gc)
    uw = _bmm3(inv, jnp.concatenate([v_beta, k_beta * e_gc], axis=2))
    u, w = uw[:, :, :dk], uw[:, :, dk:]
    attn = _bmm_nt(q, k) * gamma
    q_dec = q * e_gc
    k_dec = k * jnp.exp(g_last - gc)
    v_new = u - _bmm(w, s)
    o = _bmm(q_dec, s) + _bmm(attn, v_new)
    s_new = s * jnp.exp(g_last) + _bmm_tn(k_dec, v_new)
    out = _rms(o, norm_w) * _silu(heads(z, 0))
    return jnp.concatenate([out[h] for h in range(nh)], axis=1), s_new, inv


Z_COL = 5
SMALL_COL = 24


def _chunk_rows(n):
    row = n * CHUNK + lax.broadcasted_iota(jnp.int32, (CHUNK, 1), 0)
    return (row >= PAD).astype(F32)


def dn_fwd(y, proj, a_log, dt_bias, norm_w, *, name):
    tp = y.shape[0]
    nc = tp // CHUNK
    dk = DN_HEAD_DIM

    def body(y_ref, z_ref, small_ref, al_ref, dt_ref, nw_ref, o_ref, ssave_ref, isave_ref, state):
        n = pl.program_id(0)

        @pl.when(n == 0)
        def _():
            state[...] = jnp.zeros_like(state)

        ssave_ref[0] = state[...]
        out, s_new, inv = _dn_chunk(y_ref[...], z_ref[...], small_ref[...], state[...], al_ref[...], dt_ref[...],
                                    nw_ref[...], _chunk_rows(n))
        o_ref[...] = out
        isave_ref[0] = inv
        state[...] = s_new

    return pl.pallas_call(
        body, name=name, grid=(nc,),
        in_specs=[pl.BlockSpec((CHUNK, y.shape[1]), lambda n: (n, 0)),
                  pl.BlockSpec((CHUNK, 4 * LANE), lambda n: (n, Z_COL)),
                  pl.BlockSpec((CHUNK, LANE), lambda n: (n, SMALL_COL)),
                  _full((1, LANE)), _full((1, LANE)), _full((1, LANE))],
        out_specs=(pl.BlockSpec((CHUNK, 4 * LANE), lambda n: (n, 0)),
                   pl.BlockSpec((1, DN_HEADS, dk, dk), lambda n: (n, 0, 0, 0)),
                   pl.BlockSpec((1, DN_HEADS, CHUNK, CHUNK), lambda n: (n, 0, 0, 0))),
        out_shape=(jax.ShapeDtypeStruct((tp, 4 * LANE), F32), jax.ShapeDtypeStruct((nc, DN_HEADS, dk, dk), F32),
                   jax.ShapeDtypeStruct((nc, DN_HEADS, CHUNK, CHUNK), F32)),
        scratch_shapes=[pltpu.VMEM((DN_HEADS, dk, dk), F32)],
    )(y, proj, proj, a_log, dt_bias, norm_w)


def dn_bwd(y, proj, a_log, dt_bias, norm_w, ssave, isave, do, *, name):
    tp = y.shape[0]
    nc = tp // CHUNK
    dk = DN_HEAD_DIM
    rev = lambda i: nc - 1 - i

    def body(y_ref, z_ref, small_ref, al_ref, dt_ref, nw_ref, ss_ref, is_ref, do_ref,
             dy_ref, dz_ref, dsmall_ref, dal_ref, ddt_ref, dnw_ref, dstate):
        i = pl.program_id(0)
        n = nc - 1 - i

        @pl.when(i == 0)
        def _():
            dstate[...] = jnp.zeros_like(dstate)
            dal_ref[...] = jnp.zeros_like(dal_ref)
            ddt_ref[...] = jnp.zeros_like(ddt_ref)
            dnw_ref[...] = jnp.zeros_like(dnw_ref)

        rows = _chunk_rows(n)
        known_inv = is_ref[0]
        fn = lambda *a: _dn_chunk(*a, rows, known_inv)[:2]
        _, vjp = jax.vjp(fn, y_ref[...], z_ref[...], small_ref[...], ss_ref[0], al_ref[...], dt_ref[...], nw_ref[...])
        dy, dz, dsmall, ds, dal, ddt, dnw = vjp((do_ref[...], dstate[...]))
        dy_ref[...] = dy
        dz_ref[...] = dz
        dsmall_ref[...] = dsmall
        dstate[...] = ds
        dal_ref[...] += dal
        ddt_ref[...] += ddt
        dnw_ref[...] += dnw

    return pl.pallas_call(
        body, name=name, grid=(nc,),
        in_specs=[pl.BlockSpec((CHUNK, y.shape[1]), lambda i: (rev(i), 0)),
                  pl.BlockSpec((CHUNK, 4 * LANE), lambda i: (rev(i), Z_COL)),
                  pl.BlockSpec((CHUNK, LANE), lambda i: (rev(i), SMALL_COL)),
                  _full((1, LANE)), _full((1, LANE)), _full((1, LANE)),
                  pl.BlockSpec((1, DN_HEADS, dk, dk), lambda i: (rev(i), 0, 0, 0)),
                  pl.BlockSpec((1, DN_HEADS, CHUNK, CHUNK), lambda i: (rev(i), 0, 0, 0)),
                  pl.BlockSpec((CHUNK, 4 * LANE), lambda i: (rev(i), 1))],
        out_specs=(pl.BlockSpec((CHUNK, y.shape[1]), lambda i: (rev(i), 0)),
                   pl.BlockSpec((CHUNK, 4 * LANE), lambda i: (rev(i), 0)),
                   pl.BlockSpec((CHUNK, LANE), lambda i: (rev(i), 0)),
                   _full((1, LANE)), _full((1, LANE)), _full((1, LANE))),
        out_shape=(jax.ShapeDtypeStruct((tp, y.shape[1]), F32), jax.ShapeDtypeStruct((tp, 4 * LANE), F32),
                   jax.ShapeDtypeStruct((tp, LANE), F32), jax.ShapeDtypeStruct((1, LANE), F32),
                   jax.ShapeDtypeStruct((1, LANE), F32), jax.ShapeDtypeStruct((1, LANE), F32)),
        scratch_shapes=[pltpu.VMEM((DN_HEADS, dk, dk), F32)],
    )(y, proj, proj, a_log, dt_bias, norm_w, ssave, isave, do)


def _gla_chunk(q, k, v, gate, low, s, w_gate_up, b_gate, norm_w, rows):
    tri_incl, _, _ = _chunk_masks()
    dk, dv, nh = GLA_DK, GLA_DV, GLA_HEADS
    heads = lambda t, width: _stack([t[:, h * width:(h + 1) * width] for h in range(nh)])
    logit = _mm3(low, w_gate_up) + b_gate
    glog_all = -_softplus(-logit) * (1.0 / GLA_GATE_NORM) * rows
    glog = heads(glog_all, dk)
    bcum = heads(_cumsum_rows(glog_all), dk)
    qh = heads(q, dk) * dk ** -0.5
    kh = heads(k, dk)
    vh = heads(v, dv)
    q_dec = qh * jnp.exp(bcum)
    attn = _bmm_nt(q_dec, kh * jnp.exp(-bcum)) * tri_incl
    b_last = jnp.sum(glog, axis=1, keepdims=True)
    k_dec = kh * jnp.exp(b_last - bcum)
    r = lax.broadcasted_iota(jnp.int32, (dk, dk), 0)
    c = lax.broadcasted_iota(jnp.int32, (dk, dk), 1)
    b_last_col = jnp.sum((r == c).astype(F32) * b_last, axis=2, keepdims=True)
    o = _bmm(attn, vh) + _bmm(q_dec, s)
    s_new = s * jnp.exp(b_last_col) + _bmm_tn(k_dec, vh)
    out = _rms(o, norm_w) * _silu(heads(gate, dv))
    return jnp.concatenate([out[h] for h in range(nh)], axis=1), s_new


LOW_COL = 24


def _gla_in_specs(step):
    return [pl.BlockSpec((CHUNK, 4 * LANE), lambda i: (step(i), 0)),
            pl.BlockSpec((CHUNK, 4 * LANE), lambda i: (step(i), 1)),
            pl.BlockSpec((CHUNK, 8 * LANE), lambda i: (step(i), 1)),
            pl.BlockSpec((CHUNK, 8 * LANE), lambda i: (step(i), 2)),
            pl.BlockSpec((CHUNK, LANE), lambda i: (step(i), LOW_COL)),
            _full((LANE, 4 * LANE)), _full((1, 4 * LANE)), _full((1, GLA_DV))]


def gla_fwd(proj, w_gate_up, b_gate, norm_w, *, name):
    tp = proj.shape[0]
    nc = tp // CHUNK

    def body(q_ref, k_ref, v_ref, g_ref, low_ref, wgu_ref, bg_ref, nw_ref, o_ref, ssave_ref, state):
        n = pl.program_id(0)

        @pl.when(n == 0)
        def _():
            state[...] = jnp.zeros_like(state)

        ssave_ref[0] = state[...]
        out, s_new = _gla_chunk(q_ref[...], k_ref[...], v_ref[...], g_ref[...], low_ref[...], state[...], wgu_ref[...],
                                bg_ref[...], nw_ref[...], _chunk_rows(n))
        o_ref[...] = out
        state[...] = s_new

    return pl.pallas_call(
        body, name=name, grid=(nc,),
        in_specs=_gla_in_specs(lambda i: i),
        out_specs=(pl.BlockSpec((CHUNK, 8 * LANE), lambda n: (n, 0)),
                   pl.BlockSpec((1, GLA_HEADS, GLA_DK, GLA_DV), lambda n: (n, 0, 0, 0))),
        out_shape=(jax.ShapeDtypeStruct((tp, 8 * LANE), F32),
                   jax.ShapeDtypeStruct((nc, GLA_HEADS, GLA_DK, GLA_DV), F32)),
        scratch_shapes=[pltpu.VMEM((GLA_HEADS, GLA_DK, GLA_DV), F32)],
    )(proj, proj, proj, proj, proj, w_gate_up, b_gate, norm_w)


def gla_bwd(proj, w_gate_up, b_gate, norm_w, ssave, do, *, name):
    tp = proj.shape[0]
    nc = tp // CHUNK
    rev = lambda i: nc - 1 - i

    def body(q_ref, k_ref, v_ref, g_ref, low_ref, wgu_ref, bg_ref, nw_ref, ss_ref, do_ref,
             dq_ref, dk_ref, dv_ref, dg_ref, dlow_ref, dwgu_ref, dbg_ref, dnw_ref, dstate):
        i = pl.program_id(0)
        n = nc - 1 - i

        @pl.when(i == 0)
        def _():
            dstate[...] = jnp.zeros_like(dstate)
            dwgu_ref[...] = jnp.zeros_like(dwgu_ref)
            dbg_ref[...] = jnp.zeros_like(dbg_ref)
            dnw_ref[...] = jnp.zeros_like(dnw_ref)

        rows = _chunk_rows(n)
        fn = lambda *a: _gla_chunk(*a, rows)
        _, vjp = jax.vjp(fn, q_ref[...], k_ref[...], v_ref[...], g_ref[...], low_ref[...], ss_ref[0], wgu_ref[...],
                         bg_ref[...], nw_ref[...])
        dq, dk, dv, dg, dlow, ds, dwgu, dbg, dnw = vjp((do_ref[...], dstate[...]))
        dq_ref[...] = dq
        dk_ref[...] = dk
        dv_ref[...] = dv
        dg_ref[...] = dg
        dlow_ref[...] = dlow
        dstate[...] = ds
        dwgu_ref[...] += dwgu
        dbg_ref[...] += dbg
        dnw_ref[...] += dnw

    chunk = lambda width: pl.BlockSpec((CHUNK, width), lambda i: (rev(i), 0))
    return pl.pallas_call(
        body, name=name, grid=(nc,),
        in_specs=_gla_in_specs(rev) + [pl.BlockSpec((1, GLA_HEADS, GLA_DK, GLA_DV), lambda i: (rev(i), 0, 0, 0)),
                                       chunk(8 * LANE)],
        out_specs=(chunk(4 * LANE), chunk(4 * LANE), chunk(8 * LANE), chunk(8 * LANE), chunk(LANE),
                   _full((LANE, 4 * LANE)), _full((1, 4 * LANE)), _full((1, GLA_DV))),
        out_shape=(jax.ShapeDtypeStruct((tp, 4 * LANE), F32), jax.ShapeDtypeStruct((tp, 4 * LANE), F32),
                   jax.ShapeDtypeStruct((tp, 8 * LANE), F32), jax.ShapeDtypeStruct((tp, 8 * LANE), F32),
                   jax.ShapeDtypeStruct((tp, LANE), F32), jax.ShapeDtypeStruct((LANE, 4 * LANE), F32),
                   jax.ShapeDtypeStruct((1, 4 * LANE), F32), jax.ShapeDtypeStruct((1, GLA_DV), F32)),
        scratch_shapes=[pltpu.VMEM((GLA_HEADS, GLA_DK, GLA_DV), F32)],
    )(proj, proj, proj, proj, proj, w_gate_up, b_gate, norm_w, ssave, do)


def _even_proj_weight(w_in):
    hd = SWA_HEAD_DIM
    k0, k1 = w_in[:, 512:512 + hd], w_in[:, 512 + hd:640]
    v0, v1 = w_in[:, 640:640 + hd], w_in[:, 640 + hd:768]
    zeros = jnp.zeros((w_in.shape[0], LANE - 2 * DN_HEADS), w_in.dtype)
    return jnp.concatenate([w_in[:, :512], k0, k0, k1, k1, v0, v0, v1, v1, w_in[:, 768:2816], w_in[:, 2820:2824],
                            w_in[:, 2816:2820], zeros], axis=1)


def _even_proj_weight_grad(dw):
    hd = SWA_HEAD_DIM
    c = lambda i: dw[:, 512 + i * hd:512 + (i + 1) * hd]
    return jnp.concatenate([dw[:, :512], c(0) + c(1), c(2) + c(3), c(4) + c(5), c(6) + c(7), dw[:, 1024:3072],
                            dw[:, 3076:3080], dw[:, 3072:3076]], axis=1)


def _ffn_fwd(h, nw_in, nw_out, wts, idx, get_w):
    w_gu = wts[f"w_gu{idx}"]
    hn, g, u, a = rms_mm(h, nw_in, w_gu[0], swiglu=True, name=f"ffn_up_{idx}", widx=w_gu[1])
    wts.update(get_w(f"down{idx}", a))
    w_down = wts[f"w_down{idx}"]
    f, h_out = mm_rms_res([a], w_down[0], h, nw_out, scale=0.5, name=f"ffn_down_{idx}", widx=w_down[1])
    return h_out, (h, hn, g, u, a, f)


def _ffn_bwd(dho, saved, nw_in, nw_out, w_gu, w_down, idx, on_grads):
    h, hn, g, u, a, f = saved
    df, dnw_out, dgu = mm_rms_res_bwd(dho, f, nw_out, w_down[0], (g, u), scale=0.5, name=f"ffn_down_bwd_{idx}",
                                      widx=w_down[1])
    g_gu = mm_tn(hn, dgu, name=f"ffn_dwgu_{idx}", out_dtype=BF16)
    g_down = mm_tn(a, df, name=f"ffn_dwd_{idx}", out_dtype=BF16)
    sent = on_grads(g_gu, g_down)
    dh, dnw_in = rms_mm_bwd([dgu], w_gu[0], h, nw_in + sent, dho, name=f"ffn_up_bwd_{idx}", widx=w_gu[1])
    return dh, dnw_in, dnw_out


def local_step(x, target, wts, get_w=None, put_g=None):
    seq, d = x.shape
    wts = dict(wts)
    get_w = get_w or (lambda stage, after: {})
    put_g = put_g or (lambda stage, grads: jnp.zeros((1, 1), F32))
    row = lambda v: v.reshape(1, -1)
    lane_row = lambda v: jnp.pad(v.reshape(1, -1), ((0, 0), (0, LANE - v.size)))
    nw = wts["norm_w"]
    h = jnp.concatenate([jnp.zeros((PAD, d), F32), wts["meta_tokens"], x], axis=0)
    buckets = _swa_buckets()
    bias = swa_bias(wts["rel_bias_table"], buckets, name="swa_bias")
    sinks = lane_row(wts["swa_sinks"])
    a_log, dt_bias = lane_row(wts["dn_a_log"]), lane_row(wts["dn_dt_bias"])
    dn_norm_w = row(wts["dn_norm_w"])
    conv_w = wts["even_conv_w"][0]
    w_gate_up = jnp.pad(wts["gla_w_gate_up"][0], ((0, LANE - GLA_GATE_RANK), (0, 0)))
    b_gate, gla_norm_w = row(wts["gla_b_gate"]), row(wts["gla_norm_w"])

    saved = []
    w_in, w_out = [None, None], [None, None]
    for l in range(2):
        if l == 1:
            wts.update(get_w("layer1", h))
            w_in[1] = jnp.pad(wts["odd_w_in"], ((0, 0), (0, PROJ_DIM - wts["odd_w_in"].shape[1])))
            w_out[1] = wts["odd_w_out"]
        h, s_a = _ffn_fwd(h, row(nw[l, 0]), row(nw[l, 1]), wts, 2 * l, get_w)
        if l == 0:
            wts.update(get_w("even", h))
            w_in[0], w_out[0] = _even_proj_weight(wts["even_w_in"]), wts["even_w_out"]
        h_mix = h
        hn, proj = rms_mm(h, row(nw[l, 2]), w_in[l], swiglu=False, name=f"mix_in_{l}")
        if l == 0:
            o_a = swa_fwd(proj, bias, sinks, name="swa_fwd")
            y = conv_fwd(proj, conv_w, name="conv_fwd")
            o_b, ssave, isave = dn_fwd(y, proj, a_log, dt_bias, dn_norm_w, name="dn_fwd")
            acts, extra = [o_a, o_b], (y, ssave, isave)
        else:
            o, ssave = gla_fwd(proj, w_gate_up, b_gate, gla_norm_w, name="gla_fwd")
            acts, extra = [o], (ssave,)
        mix, h = mm_rms_res(acts, w_out[l], h, row(nw[l, 3]), scale=1.0, name=f"mix_out_{l}")
        s_m = (h_mix, hn, proj, acts, extra, mix)
        if l == 0:
            wts.update(get_w("ffn1", h))
        h, s_b = _ffn_fwd(h, row(nw[l, 4]), row(nw[l, 5]), wts, 2 * l + 1, get_w)
        saved.append((s_a, s_m, s_b))

    dh, loss = loss_and_grad(h, target, name="loss")

    grads = {}
    dnw = [[None] * 6 for _ in range(2)]
    stage_of = {3: "ffn3", 2: "layer1", 1: "ffn1", 0: "ffn0"}

    def on_grads(i):
        def put(g_gu, g_down):
            grads[f"g_gu{i}"], grads[f"g_down{i}"] = g_gu, g_down
            return put_g(stage_of[i], grads)
        return put

    for l in (1, 0):
        s_a, s_m, s_b = saved[l]
        i = 2 * l + 1
        dh, dnw[l][4], dnw[l][5] = _ffn_bwd(dh, s_b, row(nw[l, 4]), row(nw[l, 5]), wts[f"w_gu{i}"], wts[f"w_down{i}"],
                                            i, on_grads(i))
        h_mix, hn, proj, acts, extra, mix = s_m
        dmix, dnw[l][3], do = mm_rms_res_bwd(dh, mix, row(nw[l, 3]), w_out[l], None, scale=1.0, name=f"mix_out_bwd_{l}")
        dw_out = jnp.concatenate([mm_tn(a, dmix, name=f"mix_dwo_{l}_{i}") for i, a in enumerate(acts)], axis=0)
        sent = jnp.zeros((1, 1), F32)
        if l == 0:
            y, ssave, isave = extra
            dq, dkv, dbias, dsinks = swa_bwd(proj, bias, sinks, do, name="swa_bwd")
            dy, dz, dsmall, da_log, ddt_bias, ddn_norm_w = dn_bwd(y, proj, a_log, dt_bias, dn_norm_w, ssave, isave, do,
                                                                    name="dn_bwd")
            dxc, dconv_w = conv_bwd(proj, conv_w, dy, name="conv_bwd")
            dps = [dq, dkv, dxc, dz, dsmall]
            grads["rel_bias_table"] = swa_bias_bwd(dbias, buckets, name="swa_bias_bwd")[:, :SWA_Q_HEADS]
            grads["swa_sinks"] = dsinks[:, :SWA_Q_HEADS]
            grads["dn_a_log"] = da_log[:, :DN_HEADS]
            grads["dn_dt_bias"] = ddt_bias[:, :DN_HEADS]
            grads["dn_norm_w"] = ddn_norm_w
            grads["even_conv_w"] = dconv_w[None]
            grads["even_w_out"] = dw_out
        else:
            (ssave,) = extra
            dq, dk, dv, dgate, dlow, dwgu, dbg, dgnw = gla_bwd(proj, w_gate_up, b_gate, gla_norm_w, ssave, do,
                                                               name="gla_bwd")
            dps = [dq, dk, dv, dgate, dlow]
            grads["gla_w_gate_up"] = dwgu[None, :GLA_GATE_RANK]
            grads["gla_b_gate"] = dbg
            grads["gla_norm_w"] = dgnw
            grads["odd_w_out"] = dw_out
        dw_in = jnp.concatenate([mm_tn(hn, dp, name=f"mix_dwi_{l}_{i}") for i, dp in enumerate(dps)], axis=1)
        if l == 0:
            grads["even_w_in"] = _even_proj_weight_grad(dw_in)
            sent = put_g("even", grads)
        else:
            grads["odd_w_in"] = dw_in[:, :wts["odd_w_in"].shape[1]]
        dh, dnw[l][2] = rms_mm_bwd(dps, w_in[l], h_mix, row(nw[l, 2]) + sent, dh, name=f"mix_in_bwd_{l}")
        i = 2 * l
        dh, dnw[l][0], dnw[l][1] = _ffn_bwd(dh, s_a, row(nw[l, 0]), row(nw[l, 1]), wts[f"w_gu{i}"], wts[f"w_down{i}"],
                                            i, on_grads(i))

    grads["norm_w"] = jnp.stack([jnp.concatenate(r, axis=0) for r in dnw])
    grads["meta_tokens"] = dh[PAD:PAD + N_META]
    return loss[0, 0], dh[PAD + N_META:], grads


def _peer(k):
    x, y, c = (lax.axis_index(a) for a in AXES)
    flip = lambda v, bit: 1 - v if bit else v
    return (flip(x, k & 4), flip(y, k & 2), flip(c, k & 1))


def _my_index():
    x, y, c = (lax.axis_index(a) for a in AXES)
    return 4 * x + 2 * y + c


_HBM = pl.BlockSpec(memory_space=pltpu.HBM)
_SEM = pl.BlockSpec(memory_space=pltpu.SEMAPHORE)
_EFFECT = pltpu.SideEffectType.DATAFLOW_SIDE_EFFECTING


def _remote_copies(items, src_refs, land_refs, send_sems, recv_sems):
    me = _my_index()
    copies = []
    for k in range(1, N_DEV):
        px, py, pc = _peer(k)
        pj = 4 * px + 2 * py + pc
        for a, (sn, send, ln, land, _) in enumerate(items):
            sem = (k - 1) * len(items) + a
            copies.append(pltpu.make_async_remote_copy(
                src_ref=send(src_refs[sn], pj), dst_ref=land(land_refs[ln], me), send_sem=send_sems.at[sem],
                recv_sem=recv_sems.at[sem], device_id=(px, py, pc), device_id_type=MESH))
    return copies


def exchange(srcs, lands, items, after, *, name):
    sn, ln = list(srcs), list(lands)

    def body(*refs):
        src_refs = dict(zip(sn, refs[:len(sn)]))
        land_refs = dict(zip(ln, refs[len(sn) + len(ln) + 1:len(sn) + 2 * len(ln) + 1]))
        send_sems, recv_sems = refs[len(sn) + 2 * len(ln) + 1:]
        copies = _remote_copies(items, src_refs, land_refs, send_sems, recv_sems)
        for cp in copies:
            cp.start()
        for cp in copies:
            cp.wait_recv()
        for cp in copies:
            cp.wait_send()

    n_remote = (N_DEV - 1) * len(items)
    outs = pl.pallas_call(
        body, name=name,
        in_specs=[pl.BlockSpec(memory_space=pl.ANY)] * (len(sn) + len(ln) + 1),
        out_specs=tuple(pl.BlockSpec(memory_space=pl.ANY) for _ in ln),
        out_shape=tuple(jax.ShapeDtypeStruct(lands[n].shape, lands[n].dtype) for n in ln),
        input_output_aliases={len(sn) + i: i for i in range(len(ln))},
        scratch_shapes=[pltpu.SemaphoreType.DMA((n_remote,)), pltpu.SemaphoreType.DMA((n_remote,))],
    )(*[srcs[n] for n in sn], *[lands[n] for n in ln], after)
    return dict(zip(ln, outs))


def start_copies(srcs, lands, items, *, name):
    sn, ln = list(srcs), list(lands)
    n_remote = (N_DEV - 1) * len(items)

    def body(*refs):
        src_refs = dict(zip(sn, refs[:len(sn)]))
        land_refs = dict(zip(ln, refs[len(sn):len(sn) + len(ln)]))
        send_sems, recv_sems = refs[len(sn) + len(ln):len(sn) + len(ln) + 2]
        token = refs[-1]
        for cp in _remote_copies(items, src_refs, land_refs, send_sems, recv_sems):
            cp.start()
        token[...] = jnp.zeros_like(token)

    hbm = lambda a: pltpu.with_memory_space_constraint(a, pltpu.HBM)
    outs = pl.pallas_call(
        body, name=name,
        in_specs=[_HBM] * (len(sn) + len(ln)),
        out_specs=(_SEM, _SEM) + (_HBM,) * len(ln) + (pl.BlockSpec(memory_space=pltpu.VMEM),),
        out_shape=(pltpu.SemaphoreType.DMA((n_remote,)), pltpu.SemaphoreType.DMA((n_remote,)))
        + tuple(pltpu.HBM(lands[n].shape, lands[n].dtype) for n in ln) + (jax.ShapeDtypeStruct((8, LANE), F32),),
        input_output_aliases={len(sn) + i: 2 + i for i in range(len(ln))},
        compiler_params=pltpu.CompilerParams(has_side_effects=_EFFECT),
    )(*[hbm(srcs[n]) for n in sn], *[hbm(lands[n]) for n in ln])
    return (outs[0], outs[1]), dict(zip(ln, outs[2:2 + len(ln)])), outs[-1][0:1, 0:1]


def wait_copies(sems, srcs, lands, items, after, *, name):
    sn, ln = list(srcs), list(lands)

    def body(*refs):
        src_refs = dict(zip(sn, refs[:len(sn)]))
        land_refs = dict(zip(ln, refs[len(sn):len(sn) + len(ln)]))
        send_sems, recv_sems = refs[len(sn) + len(ln):len(sn) + len(ln) + 2]
        copies = _remote_copies(items, src_refs, land_refs, send_sems, recv_sems)
        for cp in copies:
            cp.wait_send()
        for cp in copies:
            cp.wait_recv()

    outs = pl.pallas_call(
        body, name=name,
        in_specs=[_HBM] * (len(sn) + len(ln)) + [_SEM, _SEM, pl.BlockSpec(memory_space=pl.ANY)],
        out_specs=(_HBM,) * len(ln),
        out_shape=tuple(pltpu.HBM(lands[n].shape, lands[n].dtype) for n in ln),
        input_output_aliases={len(sn) + i: i for i in range(len(ln))},
        compiler_params=pltpu.CompilerParams(has_side_effects=_EFFECT),
    )(*[srcs[n] for n in sn], *[lands[n] for n in ln], sems[0], sems[1], after)
    return dict(zip(ln, outs))


def _block(index, size):
    return pl.ds(pl.multiple_of(index * size, LANE), size)


def _adam_tile(rows):
    for t in (256, 176, 128):
        if rows % t == 0:
            return t
    return rows


def sum_adamw(recvs, col0, w, m, v, *, name, first_slab=0, into=None):
    _, r, c = w.shape
    b = len(recvs)
    cp = recvs[0].shape[-1]
    tr = _adam_tile(r)
    c1 = 1.0 / (1.0 - ADAM_B1 ** ADAM_STEP)
    c2 = 1.0 / (1.0 - ADAM_B2 ** ADAM_STEP)

    def body(*refs):
        recv_refs = refs[:b]
        w_ref, m_ref, v_ref = refs[b:b + 3]
        g_ref, d_ref, nm_ref, nv_ref = refs[-4:]
        for slab, recv_ref in enumerate(recv_refs):
            @pl.when(pl.program_id(0) == slab)
            def _():
                g = recv_ref[0, :, col0:col0 + c].astype(F32)
                for i in range(1, N_DEV):
                    g = g + recv_ref[i, :, col0:col0 + c].astype(F32)
                nm = ADAM_B1 * m_ref[0] + (1.0 - ADAM_B1) * g
                nv = ADAM_B2 * v_ref[0] + (1.0 - ADAM_B2) * (g * g)
                g_ref[0] = g
                nm_ref[0] = nm
                nv_ref[0] = nv
                d_ref[0] = -ADAM_LR * ((nm * c1) / (jnp.sqrt(nv * c2) + ADAM_EPS) + ADAM_WD * w_ref[0])

    tile = pl.BlockSpec((1, tr, c), lambda bi, i: (first_slab + bi, i, 0))
    piece = lambda slab: pl.BlockSpec((N_DEV, tr, cp), lambda bi, i: (0, jnp.where(bi == slab, i, 0), 0))
    earlier = [] if into is None else list(into)
    return pl.pallas_call(
        body, name=name, grid=(b, r // tr),
        in_specs=[piece(slab) for slab in range(b)] + [tile, tile, tile] + [pl.BlockSpec(memory_space=pl.ANY)] * len(earlier),
        out_specs=(tile,) * 4, out_shape=(jax.ShapeDtypeStruct(w.shape, F32),) * 4,
        input_output_aliases={b + 3 + i: i for i in range(len(earlier))},
    )(*recvs, w, m, v, *earlier)


def _flat_rows(n_elems, row_multiple):
    rows = -(-n_elems // FLAT_COLS)
    return -(-rows // row_multiple) * row_multiple


def _pack(arrays, row_multiple, dtype):
    flat = jnp.concatenate([a.reshape(-1).astype(dtype) for a in arrays])
    rows = _flat_rows(flat.size, row_multiple)
    return jnp.pad(flat, (0, rows * FLAT_COLS - flat.size)).reshape(rows, FLAT_COLS)


def _unpack(flat2d, shapes):
    lead = flat2d.shape[:-2]
    flat = flat2d.reshape(lead + (-1,))
    out, off = [], 0
    for shp in shapes:
        n = int(np.prod(shp))
        out.append(flat[..., off:off + n].reshape(lead + tuple(shp)))
        off += n
    return out


def _join_shards(stacked, axis):
    moved = jnp.moveaxis(stacked, 0, axis)
    shp = list(moved.shape)
    shp[axis:axis + 2] = [shp[axis] * shp[axis + 1]]
    return moved.reshape(shp)


def _split_shards(full, axis):
    shp = list(full.shape)
    shp[axis:axis + 1] = [N_DEV, shp[axis] // N_DEV]
    return jnp.moveaxis(full.reshape(shp), axis, 0)


def kernel(x, meta_tokens, norm_w, ffn_w_gate, ffn_w_up, ffn_w_down, rel_bias_table, even_w_in, even_conv_w, swa_sinks, dn_a_log, dn_dt_bias, dn_norm_w, even_w_out, odd_w_in, gla_w_gate_up, gla_b_gate, gla_norm_w, odd_w_out, loss_target, m_meta_tokens, m_norm_w, m_ffn_w_gate, m_ffn_w_up, m_ffn_w_down, m_rel_bias_table, m_even_w_in, m_even_conv_w, m_swa_sinks, m_dn_a_log, m_dn_dt_bias, m_dn_norm_w, m_even_w_out, m_odd_w_in, m_gla_w_gate_up, m_gla_b_gate, m_gla_norm_w, m_odd_w_out, v_meta_tokens, v_norm_w, v_ffn_w_gate, v_ffn_w_up, v_ffn_w_down, v_rel_bias_table, v_even_w_in, v_even_conv_w, v_swa_sinks, v_dn_a_log, v_dn_dt_bias, v_dn_norm_w, v_even_w_out, v_odd_w_in, v_gla_w_gate_up, v_gla_b_gate, v_gla_norm_w, v_odd_w_out):
    args = locals()
    w = {n: args[n] for n in WEIGHTS}
    m = {n: args["m_" + n] for n in WEIGHTS}
    v = {n: args["v_" + n] for n in WEIGHTS}

    d = D_MODEL
    me = _my_index()
    whole = lambda ref, j: ref
    cols = lambda size, base=0: (lambda ref, i: ref.at[(slice(None),) * (len(ref.shape) - 1)
                                                       + (pl.ds(pl.multiple_of(base + i * size, LANE), size),)])
    rows3 = lambda size: (lambda ref, i: ref.at[:, _block(i, size), :])
    rows2 = lambda size: (lambda ref, i: ref.at[_block(i, size), :])
    lead = lambda ref, i: ref.at[i]
    of_group = lambda items, g: [it for it in items if it[4] == g]
    names = lambda items, k: list(dict.fromkeys(it[k] for it in items))

    def placed(shape, dtype, parts):
        land = lax.empty(shape, dtype)
        for part, axis, start in parts:
            land = lax.dynamic_update_slice(land, part, tuple(start if a == axis else 0 for a in range(land.ndim)))
        return land

    pad_cols = lambda a, to: jnp.pad(a, [(0, 0)] * (a.ndim - 1) + [(0, to - a.shape[-1])])
    gate_s = pad_cols(w["ffn_w_gate"].reshape(N_FFN, d, FF_SHARD), FF_SHARD_PAD).astype(BF16)
    up_s = pad_cols(w["ffn_w_up"].reshape(N_FFN, d, FF_SHARD), FF_SHARD_PAD).astype(BF16)
    down_s = jnp.pad(w["ffn_w_down"].reshape(N_FFN, FF_SHARD, d),
                     ((0, 0), (0, FF_SHARD_PAD - FF_SHARD), (0, 0))).astype(BF16)
    small_s = _pack([w[n] for n in SMALL], 8, F32)
    srcs_w = {"ein": pad_cols(w["even_w_in"][0], EVEN_IN_SHARD_PAD).astype(BF16), "eout": w["even_w_out"][0].astype(BF16),
              "small": small_s, "oin": pad_cols(w["odd_w_in"][0], ODD_IN_SHARD_PAD).astype(BF16),
              "oout": w["odd_w_out"][0].astype(BF16)}
    lands_w = {"ein": placed((d, N_DEV * EVEN_IN_SHARD_PAD), BF16, [(srcs_w["ein"], 1, me * EVEN_IN_SHARD_PAD)]),
               "oin": placed((d, N_DEV * ODD_IN_SHARD_PAD), BF16, [(srcs_w["oin"], 1, me * ODD_IN_SHARD_PAD)]),
               "eout": placed((d, d), BF16, [(srcs_w["eout"], 0, me * OUT_SHARD)]),
               "oout": placed((d, d), BF16, [(srcs_w["oout"], 0, me * OUT_SHARD)]),
               "small": placed((N_DEV,) + small_s.shape, F32, [(small_s[None], 0, me)])}
    items_w = [("small", whole, "small", lead, "first"), ("ein", whole, "ein", cols(EVEN_IN_SHARD_PAD), "even"),
               ("eout", whole, "eout", rows2(OUT_SHARD), "even"), ("oin", whole, "oin", cols(ODD_IN_SHARD_PAD), "layer1"),
               ("oout", whole, "oout", rows2(OUT_SHARD), "layer1")]
    for tag, pick, group, down_group in (("0", 0, "first", "down0"), ("1", 1, "ffn1", "ffn1"),
                                         ("23", slice(2, 4), "layer1", "layer1")):
        lead_dims = (2,) if tag == "23" else ()
        nl = len(lead_dims)
        srcs_w.update({f"gate{tag}": gate_s[pick], f"up{tag}": up_s[pick], f"down{tag}": down_s[pick]})
        lands_w[f"w_gu{tag}"] = placed(lead_dims + (d, 2 * FF_PAD), BF16,
                                       [(srcs_w[f"gate{tag}"], nl + 1, me * FF_SHARD_PAD),
                                        (srcs_w[f"up{tag}"], nl + 1, FF_PAD + me * FF_SHARD_PAD)])
        lands_w[f"w_down{tag}"] = placed(lead_dims + (FF_PAD, d), BF16, [(srcs_w[f"down{tag}"], nl, me * FF_SHARD_PAD)])
        items_w += [(f"gate{tag}", whole, f"w_gu{tag}", cols(FF_SHARD_PAD), group),
                    (f"up{tag}", whole, f"w_gu{tag}", cols(FF_SHARD_PAD, FF_PAD), group),
                    (f"down{tag}", whole, f"w_down{tag}", (rows3 if lead_dims else rows2)(FF_SHARD_PAD), down_group)]
    pending, started = {}, []
    for g in ("first", "down0", "even", "ffn1", "layer1"):
        its = of_group(items_w, g)
        srcs = {n: srcs_w[n] for n in names(its, 0)}
        sems, lands, token = start_copies(srcs, {n: lands_w[n] for n in names(its, 2)}, its, name=f"gather_start_{g}")
        pending[g] = (sems, srcs, lands, its)
        started.append(token)

    unpad = lambda p, shard, shard_pad: p.reshape(d, N_DEV, shard_pad)[:, :, :shard].reshape(d, N_DEV * shard)

    def get_w(stage, after):
        if stage not in pending:
            return {}
        sems, srcs, lands, its = pending[stage]
        landed = wait_copies(sems, srcs, lands, its, after, name=f"gather_wait_{stage}")
        if stage == "first":
            got = {"w_gu0": (landed["w_gu0"], None)}
            for n, stacked in zip(SMALL, _unpack(landed["small"], [w[n].shape for n in SMALL])):
                got[n] = _join_shards(stacked, SHARD_AXIS[n])
            return got
        if stage == "down0":
            return {"w_down0": (landed["w_down0"], None)}
        if stage == "even":
            return {"even_w_out": landed["eout"], "even_w_in": unpad(landed["ein"], EVEN_IN_SHARD, EVEN_IN_SHARD_PAD)}
        if stage == "ffn1":
            return {"w_gu1": (landed["w_gu1"], None), "w_down1": (landed["w_down1"], None)}
        return {"w_gu2": (landed["w_gu23"], 0), "w_gu3": (landed["w_gu23"], 1), "w_down2": (landed["w_down23"], 0),
                "w_down3": (landed["w_down23"], 1), "odd_w_out": landed["oout"],
                "odd_w_in": unpad(landed["oin"], ODD_IN_SHARD, ODD_IN_SHARD_PAD)}

    full = {n: w[n] for n in REPL}
    full.update(get_w("first", sum(started)))

    repad = lambda g, shard, shard_pad: pad_cols(g.reshape(d, N_DEV, shard), shard_pad).reshape(d, N_DEV * shard_pad)
    mine = lambda g, axis, size, base=0: lax.dynamic_slice_in_dim(g, base + me * size, size, axis)
    half = lambda h: (lambda ref, i: ref.at[i, :, pl.ds(h * FF_SHARD_PAD, FF_SHARD_PAD)])
    items_g = [("oin", cols(ODD_IN_SHARD_PAD), "r_oin", lead, "layer1"), ("oout", rows2(OUT_SHARD), "r_oout", lead, "layer1"),
               ("ein", cols(EVEN_IN_SHARD_PAD), "r_ein", lead, "even"), ("eout", rows2(OUT_SHARD), "r_eout", lead, "even"),
               ("small", lead, "r_small", lead, "last")]
    for i, group in ((3, "ffn3"), (2, "layer1"), (1, "ffn1"), (0, "ffn0")):
        items_g += [(f"g_gu{i}", cols(FF_SHARD_PAD), f"r_gu{i}", half(0), group),
                    (f"g_gu{i}", cols(FF_SHARD_PAD, FF_PAD), f"r_gu{i}", half(1), group),
                    (f"g_down{i}", rows2(FF_SHARD_PAD), f"r_down{i}", lead, group)]

    def grad_src(n, grads):
        if n == "oin":
            return repad(grads["odd_w_in"], ODD_IN_SHARD, ODD_IN_SHARD_PAD).astype(BF16)
        if n == "ein":
            return repad(grads["even_w_in"], EVEN_IN_SHARD, EVEN_IN_SHARD_PAD).astype(BF16)
        if n in ("oout", "eout"):
            return grads["odd_w_out" if n == "oout" else "even_w_out"].astype(BF16)
        return grads[n]

    def grad_land(n, srcs):
        if n.startswith("r_gu"):
            g = srcs["g_gu" + n[4:]]
            own = jnp.concatenate([mine(g, 1, FF_SHARD_PAD), mine(g, 1, FF_SHARD_PAD, FF_PAD)], axis=1)
        elif n.startswith("r_down"):
            own = mine(srcs["g_down" + n[6:]], 0, FF_SHARD_PAD)
        elif n == "r_small":
            own = lax.dynamic_index_in_dim(srcs["small"], me, 0, keepdims=False)
        else:
            axis, size = {"r_oin": (1, ODD_IN_SHARD_PAD), "r_ein": (1, EVEN_IN_SHARD_PAD), "r_oout": (0, OUT_SHARD),
                          "r_eout": (0, OUT_SHARD)}[n]
            own = mine(srcs[n[2:]], axis, size)
        return placed((N_DEV,) + own.shape, own.dtype, [(own[None], 0, me)])

    sent = {}

    def put_g(stage, grads):
        its = of_group(items_g, stage)
        srcs = {n: grad_src(n, grads) for n in names(its, 0)}
        lands = {n: grad_land(n, srcs) for n in names(its, 2)}
        sems, lands, token = start_copies(srcs, lands, its, name=f"grads_start_{stage}")
        sent[stage] = (sems, srcs, lands, its)
        return token

    loss, grad_x, grads = local_step(x[0], loss_target[0], full, get_w, put_g)
    loss = lax.psum(loss, AXES)

    order = SMALL + REPL
    pieces = [_split_shards(grads[n].reshape(full[n].shape), SHARD_AXIS[n]) if n in SHARD_AXIS
              else jnp.broadcast_to(grads[n].reshape(w[n].shape)[None], (N_DEV,) + w[n].shape) for n in order]
    flat = jnp.concatenate([p.reshape(N_DEV, -1) for p in pieces], axis=1)
    srows = _flat_rows(flat.shape[1], 8)
    grads["small"] = jnp.pad(flat, ((0, 0), (0, srows * FLAT_COLS - flat.shape[1]))).reshape(N_DEV, srows, FLAT_COLS)
    recv = {}
    for stage, (sems, srcs, lands, its) in sent.items():
        if stage != "ffn0":
            recv.update(wait_copies(sems, srcs, lands, its, grad_x, name=f"grads_wait_{stage}"))
    result = [{} for _ in range(4)]

    def adam(n, recvs, col0, view, first_slab=0, into=None):
        return sum_adamw(recvs, col0, view(w[n]), view(m[n]), view(v[n]), name=f"adamw_{n}_{first_slab}",
                         first_slab=first_slab, into=into)

    def finish(n, outs):
        for r, o in zip(result, outs):
            r[n] = o.reshape(w[n].shape)

    ffn_views = (("ffn_w_gate", "r_gu", 0, lambda a: a.reshape(N_FFN, d, FF_SHARD)),
                 ("ffn_w_up", "r_gu", FF_SHARD_PAD, lambda a: a.reshape(N_FFN, d, FF_SHARD)),
                 ("ffn_w_down", "r_down", 0, lambda a: a.reshape(N_FFN, FF_SHARD, d)))
    early = {n: adam(n, [recv[f"{r}{i}"] for i in (1, 2, 3)], col0, view, first_slab=1) for n, r, col0, view in ffn_views}
    for n, r in (("even_w_in", "r_ein"), ("odd_w_in", "r_oin"), ("even_w_out", "r_eout"), ("odd_w_out", "r_oout")):
        finish(n, adam(n, [recv[r]], 0, lambda a: a))
    srcs = {"small": grads["small"]}
    recv.update(exchange(srcs, {"r_small": grad_land("r_small", srcs)}, of_group(items_g, "last"),
                         early["ffn_w_down"][0], name="exchange_small"))
    sems, srcs, lands, its = sent["ffn0"]
    recv.update(wait_copies(sems, srcs, lands, its, recv["r_small"], name="grads_wait_ffn0"))
    for n, r, col0, view in ffn_views:
        finish(n, adam(n, [recv[f"{r}0"]], col0, view, into=early[n]))
    pack_local = lambda t: _pack([t[n] for n in order], 8, F32)[None]
    small_outs = sum_adamw([recv["r_small"]], 0, pack_local(w), pack_local(m), pack_local(v), name="adamw_small")
    for r, o in zip(result, small_outs):
        r.update(zip(order, _unpack(o[0], [w[n].shape for n in order])))
    return (loss, grad_x[None], *[r[n] for r in result for n in WEIGHTS])
```

```python
import functools
import math

import numpy as np
import jax
import jax.numpy as jnp
from jax import lax
from jax.experimental import pallas as pl
from jax.experimental.pallas import tpu as pltpu

F32 = jnp.float32
BF16 = jnp.bfloat16
MESH = pl.DeviceIdType.MESH
AXES = ("x", "y", "c")
N_DEV = 8

D_MODEL = 1024
N_META = 16
D_FF = 2816
NORM_EPS = 1e-6
NEG_INF = -1e30
SWA_Q_HEADS = 8
SWA_HEAD_DIM = 64
SWA_WINDOW = 128
SWA_BLOCK = 128
REL_BUCKETS = 32
REL_MAX_DIST = 128
DN_HEADS = 4
DN_HEAD_DIM = 128
DN_CONV = 4
GLA_HEADS = 4
GLA_DK = 128
GLA_DV = 256
GLA_GATE_RANK = 16
GLA_GATE_NORM = 16.0
CHUNK = 64
PAD = SWA_BLOCK - N_META
LANE = 128
PROJ_DIM = 3200

ADAM_LR = 0.001
ADAM_B1 = 0.9
ADAM_B2 = 0.999
ADAM_EPS = 1e-08
ADAM_WD = 0.01
ADAM_STEP = 10

FF_SHARD = D_FF // N_DEV
FF_SHARD_PAD = 384
FF_PAD = N_DEV * FF_SHARD_PAD
N_FFN = 4
EVEN_IN_SHARD, EVEN_IN_SHARD_PAD = 353, 384
ODD_IN_SHARD, ODD_IN_SHARD_PAD = 386, 512
OUT_SHARD = D_MODEL // N_DEV

FLAT_COLS = 128
BIG = ("ffn_w_gate", "ffn_w_up", "ffn_w_down", "even_w_in", "even_w_out", "odd_w_in", "odd_w_out")
SMALL = ("meta_tokens", "norm_w", "even_conv_w", "gla_w_gate_up", "gla_b_gate", "gla_norm_w")
REPL = ("rel_bias_table", "swa_sinks", "dn_a_log", "dn_dt_bias", "dn_norm_w")
WEIGHTS = ("meta_tokens", "norm_w", "ffn_w_gate", "ffn_w_up", "ffn_w_down", "rel_bias_table", "even_w_in",
           "even_conv_w", "swa_sinks", "dn_a_log", "dn_dt_bias", "dn_norm_w", "even_w_out", "odd_w_in",
           "gla_w_gate_up", "gla_b_gate", "gla_norm_w", "odd_w_out")
SHARD_AXIS = {"ffn_w_gate": 3, "ffn_w_up": 3, "ffn_w_down": 2, "even_w_in": 2, "even_w_out": 1, "odd_w_in": 2,
              "odd_w_out": 1, "meta_tokens": 1, "norm_w": 2, "even_conv_w": 2, "gla_w_gate_up": 2,
              "gla_b_gate": 1, "gla_norm_w": 1}


def _rms(x, w):
    r = lax.rsqrt(jnp.mean(x * x, axis=-1, keepdims=True) + NORM_EPS)
    return x * r * w


def _sigmoid(x):
    return 0.5 * (jnp.tanh(0.5 * x) + 1.0)


def _silu(x):
    return x * _sigmoid(x)


def _softplus(x):
    pos = x > 0
    return jnp.where(pos, x, 0.0) + jnp.log(1.0 + jnp.exp(jnp.where(pos, -x, x)))


def _l2n(x):
    return x * lax.rsqrt(jnp.sum(x * x, axis=-1, keepdims=True) + 1e-6)


def _split_bf16(x):
    hi = x.astype(BF16)
    return hi, (x - hi.astype(F32)).astype(BF16)


def _make_mm(terms, batched):
    off = 1 if batched else 0
    bdims = ((0,), (0,)) if batched else ((), ())

    def dg(a, b, ca, cb):
        dot = lambda p, q: lax.dot_general(p, q, (((ca + off,), (cb + off,)), bdims), preferred_element_type=F32)
        a_hi, a_lo = _split_bf16(a)
        b_hi, b_lo = _split_bf16(b)
        if terms == 1:
            return dot(a_hi, b_hi)
        return dot(a_hi, b_hi) + (dot(a_hi, b_lo) + dot(a_lo, b_hi))

    @jax.custom_vjp
    def nn(a, b):
        return dg(a, b, 1, 0)

    @jax.custom_vjp
    def nt(a, b):
        return dg(a, b, 1, 1)

    @jax.custom_vjp
    def tn(a, b):
        return dg(a, b, 0, 0)

    nn.defvjp(lambda a, b: (nn(a, b), (a, b)), lambda r, g: (nt(g, r[1]), tn(r[0], g)))
    nt.defvjp(lambda a, b: (nt(a, b), (a, b)), lambda r, g: (nn(g, r[1]), tn(g, r[0])))
    tn.defvjp(lambda a, b: (tn(a, b), (a, b)), lambda r, g: (nt(r[1], g), nn(r[0], g)))
    return nn, nt, tn


_mm, _mm_nt, _mm_tn = _make_mm(1, False)
_mm3, _, _ = _make_mm(3, False)
_bmm, _bmm_nt, _bmm_tn = _make_mm(1, True)
_bmm3, _bmm3_nt, _bmm3_tn = _make_mm(3, True)


@jax.custom_vjp
def _known_inverse(a, inv):
    return inv


_known_inverse.defvjp(lambda a, inv: (inv, inv),
                      lambda inv, g: (-_bmm3_tn(inv, _bmm3_nt(g, inv)), jnp.zeros_like(inv)))


def _tri_ones_dot(x, lower):
    n = x.shape[0]
    r = lax.broadcasted_iota(jnp.int32, (n, n), 0)
    c = lax.broadcasted_iota(jnp.int32, (n, n), 1)
    t = ((r >= c) if lower else (r <= c)).astype(BF16)
    hi, lo = _split_bf16(x)
    return jnp.dot(t, hi, preferred_element_type=F32) + jnp.dot(t, lo, preferred_element_type=F32)


@jax.custom_vjp
def _cumsum_rows(x):
    return _tri_ones_dot(x, True)


_cumsum_rows.defvjp(lambda x: (_tri_ones_dot(x, True), None), lambda _, g: (_tri_ones_dot(g, False),))


def _row_tile(n_rows, cap):
    best = LANE
    for t in range(LANE, cap + 1, LANE):
        if n_rows % t == 0:
            best = t
    return best


def _real_rows(tile_index, tm):
    row = tile_index * tm + lax.broadcasted_iota(jnp.int32, (tm, 1), 0)
    return (row >= PAD).astype(F32)


def _full(shape):
    return pl.BlockSpec(shape, lambda *_: (0,) * len(shape))


def _resident(shape):
    return pl.BlockSpec(shape, lambda *_: (0,) * len(shape), pipeline_mode=pl.Buffered(1))


def _resident_w(wmat, widx):
    if wmat.ndim == 2:
        return _resident(wmat.shape)
    return pl.BlockSpec((None,) + wmat.shape[1:], lambda *_: (widx, 0, 0), pipeline_mode=pl.Buffered(1))


def rms_mm(h, w, wmat, *, swiglu, name, widx=None):
    tp, d = h.shape
    n = wmat.shape[-1]
    tm = _row_tile(tp, 384)
    half = n // 2

    def body(h_ref, w_ref, wm_ref, hn_ref, *outs):
        hn = _rms(h_ref[...], w_ref[...]).astype(BF16)
        hn_ref[...] = hn
        p = jnp.dot(hn, wm_ref[...], preferred_element_type=F32)
        if swiglu:
            g, u = p[:, :half], p[:, half:]
            outs[0][...] = g.astype(BF16)
            outs[1][...] = u.astype(BF16)
            outs[2][...] = (_silu(g) * u).astype(BF16)
        else:
            outs[0][...] = p

    row = lambda width: pl.BlockSpec((tm, width), lambda i: (i, 0))
    if swiglu:
        out_shape = (jax.ShapeDtypeStruct((tp, d), BF16),) + (jax.ShapeDtypeStruct((tp, half), BF16),) * 3
        out_specs = (row(d), row(half), row(half), row(half))
    else:
        out_shape = (jax.ShapeDtypeStruct((tp, d), BF16), jax.ShapeDtypeStruct((tp, n), F32))
        out_specs = (row(d), row(n))
    return pl.pallas_call(
        body, name=name, grid=(tp // tm,),
        in_specs=[row(d), _full((1, d)), _resident_w(wmat, widx)],
        out_specs=out_specs, out_shape=out_shape,
    )(h, w, wmat)


def mm_rms_res(acts, wmat, h, w, *, scale, name, widx=None):
    tp, d = h.shape
    tm = _row_tile(tp, 384)
    widths = [a.shape[1] for a in acts]
    offs = [sum(widths[:i]) for i in range(len(acts))]
    na = len(acts)

    def body(*refs):
        a_refs = refs[:na]
        wm_ref, h_ref, w_ref, f_ref, ho_ref = refs[na:]
        f = None
        for a_ref, off, width in zip(a_refs, offs, widths):
            part = jnp.dot(a_ref[...].astype(BF16), wm_ref[off:off + width, :], preferred_element_type=F32)
            f = part if f is None else f + part
        f_ref[...] = f
        ho_ref[...] = h_ref[...] + scale * _rms(f, w_ref[...])

    row = lambda width: pl.BlockSpec((tm, width), lambda i: (i, 0))
    return pl.pallas_call(
        body, name=name, grid=(tp // tm,),
        in_specs=[row(wd) for wd in widths] + [_resident_w(wmat, widx), row(d), _full((1, d))],
        out_specs=(row(d), row(d)),
        out_shape=(jax.ShapeDtypeStruct((tp, d), F32), jax.ShapeDtypeStruct((tp, d), F32)),
    )(*acts, wmat, h, w)


def mm_rms_res_bwd(dho, f, w, wmat, gu, *, scale, name, widx=None):
    tp, d = f.shape
    k = wmat.shape[-2]
    tm = _row_tile(tp, 384)
    swiglu = gu is not None

    def body(*refs):
        if swiglu:
            dho_ref, f_ref, w_ref, wm_ref, g_ref, u_ref, df_ref, dw_ref, dgu_ref = refs
        else:
            dho_ref, f_ref, w_ref, wm_ref, df_ref, dw_ref, da_ref = refs
        i = pl.program_id(0)
        _, vjp = jax.vjp(lambda ff, ww: scale * _rms(ff, ww), f_ref[...], w_ref[...])
        df, dw = vjp(dho_ref[...])
        dfb = (df * _real_rows(i, tm)).astype(BF16)
        df_ref[...] = dfb

        @pl.when(i == 0)
        def _():
            dw_ref[...] = jnp.zeros_like(dw_ref)

        dw_ref[...] += dw
        da = lax.dot_general(dfb, wm_ref[...], (((1,), (1,)), ((), ())), preferred_element_type=F32)
        if swiglu:
            g = g_ref[...].astype(F32)
            u = u_ref[...].astype(F32)
            s = _sigmoid(g)
            dgu_ref[:, :k] = (da * u * s * (1.0 + g * (1.0 - s))).astype(BF16)
            dgu_ref[:, k:] = (da * g * s).astype(BF16)
        else:
            da_ref[...] = da

    row = lambda width: pl.BlockSpec((tm, width), lambda i: (i, 0))
    in_specs = [row(d), row(d), _full((1, d)), _resident_w(wmat, widx)]
    args = [dho, f, w, wmat]
    out_shape = [jax.ShapeDtypeStruct((tp, d), BF16), jax.ShapeDtypeStruct((1, d), F32)]
    out_specs = [row(d), _full((1, d))]
    if swiglu:
        in_specs += [row(k), row(k)]
        args += list(gu)
        out_shape += [jax.ShapeDtypeStruct((tp, 2 * k), BF16)]
        out_specs += [row(2 * k)]
    else:
        out_shape += [jax.ShapeDtypeStruct((tp, k), F32)]
        out_specs += [row(k)]
    return pl.pallas_call(body, name=name, grid=(tp // tm,), in_specs=in_specs, out_specs=tuple(out_specs),
                          out_shape=tuple(out_shape))(*args)


def rms_mm_bwd(dps, wmat, h, w, dho, *, name, widx=None):
    tp, d = h.shape
    tm = _row_tile(tp, 384)
    widths = [p.shape[1] for p in dps]
    offs = [sum(widths[:i]) for i in range(len(dps))]
    ndp = len(dps)

    def body(*refs):
        dp_refs = refs[:ndp]
        wm_ref, h_ref, w_ref, dho_ref, dh_ref, dw_ref = refs[ndp:]
        i = pl.program_id(0)
        dhn = None
        for dp_ref, off, width in zip(dp_refs, offs, widths):
            part = lax.dot_general(dp_ref[...].astype(BF16), wm_ref[:, off:off + width], (((1,), (1,)), ((), ())),
                                   preferred_element_type=F32)
            dhn = part if dhn is None else dhn + part
        _, vjp = jax.vjp(_rms, h_ref[...], w_ref[...])
        dx, dw = vjp(dhn)
        dh_ref[...] = (dho_ref[...] + dx) * _real_rows(i, tm)

        @pl.when(i == 0)
        def _():
            dw_ref[...] = jnp.zeros_like(dw_ref)

        dw_ref[...] += dw

    row = lambda width: pl.BlockSpec((tm, width), lambda i: (i, 0))
    return pl.pallas_call(
        body, name=name, grid=(tp // tm,),
        in_specs=[row(wd) for wd in widths] + [_resident_w(wmat, widx), row(d), _full((1, d)), row(d)],
        out_specs=(row(d), _full((1, d))),
        out_shape=(jax.ShapeDtypeStruct((tp, d), F32), jax.ShapeDtypeStruct((1, d), F32)),
    )(*dps, wmat, h, w, dho)


def mm_tn(a, b, *, name, out_dtype=F32, after=None):
    t, m = a.shape
    n = b.shape[1]
    bm = _row_tile(m, 512)
    bn = _row_tile(n, 1536)
    bk = _row_tile(t, 1408)
    nk = t // bk
    ties = [] if after is None else [after]

    def body(a_ref, b_ref, *rest):
        o_ref, acc = rest[-2:]

        @pl.when(pl.program_id(2) == 0)
        def _():
            acc[...] = jnp.zeros_like(acc)

        acc[...] += lax.dot_general(a_ref[...].astype(BF16), b_ref[...].astype(BF16), (((0,), (0,)), ((), ())),
                                    preferred_element_type=F32)

        @pl.when(pl.program_id(2) == nk - 1)
        def _():
            o_ref[...] = acc[...].astype(o_ref.dtype)

    return pl.pallas_call(
        body, name=name, grid=(m // bm, n // bn, nk),
        in_specs=[pl.BlockSpec((bk, bm), lambda i, j, kk: (kk, i)), pl.BlockSpec((bk, bn), lambda i, j, kk: (kk, j))]
        + [pl.BlockSpec(memory_space=pl.ANY)] * len(ties),
        out_specs=pl.BlockSpec((bm, bn), lambda i, j, kk: (i, j)),
        out_shape=jax.ShapeDtypeStruct((m, n), out_dtype), scratch_shapes=[pltpu.VMEM((bm, bn), F32)],
    )(a, b, *ties)


def loss_and_grad(h, target, *, name):
    tp, d = h.shape
    tm = SWA_BLOCK

    def body(h_ref, t_ref, dh_ref, loss_ref):
        i = pl.program_id(0)

        @pl.when(i == 0)
        def _():
            loss_ref[...] = jnp.zeros_like(loss_ref)
            dh_ref[...] = jnp.zeros_like(dh_ref)

        @pl.when(i > 0)
        def _():
            err = h_ref[...] - t_ref[...]
            dh_ref[...] = err * (1.0 / d)
            loss_ref[...] += 0.5 * jnp.sum(jnp.sum(err * err, axis=1, keepdims=True), axis=0, keepdims=True) * (1.0 / d)

    return pl.pallas_call(
        body, name=name, grid=(tp // tm,),
        in_specs=[pl.BlockSpec((tm, d), lambda i: (i, 0)), pl.BlockSpec((tm, d), lambda i: (jnp.maximum(i - 1, 0), 0))],
        out_specs=(pl.BlockSpec((tm, d), lambda i: (i, 0)), _full((1, 1))),
        out_shape=(jax.ShapeDtypeStruct((tp, d), F32), jax.ShapeDtypeStruct((1, 1), F32)),
    )(h, target)


def _t5_bucket_np(rel):
    n = np.maximum(rel, 0)
    max_exact = REL_BUCKETS // 2
    n_f = np.maximum(n, 1).astype(np.float32)
    large = max_exact + (np.log(n_f / np.float32(max_exact)) / np.float32(math.log(REL_MAX_DIST / max_exact))
                         * np.float32(REL_BUCKETS - max_exact)).astype(np.int32)
    large = np.minimum(large, REL_BUCKETS - 1)
    return np.where(n < max_exact, n, large).astype(np.int32)


def _swa_positions_np(n):
    i = np.arange(SWA_BLOCK)[:, None]
    j = np.arange(3 * SWA_BLOCK)[None, :]
    pos_q = n * SWA_BLOCK + i - PAD
    pos_k = np.where(j < SWA_BLOCK, j - PAD, (n - 1) * SWA_BLOCK + (j - SWA_BLOCK) - PAD)
    return pos_q, pos_k


def _swa_buckets():
    out = []
    for n in range(3):
        pos_q, pos_k = _swa_positions_np(n)
        out.append(_t5_bucket_np(pos_q - pos_k))
    return jnp.asarray(np.stack(out))


def swa_bias(table, buckets, *, name):
    nc, nq, nk = buckets.shape

    def body(tab_ref, bkt_ref, out_ref):
        for c in range(nc):
            bkt = bkt_ref[c]
            for h in range(SWA_Q_HEADS):
                acc = jnp.zeros((nq, nk), F32)
                for b in range(REL_BUCKETS):
                    acc = jnp.where(bkt == b, tab_ref[b, h], acc)
                out_ref[c, h] = acc

    return pl.pallas_call(
        body, name=name,
        in_specs=[pl.BlockSpec(memory_space=pltpu.SMEM), pl.BlockSpec(memory_space=pltpu.VMEM)],
        out_specs=pl.BlockSpec(memory_space=pltpu.VMEM),
        out_shape=jax.ShapeDtypeStruct((nc, SWA_Q_HEADS, nq, nk), F32),
    )(table, buckets)


def swa_bias_bwd(dbias, buckets, *, name):
    nc = buckets.shape[0]

    def body(db_ref, bkt_ref, out_ref):
        lane = lax.broadcasted_iota(jnp.int32, (1, LANE), 1)
        for b in range(REL_BUCKETS):
            row = jnp.zeros((1, LANE), F32)
            for c in range(nc):
                hit = bkt_ref[c] == b
                for h in range(SWA_Q_HEADS):
                    part = jnp.where(hit, db_ref[c, h], 0.0)
                    tot = jnp.sum(jnp.sum(part, axis=1, keepdims=True), axis=0, keepdims=True)
                    row = row + jnp.where(lane == h, tot, 0.0)
            out_ref[b:b + 1, :] = row

    return pl.pallas_call(
        body, name=name,
        in_specs=[pl.BlockSpec(memory_space=pltpu.VMEM), pl.BlockSpec(memory_space=pltpu.VMEM)],
        out_specs=pl.BlockSpec(memory_space=pltpu.VMEM),
        out_shape=jax.ShapeDtypeStruct((REL_BUCKETS, LANE), F32),
    )(dbias, buckets)


def _swa_block(q, kvm, kvp, kvc, bias, sinks, n):
    blk = SWA_BLOCK
    i = lax.broadcasted_iota(jnp.int32, (blk, 3 * blk), 0)
    j = lax.broadcasted_iota(jnp.int32, (blk, 3 * blk), 1)
    pos_q = n * blk + i - PAD
    is_meta = j < blk
    pos_k = jnp.where(is_meta, j - PAD, (n - 1) * blk + (j - blk) - PAD)
    rel = pos_q - pos_k
    valid = ((is_meta & (pos_k >= 0) & (pos_k < N_META) & (rel >= 0))
             | (jnp.logical_not(is_meta) & (pos_k >= N_META) & (rel >= 0) & (rel < SWA_WINDOW)))
    kv = jnp.concatenate([kvm, kvp, kvc], axis=0)
    lane = lax.broadcasted_iota(jnp.int32, (1, LANE), 1)
    halves = ((lane < SWA_HEAD_DIM).astype(F32), (lane >= SWA_HEAD_DIM).astype(F32))
    scale = SWA_HEAD_DIM ** -0.5
    outs = []
    for pair in range(SWA_Q_HEADS // 2):
        qp = q[:, pair * LANE:(pair + 1) * LANE]
        grp = pair // 2
        kg = kv[:, grp * LANE:(grp + 1) * LANE]
        vg = kv[:, (2 + grp) * LANE:(3 + grp) * LANE]
        op = None
        for hh in range(2):
            h = 2 * pair + hh
            s = _mm_nt(qp * halves[hh], kg) * scale + bias[h]
            s = jnp.where(valid, s, NEG_INF)
            sink = jnp.sum(jnp.where(lane == h, sinks, 0.0), axis=1, keepdims=True)
            m = lax.stop_gradient(jnp.maximum(jnp.max(s, axis=1, keepdims=True), sink))
            e = jnp.exp(s - m)
            den = jnp.sum(e, axis=1, keepdims=True) + jnp.exp(sink - m)
            part = _mm(e / den, vg) * halves[hh]
            op = part if op is None else op + part
        outs.append(op)
    return jnp.concatenate(outs, axis=1)


def _swa_in_specs(nb, rev):
    blk = SWA_BLOCK
    step = (lambda i: nb - 1 - i) if rev else (lambda i: i)
    return [
        pl.BlockSpec((blk, 4 * LANE), lambda i: (step(i), 0)),
        pl.BlockSpec((blk, 4 * LANE), lambda i: (0, 1)),
        pl.BlockSpec((blk, 4 * LANE), lambda i: (jnp.maximum(step(i) - 1, 0), 1)),
        pl.BlockSpec((blk, 4 * LANE), lambda i: (step(i), 1)),
        pl.BlockSpec((1, SWA_Q_HEADS, blk, 3 * blk), lambda i: (jnp.minimum(step(i), 2), 0, 0, 0)),
        _full((1, LANE)),
    ]


def swa_fwd(proj, bias, sinks, *, name):
    tp = proj.shape[0]
    nb = tp // SWA_BLOCK

    def body(q_ref, kvm_ref, kvp_ref, kvc_ref, bias_ref, sinks_ref, o_ref):
        n = pl.program_id(0)
        o_ref[...] = _swa_block(q_ref[...], kvm_ref[...], kvp_ref[...], kvc_ref[...], bias_ref[0], sinks_ref[...], n)

    return pl.pallas_call(
        body, name=name, grid=(nb,),
        in_specs=_swa_in_specs(nb, False),
        out_specs=pl.BlockSpec((SWA_BLOCK, 4 * LANE), lambda i: (i, 0)),
        out_shape=jax.ShapeDtypeStruct((tp, 4 * LANE), F32),
    )(proj, proj, proj, proj, bias, sinks)


def swa_bwd(proj, bias, sinks, do, *, name):
    tp = proj.shape[0]
    nb = tp // SWA_BLOCK
    blk = SWA_BLOCK

    def body(q_ref, kvm_ref, kvp_ref, kvc_ref, bias_ref, sinks_ref, do_ref, dq_ref, dkv_ref, dbias_ref, dsinks_ref,
             carry, meta_acc):
        i = pl.program_id(0)
        n = nb - 1 - i

        @pl.when(i == 0)
        def _():
            carry[...] = jnp.zeros_like(carry)
            meta_acc[...] = jnp.zeros_like(meta_acc)
            dsinks_ref[...] = jnp.zeros_like(dsinks_ref)

        fn = lambda q, kvm, kvp, kvc, b, s: _swa_block(q, kvm, kvp, kvc, b, s, n)
        _, vjp = jax.vjp(fn, q_ref[...], kvm_ref[...], kvp_ref[...], kvc_ref[...], bias_ref[0], sinks_ref[...])
        dq, dkvm, dkvp, dkvc, dbias, dsinks = vjp(do_ref[...])
        dq_ref[...] = dq
        meta_acc[...] += dkvm
        dkv_ref[...] = dkvc + carry[...] + jnp.where(n == 0, meta_acc[...], 0.0)
        carry[...] = dkvp
        first_visit = (n == nb - 1) | (n < 2)

        @pl.when(first_visit)
        def _():
            dbias_ref[0] = dbias

        @pl.when(jnp.logical_not(first_visit))
        def _():
            dbias_ref[0] += dbias

        dsinks_ref[...] += dsinks

    rev = lambda i: nb - 1 - i
    return pl.pallas_call(
        body, name=name, grid=(nb,),
        in_specs=_swa_in_specs(nb, True) + [pl.BlockSpec((blk, 4 * LANE), lambda i: (rev(i), 0))],
        out_specs=(pl.BlockSpec((blk, 4 * LANE), lambda i: (rev(i), 0)),
                   pl.BlockSpec((blk, 4 * LANE), lambda i: (rev(i), 0)),
                   pl.BlockSpec((1, SWA_Q_HEADS, blk, 3 * blk), lambda i: (jnp.minimum(rev(i), 2), 0, 0, 0)),
                   _full((1, LANE))),
        out_shape=(jax.ShapeDtypeStruct((tp, 4 * LANE), F32), jax.ShapeDtypeStruct((tp, 4 * LANE), F32),
                   jax.ShapeDtypeStruct((3, SWA_Q_HEADS, blk, 3 * blk), F32), jax.ShapeDtypeStruct((1, LANE), F32)),
        scratch_shapes=[pltpu.VMEM((blk, 4 * LANE), F32), pltpu.VMEM((blk, 4 * LANE), F32)],
    )(proj, proj, proj, proj, bias, sinks, do)


CONV_COL0 = 2
HALO = 8


def conv_fwd(proj, conv_w, *, name):
    tp = proj.shape[0]
    tm = _row_tile(tp, 384)
    cw = 4 * LANE
    ncol = conv_w.shape[1] // cw

    def body(x_ref, halo_ref, w_ref, y_ref, buf):
        i = pl.program_id(1)
        buf[0:HALO, :] = jnp.where(i > 0, halo_ref[...], 0.0)
        buf[HALO:, :] = x_ref[...]
        acc = None
        for j in range(DN_CONV):
            term = w_ref[j:j + 1, :] * buf[pl.ds(HALO - (DN_CONV - 1) + j, tm), :]
            acc = term if acc is None else acc + term
        y_ref[...] = acc

    return pl.pallas_call(
        body, name=name, grid=(ncol, tp // tm),
        in_specs=[pl.BlockSpec((tm, cw), lambda c, i: (i, CONV_COL0 + c)),
                  pl.BlockSpec((HALO, cw), lambda c, i: (jnp.maximum(i * (tm // HALO) - 1, 0), CONV_COL0 + c)),
                  pl.BlockSpec((DN_CONV, cw), lambda c, i: (0, c))],
        out_specs=pl.BlockSpec((tm, cw), lambda c, i: (i, c)),
        out_shape=jax.ShapeDtypeStruct((tp, ncol * cw), F32),
        scratch_shapes=[pltpu.VMEM((tm + HALO, cw), F32)],
    )(proj, proj, conv_w)


def conv_bwd(proj, conv_w, dy, *, name):
    tp = proj.shape[0]
    tm = _row_tile(tp, 384)
    cw = 4 * LANE
    ncol = conv_w.shape[1] // cw
    nt = tp // tm

    def body(x_ref, xhalo_ref, w_ref, dy_ref, dyhalo_ref, dx_ref, dw_ref, xbuf, dbuf):
        i = pl.program_id(1)
        xbuf[0:HALO, :] = jnp.where(i > 0, xhalo_ref[...], 0.0)
        xbuf[HALO:, :] = x_ref[...]
        dbuf[0:tm, :] = dy_ref[...]
        dbuf[tm:, :] = jnp.where(i < nt - 1, dyhalo_ref[...], 0.0)
        dy_t = dy_ref[...]
        acc = None
        rows = []
        for j in range(DN_CONV):
            term = w_ref[j:j + 1, :] * dbuf[pl.ds(DN_CONV - 1 - j, tm), :]
            acc = term if acc is None else acc + term
            rows.append(jnp.sum(dy_t * xbuf[pl.ds(HALO - (DN_CONV - 1) + j, tm), :], axis=0, keepdims=True))
        dx_ref[...] = acc

        @pl.when(i == 0)
        def _():
            dw_ref[...] = jnp.zeros_like(dw_ref)

        for j in range(DN_CONV):
            dw_ref[j:j + 1, :] += rows[j]

    return pl.pallas_call(
        body, name=name, grid=(ncol, nt),
        in_specs=[pl.BlockSpec((tm, cw), lambda c, i: (i, CONV_COL0 + c)),
                  pl.BlockSpec((HALO, cw), lambda c, i: (jnp.maximum(i * (tm // HALO) - 1, 0), CONV_COL0 + c)),
                  pl.BlockSpec((DN_CONV, cw), lambda c, i: (0, c)),
                  pl.BlockSpec((tm, cw), lambda c, i: (i, c)),
                  pl.BlockSpec((HALO, cw), lambda c, i: (jnp.minimum((i + 1) * (tm // HALO), tp // HALO - 1), c))],
        out_specs=(pl.BlockSpec((tm, cw), lambda c, i: (i, c)), pl.BlockSpec((DN_CONV, cw), lambda c, i: (0, c))),
        out_shape=(jax.ShapeDtypeStruct((tp, ncol * cw), F32), jax.ShapeDtypeStruct((DN_CONV, ncol * cw), F32)),
        scratch_shapes=[pltpu.VMEM((tm + HALO, cw), F32), pltpu.VMEM((tm + HALO, cw), F32)],
    )(proj, proj, conv_w, dy, dy)


def _stack(parts):
    return jnp.concatenate([p[None] for p in parts], axis=0)


def _chunk_masks():
    r = lax.broadcasted_iota(jnp.int32, (CHUNK, CHUNK), 0)
    c = lax.broadcasted_iota(jnp.int32, (CHUNK, CHUNK), 1)
    return (r >= c).astype(F32), (r > c).astype(F32), (r == c).astype(F32)


def _dn_chunk(y, z, small, s, a_log, dt_bias, norm_w, rows, known_inv=None):
    tri_incl, tri_strict, eye = _chunk_masks()
    lane = lax.broadcasted_iota(jnp.int32, (1, LANE), 1)
    dk = DN_HEAD_DIM
    nh = DN_HEADS
    heads = lambda t, first: _stack([t[:, (first + h) * dk:(first + h + 1) * dk] for h in range(nh)])
    pick = lambda t, l: jnp.sum(jnp.where(lane == l, t, 0.0), axis=1, keepdims=True)
    q = _l2n(_silu(heads(y, 0))) * dk ** -0.5
    k = _l2n(_silu(heads(y, nh)))
    v = _silu(heads(y, 2 * nh))
    g_all = jnp.where(lane < nh, -jnp.exp(a_log) * _softplus(small + dt_bias), 0.0) * rows
    beta_all = _sigmoid(small)
    gc_all = _cumsum_rows(g_all)
    g_sum = jnp.sum(g_all, axis=0, keepdims=True)
    gc = _stack([pick(gc_all, h) for h in range(nh)])
    beta = _stack([pick(beta_all, nh + h) for h in range(nh)])
    g_last = _stack([pick(g_sum, h) for h in range(nh)])
    gc_row = jnp.sum(eye * gc, axis=1, keepdims=True)
    gamma = jnp.exp((gc - gc_row) * tri_incl) * tri_incl
    k_beta = k * beta
    v_beta = v * beta
    a = _bmm_nt(k_beta, k) * gamma * tri_strict
    if known_inv is None:
        inv = eye - a
        power = a
        for _ in range(5):
            power = _bmm3(power, power)
            inv = inv + _bmm3(inv, power)
    else:
        inv = _known_inverse(a, known_inv)
    e_gc = jnp.exp(gc)
    uw = _bmm3(inv, jnp.concatenate([v_beta, k_beta * e_gc], axis=2))
    u, w = uw[:, :, :dk], uw[:, :, dk:]
    attn = _bmm_nt(q, k) * gamma
    q_dec = q * e_gc
    k_dec = k * jnp.exp(g_last - gc)
    v_new = u - _bmm(w, s)
    o = _bmm(q_dec, s) + _bmm(attn, v_new)
    s_new = s * jnp.exp(g_last) + _bmm_tn(k_dec, v_new)
    out = _rms(o, norm_w) * _silu(heads(z, 0))
    return jnp.concatenate([out[h] for h in range(nh)], axis=1), s_new, inv


Z_COL = 5
SMALL_COL = 24


def _chunk_rows(n):
    row = n * CHUNK + lax.broadcasted_iota(jnp.int32, (CHUNK, 1), 0)
    return (row >= PAD).astype(F32)


def dn_fwd(y, proj, a_log, dt_bias, norm_w, *, name):
    tp = y.shape[0]
    nc = tp // CHUNK
    dk = DN_HEAD_DIM

    def body(y_ref, z_ref, small_ref, al_ref, dt_ref, nw_ref, o_ref, ssave_ref, isave_ref, state):
        n = pl.program_id(0)

        @pl.when(n == 0)
        def _():
            state[...] = jnp.zeros_like(state)

        ssave_ref[0] = state[...]
        out, s_new, inv = _dn_chunk(y_ref[...], z_ref[...], small_ref[...], state[...], al_ref[...], dt_ref[...],
                                    nw_ref[...], _chunk_rows(n))
        o_ref[...] = out
        isave_ref[0] = inv
        state[...] = s_new

    return pl.pallas_call(
        body, name=name, grid=(nc,),
        in_specs=[pl.BlockSpec((CHUNK, y.shape[1]), lambda n: (n, 0)),
                  pl.BlockSpec((CHUNK, 4 * LANE), lambda n: (n, Z_COL)),
                  pl.BlockSpec((CHUNK, LANE), lambda n: (n, SMALL_COL)),
                  _full((1, LANE)), _full((1, LANE)), _full((1, LANE))],
        out_specs=(pl.BlockSpec((CHUNK, 4 * LANE), lambda n: (n, 0)),
                   pl.BlockSpec((1, DN_HEADS, dk, dk), lambda n: (n, 0, 0, 0)),
                   pl.BlockSpec((1, DN_HEADS, CHUNK, CHUNK), lambda n: (n, 0, 0, 0))),
        out_shape=(jax.ShapeDtypeStruct((tp, 4 * LANE), F32), jax.ShapeDtypeStruct((nc, DN_HEADS, dk, dk), F32),
                   jax.ShapeDtypeStruct((nc, DN_HEADS, CHUNK, CHUNK), F32)),
        scratch_shapes=[pltpu.VMEM((DN_HEADS, dk, dk), F32)],
    )(y, proj, proj, a_log, dt_bias, norm_w)


def dn_bwd(y, proj, a_log, dt_bias, norm_w, ssave, isave, do, *, name):
    tp = y.shape[0]
    nc = tp // CHUNK
    dk = DN_HEAD_DIM
    rev = lambda i: nc - 1 - i

    def body(y_ref, z_ref, small_ref, al_ref, dt_ref, nw_ref, ss_ref, is_ref, do_ref,
             dy_ref, dz_ref, dsmall_ref, dal_ref, ddt_ref, dnw_ref, dstate):
        i = pl.program_id(0)
        n = nc - 1 - i

        @pl.when(i == 0)
        def _():
            dstate[...] = jnp.zeros_like(dstate)
            dal_ref[...] = jnp.zeros_like(dal_ref)
            ddt_ref[...] = jnp.zeros_like(ddt_ref)
            dnw_ref[...] = jnp.zeros_like(dnw_ref)

        rows = _chunk_rows(n)
        known_inv = is_ref[0]
        fn = lambda *a: _dn_chunk(*a, rows, known_inv)[:2]
        _, vjp = jax.vjp(fn, y_ref[...], z_ref[...], small_ref[...], ss_ref[0], al_ref[...], dt_ref[...], nw_ref[...])
        dy, dz, dsmall, ds, dal, ddt, dnw = vjp((do_ref[...], dstate[...]))
        dy_ref[...] = dy
        dz_ref[...] = dz
        dsmall_ref[...] = dsmall
        dstate[...] = ds
        dal_ref[...] += dal
        ddt_ref[...] += ddt
        dnw_ref[...] += dnw

    return pl.pallas_call(
        body, name=name, grid=(nc,),
        in_specs=[pl.BlockSpec((CHUNK, y.shape[1]), lambda i: (rev(i), 0)),
                  pl.BlockSpec((CHUNK, 4 * LANE), lambda i: (rev(i), Z_COL)),
                  pl.BlockSpec((CHUNK, LANE), lambda i: (rev(i), SMALL_COL)),
                  _full((1, LANE)), _full((1, LANE)), _full((1, LANE)),
                  pl.BlockSpec((1, DN_HEADS, dk, dk), lambda i: (rev(i), 0, 0, 0)),
                  pl.BlockSpec((1, DN_HEADS, CHUNK, CHUNK), lambda i: (rev(i), 0, 0, 0)),
                  pl.BlockSpec((CHUNK, 4 * LANE), lambda i: (rev(i), 1))],
        out_specs=(pl.BlockSpec((CHUNK, y.shape[1]), lambda i: (rev(i), 0)),
                   pl.BlockSpec((CHUNK, 4 * LANE), lambda i: (rev(i), 0)),
                   pl.BlockSpec((CHUNK, LANE), lambda i: (rev(i), 0)),
                   _full((1, LANE)), _full((1, LANE)), _full((1, LANE))),
        out_shape=(jax.ShapeDtypeStruct((tp, y.shape[1]), F32), jax.ShapeDtypeStruct((tp, 4 * LANE), F32),
                   jax.ShapeDtypeStruct((tp, LANE), F32), jax.ShapeDtypeStruct((1, LANE), F32),
                   jax.ShapeDtypeStruct((1, LANE), F32), jax.ShapeDtypeStruct((1, LANE), F32)),
        scratch_shapes=[pltpu.VMEM((DN_HEADS, dk, dk), F32)],
    )(y, proj, proj, a_log, dt_bias, norm_w, ssave, isave, do)


def _gla_chunk(q, k, v, gate, low, s, w_gate_up, b_gate, norm_w, rows):
    tri_incl, _, _ = _chunk_masks()
    dk, dv, nh = GLA_DK, GLA_DV, GLA_HEADS
    heads = lambda t, width: _stack([t[:, h * width:(h + 1) * width] for h in range(nh)])
    logit = _mm3(low, w_gate_up) + b_gate
    glog_all = -_softplus(-logit) * (1.0 / GLA_GATE_NORM) * rows
    glog = heads(glog_all, dk)
    bcum = heads(_cumsum_rows(glog_all), dk)
    qh = heads(q, dk) * dk ** -0.5
    kh = heads(k, dk)
    vh = heads(v, dv)
    q_dec = qh * jnp.exp(bcum)
    attn = _bmm_nt(q_dec, kh * jnp.exp(-bcum)) * tri_incl
    b_last = jnp.sum(glog, axis=1, keepdims=True)
    k_dec = kh * jnp.exp(b_last - bcum)
    r = lax.broadcasted_iota(jnp.int32, (dk, dk), 0)
    c = lax.broadcasted_iota(jnp.int32, (dk, dk), 1)
    b_last_col = jnp.sum((r == c).astype(F32) * b_last, axis=2, keepdims=True)
    o = _bmm(attn, vh) + _bmm(q_dec, s)
    s_new = s * jnp.exp(b_last_col) + _bmm_tn(k_dec, vh)
    out = _rms(o, norm_w) * _silu(heads(gate, dv))
    return jnp.concatenate([out[h] for h in range(nh)], axis=1), s_new


LOW_COL = 24


def _gla_in_specs(step):
    return [pl.BlockSpec((CHUNK, 4 * LANE), lambda i: (step(i), 0)),
            pl.BlockSpec((CHUNK, 4 * LANE), lambda i: (step(i), 1)),
            pl.BlockSpec((CHUNK, 8 * LANE), lambda i: (step(i), 1)),
            pl.BlockSpec((CHUNK, 8 * LANE), lambda i: (step(i), 2)),
            pl.BlockSpec((CHUNK, LANE), lambda i: (step(i), LOW_COL)),
            _full((LANE, 4 * LANE)), _full((1, 4 * LANE)), _full((1, GLA_DV))]


def gla_fwd(proj, w_gate_up, b_gate, norm_w, *, name):
    tp = proj.shape[0]
    nc = tp // CHUNK

    def body(q_ref, k_ref, v_ref, g_ref, low_ref, wgu_ref, bg_ref, nw_ref, o_ref, ssave_ref, state):
        n = pl.program_id(0)

        @pl.when(n == 0)
        def _():
            state[...] = jnp.zeros_like(state)

        ssave_ref[0] = state[...]
        out, s_new = _gla_chunk(q_ref[...], k_ref[...], v_ref[...], g_ref[...], low_ref[...], state[...], wgu_ref[...],
                                bg_ref[...], nw_ref[...], _chunk_rows(n))
        o_ref[...] = out
        state[...] = s_new

    return pl.pallas_call(
        body, name=name, grid=(nc,),
        in_specs=_gla_in_specs(lambda i: i),
        out_specs=(pl.BlockSpec((CHUNK, 8 * LANE), lambda n: (n, 0)),
                   pl.BlockSpec((1, GLA_HEADS, GLA_DK, GLA_DV), lambda n: (n, 0, 0, 0))),
        out_shape=(jax.ShapeDtypeStruct((tp, 8 * LANE), F32),
                   jax.ShapeDtypeStruct((nc, GLA_HEADS, GLA_DK, GLA_DV), F32)),
        scratch_shapes=[pltpu.VMEM((GLA_HEADS, GLA_DK, GLA_DV), F32)],
    )(proj, proj, proj, proj, proj, w_gate_up, b_gate, norm_w)


def gla_bwd(proj, w_gate_up, b_gate, norm_w, ssave, do, *, name):
    tp = proj.shape[0]
    nc = tp // CHUNK
    rev = lambda i: nc - 1 - i

    def body(q_ref, k_ref, v_ref, g_ref, low_ref, wgu_ref, bg_ref, nw_ref, ss_ref, do_ref,
             dq_ref, dk_ref, dv_ref, dg_ref, dlow_ref, dwgu_ref, dbg_ref, dnw_ref, dstate):
        i = pl.program_id(0)
        n = nc - 1 - i

        @pl.when(i == 0)
        def _():
            dstate[...] = jnp.zeros_like(dstate)
            dwgu_ref[...] = jnp.zeros_like(dwgu_ref)
            dbg_ref[...] = jnp.zeros_like(dbg_ref)
            dnw_ref[...] = jnp.zeros_like(dnw_ref)

        rows = _chunk_rows(n)
        fn = lambda *a: _gla_chunk(*a, rows)
        _, vjp = jax.vjp(fn, q_ref[...], k_ref[...], v_ref[...], g_ref[...], low_ref[...], ss_ref[0], wgu_ref[...],
                         bg_ref[...], nw_ref[...])
        dq, dk, dv, dg, dlow, ds, dwgu, dbg, dnw = vjp((do_ref[...], dstate[...]))
        dq_ref[...] = dq
        dk_ref[...] = dk
        dv_ref[...] = dv
        dg_ref[...] = dg
        dlow_ref[...] = dlow
        dstate[...] = ds
        dwgu_ref[...] += dwgu
        dbg_ref[...] += dbg
        dnw_ref[...] += dnw

    chunk = lambda width: pl.BlockSpec((CHUNK, width), lambda i: (rev(i), 0))
    return pl.pallas_call(
        body, name=name, grid=(nc,),
        in_specs=_gla_in_specs(rev) + [pl.BlockSpec((1, GLA_HEADS, GLA_DK, GLA_DV), lambda i: (rev(i), 0, 0, 0)),
                                       chunk(8 * LANE)],
        out_specs=(chunk(4 * LANE), chunk(4 * LANE), chunk(8 * LANE), chunk(8 * LANE), chunk(LANE),
                   _full((LANE, 4 * LANE)), _full((1, 4 * LANE)), _full((1, GLA_DV))),
        out_shape=(jax.ShapeDtypeStruct((tp, 4 * LANE), F32), jax.ShapeDtypeStruct((tp, 4 * LANE), F32),
                   jax.ShapeDtypeStruct((tp, 8 * LANE), F32), jax.ShapeDtypeStruct((tp, 8 * LANE), F32),
                   jax.ShapeDtypeStruct((tp, LANE), F32), jax.ShapeDtypeStruct((LANE, 4 * LANE), F32),
                   jax.ShapeDtypeStruct((1, 4 * LANE), F32), jax.ShapeDtypeStruct((1, GLA_DV), F32)),
        scratch_shapes=[pltpu.VMEM((GLA_HEADS, GLA_DK, GLA_DV), F32)],
    )(proj, proj, proj, proj, proj, w_gate_up, b_gate, norm_w, ssave, do)


def _even_proj_weight(w_in):
    hd = SWA_HEAD_DIM
    k0, k1 = w_in[:, 512:512 + hd], w_in[:, 512 + hd:640]
    v0, v1 = w_in[:, 640:640 + hd], w_in[:, 640 + hd:768]
    zeros = jnp.zeros((w_in.shape[0], LANE - 2 * DN_HEADS), w_in.dtype)
    return jnp.concatenate([w_in[:, :512], k0, k0, k1, k1, v0, v0, v1, v1, w_in[:, 768:2816], w_in[:, 2820:2824],
                            w_in[:, 2816:2820], zeros], axis=1)


def _even_proj_weight_grad(dw):
    hd = SWA_HEAD_DIM
    c = lambda i: dw[:, 512 + i * hd:512 + (i + 1) * hd]
    return jnp.concatenate([dw[:, :512], c(0) + c(1), c(2) + c(3), c(4) + c(5), c(6) + c(7), dw[:, 1024:3072],
                            dw[:, 3076:3080], dw[:, 3072:3076]], axis=1)


def _ffn_fwd(h, nw_in, nw_out, wts, idx, get_w):
    wts.update(get_w(f"ffn{idx}", h))
    w_gu = wts[f"w_gu{idx}"]
    hn, g, u, a = rms_mm(h, nw_in, w_gu[0], swiglu=True, name=f"ffn_up_{idx}", widx=w_gu[1])
    wts.update(get_w(f"down{idx}", a))
    w_down = wts[f"w_down{idx}"]
    f, h_out = mm_rms_res([a], w_down[0], h, nw_out, scale=0.5, name=f"ffn_down_{idx}", widx=w_down[1])
    return h_out, (h, hn, g, u, a, f)


def _ffn_bwd(dho, saved, nw_in, nw_out, w_gu, w_down, idx, on_grads):
    h, hn, g, u, a, f = saved
    df, dnw_out, dgu = mm_rms_res_bwd(dho, f, nw_out, w_down[0], (g, u), scale=0.5, name=f"ffn_down_bwd_{idx}",
                                      widx=w_down[1])
    g_down = mm_tn(a, df, name=f"ffn_dwd_{idx}", out_dtype=BF16)
    sent = on_grads("down", g_down)
    g_gu = mm_tn(hn, dgu, name=f"ffn_dwgu_{idx}", out_dtype=BF16, after=sent)
    sent = on_grads("gu", g_gu)
    dh, dnw_in = rms_mm_bwd([dgu], w_gu[0], h, nw_in + sent, dho, name=f"ffn_up_bwd_{idx}", widx=w_gu[1])
    return dh, dnw_in, dnw_out


def local_step(x, target, wts, get_w=None, put_g=None):
    seq, d = x.shape
    wts = dict(wts)
    get_w = get_w or (lambda stage, after: {})
    put_g = put_g or (lambda stage, grads: jnp.zeros((1, 1), F32))
    row = lambda v: v.reshape(1, -1)
    lane_row = lambda v: jnp.pad(v.reshape(1, -1), ((0, 0), (0, LANE - v.size)))
    nw = wts["norm_w"]
    h = jnp.concatenate([jnp.zeros((PAD, d), F32), wts["meta_tokens"], x], axis=0)
    buckets = _swa_buckets()
    bias = swa_bias(wts["rel_bias_table"], buckets, name="swa_bias")
    sinks = lane_row(wts["swa_sinks"])
    a_log, dt_bias = lane_row(wts["dn_a_log"]), lane_row(wts["dn_dt_bias"])
    dn_norm_w = row(wts["dn_norm_w"])
    conv_w = wts["even_conv_w"][0]
    w_gate_up = jnp.pad(wts["gla_w_gate_up"][0], ((0, LANE - GLA_GATE_RANK), (0, 0)))
    b_gate, gla_norm_w = row(wts["gla_b_gate"]), row(wts["gla_norm_w"])

    saved = []
    w_in, w_out = [None, None], [None, None]
    for l in range(2):
        h, s_a = _ffn_fwd(h, row(nw[l, 0]), row(nw[l, 1]), wts, 2 * l, get_w)
        if l == 0:
            wts.update(get_w("even", h))
            w_in[0], w_out[0] = _even_proj_weight(wts["even_w_in"]), wts["even_w_out"]
        else:
            wts.update(get_w("odd", h))
            w_in[1] = jnp.pad(wts["odd_w_in"], ((0, 0), (0, PROJ_DIM - wts["odd_w_in"].shape[1])))
            w_out[1] = wts["odd_w_out"]
        h_mix = h
        hn, proj = rms_mm(h, row(nw[l, 2]), w_in[l], swiglu=False, name=f"mix_in_{l}")
        if l == 0:
            o_a = swa_fwd(proj, bias, sinks, name="swa_fwd")
            y = conv_fwd(proj, conv_w, name="conv_fwd")
            o_b, ssave, isave = dn_fwd(y, proj, a_log, dt_bias, dn_norm_w, name="dn_fwd")
            acts, extra = [o_a, o_b], (y, ssave, isave)
        else:
            o, ssave = gla_fwd(proj, w_gate_up, b_gate, gla_norm_w, name="gla_fwd")
            acts, extra = [o], (ssave,)
        mix, h = mm_rms_res(acts, w_out[l], h, row(nw[l, 3]), scale=1.0, name=f"mix_out_{l}")
        s_m = (h_mix, hn, proj, acts, extra, mix)
        h, s_b = _ffn_fwd(h, row(nw[l, 4]), row(nw[l, 5]), wts, 2 * l + 1, get_w)
        saved.append((s_a, s_m, s_b))

    dh, loss = loss_and_grad(h, target, name="loss")

    grads = {}
    dnw = [[None] * 6 for _ in range(2)]
    def on_grads(i):
        def put(which, g):
            grads[f"g_{which}{i}"] = g
            return put_g(f"{which}{i}", grads)
        return put

    for l in (1, 0):
        s_a, s_m, s_b = saved[l]
        i = 2 * l + 1
        dh, dnw[l][4], dnw[l][5] = _ffn_bwd(dh, s_b, row(nw[l, 4]), row(nw[l, 5]), wts[f"w_gu{i}"], wts[f"w_down{i}"],
                                            i, on_grads(i))
        h_mix, hn, proj, acts, extra, mix = s_m
        dmix, dnw[l][3], do = mm_rms_res_bwd(dh, mix, row(nw[l, 3]), w_out[l], None, scale=1.0, name=f"mix_out_bwd_{l}")
        dw_out = jnp.concatenate([mm_tn(a, dmix, name=f"mix_dwo_{l}_{i}") for i, a in enumerate(acts)], axis=0)
        sent = jnp.zeros((1, 1), F32)
        if l == 0:
            y, ssave, isave = extra
            dq, dkv, dbias, dsinks = swa_bwd(proj, bias, sinks, do, name="swa_bwd")
            dy, dz, dsmall, da_log, ddt_bias, ddn_norm_w = dn_bwd(y, proj, a_log, dt_bias, dn_norm_w, ssave, isave, do,
                                                                    name="dn_bwd")
            dxc, dconv_w = conv_bwd(proj, conv_w, dy, name="conv_bwd")
            dps = [dq, dkv, dxc, dz, dsmall]
            grads["rel_bias_table"] = swa_bias_bwd(dbias, buckets, name="swa_bias_bwd")[:, :SWA_Q_HEADS]
            grads["swa_sinks"] = dsinks[:, :SWA_Q_HEADS]
            grads["dn_a_log"] = da_log[:, :DN_HEADS]
            grads["dn_dt_bias"] = ddt_bias[:, :DN_HEADS]
            grads["dn_norm_w"] = ddn_norm_w
            grads["even_conv_w"] = dconv_w[None]
            grads["even_w_out"] = dw_out
        else:
            (ssave,) = extra
            dq, dk, dv, dgate, dlow, dwgu, dbg, dgnw = gla_bwd(proj, w_gate_up, b_gate, gla_norm_w, ssave, do,
                                                               name="gla_bwd")
            dps = [dq, dk, dv, dgate, dlow]
            grads["gla_w_gate_up"] = dwgu[None, :GLA_GATE_RANK]
            grads["gla_b_gate"] = dbg
            grads["gla_norm_w"] = dgnw
            grads["odd_w_out"] = dw_out
        dw_in = jnp.concatenate([mm_tn(hn, dp, name=f"mix_dwi_{l}_{i}") for i, dp in enumerate(dps)], axis=1)
        if l == 0:
            grads["even_w_in"] = _even_proj_weight_grad(dw_in)
            sent = put_g("even", grads)
        else:
            grads["odd_w_in"] = dw_in[:, :wts["odd_w_in"].shape[1]]
        dh, dnw[l][2] = rms_mm_bwd(dps, w_in[l], h_mix, row(nw[l, 2]) + sent, dh, name=f"mix_in_bwd_{l}")
        i = 2 * l
        dh, dnw[l][0], dnw[l][1] = _ffn_bwd(dh, s_a, row(nw[l, 0]), row(nw[l, 1]), wts[f"w_gu{i}"], wts[f"w_down{i}"],
                                            i, on_grads(i))

    grads["norm_w"] = jnp.stack([jnp.concatenate(r, axis=0) for r in dnw])
    grads["meta_tokens"] = dh[PAD:PAD + N_META]
    return loss[0, 0], dh[PAD + N_META:], grads


def _peer(k):
    x, y, c = (lax.axis_index(a) for a in AXES)
    flip = lambda v, bit: 1 - v if bit else v
    return (flip(x, k & 4), flip(y, k & 2), flip(c, k & 1))


def _my_index():
    x, y, c = (lax.axis_index(a) for a in AXES)
    return 4 * x + 2 * y + c


_HBM = pl.BlockSpec(memory_space=pltpu.HBM)
_SEM = pl.BlockSpec(memory_space=pltpu.SEMAPHORE)
_EFFECT = pltpu.SideEffectType.DATAFLOW_SIDE_EFFECTING


def _remote_copies(items, src_refs, land_refs, send_sems, recv_sems):
    me = _my_index()
    copies = []
    for k in range(1, N_DEV):
        px, py, pc = _peer(k)
        pj = 4 * px + 2 * py + pc
        for a, (sn, send, ln, land, _) in enumerate(items):
            sem = (k - 1) * len(items) + a
            copies.append(pltpu.make_async_remote_copy(
                src_ref=send(src_refs[sn], pj), dst_ref=land(land_refs[ln], me), send_sem=send_sems.at[sem],
                recv_sem=recv_sems.at[sem], device_id=(px, py, pc), device_id_type=MESH))
    return copies


def exchange(srcs, lands, items, after, *, name):
    sn, ln = list(srcs), list(lands)

    def body(*refs):
        src_refs = dict(zip(sn, refs[:len(sn)]))
        land_refs = dict(zip(ln, refs[len(sn) + len(ln) + 1:len(sn) + 2 * len(ln) + 1]))
        send_sems, recv_sems = refs[len(sn) + 2 * len(ln) + 1:]
        copies = _remote_copies(items, src_refs, land_refs, send_sems, recv_sems)
        for cp in copies:
            cp.start()
        for cp in copies:
            cp.wait_recv()
        for cp in copies:
            cp.wait_send()

    n_remote = (N_DEV - 1) * len(items)
    outs = pl.pallas_call(
        body, name=name,
        in_specs=[pl.BlockSpec(memory_space=pl.ANY)] * (len(sn) + len(ln) + 1),
        out_specs=tuple(pl.BlockSpec(memory_space=pl.ANY) for _ in ln),
        out_shape=tuple(jax.ShapeDtypeStruct(lands[n].shape, lands[n].dtype) for n in ln),
        input_output_aliases={len(sn) + i: i for i in range(len(ln))},
        scratch_shapes=[pltpu.SemaphoreType.DMA((n_remote,)), pltpu.SemaphoreType.DMA((n_remote,))],
    )(*[srcs[n] for n in sn], *[lands[n] for n in ln], after)
    return dict(zip(ln, outs))


def start_copies(srcs, lands, items, *, name):
    sn, ln = list(srcs), list(lands)
    n_remote = (N_DEV - 1) * len(items)

    def body(*refs):
        src_refs = dict(zip(sn, refs[:len(sn)]))
        land_refs = dict(zip(ln, refs[len(sn):len(sn) + len(ln)]))
        send_sems, recv_sems = refs[len(sn) + len(ln):len(sn) + len(ln) + 2]
        token = refs[-1]
        for cp in _remote_copies(items, src_refs, land_refs, send_sems, recv_sems):
            cp.start()
        token[...] = jnp.zeros_like(token)

    hbm = lambda a: pltpu.with_memory_space_constraint(a, pltpu.HBM)
    outs = pl.pallas_call(
        body, name=name,
        in_specs=[_HBM] * (len(sn) + len(ln)),
        out_specs=(_SEM, _SEM) + (_HBM,) * len(ln) + (pl.BlockSpec(memory_space=pltpu.VMEM),),
        out_shape=(pltpu.SemaphoreType.DMA((n_remote,)), pltpu.SemaphoreType.DMA((n_remote,)))
        + tuple(pltpu.HBM(lands[n].shape, lands[n].dtype) for n in ln) + (jax.ShapeDtypeStruct((8, LANE), F32),),
        input_output_aliases={len(sn) + i: 2 + i for i in range(len(ln))},
        compiler_params=pltpu.CompilerParams(has_side_effects=_EFFECT),
    )(*[hbm(srcs[n]) for n in sn], *[hbm(lands[n]) for n in ln])
    return (outs[0], outs[1]), dict(zip(ln, outs[2:2 + len(ln)])), outs[-1][0:1, 0:1]


def wait_copies(sems, srcs, lands, items, after, *, name):
    sn, ln = list(srcs), list(lands)

    def body(*refs):
        src_refs = dict(zip(sn, refs[:len(sn)]))
        land_refs = dict(zip(ln, refs[len(sn):len(sn) + len(ln)]))
        send_sems, recv_sems = refs[len(sn) + len(ln):len(sn) + len(ln) + 2]
        copies = _remote_copies(items, src_refs, land_refs, send_sems, recv_sems)
        for cp in copies:
            cp.wait_send()
        for cp in copies:
            cp.wait_recv()

    outs = pl.pallas_call(
        body, name=name,
        in_specs=[_HBM] * (len(sn) + len(ln)) + [_SEM, _SEM, pl.BlockSpec(memory_space=pl.ANY)],
        out_specs=(_HBM,) * len(ln),
        out_shape=tuple(pltpu.HBM(lands[n].shape, lands[n].dtype) for n in ln),
        input_output_aliases={len(sn) + i: i for i in range(len(ln))},
        compiler_params=pltpu.CompilerParams(has_side_effects=_EFFECT),
    )(*[srcs[n] for n in sn], *[lands[n] for n in ln], sems[0], sems[1], after)
    return dict(zip(ln, outs))


def _block(index, size):
    return pl.ds(pl.multiple_of(index * size, LANE), size)


def _adam_tile(rows):
    for t in (256, 176, 128):
        if rows % t == 0:
            return t
    return rows


def sum_adamw(recvs, col0, w, m, v, *, name, first_slab=0, into=None):
    _, r, c = w.shape
    b = len(recvs)
    cp = recvs[0].shape[-1]
    tr = _adam_tile(r)
    c1 = 1.0 / (1.0 - ADAM_B1 ** ADAM_STEP)
    c2 = 1.0 / (1.0 - ADAM_B2 ** ADAM_STEP)

    def body(*refs):
        recv_refs = refs[:b]
        w_ref, m_ref, v_ref = refs[b:b + 3]
        g_ref, d_ref, nm_ref, nv_ref = refs[-4:]
        for slab, recv_ref in enumerate(recv_refs):
            @pl.when(pl.program_id(0) == slab)
            def _():
                g = recv_ref[0, :, col0:col0 + c].astype(F32)
                for i in range(1, N_DEV):
                    g = g + recv_ref[i, :, col0:col0 + c].astype(F32)
                nm = ADAM_B1 * m_ref[0] + (1.0 - ADAM_B1) * g
                nv = ADAM_B2 * v_ref[0] + (1.0 - ADAM_B2) * (g * g)
                g_ref[0] = g
                nm_ref[0] = nm
                nv_ref[0] = nv
                d_ref[0] = -ADAM_LR * ((nm * c1) / (jnp.sqrt(nv * c2) + ADAM_EPS) + ADAM_WD * w_ref[0])

    tile = pl.BlockSpec((1, tr, c), lambda bi, i: (first_slab + bi, i, 0))
    piece = lambda slab: pl.BlockSpec((N_DEV, tr, cp), lambda bi, i: (0, jnp.where(bi == slab, i, 0), 0))
    earlier = [] if into is None else list(into)
    return pl.pallas_call(
        body, name=name, grid=(b, r // tr),
        in_specs=[piece(slab) for slab in range(b)] + [tile, tile, tile] + [pl.BlockSpec(memory_space=pl.ANY)] * len(earlier),
        out_specs=(tile,) * 4, out_shape=(jax.ShapeDtypeStruct(w.shape, F32),) * 4,
        input_output_aliases={b + 3 + i: i for i in range(len(earlier))},
    )(*recvs, w, m, v, *earlier)


def _flat_rows(n_elems, row_multiple):
    rows = -(-n_elems // FLAT_COLS)
    return -(-rows // row_multiple) * row_multiple


def _pack(arrays, row_multiple, dtype):
    flat = jnp.concatenate([a.reshape(-1).astype(dtype) for a in arrays])
    rows = _flat_rows(flat.size, row_multiple)
    return jnp.pad(flat, (0, rows * FLAT_COLS - flat.size)).reshape(rows, FLAT_COLS)


def _unpack(flat2d, shapes):
    lead = flat2d.shape[:-2]
    flat = flat2d.reshape(lead + (-1,))
    out, off = [], 0
    for shp in shapes:
        n = int(np.prod(shp))
        out.append(flat[..., off:off + n].reshape(lead + tuple(shp)))
        off += n
    return out


def _join_shards(stacked, axis):
    moved = jnp.moveaxis(stacked, 0, axis)
    shp = list(moved.shape)
    shp[axis:axis + 2] = [shp[axis] * shp[axis + 1]]
    return moved.reshape(shp)


def _split_shards(full, axis):
    shp = list(full.shape)
    shp[axis:axis + 1] = [N_DEV, shp[axis] // N_DEV]
    return jnp.moveaxis(full.reshape(shp), axis, 0)


def kernel(x, meta_tokens, norm_w, ffn_w_gate, ffn_w_up, ffn_w_down, rel_bias_table, even_w_in, even_conv_w, swa_sinks, dn_a_log, dn_dt_bias, dn_norm_w, even_w_out, odd_w_in, gla_w_gate_up, gla_b_gate, gla_norm_w, odd_w_out, loss_target, m_meta_tokens, m_norm_w, m_ffn_w_gate, m_ffn_w_up, m_ffn_w_down, m_rel_bias_table, m_even_w_in, m_even_conv_w, m_swa_sinks, m_dn_a_log, m_dn_dt_bias, m_dn_norm_w, m_even_w_out, m_odd_w_in, m_gla_w_gate_up, m_gla_b_gate, m_gla_norm_w, m_odd_w_out, v_meta_tokens, v_norm_w, v_ffn_w_gate, v_ffn_w_up, v_ffn_w_down, v_rel_bias_table, v_even_w_in, v_even_conv_w, v_swa_sinks, v_dn_a_log, v_dn_dt_bias, v_dn_norm_w, v_even_w_out, v_odd_w_in, v_gla_w_gate_up, v_gla_b_gate, v_gla_norm_w, v_odd_w_out):
    args = locals()
    w = {n: args[n] for n in WEIGHTS}
    m = {n: args["m_" + n] for n in WEIGHTS}
    v = {n: args["v_" + n] for n in WEIGHTS}

    d = D_MODEL
    me = _my_index()
    whole = lambda ref, j: ref
    cols = lambda size, base=0: (lambda ref, i: ref.at[(slice(None),) * (len(ref.shape) - 1)
                                                       + (pl.ds(pl.multiple_of(base + i * size, LANE), size),)])
    rows3 = lambda size: (lambda ref, i: ref.at[:, _block(i, size), :])
    rows2 = lambda size: (lambda ref, i: ref.at[_block(i, size), :])
    lead = lambda ref, i: ref.at[i]
    of_group = lambda items, g: [it for it in items if it[4] == g]
    names = lambda items, k: list(dict.fromkeys(it[k] for it in items))

    def placed(shape, dtype, parts):
        land = lax.empty(shape, dtype)
        for part, axis, start in parts:
            land = lax.dynamic_update_slice(land, part, tuple(start if a == axis else 0 for a in range(land.ndim)))
        return land

    pad_cols = lambda a, to: jnp.pad(a, [(0, 0)] * (a.ndim - 1) + [(0, to - a.shape[-1])])
    gate_s = pad_cols(w["ffn_w_gate"].reshape(N_FFN, d, FF_SHARD), FF_SHARD_PAD).astype(BF16)
    up_s = pad_cols(w["ffn_w_up"].reshape(N_FFN, d, FF_SHARD), FF_SHARD_PAD).astype(BF16)
    down_s = jnp.pad(w["ffn_w_down"].reshape(N_FFN, FF_SHARD, d),
                     ((0, 0), (0, FF_SHARD_PAD - FF_SHARD), (0, 0))).astype(BF16)
    small_s = _pack([w[n] for n in SMALL], 8, F32)
    srcs_w = {"ein": pad_cols(w["even_w_in"][0], EVEN_IN_SHARD_PAD).astype(BF16), "eout": w["even_w_out"][0].astype(BF16),
              "small": small_s, "oin": pad_cols(w["odd_w_in"][0], ODD_IN_SHARD_PAD).astype(BF16),
              "oout": w["odd_w_out"][0].astype(BF16)}
    lands_w = {"ein": placed((d, N_DEV * EVEN_IN_SHARD_PAD), BF16, [(srcs_w["ein"], 1, me * EVEN_IN_SHARD_PAD)]),
               "oin": placed((d, N_DEV * ODD_IN_SHARD_PAD), BF16, [(srcs_w["oin"], 1, me * ODD_IN_SHARD_PAD)]),
               "eout": placed((d, d), BF16, [(srcs_w["eout"], 0, me * OUT_SHARD)]),
               "oout": placed((d, d), BF16, [(srcs_w["oout"], 0, me * OUT_SHARD)]),
               "small": placed((N_DEV,) + small_s.shape, F32, [(small_s[None], 0, me)])}
    items_w = [("small", whole, "small", lead, "first"), ("ein", whole, "ein", cols(EVEN_IN_SHARD_PAD), "even"),
               ("eout", whole, "eout", rows2(OUT_SHARD), "even"), ("oin", whole, "oin", cols(ODD_IN_SHARD_PAD), "odd"),
               ("oout", whole, "oout", rows2(OUT_SHARD), "odd")]
    for i, group, down_group in ((0, "first", "down0"), (1, "ffn1", "ffn1"), (2, "ffn2", "down2"), (3, "ffn3", "ffn3")):
        srcs_w.update({f"gate{i}": gate_s[i], f"up{i}": up_s[i], f"down{i}": down_s[i]})
        lands_w[f"w_gu{i}"] = placed((d, 2 * FF_PAD), BF16, [(srcs_w[f"gate{i}"], 1, me * FF_SHARD_PAD),
                                                             (srcs_w[f"up{i}"], 1, FF_PAD + me * FF_SHARD_PAD)])
        lands_w[f"w_down{i}"] = placed((FF_PAD, d), BF16, [(srcs_w[f"down{i}"], 0, me * FF_SHARD_PAD)])
        items_w += [(f"gate{i}", whole, f"w_gu{i}", cols(FF_SHARD_PAD), group),
                    (f"up{i}", whole, f"w_gu{i}", cols(FF_SHARD_PAD, FF_PAD), group),
                    (f"down{i}", whole, f"w_down{i}", rows2(FF_SHARD_PAD), down_group)]
    pending, started = {}, []
    for g in ("first", "down0", "even", "ffn1", "ffn2", "down2", "odd", "ffn3"):
        its = of_group(items_w, g)
        srcs = {n: srcs_w[n] for n in names(its, 0)}
        sems, lands, token = start_copies(srcs, {n: lands_w[n] for n in names(its, 2)}, its, name=f"gather_start_{g}")
        pending[g] = (sems, srcs, lands, its)
        started.append(token)

    unpad = lambda p, shard, shard_pad: p.reshape(d, N_DEV, shard_pad)[:, :, :shard].reshape(d, N_DEV * shard)

    def get_w(stage, after):
        if stage not in pending:
            return {}
        sems, srcs, lands, its = pending[stage]
        landed = wait_copies(sems, srcs, lands, its, after, name=f"gather_wait_{stage}")
        got = {}
        for n, arr in landed.items():
            if n == "small":
                for sn, stacked in zip(SMALL, _unpack(arr, [w[sn].shape for sn in SMALL])):
                    got[sn] = _join_shards(stacked, SHARD_AXIS[sn])
            elif n == "ein":
                got["even_w_in"] = unpad(arr, EVEN_IN_SHARD, EVEN_IN_SHARD_PAD)
            elif n == "oin":
                got["odd_w_in"] = unpad(arr, ODD_IN_SHARD, ODD_IN_SHARD_PAD)
            elif n in ("eout", "oout"):
                got["even_w_out" if n == "eout" else "odd_w_out"] = arr
            else:
                got[n] = (arr, None)
        return got

    full = {n: w[n] for n in REPL}
    full.update(get_w("first", sum(started)))

    repad = lambda g, shard, shard_pad: pad_cols(g.reshape(d, N_DEV, shard), shard_pad).reshape(d, N_DEV * shard_pad)
    mine = lambda g, axis, size, base=0: lax.dynamic_slice_in_dim(g, base + me * size, size, axis)
    half = lambda h: (lambda ref, i: ref.at[i, :, pl.ds(h * FF_SHARD_PAD, FF_SHARD_PAD)])
    items_g = [("oin", cols(ODD_IN_SHARD_PAD), "r_oin", lead, "gu2"), ("oout", rows2(OUT_SHARD), "r_oout", lead, "gu2"),
               ("ein", cols(EVEN_IN_SHARD_PAD), "r_ein", lead, "even"), ("eout", rows2(OUT_SHARD), "r_eout", lead, "even"),
               ("small", lead, "r_small", lead, "last")]
    for i in range(N_FFN):
        items_g += [(f"g_gu{i}", cols(FF_SHARD_PAD), f"r_gu{i}", half(0), f"gu{i}"),
                    (f"g_gu{i}", cols(FF_SHARD_PAD, FF_PAD), f"r_gu{i}", half(1), f"gu{i}"),
                    (f"g_down{i}", rows2(FF_SHARD_PAD), f"r_down{i}", lead, "down0" if i == 0 else f"gu{i}")]
    last_groups = ("down0", "gu0")

    def grad_src(n, grads):
        if n == "oin":
            return repad(grads["odd_w_in"], ODD_IN_SHARD, ODD_IN_SHARD_PAD).astype(BF16)
        if n == "ein":
            return repad(grads["even_w_in"], EVEN_IN_SHARD, EVEN_IN_SHARD_PAD).astype(BF16)
        if n in ("oout", "eout"):
            return grads["odd_w_out" if n == "oout" else "even_w_out"].astype(BF16)
        return grads[n]

    def grad_land(n, srcs):
        if n.startswith("r_gu"):
            g = srcs["g_gu" + n[4:]]
            own = jnp.concatenate([mine(g, 1, FF_SHARD_PAD), mine(g, 1, FF_SHARD_PAD, FF_PAD)], axis=1)
        elif n.startswith("r_down"):
            own = mine(srcs["g_down" + n[6:]], 0, FF_SHARD_PAD)
        elif n == "r_small":
            own = lax.dynamic_index_in_dim(srcs["small"], me, 0, keepdims=False)
        else:
            axis, size = {"r_oin": (1, ODD_IN_SHARD_PAD), "r_ein": (1, EVEN_IN_SHARD_PAD), "r_oout": (0, OUT_SHARD),
                          "r_eout": (0, OUT_SHARD)}[n]
            own = mine(srcs[n[2:]], axis, size)
        return placed((N_DEV,) + own.shape, own.dtype, [(own[None], 0, me)])

    sent = {}

    def put_g(stage, grads):
        its = of_group(items_g, stage)
        if not its:
            return jnp.zeros((1, 1), F32)
        srcs = {n: grad_src(n, grads) for n in names(its, 0)}
        lands = {n: grad_land(n, srcs) for n in names(its, 2)}
        sems, lands, token = start_copies(srcs, lands, its, name=f"grads_start_{stage}")
        sent[stage] = (sems, srcs, lands, its)
        return token

    loss, grad_x, grads = local_step(x[0], loss_target[0], full, get_w, put_g)
    loss = lax.psum(loss, AXES)

    order = SMALL + REPL
    pieces = [_split_shards(grads[n].reshape(full[n].shape), SHARD_AXIS[n]) if n in SHARD_AXIS
              else jnp.broadcast_to(grads[n].reshape(w[n].shape)[None], (N_DEV,) + w[n].shape) for n in order]
    flat = jnp.concatenate([p.reshape(N_DEV, -1) for p in pieces], axis=1)
    srows = _flat_rows(flat.shape[1], 8)
    grads["small"] = jnp.pad(flat, ((0, 0), (0, srows * FLAT_COLS - flat.shape[1]))).reshape(N_DEV, srows, FLAT_COLS)
    recv = {}
    for stage, (sems, srcs, lands, its) in sent.items():
        if stage not in last_groups:
            recv.update(wait_copies(sems, srcs, lands, its, grad_x, name=f"grads_wait_{stage}"))
    result = [{} for _ in range(4)]

    def adam(n, recvs, col0, view, first_slab=0, into=None):
        return sum_adamw(recvs, col0, view(w[n]), view(m[n]), view(v[n]), name=f"adamw_{n}_{first_slab}",
                         first_slab=first_slab, into=into)

    def finish(n, outs):
        for r, o in zip(result, outs):
            r[n] = o.reshape(w[n].shape)

    ffn_views = (("ffn_w_gate", "r_gu", 0, lambda a: a.reshape(N_FFN, d, FF_SHARD)),
                 ("ffn_w_up", "r_gu", FF_SHARD_PAD, lambda a: a.reshape(N_FFN, d, FF_SHARD)),
                 ("ffn_w_down", "r_down", 0, lambda a: a.reshape(N_FFN, FF_SHARD, d)))
    early = {n: adam(n, [recv[f"{r}{i}"] for i in (1, 2, 3)], col0, view, first_slab=1) for n, r, col0, view in ffn_views}
    for n, r in (("even_w_in", "r_ein"), ("odd_w_in", "r_oin"), ("even_w_out", "r_eout"), ("odd_w_out", "r_oout")):
        finish(n, adam(n, [recv[r]], 0, lambda a: a))
    srcs = {"small": grads["small"]}
    recv.update(exchange(srcs, {"r_small": grad_land("r_small", srcs)}, of_group(items_g, "last"),
                         early["ffn_w_down"][0], name="exchange_small"))
    for stage in last_groups:
        sems, srcs, lands, its = sent[stage]
        recv.update(wait_copies(sems, srcs, lands, its, recv["r_small"], name=f"grads_wait_{stage}"))
    for n, r, col0, view in ffn_views:
        finish(n, adam(n, [recv[f"{r}0"]], col0, view, into=early[n]))
    pack_local = lambda t: _pack([t[n] for n in order], 8, F32)[None]
    small_outs = sum_adamw([recv["r_small"]], 0, pack_local(w), pack_local(m), pack_local(v), name="adamw_small")
    for r, o in zip(result, small_outs):
        r.update(zip(order, _unpack(o[0], [w[n].shape for n in order])))
    return (loss, grad_x[None], *[r[n] for r in result for n in WEIGHTS])
```

```python
import functools
import math

import numpy as np
import jax
import jax.numpy as jnp
from jax import lax
from jax.experimental import pallas as pl
from jax.experimental.pallas import tpu as pltpu

F32 = jnp.float32
BF16 = jnp.bfloat16
MESH = pl.DeviceIdType.MESH
AXES = ("x", "y", "c")
N_DEV = 8

D_MODEL = 1024
N_META = 16
D_FF = 2816
NORM_EPS = 1e-6
NEG_INF = -1e30
SWA_Q_HEADS = 8
SWA_HEAD_DIM = 64
SWA_WINDOW = 128
SWA_BLOCK = 128
REL_BUCKETS = 32
REL_MAX_DIST = 128
DN_HEADS = 4
DN_HEAD_DIM = 128
DN_CONV = 4
GLA_HEADS = 4
GLA_DK = 128
GLA_DV = 256
GLA_GATE_RANK = 16
GLA_GATE_NORM = 16.0
CHUNK = 64
PAD = SWA_BLOCK - N_META
LANE = 128
PROJ_DIM = 3200

ADAM_LR = 0.001
ADAM_B1 = 0.9
ADAM_B2 = 0.999
ADAM_EPS = 1e-08
ADAM_WD = 0.01
ADAM_STEP = 10

FF_SHARD = D_FF // N_DEV
FF_SHARD_PAD = 384
FF_PAD = N_DEV * FF_SHARD_PAD
N_FFN = 4
ROW_TILE = 16
EVEN_IN_SHARD, EVEN_IN_SHARD_PAD = 353, 368
ODD_IN_SHARD, ODD_IN_SHARD_PAD = 386, 400
OUT_SHARD = D_MODEL // N_DEV

FLAT_COLS = 128
BIG = ("ffn_w_gate", "ffn_w_up", "ffn_w_down", "even_w_in", "even_w_out", "odd_w_in", "odd_w_out")
SMALL = ("meta_tokens", "norm_w", "even_conv_w", "gla_w_gate_up", "gla_b_gate", "gla_norm_w")
REPL = ("rel_bias_table", "swa_sinks", "dn_a_log", "dn_dt_bias", "dn_norm_w")
WEIGHTS = ("meta_tokens", "norm_w", "ffn_w_gate", "ffn_w_up", "ffn_w_down", "rel_bias_table", "even_w_in",
           "even_conv_w", "swa_sinks", "dn_a_log", "dn_dt_bias", "dn_norm_w", "even_w_out", "odd_w_in",
           "gla_w_gate_up", "gla_b_gate", "gla_norm_w", "odd_w_out")
SHARD_AXIS = {"ffn_w_gate": 3, "ffn_w_up": 3, "ffn_w_down": 2, "even_w_in": 2, "even_w_out": 1, "odd_w_in": 2,
              "odd_w_out": 1, "meta_tokens": 1, "norm_w": 2, "even_conv_w": 2, "gla_w_gate_up": 2,
              "gla_b_gate": 1, "gla_norm_w": 1}


def _rms(x, w):
    r = lax.rsqrt(jnp.mean(x * x, axis=-1, keepdims=True) + NORM_EPS)
    return x * r * w


def _sigmoid(x):
    return 0.5 * (jnp.tanh(0.5 * x) + 1.0)


def _silu(x):
    return x * _sigmoid(x)


def _softplus(x):
    pos = x > 0
    return jnp.where(pos, x, 0.0) + jnp.log(1.0 + jnp.exp(jnp.where(pos, -x, x)))


def _l2n(x):
    return x * lax.rsqrt(jnp.sum(x * x, axis=-1, keepdims=True) + 1e-6)


def _split_bf16(x):
    hi = x.astype(BF16)
    return hi, (x - hi.astype(F32)).astype(BF16)


def _make_mm(terms, batched):
    off = 1 if batched else 0
    bdims = ((0,), (0,)) if batched else ((), ())

    def dg(a, b, ca, cb):
        dot = lambda p, q: lax.dot_general(p, q, (((ca + off,), (cb + off,)), bdims), preferred_element_type=F32)
        a_hi, a_lo = _split_bf16(a)
        b_hi, b_lo = _split_bf16(b)
        if terms == 1:
            return dot(a_hi, b_hi)
        return dot(a_hi, b_hi) + (dot(a_hi, b_lo) + dot(a_lo, b_hi))

    @jax.custom_vjp
    def nn(a, b):
        return dg(a, b, 1, 0)

    @jax.custom_vjp
    def nt(a, b):
        return dg(a, b, 1, 1)

    @jax.custom_vjp
    def tn(a, b):
        return dg(a, b, 0, 0)

    nn.defvjp(lambda a, b: (nn(a, b), (a, b)), lambda r, g: (nt(g, r[1]), tn(r[0], g)))
    nt.defvjp(lambda a, b: (nt(a, b), (a, b)), lambda r, g: (nn(g, r[1]), tn(g, r[0])))
    tn.defvjp(lambda a, b: (tn(a, b), (a, b)), lambda r, g: (nt(r[1], g), nn(r[0], g)))
    return nn, nt, tn


_mm, _mm_nt, _mm_tn = _make_mm(1, False)
_mm3, _, _ = _make_mm(3, False)
_bmm, _bmm_nt, _bmm_tn = _make_mm(1, True)
_bmm3, _bmm3_nt, _bmm3_tn = _make_mm(3, True)


@jax.custom_vjp
def _known_inverse(a, inv):
    return inv


_known_inverse.defvjp(lambda a, inv: (inv, inv),
                      lambda inv, g: (-_bmm3_tn(inv, _bmm3_nt(g, inv)), jnp.zeros_like(inv)))


def _tri_ones_dot(x, lower):
    n = x.shape[0]
    r = lax.broadcasted_iota(jnp.int32, (n, n), 0)
    c = lax.broadcasted_iota(jnp.int32, (n, n), 1)
    t = ((r >= c) if lower else (r <= c)).astype(BF16)
    hi, lo = _split_bf16(x)
    return jnp.dot(t, hi, preferred_element_type=F32) + jnp.dot(t, lo, preferred_element_type=F32)


@jax.custom_vjp
def _cumsum_rows(x):
    return _tri_ones_dot(x, True)


_cumsum_rows.defvjp(lambda x: (_tri_ones_dot(x, True), None), lambda _, g: (_tri_ones_dot(g, False),))


def _row_tile(n_rows, cap):
    best = LANE
    for t in range(LANE, cap + 1, LANE):
        if n_rows % t == 0:
            best = t
    return best


def _real_rows(tile_index, tm):
    row = tile_index * tm + lax.broadcasted_iota(jnp.int32, (tm, 1), 0)
    return (row >= PAD).astype(F32)


def _full(shape):
    return pl.BlockSpec(shape, lambda *_: (0,) * len(shape))


def _resident(shape):
    return pl.BlockSpec(shape, lambda *_: (0,) * len(shape), pipeline_mode=pl.Buffered(1))


def _resident_w(wmat, widx):
    if wmat.ndim == 2:
        return _resident(wmat.shape)
    return pl.BlockSpec((None,) + wmat.shape[1:], lambda *_: (widx, 0, 0), pipeline_mode=pl.Buffered(1))


def rms_mm(h, w, wmat_t, *, swiglu, name, widx=None):
    tp, d = h.shape
    n = wmat_t.shape[-2]
    tm = _row_tile(tp, 384)
    half = n // 2
    wmat = wmat_t

    def body(h_ref, w_ref, wm_ref, hn_ref, *outs):
        hn = _rms(h_ref[...], w_ref[...]).astype(BF16)
        hn_ref[...] = hn
        p = lax.dot_general(hn, wm_ref[...], (((1,), (1,)), ((), ())), preferred_element_type=F32)
        if swiglu:
            g, u = p[:, :half], p[:, half:]
            outs[0][...] = g.astype(BF16)
            outs[1][...] = u.astype(BF16)
            outs[2][...] = (_silu(g) * u).astype(BF16)
        else:
            outs[0][...] = p

    row = lambda width: pl.BlockSpec((tm, width), lambda i: (i, 0))
    if swiglu:
        out_shape = (jax.ShapeDtypeStruct((tp, d), BF16),) + (jax.ShapeDtypeStruct((tp, half), BF16),) * 3
        out_specs = (row(d), row(half), row(half), row(half))
    else:
        out_shape = (jax.ShapeDtypeStruct((tp, d), BF16), jax.ShapeDtypeStruct((tp, n), F32))
        out_specs = (row(d), row(n))
    return pl.pallas_call(
        body, name=name, grid=(tp // tm,),
        in_specs=[row(d), _full((1, d)), _resident_w(wmat, widx)],
        out_specs=out_specs, out_shape=out_shape,
    )(h, w, wmat)


def mm_rms_res(acts, wmat, h, w, *, scale, name, widx=None):
    tp, d = h.shape
    tm = _row_tile(tp, 384)
    widths = [a.shape[1] for a in acts]
    offs = [sum(widths[:i]) for i in range(len(acts))]
    na = len(acts)

    def body(*refs):
        a_refs = refs[:na]
        wm_ref, h_ref, w_ref, f_ref, ho_ref = refs[na:]
        f = None
        for a_ref, off, width in zip(a_refs, offs, widths):
            part = jnp.dot(a_ref[...].astype(BF16), wm_ref[off:off + width, :], preferred_element_type=F32)
            f = part if f is None else f + part
        f_ref[...] = f
        ho_ref[...] = h_ref[...] + scale * _rms(f, w_ref[...])

    row = lambda width: pl.BlockSpec((tm, width), lambda i: (i, 0))
    return pl.pallas_call(
        body, name=name, grid=(tp // tm,),
        in_specs=[row(wd) for wd in widths] + [_resident_w(wmat, widx), row(d), _full((1, d))],
        out_specs=(row(d), row(d)),
        out_shape=(jax.ShapeDtypeStruct((tp, d), F32), jax.ShapeDtypeStruct((tp, d), F32)),
    )(*acts, wmat, h, w)


def mm_rms_res_bwd(dho, f, w, wmat, gu, *, scale, name, widx=None):
    tp, d = f.shape
    k = wmat.shape[-2]
    tm = _row_tile(tp, 384)
    swiglu = gu is not None

    def body(*refs):
        if swiglu:
            dho_ref, f_ref, w_ref, wm_ref, g_ref, u_ref, df_ref, dw_ref, dgu_ref = refs
        else:
            dho_ref, f_ref, w_ref, wm_ref, df_ref, dw_ref, da_ref = refs
        i = pl.program_id(0)
        _, vjp = jax.vjp(lambda ff, ww: scale * _rms(ff, ww), f_ref[...], w_ref[...])
        df, dw = vjp(dho_ref[...])
        dfb = (df * _real_rows(i, tm)).astype(BF16)
        df_ref[...] = dfb

        @pl.when(i == 0)
        def _():
            dw_ref[...] = jnp.zeros_like(dw_ref)

        dw_ref[...] += dw
        da = lax.dot_general(dfb, wm_ref[...], (((1,), (1,)), ((), ())), preferred_element_type=F32)
        if swiglu:
            g = g_ref[...].astype(F32)
            u = u_ref[...].astype(F32)
            s = _sigmoid(g)
            dgu_ref[:, :k] = (da * u * s * (1.0 + g * (1.0 - s))).astype(BF16)
            dgu_ref[:, k:] = (da * g * s).astype(BF16)
        else:
            da_ref[...] = da

    row = lambda width: pl.BlockSpec((tm, width), lambda i: (i, 0))
    in_specs = [row(d), row(d), _full((1, d)), _resident_w(wmat, widx)]
    args = [dho, f, w, wmat]
    out_shape = [jax.ShapeDtypeStruct((tp, d), BF16), jax.ShapeDtypeStruct((1, d), F32)]
    out_specs = [row(d), _full((1, d))]
    if swiglu:
        in_specs += [row(k), row(k)]
        args += list(gu)
        out_shape += [jax.ShapeDtypeStruct((tp, 2 * k), BF16)]
        out_specs += [row(2 * k)]
    else:
        out_shape += [jax.ShapeDtypeStruct((tp, k), F32)]
        out_specs += [row(k)]
    return pl.pallas_call(body, name=name, grid=(tp // tm,), in_specs=in_specs, out_specs=tuple(out_specs),
                          out_shape=tuple(out_shape))(*args)


def rms_mm_bwd(dps, wmat, h, w, dho, *, name, widx=None):
    tp, d = h.shape
    tm = _row_tile(tp, 384)
    widths = [p.shape[1] for p in dps]
    offs = [sum(widths[:i]) for i in range(len(dps))]
    ndp = len(dps)

    def body(*refs):
        dp_refs = refs[:ndp]
        wm_ref, h_ref, w_ref, dho_ref, dh_ref, dw_ref = refs[ndp:]
        i = pl.program_id(0)
        dhn = None
        for dp_ref, off, width in zip(dp_refs, offs, widths):
            part = jnp.dot(dp_ref[...].astype(BF16), wm_ref[off:off + width, :], preferred_element_type=F32)
            dhn = part if dhn is None else dhn + part
        _, vjp = jax.vjp(_rms, h_ref[...], w_ref[...])
        dx, dw = vjp(dhn)
        dh_ref[...] = (dho_ref[...] + dx) * _real_rows(i, tm)

        @pl.when(i == 0)
        def _():
            dw_ref[...] = jnp.zeros_like(dw_ref)

        dw_ref[...] += dw

    row = lambda width: pl.BlockSpec((tm, width), lambda i: (i, 0))
    return pl.pallas_call(
        body, name=name, grid=(tp // tm,),
        in_specs=[row(wd) for wd in widths] + [_resident_w(wmat, widx), row(d), _full((1, d)), row(d)],
        out_specs=(row(d), _full((1, d))),
        out_shape=(jax.ShapeDtypeStruct((tp, d), F32), jax.ShapeDtypeStruct((1, d), F32)),
    )(*dps, wmat, h, w, dho)


def mm_tn(a, b, *, name, out_dtype=F32, after=None):
    t, m = a.shape
    n = b.shape[1]
    bm = _row_tile(m, 1024 if n <= 1024 else 512)
    bn = _row_tile(n, 1536)
    bk = _row_tile(t, 1408)
    nk = t // bk
    ties = [] if after is None else [after]

    def body(a_ref, b_ref, *rest):
        o_ref, acc = rest[-2:]

        @pl.when(pl.program_id(2) == 0)
        def _():
            acc[...] = jnp.zeros_like(acc)

        acc[...] += lax.dot_general(a_ref[...].astype(BF16), b_ref[...].astype(BF16), (((0,), (0,)), ((), ())),
                                    preferred_element_type=F32)

        @pl.when(pl.program_id(2) == nk - 1)
        def _():
            o_ref[...] = acc[...].astype(o_ref.dtype)

    return pl.pallas_call(
        body, name=name, grid=(m // bm, n // bn, nk),
        in_specs=[pl.BlockSpec((bk, bm), lambda i, j, kk: (kk, i)), pl.BlockSpec((bk, bn), lambda i, j, kk: (kk, j))]
        + [pl.BlockSpec(memory_space=pl.ANY)] * len(ties),
        out_specs=pl.BlockSpec((bm, bn), lambda i, j, kk: (i, j)),
        out_shape=jax.ShapeDtypeStruct((m, n), out_dtype), scratch_shapes=[pltpu.VMEM((bm, bn), F32)],
    )(a, b, *ties)


def loss_and_grad(h, target, *, name):
    tp, d = h.shape
    tm = SWA_BLOCK

    def body(h_ref, t_ref, dh_ref, loss_ref):
        i = pl.program_id(0)

        @pl.when(i == 0)
        def _():
            loss_ref[...] = jnp.zeros_like(loss_ref)
            dh_ref[...] = jnp.zeros_like(dh_ref)

        @pl.when(i > 0)
        def _():
            err = h_ref[...] - t_ref[...]
            dh_ref[...] = err * (1.0 / d)
            loss_ref[...] += 0.5 * jnp.sum(jnp.sum(err * err, axis=1, keepdims=True), axis=0, keepdims=True) * (1.0 / d)

    return pl.pallas_call(
        body, name=name, grid=(tp // tm,),
        in_specs=[pl.BlockSpec((tm, d), lambda i: (i, 0)), pl.BlockSpec((tm, d), lambda i: (jnp.maximum(i - 1, 0), 0))],
        out_specs=(pl.BlockSpec((tm, d), lambda i: (i, 0)), _full((1, 1))),
        out_shape=(jax.ShapeDtypeStruct((tp, d), F32), jax.ShapeDtypeStruct((1, 1), F32)),
    )(h, target)


def _t5_bucket_np(rel):
    n = np.maximum(rel, 0)
    max_exact = REL_BUCKETS // 2
    n_f = np.maximum(n, 1).astype(np.float32)
    large = max_exact + (np.log(n_f / np.float32(max_exact)) / np.float32(math.log(REL_MAX_DIST / max_exact))
                         * np.float32(REL_BUCKETS - max_exact)).astype(np.int32)
    large = np.minimum(large, REL_BUCKETS - 1)
    return np.where(n < max_exact, n, large).astype(np.int32)


def _swa_positions_np(n):
    i = np.arange(SWA_BLOCK)[:, None]
    j = np.arange(3 * SWA_BLOCK)[None, :]
    pos_q = n * SWA_BLOCK + i - PAD
    pos_k = np.where(j < SWA_BLOCK, j - PAD, (n - 1) * SWA_BLOCK + (j - SWA_BLOCK) - PAD)
    return pos_q, pos_k


def _swa_buckets():
    out = []
    for n in range(3):
        pos_q, pos_k = _swa_positions_np(n)
        out.append(_t5_bucket_np(pos_q - pos_k))
    return jnp.asarray(np.stack(out))


def swa_bias(table, buckets, *, name):
    nc, nq, nk = buckets.shape

    def body(tab_ref, bkt_ref, out_ref):
        for c in range(nc):
            bkt = bkt_ref[c]
            for h in range(SWA_Q_HEADS):
                acc = jnp.zeros((nq, nk), F32)
                for b in range(REL_BUCKETS):
                    acc = jnp.where(bkt == b, tab_ref[b, h], acc)
                out_ref[c, h] = acc

    return pl.pallas_call(
        body, name=name,
        in_specs=[pl.BlockSpec(memory_space=pltpu.SMEM), pl.BlockSpec(memory_space=pltpu.VMEM)],
        out_specs=pl.BlockSpec(memory_space=pltpu.VMEM),
        out_shape=jax.ShapeDtypeStruct((nc, SWA_Q_HEADS, nq, nk), F32),
    )(table, buckets)


def swa_bias_bwd(dbias, buckets, *, name):
    nc = buckets.shape[0]

    def body(db_ref, bkt_ref, out_ref):
        lane = lax.broadcasted_iota(jnp.int32, (1, LANE), 1)
        for b in range(REL_BUCKETS):
            row = jnp.zeros((1, LANE), F32)
            for c in range(nc):
                hit = bkt_ref[c] == b
                for h in range(SWA_Q_HEADS):
                    part = jnp.where(hit, db_ref[c, h], 0.0)
                    tot = jnp.sum(jnp.sum(part, axis=1, keepdims=True), axis=0, keepdims=True)
                    row = row + jnp.where(lane == h, tot, 0.0)
            out_ref[b:b + 1, :] = row

    return pl.pallas_call(
        body, name=name,
        in_specs=[pl.BlockSpec(memory_space=pltpu.VMEM), pl.BlockSpec(memory_space=pltpu.VMEM)],
        out_specs=pl.BlockSpec(memory_space=pltpu.VMEM),
        out_shape=jax.ShapeDtypeStruct((REL_BUCKETS, LANE), F32),
    )(dbias, buckets)


def _swa_block(q, kvm, kvp, kvc, bias, sinks, n):
    blk = SWA_BLOCK
    i = lax.broadcasted_iota(jnp.int32, (blk, 3 * blk), 0)
    j = lax.broadcasted_iota(jnp.int32, (blk, 3 * blk), 1)
    pos_q = n * blk + i - PAD
    is_meta = j < blk
    pos_k = jnp.where(is_meta, j - PAD, (n - 1) * blk + (j - blk) - PAD)
    rel = pos_q - pos_k
    valid = ((is_meta & (pos_k >= 0) & (pos_k < N_META) & (rel >= 0))
             | (jnp.logical_not(is_meta) & (pos_k >= N_META) & (rel >= 0) & (rel < SWA_WINDOW)))
    kv = jnp.concatenate([kvm, kvp, kvc], axis=0)
    lane = lax.broadcasted_iota(jnp.int32, (1, LANE), 1)
    halves = ((lane < SWA_HEAD_DIM).astype(F32), (lane >= SWA_HEAD_DIM).astype(F32))
    scale = SWA_HEAD_DIM ** -0.5
    outs = []
    for pair in range(SWA_Q_HEADS // 2):
        qp = q[:, pair * LANE:(pair + 1) * LANE]
        grp = pair // 2
        kg = kv[:, grp * LANE:(grp + 1) * LANE]
        vg = kv[:, (2 + grp) * LANE:(3 + grp) * LANE]
        op = None
        for hh in range(2):
            h = 2 * pair + hh
            s = _mm_nt(qp * halves[hh], kg) * scale + bias[h]
            s = jnp.where(valid, s, NEG_INF)
            sink = jnp.sum(jnp.where(lane == h, sinks, 0.0), axis=1, keepdims=True)
            m = lax.stop_gradient(jnp.maximum(jnp.max(s, axis=1, keepdims=True), sink))
            e = jnp.exp(s - m)
            den = jnp.sum(e, axis=1, keepdims=True) + jnp.exp(sink - m)
            part = _mm(e / den, vg) * halves[hh]
            op = part if op is None else op + part
        outs.append(op)
    return jnp.concatenate(outs, axis=1)


def _swa_in_specs(nb, rev):
    blk = SWA_BLOCK
    step = (lambda i: nb - 1 - i) if rev else (lambda i: i)
    return [
        pl.BlockSpec((blk, 4 * LANE), lambda i: (step(i), 0)),
        pl.BlockSpec((blk, 4 * LANE), lambda i: (0, 1)),
        pl.BlockSpec((blk, 4 * LANE), lambda i: (jnp.maximum(step(i) - 1, 0), 1)),
        pl.BlockSpec((blk, 4 * LANE), lambda i: (step(i), 1)),
        pl.BlockSpec((1, SWA_Q_HEADS, blk, 3 * blk), lambda i: (jnp.minimum(step(i), 2), 0, 0, 0)),
        _full((1, LANE)),
    ]


def swa_fwd(proj, bias, sinks, *, name):
    tp = proj.shape[0]
    nb = tp // SWA_BLOCK

    def body(q_ref, kvm_ref, kvp_ref, kvc_ref, bias_ref, sinks_ref, o_ref):
        n = pl.program_id(0)
        o_ref[...] = _swa_block(q_ref[...], kvm_ref[...], kvp_ref[...], kvc_ref[...], bias_ref[0], sinks_ref[...], n)

    return pl.pallas_call(
        body, name=name, grid=(nb,),
        in_specs=_swa_in_specs(nb, False),
        out_specs=pl.BlockSpec((SWA_BLOCK, 4 * LANE), lambda i: (i, 0)),
        out_shape=jax.ShapeDtypeStruct((tp, 4 * LANE), F32),
    )(proj, proj, proj, proj, bias, sinks)


def swa_bwd(proj, bias, sinks, do, *, name):
    tp = proj.shape[0]
    nb = tp // SWA_BLOCK
    blk = SWA_BLOCK

    def body(q_ref, kvm_ref, kvp_ref, kvc_ref, bias_ref, sinks_ref, do_ref, dq_ref, dkv_ref, dbias_ref, dsinks_ref,
             carry, meta_acc):
        i = pl.program_id(0)
        n = nb - 1 - i

        @pl.when(i == 0)
        def _():
            carry[...] = jnp.zeros_like(carry)
            meta_acc[...] = jnp.zeros_like(meta_acc)
            dsinks_ref[...] = jnp.zeros_like(dsinks_ref)

        fn = lambda q, kvm, kvp, kvc, b, s: _swa_block(q, kvm, kvp, kvc, b, s, n)
        _, vjp = jax.vjp(fn, q_ref[...], kvm_ref[...], kvp_ref[...], kvc_ref[...], bias_ref[0], sinks_ref[...])
        dq, dkvm, dkvp, dkvc, dbias, dsinks = vjp(do_ref[...])
        dq_ref[...] = dq
        meta_acc[...] += dkvm
        dkv_ref[...] = dkvc + carry[...] + jnp.where(n == 0, meta_acc[...], 0.0)
        carry[...] = dkvp
        first_visit = (n == nb - 1) | (n < 2)

        @pl.when(first_visit)
        def _():
            dbias_ref[0] = dbias

        @pl.when(jnp.logical_not(first_visit))
        def _():
            dbias_ref[0] += dbias

        dsinks_ref[...] += dsinks

    rev = lambda i: nb - 1 - i
    return pl.pallas_call(
        body, name=name, grid=(nb,),
        in_specs=_swa_in_specs(nb, True) + [pl.BlockSpec((blk, 4 * LANE), lambda i: (rev(i), 0))],
        out_specs=(pl.BlockSpec((blk, 4 * LANE), lambda i: (rev(i), 0)),
                   pl.BlockSpec((blk, 4 * LANE), lambda i: (rev(i), 0)),
                   pl.BlockSpec((1, SWA_Q_HEADS, blk, 3 * blk), lambda i: (jnp.minimum(rev(i), 2), 0, 0, 0)),
                   _full((1, LANE))),
        out_shape=(jax.ShapeDtypeStruct((tp, 4 * LANE), F32), jax.ShapeDtypeStruct((tp, 4 * LANE), F32),
                   jax.ShapeDtypeStruct((3, SWA_Q_HEADS, blk, 3 * blk), F32), jax.ShapeDtypeStruct((1, LANE), F32)),
        scratch_shapes=[pltpu.VMEM((blk, 4 * LANE), F32), pltpu.VMEM((blk, 4 * LANE), F32)],
    )(proj, proj, proj, proj, bias, sinks, do)


CONV_COL0 = 2
HALO = 8


def conv_fwd(proj, conv_w, *, name):
    tp = proj.shape[0]
    tm = _row_tile(tp, 384)
    cw = 4 * LANE
    ncol = conv_w.shape[1] // cw

    def body(x_ref, halo_ref, w_ref, y_ref, buf):
        i = pl.program_id(1)
        buf[0:HALO, :] = jnp.where(i > 0, halo_ref[...], 0.0)
        buf[HALO:, :] = x_ref[...]
        acc = None
        for j in range(DN_CONV):
            term = w_ref[j:j + 1, :] * buf[pl.ds(HALO - (DN_CONV - 1) + j, tm), :]
            acc = term if acc is None else acc + term
        y_ref[...] = acc

    return pl.pallas_call(
        body, name=name, grid=(ncol, tp // tm),
        in_specs=[pl.BlockSpec((tm, cw), lambda c, i: (i, CONV_COL0 + c)),
                  pl.BlockSpec((HALO, cw), lambda c, i: (jnp.maximum(i * (tm // HALO) - 1, 0), CONV_COL0 + c)),
                  pl.BlockSpec((DN_CONV, cw), lambda c, i: (0, c))],
        out_specs=pl.BlockSpec((tm, cw), lambda c, i: (i, c)),
        out_shape=jax.ShapeDtypeStruct((tp, ncol * cw), F32),
        scratch_shapes=[pltpu.VMEM((tm + HALO, cw), F32)],
    )(proj, proj, conv_w)


def conv_bwd(proj, conv_w, dy, *, name):
    tp = proj.shape[0]
    tm = _row_tile(tp, 384)
    cw = 4 * LANE
    ncol = conv_w.shape[1] // cw
    nt = tp // tm

    def body(x_ref, xhalo_ref, w_ref, dy_ref, dyhalo_ref, dx_ref, dw_ref, xbuf, dbuf):
        i = pl.program_id(1)
        xbuf[0:HALO, :] = jnp.where(i > 0, xhalo_ref[...], 0.0)
        xbuf[HALO:, :] = x_ref[...]
        dbuf[0:tm, :] = dy_ref[...]
        dbuf[tm:, :] = jnp.where(i < nt - 1, dyhalo_ref[...], 0.0)
        dy_t = dy_ref[...]
        acc = None
        rows = []
        for j in range(DN_CONV):
            term = w_ref[j:j + 1, :] * dbuf[pl.ds(DN_CONV - 1 - j, tm), :]
            acc = term if acc is None else acc + term
            rows.append(jnp.sum(dy_t * xbuf[pl.ds(HALO - (DN_CONV - 1) + j, tm), :], axis=0, keepdims=True))
        dx_ref[...] = acc

        @pl.when(i == 0)
        def _():
            dw_ref[...] = jnp.zeros_like(dw_ref)

        for j in range(DN_CONV):
            dw_ref[j:j + 1, :] += rows[j]

    return pl.pallas_call(
        body, name=name, grid=(ncol, nt),
        in_specs=[pl.BlockSpec((tm, cw), lambda c, i: (i, CONV_COL0 + c)),
                  pl.BlockSpec((HALO, cw), lambda c, i: (jnp.maximum(i * (tm // HALO) - 1, 0), CONV_COL0 + c)),
                  pl.BlockSpec((DN_CONV, cw), lambda c, i: (0, c)),
                  pl.BlockSpec((tm, cw), lambda c, i: (i, c)),
                  pl.BlockSpec((HALO, cw), lambda c, i: (jnp.minimum((i + 1) * (tm // HALO), tp // HALO - 1), c))],
        out_specs=(pl.BlockSpec((tm, cw), lambda c, i: (i, c)), pl.BlockSpec((DN_CONV, cw), lambda c, i: (0, c))),
        out_shape=(jax.ShapeDtypeStruct((tp, ncol * cw), F32), jax.ShapeDtypeStruct((DN_CONV, ncol * cw), F32)),
        scratch_shapes=[pltpu.VMEM((tm + HALO, cw), F32), pltpu.VMEM((tm + HALO, cw), F32)],
    )(proj, proj, conv_w, dy, dy)


def _stack(parts):
    return jnp.concatenate([p[None] for p in parts], axis=0)


def _chunk_masks():
    r = lax.broadcasted_iota(jnp.int32, (CHUNK, CHUNK), 0)
    c = lax.broadcasted_iota(jnp.int32, (CHUNK, CHUNK), 1)
    return (r >= c).astype(F32), (r > c).astype(F32), (r == c).astype(F32)


def _dn_chunk(y, z, small, s, a_log, dt_bias, norm_w, rows, known_inv=None):
    tri_incl, tri_strict, eye = _chunk_masks()
    lane = lax.broadcasted_iota(jnp.int32, (1, LANE), 1)
    dk = DN_HEAD_DIM
    nh = DN_HEADS
    heads = lambda t, first: _stack([t[:, (first + h) * dk:(first + h + 1) * dk] for h in range(nh)])
    pick = lambda t, l: jnp.sum(jnp.where(lane == l, t, 0.0), axis=1, keepdims=True)
    q = _l2n(_silu(heads(y, 0))) * dk ** -0.5
    k = _l2n(_silu(heads(y, nh)))
    v = _silu(heads(y, 2 * nh))
    g_all = jnp.where(lane < nh, -jnp.exp(a_log) * _softplus(small + dt_bias), 0.0) * rows
    beta_all = _sigmoid(small)
    gc_all = _cumsum_rows(g_all)
    g_sum = jnp.sum(g_all, axis=0, keepdims=True)
    gc = _stack([pick(gc_all, h) for h in range(nh)])
    beta = _stack([pick(beta_all, nh + h) for h in range(nh)])
    g_last = _stack([pick(g_sum, h) for h in range(nh)])
    gc_row = jnp.sum(eye * gc, axis=1, keepdims=True)
    gamma = jnp.exp((gc - gc_row) * tri_incl) * tri_incl
    k_beta = k * beta
    v_beta = v * beta
    a = _bmm_nt(k_beta, k) * gamma * tri_strict
    if known_inv is None:
        inv = eye - a
        power = a
        for _ in range(5):
            power = _bmm3(power, power)
            inv = inv + _bmm3(inv, power)
    else:
        inv = _known_inverse(a, known_inv)
    e_gc = jnp.exp(gc)
    uw = _bmm3(inv, jnp.concatenate([v_beta, k_beta * e_gc], axis=2))
    u, w = uw[:, :, :dk], uw[:, :, dk:]
    attn = _bmm_nt(q, k) * gamma
    q_dec = q * e_gc
    k_dec = k * jnp.exp(g_last - gc)
    v_new = u - _bmm(w, s)
    o = _bmm(q_dec, s) + _bmm(attn, v_new)
    s_new = s * jnp.exp(g_last) + _bmm_tn(k_dec, v_new)
    out = _rms(o, norm_w) * _silu(heads(z, 0))
    return jnp.concatenate([out[h] for h in range(nh)], axis=1), s_new, inv


Z_COL = 5
SMALL_COL = 24


def _chunk_rows(n):
    row = n * CHUNK + lax.broadcasted_iota(jnp.int32, (CHUNK, 1), 0)
    return (row >= PAD).astype(F32)


def dn_fwd(y, proj, a_log, dt_bias, norm_w, *, name):
    tp = y.shape[0]
    nc = tp // CHUNK
    dk = DN_HEAD_DIM

    def body(y_ref, z_ref, small_ref, al_ref, dt_ref, nw_ref, o_ref, ssave_ref, isave_ref, state):
        n = pl.program_id(0)

        @pl.when(n == 0)
        def _():
            state[...] = jnp.zeros_like(state)

        ssave_ref[0] = state[...]
        out, s_new, inv = _dn_chunk(y_ref[...], z_ref[...], small_ref[...], state[...], al_ref[...], dt_ref[...],
                                    nw_ref[...], _chunk_rows(n))
        o_ref[...] = out
        isave_ref[0] = inv
        state[...] = s_new

    return pl.pallas_call(
        body, name=name, grid=(nc,),
        in_specs=[pl.BlockSpec((CHUNK, y.shape[1]), lambda n: (n, 0)),
                  pl.BlockSpec((CHUNK, 4 * LANE), lambda n: (n, Z_COL)),
                  pl.BlockSpec((CHUNK, LANE), lambda n: (n, SMALL_COL)),
                  _full((1, LANE)), _full((1, LANE)), _full((1, LANE))],
        out_specs=(pl.BlockSpec((CHUNK, 4 * LANE), lambda n: (n, 0)),
                   pl.BlockSpec((1, DN_HEADS, dk, dk), lambda n: (n, 0, 0, 0)),
                   pl.BlockSpec((1, DN_HEADS, CHUNK, CHUNK), lambda n: (n, 0, 0, 0))),
        out_shape=(jax.ShapeDtypeStruct((tp, 4 * LANE), F32), jax.ShapeDtypeStruct((nc, DN_HEADS, dk, dk), F32),
                   jax.ShapeDtypeStruct((nc, DN_HEADS, CHUNK, CHUNK), F32)),
        scratch_shapes=[pltpu.VMEM((DN_HEADS, dk, dk), F32)],
    )(y, proj, proj, a_log, dt_bias, norm_w)


def dn_bwd(y, proj, a_log, dt_bias, norm_w, ssave, isave, do, *, name):
    tp = y.shape[0]
    nc = tp // CHUNK
    dk = DN_HEAD_DIM
    rev = lambda i: nc - 1 - i

    def body(y_ref, z_ref, small_ref, al_ref, dt_ref, nw_ref, ss_ref, is_ref, do_ref,
             dy_ref, dz_ref, dsmall_ref, dal_ref, ddt_ref, dnw_ref, dstate):
        i = pl.program_id(0)
        n = nc - 1 - i

        @pl.when(i == 0)
        def _():
            dstate[...] = jnp.zeros_like(dstate)
            dal_ref[...] = jnp.zeros_like(dal_ref)
            ddt_ref[...] = jnp.zeros_like(ddt_ref)
            dnw_ref[...] = jnp.zeros_like(dnw_ref)

        rows = _chunk_rows(n)
        known_inv = is_ref[0]
        fn = lambda *a: _dn_chunk(*a, rows, known_inv)[:2]
        _, vjp = jax.vjp(fn, y_ref[...], z_ref[...], small_ref[...], ss_ref[0], al_ref[...], dt_ref[...], nw_ref[...])
        dy, dz, dsmall, ds, dal, ddt, dnw = vjp((do_ref[...], dstate[...]))
        dy_ref[...] = dy
        dz_ref[...] = dz
        dsmall_ref[...] = dsmall
        dstate[...] = ds
        dal_ref[...] += dal
        ddt_ref[...] += ddt
        dnw_ref[...] += dnw

    return pl.pallas_call(
        body, name=name, grid=(nc,),
        in_specs=[pl.BlockSpec((CHUNK, y.shape[1]), lambda i: (rev(i), 0)),
                  pl.BlockSpec((CHUNK, 4 * LANE), lambda i: (rev(i), Z_COL)),
                  pl.BlockSpec((CHUNK, LANE), lambda i: (rev(i), SMALL_COL)),
                  _full((1, LANE)), _full((1, LANE)), _full((1, LANE)),
                  pl.BlockSpec((1, DN_HEADS, dk, dk), lambda i: (rev(i), 0, 0, 0)),
                  pl.BlockSpec((1, DN_HEADS, CHUNK, CHUNK), lambda i: (rev(i), 0, 0, 0)),
                  pl.BlockSpec((CHUNK, 4 * LANE), lambda i: (rev(i), 1))],
        out_specs=(pl.BlockSpec((CHUNK, y.shape[1]), lambda i: (rev(i), 0)),
                   pl.BlockSpec((CHUNK, 4 * LANE), lambda i: (rev(i), 0)),
                   pl.BlockSpec((CHUNK, LANE), lambda i: (rev(i), 0)),
                   _full((1, LANE)), _full((1, LANE)), _full((1, LANE))),
        out_shape=(jax.ShapeDtypeStruct((tp, y.shape[1]), F32), jax.ShapeDtypeStruct((tp, 4 * LANE), F32),
                   jax.ShapeDtypeStruct((tp, LANE), F32), jax.ShapeDtypeStruct((1, LANE), F32),
                   jax.ShapeDtypeStruct((1, LANE), F32), jax.ShapeDtypeStruct((1, LANE), F32)),
        scratch_shapes=[pltpu.VMEM((DN_HEADS, dk, dk), F32)],
    )(y, proj, proj, a_log, dt_bias, norm_w, ssave, isave, do)


def _gla_chunk(q, k, v, gate, low, s, w_gate_up, b_gate, norm_w, rows):
    tri_incl, _, _ = _chunk_masks()
    dk, dv, nh = GLA_DK, GLA_DV, GLA_HEADS
    heads = lambda t, width: _stack([t[:, h * width:(h + 1) * width] for h in range(nh)])
    logit = _mm3(low, w_gate_up) + b_gate
    glog_all = -_softplus(-logit) * (1.0 / GLA_GATE_NORM) * rows
    glog = heads(glog_all, dk)
    bcum = heads(_cumsum_rows(glog_all), dk)
    qh = heads(q, dk) * dk ** -0.5
    kh = heads(k, dk)
    vh = heads(v, dv)
    q_dec = qh * jnp.exp(bcum)
    attn = _bmm_nt(q_dec, kh * jnp.exp(-bcum)) * tri_incl
    b_last = jnp.sum(glog, axis=1, keepdims=True)
    k_dec = kh * jnp.exp(b_last - bcum)
    r = lax.broadcasted_iota(jnp.int32, (dk, dk), 0)
    c = lax.broadcasted_iota(jnp.int32, (dk, dk), 1)
    b_last_col = jnp.sum((r == c).astype(F32) * b_last, axis=2, keepdims=True)
    o = _bmm(attn, vh) + _bmm(q_dec, s)
    s_new = s * jnp.exp(b_last_col) + _bmm_tn(k_dec, vh)
    out = _rms(o, norm_w) * _silu(heads(gate, dv))
    return jnp.concatenate([out[h] for h in range(nh)], axis=1), s_new


LOW_COL = 24


def _gla_in_specs(step):
    return [pl.BlockSpec((CHUNK, 4 * LANE), lambda i: (step(i), 0)),
            pl.BlockSpec((CHUNK, 4 * LANE), lambda i: (step(i), 1)),
            pl.BlockSpec((CHUNK, 8 * LANE), lambda i: (step(i), 1)),
            pl.BlockSpec((CHUNK, 8 * LANE), lambda i: (step(i), 2)),
            pl.BlockSpec((CHUNK, LANE), lambda i: (step(i), LOW_COL)),
            _full((LANE, 4 * LANE)), _full((1, 4 * LANE)), _full((1, GLA_DV))]


def gla_fwd(proj, w_gate_up, b_gate, norm_w, *, name):
    tp = proj.shape[0]
    nc = tp // CHUNK

    def body(q_ref, k_ref, v_ref, g_ref, low_ref, wgu_ref, bg_ref, nw_ref, o_ref, ssave_ref, state):
        n = pl.program_id(0)

        @pl.when(n == 0)
        def _():
            state[...] = jnp.zeros_like(state)

        ssave_ref[0] = state[...]
        out, s_new = _gla_chunk(q_ref[...], k_ref[...], v_ref[...], g_ref[...], low_ref[...], state[...], wgu_ref[...],
                                bg_ref[...], nw_ref[...], _chunk_rows(n))
        o_ref[...] = out
        state[...] = s_new

    return pl.pallas_call(
        body, name=name, grid=(nc,),
        in_specs=_gla_in_specs(lambda i: i),
        out_specs=(pl.BlockSpec((CHUNK, 8 * LANE), lambda n: (n, 0)),
                   pl.BlockSpec((1, GLA_HEADS, GLA_DK, GLA_DV), lambda n: (n, 0, 0, 0))),
        out_shape=(jax.ShapeDtypeStruct((tp, 8 * LANE), F32),
                   jax.ShapeDtypeStruct((nc, GLA_HEADS, GLA_DK, GLA_DV), F32)),
        scratch_shapes=[pltpu.VMEM((GLA_HEADS, GLA_DK, GLA_DV), F32)],
    )(proj, proj, proj, proj, proj, w_gate_up, b_gate, norm_w)


def gla_bwd(proj, w_gate_up, b_gate, norm_w, ssave, do, *, name):
    tp = proj.shape[0]
    nc = tp // CHUNK
    rev = lambda i: nc - 1 - i

    def body(q_ref, k_ref, v_ref, g_ref, low_ref, wgu_ref, bg_ref, nw_ref, ss_ref, do_ref,
             dq_ref, dk_ref, dv_ref, dg_ref, dlow_ref, dwgu_ref, dbg_ref, dnw_ref, dstate):
        i = pl.program_id(0)
        n = nc - 1 - i

        @pl.when(i == 0)
        def _():
            dstate[...] = jnp.zeros_like(dstate)
            dwgu_ref[...] = jnp.zeros_like(dwgu_ref)
            dbg_ref[...] = jnp.zeros_like(dbg_ref)
            dnw_ref[...] = jnp.zeros_like(dnw_ref)

        rows = _chunk_rows(n)
        fn = lambda *a: _gla_chunk(*a, rows)
        _, vjp = jax.vjp(fn, q_ref[...], k_ref[...], v_ref[...], g_ref[...], low_ref[...], ss_ref[0], wgu_ref[...],
                         bg_ref[...], nw_ref[...])
        dq, dk, dv, dg, dlow, ds, dwgu, dbg, dnw = vjp((do_ref[...], dstate[...]))
        dq_ref[...] = dq
        dk_ref[...] = dk
        dv_ref[...] = dv
        dg_ref[...] = dg
        dlow_ref[...] = dlow
        dstate[...] = ds
        dwgu_ref[...] += dwgu
        dbg_ref[...] += dbg
        dnw_ref[...] += dnw

    chunk = lambda width: pl.BlockSpec((CHUNK, width), lambda i: (rev(i), 0))
    return pl.pallas_call(
        body, name=name, grid=(nc,),
        in_specs=_gla_in_specs(rev) + [pl.BlockSpec((1, GLA_HEADS, GLA_DK, GLA_DV), lambda i: (rev(i), 0, 0, 0)),
                                       chunk(8 * LANE)],
        out_specs=(chunk(4 * LANE), chunk(4 * LANE), chunk(8 * LANE), chunk(8 * LANE), chunk(LANE),
                   _full((LANE, 4 * LANE)), _full((1, 4 * LANE)), _full((1, GLA_DV))),
        out_shape=(jax.ShapeDtypeStruct((tp, 4 * LANE), F32), jax.ShapeDtypeStruct((tp, 4 * LANE), F32),
                   jax.ShapeDtypeStruct((tp, 8 * LANE), F32), jax.ShapeDtypeStruct((tp, 8 * LANE), F32),
                   jax.ShapeDtypeStruct((tp, LANE), F32), jax.ShapeDtypeStruct((LANE, 4 * LANE), F32),
                   jax.ShapeDtypeStruct((1, 4 * LANE), F32), jax.ShapeDtypeStruct((1, GLA_DV), F32)),
        scratch_shapes=[pltpu.VMEM((GLA_HEADS, GLA_DK, GLA_DV), F32)],
    )(proj, proj, proj, proj, proj, w_gate_up, b_gate, norm_w, ssave, do)


def _even_proj_weight(w_t):
    hd = SWA_HEAD_DIM
    k0, k1 = w_t[512:512 + hd], w_t[512 + hd:640]
    v0, v1 = w_t[640:640 + hd], w_t[640 + hd:768]
    zeros = jnp.zeros((LANE - 2 * DN_HEADS, w_t.shape[1]), w_t.dtype)
    return jnp.concatenate([w_t[:512], k0, k0, k1, k1, v0, v0, v1, v1, w_t[768:2816], w_t[2820:2824], w_t[2816:2820],
                            zeros], axis=0)


def _even_proj_weight_grad(dw):
    hd = SWA_HEAD_DIM
    c = lambda i: dw[512 + i * hd:512 + (i + 1) * hd]
    return jnp.concatenate([dw[:512], c(0) + c(1), c(2) + c(3), c(4) + c(5), c(6) + c(7), dw[1024:3072],
                            dw[3076:3080], dw[3072:3076]], axis=0)


def _ffn_fwd(h, nw_in, nw_out, wts, idx, get_w):
    wts.update(get_w(f"ffn{idx}", h))
    w_gu = wts[f"w_gu{idx}"]
    hn, g, u, a = rms_mm(h, nw_in, w_gu[0], swiglu=True, name=f"ffn_up_{idx}", widx=w_gu[1])
    wts.update(get_w(f"down{idx}", a))
    w_down = wts[f"w_down{idx}"]
    f, h_out = mm_rms_res([a], w_down[0], h, nw_out, scale=0.5, name=f"ffn_down_{idx}", widx=w_down[1])
    return h_out, (h, hn, g, u, a, f)


def _ffn_bwd(dho, saved, nw_in, nw_out, w_gu, w_down, idx, on_grads):
    h, hn, g, u, a, f = saved
    df, dnw_out, dgu = mm_rms_res_bwd(dho, f, nw_out, w_down[0], (g, u), scale=0.5, name=f"ffn_down_bwd_{idx}",
                                      widx=w_down[1])
    g_down = mm_tn(a, df, name=f"ffn_dwd_{idx}", out_dtype=BF16)
    sent = on_grads("down", g_down)
    g_gu = mm_tn(dgu, hn, name=f"ffn_dwgu_{idx}", out_dtype=BF16, after=sent)
    sent = on_grads("gu", g_gu)
    dh, dnw_in = rms_mm_bwd([dgu], w_gu[0], h, nw_in + sent, dho, name=f"ffn_up_bwd_{idx}", widx=w_gu[1])
    return dh, dnw_in, dnw_out


def local_step(x, target, wts, get_w=None, put_g=None):
    seq, d = x.shape
    wts = dict(wts)
    get_w = get_w or (lambda stage, after: {})
    put_g = put_g or (lambda stage, grads: jnp.zeros((1, 1), F32))
    row = lambda v: v.reshape(1, -1)
    lane_row = lambda v: jnp.pad(v.reshape(1, -1), ((0, 0), (0, LANE - v.size)))
    nw = wts["norm_w"]
    h = jnp.concatenate([jnp.zeros((PAD, d), F32), wts["meta_tokens"], x], axis=0)
    buckets = _swa_buckets()
    bias = swa_bias(wts["rel_bias_table"], buckets, name="swa_bias")
    sinks = lane_row(wts["swa_sinks"])
    a_log, dt_bias = lane_row(wts["dn_a_log"]), lane_row(wts["dn_dt_bias"])
    dn_norm_w = row(wts["dn_norm_w"])
    conv_w = wts["even_conv_w"][0]
    w_gate_up = jnp.pad(wts["gla_w_gate_up"][0], ((0, LANE - GLA_GATE_RANK), (0, 0)))
    b_gate, gla_norm_w = row(wts["gla_b_gate"]), row(wts["gla_norm_w"])

    saved = []
    w_in, w_out = [None, None], [None, None]
    for l in range(2):
        h, s_a = _ffn_fwd(h, row(nw[l, 0]), row(nw[l, 1]), wts, 2 * l, get_w)
        if l == 0:
            wts.update(get_w("even", h))
            w_in[0], w_out[0] = _even_proj_weight(wts["even_w_in"]), wts["even_w_out"]
        else:
            wts.update(get_w("odd", h))
            w_in[1] = jnp.pad(wts["odd_w_in"], ((0, PROJ_DIM - wts["odd_w_in"].shape[0]), (0, 0)))
            w_out[1] = wts["odd_w_out"]
        h_mix = h
        hn, proj = rms_mm(h, row(nw[l, 2]), w_in[l], swiglu=False, name=f"mix_in_{l}")
        if l == 0:
            o_a = swa_fwd(proj, bias, sinks, name="swa_fwd")
            y = conv_fwd(proj, conv_w, name="conv_fwd")
            o_b, ssave, isave = dn_fwd(y, proj, a_log, dt_bias, dn_norm_w, name="dn_fwd")
            acts, extra = [o_a, o_b], (y, ssave, isave)
        else:
            o, ssave = gla_fwd(proj, w_gate_up, b_gate, gla_norm_w, name="gla_fwd")
            acts, extra = [o], (ssave,)
        mix, h = mm_rms_res(acts, w_out[l], h, row(nw[l, 3]), scale=1.0, name=f"mix_out_{l}")
        s_m = (h_mix, hn, proj, acts, extra, mix)
        h, s_b = _ffn_fwd(h, row(nw[l, 4]), row(nw[l, 5]), wts, 2 * l + 1, get_w)
        saved.append((s_a, s_m, s_b))

    dh, loss = loss_and_grad(h, target, name="loss")

    grads = {}
    dnw = [[None] * 6 for _ in range(2)]
    def on_grads(i):
        def put(which, g):
            grads[f"g_{which}{i}"] = g
            return put_g(f"{which}{i}", grads)
        return put

    for l in (1, 0):
        s_a, s_m, s_b = saved[l]
        i = 2 * l + 1
        dh, dnw[l][4], dnw[l][5] = _ffn_bwd(dh, s_b, row(nw[l, 4]), row(nw[l, 5]), wts[f"w_gu{i}"], wts[f"w_down{i}"],
                                            i, on_grads(i))
        h_mix, hn, proj, acts, extra, mix = s_m
        dmix, dnw[l][3], do = mm_rms_res_bwd(dh, mix, row(nw[l, 3]), w_out[l], None, scale=1.0, name=f"mix_out_bwd_{l}")
        dw_out = jnp.concatenate([mm_tn(a, dmix, name=f"mix_dwo_{l}_{i}") for i, a in enumerate(acts)], axis=0)
        sent = jnp.zeros((1, 1), F32)
        if l == 0:
            y, ssave, isave = extra
            dq, dkv, dbias, dsinks = swa_bwd(proj, bias, sinks, do, name="swa_bwd")
            dy, dz, dsmall, da_log, ddt_bias, ddn_norm_w = dn_bwd(y, proj, a_log, dt_bias, dn_norm_w, ssave, isave, do,
                                                                    name="dn_bwd")
            dxc, dconv_w = conv_bwd(proj, conv_w, dy, name="conv_bwd")
            dps = [dq, dkv, dxc, dz, dsmall]
            grads["rel_bias_table"] = swa_bias_bwd(dbias, buckets, name="swa_bias_bwd")[:, :SWA_Q_HEADS]
            grads["swa_sinks"] = dsinks[:, :SWA_Q_HEADS]
            grads["dn_a_log"] = da_log[:, :DN_HEADS]
            grads["dn_dt_bias"] = ddt_bias[:, :DN_HEADS]
            grads["dn_norm_w"] = ddn_norm_w
            grads["even_conv_w"] = dconv_w[None]
            grads["even_w_out"] = dw_out
        else:
            (ssave,) = extra
            dq, dk, dv, dgate, dlow, dwgu, dbg, dgnw = gla_bwd(proj, w_gate_up, b_gate, gla_norm_w, ssave, do,
                                                               name="gla_bwd")
            dps = [dq, dk, dv, dgate, dlow]
            grads["gla_w_gate_up"] = dwgu[None, :GLA_GATE_RANK]
            grads["gla_b_gate"] = dbg
            grads["gla_norm_w"] = dgnw
            grads["odd_w_out"] = dw_out
        dw_in = jnp.concatenate([mm_tn(dp, hn, name=f"mix_dwi_{l}_{i}") for i, dp in enumerate(dps)], axis=0)
        if l == 0:
            grads["even_w_in"] = _even_proj_weight_grad(dw_in)
            sent = put_g("even", grads)
        else:
            grads["odd_w_in"] = dw_in[:wts["odd_w_in"].shape[0]]
        dh, dnw[l][2] = rms_mm_bwd(dps, w_in[l], h_mix, row(nw[l, 2]) + sent, dh, name=f"mix_in_bwd_{l}")
        i = 2 * l
        dh, dnw[l][0], dnw[l][1] = _ffn_bwd(dh, s_a, row(nw[l, 0]), row(nw[l, 1]), wts[f"w_gu{i}"], wts[f"w_down{i}"],
                                            i, on_grads(i))

    grads["norm_w"] = jnp.stack([jnp.concatenate(r, axis=0) for r in dnw])
    grads["meta_tokens"] = dh[PAD:PAD + N_META]
    return loss[0, 0], dh[PAD + N_META:], grads


def _peer(k):
    x, y, c = (lax.axis_index(a) for a in AXES)
    flip = lambda v, bit: 1 - v if bit else v
    return (flip(x, k & 4), flip(y, k & 2), flip(c, k & 1))


def _my_index():
    x, y, c = (lax.axis_index(a) for a in AXES)
    return 4 * x + 2 * y + c


_HBM = pl.BlockSpec(memory_space=pltpu.HBM)
_SEM = pl.BlockSpec(memory_space=pltpu.SEMAPHORE)
_EFFECT = pltpu.SideEffectType.DATAFLOW_SIDE_EFFECTING


def _remote_copies(items, src_refs, land_refs, send_sems, recv_sems):
    me = _my_index()
    copies = []
    for k in range(1, N_DEV):
        px, py, pc = _peer(k)
        pj = 4 * px + 2 * py + pc
        for a, (sn, send, ln, land, _) in enumerate(items):
            sem = (k - 1) * len(items) + a
            copies.append(pltpu.make_async_remote_copy(
                src_ref=send(src_refs[sn], pj), dst_ref=land(land_refs[ln], me), send_sem=send_sems.at[sem],
                recv_sem=recv_sems.at[sem], device_id=(px, py, pc), device_id_type=MESH))
    return copies


def exchange(srcs, lands, items, after, *, name):
    sn, ln = list(srcs), list(lands)

    def body(*refs):
        src_refs = dict(zip(sn, refs[:len(sn)]))
        land_refs = dict(zip(ln, refs[len(sn) + len(ln) + 1:len(sn) + 2 * len(ln) + 1]))
        send_sems, recv_sems = refs[len(sn) + 2 * len(ln) + 1:]
        copies = _remote_copies(items, src_refs, land_refs, send_sems, recv_sems)
        for cp in copies:
            cp.start()
        for cp in copies:
            cp.wait_recv()
        for cp in copies:
            cp.wait_send()

    n_remote = (N_DEV - 1) * len(items)
    outs = pl.pallas_call(
        body, name=name,
        in_specs=[pl.BlockSpec(memory_space=pl.ANY)] * (len(sn) + len(ln) + 1),
        out_specs=tuple(pl.BlockSpec(memory_space=pl.ANY) for _ in ln),
        out_shape=tuple(jax.ShapeDtypeStruct(lands[n].shape, lands[n].dtype) for n in ln),
        input_output_aliases={len(sn) + i: i for i in range(len(ln))},
        scratch_shapes=[pltpu.SemaphoreType.DMA((n_remote,)), pltpu.SemaphoreType.DMA((n_remote,))],
    )(*[srcs[n] for n in sn], *[lands[n] for n in ln], after)
    return dict(zip(ln, outs))


def start_copies(srcs, lands, items, *, name):
    sn, ln = list(srcs), list(lands)
    n_remote = (N_DEV - 1) * len(items)

    def body(*refs):
        src_refs = dict(zip(sn, refs[:len(sn)]))
        land_refs = dict(zip(ln, refs[len(sn):len(sn) + len(ln)]))
        send_sems, recv_sems = refs[len(sn) + len(ln):len(sn) + len(ln) + 2]
        token = refs[-1]
        for cp in _remote_copies(items, src_refs, land_refs, send_sems, recv_sems):
            cp.start()
        token[...] = jnp.zeros_like(token)

    hbm = lambda a: pltpu.with_memory_space_constraint(a, pltpu.HBM)
    outs = pl.pallas_call(
        body, name=name,
        in_specs=[_HBM] * (len(sn) + len(ln)),
        out_specs=(_SEM, _SEM) + (_HBM,) * len(ln) + (pl.BlockSpec(memory_space=pltpu.VMEM),),
        out_shape=(pltpu.SemaphoreType.DMA((n_remote,)), pltpu.SemaphoreType.DMA((n_remote,)))
        + tuple(pltpu.HBM(lands[n].shape, lands[n].dtype) for n in ln) + (jax.ShapeDtypeStruct((8, LANE), F32),),
        input_output_aliases={len(sn) + i: 2 + i for i in range(len(ln))},
        compiler_params=pltpu.CompilerParams(has_side_effects=_EFFECT),
    )(*[hbm(srcs[n]) for n in sn], *[hbm(lands[n]) for n in ln])
    return (outs[0], outs[1]), dict(zip(ln, outs[2:2 + len(ln)])), outs[-1][0:1, 0:1]


def wait_copies(sems, srcs, lands, items, after, *, name):
    sn, ln = list(srcs), list(lands)

    def body(*refs):
        src_refs = dict(zip(sn, refs[:len(sn)]))
        land_refs = dict(zip(ln, refs[len(sn):len(sn) + len(ln)]))
        send_sems, recv_sems = refs[len(sn) + len(ln):len(sn) + len(ln) + 2]
        copies = _remote_copies(items, src_refs, land_refs, send_sems, recv_sems)
        for cp in copies:
            cp.wait_send()
        for cp in copies:
            cp.wait_recv()

    outs = pl.pallas_call(
        body, name=name,
        in_specs=[_HBM] * (len(sn) + len(ln)) + [_SEM, _SEM, pl.BlockSpec(memory_space=pl.ANY)],
        out_specs=(_HBM,) * len(ln),
        out_shape=tuple(pltpu.HBM(lands[n].shape, lands[n].dtype) for n in ln),
        input_output_aliases={len(sn) + i: i for i in range(len(ln))},
        compiler_params=pltpu.CompilerParams(has_side_effects=_EFFECT),
    )(*[srcs[n] for n in sn], *[lands[n] for n in ln], sems[0], sems[1], after)
    return dict(zip(ln, outs))


def _block(index, size, base=0):
    return pl.ds(pl.multiple_of(base + index * size, ROW_TILE), size)


def _adam_tile(rows):
    for t in (256, 176, 128):
        if rows % t == 0:
            return t
    return rows


def sum_adamw(recvs, w, m, v, *, name, first_slab=0, into=None):
    _, r, c = w.shape
    b = len(recvs)
    rp = recvs[0].shape[1]
    whole = r % ROW_TILE != 0
    tr = r if whole else _adam_tile(r)
    c1 = 1.0 / (1.0 - ADAM_B1 ** ADAM_STEP)
    c2 = 1.0 / (1.0 - ADAM_B2 ** ADAM_STEP)

    def body(*refs):
        recv_refs = refs[:b]
        w_ref, m_ref, v_ref = refs[b:b + 3]
        g_ref, d_ref, nm_ref, nv_ref = refs[b + 3 + (0 if into is None else 4):][:4]
        for slab, recv_ref in enumerate(recv_refs):
            @pl.when(pl.program_id(0) == slab)
            def _():
                g = recv_ref[0].astype(F32)
                for i in range(1, N_DEV):
                    g = g + recv_ref[i].astype(F32)
                if whole:
                    sum_ref = refs[-1]
                    sum_ref[...] = g
                    g = sum_ref[0:r, :]
                nm = ADAM_B1 * m_ref[0] + (1.0 - ADAM_B1) * g
                nv = ADAM_B2 * v_ref[0] + (1.0 - ADAM_B2) * (g * g)
                g_ref[0] = g
                nm_ref[0] = nm
                nv_ref[0] = nv
                d_ref[0] = -ADAM_LR * ((nm * c1) / (jnp.sqrt(nv * c2) + ADAM_EPS) + ADAM_WD * w_ref[0])

    tile = pl.BlockSpec((1, tr, c), lambda bi, i: (first_slab + bi, i, 0))
    piece = lambda slab: pl.BlockSpec((N_DEV, rp if whole else tr, c), lambda bi, i: (0, jnp.where(bi == slab, i, 0), 0))
    earlier = [] if into is None else list(into)
    return pl.pallas_call(
        body, name=name, grid=(b, r // tr),
        in_specs=[piece(slab) for slab in range(b)] + [tile, tile, tile] + [pl.BlockSpec(memory_space=pl.ANY)] * len(earlier),
        out_specs=(tile,) * 4, out_shape=(jax.ShapeDtypeStruct(w.shape, F32),) * 4,
        input_output_aliases={b + 3 + i: i for i in range(len(earlier))},
        scratch_shapes=[pltpu.VMEM((rp, c), F32)] if whole else [],
    )(*recvs, w, m, v, *earlier)


def _flat_rows(n_elems, row_multiple):
    rows = -(-n_elems // FLAT_COLS)
    return -(-rows // row_multiple) * row_multiple


def _pack(arrays, row_multiple, dtype):
    flat = jnp.concatenate([a.reshape(-1).astype(dtype) for a in arrays])
    rows = _flat_rows(flat.size, row_multiple)
    return jnp.pad(flat, (0, rows * FLAT_COLS - flat.size)).reshape(rows, FLAT_COLS)


def _unpack(flat2d, shapes):
    lead = flat2d.shape[:-2]
    flat = flat2d.reshape(lead + (-1,))
    out, off = [], 0
    for shp in shapes:
        n = int(np.prod(shp))
        out.append(flat[..., off:off + n].reshape(lead + tuple(shp)))
        off += n
    return out


def _join_shards(stacked, axis):
    moved = jnp.moveaxis(stacked, 0, axis)
    shp = list(moved.shape)
    shp[axis:axis + 2] = [shp[axis] * shp[axis + 1]]
    return moved.reshape(shp)


def _split_shards(full, axis):
    shp = list(full.shape)
    shp[axis:axis + 1] = [N_DEV, shp[axis] // N_DEV]
    return jnp.moveaxis(full.reshape(shp), axis, 0)


def kernel(x, meta_tokens, norm_w, ffn_w_gate, ffn_w_up, ffn_w_down, rel_bias_table, even_w_in, even_conv_w, swa_sinks, dn_a_log, dn_dt_bias, dn_norm_w, even_w_out, odd_w_in, gla_w_gate_up, gla_b_gate, gla_norm_w, odd_w_out, loss_target, m_meta_tokens, m_norm_w, m_ffn_w_gate, m_ffn_w_up, m_ffn_w_down, m_rel_bias_table, m_even_w_in, m_even_conv_w, m_swa_sinks, m_dn_a_log, m_dn_dt_bias, m_dn_norm_w, m_even_w_out, m_odd_w_in, m_gla_w_gate_up, m_gla_b_gate, m_gla_norm_w, m_odd_w_out, v_meta_tokens, v_norm_w, v_ffn_w_gate, v_ffn_w_up, v_ffn_w_down, v_rel_bias_table, v_even_w_in, v_even_conv_w, v_swa_sinks, v_dn_a_log, v_dn_dt_bias, v_dn_norm_w, v_even_w_out, v_odd_w_in, v_gla_w_gate_up, v_gla_b_gate, v_gla_norm_w, v_odd_w_out):
    args = locals()
    w = {n: args[n] for n in WEIGHTS}
    m = {n: args["m_" + n] for n in WEIGHTS}
    v = {n: args["v_" + n] for n in WEIGHTS}

    d = D_MODEL
    me = _my_index()
    whole = lambda ref, j: ref
    rows = lambda size, base=0: (lambda ref, i: ref.at[_block(i, size, base), :])
    lead = lambda ref, i: ref.at[i]
    of_group = lambda items, g: [it for it in items if it[4] == g]
    names = lambda items, k: list(dict.fromkeys(it[k] for it in items))

    def placed(shape, dtype, parts):
        land = lax.empty(shape, dtype)
        for part, axis, start in parts:
            land = lax.dynamic_update_slice(land, part, tuple(start if a == axis else 0 for a in range(land.ndim)))
        return land

    as_rows = lambda a: jnp.swapaxes(a, -1, -2)
    pad_rows = lambda a, to: jnp.pad(a, [(0, 0)] * (a.ndim - 2) + [(0, to - a.shape[-2]), (0, 0)])
    gate_s = pad_rows(as_rows(w["ffn_w_gate"].reshape(N_FFN, d, FF_SHARD)), FF_SHARD_PAD).astype(BF16)
    up_s = pad_rows(as_rows(w["ffn_w_up"].reshape(N_FFN, d, FF_SHARD)), FF_SHARD_PAD).astype(BF16)
    down_s = pad_rows(w["ffn_w_down"].reshape(N_FFN, FF_SHARD, d), FF_SHARD_PAD).astype(BF16)
    small_s = _pack([w[n] for n in SMALL], 8, F32)
    srcs_w = {"ein": pad_rows(as_rows(w["even_w_in"][0]), EVEN_IN_SHARD_PAD).astype(BF16),
              "oin": pad_rows(as_rows(w["odd_w_in"][0]), ODD_IN_SHARD_PAD).astype(BF16),
              "eout": w["even_w_out"][0].astype(BF16), "oout": w["odd_w_out"][0].astype(BF16), "small": small_s}
    lands_w = {"ein": placed((N_DEV * EVEN_IN_SHARD_PAD, d), BF16, [(srcs_w["ein"], 0, me * EVEN_IN_SHARD_PAD)]),
               "oin": placed((N_DEV * ODD_IN_SHARD_PAD, d), BF16, [(srcs_w["oin"], 0, me * ODD_IN_SHARD_PAD)]),
               "eout": placed((d, d), BF16, [(srcs_w["eout"], 0, me * OUT_SHARD)]),
               "oout": placed((d, d), BF16, [(srcs_w["oout"], 0, me * OUT_SHARD)]),
               "small": placed((N_DEV,) + small_s.shape, F32, [(small_s[None], 0, me)])}
    items_w = [("small", whole, "small", lead, "first"), ("ein", whole, "ein", rows(EVEN_IN_SHARD_PAD), "even"),
               ("eout", whole, "eout", rows(OUT_SHARD), "even"), ("oin", whole, "oin", rows(ODD_IN_SHARD_PAD), "odd"),
               ("oout", whole, "oout", rows(OUT_SHARD), "odd")]
    for i, group, down_group in ((0, "first", "down0"), (1, "ffn1", "ffn1"), (2, "ffn2", "down2"), (3, "ffn3", "ffn3")):
        srcs_w.update({f"gate{i}": gate_s[i], f"up{i}": up_s[i], f"down{i}": down_s[i]})
        lands_w[f"w_gu{i}"] = placed((2 * FF_PAD, d), BF16, [(srcs_w[f"gate{i}"], 0, me * FF_SHARD_PAD),
                                                             (srcs_w[f"up{i}"], 0, FF_PAD + me * FF_SHARD_PAD)])
        lands_w[f"w_down{i}"] = placed((FF_PAD, d), BF16, [(srcs_w[f"down{i}"], 0, me * FF_SHARD_PAD)])
        items_w += [(f"gate{i}", whole, f"w_gu{i}", rows(FF_SHARD_PAD), group),
                    (f"up{i}", whole, f"w_gu{i}", rows(FF_SHARD_PAD, FF_PAD), group),
                    (f"down{i}", whole, f"w_down{i}", rows(FF_SHARD_PAD), down_group)]
    pending, started = {}, []
    for g in ("first", "down0", "even", "ffn1", "ffn2", "down2", "odd", "ffn3"):
        its = of_group(items_w, g)
        srcs = {n: srcs_w[n] for n in names(its, 0)}
        sems, lands, token = start_copies(srcs, {n: lands_w[n] for n in names(its, 2)}, its, name=f"gather_start_{g}")
        pending[g] = (sems, srcs, lands, its)
        started.append(token)

    unpad = lambda p, shard, shard_pad: p.reshape(N_DEV, shard_pad, d)[:, :shard].reshape(N_DEV * shard, d)

    def get_w(stage, after):
        if stage not in pending:
            return {}
        sems, srcs, lands, its = pending[stage]
        landed = wait_copies(sems, srcs, lands, its, after, name=f"gather_wait_{stage}")
        got = {}
        for n, arr in landed.items():
            if n == "small":
                for sn, stacked in zip(SMALL, _unpack(arr, [w[sn].shape for sn in SMALL])):
                    got[sn] = _join_shards(stacked, SHARD_AXIS[sn])
            elif n == "ein":
                got["even_w_in"] = unpad(arr, EVEN_IN_SHARD, EVEN_IN_SHARD_PAD)
            elif n == "oin":
                got["odd_w_in"] = unpad(arr, ODD_IN_SHARD, ODD_IN_SHARD_PAD)
            elif n in ("eout", "oout"):
                got["even_w_out" if n == "eout" else "odd_w_out"] = arr
            else:
                got[n] = (arr, None)
        return got

    full = {n: w[n] for n in REPL}
    full.update(get_w("first", sum(started)))

    repad = lambda g, shard, shard_pad: pad_rows(g.reshape(N_DEV, shard, d), shard_pad)
    pieces_g = {"r_oin": ("oin", None, "gu2"), "r_oout": ("oout", (OUT_SHARD, 0), "gu2"),
                "r_ein": ("ein", None, "even"), "r_eout": ("eout", (OUT_SHARD, 0), "even"), "r_small": ("small", None, "last")}
    for i in range(N_FFN):
        pieces_g.update({f"r_gate{i}": (f"g_gu{i}", (FF_SHARD_PAD, 0), f"gu{i}"),
                         f"r_up{i}": (f"g_gu{i}", (FF_SHARD_PAD, FF_PAD), f"gu{i}"),
                         f"r_down{i}": (f"g_down{i}", (FF_SHARD_PAD, 0), "down0" if i == 0 else f"gu{i}")})
    items_g = [(src, lead if blk is None else rows(*blk), land, lead, group) for land, (src, blk, group) in pieces_g.items()]
    last_groups = ("down0", "gu0")

    def grad_src(n, grads):
        if n == "oin":
            return repad(grads["odd_w_in"], ODD_IN_SHARD, ODD_IN_SHARD_PAD).astype(BF16)
        if n == "ein":
            return repad(grads["even_w_in"], EVEN_IN_SHARD, EVEN_IN_SHARD_PAD).astype(BF16)
        if n in ("oout", "eout"):
            return grads["odd_w_out" if n == "oout" else "even_w_out"].astype(BF16)
        return grads[n]

    def grad_land(n, srcs):
        src, blk, _ = pieces_g[n]
        if blk is None:
            own = lax.dynamic_index_in_dim(srcs[src], me, 0, keepdims=False)
        else:
            own = lax.dynamic_slice_in_dim(srcs[src], blk[1] + me * blk[0], blk[0], 0)
        return placed((N_DEV,) + own.shape, own.dtype, [(own[None], 0, me)])

    sent = {}

    def put_g(stage, grads):
        its = of_group(items_g, stage)
        if not its:
            return jnp.zeros((1, 1), F32)
        srcs = {n: grad_src(n, grads) for n in names(its, 0)}
        lands = {n: grad_land(n, srcs) for n in names(its, 2)}
        sems, lands, token = start_copies(srcs, lands, its, name=f"grads_start_{stage}")
        sent[stage] = (sems, srcs, lands, its)
        return token

    loss, grad_x, grads = local_step(x[0], loss_target[0], full, get_w, put_g)
    loss = lax.psum(loss, AXES)

    order = SMALL + REPL
    pieces = [_split_shards(grads[n].reshape(full[n].shape), SHARD_AXIS[n]) if n in SHARD_AXIS
              else jnp.broadcast_to(grads[n].reshape(w[n].shape)[None], (N_DEV,) + w[n].shape) for n in order]
    flat = jnp.concatenate([p.reshape(N_DEV, -1) for p in pieces], axis=1)
    srows = _flat_rows(flat.shape[1], 8)
    grads["small"] = jnp.pad(flat, ((0, 0), (0, srows * FLAT_COLS - flat.shape[1]))).reshape(N_DEV, srows, FLAT_COLS)
    recv = {}
    for stage, (sems, srcs, lands, its) in sent.items():
        if stage not in last_groups:
            recv.update(wait_copies(sems, srcs, lands, its, grad_x, name=f"grads_wait_{stage}"))
    result = [{} for _ in range(4)]

    views = {"ffn_w_gate": (lambda a: as_rows(a.reshape(N_FFN, d, FF_SHARD)), lambda o, n: as_rows(o).reshape(w[n].shape)),
             "ffn_w_up": (lambda a: as_rows(a.reshape(N_FFN, d, FF_SHARD)), lambda o, n: as_rows(o).reshape(w[n].shape)),
             "ffn_w_down": (lambda a: a.reshape(N_FFN, FF_SHARD, d), lambda o, n: o.reshape(w[n].shape)),
             "even_w_in": (as_rows, lambda o, n: as_rows(o)), "odd_w_in": (as_rows, lambda o, n: as_rows(o)),
             "even_w_out": (lambda a: a, lambda o, n: o), "odd_w_out": (lambda a: a, lambda o, n: o)}

    def adam(n, recvs, first_slab=0, into=None):
        view = views[n][0]
        return sum_adamw(recvs, view(w[n]), view(m[n]), view(v[n]), name=f"adamw_{n}_{first_slab}",
                         first_slab=first_slab, into=into)

    def finish(n, outs):
        for r, o in zip(result, outs):
            r[n] = views[n][1](o, n)

    ffn_recv = (("ffn_w_gate", "r_gate"), ("ffn_w_up", "r_up"), ("ffn_w_down", "r_down"))
    early = {n: adam(n, [recv[f"{r}{i}"] for i in (1, 2, 3)], first_slab=1) for n, r in ffn_recv}
    for n, r in (("even_w_in", "r_ein"), ("odd_w_in", "r_oin"), ("even_w_out", "r_eout"), ("odd_w_out", "r_oout")):
        finish(n, adam(n, [recv[r]]))
    srcs = {"small": grads["small"]}
    recv.update(exchange(srcs, {"r_small": grad_land("r_small", srcs)}, of_group(items_g, "last"),
                         early["ffn_w_down"][0], name="exchange_small"))
    for stage in last_groups:
        sems, srcs, lands, its = sent[stage]
        recv.update(wait_copies(sems, srcs, lands, its, recv["r_small"], name=f"grads_wait_{stage}"))
    for n, r in ffn_recv:
        finish(n, adam(n, [recv[f"{r}0"]], into=early[n]))
    pack_local = lambda t: _pack([t[n] for n in order], 8, F32)[None]
    small_outs = sum_adamw([recv["r_small"]], pack_local(w), pack_local(m), pack_local(v), name="adamw_small")
    for r, o in zip(result, small_outs):
        r.update(zip(order, _unpack(o[0], [w[n].shape for n in order])))
    return (loss, grad_x[None], *[r[n] for r in result for n in WEIGHTS])
```

```python
import functools
import math

import numpy as np
import jax
import jax.numpy as jnp
from jax import lax
from jax.experimental import pallas as pl
from jax.experimental.pallas import tpu as pltpu

F32 = jnp.float32
BF16 = jnp.bfloat16
MESH = pl.DeviceIdType.MESH
AXES = ("x", "y", "c")
N_DEV = 8

D_MODEL = 1024
N_META = 16
D_FF = 2816
NORM_EPS = 1e-6
NEG_INF = -1e30
SWA_Q_HEADS = 8
SWA_HEAD_DIM = 64
SWA_WINDOW = 128
SWA_BLOCK = 128
REL_BUCKETS = 32
REL_MAX_DIST = 128
DN_HEADS = 4
DN_HEAD_DIM = 128
DN_CONV = 4
GLA_HEADS = 4
GLA_DK = 128
GLA_DV = 256
GLA_GATE_RANK = 16
GLA_GATE_NORM = 16.0
CHUNK = 64
PAD = SWA_BLOCK - N_META
LANE = 128
PROJ_DIM = 3200

ADAM_LR = 0.001
ADAM_B1 = 0.9
ADAM_B2 = 0.999
ADAM_EPS = 1e-08
ADAM_WD = 0.01
ADAM_STEP = 10

FF_SHARD = D_FF // N_DEV
FF_SHARD_PAD = 384
FF_PAD = N_DEV * FF_SHARD_PAD
N_FFN = 4
ROW_TILE = 16
EVEN_IN_SHARD, EVEN_IN_SHARD_PAD = 353, 368
ODD_IN_SHARD, ODD_IN_SHARD_PAD = 386, 400
OUT_SHARD = D_MODEL // N_DEV

FLAT_COLS = 128
BIG = ("ffn_w_gate", "ffn_w_up", "ffn_w_down", "even_w_in", "even_w_out", "odd_w_in", "odd_w_out")
SMALL = ("meta_tokens", "norm_w", "even_conv_w", "gla_w_gate_up", "gla_b_gate", "gla_norm_w")
REPL = ("rel_bias_table", "swa_sinks", "dn_a_log", "dn_dt_bias", "dn_norm_w")
WEIGHTS = ("meta_tokens", "norm_w", "ffn_w_gate", "ffn_w_up", "ffn_w_down", "rel_bias_table", "even_w_in",
           "even_conv_w", "swa_sinks", "dn_a_log", "dn_dt_bias", "dn_norm_w", "even_w_out", "odd_w_in",
           "gla_w_gate_up", "gla_b_gate", "gla_norm_w", "odd_w_out")
SHARD_AXIS = {"ffn_w_gate": 3, "ffn_w_up": 3, "ffn_w_down": 2, "even_w_in": 2, "even_w_out": 1, "odd_w_in": 2,
              "odd_w_out": 1, "meta_tokens": 1, "norm_w": 2, "even_conv_w": 2, "gla_w_gate_up": 2,
              "gla_b_gate": 1, "gla_norm_w": 1}


def _rms(x, w):
    r = lax.rsqrt(jnp.mean(x * x, axis=-1, keepdims=True) + NORM_EPS)
    return x * r * w


def _sigmoid(x):
    return 0.5 * (jnp.tanh(0.5 * x) + 1.0)


def _silu(x):
    return x * _sigmoid(x)


def _softplus(x):
    pos = x > 0
    return jnp.where(pos, x, 0.0) + jnp.log(1.0 + jnp.exp(jnp.where(pos, -x, x)))


def _l2n(x):
    return x * lax.rsqrt(jnp.sum(x * x, axis=-1, keepdims=True) + 1e-6)


def _split_bf16(x):
    hi = x.astype(BF16)
    return hi, (x - hi.astype(F32)).astype(BF16)


def _make_mm(terms, batched):
    off = 1 if batched else 0
    bdims = ((0,), (0,)) if batched else ((), ())

    def dg(a, b, ca, cb):
        dot = lambda p, q: lax.dot_general(p, q, (((ca + off,), (cb + off,)), bdims), preferred_element_type=F32)
        a_hi, a_lo = _split_bf16(a)
        b_hi, b_lo = _split_bf16(b)
        if terms == 1:
            return dot(a_hi, b_hi)
        return dot(a_hi, b_hi) + (dot(a_hi, b_lo) + dot(a_lo, b_hi))

    @jax.custom_vjp
    def nn(a, b):
        return dg(a, b, 1, 0)

    @jax.custom_vjp
    def nt(a, b):
        return dg(a, b, 1, 1)

    @jax.custom_vjp
    def tn(a, b):
        return dg(a, b, 0, 0)

    nn.defvjp(lambda a, b: (nn(a, b), (a, b)), lambda r, g: (nt(g, r[1]), tn(r[0], g)))
    nt.defvjp(lambda a, b: (nt(a, b), (a, b)), lambda r, g: (nn(g, r[1]), tn(g, r[0])))
    tn.defvjp(lambda a, b: (tn(a, b), (a, b)), lambda r, g: (nt(r[1], g), nn(r[0], g)))
    return nn, nt, tn


_mm, _mm_nt, _mm_tn = _make_mm(1, False)
_mm3, _, _ = _make_mm(3, False)
_bmm, _bmm_nt, _bmm_tn = _make_mm(1, True)
_bmm3, _bmm3_nt, _bmm3_tn = _make_mm(3, True)


@jax.custom_vjp
def _known_inverse(a, inv):
    return inv


_known_inverse.defvjp(lambda a, inv: (inv, inv),
                      lambda inv, g: (-_bmm3_tn(inv, _bmm3_nt(g, inv)), jnp.zeros_like(inv)))


def _tri_ones_dot(x, lower):
    n = x.shape[0]
    r = lax.broadcasted_iota(jnp.int32, (n, n), 0)
    c = lax.broadcasted_iota(jnp.int32, (n, n), 1)
    t = ((r >= c) if lower else (r <= c)).astype(BF16)
    hi, lo = _split_bf16(x)
    return jnp.dot(t, hi, preferred_element_type=F32) + jnp.dot(t, lo, preferred_element_type=F32)


@jax.custom_vjp
def _cumsum_rows(x):
    return _tri_ones_dot(x, True)


_cumsum_rows.defvjp(lambda x: (_tri_ones_dot(x, True), None), lambda _, g: (_tri_ones_dot(g, False),))


def _row_tile(n_rows, cap):
    best = LANE
    for t in range(LANE, cap + 1, LANE):
        if n_rows % t == 0:
            best = t
    return best


def _real_rows(tile_index, tm):
    row = tile_index * tm + lax.broadcasted_iota(jnp.int32, (tm, 1), 0)
    return (row >= PAD).astype(F32)


def _full(shape):
    return pl.BlockSpec(shape, lambda *_: (0,) * len(shape))


def _resident(shape):
    return pl.BlockSpec(shape, lambda *_: (0,) * len(shape), pipeline_mode=pl.Buffered(1))


def _resident_w(wmat, widx):
    if wmat.ndim == 2:
        return _resident(wmat.shape)
    return pl.BlockSpec((None,) + wmat.shape[1:], lambda *_: (widx, 0, 0), pipeline_mode=pl.Buffered(1))


def rms_mm(h, w, wmat_t, *, swiglu, name, widx=None):
    tp, d = h.shape
    n = wmat_t.shape[-2]
    tm = _row_tile(tp, 384)
    half = n // 2
    wmat = wmat_t

    def body(h_ref, w_ref, wm_ref, hn_ref, *outs):
        hn = _rms(h_ref[...], w_ref[...]).astype(BF16)
        hn_ref[...] = hn
        p = lax.dot_general(hn, wm_ref[...], (((1,), (1,)), ((), ())), preferred_element_type=F32)
        if swiglu:
            g, u = p[:, :half], p[:, half:]
            outs[0][...] = g.astype(BF16)
            outs[1][...] = u.astype(BF16)
            outs[2][...] = (_silu(g) * u).astype(BF16)
        else:
            outs[0][...] = p

    row = lambda width: pl.BlockSpec((tm, width), lambda i: (i, 0))
    if swiglu:
        out_shape = (jax.ShapeDtypeStruct((tp, d), BF16),) + (jax.ShapeDtypeStruct((tp, half), BF16),) * 3
        out_specs = (row(d), row(half), row(half), row(half))
    else:
        out_shape = (jax.ShapeDtypeStruct((tp, d), BF16), jax.ShapeDtypeStruct((tp, n), F32))
        out_specs = (row(d), row(n))
    return pl.pallas_call(
        body, name=name, grid=(tp // tm,),
        in_specs=[row(d), _full((1, d)), _resident_w(wmat, widx)],
        out_specs=out_specs, out_shape=out_shape,
    )(h, w, wmat)


def mm_rms_res(acts, wmat, h, w, *, scale, name, widx=None):
    tp, d = h.shape
    tm = _row_tile(tp, 384)
    widths = [a.shape[1] for a in acts]
    offs = [sum(widths[:i]) for i in range(len(acts))]
    na = len(acts)

    def body(*refs):
        a_refs = refs[:na]
        wm_ref, h_ref, w_ref, f_ref, ho_ref = refs[na:]
        f = None
        for a_ref, off, width in zip(a_refs, offs, widths):
            part = jnp.dot(a_ref[...].astype(BF16), wm_ref[off:off + width, :], preferred_element_type=F32)
            f = part if f is None else f + part
        f_ref[...] = f
        ho_ref[...] = h_ref[...] + scale * _rms(f, w_ref[...])

    row = lambda width: pl.BlockSpec((tm, width), lambda i: (i, 0))
    return pl.pallas_call(
        body, name=name, grid=(tp // tm,),
        in_specs=[row(wd) for wd in widths] + [_resident_w(wmat, widx), row(d), _full((1, d))],
        out_specs=(row(d), row(d)),
        out_shape=(jax.ShapeDtypeStruct((tp, d), F32), jax.ShapeDtypeStruct((tp, d), F32)),
    )(*acts, wmat, h, w)


def mm_rms_res_bwd(dho, f, w, wmat, gu, *, scale, name, widx=None):
    tp, d = f.shape
    k = wmat.shape[-2]
    tm = _row_tile(tp, 384)
    swiglu = gu is not None

    def body(*refs):
        if swiglu:
            dho_ref, f_ref, w_ref, wm_ref, g_ref, u_ref, df_ref, dw_ref, dgu_ref = refs
        else:
            dho_ref, f_ref, w_ref, wm_ref, df_ref, dw_ref, da_ref = refs
        i = pl.program_id(0)
        _, vjp = jax.vjp(lambda ff, ww: scale * _rms(ff, ww), f_ref[...], w_ref[...])
        df, dw = vjp(dho_ref[...])
        dfb = (df * _real_rows(i, tm)).astype(BF16)
        df_ref[...] = dfb

        @pl.when(i == 0)
        def _():
            dw_ref[...] = jnp.zeros_like(dw_ref)

        dw_ref[...] += dw
        da = lax.dot_general(dfb, wm_ref[...], (((1,), (1,)), ((), ())), preferred_element_type=F32)
        if swiglu:
            g, u, dab = g_ref[...], u_ref[...], da.astype(BF16)
            s = _sigmoid(g)
            dgu_ref[:, :k] = dab * u * s * (1.0 + g * (1.0 - s))
            dgu_ref[:, k:] = dab * g * s
        else:
            da_ref[...] = da

    row = lambda width: pl.BlockSpec((tm, width), lambda i: (i, 0))
    in_specs = [row(d), row(d), _full((1, d)), _resident_w(wmat, widx)]
    args = [dho, f, w, wmat]
    out_shape = [jax.ShapeDtypeStruct((tp, d), BF16), jax.ShapeDtypeStruct((1, d), F32)]
    out_specs = [row(d), _full((1, d))]
    if swiglu:
        in_specs += [row(k), row(k)]
        args += list(gu)
        out_shape += [jax.ShapeDtypeStruct((tp, 2 * k), BF16)]
        out_specs += [row(2 * k)]
    else:
        out_shape += [jax.ShapeDtypeStruct((tp, k), F32)]
        out_specs += [row(k)]
    return pl.pallas_call(body, name=name, grid=(tp // tm,), in_specs=in_specs, out_specs=tuple(out_specs),
                          out_shape=tuple(out_shape))(*args)


def rms_mm_bwd(dps, wmat, h, w, dho, *, name, widx=None):
    tp, d = h.shape
    tm = _row_tile(tp, 384)
    widths = [p.shape[1] for p in dps]
    offs = [sum(widths[:i]) for i in range(len(dps))]
    ndp = len(dps)

    def body(*refs):
        dp_refs = refs[:ndp]
        wm_ref, h_ref, w_ref, dho_ref, dh_ref, dw_ref = refs[ndp:]
        i = pl.program_id(0)
        dhn = None
        for dp_ref, off, width in zip(dp_refs, offs, widths):
            part = jnp.dot(dp_ref[...].astype(BF16), wm_ref[off:off + width, :], preferred_element_type=F32)
            dhn = part if dhn is None else dhn + part
        _, vjp = jax.vjp(_rms, h_ref[...], w_ref[...])
        dx, dw = vjp(dhn)
        dh_ref[...] = (dho_ref[...] + dx) * _real_rows(i, tm)

        @pl.when(i == 0)
        def _():
            dw_ref[...] = jnp.zeros_like(dw_ref)

        dw_ref[...] += dw

    row = lambda width: pl.BlockSpec((tm, width), lambda i: (i, 0))
    return pl.pallas_call(
        body, name=name, grid=(tp // tm,),
        in_specs=[row(wd) for wd in widths] + [_resident_w(wmat, widx), row(d), _full((1, d)), row(d)],
        out_specs=(row(d), _full((1, d))),
        out_shape=(jax.ShapeDtypeStruct((tp, d), F32), jax.ShapeDtypeStruct((1, d), F32)),
    )(*dps, wmat, h, w, dho)


def mm_tn(a, b, *, name, out_dtype=F32, after=None):
    t, m = a.shape
    n = b.shape[1]
    bm = _row_tile(m, 1024 if n <= 1024 else 512)
    bn = _row_tile(n, 1536)
    bk = _row_tile(t, 1408)
    nk = t // bk
    ties = [] if after is None else [after]

    def body(a_ref, b_ref, *rest):
        o_ref, acc = rest[-2:]

        @pl.when(pl.program_id(2) == 0)
        def _():
            acc[...] = jnp.zeros_like(acc)

        acc[...] += lax.dot_general(a_ref[...].astype(BF16), b_ref[...].astype(BF16), (((0,), (0,)), ((), ())),
                                    preferred_element_type=F32)

        @pl.when(pl.program_id(2) == nk - 1)
        def _():
            o_ref[...] = acc[...].astype(o_ref.dtype)

    return pl.pallas_call(
        body, name=name, grid=(m // bm, n // bn, nk),
        in_specs=[pl.BlockSpec((bk, bm), lambda i, j, kk: (kk, i)), pl.BlockSpec((bk, bn), lambda i, j, kk: (kk, j))]
        + [pl.BlockSpec(memory_space=pl.ANY)] * len(ties),
        out_specs=pl.BlockSpec((bm, bn), lambda i, j, kk: (i, j)),
        out_shape=jax.ShapeDtypeStruct((m, n), out_dtype), scratch_shapes=[pltpu.VMEM((bm, bn), F32)],
    )(a, b, *ties)


def loss_and_grad(h, target, *, name):
    tp, d = h.shape
    tm = SWA_BLOCK

    def body(h_ref, t_ref, dh_ref, loss_ref):
        i = pl.program_id(0)

        @pl.when(i == 0)
        def _():
            loss_ref[...] = jnp.zeros_like(loss_ref)
            dh_ref[...] = jnp.zeros_like(dh_ref)

        @pl.when(i > 0)
        def _():
            err = h_ref[...] - t_ref[...]
            dh_ref[...] = err * (1.0 / d)
            loss_ref[...] += 0.5 * jnp.sum(jnp.sum(err * err, axis=1, keepdims=True), axis=0, keepdims=True) * (1.0 / d)

    return pl.pallas_call(
        body, name=name, grid=(tp // tm,),
        in_specs=[pl.BlockSpec((tm, d), lambda i: (i, 0)), pl.BlockSpec((tm, d), lambda i: (jnp.maximum(i - 1, 0), 0))],
        out_specs=(pl.BlockSpec((tm, d), lambda i: (i, 0)), _full((1, 1))),
        out_shape=(jax.ShapeDtypeStruct((tp, d), F32), jax.ShapeDtypeStruct((1, 1), F32)),
    )(h, target)


def _t5_bucket_np(rel):
    n = np.maximum(rel, 0)
    max_exact = REL_BUCKETS // 2
    n_f = np.maximum(n, 1).astype(np.float32)
    large = max_exact + (np.log(n_f / np.float32(max_exact)) / np.float32(math.log(REL_MAX_DIST / max_exact))
                         * np.float32(REL_BUCKETS - max_exact)).astype(np.int32)
    large = np.minimum(large, REL_BUCKETS - 1)
    return np.where(n < max_exact, n, large).astype(np.int32)


def _swa_positions_np(n):
    i = np.arange(SWA_BLOCK)[:, None]
    j = np.arange(3 * SWA_BLOCK)[None, :]
    pos_q = n * SWA_BLOCK + i - PAD
    pos_k = np.where(j < SWA_BLOCK, j - PAD, (n - 1) * SWA_BLOCK + (j - SWA_BLOCK) - PAD)
    return pos_q, pos_k


def _swa_buckets():
    out = []
    for n in range(3):
        pos_q, pos_k = _swa_positions_np(n)
        out.append(_t5_bucket_np(pos_q - pos_k))
    return jnp.asarray(np.stack(out))


def swa_bias(table, buckets, *, name):
    nc, nq, nk = buckets.shape

    def body(tab_ref, bkt_ref, out_ref):
        for c in range(nc):
            bkt = bkt_ref[c]
            for h in range(SWA_Q_HEADS):
                acc = jnp.zeros((nq, nk), F32)
                for b in range(REL_BUCKETS):
                    acc = jnp.where(bkt == b, tab_ref[b, h], acc)
                out_ref[c, h] = acc

    return pl.pallas_call(
        body, name=name,
        in_specs=[pl.BlockSpec(memory_space=pltpu.SMEM), pl.BlockSpec(memory_space=pltpu.VMEM)],
        out_specs=pl.BlockSpec(memory_space=pltpu.VMEM),
        out_shape=jax.ShapeDtypeStruct((nc, SWA_Q_HEADS, nq, nk), F32),
    )(table, buckets)


def swa_bias_bwd(dbias, buckets, *, name):
    nc = buckets.shape[0]

    def body(db_ref, bkt_ref, out_ref):
        lane = lax.broadcasted_iota(jnp.int32, (1, LANE), 1)
        for b in range(REL_BUCKETS):
            row = jnp.zeros((1, LANE), F32)
            for c in range(nc):
                hit = bkt_ref[c] == b
                for h in range(SWA_Q_HEADS):
                    part = jnp.where(hit, db_ref[c, h], 0.0)
                    tot = jnp.sum(jnp.sum(part, axis=1, keepdims=True), axis=0, keepdims=True)
                    row = row + jnp.where(lane == h, tot, 0.0)
            out_ref[b:b + 1, :] = row

    return pl.pallas_call(
        body, name=name,
        in_specs=[pl.BlockSpec(memory_space=pltpu.VMEM), pl.BlockSpec(memory_space=pltpu.VMEM)],
        out_specs=pl.BlockSpec(memory_space=pltpu.VMEM),
        out_shape=jax.ShapeDtypeStruct((REL_BUCKETS, LANE), F32),
    )(dbias, buckets)


def _swa_block(q, kvm, kvp, kvc, bias, sinks, n, batched):
    blk = SWA_BLOCK
    i = lax.broadcasted_iota(jnp.int32, (blk, 3 * blk), 0)
    j = lax.broadcasted_iota(jnp.int32, (blk, 3 * blk), 1)
    pos_q = n * blk + i - PAD
    is_meta = j < blk
    pos_k = jnp.where(is_meta, j - PAD, (n - 1) * blk + (j - blk) - PAD)
    rel = pos_q - pos_k
    valid = ((is_meta & (pos_k >= 0) & (pos_k < N_META) & (rel >= 0))
             | (jnp.logical_not(is_meta) & (pos_k >= N_META) & (rel >= 0) & (rel < SWA_WINDOW))).astype(F32)
    kv = jnp.concatenate([kvm, kvp, kvc], axis=0)
    lane = lax.broadcasted_iota(jnp.int32, (1, LANE), 1)
    halves = ((lane < SWA_HEAD_DIM).astype(F32), (lane >= SWA_HEAD_DIM).astype(F32))
    nh, group = SWA_Q_HEADS, SWA_Q_HEADS // 2
    q_of = lambda h: q[:, (h // 2) * LANE:(h // 2 + 1) * LANE] * halves[h % 2]
    k_of = lambda h: kv[:, (h // group) * LANE:(h // group + 1) * LANE]
    v_of = lambda h: kv[:, (2 + h // group) * LANE:(3 + h // group) * LANE]
    sink_of = lambda h: jnp.sum(jnp.where(lane == h, sinks, 0.0), axis=1, keepdims=True)

    scale = SWA_HEAD_DIM ** -0.5

    def attend(logits, sink, pv):
        s = logits * valid + (valid - 1.0) * (-NEG_INF)
        m = lax.stop_gradient(jnp.maximum(jnp.max(s, axis=-1, keepdims=True), sink))
        e = jnp.exp(s - m)
        return pv(e / (jnp.sum(e, axis=-1, keepdims=True) + jnp.exp(sink - m)))

    if batched:
        heads = range(nh)
        vh = _stack([v_of(h) for h in heads])
        qk = _bmm_nt(_stack([q_of(h) for h in heads]), _stack([k_of(h) for h in heads]))
        o = attend(qk * scale + bias, _stack([sink_of(h) for h in heads]), lambda p: _bmm(p, vh))
    else:
        o = [attend(_mm_nt(q_of(h), k_of(h)) * scale + bias[h], sink_of(h), lambda p, h=h: _mm(p, v_of(h)))
             for h in range(nh)]
    return jnp.concatenate([o[2 * p] * halves[0] + o[2 * p + 1] * halves[1] for p in range(nh // 2)], axis=1)


def _swa_in_specs(nb, rev):
    blk = SWA_BLOCK
    step = (lambda i: nb - 1 - i) if rev else (lambda i: i)
    return [
        pl.BlockSpec((blk, 4 * LANE), lambda i: (step(i), 0)),
        pl.BlockSpec((blk, 4 * LANE), lambda i: (0, 1)),
        pl.BlockSpec((blk, 4 * LANE), lambda i: (jnp.maximum(step(i) - 1, 0), 1)),
        pl.BlockSpec((blk, 4 * LANE), lambda i: (step(i), 1)),
        pl.BlockSpec((1, SWA_Q_HEADS, blk, 3 * blk), lambda i: (jnp.minimum(step(i), 2), 0, 0, 0)),
        _full((1, LANE)),
    ]


def swa_fwd(proj, bias, sinks, *, name):
    tp = proj.shape[0]
    nb = tp // SWA_BLOCK

    def body(q_ref, kvm_ref, kvp_ref, kvc_ref, bias_ref, sinks_ref, o_ref):
        n = pl.program_id(0)
        o_ref[...] = _swa_block(q_ref[...], kvm_ref[...], kvp_ref[...], kvc_ref[...], bias_ref[0], sinks_ref[...], n, True)

    return pl.pallas_call(
        body, name=name, grid=(nb,),
        in_specs=_swa_in_specs(nb, False),
        out_specs=pl.BlockSpec((SWA_BLOCK, 4 * LANE), lambda i: (i, 0)),
        out_shape=jax.ShapeDtypeStruct((tp, 4 * LANE), F32),
    )(proj, proj, proj, proj, bias, sinks)


def swa_bwd(proj, bias, sinks, do, *, name):
    tp = proj.shape[0]
    nb = tp // SWA_BLOCK
    blk = SWA_BLOCK

    def body(q_ref, kvm_ref, kvp_ref, kvc_ref, bias_ref, sinks_ref, do_ref, dq_ref, dkv_ref, dbias_ref, dsinks_ref,
             carry, meta_acc):
        i = pl.program_id(0)
        n = nb - 1 - i

        @pl.when(i == 0)
        def _():
            carry[...] = jnp.zeros_like(carry)
            meta_acc[...] = jnp.zeros_like(meta_acc)
            dsinks_ref[...] = jnp.zeros_like(dsinks_ref)

        fn = lambda q, kvm, kvp, kvc, b, s: _swa_block(q, kvm, kvp, kvc, b, s, n, False)
        _, vjp = jax.vjp(fn, q_ref[...], kvm_ref[...], kvp_ref[...], kvc_ref[...], bias_ref[0], sinks_ref[...])
        dq, dkvm, dkvp, dkvc, dbias, dsinks = vjp(do_ref[...])
        dq_ref[...] = dq
        meta_acc[...] += dkvm
        dkv_ref[...] = dkvc + carry[...] + jnp.where(n == 0, meta_acc[...], 0.0)
        carry[...] = dkvp
        first_visit = (n == nb - 1) | (n < 2)

        @pl.when(first_visit)
        def _():
            dbias_ref[0] = dbias

        @pl.when(jnp.logical_not(first_visit))
        def _():
            dbias_ref[0] += dbias

        dsinks_ref[...] += dsinks

    rev = lambda i: nb - 1 - i
    return pl.pallas_call(
        body, name=name, grid=(nb,),
        in_specs=_swa_in_specs(nb, True) + [pl.BlockSpec((blk, 4 * LANE), lambda i: (rev(i), 0))],
        out_specs=(pl.BlockSpec((blk, 4 * LANE), lambda i: (rev(i), 0)),
                   pl.BlockSpec((blk, 4 * LANE), lambda i: (rev(i), 0)),
                   pl.BlockSpec((1, SWA_Q_HEADS, blk, 3 * blk), lambda i: (jnp.minimum(rev(i), 2), 0, 0, 0)),
                   _full((1, LANE))),
        out_shape=(jax.ShapeDtypeStruct((tp, 4 * LANE), F32), jax.ShapeDtypeStruct((tp, 4 * LANE), F32),
                   jax.ShapeDtypeStruct((3, SWA_Q_HEADS, blk, 3 * blk), F32), jax.ShapeDtypeStruct((1, LANE), F32)),
        scratch_shapes=[pltpu.VMEM((blk, 4 * LANE), F32), pltpu.VMEM((blk, 4 * LANE), F32)],
    )(proj, proj, proj, proj, bias, sinks, do)


CONV_COL0 = 2
HALO = 8


def conv_fwd(proj, conv_w, *, name):
    tp = proj.shape[0]
    tm = _row_tile(tp, 384)
    cw = 4 * LANE
    ncol = conv_w.shape[1] // cw

    def body(x_ref, halo_ref, w_ref, y_ref, buf):
        i = pl.program_id(1)
        buf[0:HALO, :] = jnp.where(i > 0, halo_ref[...], 0.0)
        buf[HALO:, :] = x_ref[...]
        acc = None
        for j in range(DN_CONV):
            term = w_ref[j:j + 1, :] * buf[pl.ds(HALO - (DN_CONV - 1) + j, tm), :]
            acc = term if acc is None else acc + term
        y_ref[...] = acc

    return pl.pallas_call(
        body, name=name, grid=(ncol, tp // tm),
        in_specs=[pl.BlockSpec((tm, cw), lambda c, i: (i, CONV_COL0 + c)),
                  pl.BlockSpec((HALO, cw), lambda c, i: (jnp.maximum(i * (tm // HALO) - 1, 0), CONV_COL0 + c)),
                  pl.BlockSpec((DN_CONV, cw), lambda c, i: (0, c))],
        out_specs=pl.BlockSpec((tm, cw), lambda c, i: (i, c)),
        out_shape=jax.ShapeDtypeStruct((tp, ncol * cw), F32),
        scratch_shapes=[pltpu.VMEM((tm + HALO, cw), F32)],
    )(proj, proj, conv_w)


def conv_bwd(proj, conv_w, dy, *, name):
    tp = proj.shape[0]
    tm = _row_tile(tp, 384)
    cw = 4 * LANE
    ncol = conv_w.shape[1] // cw
    nt = tp // tm

    def body(x_ref, xhalo_ref, w_ref, dy_ref, dyhalo_ref, dx_ref, dw_ref, xbuf, dbuf):
        i = pl.program_id(1)
        xbuf[0:HALO, :] = jnp.where(i > 0, xhalo_ref[...], 0.0)
        xbuf[HALO:, :] = x_ref[...]
        dbuf[0:tm, :] = dy_ref[...]
        dbuf[tm:, :] = jnp.where(i < nt - 1, dyhalo_ref[...], 0.0)
        dy_t = dy_ref[...]
        acc = None
        rows = []
        for j in range(DN_CONV):
            term = w_ref[j:j + 1, :] * dbuf[pl.ds(DN_CONV - 1 - j, tm), :]
            acc = term if acc is None else acc + term
            rows.append(jnp.sum(dy_t * xbuf[pl.ds(HALO - (DN_CONV - 1) + j, tm), :], axis=0, keepdims=True))
        dx_ref[...] = acc

        @pl.when(i == 0)
        def _():
            dw_ref[...] = jnp.zeros_like(dw_ref)

        for j in range(DN_CONV):
            dw_ref[j:j + 1, :] += rows[j]

    return pl.pallas_call(
        body, name=name, grid=(ncol, nt),
        in_specs=[pl.BlockSpec((tm, cw), lambda c, i: (i, CONV_COL0 + c)),
                  pl.BlockSpec((HALO, cw), lambda c, i: (jnp.maximum(i * (tm // HALO) - 1, 0), CONV_COL0 + c)),
                  pl.BlockSpec((DN_CONV, cw), lambda c, i: (0, c)),
                  pl.BlockSpec((tm, cw), lambda c, i: (i, c)),
                  pl.BlockSpec((HALO, cw), lambda c, i: (jnp.minimum((i + 1) * (tm // HALO), tp // HALO - 1), c))],
        out_specs=(pl.BlockSpec((tm, cw), lambda c, i: (i, c)), pl.BlockSpec((DN_CONV, cw), lambda c, i: (0, c))),
        out_shape=(jax.ShapeDtypeStruct((tp, ncol * cw), F32), jax.ShapeDtypeStruct((DN_CONV, ncol * cw), F32)),
        scratch_shapes=[pltpu.VMEM((tm + HALO, cw), F32), pltpu.VMEM((tm + HALO, cw), F32)],
    )(proj, proj, conv_w, dy, dy)


def _stack(parts):
    return jnp.concatenate([p[None] for p in parts], axis=0)


def _chunk_masks():
    r = lax.broadcasted_iota(jnp.int32, (CHUNK, CHUNK), 0)
    c = lax.broadcasted_iota(jnp.int32, (CHUNK, CHUNK), 1)
    return (r >= c).astype(F32), (r > c).astype(F32), (r == c).astype(F32)


def _dn_chunk(y, z, small, s, a_log, dt_bias, norm_w, rows, known_inv=None):
    tri_incl, tri_strict, eye = _chunk_masks()
    lane = lax.broadcasted_iota(jnp.int32, (1, LANE), 1)
    dk = DN_HEAD_DIM
    nh = DN_HEADS
    heads = lambda t, first: _stack([t[:, (first + h) * dk:(first + h + 1) * dk] for h in range(nh)])
    pick = lambda t, l: jnp.sum(jnp.where(lane == l, t, 0.0), axis=1, keepdims=True)
    q = _l2n(_silu(heads(y, 0))) * dk ** -0.5
    k = _l2n(_silu(heads(y, nh)))
    v = _silu(heads(y, 2 * nh))
    g_all = jnp.where(lane < nh, -jnp.exp(a_log) * _softplus(small + dt_bias), 0.0) * rows
    beta_all = _sigmoid(small)
    gc_all = _cumsum_rows(g_all)
    g_sum = jnp.sum(g_all, axis=0, keepdims=True)
    gc = _stack([pick(gc_all, h) for h in range(nh)])
    beta = _stack([pick(beta_all, nh + h) for h in range(nh)])
    g_last = _stack([pick(g_sum, h) for h in range(nh)])
    gc_row = jnp.sum(eye * gc, axis=1, keepdims=True)
    gamma = jnp.exp((gc - gc_row) * tri_incl) * tri_incl
    k_beta = k * beta
    v_beta = v * beta
    a = _bmm_nt(k_beta, k) * gamma * tri_strict
    if known_inv is None:
        inv = eye - a
        power = a
        for _ in range(5):
            power = _bmm3(power, power)
            inv = inv + _bmm3(inv, power)
    else:
        inv = _known_inverse(a, known_inv)
    e_gc = jnp.exp(gc)
    uw = _bmm3(inv, jnp.concatenate([v_beta, k_beta * e_gc], axis=2))
    u, w = uw[:, :, :dk], uw[:, :, dk:]
    attn = _bmm_nt(q, k) * gamma
    q_dec = q * e_gc
    k_dec = k * jnp.exp(g_last - gc)
    v_new = u - _bmm(w, s)
    o = _bmm(q_dec, s) + _bmm(attn, v_new)
    s_new = s * jnp.exp(g_last) + _bmm_tn(k_dec, v_new)
    out = _rms(o, norm_w) * _silu(heads(z, 0))
    return jnp.concatenate([out[h] for h in range(nh)], axis=1), s_new, inv


Z_COL = 5
SMALL_COL = 24


def _chunk_rows(n):
    row = n * CHUNK + lax.broadcasted_iota(jnp.int32, (CHUNK, 1), 0)
    return (row >= PAD).astype(F32)


def dn_fwd(y, proj, a_log, dt_bias, norm_w, *, name):
    tp = y.shape[0]
    nc = tp // CHUNK
    dk = DN_HEAD_DIM

    def body(y_ref, z_ref, small_ref, al_ref, dt_ref, nw_ref, o_ref, ssave_ref, isave_ref, state):
        n = pl.program_id(0)

        @pl.when(n == 0)
        def _():
            state[...] = jnp.zeros_like(state)

        ssave_ref[0] = state[...]
        out, s_new, inv = _dn_chunk(y_ref[...], z_ref[...], small_ref[...], state[...], al_ref[...], dt_ref[...],
                                    nw_ref[...], _chunk_rows(n))
        o_ref[...] = out
        isave_ref[0] = inv
        state[...] = s_new

    return pl.pallas_call(
        body, name=name, grid=(nc,),
        in_specs=[pl.BlockSpec((CHUNK, y.shape[1]), lambda n: (n, 0)),
                  pl.BlockSpec((CHUNK, 4 * LANE), lambda n: (n, Z_COL)),
                  pl.BlockSpec((CHUNK, LANE), lambda n: (n, SMALL_COL)),
                  _full((1, LANE)), _full((1, LANE)), _full((1, LANE))],
        out_specs=(pl.BlockSpec((CHUNK, 4 * LANE), lambda n: (n, 0)),
                   pl.BlockSpec((1, DN_HEADS, dk, dk), lambda n: (n, 0, 0, 0)),
                   pl.BlockSpec((1, DN_HEADS, CHUNK, CHUNK), lambda n: (n, 0, 0, 0))),
        out_shape=(jax.ShapeDtypeStruct((tp, 4 * LANE), F32), jax.ShapeDtypeStruct((nc, DN_HEADS, dk, dk), F32),
                   jax.ShapeDtypeStruct((nc, DN_HEADS, CHUNK, CHUNK), F32)),
        scratch_shapes=[pltpu.VMEM((DN_HEADS, dk, dk), F32)],
    )(y, proj, proj, a_log, dt_bias, norm_w)


def dn_bwd(y, proj, a_log, dt_bias, norm_w, ssave, isave, do, *, name):
    tp = y.shape[0]
    nc = tp // CHUNK
    dk = DN_HEAD_DIM
    rev = lambda i: nc - 1 - i

    def body(y_ref, z_ref, small_ref, al_ref, dt_ref, nw_ref, ss_ref, is_ref, do_ref,
             dy_ref, dz_ref, dsmall_ref, dal_ref, ddt_ref, dnw_ref, dstate):
        i = pl.program_id(0)
        n = nc - 1 - i

        @pl.when(i == 0)
        def _():
            dstate[...] = jnp.zeros_like(dstate)
            dal_ref[...] = jnp.zeros_like(dal_ref)
            ddt_ref[...] = jnp.zeros_like(ddt_ref)
            dnw_ref[...] = jnp.zeros_like(dnw_ref)

        rows = _chunk_rows(n)
        known_inv = is_ref[0]
        fn = lambda *a: _dn_chunk(*a, rows, known_inv)[:2]
        _, vjp = jax.vjp(fn, y_ref[...], z_ref[...], small_ref[...], ss_ref[0], al_ref[...], dt_ref[...], nw_ref[...])
        dy, dz, dsmall, ds, dal, ddt, dnw = vjp((do_ref[...], dstate[...]))
        dy_ref[...] = dy
        dz_ref[...] = dz
        dsmall_ref[...] = dsmall
        dstate[...] = ds
        dal_ref[...] += dal
        ddt_ref[...] += ddt
        dnw_ref[...] += dnw

    return pl.pallas_call(
        body, name=name, grid=(nc,),
        in_specs=[pl.BlockSpec((CHUNK, y.shape[1]), lambda i: (rev(i), 0)),
                  pl.BlockSpec((CHUNK, 4 * LANE), lambda i: (rev(i), Z_COL)),
                  pl.BlockSpec((CHUNK, LANE), lambda i: (rev(i), SMALL_COL)),
                  _full((1, LANE)), _full((1, LANE)), _full((1, LANE)),
                  pl.BlockSpec((1, DN_HEADS, dk, dk), lambda i: (rev(i), 0, 0, 0)),
                  pl.BlockSpec((1, DN_HEADS, CHUNK, CHUNK), lambda i: (rev(i), 0, 0, 0)),
                  pl.BlockSpec((CHUNK, 4 * LANE), lambda i: (rev(i), 1))],
        out_specs=(pl.BlockSpec((CHUNK, y.shape[1]), lambda i: (rev(i), 0)),
                   pl.BlockSpec((CHUNK, 4 * LANE), lambda i: (rev(i), 0)),
                   pl.BlockSpec((CHUNK, LANE), lambda i: (rev(i), 0)),
                   _full((1, LANE)), _full((1, LANE)), _full((1, LANE))),
        out_shape=(jax.ShapeDtypeStruct((tp, y.shape[1]), F32), jax.ShapeDtypeStruct((tp, 4 * LANE), F32),
                   jax.ShapeDtypeStruct((tp, LANE), F32), jax.ShapeDtypeStruct((1, LANE), F32),
                   jax.ShapeDtypeStruct((1, LANE), F32), jax.ShapeDtypeStruct((1, LANE), F32)),
        scratch_shapes=[pltpu.VMEM((DN_HEADS, dk, dk), F32)],
    )(y, proj, proj, a_log, dt_bias, norm_w, ssave, isave, do)


def _gla_chunk(q, k, v, gate, low, s, w_gate_up, b_gate, norm_w, rows):
    tri_incl, _, _ = _chunk_masks()
    dk, dv, nh = GLA_DK, GLA_DV, GLA_HEADS
    heads = lambda t, width: _stack([t[:, h * width:(h + 1) * width] for h in range(nh)])
    logit = _mm3(low, w_gate_up) + b_gate
    glog_all = -_softplus(-logit) * (1.0 / GLA_GATE_NORM) * rows
    glog = heads(glog_all, dk)
    bcum = heads(_cumsum_rows(glog_all), dk)
    qh = heads(q, dk) * dk ** -0.5
    kh = heads(k, dk)
    vh = heads(v, dv)
    q_dec = qh * jnp.exp(bcum)
    attn = _bmm_nt(q_dec, kh * jnp.exp(-bcum)) * tri_incl
    b_last = jnp.sum(glog, axis=1, keepdims=True)
    k_dec = kh * jnp.exp(b_last - bcum)
    r = lax.broadcasted_iota(jnp.int32, (dk, dk), 0)
    c = lax.broadcasted_iota(jnp.int32, (dk, dk), 1)
    b_last_col = jnp.sum((r == c).astype(F32) * b_last, axis=2, keepdims=True)
    o = _bmm(attn, vh) + _bmm(q_dec, s)
    s_new = s * jnp.exp(b_last_col) + _bmm_tn(k_dec, vh)
    out = _rms(o, norm_w) * _silu(heads(gate, dv))
    return jnp.concatenate([out[h] for h in range(nh)], axis=1), s_new


LOW_COL = 24


def _gla_in_specs(step):
    return [pl.BlockSpec((CHUNK, 4 * LANE), lambda i: (step(i), 0)),
            pl.BlockSpec((CHUNK, 4 * LANE), lambda i: (step(i), 1)),
            pl.BlockSpec((CHUNK, 8 * LANE), lambda i: (step(i), 1)),
            pl.BlockSpec((CHUNK, 8 * LANE), lambda i: (step(i), 2)),
            pl.BlockSpec((CHUNK, LANE), lambda i: (step(i), LOW_COL)),
            _full((LANE, 4 * LANE)), _full((1, 4 * LANE)), _full((1, GLA_DV))]


def gla_fwd(proj, w_gate_up, b_gate, norm_w, *, name):
    tp = proj.shape[0]
    nc = tp // CHUNK

    def body(q_ref, k_ref, v_ref, g_ref, low_ref, wgu_ref, bg_ref, nw_ref, o_ref, ssave_ref, state):
        n = pl.program_id(0)

        @pl.when(n == 0)
        def _():
            state[...] = jnp.zeros_like(state)

        ssave_ref[0] = state[...]
        out, s_new = _gla_chunk(q_ref[...], k_ref[...], v_ref[...], g_ref[...], low_ref[...], state[...], wgu_ref[...],
                                bg_ref[...], nw_ref[...], _chunk_rows(n))
        o_ref[...] = out
        state[...] = s_new

    return pl.pallas_call(
        body, name=name, grid=(nc,),
        in_specs=_gla_in_specs(lambda i: i),
        out_specs=(pl.BlockSpec((CHUNK, 8 * LANE), lambda n: (n, 0)),
                   pl.BlockSpec((1, GLA_HEADS, GLA_DK, GLA_DV), lambda n: (n, 0, 0, 0))),
        out_shape=(jax.ShapeDtypeStruct((tp, 8 * LANE), F32),
                   jax.ShapeDtypeStruct((nc, GLA_HEADS, GLA_DK, GLA_DV), F32)),
        scratch_shapes=[pltpu.VMEM((GLA_HEADS, GLA_DK, GLA_DV), F32)],
    )(proj, proj, proj, proj, proj, w_gate_up, b_gate, norm_w)


def gla_bwd(proj, w_gate_up, b_gate, norm_w, ssave, do, *, name):
    tp = proj.shape[0]
    nc = tp // CHUNK
    rev = lambda i: nc - 1 - i

    def body(q_ref, k_ref, v_ref, g_ref, low_ref, wgu_ref, bg_ref, nw_ref, ss_ref, do_ref,
             dq_ref, dk_ref, dv_ref, dg_ref, dlow_ref, dwgu_ref, dbg_ref, dnw_ref, dstate):
        i = pl.program_id(0)
        n = nc - 1 - i

        @pl.when(i == 0)
        def _():
            dstate[...] = jnp.zeros_like(dstate)
            dwgu_ref[...] = jnp.zeros_like(dwgu_ref)
            dbg_ref[...] = jnp.zeros_like(dbg_ref)
            dnw_ref[...] = jnp.zeros_like(dnw_ref)

        rows = _chunk_rows(n)
        fn = lambda *a: _gla_chunk(*a, rows)
        _, vjp = jax.vjp(fn, q_ref[...], k_ref[...], v_ref[...], g_ref[...], low_ref[...], ss_ref[0], wgu_ref[...],
                         bg_ref[...], nw_ref[...])
        dq, dk, dv, dg, dlow, ds, dwgu, dbg, dnw = vjp((do_ref[...], dstate[...]))
        dq_ref[...] = dq
        dk_ref[...] = dk
        dv_ref[...] = dv
        dg_ref[...] = dg
        dlow_ref[...] = dlow
        dstate[...] = ds
        dwgu_ref[...] += dwgu
        dbg_ref[...] += dbg
        dnw_ref[...] += dnw

    chunk = lambda width: pl.BlockSpec((CHUNK, width), lambda i: (rev(i), 0))
    return pl.pallas_call(
        body, name=name, grid=(nc,),
        in_specs=_gla_in_specs(rev) + [pl.BlockSpec((1, GLA_HEADS, GLA_DK, GLA_DV), lambda i: (rev(i), 0, 0, 0)),
                                       chunk(8 * LANE)],
        out_specs=(chunk(4 * LANE), chunk(4 * LANE), chunk(8 * LANE), chunk(8 * LANE), chunk(LANE),
                   _full((LANE, 4 * LANE)), _full((1, 4 * LANE)), _full((1, GLA_DV))),
        out_shape=(jax.ShapeDtypeStruct((tp, 4 * LANE), F32), jax.ShapeDtypeStruct((tp, 4 * LANE), F32),
                   jax.ShapeDtypeStruct((tp, 8 * LANE), F32), jax.ShapeDtypeStruct((tp, 8 * LANE), F32),
                   jax.ShapeDtypeStruct((tp, LANE), F32), jax.ShapeDtypeStruct((LANE, 4 * LANE), F32),
                   jax.ShapeDtypeStruct((1, 4 * LANE), F32), jax.ShapeDtypeStruct((1, GLA_DV), F32)),
        scratch_shapes=[pltpu.VMEM((GLA_HEADS, GLA_DK, GLA_DV), F32)],
    )(proj, proj, proj, proj, proj, w_gate_up, b_gate, norm_w, ssave, do)


def _even_proj_weight(w_t):
    hd = SWA_HEAD_DIM
    k0, k1 = w_t[512:512 + hd], w_t[512 + hd:640]
    v0, v1 = w_t[640:640 + hd], w_t[640 + hd:768]
    zeros = jnp.zeros((LANE - 2 * DN_HEADS, w_t.shape[1]), w_t.dtype)
    return jnp.concatenate([w_t[:512], k0, k0, k1, k1, v0, v0, v1, v1, w_t[768:2816], w_t[2820:2824], w_t[2816:2820],
                            zeros], axis=0)


def _even_proj_weight_grad(dw):
    hd = SWA_HEAD_DIM
    c = lambda i: dw[512 + i * hd:512 + (i + 1) * hd]
    return jnp.concatenate([dw[:512], c(0) + c(1), c(2) + c(3), c(4) + c(5), c(6) + c(7), dw[1024:3072],
                            dw[3076:3080], dw[3072:3076]], axis=0)


def _ffn_fwd(h, nw_in, nw_out, wts, idx, get_w):
    wts.update(get_w(f"ffn{idx}", h))
    w_gu = wts[f"w_gu{idx}"]
    hn, g, u, a = rms_mm(h, nw_in, w_gu[0], swiglu=True, name=f"ffn_up_{idx}", widx=w_gu[1])
    wts.update(get_w(f"down{idx}", a))
    w_down = wts[f"w_down{idx}"]
    f, h_out = mm_rms_res([a], w_down[0], h, nw_out, scale=0.5, name=f"ffn_down_{idx}", widx=w_down[1])
    return h_out, (h, hn, g, u, a, f)


def _ffn_bwd(dho, saved, nw_in, nw_out, w_gu, w_down, idx, on_grads):
    h, hn, g, u, a, f = saved
    df, dnw_out, dgu = mm_rms_res_bwd(dho, f, nw_out, w_down[0], (g, u), scale=0.5, name=f"ffn_down_bwd_{idx}",
                                      widx=w_down[1])
    g_down = mm_tn(a, df, name=f"ffn_dwd_{idx}", out_dtype=BF16)
    sent = on_grads("down", g_down)
    g_gu = mm_tn(dgu, hn, name=f"ffn_dwgu_{idx}", out_dtype=BF16, after=sent)
    sent = on_grads("gu", g_gu)
    dh, dnw_in = rms_mm_bwd([dgu], w_gu[0], h, nw_in + sent, dho, name=f"ffn_up_bwd_{idx}", widx=w_gu[1])
    return dh, dnw_in, dnw_out


def local_step(x, target, wts, get_w=None, put_g=None):
    seq, d = x.shape
    wts = dict(wts)
    get_w = get_w or (lambda stage, after: {})
    put_g = put_g or (lambda stage, grads: jnp.zeros((1, 1), F32))
    row = lambda v: v.reshape(1, -1)
    lane_row = lambda v: jnp.pad(v.reshape(1, -1), ((0, 0), (0, LANE - v.size)))
    nw = wts["norm_w"]
    h = jnp.concatenate([jnp.zeros((PAD, d), F32), wts["meta_tokens"], x], axis=0)
    buckets = _swa_buckets()
    bias = swa_bias(wts["rel_bias_table"], buckets, name="swa_bias")
    sinks = lane_row(wts["swa_sinks"])
    a_log, dt_bias = lane_row(wts["dn_a_log"]), lane_row(wts["dn_dt_bias"])
    dn_norm_w = row(wts["dn_norm_w"])
    conv_w = wts["even_conv_w"][0]
    w_gate_up = jnp.pad(wts["gla_w_gate_up"][0], ((0, LANE - GLA_GATE_RANK), (0, 0)))
    b_gate, gla_norm_w = row(wts["gla_b_gate"]), row(wts["gla_norm_w"])

    saved = []
    w_in, w_out = [None, None], [None, None]
    for l in range(2):
        h, s_a = _ffn_fwd(h, row(nw[l, 0]), row(nw[l, 1]), wts, 2 * l, get_w)
        if l == 0:
            wts.update(get_w("even", h))
            w_in[0], w_out[0] = _even_proj_weight(wts["even_w_in"]), wts["even_w_out"]
        else:
            wts.update(get_w("odd", h))
            w_in[1] = jnp.pad(wts["odd_w_in"], ((0, PROJ_DIM - wts["odd_w_in"].shape[0]), (0, 0)))
            w_out[1] = wts["odd_w_out"]
        h_mix = h
        hn, proj = rms_mm(h, row(nw[l, 2]), w_in[l], swiglu=False, name=f"mix_in_{l}")
        if l == 0:
            o_a = swa_fwd(proj, bias, sinks, name="swa_fwd")
            y = conv_fwd(proj, conv_w, name="conv_fwd")
            o_b, ssave, isave = dn_fwd(y, proj, a_log, dt_bias, dn_norm_w, name="dn_fwd")
            acts, extra = [o_a, o_b], (y, ssave, isave)
        else:
            o, ssave = gla_fwd(proj, w_gate_up, b_gate, gla_norm_w, name="gla_fwd")
            acts, extra = [o], (ssave,)
        mix, h = mm_rms_res(acts, w_out[l], h, row(nw[l, 3]), scale=1.0, name=f"mix_out_{l}")
        s_m = (h_mix, hn, proj, acts, extra, mix)
        h, s_b = _ffn_fwd(h, row(nw[l, 4]), row(nw[l, 5]), wts, 2 * l + 1, get_w)
        saved.append((s_a, s_m, s_b))

    dh, loss = loss_and_grad(h, target, name="loss")

    grads = {}
    dnw = [[None] * 6 for _ in range(2)]
    def on_grads(i):
        def put(which, g):
            grads[f"g_{which}{i}"] = g
            return put_g(f"{which}{i}", grads)
        return put

    for l in (1, 0):
        s_a, s_m, s_b = saved[l]
        i = 2 * l + 1
        dh, dnw[l][4], dnw[l][5] = _ffn_bwd(dh, s_b, row(nw[l, 4]), row(nw[l, 5]), wts[f"w_gu{i}"], wts[f"w_down{i}"],
                                            i, on_grads(i))
        h_mix, hn, proj, acts, extra, mix = s_m
        dmix, dnw[l][3], do = mm_rms_res_bwd(dh, mix, row(nw[l, 3]), w_out[l], None, scale=1.0, name=f"mix_out_bwd_{l}")
        dw_out = jnp.concatenate([mm_tn(a, dmix, name=f"mix_dwo_{l}_{i}") for i, a in enumerate(acts)], axis=0)
        sent = jnp.zeros((1, 1), F32)
        if l == 0:
            y, ssave, isave = extra
            dq, dkv, dbias, dsinks = swa_bwd(proj, bias, sinks, do, name="swa_bwd")
            dy, dz, dsmall, da_log, ddt_bias, ddn_norm_w = dn_bwd(y, proj, a_log, dt_bias, dn_norm_w, ssave, isave, do,
                                                                    name="dn_bwd")
            dxc, dconv_w = conv_bwd(proj, conv_w, dy, name="conv_bwd")
            dps = [dq, dkv, dxc, dz, dsmall]
            grads["rel_bias_table"] = swa_bias_bwd(dbias, buckets, name="swa_bias_bwd")[:, :SWA_Q_HEADS]
            grads["swa_sinks"] = dsinks[:, :SWA_Q_HEADS]
            grads["dn_a_log"] = da_log[:, :DN_HEADS]
            grads["dn_dt_bias"] = ddt_bias[:, :DN_HEADS]
            grads["dn_norm_w"] = ddn_norm_w
            grads["even_conv_w"] = dconv_w[None]
            grads["even_w_out"] = dw_out
        else:
            (ssave,) = extra
            dq, dk, dv, dgate, dlow, dwgu, dbg, dgnw = gla_bwd(proj, w_gate_up, b_gate, gla_norm_w, ssave, do,
                                                               name="gla_bwd")
            dps = [dq, dk, dv, dgate, dlow]
            grads["gla_w_gate_up"] = dwgu[None, :GLA_GATE_RANK]
            grads["gla_b_gate"] = dbg
            grads["gla_norm_w"] = dgnw
            grads["odd_w_out"] = dw_out
        dw_in = jnp.concatenate([mm_tn(dp, hn, name=f"mix_dwi_{l}_{i}") for i, dp in enumerate(dps)], axis=0)
        if l == 0:
            grads["even_w_in"] = _even_proj_weight_grad(dw_in)
            sent = put_g("even", grads)
        else:
            grads["odd_w_in"] = dw_in[:wts["odd_w_in"].shape[0]]
        dh, dnw[l][2] = rms_mm_bwd(dps, w_in[l], h_mix, row(nw[l, 2]) + sent, dh, name=f"mix_in_bwd_{l}")
        i = 2 * l
        dh, dnw[l][0], dnw[l][1] = _ffn_bwd(dh, s_a, row(nw[l, 0]), row(nw[l, 1]), wts[f"w_gu{i}"], wts[f"w_down{i}"],
                                            i, on_grads(i))

    grads["norm_w"] = jnp.stack([jnp.concatenate(r, axis=0) for r in dnw])
    grads["meta_tokens"] = dh[PAD:PAD + N_META]
    return loss[0, 0], dh[PAD + N_META:], grads


def _peer(k):
    x, y, c = (lax.axis_index(a) for a in AXES)
    flip = lambda v, bit: 1 - v if bit else v
    return (flip(x, k & 4), flip(y, k & 2), flip(c, k & 1))


def _my_index():
    x, y, c = (lax.axis_index(a) for a in AXES)
    return 4 * x + 2 * y + c


_HBM = pl.BlockSpec(memory_space=pltpu.HBM)
_SEM = pl.BlockSpec(memory_space=pltpu.SEMAPHORE)
_EFFECT = pltpu.SideEffectType.DATAFLOW_SIDE_EFFECTING


def _remote_copies(items, src_refs, land_refs, send_sems, recv_sems):
    me = _my_index()
    copies = []
    for k in range(1, N_DEV):
        px, py, pc = _peer(k)
        pj = 4 * px + 2 * py + pc
        for a, (sn, send, ln, land, _) in enumerate(items):
            sem = (k - 1) * len(items) + a
            copies.append(pltpu.make_async_remote_copy(
                src_ref=send(src_refs[sn], pj), dst_ref=land(land_refs[ln], me), send_sem=send_sems.at[sem],
                recv_sem=recv_sems.at[sem], device_id=(px, py, pc), device_id_type=MESH))
    return copies


def exchange(srcs, lands, items, after, *, name):
    sn, ln = list(srcs), list(lands)

    def body(*refs):
        src_refs = dict(zip(sn, refs[:len(sn)]))
        land_refs = dict(zip(ln, refs[len(sn) + len(ln) + 1:len(sn) + 2 * len(ln) + 1]))
        send_sems, recv_sems = refs[len(sn) + 2 * len(ln) + 1:]
        copies = _remote_copies(items, src_refs, land_refs, send_sems, recv_sems)
        for cp in copies:
            cp.start()
        for cp in copies:
            cp.wait_recv()
        for cp in copies:
            cp.wait_send()

    n_remote = (N_DEV - 1) * len(items)
    outs = pl.pallas_call(
        body, name=name,
        in_specs=[pl.BlockSpec(memory_space=pl.ANY)] * (len(sn) + len(ln) + 1),
        out_specs=tuple(pl.BlockSpec(memory_space=pl.ANY) for _ in ln),
        out_shape=tuple(jax.ShapeDtypeStruct(lands[n].shape, lands[n].dtype) for n in ln),
        input_output_aliases={len(sn) + i: i for i in range(len(ln))},
        scratch_shapes=[pltpu.SemaphoreType.DMA((n_remote,)), pltpu.SemaphoreType.DMA((n_remote,))],
    )(*[srcs[n] for n in sn], *[lands[n] for n in ln], after)
    return dict(zip(ln, outs))


def start_copies(srcs, lands, items, *, name):
    sn, ln = list(srcs), list(lands)
    n_remote = (N_DEV - 1) * len(items)

    def body(*refs):
        src_refs = dict(zip(sn, refs[:len(sn)]))
        land_refs = dict(zip(ln, refs[len(sn):len(sn) + len(ln)]))
        send_sems, recv_sems = refs[len(sn) + len(ln):len(sn) + len(ln) + 2]
        token = refs[-1]
        for cp in _remote_copies(items, src_refs, land_refs, send_sems, recv_sems):
            cp.start()
        token[...] = jnp.zeros_like(token)

    hbm = lambda a: pltpu.with_memory_space_constraint(a, pltpu.HBM)
    outs = pl.pallas_call(
        body, name=name,
        in_specs=[_HBM] * (len(sn) + len(ln)),
        out_specs=(_SEM, _SEM) + (_HBM,) * len(ln) + (pl.BlockSpec(memory_space=pltpu.VMEM),),
        out_shape=(pltpu.SemaphoreType.DMA((n_remote,)), pltpu.SemaphoreType.DMA((n_remote,)))
        + tuple(pltpu.HBM(lands[n].shape, lands[n].dtype) for n in ln) + (jax.ShapeDtypeStruct((8, LANE), F32),),
        input_output_aliases={len(sn) + i: 2 + i for i in range(len(ln))},
        compiler_params=pltpu.CompilerParams(has_side_effects=_EFFECT),
    )(*[hbm(srcs[n]) for n in sn], *[hbm(lands[n]) for n in ln])
    return (outs[0], outs[1]), dict(zip(ln, outs[2:2 + len(ln)])), outs[-1][0:1, 0:1]


def wait_copies(sems, srcs, lands, items, after, *, name):
    sn, ln = list(srcs), list(lands)

    def body(*refs):
        src_refs = dict(zip(sn, refs[:len(sn)]))
        land_refs = dict(zip(ln, refs[len(sn):len(sn) + len(ln)]))
        send_sems, recv_sems = refs[len(sn) + len(ln):len(sn) + len(ln) + 2]
        copies = _remote_copies(items, src_refs, land_refs, send_sems, recv_sems)
        for cp in copies:
            cp.wait_send()
        for cp in copies:
            cp.wait_recv()

    outs = pl.pallas_call(
        body, name=name,
        in_specs=[_HBM] * (len(sn) + len(ln)) + [_SEM, _SEM, pl.BlockSpec(memory_space=pl.ANY)],
        out_specs=(_HBM,) * len(ln),
        out_shape=tuple(pltpu.HBM(lands[n].shape, lands[n].dtype) for n in ln),
        input_output_aliases={len(sn) + i: i for i in range(len(ln))},
        compiler_params=pltpu.CompilerParams(has_side_effects=_EFFECT),
    )(*[srcs[n] for n in sn], *[lands[n] for n in ln], sems[0], sems[1], after)
    return dict(zip(ln, outs))


def _block(index, size, base=0):
    return pl.ds(pl.multiple_of(base + index * size, ROW_TILE), size)


def _adam_tile(rows):
    for t in (256, 176, 128):
        if rows % t == 0:
            return t
    return rows


def sum_adamw(recvs, w, m, v, *, name, first_slab=0, into=None):
    _, r, c = w.shape
    b = len(recvs)
    rp = recvs[0].shape[1]
    whole = r % ROW_TILE != 0
    tr = r if whole else _adam_tile(r)
    c1 = 1.0 / (1.0 - ADAM_B1 ** ADAM_STEP)
    c2 = 1.0 / (1.0 - ADAM_B2 ** ADAM_STEP)

    def body(*refs):
        recv_refs = refs[:b]
        w_ref, m_ref, v_ref = refs[b:b + 3]
        g_ref, d_ref, nm_ref, nv_ref = refs[b + 3 + (0 if into is None else 4):][:4]
        for slab, recv_ref in enumerate(recv_refs):
            @pl.when(pl.program_id(0) == slab)
            def _():
                g = recv_ref[0].astype(F32)
                for i in range(1, N_DEV):
                    g = g + recv_ref[i].astype(F32)
                if whole:
                    sum_ref = refs[-1]
                    sum_ref[...] = g
                    g = sum_ref[0:r, :]
                nm = ADAM_B1 * m_ref[0] + (1.0 - ADAM_B1) * g
                nv = ADAM_B2 * v_ref[0] + (1.0 - ADAM_B2) * (g * g)
                g_ref[0] = g
                nm_ref[0] = nm
                nv_ref[0] = nv
                d_ref[0] = -ADAM_LR * ((nm * c1) / (jnp.sqrt(nv * c2) + ADAM_EPS) + ADAM_WD * w_ref[0])

    tile = pl.BlockSpec((1, tr, c), lambda bi, i: (first_slab + bi, i, 0))
    piece = lambda slab: pl.BlockSpec((N_DEV, rp if whole else tr, c), lambda bi, i: (0, jnp.where(bi == slab, i, 0), 0))
    earlier = [] if into is None else list(into)
    return pl.pallas_call(
        body, name=name, grid=(b, r // tr),
        in_specs=[piece(slab) for slab in range(b)] + [tile, tile, tile] + [pl.BlockSpec(memory_space=pl.ANY)] * len(earlier),
        out_specs=(tile,) * 4, out_shape=(jax.ShapeDtypeStruct(w.shape, F32),) * 4,
        input_output_aliases={b + 3 + i: i for i in range(len(earlier))},
        scratch_shapes=[pltpu.VMEM((rp, c), F32)] if whole else [],
    )(*recvs, w, m, v, *earlier)


def _flat_rows(n_elems, row_multiple):
    rows = -(-n_elems // FLAT_COLS)
    return -(-rows // row_multiple) * row_multiple


def _pack(arrays, row_multiple, dtype):
    flat = jnp.concatenate([a.reshape(-1).astype(dtype) for a in arrays])
    rows = _flat_rows(flat.size, row_multiple)
    return jnp.pad(flat, (0, rows * FLAT_COLS - flat.size)).reshape(rows, FLAT_COLS)


def _unpack(flat2d, shapes):
    lead = flat2d.shape[:-2]
    flat = flat2d.reshape(lead + (-1,))
    out, off = [], 0
    for shp in shapes:
        n = int(np.prod(shp))
        out.append(flat[..., off:off + n].reshape(lead + tuple(shp)))
        off += n
    return out


def _join_shards(stacked, axis):
    moved = jnp.moveaxis(stacked, 0, axis)
    shp = list(moved.shape)
    shp[axis:axis + 2] = [shp[axis] * shp[axis + 1]]
    return moved.reshape(shp)


def _split_shards(full, axis):
    shp = list(full.shape)
    shp[axis:axis + 1] = [N_DEV, shp[axis] // N_DEV]
    return jnp.moveaxis(full.reshape(shp), axis, 0)


def kernel(x, meta_tokens, norm_w, ffn_w_gate, ffn_w_up, ffn_w_down, rel_bias_table, even_w_in, even_conv_w, swa_sinks, dn_a_log, dn_dt_bias, dn_norm_w, even_w_out, odd_w_in, gla_w_gate_up, gla_b_gate, gla_norm_w, odd_w_out, loss_target, m_meta_tokens, m_norm_w, m_ffn_w_gate, m_ffn_w_up, m_ffn_w_down, m_rel_bias_table, m_even_w_in, m_even_conv_w, m_swa_sinks, m_dn_a_log, m_dn_dt_bias, m_dn_norm_w, m_even_w_out, m_odd_w_in, m_gla_w_gate_up, m_gla_b_gate, m_gla_norm_w, m_odd_w_out, v_meta_tokens, v_norm_w, v_ffn_w_gate, v_ffn_w_up, v_ffn_w_down, v_rel_bias_table, v_even_w_in, v_even_conv_w, v_swa_sinks, v_dn_a_log, v_dn_dt_bias, v_dn_norm_w, v_even_w_out, v_odd_w_in, v_gla_w_gate_up, v_gla_b_gate, v_gla_norm_w, v_odd_w_out):
    args = locals()
    w = {n: args[n] for n in WEIGHTS}
    m = {n: args["m_" + n] for n in WEIGHTS}
    v = {n: args["v_" + n] for n in WEIGHTS}

    d = D_MODEL
    me = _my_index()
    whole = lambda ref, j: ref
    rows = lambda size, base=0: (lambda ref, i: ref.at[_block(i, size, base), :])
    lead = lambda ref, i: ref.at[i]
    of_group = lambda items, g: [it for it in items if it[4] == g]
    names = lambda items, k: list(dict.fromkeys(it[k] for it in items))

    def placed(shape, dtype, parts):
        land = lax.empty(shape, dtype)
        for part, axis, start in parts:
            land = lax.dynamic_update_slice(land, part, tuple(start if a == axis else 0 for a in range(land.ndim)))
        return land

    as_rows = lambda a: jnp.swapaxes(a, -1, -2)
    pad_rows = lambda a, to: jnp.pad(a, [(0, 0)] * (a.ndim - 2) + [(0, to - a.shape[-2]), (0, 0)])
    gate_s = pad_rows(as_rows(w["ffn_w_gate"].reshape(N_FFN, d, FF_SHARD)), FF_SHARD_PAD).astype(BF16)
    up_s = pad_rows(as_rows(w["ffn_w_up"].reshape(N_FFN, d, FF_SHARD)), FF_SHARD_PAD).astype(BF16)
    down_s = pad_rows(w["ffn_w_down"].reshape(N_FFN, FF_SHARD, d), FF_SHARD_PAD).astype(BF16)
    small_s = _pack([w[n] for n in SMALL], 8, F32)
    srcs_w = {"ein": pad_rows(as_rows(w["even_w_in"][0]), EVEN_IN_SHARD_PAD).astype(BF16),
              "oin": pad_rows(as_rows(w["odd_w_in"][0]), ODD_IN_SHARD_PAD).astype(BF16),
              "eout": w["even_w_out"][0].astype(BF16), "oout": w["odd_w_out"][0].astype(BF16), "small": small_s}
    lands_w = {"ein": placed((N_DEV * EVEN_IN_SHARD_PAD, d), BF16, [(srcs_w["ein"], 0, me * EVEN_IN_SHARD_PAD)]),
               "oin": placed((N_DEV * ODD_IN_SHARD_PAD, d), BF16, [(srcs_w["oin"], 0, me * ODD_IN_SHARD_PAD)]),
               "eout": placed((d, d), BF16, [(srcs_w["eout"], 0, me * OUT_SHARD)]),
               "oout": placed((d, d), BF16, [(srcs_w["oout"], 0, me * OUT_SHARD)]),
               "small": placed((N_DEV,) + small_s.shape, F32, [(small_s[None], 0, me)])}
    items_w = [("small", whole, "small", lead, "first"), ("ein", whole, "ein", rows(EVEN_IN_SHARD_PAD), "even"),
               ("eout", whole, "eout", rows(OUT_SHARD), "even"), ("oin", whole, "oin", rows(ODD_IN_SHARD_PAD), "odd"),
               ("oout", whole, "oout", rows(OUT_SHARD), "odd")]
    for i, group, down_group in ((0, "first", "down0"), (1, "ffn1", "ffn1"), (2, "ffn2", "down2"), (3, "ffn3", "ffn3")):
        srcs_w.update({f"gate{i}": gate_s[i], f"up{i}": up_s[i], f"down{i}": down_s[i]})
        lands_w[f"w_gu{i}"] = placed((2 * FF_PAD, d), BF16, [(srcs_w[f"gate{i}"], 0, me * FF_SHARD_PAD),
                                                             (srcs_w[f"up{i}"], 0, FF_PAD + me * FF_SHARD_PAD)])
        lands_w[f"w_down{i}"] = placed((FF_PAD, d), BF16, [(srcs_w[f"down{i}"], 0, me * FF_SHARD_PAD)])
        items_w += [(f"gate{i}", whole, f"w_gu{i}", rows(FF_SHARD_PAD), group),
                    (f"up{i}", whole, f"w_gu{i}", rows(FF_SHARD_PAD, FF_PAD), group),
                    (f"down{i}", whole, f"w_down{i}", rows(FF_SHARD_PAD), down_group)]
    pending, started = {}, []
    for g in ("first", "down0", "even", "ffn1", "ffn2", "down2", "odd", "ffn3"):
        its = of_group(items_w, g)
        srcs = {n: srcs_w[n] for n in names(its, 0)}
        sems, lands, token = start_copies(srcs, {n: lands_w[n] for n in names(its, 2)}, its, name=f"gather_start_{g}")
        pending[g] = (sems, srcs, lands, its)
        started.append(token)

    unpad = lambda p, shard, shard_pad: p.reshape(N_DEV, shard_pad, d)[:, :shard].reshape(N_DEV * shard, d)

    def get_w(stage, after):
        if stage not in pending:
            return {}
        sems, srcs, lands, its = pending[stage]
        landed = wait_copies(sems, srcs, lands, its, after, name=f"gather_wait_{stage}")
        got = {}
        for n, arr in landed.items():
            if n == "small":
                for sn, stacked in zip(SMALL, _unpack(arr, [w[sn].shape for sn in SMALL])):
                    got[sn] = _join_shards(stacked, SHARD_AXIS[sn])
            elif n == "ein":
                got["even_w_in"] = unpad(arr, EVEN_IN_SHARD, EVEN_IN_SHARD_PAD)
            elif n == "oin":
                got["odd_w_in"] = unpad(arr, ODD_IN_SHARD, ODD_IN_SHARD_PAD)
            elif n in ("eout", "oout"):
                got["even_w_out" if n == "eout" else "odd_w_out"] = arr
            else:
                got[n] = (arr, None)
        return got

    full = {n: w[n] for n in REPL}
    full.update(get_w("first", sum(started)))

    repad = lambda g, shard, shard_pad: pad_rows(g.reshape(N_DEV, shard, d), shard_pad)
    pieces_g = {"r_oin": ("oin", None, "gu2"), "r_oout": ("oout", (OUT_SHARD, 0), "gu2"),
                "r_ein": ("ein", None, "even"), "r_eout": ("eout", (OUT_SHARD, 0), "even"), "r_small": ("small", None, "last")}
    for i in range(N_FFN):
        pieces_g.update({f"r_gate{i}": (f"g_gu{i}", (FF_SHARD_PAD, 0), f"gu{i}"),
                         f"r_up{i}": (f"g_gu{i}", (FF_SHARD_PAD, FF_PAD), f"gu{i}"),
                         f"r_down{i}": (f"g_down{i}", (FF_SHARD_PAD, 0), "down0" if i == 0 else f"gu{i}")})
    items_g = [(src, lead if blk is None else rows(*blk), land, lead, group) for land, (src, blk, group) in pieces_g.items()]
    last_groups = ("down0", "gu0")

    def grad_src(n, grads):
        if n == "oin":
            return repad(grads["odd_w_in"], ODD_IN_SHARD, ODD_IN_SHARD_PAD).astype(BF16)
        if n == "ein":
            return repad(grads["even_w_in"], EVEN_IN_SHARD, EVEN_IN_SHARD_PAD).astype(BF16)
        if n in ("oout", "eout"):
            return grads["odd_w_out" if n == "oout" else "even_w_out"].astype(BF16)
        return grads[n]

    def grad_land(n, srcs):
        src, blk, _ = pieces_g[n]
        if blk is None:
            own = lax.dynamic_index_in_dim(srcs[src], me, 0, keepdims=False)
        else:
            own = lax.dynamic_slice_in_dim(srcs[src], blk[1] + me * blk[0], blk[0], 0)
        return placed((N_DEV,) + own.shape, own.dtype, [(own[None], 0, me)])

    sent = {}

    def put_g(stage, grads):
        its = of_group(items_g, stage)
        if not its:
            return jnp.zeros((1, 1), F32)
        srcs = {n: grad_src(n, grads) for n in names(its, 0)}
        lands = {n: grad_land(n, srcs) for n in names(its, 2)}
        sems, lands, token = start_copies(srcs, lands, its, name=f"grads_start_{stage}")
        sent[stage] = (sems, srcs, lands, its)
        return token

    loss, grad_x, grads = local_step(x[0], loss_target[0], full, get_w, put_g)
    loss = lax.psum(loss, AXES)

    order = SMALL + REPL
    pieces = [_split_shards(grads[n].reshape(full[n].shape), SHARD_AXIS[n]) if n in SHARD_AXIS
              else jnp.broadcast_to(grads[n].reshape(w[n].shape)[None], (N_DEV,) + w[n].shape) for n in order]
    flat = jnp.concatenate([p.reshape(N_DEV, -1) for p in pieces], axis=1)
    srows = _flat_rows(flat.shape[1], 8)
    grads["small"] = jnp.pad(flat, ((0, 0), (0, srows * FLAT_COLS - flat.shape[1]))).reshape(N_DEV, srows, FLAT_COLS)
    recv = {}
    for stage, (sems, srcs, lands, its) in sent.items():
        if stage not in last_groups:
            recv.update(wait_copies(sems, srcs, lands, its, grad_x, name=f"grads_wait_{stage}"))
    result = [{} for _ in range(4)]

    views = {"ffn_w_gate": (lambda a: as_rows(a.reshape(N_FFN, d, FF_SHARD)), lambda o, n: as_rows(o).reshape(w[n].shape)),
             "ffn_w_up": (lambda a: as_rows(a.reshape(N_FFN, d, FF_SHARD)), lambda o, n: as_rows(o).reshape(w[n].shape)),
             "ffn_w_down": (lambda a: a.reshape(N_FFN, FF_SHARD, d), lambda o, n: o.reshape(w[n].shape)),
             "even_w_in": (as_rows, lambda o, n: as_rows(o)), "odd_w_in": (as_rows, lambda o, n: as_rows(o)),
             "even_w_out": (lambda a: a, lambda o, n: o), "odd_w_out": (lambda a: a, lambda o, n: o)}

    def adam(n, recvs, first_slab=0, into=None):
        view = views[n][0]
        return sum_adamw(recvs, view(w[n]), view(m[n]), view(v[n]), name=f"adamw_{n}_{first_slab}",
                         first_slab=first_slab, into=into)

    def finish(n, outs):
        for r, o in zip(result, outs):
            r[n] = views[n][1](o, n)

    ffn_recv = (("ffn_w_gate", "r_gate"), ("ffn_w_up", "r_up"), ("ffn_w_down", "r_down"))
    early = {n: adam(n, [recv[f"{r}{i}"] for i in (1, 2, 3)], first_slab=1) for n, r in ffn_recv}
    for n, r in (("even_w_in", "r_ein"), ("odd_w_in", "r_oin"), ("even_w_out", "r_eout"), ("odd_w_out", "r_oout")):
        finish(n, adam(n, [recv[r]]))
    srcs = {"small": grads["small"]}
    recv.update(exchange(srcs, {"r_small": grad_land("r_small", srcs)}, of_group(items_g, "last"),
                         early["ffn_w_down"][0], name="exchange_small"))
    for stage in last_groups:
        sems, srcs, lands, its = sent[stage]
        recv.update(wait_copies(sems, srcs, lands, its, recv["r_small"], name=f"grads_wait_{stage}"))
    for n, r in ffn_recv:
        finish(n, adam(n, [recv[f"{r}0"]], into=early[n]))
    pack_local = lambda t: _pack([t[n] for n in order], 8, F32)[None]
    small_outs = sum_adamw([recv["r_small"]], pack_local(w), pack_local(m), pack_local(v), name="adamw_small")
    for r, o in zip(result, small_outs):
        r.update(zip(order, _unpack(o[0], [w[n].shape for n in order])))
    return (loss, grad_x[None], *[r[n] for r in result for n in WEIGHTS])
```

```python
import functools
import math

import numpy as np
import jax
import jax.numpy as jnp
from jax import lax
from jax.experimental import pallas as pl
from jax.experimental.pallas import tpu as pltpu

F32 = jnp.float32
BF16 = jnp.bfloat16
MESH = pl.DeviceIdType.MESH
AXES = ("x", "y", "c")
N_DEV = 8

D_MODEL = 1024
N_META = 16
D_FF = 2816
NORM_EPS = 1e-6
NEG_INF = -1e30
SWA_Q_HEADS = 8
SWA_HEAD_DIM = 64
SWA_WINDOW = 128
SWA_BLOCK = 128
REL_BUCKETS = 32
REL_MAX_DIST = 128
DN_HEADS = 4
DN_HEAD_DIM = 128
DN_CONV = 4
GLA_HEADS = 4
GLA_DK = 128
GLA_DV = 256
GLA_GATE_RANK = 16
GLA_GATE_NORM = 16.0
CHUNK = 64
PAD = SWA_BLOCK - N_META
LANE = 128
PROJ_DIM = 3200

ADAM_LR = 0.001
ADAM_B1 = 0.9
ADAM_B2 = 0.999
ADAM_EPS = 1e-08
ADAM_WD = 0.01
ADAM_STEP = 10

FF_SHARD = D_FF // N_DEV
FF_SHARD_PAD = 384
FF_PAD = N_DEV * FF_SHARD_PAD
N_FFN = 4
ROW_TILE = 16
EVEN_IN_SHARD, EVEN_IN_SHARD_PAD = 353, 368
ODD_IN_SHARD, ODD_IN_SHARD_PAD = 386, 400
OUT_SHARD = D_MODEL // N_DEV

FLAT_COLS = 128
BIG = ("ffn_w_gate", "ffn_w_up", "ffn_w_down", "even_w_in", "even_w_out", "odd_w_in", "odd_w_out")
SMALL = ("meta_tokens", "norm_w", "even_conv_w", "gla_w_gate_up", "gla_b_gate", "gla_norm_w")
REPL = ("rel_bias_table", "swa_sinks", "dn_a_log", "dn_dt_bias", "dn_norm_w")
WEIGHTS = ("meta_tokens", "norm_w", "ffn_w_gate", "ffn_w_up", "ffn_w_down", "rel_bias_table", "even_w_in",
           "even_conv_w", "swa_sinks", "dn_a_log", "dn_dt_bias", "dn_norm_w", "even_w_out", "odd_w_in",
           "gla_w_gate_up", "gla_b_gate", "gla_norm_w", "odd_w_out")
SHARD_AXIS = {"ffn_w_gate": 3, "ffn_w_up": 3, "ffn_w_down": 2, "even_w_in": 2, "even_w_out": 1, "odd_w_in": 2,
              "odd_w_out": 1, "meta_tokens": 1, "norm_w": 2, "even_conv_w": 2, "gla_w_gate_up": 2,
              "gla_b_gate": 1, "gla_norm_w": 1}


def _rms(x, w):
    r = lax.rsqrt(jnp.mean(x * x, axis=-1, keepdims=True) + NORM_EPS)
    return x * r * w


def _sigmoid(x):
    return 0.5 * (jnp.tanh(0.5 * x) + 1.0)


def _silu(x):
    return x * _sigmoid(x)


def _softplus(x):
    pos = x > 0
    return jnp.where(pos, x, 0.0) + jnp.log(1.0 + jnp.exp(jnp.where(pos, -x, x)))


def _l2n(x):
    return x * lax.rsqrt(jnp.sum(x * x, axis=-1, keepdims=True) + 1e-6)


def _split_bf16(x):
    hi = x.astype(BF16)
    return hi, (x - hi.astype(F32)).astype(BF16)


def _make_mm(terms, batched):
    off = 1 if batched else 0
    bdims = ((0,), (0,)) if batched else ((), ())

    def dg(a, b, ca, cb):
        dot = lambda p, q: lax.dot_general(p, q, (((ca + off,), (cb + off,)), bdims), preferred_element_type=F32)
        a_hi, a_lo = _split_bf16(a)
        b_hi, b_lo = _split_bf16(b)
        if terms == 1:
            return dot(a_hi, b_hi)
        return dot(a_hi, b_hi) + (dot(a_hi, b_lo) + dot(a_lo, b_hi))

    @jax.custom_vjp
    def nn(a, b):
        return dg(a, b, 1, 0)

    @jax.custom_vjp
    def nt(a, b):
        return dg(a, b, 1, 1)

    @jax.custom_vjp
    def tn(a, b):
        return dg(a, b, 0, 0)

    nn.defvjp(lambda a, b: (nn(a, b), (a, b)), lambda r, g: (nt(g, r[1]), tn(r[0], g)))
    nt.defvjp(lambda a, b: (nt(a, b), (a, b)), lambda r, g: (nn(g, r[1]), tn(g, r[0])))
    tn.defvjp(lambda a, b: (tn(a, b), (a, b)), lambda r, g: (nt(r[1], g), nn(r[0], g)))
    return nn, nt, tn


_mm, _mm_nt, _mm_tn = _make_mm(1, False)
_mm3, _, _ = _make_mm(3, False)
_bmm, _bmm_nt, _bmm_tn = _make_mm(1, True)
_bmm3, _bmm3_nt, _bmm3_tn = _make_mm(3, True)


@jax.custom_vjp
def _known_inverse(a, inv):
    return inv


_known_inverse.defvjp(lambda a, inv: (inv, inv),
                      lambda inv, g: (-_bmm3_tn(inv, _bmm3_nt(g, inv)), jnp.zeros_like(inv)))


def _tri_ones_dot(x, lower):
    n = x.shape[0]
    r = lax.broadcasted_iota(jnp.int32, (n, n), 0)
    c = lax.broadcasted_iota(jnp.int32, (n, n), 1)
    t = ((r >= c) if lower else (r <= c)).astype(BF16)
    hi, lo = _split_bf16(x)
    return jnp.dot(t, hi, preferred_element_type=F32) + jnp.dot(t, lo, preferred_element_type=F32)


@jax.custom_vjp
def _cumsum_rows(x):
    return _tri_ones_dot(x, True)


_cumsum_rows.defvjp(lambda x: (_tri_ones_dot(x, True), None), lambda _, g: (_tri_ones_dot(g, False),))


def _row_tile(n_rows, cap):
    best = LANE
    for t in range(LANE, cap + 1, LANE):
        if n_rows % t == 0:
            best = t
    return best


def _real_rows(tile_index, tm):
    row = tile_index * tm + lax.broadcasted_iota(jnp.int32, (tm, 1), 0)
    return (row >= PAD).astype(F32)


def _full(shape):
    return pl.BlockSpec(shape, lambda *_: (0,) * len(shape))


def _resident(shape):
    return pl.BlockSpec(shape, lambda *_: (0,) * len(shape), pipeline_mode=pl.Buffered(1))


def _resident_w(wmat, widx):
    if wmat.ndim == 2:
        return _resident(wmat.shape)
    return pl.BlockSpec((None,) + wmat.shape[1:], lambda *_: (widx, 0, 0), pipeline_mode=pl.Buffered(1))


def rms_mm(h, w, wmat_t, *, swiglu, name, widx=None):
    tp, d = h.shape
    n = wmat_t.shape[-2]
    tm = _row_tile(tp, 384)
    half = n // 2
    wmat = wmat_t

    def body(h_ref, w_ref, wm_ref, hn_ref, *outs):
        hn = _rms(h_ref[...], w_ref[...]).astype(BF16)
        hn_ref[...] = hn
        p = lax.dot_general(hn, wm_ref[...], (((1,), (1,)), ((), ())), preferred_element_type=F32)
        if swiglu:
            g, u = p[:, :half], p[:, half:]
            outs[0][...] = g.astype(BF16)
            outs[1][...] = u.astype(BF16)
            outs[2][...] = (_silu(g) * u).astype(BF16)
        else:
            outs[0][...] = p

    row = lambda width: pl.BlockSpec((tm, width), lambda i: (i, 0))
    if swiglu:
        out_shape = (jax.ShapeDtypeStruct((tp, d), BF16),) + (jax.ShapeDtypeStruct((tp, half), BF16),) * 3
        out_specs = (row(d), row(half), row(half), row(half))
    else:
        out_shape = (jax.ShapeDtypeStruct((tp, d), BF16), jax.ShapeDtypeStruct((tp, n), F32))
        out_specs = (row(d), row(n))
    return pl.pallas_call(
        body, name=name, grid=(tp // tm,),
        in_specs=[row(d), _full((1, d)), _resident_w(wmat, widx)],
        out_specs=out_specs, out_shape=out_shape,
    )(h, w, wmat)


def mm_rms_res(acts, wmat, h, w, *, scale, name, widx=None):
    tp, d = h.shape
    tm = _row_tile(tp, 384)
    widths = [a.shape[1] for a in acts]
    offs = [sum(widths[:i]) for i in range(len(acts))]
    na = len(acts)

    def body(*refs):
        a_refs = refs[:na]
        wm_ref, h_ref, w_ref, f_ref, ho_ref = refs[na:]
        f = None
        for a_ref, off, width in zip(a_refs, offs, widths):
            part = jnp.dot(a_ref[...].astype(BF16), wm_ref[off:off + width, :], preferred_element_type=F32)
            f = part if f is None else f + part
        f_ref[...] = f
        ho_ref[...] = h_ref[...] + scale * _rms(f, w_ref[...])

    row = lambda width: pl.BlockSpec((tm, width), lambda i: (i, 0))
    return pl.pallas_call(
        body, name=name, grid=(tp // tm,),
        in_specs=[row(wd) for wd in widths] + [_resident_w(wmat, widx), row(d), _full((1, d))],
        out_specs=(row(d), row(d)),
        out_shape=(jax.ShapeDtypeStruct((tp, d), F32), jax.ShapeDtypeStruct((tp, d), F32)),
    )(*acts, wmat, h, w)


def mm_rms_res_bwd(dho, f, w, wmat, gu, *, scale, name, widx=None):
    tp, d = f.shape
    k = wmat.shape[-2]
    tm = _row_tile(tp, 384)
    swiglu = gu is not None

    def body(*refs):
        if swiglu:
            dho_ref, f_ref, w_ref, wm_ref, g_ref, u_ref, df_ref, dw_ref, dgu_ref = refs
        else:
            dho_ref, f_ref, w_ref, wm_ref, df_ref, dw_ref, da_ref = refs
        i = pl.program_id(0)
        _, vjp = jax.vjp(lambda ff, ww: scale * _rms(ff, ww), f_ref[...], w_ref[...])
        df, dw = vjp(dho_ref[...])
        dfb = (df * _real_rows(i, tm)).astype(BF16)
        df_ref[...] = dfb

        @pl.when(i == 0)
        def _():
            dw_ref[...] = jnp.zeros_like(dw_ref)

        dw_ref[...] += dw
        da = lax.dot_general(dfb, wm_ref[...], (((1,), (1,)), ((), ())), preferred_element_type=F32)
        if swiglu:
            g, u, dab = g_ref[...], u_ref[...], da.astype(BF16)
            s = _sigmoid(g)
            dgu_ref[:, :k] = dab * u * s * (1.0 + g * (1.0 - s))
            dgu_ref[:, k:] = dab * g * s
        else:
            da_ref[...] = da

    row = lambda width: pl.BlockSpec((tm, width), lambda i: (i, 0))
    in_specs = [row(d), row(d), _full((1, d)), _resident_w(wmat, widx)]
    args = [dho, f, w, wmat]
    out_shape = [jax.ShapeDtypeStruct((tp, d), BF16), jax.ShapeDtypeStruct((1, d), F32)]
    out_specs = [row(d), _full((1, d))]
    if swiglu:
        in_specs += [row(k), row(k)]
        args += list(gu)
        out_shape += [jax.ShapeDtypeStruct((tp, 2 * k), BF16)]
        out_specs += [row(2 * k)]
    else:
        out_shape += [jax.ShapeDtypeStruct((tp, k), F32)]
        out_specs += [row(k)]
    return pl.pallas_call(body, name=name, grid=(tp // tm,), in_specs=in_specs, out_specs=tuple(out_specs),
                          out_shape=tuple(out_shape))(*args)


def rms_mm_bwd(dps, wmat, h, w, dho, *, name, widx=None):
    tp, d = h.shape
    tm = _row_tile(tp, 384)
    widths = [p.shape[1] for p in dps]
    offs = [sum(widths[:i]) for i in range(len(dps))]
    ndp = len(dps)

    def body(*refs):
        dp_refs = refs[:ndp]
        wm_ref, h_ref, w_ref, dho_ref, dh_ref, dw_ref = refs[ndp:]
        i = pl.program_id(0)
        dhn = None
        for dp_ref, off, width in zip(dp_refs, offs, widths):
            part = jnp.dot(dp_ref[...].astype(BF16), wm_ref[off:off + width, :], preferred_element_type=F32)
            dhn = part if dhn is None else dhn + part
        _, vjp = jax.vjp(_rms, h_ref[...], w_ref[...])
        dx, dw = vjp(dhn)
        dh_ref[...] = (dho_ref[...] + dx) * _real_rows(i, tm)

        @pl.when(i == 0)
        def _():
            dw_ref[...] = jnp.zeros_like(dw_ref)

        dw_ref[...] += dw

    row = lambda width: pl.BlockSpec((tm, width), lambda i: (i, 0))
    return pl.pallas_call(
        body, name=name, grid=(tp // tm,),
        in_specs=[row(wd) for wd in widths] + [_resident_w(wmat, widx), row(d), _full((1, d)), row(d)],
        out_specs=(row(d), _full((1, d))),
        out_shape=(jax.ShapeDtypeStruct((tp, d), F32), jax.ShapeDtypeStruct((1, d), F32)),
    )(*dps, wmat, h, w, dho)


def mm_tn(a, b, *, name, out_dtype=F32, after=None):
    t, m = a.shape
    n = b.shape[1]
    bm = _row_tile(m, 1024 if n <= 1024 else 512)
    bn = _row_tile(n, 1536)
    bk = _row_tile(t, 1408)
    nk = t // bk
    ties = [] if after is None else [after]

    def body(a_ref, b_ref, *rest):
        o_ref, acc = rest[-2:]

        @pl.when(pl.program_id(2) == 0)
        def _():
            acc[...] = jnp.zeros_like(acc)

        acc[...] += lax.dot_general(a_ref[...].astype(BF16), b_ref[...].astype(BF16), (((0,), (0,)), ((), ())),
                                    preferred_element_type=F32)

        @pl.when(pl.program_id(2) == nk - 1)
        def _():
            o_ref[...] = acc[...].astype(o_ref.dtype)

    return pl.pallas_call(
        body, name=name, grid=(m // bm, n // bn, nk),
        in_specs=[pl.BlockSpec((bk, bm), lambda i, j, kk: (kk, i)), pl.BlockSpec((bk, bn), lambda i, j, kk: (kk, j))]
        + [pl.BlockSpec(memory_space=pl.ANY)] * len(ties),
        out_specs=pl.BlockSpec((bm, bn), lambda i, j, kk: (i, j)),
        out_shape=jax.ShapeDtypeStruct((m, n), out_dtype), scratch_shapes=[pltpu.VMEM((bm, bn), F32)],
    )(a, b, *ties)


def loss_and_grad(h, target, *, name):
    tp, d = h.shape
    tm = SWA_BLOCK

    def body(h_ref, t_ref, dh_ref, loss_ref):
        i = pl.program_id(0)

        @pl.when(i == 0)
        def _():
            loss_ref[...] = jnp.zeros_like(loss_ref)
            dh_ref[...] = jnp.zeros_like(dh_ref)

        @pl.when(i > 0)
        def _():
            err = h_ref[...] - t_ref[...]
            dh_ref[...] = err * (1.0 / d)
            loss_ref[...] += 0.5 * jnp.sum(jnp.sum(err * err, axis=1, keepdims=True), axis=0, keepdims=True) * (1.0 / d)

    return pl.pallas_call(
        body, name=name, grid=(tp // tm,),
        in_specs=[pl.BlockSpec((tm, d), lambda i: (i, 0)), pl.BlockSpec((tm, d), lambda i: (jnp.maximum(i - 1, 0), 0))],
        out_specs=(pl.BlockSpec((tm, d), lambda i: (i, 0)), _full((1, 1))),
        out_shape=(jax.ShapeDtypeStruct((tp, d), F32), jax.ShapeDtypeStruct((1, 1), F32)),
    )(h, target)


def _t5_bucket_np(rel):
    n = np.maximum(rel, 0)
    max_exact = REL_BUCKETS // 2
    n_f = np.maximum(n, 1).astype(np.float32)
    large = max_exact + (np.log(n_f / np.float32(max_exact)) / np.float32(math.log(REL_MAX_DIST / max_exact))
                         * np.float32(REL_BUCKETS - max_exact)).astype(np.int32)
    large = np.minimum(large, REL_BUCKETS - 1)
    return np.where(n < max_exact, n, large).astype(np.int32)


def _swa_positions_np(n):
    i = np.arange(SWA_BLOCK)[:, None]
    j = np.arange(3 * SWA_BLOCK)[None, :]
    pos_q = n * SWA_BLOCK + i - PAD
    pos_k = np.where(j < SWA_BLOCK, j - PAD, (n - 1) * SWA_BLOCK + (j - SWA_BLOCK) - PAD)
    return pos_q, pos_k


def _swa_buckets():
    out = []
    for n in range(3):
        pos_q, pos_k = _swa_positions_np(n)
        out.append(_t5_bucket_np(pos_q - pos_k))
    return jnp.asarray(np.stack(out))


def swa_bias(table, buckets, *, name):
    nc, nq, nk = buckets.shape

    def body(tab_ref, bkt_ref, out_ref):
        for c in range(nc):
            bkt = bkt_ref[c]
            for h in range(SWA_Q_HEADS):
                acc = jnp.zeros((nq, nk), F32)
                for b in range(REL_BUCKETS):
                    acc = jnp.where(bkt == b, tab_ref[b, h], acc)
                out_ref[c, h] = acc

    return pl.pallas_call(
        body, name=name,
        in_specs=[pl.BlockSpec(memory_space=pltpu.SMEM), pl.BlockSpec(memory_space=pltpu.VMEM)],
        out_specs=pl.BlockSpec(memory_space=pltpu.VMEM),
        out_shape=jax.ShapeDtypeStruct((nc, SWA_Q_HEADS, nq, nk), F32),
    )(table, buckets)


def swa_bias_bwd(dbias, buckets, *, name):
    nc = buckets.shape[0]

    def body(db_ref, bkt_ref, out_ref):
        lane = lax.broadcasted_iota(jnp.int32, (1, LANE), 1)
        for b in range(REL_BUCKETS):
            row = jnp.zeros((1, LANE), F32)
            for c in range(nc):
                hit = bkt_ref[c] == b
                for h in range(SWA_Q_HEADS):
                    part = jnp.where(hit, db_ref[c, h], 0.0)
                    tot = jnp.sum(jnp.sum(part, axis=1, keepdims=True), axis=0, keepdims=True)
                    row = row + jnp.where(lane == h, tot, 0.0)
            out_ref[b:b + 1, :] = row

    return pl.pallas_call(
        body, name=name,
        in_specs=[pl.BlockSpec(memory_space=pltpu.VMEM), pl.BlockSpec(memory_space=pltpu.VMEM)],
        out_specs=pl.BlockSpec(memory_space=pltpu.VMEM),
        out_shape=jax.ShapeDtypeStruct((REL_BUCKETS, LANE), F32),
    )(dbias, buckets)


def _swa_block(q, kvm, kvp, kvc, bias, sinks, n, batched):
    blk = SWA_BLOCK
    i = lax.broadcasted_iota(jnp.int32, (blk, 3 * blk), 0)
    j = lax.broadcasted_iota(jnp.int32, (blk, 3 * blk), 1)
    pos_q = n * blk + i - PAD
    is_meta = j < blk
    pos_k = jnp.where(is_meta, j - PAD, (n - 1) * blk + (j - blk) - PAD)
    rel = pos_q - pos_k
    valid = ((is_meta & (pos_k >= 0) & (pos_k < N_META) & (rel >= 0))
             | (jnp.logical_not(is_meta) & (pos_k >= N_META) & (rel >= 0) & (rel < SWA_WINDOW)))
    valid_f = valid.astype(F32)
    kv =jnp.concatenate([kvm, kvp, kvc], axis=0)
    lane = lax.broadcasted_iota(jnp.int32, (1, LANE), 1)
    halves = ((lane < SWA_HEAD_DIM).astype(F32), (lane >= SWA_HEAD_DIM).astype(F32))
    nh, group = SWA_Q_HEADS, SWA_Q_HEADS // 2
    q_of = lambda h: q[:, (h // 2) * LANE:(h // 2 + 1) * LANE] * halves[h % 2]
    k_of = lambda h: kv[:, (h // group) * LANE:(h // group + 1) * LANE]
    v_of = lambda h: kv[:, (2 + h // group) * LANE:(3 + h // group) * LANE]
    sink_of = lambda h: jnp.sum(jnp.where(lane == h, sinks, 0.0), axis=1, keepdims=True)

    scale = SWA_HEAD_DIM ** -0.5

    def attend(logits, sink, pv):
        if batched:
            s = logits * valid_f + (valid_f - 1.0) * (-NEG_INF)
        else:
            s = jnp.where(valid, logits, NEG_INF)
        m =lax.stop_gradient(jnp.maximum(jnp.max(s, axis=-1, keepdims=True), sink))
        e = jnp.exp(s - m)
        return pv(e / (jnp.sum(e, axis=-1, keepdims=True) + jnp.exp(sink - m)))

    if batched:
        heads = range(nh)
        vh = _stack([v_of(h) for h in heads])
        qk = _bmm_nt(_stack([q_of(h) for h in heads]), _stack([k_of(h) for h in heads]))
        o = attend(qk * scale + bias, _stack([sink_of(h) for h in heads]), lambda p: _bmm(p, vh))
    else:
        o = [attend(_mm_nt(q_of(h), k_of(h)) * scale + bias[h], sink_of(h), lambda p, h=h: _mm(p, v_of(h)))
             for h in range(nh)]
    return jnp.concatenate([o[2 * p] * halves[0] + o[2 * p + 1] * halves[1] for p in range(nh // 2)], axis=1)


def _swa_in_specs(nb, rev):
    blk = SWA_BLOCK
    step = (lambda i: nb - 1 - i) if rev else (lambda i: i)
    return [
        pl.BlockSpec((blk, 4 * LANE), lambda i: (step(i), 0)),
        pl.BlockSpec((blk, 4 * LANE), lambda i: (0, 1)),
        pl.BlockSpec((blk, 4 * LANE), lambda i: (jnp.maximum(step(i) - 1, 0), 1)),
        pl.BlockSpec((blk, 4 * LANE), lambda i: (step(i), 1)),
        pl.BlockSpec((1, SWA_Q_HEADS, blk, 3 * blk), lambda i: (jnp.minimum(step(i), 2), 0, 0, 0)),
        _full((1, LANE)),
    ]


def swa_fwd(proj, bias, sinks, *, name):
    tp = proj.shape[0]
    nb = tp // SWA_BLOCK

    def body(q_ref, kvm_ref, kvp_ref, kvc_ref, bias_ref, sinks_ref, o_ref):
        n = pl.program_id(0)
        o_ref[...] = _swa_block(q_ref[...], kvm_ref[...], kvp_ref[...], kvc_ref[...], bias_ref[0], sinks_ref[...], n, True)

    return pl.pallas_call(
        body, name=name, grid=(nb,),
        in_specs=_swa_in_specs(nb, False),
        out_specs=pl.BlockSpec((SWA_BLOCK, 4 * LANE), lambda i: (i, 0)),
        out_shape=jax.ShapeDtypeStruct((tp, 4 * LANE), F32),
    )(proj, proj, proj, proj, bias, sinks)


def swa_bwd(proj, bias, sinks, do, *, name):
    tp = proj.shape[0]
    nb = tp // SWA_BLOCK
    blk = SWA_BLOCK

    def body(q_ref, kvm_ref, kvp_ref, kvc_ref, bias_ref, sinks_ref, do_ref, dqkv_ref, dbias_ref, dsinks_ref,
             carry, meta_acc):
        i = pl.program_id(0)
        n = nb - 1 - i

        @pl.when(i == 0)
        def _():
            carry[...] = jnp.zeros_like(carry)
            meta_acc[...] = jnp.zeros_like(meta_acc)
            dsinks_ref[...] = jnp.zeros_like(dsinks_ref)

        fn = lambda q, kvm, kvp, kvc, b, s: _swa_block(q, kvm, kvp, kvc, b, s, n, False)
        _, vjp = jax.vjp(fn, q_ref[...], kvm_ref[...], kvp_ref[...], kvc_ref[...], bias_ref[0], sinks_ref[...])
        dq, dkvm, dkvp, dkvc, dbias, dsinks = vjp(do_ref[...])
        dqkv_ref[:, :4 * LANE] = dq.astype(BF16)
        meta_acc[...] += dkvm
        dqkv_ref[:, 4 * LANE:] = (dkvc + carry[...] + jnp.where(n == 0, meta_acc[...], 0.0)).astype(BF16)
        carry[...] = dkvp
        first_visit = (n == nb - 1) | (n < 2)

        @pl.when(first_visit)
        def _():
            dbias_ref[0] = dbias

        @pl.when(jnp.logical_not(first_visit))
        def _():
            dbias_ref[0] += dbias

        dsinks_ref[...] += dsinks

    rev = lambda i: nb - 1 - i
    return pl.pallas_call(
        body, name=name, grid=(nb,),
        in_specs=_swa_in_specs(nb, True) + [pl.BlockSpec((blk, 4 * LANE), lambda i: (rev(i), 0))],
        out_specs=(pl.BlockSpec((blk, 8 * LANE), lambda i: (rev(i), 0)),
                   pl.BlockSpec((1, SWA_Q_HEADS, blk, 3 * blk), lambda i: (jnp.minimum(rev(i), 2), 0, 0, 0)),
                   _full((1, LANE))),
        out_shape=(jax.ShapeDtypeStruct((tp, PROJ_DIM), BF16),
                   jax.ShapeDtypeStruct((3, SWA_Q_HEADS, blk, 3 * blk), F32), jax.ShapeDtypeStruct((1, LANE), F32)),
        scratch_shapes=[pltpu.VMEM((blk, 4 * LANE), F32), pltpu.VMEM((blk, 4 * LANE), F32)],
    )(proj, proj, proj, proj, bias, sinks, do)


CONV_COL0 = 2
HALO = 8


def conv_fwd(proj, conv_w, *, name):
    tp = proj.shape[0]
    tm = _row_tile(tp, 384)
    cw = 4 * LANE
    ncol = conv_w.shape[1] // cw

    def body(x_ref, halo_ref, w_ref, y_ref, buf):
        i = pl.program_id(1)
        buf[0:HALO, :] = jnp.where(i > 0, halo_ref[...], 0.0)
        buf[HALO:, :] = x_ref[...]
        acc = None
        for j in range(DN_CONV):
            term = w_ref[j:j + 1, :] * buf[pl.ds(HALO - (DN_CONV - 1) + j, tm), :]
            acc = term if acc is None else acc + term
        y_ref[...] = acc

    return pl.pallas_call(
        body, name=name, grid=(ncol, tp // tm),
        in_specs=[pl.BlockSpec((tm, cw), lambda c, i: (i, CONV_COL0 + c)),
                  pl.BlockSpec((HALO, cw), lambda c, i: (jnp.maximum(i * (tm // HALO) - 1, 0), CONV_COL0 + c)),
                  pl.BlockSpec((DN_CONV, cw), lambda c, i: (0, c))],
        out_specs=pl.BlockSpec((tm, cw), lambda c, i: (i, c)),
        out_shape=jax.ShapeDtypeStruct((tp, ncol * cw), F32),
        scratch_shapes=[pltpu.VMEM((tm + HALO, cw), F32)],
    )(proj, proj, conv_w)


def conv_bwd(proj, conv_w, dy, dproj, *, name):
    tp = proj.shape[0]
    tm = _row_tile(tp, 384)
    cw = 4 * LANE
    ncol = conv_w.shape[1] // cw
    nt = tp // tm

    def body(x_ref, xhalo_ref, w_ref, dy_ref, dyhalo_ref, _, dx_ref, dw_ref, xbuf, dbuf):
        i = pl.program_id(1)
        xbuf[0:HALO, :] = jnp.where(i > 0, xhalo_ref[...], 0.0)
        xbuf[HALO:, :] = x_ref[...]
        dbuf[0:tm, :] = dy_ref[...]
        dbuf[tm:, :] = jnp.where(i < nt - 1, dyhalo_ref[...], 0.0)
        dy_t = dy_ref[...]
        acc = None
        rows = []
        for j in range(DN_CONV):
            term = w_ref[j:j + 1, :] * dbuf[pl.ds(DN_CONV - 1 - j, tm), :]
            acc = term if acc is None else acc + term
            rows.append(jnp.sum(dy_t * xbuf[pl.ds(HALO - (DN_CONV - 1) + j, tm), :], axis=0, keepdims=True))
        dx_ref[...] = acc.astype(BF16)

        @pl.when(i == 0)
        def _():
            dw_ref[...] = jnp.zeros_like(dw_ref)

        for j in range(DN_CONV):
            dw_ref[j:j + 1, :] += rows[j]

    return pl.pallas_call(
        body, name=name, grid=(ncol, nt),
        in_specs=[pl.BlockSpec((tm, cw), lambda c, i: (i, CONV_COL0 + c)),
                  pl.BlockSpec((HALO, cw), lambda c, i: (jnp.maximum(i * (tm // HALO) - 1, 0), CONV_COL0 + c)),
                  pl.BlockSpec((DN_CONV, cw), lambda c, i: (0, c)),
                  pl.BlockSpec((tm, cw), lambda c, i: (i, c)),
                  pl.BlockSpec((HALO, cw), lambda c, i: (jnp.minimum((i + 1) * (tm // HALO), tp // HALO - 1), c)),
                  pl.BlockSpec(memory_space=pl.ANY)],
        out_specs=(pl.BlockSpec((tm, cw), lambda c, i: (i, CONV_COL0 + c)), pl.BlockSpec((DN_CONV, cw), lambda c, i: (0, c))),
        out_shape=(jax.ShapeDtypeStruct(dproj.shape, dproj.dtype), jax.ShapeDtypeStruct((DN_CONV, ncol * cw), F32)),
        scratch_shapes=[pltpu.VMEM((tm + HALO, cw), F32), pltpu.VMEM((tm + HALO, cw), F32)],
        input_output_aliases={5: 0},
    )(proj, proj, conv_w, dy, dy, dproj)


def _stack(parts):
    return jnp.concatenate([p[None] for p in parts], axis=0)


def _chunk_masks():
    r = lax.broadcasted_iota(jnp.int32, (CHUNK, CHUNK), 0)
    c = lax.broadcasted_iota(jnp.int32, (CHUNK, CHUNK), 1)
    return (r >= c).astype(F32), (r > c).astype(F32), (r == c).astype(F32)


def _dn_chunk(y, z, small, s, a_log, dt_bias, norm_w, rows, known_inv=None):
    tri_incl, tri_strict, eye = _chunk_masks()
    lane = lax.broadcasted_iota(jnp.int32, (1, LANE), 1)
    dk = DN_HEAD_DIM
    nh = DN_HEADS
    heads = lambda t, first: _stack([t[:, (first + h) * dk:(first + h + 1) * dk] for h in range(nh)])
    pick = lambda t, l: jnp.sum(jnp.where(lane == l, t, 0.0), axis=1, keepdims=True)
    q = _l2n(_silu(heads(y, 0))) * dk ** -0.5
    k = _l2n(_silu(heads(y, nh)))
    v = _silu(heads(y, 2 * nh))
    g_all = jnp.where(lane < nh, -jnp.exp(a_log) * _softplus(small + dt_bias), 0.0) * rows
    beta_all = _sigmoid(small)
    gc_all = _cumsum_rows(g_all)
    g_sum = jnp.sum(g_all, axis=0, keepdims=True)
    gc = _stack([pick(gc_all, h) for h in range(nh)])
    beta = _stack([pick(beta_all, nh + h) for h in range(nh)])
    g_last = _stack([pick(g_sum, h) for h in range(nh)])
    gc_row = jnp.sum(eye * gc, axis=1, keepdims=True)
    gamma = jnp.exp((gc - gc_row) * tri_incl) * tri_incl
    k_beta = k * beta
    v_beta = v * beta
    a = _bmm_nt(k_beta, k) * gamma * tri_strict
    if known_inv is None:
        inv = eye - a
        power = a
        for _ in range(5):
            power = _bmm3(power, power)
            inv = inv + _bmm3(inv, power)
    else:
        inv = _known_inverse(a, known_inv)
    e_gc = jnp.exp(gc)
    uw = _bmm3(inv, jnp.concatenate([v_beta, k_beta * e_gc], axis=2))
    u, w = uw[:, :, :dk], uw[:, :, dk:]
    attn = _bmm_nt(q, k) * gamma
    q_dec = q * e_gc
    k_dec = k * jnp.exp(g_last - gc)
    v_new = u - _bmm(w, s)
    o = _bmm(q_dec, s) + _bmm(attn, v_new)
    s_new = s * jnp.exp(g_last) + _bmm_tn(k_dec, v_new)
    out = _rms(o, norm_w) * _silu(heads(z, 0))
    return jnp.concatenate([out[h] for h in range(nh)], axis=1), s_new, inv


Z_COL = 5
SMALL_COL = 24


def _chunk_rows(n):
    row = n * CHUNK + lax.broadcasted_iota(jnp.int32, (CHUNK, 1), 0)
    return (row >= PAD).astype(F32)


def dn_fwd(y, proj, a_log, dt_bias, norm_w, *, name):
    tp = y.shape[0]
    nc = tp // CHUNK
    dk = DN_HEAD_DIM

    def body(y_ref, z_ref, small_ref, al_ref, dt_ref, nw_ref, o_ref, ssave_ref, isave_ref, state):
        n = pl.program_id(0)

        @pl.when(n == 0)
        def _():
            state[...] = jnp.zeros_like(state)

        ssave_ref[0] = state[...]
        out, s_new, inv = _dn_chunk(y_ref[...], z_ref[...], small_ref[...], state[...], al_ref[...], dt_ref[...],
                                    nw_ref[...], _chunk_rows(n))
        o_ref[...] = out
        isave_ref[0] = inv
        state[...] = s_new

    return pl.pallas_call(
        body, name=name, grid=(nc,),
        in_specs=[pl.BlockSpec((CHUNK, y.shape[1]), lambda n: (n, 0)),
                  pl.BlockSpec((CHUNK, 4 * LANE), lambda n: (n, Z_COL)),
                  pl.BlockSpec((CHUNK, LANE), lambda n: (n, SMALL_COL)),
                  _full((1, LANE)), _full((1, LANE)), _full((1, LANE))],
        out_specs=(pl.BlockSpec((CHUNK, 4 * LANE), lambda n: (n, 0)),
                   pl.BlockSpec((1, DN_HEADS, dk, dk), lambda n: (n, 0, 0, 0)),
                   pl.BlockSpec((1, DN_HEADS, CHUNK, CHUNK), lambda n: (n, 0, 0, 0))),
        out_shape=(jax.ShapeDtypeStruct((tp, 4 * LANE), F32), jax.ShapeDtypeStruct((nc, DN_HEADS, dk, dk), F32),
                   jax.ShapeDtypeStruct((nc, DN_HEADS, CHUNK, CHUNK), F32)),
        scratch_shapes=[pltpu.VMEM((DN_HEADS, dk, dk), F32)],
    )(y, proj, proj, a_log, dt_bias, norm_w)


def dn_bwd(y, proj, a_log, dt_bias, norm_w, ssave, isave, do, dproj, *, name):
    tp = y.shape[0]
    nc = tp // CHUNK
    dk = DN_HEAD_DIM
    rev = lambda i: nc - 1 - i
    zs_width = 5 * LANE

    def body(y_ref, z_ref, small_ref, al_ref, dt_ref, nw_ref, ss_ref, is_ref, do_ref, _,
             dy_ref, dzs_ref, dal_ref, ddt_ref, dnw_ref, dstate):
        i = pl.program_id(0)
        n = nc - 1 - i

        @pl.when(i == 0)
        def _():
            dstate[...] = jnp.zeros_like(dstate)
            dal_ref[...] = jnp.zeros_like(dal_ref)
            ddt_ref[...] = jnp.zeros_like(ddt_ref)
            dnw_ref[...] = jnp.zeros_like(dnw_ref)

        rows = _chunk_rows(n)
        known_inv = is_ref[0]
        fn = lambda *a: _dn_chunk(*a, rows, known_inv)[:2]
        _, vjp = jax.vjp(fn, y_ref[...], z_ref[...], small_ref[...], ss_ref[0], al_ref[...], dt_ref[...], nw_ref[...])
        dy, dz, dsmall, ds, dal, ddt, dnw = vjp((do_ref[...], dstate[...]))
        dy_ref[...] = dy
        dzs_ref[:, :4 * LANE] = dz.astype(BF16)
        dzs_ref[:, 4 * LANE:] = dsmall.astype(BF16)
        dstate[...] = ds
        dal_ref[...] += dal
        ddt_ref[...] += ddt
        dnw_ref[...] += dnw

    return pl.pallas_call(
        body, name=name, grid=(nc,),
        in_specs=[pl.BlockSpec((CHUNK, y.shape[1]), lambda i: (rev(i), 0)),
                  pl.BlockSpec((CHUNK, 4 * LANE), lambda i: (rev(i), Z_COL)),
                  pl.BlockSpec((CHUNK, LANE), lambda i: (rev(i), SMALL_COL)),
                  _full((1, LANE)), _full((1, LANE)), _full((1, LANE)),
                  pl.BlockSpec((1, DN_HEADS, dk, dk), lambda i: (rev(i), 0, 0, 0)),
                  pl.BlockSpec((1, DN_HEADS, CHUNK, CHUNK), lambda i: (rev(i), 0, 0, 0)),
                  pl.BlockSpec((CHUNK, 4 * LANE), lambda i: (rev(i), 1)),
                  pl.BlockSpec(memory_space=pl.ANY)],
        out_specs=(pl.BlockSpec((CHUNK, y.shape[1]), lambda i: (rev(i), 0)),
                   pl.BlockSpec((CHUNK, zs_width), lambda i: (rev(i), Z_COL * 4 * LANE // zs_width)),
                   _full((1, LANE)), _full((1, LANE)), _full((1, LANE))),
        out_shape=(jax.ShapeDtypeStruct((tp, y.shape[1]), F32), jax.ShapeDtypeStruct(dproj.shape, dproj.dtype),
                   jax.ShapeDtypeStruct((1, LANE), F32), jax.ShapeDtypeStruct((1, LANE), F32),
                   jax.ShapeDtypeStruct((1, LANE), F32)),
        scratch_shapes=[pltpu.VMEM((DN_HEADS, dk, dk), F32)],
        input_output_aliases={9: 1},
    )(y, proj, proj, a_log, dt_bias, norm_w, ssave, isave, do, dproj)


def _gla_chunk(q, k, v, gate, low, s, w_gate_up, b_gate, norm_w, rows):
    tri_incl, _, _ = _chunk_masks()
    dk, dv, nh = GLA_DK, GLA_DV, GLA_HEADS
    heads = lambda t, width: _stack([t[:, h * width:(h + 1) * width] for h in range(nh)])
    logit = _mm3(low, w_gate_up) + b_gate
    glog_all = -_softplus(-logit) * (1.0 / GLA_GATE_NORM) * rows
    glog = heads(glog_all, dk)
    bcum = heads(_cumsum_rows(glog_all), dk)
    qh = heads(q, dk) * dk ** -0.5
    kh = heads(k, dk)
    vh = heads(v, dv)
    q_dec = qh * jnp.exp(bcum)
    attn = _bmm_nt(q_dec, kh * jnp.exp(-bcum)) * tri_incl
    b_last = jnp.sum(glog, axis=1, keepdims=True)
    k_dec = kh * jnp.exp(b_last - bcum)
    r = lax.broadcasted_iota(jnp.int32, (dk, dk), 0)
    c = lax.broadcasted_iota(jnp.int32, (dk, dk), 1)
    b_last_col = jnp.sum((r == c).astype(F32) * b_last, axis=2, keepdims=True)
    o = _bmm(attn, vh) + _bmm(q_dec, s)
    s_new = s * jnp.exp(b_last_col) + _bmm_tn(k_dec, vh)
    out = _rms(o, norm_w) * _silu(heads(gate, dv))
    return jnp.concatenate([out[h] for h in range(nh)], axis=1), s_new


LOW_COL = 24


def _gla_in_specs(step):
    return [pl.BlockSpec((CHUNK, 4 * LANE), lambda i: (step(i), 0)),
            pl.BlockSpec((CHUNK, 4 * LANE), lambda i: (step(i), 1)),
            pl.BlockSpec((CHUNK, 8 * LANE), lambda i: (step(i), 1)),
            pl.BlockSpec((CHUNK, 8 * LANE), lambda i: (step(i), 2)),
            pl.BlockSpec((CHUNK, LANE), lambda i: (step(i), LOW_COL)),
            _full((LANE, 4 * LANE)), _full((1, 4 * LANE)), _full((1, GLA_DV))]


def gla_fwd(proj, w_gate_up, b_gate, norm_w, *, name):
    tp = proj.shape[0]
    nc = tp // CHUNK

    def body(q_ref, k_ref, v_ref, g_ref, low_ref, wgu_ref, bg_ref, nw_ref, o_ref, ssave_ref, state):
        n = pl.program_id(0)

        @pl.when(n == 0)
        def _():
            state[...] = jnp.zeros_like(state)

        ssave_ref[0] = state[...]
        out, s_new = _gla_chunk(q_ref[...], k_ref[...], v_ref[...], g_ref[...], low_ref[...], state[...], wgu_ref[...],
                                bg_ref[...], nw_ref[...], _chunk_rows(n))
        o_ref[...] = out
        state[...] = s_new

    return pl.pallas_call(
        body, name=name, grid=(nc,),
        in_specs=_gla_in_specs(lambda i: i),
        out_specs=(pl.BlockSpec((CHUNK, 8 * LANE), lambda n: (n, 0)),
                   pl.BlockSpec((1, GLA_HEADS, GLA_DK, GLA_DV), lambda n: (n, 0, 0, 0))),
        out_shape=(jax.ShapeDtypeStruct((tp, 8 * LANE), F32),
                   jax.ShapeDtypeStruct((nc, GLA_HEADS, GLA_DK, GLA_DV), F32)),
        scratch_shapes=[pltpu.VMEM((GLA_HEADS, GLA_DK, GLA_DV), F32)],
    )(proj, proj, proj, proj, proj, w_gate_up, b_gate, norm_w)


def gla_bwd(proj, w_gate_up, b_gate, norm_w, ssave, do, *, name):
    tp = proj.shape[0]
    nc = tp // CHUNK
    rev = lambda i: nc - 1 - i

    def body(q_ref, k_ref, v_ref, g_ref, low_ref, wgu_ref, bg_ref, nw_ref, ss_ref, do_ref,
             dproj_ref, dwgu_ref, dbg_ref, dnw_ref, dstate):
        i = pl.program_id(0)
        n = nc - 1 - i

        @pl.when(i == 0)
        def _():
            dstate[...] = jnp.zeros_like(dstate)
            dwgu_ref[...] = jnp.zeros_like(dwgu_ref)
            dbg_ref[...] = jnp.zeros_like(dbg_ref)
            dnw_ref[...] = jnp.zeros_like(dnw_ref)

        rows = _chunk_rows(n)
        fn = lambda *a: _gla_chunk(*a, rows)
        _, vjp = jax.vjp(fn, q_ref[...], k_ref[...], v_ref[...], g_ref[...], low_ref[...], ss_ref[0], wgu_ref[...],
                         bg_ref[...], nw_ref[...])
        dq, dk, dv, dg, dlow, ds, dwgu, dbg, dnw = vjp((do_ref[...], dstate[...]))
        off = 0
        for part in (dq, dk, dv, dg, dlow):
            dproj_ref[:, off:off + part.shape[1]] = part.astype(BF16)
            off += part.shape[1]
        dstate[...] = ds
        dwgu_ref[...] += dwgu
        dbg_ref[...] += dbg
        dnw_ref[...] += dnw

    chunk = lambda width: pl.BlockSpec((CHUNK, width), lambda i: (rev(i), 0))
    return pl.pallas_call(
        body, name=name, grid=(nc,),
        in_specs=_gla_in_specs(rev) + [pl.BlockSpec((1, GLA_HEADS, GLA_DK, GLA_DV), lambda i: (rev(i), 0, 0, 0)),
                                       chunk(8 * LANE)],
        out_specs=(chunk(PROJ_DIM), _full((LANE, 4 * LANE)), _full((1, 4 * LANE)), _full((1, GLA_DV))),
        out_shape=(jax.ShapeDtypeStruct((tp, PROJ_DIM), BF16), jax.ShapeDtypeStruct((LANE, 4 * LANE), F32),
                   jax.ShapeDtypeStruct((1, 4 * LANE), F32), jax.ShapeDtypeStruct((1, GLA_DV), F32)),
        scratch_shapes=[pltpu.VMEM((GLA_HEADS, GLA_DK, GLA_DV), F32)],
    )(proj, proj, proj, proj, proj, w_gate_up, b_gate, norm_w, ssave, do)


def _even_proj_weight(w_t):
    hd = SWA_HEAD_DIM
    k0, k1 = w_t[512:512 + hd], w_t[512 + hd:640]
    v0, v1 = w_t[640:640 + hd], w_t[640 + hd:768]
    zeros = jnp.zeros((LANE - 2 * DN_HEADS, w_t.shape[1]), w_t.dtype)
    return jnp.concatenate([w_t[:512], k0, k0, k1, k1, v0, v0, v1, v1, w_t[768:2816], w_t[2820:2824], w_t[2816:2820],
                            zeros], axis=0)


def _even_proj_weight_grad(dw):
    hd = SWA_HEAD_DIM
    c = lambda i: dw[512 + i * hd:512 + (i + 1) * hd]
    return jnp.concatenate([dw[:512], c(0) + c(1), c(2) + c(3), c(4) + c(5), c(6) + c(7), dw[1024:3072],
                            dw[3076:3080], dw[3072:3076]], axis=0)


def _ffn_fwd(h, nw_in, nw_out, wts, idx, get_w):
    wts.update(get_w(f"ffn{idx}", h))
    w_gu = wts[f"w_gu{idx}"]
    hn, g, u, a = rms_mm(h, nw_in, w_gu[0], swiglu=True, name=f"ffn_up_{idx}", widx=w_gu[1])
    wts.update(get_w(f"down{idx}", a))
    w_down = wts[f"w_down{idx}"]
    f, h_out = mm_rms_res([a], w_down[0], h, nw_out, scale=0.5, name=f"ffn_down_{idx}", widx=w_down[1])
    return h_out, (h, hn, g, u, a, f)


def _ffn_bwd(dho, saved, nw_in, nw_out, w_gu, w_down, idx, on_grads):
    h, hn, g, u, a, f = saved
    df, dnw_out, dgu = mm_rms_res_bwd(dho, f, nw_out, w_down[0], (g, u), scale=0.5, name=f"ffn_down_bwd_{idx}",
                                      widx=w_down[1])
    g_down = mm_tn(a, df, name=f"ffn_dwd_{idx}", out_dtype=BF16)
    sent = on_grads("down", g_down)
    g_gu = mm_tn(dgu, hn, name=f"ffn_dwgu_{idx}", out_dtype=BF16, after=sent)
    sent = on_grads("gu", g_gu)
    dh, dnw_in = rms_mm_bwd([dgu], w_gu[0], h, nw_in + sent, dho, name=f"ffn_up_bwd_{idx}", widx=w_gu[1])
    return dh, dnw_in, dnw_out


def local_step(x, target, wts, get_w=None, put_g=None):
    seq, d = x.shape
    wts = dict(wts)
    get_w = get_w or (lambda stage, after: {})
    put_g = put_g or (lambda stage, grads: jnp.zeros((1, 1), F32))
    row = lambda v: v.reshape(1, -1)
    lane_row = lambda v: jnp.pad(v.reshape(1, -1), ((0, 0), (0, LANE - v.size)))
    nw = wts["norm_w"]
    h = jnp.concatenate([jnp.zeros((PAD, d), F32), wts["meta_tokens"], x], axis=0)
    buckets = _swa_buckets()
    bias = swa_bias(wts["rel_bias_table"], buckets, name="swa_bias")
    sinks = lane_row(wts["swa_sinks"])
    a_log, dt_bias = lane_row(wts["dn_a_log"]), lane_row(wts["dn_dt_bias"])
    dn_norm_w = row(wts["dn_norm_w"])
    conv_w = wts["even_conv_w"][0]
    w_gate_up = jnp.pad(wts["gla_w_gate_up"][0], ((0, LANE - GLA_GATE_RANK), (0, 0)))
    b_gate, gla_norm_w = row(wts["gla_b_gate"]), row(wts["gla_norm_w"])

    saved = []
    w_in, w_out = [None, None], [None, None]
    for l in range(2):
        h, s_a = _ffn_fwd(h, row(nw[l, 0]), row(nw[l, 1]), wts, 2 * l, get_w)
        if l == 0:
            wts.update(get_w("even", h))
            w_in[0], w_out[0] = _even_proj_weight(wts["even_w_in"]), wts["even_w_out"]
        else:
            wts.update(get_w("odd", h))
            w_in[1] = jnp.pad(wts["odd_w_in"], ((0, PROJ_DIM - wts["odd_w_in"].shape[0]), (0, 0)))
            w_out[1] = wts["odd_w_out"]
        h_mix = h
        hn, proj = rms_mm(h, row(nw[l, 2]), w_in[l], swiglu=False, name=f"mix_in_{l}")
        if l == 0:
            o_a = swa_fwd(proj, bias, sinks, name="swa_fwd")
            y = conv_fwd(proj, conv_w, name="conv_fwd")
            o_b, ssave, isave = dn_fwd(y, proj, a_log, dt_bias, dn_norm_w, name="dn_fwd")
            acts, extra = [o_a, o_b], (y, ssave, isave)
        else:
            o, ssave = gla_fwd(proj, w_gate_up, b_gate, gla_norm_w, name="gla_fwd")
            acts, extra = [o], (ssave,)
        mix, h = mm_rms_res(acts, w_out[l], h, row(nw[l, 3]), scale=1.0, name=f"mix_out_{l}")
        s_m = (h_mix, hn, proj, acts, extra, mix)
        h, s_b = _ffn_fwd(h, row(nw[l, 4]), row(nw[l, 5]), wts, 2 * l + 1, get_w)
        saved.append((s_a, s_m, s_b))

    dh, loss = loss_and_grad(h, target, name="loss")

    grads = {}
    dnw = [[None] * 6 for _ in range(2)]
    def on_grads(i):
        def put(which, g):
            grads[f"g_{which}{i}"] = g
            return put_g(f"{which}{i}", grads)
        return put

    for l in (1, 0):
        s_a, s_m, s_b = saved[l]
        i = 2 * l + 1
        dh, dnw[l][4], dnw[l][5] = _ffn_bwd(dh, s_b, row(nw[l, 4]), row(nw[l, 5]), wts[f"w_gu{i}"], wts[f"w_down{i}"],
                                            i, on_grads(i))
        h_mix, hn, proj, acts, extra, mix = s_m
        dmix, dnw[l][3], do = mm_rms_res_bwd(dh, mix, row(nw[l, 3]), w_out[l], None, scale=1.0, name=f"mix_out_bwd_{l}")
        dw_out = jnp.concatenate([mm_tn(a, dmix, name=f"mix_dwo_{l}_{i}") for i, a in enumerate(acts)], axis=0)
        sent = jnp.zeros((1, 1), F32)
        if l == 0:
            y, ssave, isave = extra
            dproj, dbias, dsinks = swa_bwd(proj, bias, sinks, do, name="swa_bwd")
            dy, dproj, da_log, ddt_bias, ddn_norm_w = dn_bwd(y, proj, a_log, dt_bias, dn_norm_w, ssave, isave, do, dproj,
                                                             name="dn_bwd")
            dproj, dconv_w = conv_bwd(proj, conv_w, dy, dproj, name="conv_bwd")
            grads["rel_bias_table"] = swa_bias_bwd(dbias, buckets, name="swa_bias_bwd")[:, :SWA_Q_HEADS]
            grads["swa_sinks"] = dsinks[:, :SWA_Q_HEADS]
            grads["dn_a_log"] = da_log[:, :DN_HEADS]
            grads["dn_dt_bias"] = ddt_bias[:, :DN_HEADS]
            grads["dn_norm_w"] = ddn_norm_w
            grads["even_conv_w"] = dconv_w[None]
            grads["even_w_out"] = dw_out
        else:
            (ssave,) = extra
            dproj, dwgu, dbg, dgnw = gla_bwd(proj, w_gate_up, b_gate, gla_norm_w, ssave, do, name="gla_bwd")
            grads["gla_w_gate_up"] = dwgu[None, :GLA_GATE_RANK]
            grads["gla_b_gate"] = dbg
            grads["gla_norm_w"] = dgnw
            grads["odd_w_out"] = dw_out
        dw_in = mm_tn(dproj, hn, name=f"mix_dwi_{l}")
        if l == 0:
            grads["even_w_in"] = _even_proj_weight_grad(dw_in)
            sent = put_g("even", grads)
        else:
            grads["odd_w_in"] = dw_in[:wts["odd_w_in"].shape[0]]
        dh, dnw[l][2] = rms_mm_bwd([dproj], w_in[l], h_mix, row(nw[l, 2]) + sent, dh, name=f"mix_in_bwd_{l}")
        i = 2 * l
        dh, dnw[l][0], dnw[l][1] = _ffn_bwd(dh, s_a, row(nw[l, 0]), row(nw[l, 1]), wts[f"w_gu{i}"], wts[f"w_down{i}"],
                                            i, on_grads(i))

    grads["norm_w"] = jnp.stack([jnp.concatenate(r, axis=0) for r in dnw])
    grads["meta_tokens"] = dh[PAD:PAD + N_META]
    return loss[0, 0], dh[PAD + N_META:], grads


def _peer(k):
    x, y, c = (lax.axis_index(a) for a in AXES)
    flip = lambda v, bit: 1 - v if bit else v
    return (flip(x, k & 4), flip(y, k & 2), flip(c, k & 1))


def _my_index():
    x, y, c = (lax.axis_index(a) for a in AXES)
    return 4 * x + 2 * y + c


_HBM = pl.BlockSpec(memory_space=pltpu.HBM)
_SEM = pl.BlockSpec(memory_space=pltpu.SEMAPHORE)
_EFFECT = pltpu.SideEffectType.DATAFLOW_SIDE_EFFECTING


def _remote_copies(items, src_refs, land_refs, send_sems, recv_sems):
    me = _my_index()
    copies = []
    for k in range(1, N_DEV):
        px, py, pc = _peer(k)
        pj = 4 * px + 2 * py + pc
        for a, (sn, send, ln, land, _) in enumerate(items):
            sem = (k - 1) * len(items) + a
            copies.append(pltpu.make_async_remote_copy(
                src_ref=send(src_refs[sn], pj), dst_ref=land(land_refs[ln], me), send_sem=send_sems.at[sem],
                recv_sem=recv_sems.at[sem], device_id=(px, py, pc), device_id_type=MESH))
    return copies


def exchange(srcs, lands, items, after, *, name):
    sn, ln = list(srcs), list(lands)

    def body(*refs):
        src_refs = dict(zip(sn, refs[:len(sn)]))
        land_refs = dict(zip(ln, refs[len(sn) + len(ln) + 1:len(sn) + 2 * len(ln) + 1]))
        send_sems, recv_sems = refs[len(sn) + 2 * len(ln) + 1:]
        copies = _remote_copies(items, src_refs, land_refs, send_sems, recv_sems)
        for cp in copies:
            cp.start()
        for cp in copies:
            cp.wait_recv()
        for cp in copies:
            cp.wait_send()

    n_remote = (N_DEV - 1) * len(items)
    outs = pl.pallas_call(
        body, name=name,
        in_specs=[pl.BlockSpec(memory_space=pl.ANY)] * (len(sn) + len(ln) + 1),
        out_specs=tuple(pl.BlockSpec(memory_space=pl.ANY) for _ in ln),
        out_shape=tuple(jax.ShapeDtypeStruct(lands[n].shape, lands[n].dtype) for n in ln),
        input_output_aliases={len(sn) + i: i for i in range(len(ln))},
        scratch_shapes=[pltpu.SemaphoreType.DMA((n_remote,)), pltpu.SemaphoreType.DMA((n_remote,))],
    )(*[srcs[n] for n in sn], *[lands[n] for n in ln], after)
    return dict(zip(ln, outs))


def start_copies(srcs, lands, items, *, name):
    sn, ln = list(srcs), list(lands)
    n_remote = (N_DEV - 1) * len(items)

    def body(*refs):
        src_refs = dict(zip(sn, refs[:len(sn)]))
        land_refs = dict(zip(ln, refs[len(sn):len(sn) + len(ln)]))
        send_sems, recv_sems = refs[len(sn) + len(ln):len(sn) + len(ln) + 2]
        token = refs[-1]
        for cp in _remote_copies(items, src_refs, land_refs, send_sems, recv_sems):
            cp.start()
        token[...] = jnp.zeros_like(token)

    hbm = lambda a: pltpu.with_memory_space_constraint(a, pltpu.HBM)
    outs = pl.pallas_call(
        body, name=name,
        in_specs=[_HBM] * (len(sn) + len(ln)),
        out_specs=(_SEM, _SEM) + (_HBM,) * len(ln) + (pl.BlockSpec(memory_space=pltpu.VMEM),),
        out_shape=(pltpu.SemaphoreType.DMA((n_remote,)), pltpu.SemaphoreType.DMA((n_remote,)))
        + tuple(pltpu.HBM(lands[n].shape, lands[n].dtype) for n in ln) + (jax.ShapeDtypeStruct((8, LANE), F32),),
        input_output_aliases={len(sn) + i: 2 + i for i in range(len(ln))},
        compiler_params=pltpu.CompilerParams(has_side_effects=_EFFECT),
    )(*[hbm(srcs[n]) for n in sn], *[hbm(lands[n]) for n in ln])
    return (outs[0], outs[1]), dict(zip(ln, outs[2:2 + len(ln)])), outs[-1][0:1, 0:1]


def wait_copies(sems, srcs, lands, items, after, *, name):
    sn, ln = list(srcs), list(lands)

    def body(*refs):
        src_refs = dict(zip(sn, refs[:len(sn)]))
        land_refs = dict(zip(ln, refs[len(sn):len(sn) + len(ln)]))
        send_sems, recv_sems = refs[len(sn) + len(ln):len(sn) + len(ln) + 2]
        copies = _remote_copies(items, src_refs, land_refs, send_sems, recv_sems)
        for cp in copies:
            cp.wait_send()
        for cp in copies:
            cp.wait_recv()

    outs = pl.pallas_call(
        body, name=name,
        in_specs=[_HBM] * (len(sn) + len(ln)) + [_SEM, _SEM, pl.BlockSpec(memory_space=pl.ANY)],
        out_specs=(_HBM,) * len(ln),
        out_shape=tuple(pltpu.HBM(lands[n].shape, lands[n].dtype) for n in ln),
        input_output_aliases={len(sn) + i: i for i in range(len(ln))},
        compiler_params=pltpu.CompilerParams(has_side_effects=_EFFECT),
    )(*[srcs[n] for n in sn], *[lands[n] for n in ln], sems[0], sems[1], after)
    return dict(zip(ln, outs))


def _block(index, size, base=0):
    return pl.ds(pl.multiple_of(base + index * size, ROW_TILE), size)


def _adam_tile(rows):
    for t in (256, 176, 128):
        if rows % t == 0:
            return t
    return rows


def sum_adamw(recvs, w, m, v, *, name, first_slab=0, into=None):
    _, r, c = w.shape
    b = len(recvs)
    rp = recvs[0].shape[1]
    whole = r % ROW_TILE != 0
    tr = r if whole else _adam_tile(r)
    c1 = 1.0 / (1.0 - ADAM_B1 ** ADAM_STEP)
    c2 = 1.0 / (1.0 - ADAM_B2 ** ADAM_STEP)

    def body(*refs):
        recv_refs = refs[:b]
        w_ref, m_ref, v_ref = refs[b:b + 3]
        g_ref, d_ref, nm_ref, nv_ref = refs[b + 3 + (0 if into is None else 4):][:4]
        for slab, recv_ref in enumerate(recv_refs):
            @pl.when(pl.program_id(0) == slab)
            def _():
                g = recv_ref[0].astype(F32)
                for i in range(1, N_DEV):
                    g = g + recv_ref[i].astype(F32)
                if whole:
                    sum_ref = refs[-1]
                    sum_ref[...] = g
                    g = sum_ref[0:r, :]
                nm = ADAM_B1 * m_ref[0] + (1.0 - ADAM_B1) * g
                nv = ADAM_B2 * v_ref[0] + (1.0 - ADAM_B2) * (g * g)
                g_ref[0] = g
                nm_ref[0] = nm
                nv_ref[0] = nv
                d_ref[0] = -ADAM_LR * ((nm * c1) / (jnp.sqrt(nv * c2) + ADAM_EPS) + ADAM_WD * w_ref[0])

    tile = pl.BlockSpec((1, tr, c), lambda bi, i: (first_slab + bi, i, 0))
    piece = lambda slab: pl.BlockSpec((N_DEV, rp if whole else tr, c), lambda bi, i: (0, jnp.where(bi == slab, i, 0), 0))
    earlier = [] if into is None else list(into)
    return pl.pallas_call(
        body, name=name, grid=(b, r // tr),
        in_specs=[piece(slab) for slab in range(b)] + [tile, tile, tile] + [pl.BlockSpec(memory_space=pl.ANY)] * len(earlier),
        out_specs=(tile,) * 4, out_shape=(jax.ShapeDtypeStruct(w.shape, F32),) * 4,
        input_output_aliases={b + 3 + i: i for i in range(len(earlier))},
        scratch_shapes=[pltpu.VMEM((rp, c), F32)] if whole else [],
    )(*recvs, w, m, v, *earlier)


def _flat_rows(n_elems, row_multiple):
    rows = -(-n_elems // FLAT_COLS)
    return -(-rows // row_multiple) * row_multiple


def _pack(arrays, row_multiple, dtype):
    flat = jnp.concatenate([a.reshape(-1).astype(dtype) for a in arrays])
    rows = _flat_rows(flat.size, row_multiple)
    return jnp.pad(flat, (0, rows * FLAT_COLS - flat.size)).reshape(rows, FLAT_COLS)


def _unpack(flat2d, shapes):
    lead = flat2d.shape[:-2]
    flat = flat2d.reshape(lead + (-1,))
    out, off = [], 0
    for shp in shapes:
        n = int(np.prod(shp))
        out.append(flat[..., off:off + n].reshape(lead + tuple(shp)))
        off += n
    return out


def _join_shards(stacked, axis):
    moved = jnp.moveaxis(stacked, 0, axis)
    shp = list(moved.shape)
    shp[axis:axis + 2] = [shp[axis] * shp[axis + 1]]
    return moved.reshape(shp)


def _split_shards(full, axis):
    shp = list(full.shape)
    shp[axis:axis + 1] = [N_DEV, shp[axis] // N_DEV]
    return jnp.moveaxis(full.reshape(shp), axis, 0)


def kernel(x, meta_tokens, norm_w, ffn_w_gate, ffn_w_up, ffn_w_down, rel_bias_table, even_w_in, even_conv_w, swa_sinks, dn_a_log, dn_dt_bias, dn_norm_w, even_w_out, odd_w_in, gla_w_gate_up, gla_b_gate, gla_norm_w, odd_w_out, loss_target, m_meta_tokens, m_norm_w, m_ffn_w_gate, m_ffn_w_up, m_ffn_w_down, m_rel_bias_table, m_even_w_in, m_even_conv_w, m_swa_sinks, m_dn_a_log, m_dn_dt_bias, m_dn_norm_w, m_even_w_out, m_odd_w_in, m_gla_w_gate_up, m_gla_b_gate, m_gla_norm_w, m_odd_w_out, v_meta_tokens, v_norm_w, v_ffn_w_gate, v_ffn_w_up, v_ffn_w_down, v_rel_bias_table, v_even_w_in, v_even_conv_w, v_swa_sinks, v_dn_a_log, v_dn_dt_bias, v_dn_norm_w, v_even_w_out, v_odd_w_in, v_gla_w_gate_up, v_gla_b_gate, v_gla_norm_w, v_odd_w_out):
    args = locals()
    w = {n: args[n] for n in WEIGHTS}
    m = {n: args["m_" + n] for n in WEIGHTS}
    v = {n: args["v_" + n] for n in WEIGHTS}

    d = D_MODEL
    me = _my_index()
    whole = lambda ref, j: ref
    rows = lambda size, base=0: (lambda ref, i: ref.at[_block(i, size, base), :])
    lead = lambda ref, i: ref.at[i]
    of_group = lambda items, g: [it for it in items if it[4] == g]
    names = lambda items, k: list(dict.fromkeys(it[k] for it in items))

    def placed(shape, dtype, parts):
        land = lax.empty(shape, dtype)
        for part, axis, start in parts:
            land = lax.dynamic_update_slice(land, part, tuple(start if a == axis else 0 for a in range(land.ndim)))
        return land

    as_rows = lambda a: jnp.swapaxes(a, -1, -2)
    pad_rows = lambda a, to: jnp.pad(a, [(0, 0)] * (a.ndim - 2) + [(0, to - a.shape[-2]), (0, 0)])
    gate_s = pad_rows(as_rows(w["ffn_w_gate"].reshape(N_FFN, d, FF_SHARD)), FF_SHARD_PAD).astype(BF16)
    up_s = pad_rows(as_rows(w["ffn_w_up"].reshape(N_FFN, d, FF_SHARD)), FF_SHARD_PAD).astype(BF16)
    down_s = pad_rows(w["ffn_w_down"].reshape(N_FFN, FF_SHARD, d), FF_SHARD_PAD).astype(BF16)
    small_s = _pack([w[n] for n in SMALL], 8, F32)
    srcs_w = {"ein": pad_rows(as_rows(w["even_w_in"][0]), EVEN_IN_SHARD_PAD).astype(BF16),
              "oin": pad_rows(as_rows(w["odd_w_in"][0]), ODD_IN_SHARD_PAD).astype(BF16),
              "eout": w["even_w_out"][0].astype(BF16), "oout": w["odd_w_out"][0].astype(BF16), "small": small_s}
    lands_w = {"ein": placed((N_DEV * EVEN_IN_SHARD_PAD, d), BF16, [(srcs_w["ein"], 0, me * EVEN_IN_SHARD_PAD)]),
               "oin": placed((N_DEV * ODD_IN_SHARD_PAD, d), BF16, [(srcs_w["oin"], 0, me * ODD_IN_SHARD_PAD)]),
               "eout": placed((d, d), BF16, [(srcs_w["eout"], 0, me * OUT_SHARD)]),
               "oout": placed((d, d), BF16, [(srcs_w["oout"], 0, me * OUT_SHARD)]),
               "small": placed((N_DEV,) + small_s.shape, F32, [(small_s[None], 0, me)])}
    items_w = [("small", whole, "small", lead, "first"), ("ein", whole, "ein", rows(EVEN_IN_SHARD_PAD), "even"),
               ("eout", whole, "eout", rows(OUT_SHARD), "even"), ("oin", whole, "oin", rows(ODD_IN_SHARD_PAD), "odd"),
               ("oout", whole, "oout", rows(OUT_SHARD), "odd")]
    for i, group, down_group in ((0, "first", "down0"), (1, "ffn1", "ffn1"), (2, "ffn2", "down2"), (3, "ffn3", "ffn3")):
        srcs_w.update({f"gate{i}": gate_s[i], f"up{i}": up_s[i], f"down{i}": down_s[i]})
        lands_w[f"w_gu{i}"] = placed((2 * FF_PAD, d), BF16, [(srcs_w[f"gate{i}"], 0, me * FF_SHARD_PAD),
                                                             (srcs_w[f"up{i}"], 0, FF_PAD + me * FF_SHARD_PAD)])
        lands_w[f"w_down{i}"] = placed((FF_PAD, d), BF16, [(srcs_w[f"down{i}"], 0, me * FF_SHARD_PAD)])
        items_w += [(f"gate{i}", whole, f"w_gu{i}", rows(FF_SHARD_PAD), group),
                    (f"up{i}", whole, f"w_gu{i}", rows(FF_SHARD_PAD, FF_PAD), group),
                    (f"down{i}", whole, f"w_down{i}", rows(FF_SHARD_PAD), down_group)]
    pending, started = {}, []
    for g in ("first", "down0", "even", "ffn1", "ffn2", "down2", "odd", "ffn3"):
        its = of_group(items_w, g)
        srcs = {n: srcs_w[n] for n in names(its, 0)}
        sems, lands, token = start_copies(srcs, {n: lands_w[n] for n in names(its, 2)}, its, name=f"gather_start_{g}")
        pending[g] = (sems, srcs, lands, its)
        started.append(token)

    unpad = lambda p, shard, shard_pad: p.reshape(N_DEV, shard_pad, d)[:, :shard].reshape(N_DEV * shard, d)

    def get_w(stage, after):
        if stage not in pending:
            return {}
        sems, srcs, lands, its = pending[stage]
        landed = wait_copies(sems, srcs, lands, its, after, name=f"gather_wait_{stage}")
        got = {}
        for n, arr in landed.items():
            if n == "small":
                for sn, stacked in zip(SMALL, _unpack(arr, [w[sn].shape for sn in SMALL])):
                    got[sn] = _join_shards(stacked, SHARD_AXIS[sn])
            elif n == "ein":
                got["even_w_in"] = unpad(arr, EVEN_IN_SHARD, EVEN_IN_SHARD_PAD)
            elif n == "oin":
                got["odd_w_in"] = unpad(arr, ODD_IN_SHARD, ODD_IN_SHARD_PAD)
            elif n in ("eout", "oout"):
                got["even_w_out" if n == "eout" else "odd_w_out"] = arr
            else:
                got[n] = (arr, None)
        return got

    full = {n: w[n] for n in REPL}
    full.update(get_w("first", sum(started)))

    repad = lambda g, shard, shard_pad: pad_rows(g.reshape(N_DEV, shard, d), shard_pad)
    pieces_g = {"r_oin": ("oin", None, "gu2"), "r_oout": ("oout", (OUT_SHARD, 0), "gu2"),
                "r_ein": ("ein", None, "even"), "r_eout": ("eout", (OUT_SHARD, 0), "even"), "r_small": ("small", None, "last")}
    for i in range(N_FFN):
        pieces_g.update({f"r_gate{i}": (f"g_gu{i}", (FF_SHARD_PAD, 0), f"gu{i}"),
                         f"r_up{i}": (f"g_gu{i}", (FF_SHARD_PAD, FF_PAD), f"gu{i}"),
                         f"r_down{i}": (f"g_down{i}", (FF_SHARD_PAD, 0), "down0" if i == 0 else f"gu{i}")})
    items_g = [(src, lead if blk is None else rows(*blk), land, lead, group) for land, (src, blk, group) in pieces_g.items()]
    last_groups = ("down0", "gu0")

    def grad_src(n, grads):
        if n == "oin":
            return repad(grads["odd_w_in"], ODD_IN_SHARD, ODD_IN_SHARD_PAD).astype(BF16)
        if n == "ein":
            return repad(grads["even_w_in"], EVEN_IN_SHARD, EVEN_IN_SHARD_PAD).astype(BF16)
        if n in ("oout", "eout"):
            return grads["odd_w_out" if n == "oout" else "even_w_out"].astype(BF16)
        return grads[n]

    def grad_land(n, srcs):
        src, blk, _ = pieces_g[n]
        if blk is None:
            own = lax.dynamic_index_in_dim(srcs[src], me, 0, keepdims=False)
        else:
            own = lax.dynamic_slice_in_dim(srcs[src], blk[1] + me * blk[0], blk[0], 0)
        return placed((N_DEV,) + own.shape, own.dtype, [(own[None], 0, me)])

    sent = {}

    def put_g(stage, grads):
        its = of_group(items_g, stage)
        if not its:
            return jnp.zeros((1, 1), F32)
        srcs = {n: grad_src(n, grads) for n in names(its, 0)}
        lands = {n: grad_land(n, srcs) for n in names(its, 2)}
        sems, lands, token = start_copies(srcs, lands, its, name=f"grads_start_{stage}")
        sent[stage] = (sems, srcs, lands, its)
        return token

    loss, grad_x, grads = local_step(x[0], loss_target[0], full, get_w, put_g)
    loss = lax.psum(loss, AXES)

    order = SMALL + REPL
    pieces = [_split_shards(grads[n].reshape(full[n].shape), SHARD_AXIS[n]) if n in SHARD_AXIS
              else jnp.broadcast_to(grads[n].reshape(w[n].shape)[None], (N_DEV,) + w[n].shape) for n in order]
    flat = jnp.concatenate([p.reshape(N_DEV, -1) for p in pieces], axis=1)
    srows = _flat_rows(flat.shape[1], 8)
    grads["small"] = jnp.pad(flat, ((0, 0), (0, srows * FLAT_COLS - flat.shape[1]))).reshape(N_DEV, srows, FLAT_COLS)
    recv = {}
    for stage, (sems, srcs, lands, its) in sent.items():
        if stage not in last_groups:
            recv.update(wait_copies(sems, srcs, lands, its, grad_x, name=f"grads_wait_{stage}"))
    result = [{} for _ in range(4)]

    views = {"ffn_w_gate": (lambda a: as_rows(a.reshape(N_FFN, d, FF_SHARD)), lambda o, n: as_rows(o).reshape(w[n].shape)),
             "ffn_w_up": (lambda a: as_rows(a.reshape(N_FFN, d, FF_SHARD)), lambda o, n: as_rows(o).reshape(w[n].shape)),
             "ffn_w_down": (lambda a: a.reshape(N_FFN, FF_SHARD, d), lambda o, n: o.reshape(w[n].shape)),
             "even_w_in": (as_rows, lambda o, n: as_rows(o)), "odd_w_in": (as_rows, lambda o, n: as_rows(o)),
             "even_w_out": (lambda a: a, lambda o, n: o), "odd_w_out": (lambda a: a, lambda o, n: o)}

    def adam(n, recvs, first_slab=0, into=None):
        view = views[n][0]
        return sum_adamw(recvs, view(w[n]), view(m[n]), view(v[n]), name=f"adamw_{n}_{first_slab}",
                         first_slab=first_slab, into=into)

    def finish(n, outs):
        for r, o in zip(result, outs):
            r[n] = views[n][1](o, n)

    ffn_recv = (("ffn_w_gate", "r_gate"), ("ffn_w_up", "r_up"), ("ffn_w_down", "r_down"))
    early = {n: adam(n, [recv[f"{r}{i}"] for i in (1, 2, 3)], first_slab=1) for n, r in ffn_recv}
    for n, r in (("even_w_in", "r_ein"), ("odd_w_in", "r_oin"), ("even_w_out", "r_eout"), ("odd_w_out", "r_oout")):
        finish(n, adam(n, [recv[r]]))
    srcs = {"small": grads["small"]}
    recv.update(exchange(srcs, {"r_small": grad_land("r_small", srcs)}, of_group(items_g, "last"),
                         early["ffn_w_down"][0], name="exchange_small"))
    for stage in last_groups:
        sems, srcs, lands, its = sent[stage]
        recv.update(wait_copies(sems, srcs, lands, its, recv["r_small"], name=f"grads_wait_{stage}"))
    for n, r in ffn_recv:
        finish(n, adam(n, [recv[f"{r}0"]], into=early[n]))
    pack_local = lambda t: _pack([t[n] for n in order], 8, F32)[None]
    small_outs = sum_adamw([recv["r_small"]], pack_local(w), pack_local(m), pack_local(v), name="adamw_small")
    for r, o in zip(result, small_outs):
        r.update(zip(order, _unpack(o[0], [w[n].shape for n in order])))
    return (loss, grad_x[None], *[r[n] for r in result for n in WEIGHTS])
```

```python
import functools
import math

import numpy as np
import jax
import jax.numpy as jnp
from jax import lax
from jax.experimental import pallas as pl
from jax.experimental.pallas import tpu as pltpu

F32 = jnp.float32
BF16 = jnp.bfloat16
MESH = pl.DeviceIdType.MESH
AXES = ("x", "y", "c")
N_DEV = 8

D_MODEL = 1024
N_META = 16
D_FF = 2816
NORM_EPS = 1e-6
NEG_INF = -1e30
SWA_Q_HEADS = 8
SWA_HEAD_DIM = 64
SWA_WINDOW = 128
SWA_BLOCK = 128
REL_BUCKETS = 32
REL_MAX_DIST = 128
DN_HEADS = 4
DN_HEAD_DIM = 128
DN_CONV = 4
GLA_HEADS = 4
GLA_DK = 128
GLA_DV = 256
GLA_GATE_RANK = 16
GLA_GATE_NORM = 16.0
CHUNK = 64
PAD = SWA_BLOCK - N_META
LANE = 128
PROJ_DIM = 3200

ADAM_LR = 0.001
ADAM_B1 = 0.9
ADAM_B2 = 0.999
ADAM_EPS = 1e-08
ADAM_WD = 0.01
ADAM_STEP = 10

ROW_TILE = 16
FF_SHARD = D_FF // N_DEV
FF_SHARD_PAD = -(-FF_SHARD // ROW_TILE) * ROW_TILE
FF_PAD = N_DEV * FF_SHARD_PAD
N_FFN = 4
EVEN_IN_SHARD, EVEN_IN_SHARD_PAD = 353, 368
ODD_IN_SHARD, ODD_IN_SHARD_PAD = 386, 400
OUT_SHARD = D_MODEL // N_DEV

FLAT_COLS = 128
BIG = ("ffn_w_gate", "ffn_w_up", "ffn_w_down", "even_w_in", "even_w_out", "odd_w_in", "odd_w_out")
SMALL = ("meta_tokens", "norm_w", "even_conv_w", "gla_w_gate_up", "gla_b_gate", "gla_norm_w")
REPL = ("rel_bias_table", "swa_sinks", "dn_a_log", "dn_dt_bias", "dn_norm_w")
WEIGHTS = ("meta_tokens", "norm_w", "ffn_w_gate", "ffn_w_up", "ffn_w_down", "rel_bias_table", "even_w_in",
           "even_conv_w", "swa_sinks", "dn_a_log", "dn_dt_bias", "dn_norm_w", "even_w_out", "odd_w_in",
           "gla_w_gate_up", "gla_b_gate", "gla_norm_w", "odd_w_out")
SHARD_AXIS = {"ffn_w_gate": 3, "ffn_w_up": 3, "ffn_w_down": 2, "even_w_in": 2, "even_w_out": 1, "odd_w_in": 2,
              "odd_w_out": 1, "meta_tokens": 1, "norm_w": 2, "even_conv_w": 2, "gla_w_gate_up": 2,
              "gla_b_gate": 1, "gla_norm_w": 1}


def _rms(x, w):
    r = lax.rsqrt(jnp.mean(x * x, axis=-1, keepdims=True) + NORM_EPS)
    return x * r * w


def _sigmoid(x):
    return 0.5 * (jnp.tanh(0.5 * x) + 1.0)


def _silu(x):
    return x * _sigmoid(x)


def _softplus(x):
    pos = x > 0
    return jnp.where(pos, x, 0.0) + jnp.log(1.0 + jnp.exp(jnp.where(pos, -x, x)))


def _l2n(x):
    return x * lax.rsqrt(jnp.sum(x * x, axis=-1, keepdims=True) + 1e-6)


def _split_bf16(x):
    hi = x.astype(BF16)
    return hi, (x - hi.astype(F32)).astype(BF16)


def _make_mm(terms, batched):
    off = 1 if batched else 0
    bdims = ((0,), (0,)) if batched else ((), ())

    def dg(a, b, ca, cb):
        dot = lambda p, q: lax.dot_general(p, q, (((ca + off,), (cb + off,)), bdims), preferred_element_type=F32)
        a_hi, a_lo = _split_bf16(a)
        b_hi, b_lo = _split_bf16(b)
        if terms == 1:
            return dot(a_hi, b_hi)
        return dot(a_hi, b_hi) + (dot(a_hi, b_lo) + dot(a_lo, b_hi))

    @jax.custom_vjp
    def nn(a, b):
        return dg(a, b, 1, 0)

    @jax.custom_vjp
    def nt(a, b):
        return dg(a, b, 1, 1)

    @jax.custom_vjp
    def tn(a, b):
        return dg(a, b, 0, 0)

    nn.defvjp(lambda a, b: (nn(a, b), (a, b)), lambda r, g: (nt(g, r[1]), tn(r[0], g)))
    nt.defvjp(lambda a, b: (nt(a, b), (a, b)), lambda r, g: (nn(g, r[1]), tn(g, r[0])))
    tn.defvjp(lambda a, b: (tn(a, b), (a, b)), lambda r, g: (nt(r[1], g), nn(r[0], g)))
    return nn, nt, tn


_mm, _mm_nt, _mm_tn = _make_mm(1, False)
_mm3, _, _ = _make_mm(3, False)
_bmm, _bmm_nt, _bmm_tn = _make_mm(1, True)
_bmm3, _bmm3_nt, _bmm3_tn = _make_mm(3, True)


@jax.custom_vjp
def _known_inverse(a, inv):
    return inv


_known_inverse.defvjp(lambda a, inv: (inv, inv),
                      lambda inv, g: (-_bmm3_tn(inv, _bmm3_nt(g, inv)), jnp.zeros_like(inv)))


def _tri_ones_dot(x, lower):
    n = x.shape[0]
    r = lax.broadcasted_iota(jnp.int32, (n, n), 0)
    c = lax.broadcasted_iota(jnp.int32, (n, n), 1)
    t = ((r >= c) if lower else (r <= c)).astype(BF16)
    hi, lo = _split_bf16(x)
    return jnp.dot(t, hi, preferred_element_type=F32) + jnp.dot(t, lo, preferred_element_type=F32)


@jax.custom_vjp
def _cumsum_rows(x):
    return _tri_ones_dot(x, True)


_cumsum_rows.defvjp(lambda x: (_tri_ones_dot(x, True), None), lambda _, g: (_tri_ones_dot(g, False),))


def _row_tile(n_rows, cap):
    best = LANE
    for t in range(LANE, cap + 1, LANE):
        if n_rows % t == 0:
            best = t
    return best


def _real_rows(tile_index, tm):
    row = tile_index * tm + lax.broadcasted_iota(jnp.int32, (tm, 1), 0)
    return (row >= PAD).astype(F32)


def _full(shape):
    return pl.BlockSpec(shape, lambda *_: (0,) * len(shape))


def _resident(shape):
    return pl.BlockSpec(shape, lambda *_: (0,) * len(shape), pipeline_mode=pl.Buffered(1))


def _resident_w(wmat, widx):
    if wmat.ndim == 2:
        return _resident(wmat.shape)
    return pl.BlockSpec((None,) + wmat.shape[1:], lambda *_: (widx, 0, 0), pipeline_mode=pl.Buffered(1))


def rms_mm(h, w, wmat_t, *, swiglu, name, widx=None):
    tp, d = h.shape
    n = wmat_t.shape[-2]
    tm = _row_tile(tp, 384)
    half = n // 2
    wmat = wmat_t

    def body(h_ref, w_ref, wm_ref, hn_ref, *outs):
        hn = _rms(h_ref[...], w_ref[...]).astype(BF16)
        hn_ref[...] = hn
        p = lax.dot_general(hn, wm_ref[...], (((1,), (1,)), ((), ())), preferred_element_type=F32)
        if swiglu:
            g, u = p[:, :half], p[:, half:]
            outs[0][...] = g.astype(BF16)
            outs[1][...] = u.astype(BF16)
            outs[2][...] = (_silu(g) * u).astype(BF16)
        else:
            outs[0][...] = p

    row = lambda width: pl.BlockSpec((tm, width), lambda i: (i, 0))
    if swiglu:
        out_shape = (jax.ShapeDtypeStruct((tp, d), BF16),) + (jax.ShapeDtypeStruct((tp, half), BF16),) * 3
        out_specs = (row(d), row(half), row(half), row(half))
    else:
        out_shape = (jax.ShapeDtypeStruct((tp, d), BF16), jax.ShapeDtypeStruct((tp, n), F32))
        out_specs = (row(d), row(n))
    return pl.pallas_call(
        body, name=name, grid=(tp // tm,),
        in_specs=[row(d), _full((1, d)), _resident_w(wmat, widx)],
        out_specs=out_specs, out_shape=out_shape,
    )(h, w, wmat)


def mm_rms_res(acts, wmat, h, w, *, scale, name, widx=None):
    tp, d = h.shape
    tm = _row_tile(tp, 384)
    widths = [a.shape[1] for a in acts]
    offs = [sum(widths[:i]) for i in range(len(acts))]
    na = len(acts)

    def body(*refs):
        a_refs = refs[:na]
        wm_ref, h_ref, w_ref, f_ref, ho_ref = refs[na:]
        f = None
        for a_ref, off, width in zip(a_refs, offs, widths):
            part = jnp.dot(a_ref[...].astype(BF16), wm_ref[off:off + width, :], preferred_element_type=F32)
            f = part if f is None else f + part
        f_ref[...] = f
        ho_ref[...] = h_ref[...] + scale * _rms(f, w_ref[...])

    row = lambda width: pl.BlockSpec((tm, width), lambda i: (i, 0))
    return pl.pallas_call(
        body, name=name, grid=(tp // tm,),
        in_specs=[row(wd) for wd in widths] + [_resident_w(wmat, widx), row(d), _full((1, d))],
        out_specs=(row(d), row(d)),
        out_shape=(jax.ShapeDtypeStruct((tp, d), F32), jax.ShapeDtypeStruct((tp, d), F32)),
    )(*acts, wmat, h, w)


def mm_rms_res_bwd(dho, f, w, wmat, gu, *, scale, name, widx=None):
    tp, d = f.shape
    k = wmat.shape[-2]
    tm = _row_tile(tp, 384)
    swiglu = gu is not None

    def body(*refs):
        if swiglu:
            dho_ref, f_ref, w_ref, wm_ref, g_ref, u_ref, df_ref, dw_ref, dgu_ref = refs
        else:
            dho_ref, f_ref, w_ref, wm_ref, df_ref, dw_ref, da_ref = refs
        i = pl.program_id(0)
        _, vjp = jax.vjp(lambda ff, ww: scale * _rms(ff, ww), f_ref[...], w_ref[...])
        df, dw = vjp(dho_ref[...])
        dfb = (df * _real_rows(i, tm)).astype(BF16)
        df_ref[...] = dfb

        @pl.when(i == 0)
        def _():
            dw_ref[...] = jnp.zeros_like(dw_ref)

        dw_ref[...] += dw
        da = lax.dot_general(dfb, wm_ref[...], (((1,), (1,)), ((), ())), preferred_element_type=F32)
        if swiglu:
            g, u, dab = g_ref[...], u_ref[...], da.astype(BF16)
            s = _sigmoid(g)
            dgu_ref[:, :k] = dab * u * s * (1.0 + g * (1.0 - s))
            dgu_ref[:, k:] = dab * g * s
        else:
            da_ref[...] = da

    row = lambda width: pl.BlockSpec((tm, width), lambda i: (i, 0))
    in_specs = [row(d), row(d), _full((1, d)), _resident_w(wmat, widx)]
    args = [dho, f, w, wmat]
    out_shape = [jax.ShapeDtypeStruct((tp, d), BF16), jax.ShapeDtypeStruct((1, d), F32)]
    out_specs = [row(d), _full((1, d))]
    if swiglu:
        in_specs += [row(k), row(k)]
        args += list(gu)
        out_shape += [jax.ShapeDtypeStruct((tp, 2 * k), BF16)]
        out_specs += [row(2 * k)]
    else:
        out_shape += [jax.ShapeDtypeStruct((tp, k), F32)]
        out_specs += [row(k)]
    return pl.pallas_call(body, name=name, grid=(tp // tm,), in_specs=in_specs, out_specs=tuple(out_specs),
                          out_shape=tuple(out_shape))(*args)


def rms_mm_bwd(dps, wmat, h, w, dho, *, name, widx=None):
    tp, d = h.shape
    tm = _row_tile(tp, 384)
    widths = [p.shape[1] for p in dps]
    offs = [sum(widths[:i]) for i in range(len(dps))]
    ndp = len(dps)

    def body(*refs):
        dp_refs = refs[:ndp]
        wm_ref, h_ref, w_ref, dho_ref, dh_ref, dw_ref = refs[ndp:]
        i = pl.program_id(0)
        dhn = None
        for dp_ref, off, width in zip(dp_refs, offs, widths):
            part = jnp.dot(dp_ref[...].astype(BF16), wm_ref[off:off + width, :], preferred_element_type=F32)
            dhn = part if dhn is None else dhn + part
        _, vjp = jax.vjp(_rms, h_ref[...], w_ref[...])
        dx, dw = vjp(dhn)
        dh_ref[...] = (dho_ref[...] + dx) * _real_rows(i, tm)

        @pl.when(i == 0)
        def _():
            dw_ref[...] = jnp.zeros_like(dw_ref)

        dw_ref[...] += dw

    row = lambda width: pl.BlockSpec((tm, width), lambda i: (i, 0))
    return pl.pallas_call(
        body, name=name, grid=(tp // tm,),
        in_specs=[row(wd) for wd in widths] + [_resident_w(wmat, widx), row(d), _full((1, d)), row(d)],
        out_specs=(row(d), _full((1, d))),
        out_shape=(jax.ShapeDtypeStruct((tp, d), F32), jax.ShapeDtypeStruct((1, d), F32)),
    )(*dps, wmat, h, w, dho)


def mm_tn(a, b, *, name, out_dtype=F32, after=None):
    t, m = a.shape
    n = b.shape[1]
    bm = _row_tile(m, 1408 if n <= 1024 else 512)
    bn = _row_tile(n, 1536)
    bk = _row_tile(t, 1408)
    nk = t // bk
    ties = [] if after is None else [after]

    def body(a_ref, b_ref, *rest):
        o_ref, acc = rest[-2:]

        @pl.when(pl.program_id(2) == 0)
        def _():
            acc[...] = jnp.zeros_like(acc)

        acc[...] += lax.dot_general(a_ref[...].astype(BF16), b_ref[...].astype(BF16), (((0,), (0,)), ((), ())),
                                    preferred_element_type=F32)

        @pl.when(pl.program_id(2) == nk - 1)
        def _():
            o_ref[...] = acc[...].astype(o_ref.dtype)

    return pl.pallas_call(
        body, name=name, grid=(m // bm, n // bn, nk),
        in_specs=[pl.BlockSpec((bk, bm), lambda i, j, kk: (kk, i)), pl.BlockSpec((bk, bn), lambda i, j, kk: (kk, j))]
        + [pl.BlockSpec(memory_space=pl.ANY)] * len(ties),
        out_specs=pl.BlockSpec((bm, bn), lambda i, j, kk: (i, j)),
        out_shape=jax.ShapeDtypeStruct((m, n), out_dtype), scratch_shapes=[pltpu.VMEM((bm, bn), F32)],
    )(a, b, *ties)


def loss_and_grad(h, target, *, name):
    tp, d = h.shape
    tm = SWA_BLOCK

    def body(h_ref, t_ref, dh_ref, loss_ref):
        i = pl.program_id(0)

        @pl.when(i == 0)
        def _():
            loss_ref[...] = jnp.zeros_like(loss_ref)
            dh_ref[...] = jnp.zeros_like(dh_ref)

        @pl.when(i > 0)
        def _():
            err = h_ref[...] - t_ref[...]
            dh_ref[...] = err * (1.0 / d)
            loss_ref[...] += 0.5 * jnp.sum(jnp.sum(err * err, axis=1, keepdims=True), axis=0, keepdims=True) * (1.0 / d)

    return pl.pallas_call(
        body, name=name, grid=(tp // tm,),
        in_specs=[pl.BlockSpec((tm, d), lambda i: (i, 0)), pl.BlockSpec((tm, d), lambda i: (jnp.maximum(i - 1, 0), 0))],
        out_specs=(pl.BlockSpec((tm, d), lambda i: (i, 0)), _full((1, 1))),
        out_shape=(jax.ShapeDtypeStruct((tp, d), F32), jax.ShapeDtypeStruct((1, 1), F32)),
    )(h, target)


def _t5_bucket_np(rel):
    n = np.maximum(rel, 0)
    max_exact = REL_BUCKETS // 2
    n_f = np.maximum(n, 1).astype(np.float32)
    large = max_exact + (np.log(n_f / np.float32(max_exact)) / np.float32(math.log(REL_MAX_DIST / max_exact))
                         * np.float32(REL_BUCKETS - max_exact)).astype(np.int32)
    large = np.minimum(large, REL_BUCKETS - 1)
    return np.where(n < max_exact, n, large).astype(np.int32)


def _swa_positions_np(n):
    i = np.arange(SWA_BLOCK)[:, None]
    j = np.arange(3 * SWA_BLOCK)[None, :]
    pos_q = n * SWA_BLOCK + i - PAD
    pos_k = np.where(j < SWA_BLOCK, j - PAD, (n - 1) * SWA_BLOCK + (j - SWA_BLOCK) - PAD)
    return pos_q, pos_k


def _swa_buckets():
    out = []
    for n in range(3):
        pos_q, pos_k = _swa_positions_np(n)
        out.append(_t5_bucket_np(pos_q - pos_k))
    return jnp.asarray(np.stack(out))


def swa_bias(table, buckets, *, name):
    nc, nq, nk = buckets.shape

    def body(tab_ref, bkt_ref, out_ref):
        for c in range(nc):
            bkt = bkt_ref[c]
            for h in range(SWA_Q_HEADS):
                acc = jnp.zeros((nq, nk), F32)
                for b in range(REL_BUCKETS):
                    acc = jnp.where(bkt == b, tab_ref[b, h], acc)
                out_ref[c, h] = acc

    return pl.pallas_call(
        body, name=name,
        in_specs=[pl.BlockSpec(memory_space=pltpu.SMEM), pl.BlockSpec(memory_space=pltpu.VMEM)],
        out_specs=pl.BlockSpec(memory_space=pltpu.VMEM),
        out_shape=jax.ShapeDtypeStruct((nc, SWA_Q_HEADS, nq, nk), F32),
    )(table, buckets)


def swa_bias_bwd(dbias, buckets, *, name):
    nc = buckets.shape[0]

    def body(db_ref, bkt_ref, out_ref):
        lane = lax.broadcasted_iota(jnp.int32, (1, LANE), 1)
        for b in range(REL_BUCKETS):
            row = jnp.zeros((1, LANE), F32)
            for c in range(nc):
                hit = bkt_ref[c] == b
                for h in range(SWA_Q_HEADS):
                    part = jnp.where(hit, db_ref[c, h], 0.0)
                    tot = jnp.sum(jnp.sum(part, axis=1, keepdims=True), axis=0, keepdims=True)
                    row = row + jnp.where(lane == h, tot, 0.0)
            out_ref[b:b + 1, :] = row

    return pl.pallas_call(
        body, name=name,
        in_specs=[pl.BlockSpec(memory_space=pltpu.VMEM), pl.BlockSpec(memory_space=pltpu.VMEM)],
        out_specs=pl.BlockSpec(memory_space=pltpu.VMEM),
        out_shape=jax.ShapeDtypeStruct((REL_BUCKETS, LANE), F32),
    )(dbias, buckets)


def _swa_block(q, kvm, kvp, kvc, bias, sinks, n, batched):
    blk = SWA_BLOCK
    i = lax.broadcasted_iota(jnp.int32, (blk, 3 * blk), 0)
    j = lax.broadcasted_iota(jnp.int32, (blk, 3 * blk), 1)
    pos_q = n * blk + i - PAD
    is_meta = j < blk
    pos_k = jnp.where(is_meta, j - PAD, (n - 1) * blk + (j - blk) - PAD)
    rel = pos_q - pos_k
    valid = ((is_meta & (pos_k >= 0) & (pos_k < N_META) & (rel >= 0))
             | (jnp.logical_not(is_meta) & (pos_k >= N_META) & (rel >= 0) & (rel < SWA_WINDOW)))
    valid_f = valid.astype(F32)
    kv =jnp.concatenate([kvm, kvp, kvc], axis=0)
    lane = lax.broadcasted_iota(jnp.int32, (1, LANE), 1)
    halves = ((lane < SWA_HEAD_DIM).astype(F32), (lane >= SWA_HEAD_DIM).astype(F32))
    nh, group = SWA_Q_HEADS, SWA_Q_HEADS // 2
    q_of = lambda h: q[:, (h // 2) * LANE:(h // 2 + 1) * LANE] * halves[h % 2]
    k_of = lambda h: kv[:, (h // group) * LANE:(h // group + 1) * LANE]
    v_of = lambda h: kv[:, (2 + h // group) * LANE:(3 + h // group) * LANE]
    sink_of = lambda h: jnp.sum(jnp.where(lane == h, sinks, 0.0), axis=1, keepdims=True)

    scale = SWA_HEAD_DIM ** -0.5

    def attend(logits, sink, pv):
        if batched:
            s = logits * valid_f + (valid_f - 1.0) * (-NEG_INF)
        else:
            s = jnp.where(valid, logits, NEG_INF)
        m =lax.stop_gradient(jnp.maximum(jnp.max(s, axis=-1, keepdims=True), sink))
        e = jnp.exp(s - m)
        return pv(e / (jnp.sum(e, axis=-1, keepdims=True) + jnp.exp(sink - m)))

    if batched:
        heads = range(nh)
        vh = _stack([v_of(h) for h in heads])
        qk = _bmm_nt(_stack([q_of(h) for h in heads]), _stack([k_of(h) for h in heads]))
        o = attend(qk * scale + bias, _stack([sink_of(h) for h in heads]), lambda p: _bmm(p, vh))
        head = lambda h: o[h]
    else:
        head = lambda h: attend(_mm_nt(q_of(h), k_of(h)) * scale + bias[h], sink_of(h), lambda p: _mm(p, v_of(h)))
    return jnp.concatenate([head(2 * p) * halves[0] + head(2 * p + 1) * halves[1] for p in range(nh // 2)], axis=1)


def _swa_in_specs(nb, rev):
    blk = SWA_BLOCK
    step = (lambda i: nb - 1 - i) if rev else (lambda i: i)
    return [
        pl.BlockSpec((blk, 4 * LANE), lambda i: (step(i), 0)),
        pl.BlockSpec((blk, 4 * LANE), lambda i: (0, 1)),
        pl.BlockSpec((blk, 4 * LANE), lambda i: (jnp.maximum(step(i) - 1, 0), 1)),
        pl.BlockSpec((blk, 4 * LANE), lambda i: (step(i), 1)),
        pl.BlockSpec((1, SWA_Q_HEADS, blk, 3 * blk), lambda i: (jnp.minimum(step(i), 2), 0, 0, 0)),
        _full((1, LANE)),
    ]


def swa_fwd(proj, bias, sinks, *, name):
    tp = proj.shape[0]
    nb = tp // SWA_BLOCK

    def body(q_ref, kvm_ref, kvp_ref, kvc_ref, bias_ref, sinks_ref, o_ref):
        n = pl.program_id(0)
        o_ref[...] = _swa_block(q_ref[...], kvm_ref[...], kvp_ref[...], kvc_ref[...], bias_ref[0], sinks_ref[...], n, True)

    return pl.pallas_call(
        body, name=name, grid=(nb,),
        in_specs=_swa_in_specs(nb, False),
        out_specs=pl.BlockSpec((SWA_BLOCK, 4 * LANE), lambda i: (i, 0)),
        out_shape=jax.ShapeDtypeStruct((tp, 4 * LANE), F32),
    )(proj, proj, proj, proj, bias, sinks)


def swa_bwd(proj, bias, sinks, do, *, name):
    tp = proj.shape[0]
    nb = tp // SWA_BLOCK
    blk = SWA_BLOCK

    def body(q_ref, kvm_ref, kvp_ref, kvc_ref, bias_ref, sinks_ref, do_ref, dqkv_ref, dbias_ref, dsinks_ref,
             carry, meta_acc):
        i = pl.program_id(0)
        n = nb - 1 - i

        @pl.when(i == 0)
        def _():
            carry[...] = jnp.zeros_like(carry)
            meta_acc[...] = jnp.zeros_like(meta_acc)
            dsinks_ref[...] = jnp.zeros_like(dsinks_ref)

        fn = lambda q, kvm, kvp, kvc, b, s: _swa_block(q, kvm, kvp, kvc, b, s, n, False)
        _, vjp = jax.vjp(fn, q_ref[...], kvm_ref[...], kvp_ref[...], kvc_ref[...], bias_ref[0], sinks_ref[...])
        dq, dkvm, dkvp, dkvc, dbias, dsinks = vjp(do_ref[...])
        dqkv_ref[:, :4 * LANE] = dq.astype(BF16)
        meta_acc[...] += dkvm
        dqkv_ref[:, 4 * LANE:] = (dkvc + carry[...] + jnp.where(n == 0, meta_acc[...], 0.0)).astype(BF16)
        carry[...] = dkvp
        first_visit = (n == nb - 1) | (n < 2)

        @pl.when(first_visit)
        def _():
            dbias_ref[0] = dbias

        @pl.when(jnp.logical_not(first_visit))
        def _():
            dbias_ref[0] += dbias

        dsinks_ref[...] += dsinks

    rev = lambda i: nb - 1 - i
    return pl.pallas_call(
        body, name=name, grid=(nb,),
        in_specs=_swa_in_specs(nb, True) + [pl.BlockSpec((blk, 4 * LANE), lambda i: (rev(i), 0))],
        out_specs=(pl.BlockSpec((blk, 8 * LANE), lambda i: (rev(i), 0)),
                   pl.BlockSpec((1, SWA_Q_HEADS, blk, 3 * blk), lambda i: (jnp.minimum(rev(i), 2), 0, 0, 0)),
                   _full((1, LANE))),
        out_shape=(jax.ShapeDtypeStruct((tp, PROJ_DIM), BF16),
                   jax.ShapeDtypeStruct((3, SWA_Q_HEADS, blk, 3 * blk), F32), jax.ShapeDtypeStruct((1, LANE), F32)),
        scratch_shapes=[pltpu.VMEM((blk, 4 * LANE), F32), pltpu.VMEM((blk, 4 * LANE), F32)],
    )(proj, proj, proj, proj, bias, sinks, do)


CONV_COL0 = 2
HALO = 8


def conv_fwd(proj, conv_w, *, name):
    tp = proj.shape[0]
    tm = _row_tile(tp, 384)
    cw = 4 * LANE
    ncol = conv_w.shape[1] // cw

    def body(x_ref, halo_ref, w_ref, y_ref, buf):
        i = pl.program_id(1)
        buf[0:HALO, :] = jnp.where(i > 0, halo_ref[...], 0.0)
        buf[HALO:, :] = x_ref[...]
        acc = None
        for j in range(DN_CONV):
            term = w_ref[j:j + 1, :] * buf[pl.ds(HALO - (DN_CONV - 1) + j, tm), :]
            acc = term if acc is None else acc + term
        y_ref[...] = acc

    return pl.pallas_call(
        body, name=name, grid=(ncol, tp // tm),
        in_specs=[pl.BlockSpec((tm, cw), lambda c, i: (i, CONV_COL0 + c)),
                  pl.BlockSpec((HALO, cw), lambda c, i: (jnp.maximum(i * (tm // HALO) - 1, 0), CONV_COL0 + c)),
                  pl.BlockSpec((DN_CONV, cw), lambda c, i: (0, c))],
        out_specs=pl.BlockSpec((tm, cw), lambda c, i: (i, c)),
        out_shape=jax.ShapeDtypeStruct((tp, ncol * cw), F32),
        scratch_shapes=[pltpu.VMEM((tm + HALO, cw), F32)],
    )(proj, proj, conv_w)


def conv_bwd(proj, conv_w, dy, dproj, *, name):
    tp = proj.shape[0]
    tm = _row_tile(tp, 384)
    cw = 4 * LANE
    ncol = conv_w.shape[1] // cw
    nt = tp // tm

    def body(x_ref, xhalo_ref, w_ref, dy_ref, dyhalo_ref, _, dx_ref, dw_ref, xbuf, dbuf):
        i = pl.program_id(1)
        xbuf[0:HALO, :] = jnp.where(i > 0, xhalo_ref[...], 0.0)
        xbuf[HALO:, :] = x_ref[...]
        dbuf[0:tm, :] = dy_ref[...]
        dbuf[tm:, :] = jnp.where(i < nt - 1, dyhalo_ref[...], 0.0)
        dy_t = dy_ref[...]
        acc = None
        rows = []
        for j in range(DN_CONV):
            term = w_ref[j:j + 1, :] * dbuf[pl.ds(DN_CONV - 1 - j, tm), :]
            acc = term if acc is None else acc + term
            rows.append(jnp.sum(dy_t * xbuf[pl.ds(HALO - (DN_CONV - 1) + j, tm), :], axis=0, keepdims=True))
        dx_ref[...] = acc.astype(BF16)

        @pl.when(i == 0)
        def _():
            dw_ref[...] = jnp.zeros_like(dw_ref)

        for j in range(DN_CONV):
            dw_ref[j:j + 1, :] += rows[j]

    return pl.pallas_call(
        body, name=name, grid=(ncol, nt),
        in_specs=[pl.BlockSpec((tm, cw), lambda c, i: (i, CONV_COL0 + c)),
                  pl.BlockSpec((HALO, cw), lambda c, i: (jnp.maximum(i * (tm // HALO) - 1, 0), CONV_COL0 + c)),
                  pl.BlockSpec((DN_CONV, cw), lambda c, i: (0, c)),
                  pl.BlockSpec((tm, cw), lambda c, i: (i, c)),
                  pl.BlockSpec((HALO, cw), lambda c, i: (jnp.minimum((i + 1) * (tm // HALO), tp // HALO - 1), c)),
                  pl.BlockSpec(memory_space=pl.ANY)],
        out_specs=(pl.BlockSpec((tm, cw), lambda c, i: (i, CONV_COL0 + c)), pl.BlockSpec((DN_CONV, cw), lambda c, i: (0, c))),
        out_shape=(jax.ShapeDtypeStruct(dproj.shape, dproj.dtype), jax.ShapeDtypeStruct((DN_CONV, ncol * cw), F32)),
        scratch_shapes=[pltpu.VMEM((tm + HALO, cw), F32), pltpu.VMEM((tm + HALO, cw), F32)],
        input_output_aliases={5: 0},
    )(proj, proj, conv_w, dy, dy, dproj)


def _stack(parts):
    return jnp.concatenate([p[None] for p in parts], axis=0)


def _chunk_masks():
    r = lax.broadcasted_iota(jnp.int32, (CHUNK, CHUNK), 0)
    c = lax.broadcasted_iota(jnp.int32, (CHUNK, CHUNK), 1)
    return (r >= c).astype(F32), (r > c).astype(F32), (r == c).astype(F32)


def _dn_chunk(y, z, small, s, a_log, dt_bias, norm_w, rows, known_inv=None):
    tri_incl, tri_strict, eye = _chunk_masks()
    lane = lax.broadcasted_iota(jnp.int32, (1, LANE), 1)
    dk = DN_HEAD_DIM
    nh = DN_HEADS
    heads = lambda t, first: _stack([t[:, (first + h) * dk:(first + h + 1) * dk] for h in range(nh)])
    pick = lambda t, l: jnp.sum(jnp.where(lane == l, t, 0.0), axis=1, keepdims=True)
    q = _l2n(_silu(heads(y, 0))) * dk ** -0.5
    k = _l2n(_silu(heads(y, nh)))
    v = _silu(heads(y, 2 * nh))
    g_all = jnp.where(lane < nh, -jnp.exp(a_log) * _softplus(small + dt_bias), 0.0) * rows
    beta_all = _sigmoid(small)
    gc_all = _cumsum_rows(g_all)
    g_sum = jnp.sum(g_all, axis=0, keepdims=True)
    gc = _stack([pick(gc_all, h) for h in range(nh)])
    beta = _stack([pick(beta_all, nh + h) for h in range(nh)])
    g_last = _stack([pick(g_sum, h) for h in range(nh)])
    gc_row = jnp.sum(eye * gc, axis=1, keepdims=True)
    gamma = jnp.exp((gc - gc_row) * tri_incl) * tri_incl
    k_beta = k * beta
    v_beta = v * beta
    a = _bmm_nt(k_beta, k) * gamma * tri_strict
    if known_inv is None:
        inv = eye - a
        power = a
        for _ in range(5):
            power = _bmm3(power, power)
            inv = inv + _bmm3(inv, power)
    else:
        inv = _known_inverse(a, known_inv)
    e_gc = jnp.exp(gc)
    uw = _bmm3(inv, jnp.concatenate([v_beta, k_beta * e_gc], axis=2))
    u, w = uw[:, :, :dk], uw[:, :, dk:]
    attn = _bmm_nt(q, k) * gamma
    q_dec = q * e_gc
    k_dec = k * jnp.exp(g_last - gc)
    v_new = u - _bmm(w, s)
    o = _bmm(q_dec, s) + _bmm(attn, v_new)
    s_new = s * jnp.exp(g_last) + _bmm_tn(k_dec, v_new)
    out = _rms(o, norm_w) * _silu(heads(z, 0))
    return jnp.concatenate([out[h] for h in range(nh)], axis=1), s_new, inv


Z_COL = 5
SMALL_COL = 24


def _chunk_rows(n):
    row = n * CHUNK + lax.broadcasted_iota(jnp.int32, (CHUNK, 1), 0)
    return (row >= PAD).astype(F32)


def dn_fwd(y, proj, a_log, dt_bias, norm_w, *, name):
    tp = y.shape[0]
    nc = tp // CHUNK
    dk = DN_HEAD_DIM

    def body(y_ref, z_ref, small_ref, al_ref, dt_ref, nw_ref, o_ref, ssave_ref, isave_ref, state):
        n = pl.program_id(0)

        @pl.when(n == 0)
        def _():
            state[...] = jnp.zeros_like(state)

        ssave_ref[0] = state[...]
        out, s_new, inv = _dn_chunk(y_ref[...], z_ref[...], small_ref[...], state[...], al_ref[...], dt_ref[...],
                                    nw_ref[...], _chunk_rows(n))
        o_ref[...] = out
        isave_ref[0] = inv
        state[...] = s_new

    return pl.pallas_call(
        body, name=name, grid=(nc,),
        in_specs=[pl.BlockSpec((CHUNK, y.shape[1]), lambda n: (n, 0)),
                  pl.BlockSpec((CHUNK, 4 * LANE), lambda n: (n, Z_COL)),
                  pl.BlockSpec((CHUNK, LANE), lambda n: (n, SMALL_COL)),
                  _full((1, LANE)), _full((1, LANE)), _full((1, LANE))],
        out_specs=(pl.BlockSpec((CHUNK, 4 * LANE), lambda n: (n, 0)),
                   pl.BlockSpec((1, DN_HEADS, dk, dk), lambda n: (n, 0, 0, 0)),
                   pl.BlockSpec((1, DN_HEADS, CHUNK, CHUNK), lambda n: (n, 0, 0, 0))),
        out_shape=(jax.ShapeDtypeStruct((tp, 4 * LANE), F32), jax.ShapeDtypeStruct((nc, DN_HEADS, dk, dk), F32),
                   jax.ShapeDtypeStruct((nc, DN_HEADS, CHUNK, CHUNK), F32)),
        scratch_shapes=[pltpu.VMEM((DN_HEADS, dk, dk), F32)],
    )(y, proj, proj, a_log, dt_bias, norm_w)


def dn_bwd(y, proj, a_log, dt_bias, norm_w, ssave, isave, do, dproj, *, name):
    tp = y.shape[0]
    nc = tp // CHUNK
    dk = DN_HEAD_DIM
    rev = lambda i: nc - 1 - i
    zs_width = 5 * LANE

    def body(y_ref, z_ref, small_ref, al_ref, dt_ref, nw_ref, ss_ref, is_ref, do_ref, _,
             dy_ref, dzs_ref, dal_ref, ddt_ref, dnw_ref, dstate):
        i = pl.program_id(0)
        n = nc - 1 - i

        @pl.when(i == 0)
        def _():
            dstate[...] = jnp.zeros_like(dstate)
            dal_ref[...] = jnp.zeros_like(dal_ref)
            ddt_ref[...] = jnp.zeros_like(ddt_ref)
            dnw_ref[...] = jnp.zeros_like(dnw_ref)

        rows = _chunk_rows(n)
        known_inv = is_ref[0]
        fn = lambda *a: _dn_chunk(*a, rows, known_inv)[:2]
        _, vjp = jax.vjp(fn, y_ref[...], z_ref[...], small_ref[...], ss_ref[0], al_ref[...], dt_ref[...], nw_ref[...])
        dy, dz, dsmall, ds, dal, ddt, dnw = vjp((do_ref[...], dstate[...]))
        dy_ref[...] = dy
        dzs_ref[:, :4 * LANE] = dz.astype(BF16)
        dzs_ref[:, 4 * LANE:] = dsmall.astype(BF16)
        dstate[...] = ds
        dal_ref[...] += dal
        ddt_ref[...] += ddt
        dnw_ref[...] += dnw

    return pl.pallas_call(
        body, name=name, grid=(nc,),
        in_specs=[pl.BlockSpec((CHUNK, y.shape[1]), lambda i: (rev(i), 0)),
                  pl.BlockSpec((CHUNK, 4 * LANE), lambda i: (rev(i), Z_COL)),
                  pl.BlockSpec((CHUNK, LANE), lambda i: (rev(i), SMALL_COL)),
                  _full((1, LANE)), _full((1, LANE)), _full((1, LANE)),
                  pl.BlockSpec((1, DN_HEADS, dk, dk), lambda i: (rev(i), 0, 0, 0)),
                  pl.BlockSpec((1, DN_HEADS, CHUNK, CHUNK), lambda i: (rev(i), 0, 0, 0)),
                  pl.BlockSpec((CHUNK, 4 * LANE), lambda i: (rev(i), 1)),
                  pl.BlockSpec(memory_space=pl.ANY)],
        out_specs=(pl.BlockSpec((CHUNK, y.shape[1]), lambda i: (rev(i), 0)),
                   pl.BlockSpec((CHUNK, zs_width), lambda i: (rev(i), Z_COL * 4 * LANE // zs_width)),
                   _full((1, LANE)), _full((1, LANE)), _full((1, LANE))),
        out_shape=(jax.ShapeDtypeStruct((tp, y.shape[1]), F32), jax.ShapeDtypeStruct(dproj.shape, dproj.dtype),
                   jax.ShapeDtypeStruct((1, LANE), F32), jax.ShapeDtypeStruct((1, LANE), F32),
                   jax.ShapeDtypeStruct((1, LANE), F32)),
        scratch_shapes=[pltpu.VMEM((DN_HEADS, dk, dk), F32)],
        input_output_aliases={9: 1},
    )(y, proj, proj, a_log, dt_bias, norm_w, ssave, isave, do, dproj)


def _gla_chunk(q, k, v, gate, low, s, w_gate_up, b_gate, norm_w, rows):
    tri_incl, _, _ = _chunk_masks()
    dk, dv, nh = GLA_DK, GLA_DV, GLA_HEADS
    heads = lambda t, width: _stack([t[:, h * width:(h + 1) * width] for h in range(nh)])
    logit = _mm3(low, w_gate_up) + b_gate
    glog_all = -_softplus(-logit) * (1.0 / GLA_GATE_NORM) * rows
    glog = heads(glog_all, dk)
    bcum = heads(_cumsum_rows(glog_all), dk)
    qh = heads(q, dk) * dk ** -0.5
    kh = heads(k, dk)
    vh = heads(v, dv)
    q_dec = qh * jnp.exp(bcum)
    attn = _bmm_nt(q_dec, kh * jnp.exp(-bcum)) * tri_incl
    b_last = jnp.sum(glog, axis=1, keepdims=True)
    k_dec = kh * jnp.exp(b_last - bcum)
    r = lax.broadcasted_iota(jnp.int32, (dk, dk), 0)
    c = lax.broadcasted_iota(jnp.int32, (dk, dk), 1)
    b_last_col = jnp.sum((r == c).astype(F32) * b_last, axis=2, keepdims=True)
    o = _bmm(attn, vh) + _bmm(q_dec, s)
    s_new = s * jnp.exp(b_last_col) + _bmm_tn(k_dec, vh)
    out = _rms(o, norm_w) * _silu(heads(gate, dv))
    return jnp.concatenate([out[h] for h in range(nh)], axis=1), s_new


LOW_COL = 24


def _gla_in_specs(step):
    return [pl.BlockSpec((CHUNK, 4 * LANE), lambda i: (step(i), 0)),
            pl.BlockSpec((CHUNK, 4 * LANE), lambda i: (step(i), 1)),
            pl.BlockSpec((CHUNK, 8 * LANE), lambda i: (step(i), 1)),
            pl.BlockSpec((CHUNK, 8 * LANE), lambda i: (step(i), 2)),
            pl.BlockSpec((CHUNK, LANE), lambda i: (step(i), LOW_COL)),
            _full((LANE, 4 * LANE)), _full((1, 4 * LANE)), _full((1, GLA_DV))]


def gla_fwd(proj, w_gate_up, b_gate, norm_w, *, name):
    tp = proj.shape[0]
    nc = tp // CHUNK

    def body(q_ref, k_ref, v_ref, g_ref, low_ref, wgu_ref, bg_ref, nw_ref, o_ref, ssave_ref, state):
        n = pl.program_id(0)

        @pl.when(n == 0)
        def _():
            state[...] = jnp.zeros_like(state)

        ssave_ref[0] = state[...]
        out, s_new = _gla_chunk(q_ref[...], k_ref[...], v_ref[...], g_ref[...], low_ref[...], state[...], wgu_ref[...],
                                bg_ref[...], nw_ref[...], _chunk_rows(n))
        o_ref[...] = out
        state[...] = s_new

    return pl.pallas_call(
        body, name=name, grid=(nc,),
        in_specs=_gla_in_specs(lambda i: i),
        out_specs=(pl.BlockSpec((CHUNK, 8 * LANE), lambda n: (n, 0)),
                   pl.BlockSpec((1, GLA_HEADS, GLA_DK, GLA_DV), lambda n: (n, 0, 0, 0))),
        out_shape=(jax.ShapeDtypeStruct((tp, 8 * LANE), F32),
                   jax.ShapeDtypeStruct((nc, GLA_HEADS, GLA_DK, GLA_DV), F32)),
        scratch_shapes=[pltpu.VMEM((GLA_HEADS, GLA_DK, GLA_DV), F32)],
    )(proj, proj, proj, proj, proj, w_gate_up, b_gate, norm_w)


def gla_bwd(proj, w_gate_up, b_gate, norm_w, ssave, do, *, name):
    tp = proj.shape[0]
    nc = tp // CHUNK
    rev = lambda i: nc - 1 - i

    def body(q_ref, k_ref, v_ref, g_ref, low_ref, wgu_ref, bg_ref, nw_ref, ss_ref, do_ref,
             dproj_ref, dwgu_ref, dbg_ref, dnw_ref, dstate):
        i = pl.program_id(0)
        n = nc - 1 - i

        @pl.when(i == 0)
        def _():
            dstate[...] = jnp.zeros_like(dstate)
            dwgu_ref[...] = jnp.zeros_like(dwgu_ref)
            dbg_ref[...] = jnp.zeros_like(dbg_ref)
            dnw_ref[...] = jnp.zeros_like(dnw_ref)

        rows = _chunk_rows(n)
        fn = lambda *a: _gla_chunk(*a, rows)
        _, vjp = jax.vjp(fn, q_ref[...], k_ref[...], v_ref[...], g_ref[...], low_ref[...], ss_ref[0], wgu_ref[...],
                         bg_ref[...], nw_ref[...])
        dq, dk, dv, dg, dlow, ds, dwgu, dbg, dnw = vjp((do_ref[...], dstate[...]))
        off = 0
        for part in (dq, dk, dv, dg, dlow):
            dproj_ref[:, off:off + part.shape[1]] = part.astype(BF16)
            off += part.shape[1]
        dstate[...] = ds
        dwgu_ref[...] += dwgu
        dbg_ref[...] += dbg
        dnw_ref[...] += dnw

    chunk = lambda width: pl.BlockSpec((CHUNK, width), lambda i: (rev(i), 0))
    return pl.pallas_call(
        body, name=name, grid=(nc,),
        in_specs=_gla_in_specs(rev) + [pl.BlockSpec((1, GLA_HEADS, GLA_DK, GLA_DV), lambda i: (rev(i), 0, 0, 0)),
                                       chunk(8 * LANE)],
        out_specs=(chunk(PROJ_DIM), _full((LANE, 4 * LANE)), _full((1, 4 * LANE)), _full((1, GLA_DV))),
        out_shape=(jax.ShapeDtypeStruct((tp, PROJ_DIM), BF16), jax.ShapeDtypeStruct((LANE, 4 * LANE), F32),
                   jax.ShapeDtypeStruct((1, 4 * LANE), F32), jax.ShapeDtypeStruct((1, GLA_DV), F32)),
        scratch_shapes=[pltpu.VMEM((GLA_HEADS, GLA_DK, GLA_DV), F32)],
    )(proj, proj, proj, proj, proj, w_gate_up, b_gate, norm_w, ssave, do)


def _even_proj_weight(w_t):
    hd = SWA_HEAD_DIM
    k0, k1 = w_t[512:512 + hd], w_t[512 + hd:640]
    v0, v1 = w_t[640:640 + hd], w_t[640 + hd:768]
    zeros = jnp.zeros((LANE - 2 * DN_HEADS, w_t.shape[1]), w_t.dtype)
    return jnp.concatenate([w_t[:512], k0, k0, k1, k1, v0, v0, v1, v1, w_t[768:2816], w_t[2820:2824], w_t[2816:2820],
                            zeros], axis=0)


def _even_proj_weight_grad(dw):
    hd = SWA_HEAD_DIM
    c = lambda i: dw[512 + i * hd:512 + (i + 1) * hd]
    return jnp.concatenate([dw[:512], c(0) + c(1), c(2) + c(3), c(4) + c(5), c(6) + c(7), dw[1024:3072],
                            dw[3076:3080], dw[3072:3076]], axis=0)


def _ffn_fwd(h, nw_in, nw_out, wts, idx, get_w):
    wts.update(get_w(f"ffn{idx}", h))
    w_gu = wts[f"w_gu{idx}"]
    hn, g, u, a = rms_mm(h, nw_in, w_gu[0], swiglu=True, name=f"ffn_up_{idx}", widx=w_gu[1])
    wts.update(get_w(f"down{idx}", a))
    w_down = wts[f"w_down{idx}"]
    f, h_out = mm_rms_res([a], w_down[0], h, nw_out, scale=0.5, name=f"ffn_down_{idx}", widx=w_down[1])
    return h_out, (h, hn, g, u, a, f)


def _ffn_bwd(dho, saved, nw_in, nw_out, w_gu, w_down, idx, on_grads):
    h, hn, g, u, a, f = saved
    df, dnw_out, dgu = mm_rms_res_bwd(dho, f, nw_out, w_down[0], (g, u), scale=0.5, name=f"ffn_down_bwd_{idx}",
                                      widx=w_down[1])
    g_down = mm_tn(a, df, name=f"ffn_dwd_{idx}", out_dtype=BF16)
    sent = on_grads("down", g_down)
    g_gu = mm_tn(dgu, hn, name=f"ffn_dwgu_{idx}", out_dtype=BF16, after=sent)
    sent = on_grads("gu", g_gu)
    dh, dnw_in = rms_mm_bwd([dgu], w_gu[0], h, nw_in + sent, dho, name=f"ffn_up_bwd_{idx}", widx=w_gu[1])
    return dh, dnw_in, dnw_out


def local_step(x, target, wts, get_w=None, put_g=None):
    seq, d = x.shape
    wts = dict(wts)
    get_w = get_w or (lambda stage, after: {})
    put_g = put_g or (lambda stage, grads: jnp.zeros((1, 1), F32))
    row = lambda v: v.reshape(1, -1)
    lane_row = lambda v: jnp.pad(v.reshape(1, -1), ((0, 0), (0, LANE - v.size)))
    nw = wts["norm_w"]
    h = jnp.concatenate([jnp.zeros((PAD, d), F32), wts["meta_tokens"], x], axis=0)
    buckets = _swa_buckets()
    bias = swa_bias(wts["rel_bias_table"], buckets, name="swa_bias")
    sinks = lane_row(wts["swa_sinks"])
    a_log, dt_bias = lane_row(wts["dn_a_log"]), lane_row(wts["dn_dt_bias"])
    dn_norm_w = row(wts["dn_norm_w"])
    conv_w = wts["even_conv_w"][0]
    w_gate_up = jnp.pad(wts["gla_w_gate_up"][0], ((0, LANE - GLA_GATE_RANK), (0, 0)))
    b_gate, gla_norm_w = row(wts["gla_b_gate"]), row(wts["gla_norm_w"])

    saved = []
    w_in, w_out = [None, None], [None, None]
    for l in range(2):
        h, s_a = _ffn_fwd(h, row(nw[l, 0]), row(nw[l, 1]), wts, 2 * l, get_w)
        if l == 0:
            wts.update(get_w("even", h))
            w_in[0], w_out[0] = _even_proj_weight(wts["even_w_in"]), wts["even_w_out"]
        else:
            wts.update(get_w("odd", h))
            w_in[1] = jnp.pad(wts["odd_w_in"], ((0, PROJ_DIM - wts["odd_w_in"].shape[0]), (0, 0)))
            w_out[1] = wts["odd_w_out"]
        h_mix = h
        hn, proj = rms_mm(h, row(nw[l, 2]), w_in[l], swiglu=False, name=f"mix_in_{l}")
        if l == 0:
            o_a = swa_fwd(proj, bias, sinks, name="swa_fwd")
            y = conv_fwd(proj, conv_w, name="conv_fwd")
            o_b, ssave, isave = dn_fwd(y, proj, a_log, dt_bias, dn_norm_w, name="dn_fwd")
            acts, extra = [o_a, o_b], (y, ssave, isave)
        else:
            o, ssave = gla_fwd(proj, w_gate_up, b_gate, gla_norm_w, name="gla_fwd")
            acts, extra = [o], (ssave,)
        mix, h = mm_rms_res(acts, w_out[l], h, row(nw[l, 3]), scale=1.0, name=f"mix_out_{l}")
        s_m = (h_mix, hn, proj, acts, extra, mix)
        h, s_b = _ffn_fwd(h, row(nw[l, 4]), row(nw[l, 5]), wts, 2 * l + 1, get_w)
        saved.append((s_a, s_m, s_b))

    dh, loss = loss_and_grad(h, target, name="loss")

    grads = {}
    dnw = [[None] * 6 for _ in range(2)]
    def on_grads(i):
        def put(which, g):
            grads[f"g_{which}{i}"] = g
            return put_g(f"{which}{i}", grads)
        return put

    for l in (1, 0):
        s_a, s_m, s_b = saved[l]
        i = 2 * l + 1
        dh, dnw[l][4], dnw[l][5] = _ffn_bwd(dh, s_b, row(nw[l, 4]), row(nw[l, 5]), wts[f"w_gu{i}"], wts[f"w_down{i}"],
                                            i, on_grads(i))
        h_mix, hn, proj, acts, extra, mix = s_m
        dmix, dnw[l][3], do = mm_rms_res_bwd(dh, mix, row(nw[l, 3]), w_out[l], None, scale=1.0, name=f"mix_out_bwd_{l}")
        dw_out = jnp.concatenate([mm_tn(a, dmix, name=f"mix_dwo_{l}_{i}") for i, a in enumerate(acts)], axis=0)
        sent = jnp.zeros((1, 1), F32)
        if l == 0:
            y, ssave, isave = extra
            dproj, dbias, dsinks = swa_bwd(proj, bias, sinks, do, name="swa_bwd")
            dy, dproj, da_log, ddt_bias, ddn_norm_w = dn_bwd(y, proj, a_log, dt_bias, dn_norm_w, ssave, isave, do, dproj,
                                                             name="dn_bwd")
            dproj, dconv_w = conv_bwd(proj, conv_w, dy, dproj, name="conv_bwd")
            grads["rel_bias_table"] = swa_bias_bwd(dbias, buckets, name="swa_bias_bwd")[:, :SWA_Q_HEADS]
            grads["swa_sinks"] = dsinks[:, :SWA_Q_HEADS]
            grads["dn_a_log"] = da_log[:, :DN_HEADS]
            grads["dn_dt_bias"] = ddt_bias[:, :DN_HEADS]
            grads["dn_norm_w"] = ddn_norm_w
            grads["even_conv_w"] = dconv_w[None]
            grads["even_w_out"] = dw_out
        else:
            (ssave,) = extra
            dproj, dwgu, dbg, dgnw = gla_bwd(proj, w_gate_up, b_gate, gla_norm_w, ssave, do, name="gla_bwd")
            grads["gla_w_gate_up"] = dwgu[None, :GLA_GATE_RANK]
            grads["gla_b_gate"] = dbg
            grads["gla_norm_w"] = dgnw
            grads["odd_w_out"] = dw_out
        dw_in = mm_tn(dproj, hn, name=f"mix_dwi_{l}")
        if l == 0:
            grads["even_w_in"] = _even_proj_weight_grad(dw_in)
            sent = put_g("even", grads)
        else:
            grads["odd_w_in"] = dw_in[:wts["odd_w_in"].shape[0]]
        dh, dnw[l][2] = rms_mm_bwd([dproj], w_in[l], h_mix, row(nw[l, 2]) + sent, dh, name=f"mix_in_bwd_{l}")
        i = 2 * l
        dh, dnw[l][0], dnw[l][1] = _ffn_bwd(dh, s_a, row(nw[l, 0]), row(nw[l, 1]), wts[f"w_gu{i}"], wts[f"w_down{i}"],
                                            i, on_grads(i))

    grads["norm_w"] = jnp.stack([jnp.concatenate(r, axis=0) for r in dnw])
    grads["meta_tokens"] = dh[PAD:PAD + N_META]
    return loss[0, 0], dh[PAD + N_META:], grads


def _peer(k):
    x, y, c = (lax.axis_index(a) for a in AXES)
    flip = lambda v, bit: 1 - v if bit else v
    return (flip(x, k & 4), flip(y, k & 2), flip(c, k & 1))


def _my_index():
    x, y, c = (lax.axis_index(a) for a in AXES)
    return 4 * x + 2 * y + c


_HBM = pl.BlockSpec(memory_space=pltpu.HBM)
_SEM = pl.BlockSpec(memory_space=pltpu.SEMAPHORE)
_EFFECT = pltpu.SideEffectType.DATAFLOW_SIDE_EFFECTING


def _remote_copies(items, src_refs, land_refs, send_sems, recv_sems):
    me = _my_index()
    copies = []
    for k in range(1, N_DEV):
        px, py, pc = _peer(k)
        pj = 4 * px + 2 * py + pc
        for a, (sn, send, ln, land, _) in enumerate(items):
            sem = (k - 1) * len(items) + a
            copies.append(pltpu.make_async_remote_copy(
                src_ref=send(src_refs[sn], pj), dst_ref=land(land_refs[ln], me), send_sem=send_sems.at[sem],
                recv_sem=recv_sems.at[sem], device_id=(px, py, pc), device_id_type=MESH))
    return copies


def exchange(srcs, lands, items, after, *, name):
    sn, ln = list(srcs), list(lands)

    def body(*refs):
        src_refs = dict(zip(sn, refs[:len(sn)]))
        land_refs = dict(zip(ln, refs[len(sn) + len(ln) + 1:len(sn) + 2 * len(ln) + 1]))
        send_sems, recv_sems = refs[len(sn) + 2 * len(ln) + 1:]
        copies = _remote_copies(items, src_refs, land_refs, send_sems, recv_sems)
        for cp in copies:
            cp.start()
        for cp in copies:
            cp.wait_recv()
        for cp in copies:
            cp.wait_send()

    n_remote = (N_DEV - 1) * len(items)
    outs = pl.pallas_call(
        body, name=name,
        in_specs=[pl.BlockSpec(memory_space=pl.ANY)] * (len(sn) + len(ln) + 1),
        out_specs=tuple(pl.BlockSpec(memory_space=pl.ANY) for _ in ln),
        out_shape=tuple(jax.ShapeDtypeStruct(lands[n].shape, lands[n].dtype) for n in ln),
        input_output_aliases={len(sn) + i: i for i in range(len(ln))},
        scratch_shapes=[pltpu.SemaphoreType.DMA((n_remote,)), pltpu.SemaphoreType.DMA((n_remote,))],
    )(*[srcs[n] for n in sn], *[lands[n] for n in ln], after)
    return dict(zip(ln, outs))


def start_copies(srcs, lands, items, *, name):
    sn, ln = list(srcs), list(lands)
    n_remote = (N_DEV - 1) * len(items)

    def body(*refs):
        src_refs = dict(zip(sn, refs[:len(sn)]))
        land_refs = dict(zip(ln, refs[len(sn):len(sn) + len(ln)]))
        send_sems, recv_sems = refs[len(sn) + len(ln):len(sn) + len(ln) + 2]
        token = refs[-1]
        for cp in _remote_copies(items, src_refs, land_refs, send_sems, recv_sems):
            cp.start()
        token[...] = jnp.zeros_like(token)

    hbm = lambda a: pltpu.with_memory_space_constraint(a, pltpu.HBM)
    outs = pl.pallas_call(
        body, name=name,
        in_specs=[_HBM] * (len(sn) + len(ln)),
        out_specs=(_SEM, _SEM) + (_HBM,) * len(ln) + (pl.BlockSpec(memory_space=pltpu.VMEM),),
        out_shape=(pltpu.SemaphoreType.DMA((n_remote,)), pltpu.SemaphoreType.DMA((n_remote,)))
        + tuple(pltpu.HBM(lands[n].shape, lands[n].dtype) for n in ln) + (jax.ShapeDtypeStruct((8, LANE), F32),),
        input_output_aliases={len(sn) + i: 2 + i for i in range(len(ln))},
        compiler_params=pltpu.CompilerParams(has_side_effects=_EFFECT),
    )(*[hbm(srcs[n]) for n in sn], *[hbm(lands[n]) for n in ln])
    return (outs[0], outs[1]), dict(zip(ln, outs[2:2 + len(ln)])), outs[-1][0:1, 0:1]


def wait_copies(sems, srcs, lands, items, after, *, name):
    sn, ln = list(srcs), list(lands)

    def body(*refs):
        src_refs = dict(zip(sn, refs[:len(sn)]))
        land_refs = dict(zip(ln, refs[len(sn):len(sn) + len(ln)]))
        send_sems, recv_sems = refs[len(sn) + len(ln):len(sn) + len(ln) + 2]
        copies = _remote_copies(items, src_refs, land_refs, send_sems, recv_sems)
        for cp in copies:
            cp.wait_send()
        for cp in copies:
            cp.wait_recv()

    outs = pl.pallas_call(
        body, name=name,
        in_specs=[_HBM] * (len(sn) + len(ln)) + [_SEM, _SEM, pl.BlockSpec(memory_space=pl.ANY)],
        out_specs=(_HBM,) * len(ln),
        out_shape=tuple(pltpu.HBM(lands[n].shape, lands[n].dtype) for n in ln),
        input_output_aliases={len(sn) + i: i for i in range(len(ln))},
        compiler_params=pltpu.CompilerParams(has_side_effects=_EFFECT),
    )(*[srcs[n] for n in sn], *[lands[n] for n in ln], sems[0], sems[1], after)
    return dict(zip(ln, outs))


def _block(index, size, base=0):
    return pl.ds(pl.multiple_of(base + index * size, ROW_TILE), size)


def _adam_tile(rows):
    for t in (256, 176, 128):
        if rows % t == 0:
            return t
    return rows


def sum_adamw(recvs, w, m, v, *, name, first_slab=0, into=None):
    _, r, c = w.shape
    b = len(recvs)
    rp = recvs[0].shape[1]
    whole = r % ROW_TILE != 0
    tr = r if whole else _adam_tile(r)
    c1 = 1.0 / (1.0 - ADAM_B1 ** ADAM_STEP)
    c2 = 1.0 / (1.0 - ADAM_B2 ** ADAM_STEP)

    def body(*refs):
        recv_refs = refs[:b]
        w_ref, m_ref, v_ref = refs[b:b + 3]
        g_ref, d_ref, nm_ref, nv_ref = refs[b + 3 + (0 if into is None else 4):][:4]
        for slab, recv_ref in enumerate(recv_refs):
            @pl.when(pl.program_id(0) == slab)
            def _():
                g = recv_ref[0].astype(F32)
                for i in range(1, N_DEV):
                    g = g + recv_ref[i].astype(F32)
                if whole:
                    sum_ref = refs[-1]
                    sum_ref[...] = g
                    g = sum_ref[0:r, :]
                nm = ADAM_B1 * m_ref[0] + (1.0 - ADAM_B1) * g
                nv = ADAM_B2 * v_ref[0] + (1.0 - ADAM_B2) * (g * g)
                g_ref[0] = g
                nm_ref[0] = nm
                nv_ref[0] = nv
                d_ref[0] = -ADAM_LR * ((nm * c1) / (jnp.sqrt(nv * c2) + ADAM_EPS) + ADAM_WD * w_ref[0])

    tile = pl.BlockSpec((1, tr, c), lambda bi, i: (first_slab + bi, i, 0))
    piece = lambda slab: pl.BlockSpec((N_DEV, rp if whole else tr, c), lambda bi, i: (0, jnp.where(bi == slab, i, 0), 0))
    earlier = [] if into is None else list(into)
    return pl.pallas_call(
        body, name=name, grid=(b, r // tr),
        in_specs=[piece(slab) for slab in range(b)] + [tile, tile, tile] + [pl.BlockSpec(memory_space=pl.ANY)] * len(earlier),
        out_specs=(tile,) * 4, out_shape=(jax.ShapeDtypeStruct(w.shape, F32),) * 4,
        input_output_aliases={b + 3 + i: i for i in range(len(earlier))},
        scratch_shapes=[pltpu.VMEM((rp, c), F32)] if whole else [],
    )(*recvs, w, m, v, *earlier)


def _flat_rows(n_elems, row_multiple):
    rows = -(-n_elems // FLAT_COLS)
    return -(-rows // row_multiple) * row_multiple


def _pack(arrays, row_multiple, dtype):
    flat = jnp.concatenate([a.reshape(-1).astype(dtype) for a in arrays])
    rows = _flat_rows(flat.size, row_multiple)
    return jnp.pad(flat, (0, rows * FLAT_COLS - flat.size)).reshape(rows, FLAT_COLS)


def _unpack(flat2d, shapes):
    lead = flat2d.shape[:-2]
    flat = flat2d.reshape(lead + (-1,))
    out, off = [], 0
    for shp in shapes:
        n = int(np.prod(shp))
        out.append(flat[..., off:off + n].reshape(lead + tuple(shp)))
        off += n
    return out


def _join_shards(stacked, axis):
    moved = jnp.moveaxis(stacked, 0, axis)
    shp = list(moved.shape)
    shp[axis:axis + 2] = [shp[axis] * shp[axis + 1]]
    return moved.reshape(shp)


def _split_shards(full, axis):
    shp = list(full.shape)
    shp[axis:axis + 1] = [N_DEV, shp[axis] // N_DEV]
    return jnp.moveaxis(full.reshape(shp), axis, 0)


def kernel(x, meta_tokens, norm_w, ffn_w_gate, ffn_w_up, ffn_w_down, rel_bias_table, even_w_in, even_conv_w, swa_sinks, dn_a_log, dn_dt_bias, dn_norm_w, even_w_out, odd_w_in, gla_w_gate_up, gla_b_gate, gla_norm_w, odd_w_out, loss_target, m_meta_tokens, m_norm_w, m_ffn_w_gate, m_ffn_w_up, m_ffn_w_down, m_rel_bias_table, m_even_w_in, m_even_conv_w, m_swa_sinks, m_dn_a_log, m_dn_dt_bias, m_dn_norm_w, m_even_w_out, m_odd_w_in, m_gla_w_gate_up, m_gla_b_gate, m_gla_norm_w, m_odd_w_out, v_meta_tokens, v_norm_w, v_ffn_w_gate, v_ffn_w_up, v_ffn_w_down, v_rel_bias_table, v_even_w_in, v_even_conv_w, v_swa_sinks, v_dn_a_log, v_dn_dt_bias, v_dn_norm_w, v_even_w_out, v_odd_w_in, v_gla_w_gate_up, v_gla_b_gate, v_gla_norm_w, v_odd_w_out):
    args = locals()
    w = {n: args[n] for n in WEIGHTS}
    m = {n: args["m_" + n] for n in WEIGHTS}
    v = {n: args["v_" + n] for n in WEIGHTS}

    d = D_MODEL
    me = _my_index()
    whole = lambda ref, j: ref
    rows = lambda size, base=0: (lambda ref, i: ref.at[_block(i, size, base), :])
    lead = lambda ref, i: ref.at[i]
    of_group = lambda items, g: [it for it in items if it[4] == g]
    names = lambda items, k: list(dict.fromkeys(it[k] for it in items))

    def placed(shape, dtype, parts):
        land = lax.empty(shape, dtype)
        for part, axis, start in parts:
            land = lax.dynamic_update_slice(land, part, tuple(start if a == axis else 0 for a in range(land.ndim)))
        return land

    as_rows = lambda a: jnp.swapaxes(a, -1, -2)
    pad_rows = lambda a, to: jnp.pad(a, [(0, 0)] * (a.ndim - 2) + [(0, to - a.shape[-2]), (0, 0)])
    gate_s = pad_rows(as_rows(w["ffn_w_gate"].reshape(N_FFN, d, FF_SHARD)), FF_SHARD_PAD).astype(BF16)
    up_s = pad_rows(as_rows(w["ffn_w_up"].reshape(N_FFN, d, FF_SHARD)), FF_SHARD_PAD).astype(BF16)
    down_s = pad_rows(w["ffn_w_down"].reshape(N_FFN, FF_SHARD, d), FF_SHARD_PAD).astype(BF16)
    small_s = _pack([w[n] for n in SMALL], 8, F32)
    srcs_w = {"ein": pad_rows(as_rows(w["even_w_in"][0]), EVEN_IN_SHARD_PAD).astype(BF16),
              "oin": pad_rows(as_rows(w["odd_w_in"][0]), ODD_IN_SHARD_PAD).astype(BF16),
              "eout": w["even_w_out"][0].astype(BF16), "oout": w["odd_w_out"][0].astype(BF16), "small": small_s}
    lands_w = {"ein": placed((N_DEV * EVEN_IN_SHARD_PAD, d), BF16, [(srcs_w["ein"], 0, me * EVEN_IN_SHARD_PAD)]),
               "oin": placed((N_DEV * ODD_IN_SHARD_PAD, d), BF16, [(srcs_w["oin"], 0, me * ODD_IN_SHARD_PAD)]),
               "eout": placed((d, d), BF16, [(srcs_w["eout"], 0, me * OUT_SHARD)]),
               "oout": placed((d, d), BF16, [(srcs_w["oout"], 0, me * OUT_SHARD)]),
               "small": placed((N_DEV,) + small_s.shape, F32, [(small_s[None], 0, me)])}
    items_w = [("small", whole, "small", lead, "first"), ("ein", whole, "ein", rows(EVEN_IN_SHARD_PAD), "even"),
               ("eout", whole, "eout", rows(OUT_SHARD), "even"), ("oin", whole, "oin", rows(ODD_IN_SHARD_PAD), "odd"),
               ("oout", whole, "oout", rows(OUT_SHARD), "odd")]
    for i, group, down_group in ((0, "first", "down0"), (1, "ffn1", "ffn1"), (2, "ffn2", "down2"), (3, "ffn3", "ffn3")):
        srcs_w.update({f"gate{i}": gate_s[i], f"up{i}": up_s[i], f"down{i}": down_s[i]})
        lands_w[f"w_gu{i}"] = placed((2 * FF_PAD, d), BF16, [(srcs_w[f"gate{i}"], 0, me * FF_SHARD_PAD),
                                                             (srcs_w[f"up{i}"], 0, FF_PAD + me * FF_SHARD_PAD)])
        lands_w[f"w_down{i}"] = placed((FF_PAD, d), BF16, [(srcs_w[f"down{i}"], 0, me * FF_SHARD_PAD)])
        items_w += [(f"gate{i}", whole, f"w_gu{i}", rows(FF_SHARD_PAD), group),
                    (f"up{i}", whole, f"w_gu{i}", rows(FF_SHARD_PAD, FF_PAD), group),
                    (f"down{i}", whole, f"w_down{i}", rows(FF_SHARD_PAD), down_group)]
    pending, started = {}, []
    for g in ("first", "down0", "even", "ffn1", "ffn2", "down2", "odd", "ffn3"):
        its = of_group(items_w, g)
        srcs = {n: srcs_w[n] for n in names(its, 0)}
        sems, lands, token = start_copies(srcs, {n: lands_w[n] for n in names(its, 2)}, its, name=f"gather_start_{g}")
        pending[g] = (sems, srcs, lands, its)
        started.append(token)

    unpad = lambda p, shard, shard_pad: p.reshape(N_DEV, shard_pad, d)[:, :shard].reshape(N_DEV * shard, d)

    def get_w(stage, after):
        if stage not in pending:
            return {}
        sems, srcs, lands, its = pending[stage]
        landed = wait_copies(sems, srcs, lands, its, after, name=f"gather_wait_{stage}")
        got = {}
        for n, arr in landed.items():
            if n == "small":
                for sn, stacked in zip(SMALL, _unpack(arr, [w[sn].shape for sn in SMALL])):
                    got[sn] = _join_shards(stacked, SHARD_AXIS[sn])
            elif n == "ein":
                got["even_w_in"] = unpad(arr, EVEN_IN_SHARD, EVEN_IN_SHARD_PAD)
            elif n == "oin":
                got["odd_w_in"] = unpad(arr, ODD_IN_SHARD, ODD_IN_SHARD_PAD)
            elif n in ("eout", "oout"):
                got["even_w_out" if n == "eout" else "odd_w_out"] = arr
            else:
                got[n] = (arr, None)
        return got

    full = {n: w[n] for n in REPL}
    full.update(get_w("first", sum(started)))

    repad = lambda g, shard, shard_pad: pad_rows(g.reshape(N_DEV, shard, d), shard_pad)
    pieces_g = {"r_oin": ("oin", None, "gu2"), "r_oout": ("oout", (OUT_SHARD, 0), "gu2"),
                "r_ein": ("ein", None, "even"), "r_eout": ("eout", (OUT_SHARD, 0), "even"), "r_small": ("small", None, "last")}
    for i in range(N_FFN):
        pieces_g.update({f"r_gate{i}": (f"g_gu{i}", (FF_SHARD_PAD, 0), f"gu{i}"),
                         f"r_up{i}": (f"g_gu{i}", (FF_SHARD_PAD, FF_PAD), f"gu{i}"),
                         f"r_down{i}": (f"g_down{i}", (FF_SHARD_PAD, 0), "down0" if i == 0 else f"gu{i}")})
    items_g = [(src, lead if blk is None else rows(*blk), land, lead, group) for land, (src, blk, group) in pieces_g.items()]
    last_groups = ("down0", "gu0")

    def grad_src(n, grads):
        if n == "oin":
            return repad(grads["odd_w_in"], ODD_IN_SHARD, ODD_IN_SHARD_PAD).astype(BF16)
        if n == "ein":
            return repad(grads["even_w_in"], EVEN_IN_SHARD, EVEN_IN_SHARD_PAD).astype(BF16)
        if n in ("oout", "eout"):
            return grads["odd_w_out" if n == "oout" else "even_w_out"].astype(BF16)
        return grads[n]

    def grad_land(n, srcs):
        src, blk, _ = pieces_g[n]
        if blk is None:
            own = lax.dynamic_index_in_dim(srcs[src], me, 0, keepdims=False)
        else:
            own = lax.dynamic_slice_in_dim(srcs[src], blk[1] + me * blk[0], blk[0], 0)
        return placed((N_DEV,) + own.shape, own.dtype, [(own[None], 0, me)])

    sent = {}

    def put_g(stage, grads):
        its = of_group(items_g, stage)
        if not its:
            return jnp.zeros((1, 1), F32)
        srcs = {n: grad_src(n, grads) for n in names(its, 0)}
        lands = {n: grad_land(n, srcs) for n in names(its, 2)}
        sems, lands, token = start_copies(srcs, lands, its, name=f"grads_start_{stage}")
        sent[stage] = (sems, srcs, lands, its)
        return token

    loss, grad_x, grads = local_step(x[0], loss_target[0], full, get_w, put_g)
    loss = lax.psum(loss, AXES)

    order = SMALL + REPL
    pieces = [_split_shards(grads[n].reshape(full[n].shape), SHARD_AXIS[n]) if n in SHARD_AXIS
              else jnp.broadcast_to(grads[n].reshape(w[n].shape)[None], (N_DEV,) + w[n].shape) for n in order]
    flat = jnp.concatenate([p.reshape(N_DEV, -1) for p in pieces], axis=1)
    srows = _flat_rows(flat.shape[1], 8)
    grads["small"] = jnp.pad(flat, ((0, 0), (0, srows * FLAT_COLS - flat.shape[1]))).reshape(N_DEV, srows, FLAT_COLS)
    recv = {}
    for stage, (sems, srcs, lands, its) in sent.items():
        if stage not in last_groups:
            recv.update(wait_copies(sems, srcs, lands, its, grad_x, name=f"grads_wait_{stage}"))
    result = [{} for _ in range(4)]

    views = {"ffn_w_gate": (lambda a: as_rows(a.reshape(N_FFN, d, FF_SHARD)), lambda o, n: as_rows(o).reshape(w[n].shape)),
             "ffn_w_up": (lambda a: as_rows(a.reshape(N_FFN, d, FF_SHARD)), lambda o, n: as_rows(o).reshape(w[n].shape)),
             "ffn_w_down": (lambda a: a.reshape(N_FFN, FF_SHARD, d), lambda o, n: o.reshape(w[n].shape)),
             "even_w_in": (as_rows, lambda o, n: as_rows(o)), "odd_w_in": (as_rows, lambda o, n: as_rows(o)),
             "even_w_out": (lambda a: a, lambda o, n: o), "odd_w_out": (lambda a: a, lambda o, n: o)}

    def adam(n, recvs, first_slab=0, into=None):
        view = views[n][0]
        return sum_adamw(recvs, view(w[n]), view(m[n]), view(v[n]), name=f"adamw_{n}_{first_slab}",
                         first_slab=first_slab, into=into)

    def finish(n, outs):
        for r, o in zip(result, outs):
            r[n] = views[n][1](o, n)

    ffn_recv = (("ffn_w_gate", "r_gate"), ("ffn_w_up", "r_up"), ("ffn_w_down", "r_down"))
    early = {n: adam(n, [recv[f"{r}{i}"] for i in (1, 2, 3)], first_slab=1) for n, r in ffn_recv}
    for n, r in (("even_w_in", "r_ein"), ("odd_w_in", "r_oin"), ("even_w_out", "r_eout"), ("odd_w_out", "r_oout")):
        finish(n, adam(n, [recv[r]]))
    srcs = {"small": grads["small"]}
    recv.update(exchange(srcs, {"r_small": grad_land("r_small", srcs)}, of_group(items_g, "last"),
                         early["ffn_w_down"][0], name="exchange_small"))
    for stage in last_groups:
        sems, srcs, lands, its = sent[stage]
        recv.update(wait_copies(sems, srcs, lands, its, recv["r_small"], name=f"grads_wait_{stage}"))
    for n, r in ffn_recv:
        finish(n, adam(n, [recv[f"{r}0"]], into=early[n]))
    pack_local = lambda t: _pack([t[n] for n in order], 8, F32)[None]
    small_outs = sum_adamw([recv["r_small"]], pack_local(w), pack_local(m), pack_local(v), name="adamw_small")
    for r, o in zip(result, small_outs):
        r.update(zip(order, _unpack(o[0], [w[n].shape for n in order])))
    return (loss, grad_x[None], *[r[n] for r in result for n in WEIGHTS])
```

```python
import functools
import math

import numpy as np
import jax
import jax.numpy as jnp
from jax import lax
from jax.experimental import pallas as pl
from jax.experimental.pallas import tpu as pltpu

F32 = jnp.float32
BF16 = jnp.bfloat16
MESH = pl.DeviceIdType.MESH
AXES = ("x", "y", "c")
N_DEV = 8

D_MODEL = 1024
N_META = 16
D_FF = 2816
NORM_EPS = 1e-6
NEG_INF = -1e30
SWA_Q_HEADS = 8
SWA_HEAD_DIM = 64
SWA_WINDOW = 128
SWA_BLOCK = 128
REL_BUCKETS = 32
REL_MAX_DIST = 128
DN_HEADS = 4
DN_HEAD_DIM = 128
DN_CONV = 4
GLA_HEADS = 4
GLA_DK = 128
GLA_DV = 256
GLA_GATE_RANK = 16
GLA_GATE_NORM = 16.0
CHUNK = 64
CHUNKS_PER_STEP = 3
PAD = SWA_BLOCK - N_META
LANE = 128
PROJ_DIM = 3200

ADAM_LR = 0.001
ADAM_B1 = 0.9
ADAM_B2 = 0.999
ADAM_EPS = 1e-08
ADAM_WD = 0.01
ADAM_STEP = 10

ROW_TILE = 16
FF_SHARD = D_FF // N_DEV
FF_SHARD_PAD = -(-FF_SHARD // ROW_TILE) * ROW_TILE
FF_PAD = N_DEV * FF_SHARD_PAD
N_FFN = 4
EVEN_IN_SHARD, EVEN_IN_SHARD_PAD = 353, 368
ODD_IN_SHARD, ODD_IN_SHARD_PAD = 386, 400
OUT_SHARD = D_MODEL // N_DEV

FLAT_COLS = 128
BIG = ("ffn_w_gate", "ffn_w_up", "ffn_w_down", "even_w_in", "even_w_out", "odd_w_in", "odd_w_out")
SMALL = ("meta_tokens", "norm_w", "even_conv_w", "gla_w_gate_up", "gla_b_gate", "gla_norm_w")
REPL = ("rel_bias_table", "swa_sinks", "dn_a_log", "dn_dt_bias", "dn_norm_w")
WEIGHTS = ("meta_tokens", "norm_w", "ffn_w_gate", "ffn_w_up", "ffn_w_down", "rel_bias_table", "even_w_in",
           "even_conv_w", "swa_sinks", "dn_a_log", "dn_dt_bias", "dn_norm_w", "even_w_out", "odd_w_in",
           "gla_w_gate_up", "gla_b_gate", "gla_norm_w", "odd_w_out")
SHARD_AXIS = {"ffn_w_gate": 3, "ffn_w_up": 3, "ffn_w_down": 2, "even_w_in": 2, "even_w_out": 1, "odd_w_in": 2,
              "odd_w_out": 1, "meta_tokens": 1, "norm_w": 2, "even_conv_w": 2, "gla_w_gate_up": 2,
              "gla_b_gate": 1, "gla_norm_w": 1}


def _rms(x, w):
    r = lax.rsqrt(jnp.mean(x * x, axis=-1, keepdims=True) + NORM_EPS)
    return x * r * w


def _sigmoid(x):
    return 0.5 * (jnp.tanh(0.5 * x) + 1.0)


def _silu(x):
    return x * _sigmoid(x)


def _softplus(x):
    pos = x > 0
    return jnp.where(pos, x, 0.0) + jnp.log(1.0 + jnp.exp(jnp.where(pos, -x, x)))


def _l2n(x):
    return x * lax.rsqrt(jnp.sum(x * x, axis=-1, keepdims=True) + 1e-6)


def _split_bf16(x):
    hi = x.astype(BF16)
    return hi, (x - hi.astype(F32)).astype(BF16)


def _make_mm(terms, batched):
    off = 1 if batched else 0
    bdims = ((0,), (0,)) if batched else ((), ())

    def dg(a, b, ca, cb):
        dot = lambda p, q: lax.dot_general(p, q, (((ca + off,), (cb + off,)), bdims), preferred_element_type=F32)
        a_hi, a_lo = _split_bf16(a)
        b_hi, b_lo = _split_bf16(b)
        if terms == 1:
            return dot(a_hi, b_hi)
        return dot(a_hi, b_hi) + (dot(a_hi, b_lo) + dot(a_lo, b_hi))

    @jax.custom_vjp
    def nn(a, b):
        return dg(a, b, 1, 0)

    @jax.custom_vjp
    def nt(a, b):
        return dg(a, b, 1, 1)

    @jax.custom_vjp
    def tn(a, b):
        return dg(a, b, 0, 0)

    nn.defvjp(lambda a, b: (nn(a, b), (a, b)), lambda r, g: (nt(g, r[1]), tn(r[0], g)))
    nt.defvjp(lambda a, b: (nt(a, b), (a, b)), lambda r, g: (nn(g, r[1]), tn(g, r[0])))
    tn.defvjp(lambda a, b: (tn(a, b), (a, b)), lambda r, g: (nt(r[1], g), nn(r[0], g)))
    return nn, nt, tn


_mm, _mm_nt, _mm_tn = _make_mm(1, False)
_mm3, _, _ = _make_mm(3, False)
_bmm, _bmm_nt, _bmm_tn = _make_mm(1, True)
_bmm3, _bmm3_nt, _bmm3_tn = _make_mm(3, True)


@jax.custom_vjp
def _known_inverse(a, inv):
    return inv


_known_inverse.defvjp(lambda a, inv: (inv, inv),
                      lambda inv, g: (-_bmm3_tn(inv, _bmm3_nt(g, inv)), jnp.zeros_like(inv)))


def _tri_ones_dot(x, lower):
    n = x.shape[0]
    r = lax.broadcasted_iota(jnp.int32, (n, n), 0)
    c = lax.broadcasted_iota(jnp.int32, (n, n), 1)
    t = ((r >= c) if lower else (r <= c)).astype(BF16)
    hi, lo = _split_bf16(x)
    return jnp.dot(t, hi, preferred_element_type=F32) + jnp.dot(t, lo, preferred_element_type=F32)


@jax.custom_vjp
def _cumsum_rows(x):
    return _tri_ones_dot(x, True)


_cumsum_rows.defvjp(lambda x: (_tri_ones_dot(x, True), None), lambda _, g: (_tri_ones_dot(g, False),))


def _row_tile(n_rows, cap):
    best = LANE
    for t in range(LANE, cap + 1, LANE):
        if n_rows % t == 0:
            best = t
    return best


def _real_rows(tile_index, tm):
    row = tile_index * tm + lax.broadcasted_iota(jnp.int32, (tm, 1), 0)
    return (row >= PAD).astype(F32)


def _full(shape):
    return pl.BlockSpec(shape, lambda *_: (0,) * len(shape))


def _resident(shape):
    return pl.BlockSpec(shape, lambda *_: (0,) * len(shape), pipeline_mode=pl.Buffered(1))


def _resident_w(wmat, widx):
    if wmat.ndim == 2:
        return _resident(wmat.shape)
    return pl.BlockSpec((None,) + wmat.shape[1:], lambda *_: (widx, 0, 0), pipeline_mode=pl.Buffered(1))


def rms_mm(h, w, wmat_t, *, swiglu, name, widx=None):
    tp, d = h.shape
    n = wmat_t.shape[-2]
    tm = _row_tile(tp, 384)
    half = n // 2
    wmat = wmat_t

    def body(h_ref, w_ref, wm_ref, hn_ref, *outs):
        hn = _rms(h_ref[...], w_ref[...]).astype(BF16)
        hn_ref[...] = hn
        p = lax.dot_general(hn, wm_ref[...], (((1,), (1,)), ((), ())), preferred_element_type=F32)
        if swiglu:
            g, u = p[:, :half], p[:, half:]
            outs[0][...] = g.astype(BF16)
            outs[1][...] = u.astype(BF16)
            outs[2][...] = (_silu(g) * u).astype(BF16)
        else:
            outs[0][...] = p

    row = lambda width: pl.BlockSpec((tm, width), lambda i: (i, 0))
    if swiglu:
        out_shape = (jax.ShapeDtypeStruct((tp, d), BF16),) + (jax.ShapeDtypeStruct((tp, half), BF16),) * 3
        out_specs = (row(d), row(half), row(half), row(half))
    else:
        out_shape = (jax.ShapeDtypeStruct((tp, d), BF16), jax.ShapeDtypeStruct((tp, n), F32))
        out_specs = (row(d), row(n))
    return pl.pallas_call(
        body, name=name, grid=(tp // tm,),
        in_specs=[row(d), _full((1, d)), _resident_w(wmat, widx)],
        out_specs=out_specs, out_shape=out_shape,
    )(h, w, wmat)


def mm_rms_res(acts, wmat, h, w, *, scale, name, widx=None):
    tp, d = h.shape
    tm = _row_tile(tp, 384)
    widths = [a.shape[1] for a in acts]
    offs = [sum(widths[:i]) for i in range(len(acts))]
    na = len(acts)

    def body(*refs):
        a_refs = refs[:na]
        wm_ref, h_ref, w_ref, f_ref, ho_ref = refs[na:]
        f = None
        for a_ref, off, width in zip(a_refs, offs, widths):
            part = jnp.dot(a_ref[...].astype(BF16), wm_ref[off:off + width, :], preferred_element_type=F32)
            f = part if f is None else f + part
        f_ref[...] = f
        ho_ref[...] = h_ref[...] + scale * _rms(f, w_ref[...])

    row = lambda width: pl.BlockSpec((tm, width), lambda i: (i, 0))
    return pl.pallas_call(
        body, name=name, grid=(tp // tm,),
        in_specs=[row(wd) for wd in widths] + [_resident_w(wmat, widx), row(d), _full((1, d))],
        out_specs=(row(d), row(d)),
        out_shape=(jax.ShapeDtypeStruct((tp, d), F32), jax.ShapeDtypeStruct((tp, d), F32)),
    )(*acts, wmat, h, w)


def mm_rms_res_bwd(dho, f, w, wmat, gu, *, scale, name, widx=None):
    tp, d = f.shape
    k = wmat.shape[-2]
    tm = _row_tile(tp, 384)
    swiglu = gu is not None

    def body(*refs):
        if swiglu:
            dho_ref, f_ref, w_ref, wm_ref, g_ref, u_ref, df_ref, dw_ref, dgu_ref = refs
        else:
            dho_ref, f_ref, w_ref, wm_ref, df_ref, dw_ref, da_ref = refs
        i = pl.program_id(0)
        _, vjp = jax.vjp(lambda ff, ww: scale * _rms(ff, ww), f_ref[...], w_ref[...])
        df, dw = vjp(dho_ref[...])
        dfb = (df * _real_rows(i, tm)).astype(BF16)
        df_ref[...] = dfb

        @pl.when(i == 0)
        def _():
            dw_ref[...] = jnp.zeros_like(dw_ref)

        dw_ref[...] += dw
        da = lax.dot_general(dfb, wm_ref[...], (((1,), (1,)), ((), ())), preferred_element_type=F32)
        if swiglu:
            g, u, dab = g_ref[...], u_ref[...], da.astype(BF16)
            s = _sigmoid(g)
            dgu_ref[:, :k] = dab * u * s * (1.0 + g * (1.0 - s))
            dgu_ref[:, k:] = dab * g * s
        else:
            da_ref[...] = da

    row = lambda width: pl.BlockSpec((tm, width), lambda i: (i, 0))
    in_specs = [row(d), row(d), _full((1, d)), _resident_w(wmat, widx)]
    args = [dho, f, w, wmat]
    out_shape = [jax.ShapeDtypeStruct((tp, d), BF16), jax.ShapeDtypeStruct((1, d), F32)]
    out_specs = [row(d), _full((1, d))]
    if swiglu:
        in_specs += [row(k), row(k)]
        args += list(gu)
        out_shape += [jax.ShapeDtypeStruct((tp, 2 * k), BF16)]
        out_specs += [row(2 * k)]
    else:
        out_shape += [jax.ShapeDtypeStruct((tp, k), F32)]
        out_specs += [row(k)]
    return pl.pallas_call(body, name=name, grid=(tp // tm,), in_specs=in_specs, out_specs=tuple(out_specs),
                          out_shape=tuple(out_shape))(*args)


def rms_mm_bwd(dps, wmat, h, w, dho, *, name, widx=None):
    tp, d = h.shape
    tm = _row_tile(tp, 384)
    widths = [p.shape[1] for p in dps]
    offs = [sum(widths[:i]) for i in range(len(dps))]
    ndp = len(dps)

    def body(*refs):
        dp_refs = refs[:ndp]
        wm_ref, h_ref, w_ref, dho_ref, dh_ref, dw_ref = refs[ndp:]
        i = pl.program_id(0)
        dhn = None
        for dp_ref, off, width in zip(dp_refs, offs, widths):
            part = jnp.dot(dp_ref[...].astype(BF16), wm_ref[off:off + width, :], preferred_element_type=F32)
            dhn = part if dhn is None else dhn + part
        _, vjp = jax.vjp(_rms, h_ref[...], w_ref[...])
        dx, dw = vjp(dhn)
        dh_ref[...] = (dho_ref[...] + dx) * _real_rows(i, tm)

        @pl.when(i == 0)
        def _():
            dw_ref[...] = jnp.zeros_like(dw_ref)

        dw_ref[...] += dw

    row = lambda width: pl.BlockSpec((tm, width), lambda i: (i, 0))
    return pl.pallas_call(
        body, name=name, grid=(tp // tm,),
        in_specs=[row(wd) for wd in widths] + [_resident_w(wmat, widx), row(d), _full((1, d)), row(d)],
        out_specs=(row(d), _full((1, d))),
        out_shape=(jax.ShapeDtypeStruct((tp, d), F32), jax.ShapeDtypeStruct((1, d), F32)),
    )(*dps, wmat, h, w, dho)


def mm_tn(a, b, *, name, out_dtype=F32, after=None):
    t, m = a.shape
    n = b.shape[1]
    bm = _row_tile(m, 1408 if n <= 1024 else 512)
    bn = _row_tile(n, 1536)
    bk = _row_tile(t, 1408)
    nk = t // bk
    ties = [] if after is None else [after]

    def body(a_ref, b_ref, *rest):
        o_ref, acc = rest[-2:]

        @pl.when(pl.program_id(2) == 0)
        def _():
            acc[...] = jnp.zeros_like(acc)

        acc[...] += lax.dot_general(a_ref[...].astype(BF16), b_ref[...].astype(BF16), (((0,), (0,)), ((), ())),
                                    preferred_element_type=F32)

        @pl.when(pl.program_id(2) == nk - 1)
        def _():
            o_ref[...] = acc[...].astype(o_ref.dtype)

    return pl.pallas_call(
        body, name=name, grid=(m // bm, n // bn, nk),
        in_specs=[pl.BlockSpec((bk, bm), lambda i, j, kk: (kk, i)), pl.BlockSpec((bk, bn), lambda i, j, kk: (kk, j))]
        + [pl.BlockSpec(memory_space=pl.ANY)] * len(ties),
        out_specs=pl.BlockSpec((bm, bn), lambda i, j, kk: (i, j)),
        out_shape=jax.ShapeDtypeStruct((m, n), out_dtype), scratch_shapes=[pltpu.VMEM((bm, bn), F32)],
    )(a, b, *ties)


def loss_and_grad(h, target, *, name):
    tp, d = h.shape
    tm = SWA_BLOCK

    def body(h_ref, t_ref, dh_ref, loss_ref):
        i = pl.program_id(0)

        @pl.when(i == 0)
        def _():
            loss_ref[...] = jnp.zeros_like(loss_ref)
            dh_ref[...] = jnp.zeros_like(dh_ref)

        @pl.when(i > 0)
        def _():
            err = h_ref[...] - t_ref[...]
            dh_ref[...] = err * (1.0 / d)
            loss_ref[...] += 0.5 * jnp.sum(jnp.sum(err * err, axis=1, keepdims=True), axis=0, keepdims=True) * (1.0 / d)

    return pl.pallas_call(
        body, name=name, grid=(tp // tm,),
        in_specs=[pl.BlockSpec((tm, d), lambda i: (i, 0)), pl.BlockSpec((tm, d), lambda i: (jnp.maximum(i - 1, 0), 0))],
        out_specs=(pl.BlockSpec((tm, d), lambda i: (i, 0)), _full((1, 1))),
        out_shape=(jax.ShapeDtypeStruct((tp, d), F32), jax.ShapeDtypeStruct((1, 1), F32)),
    )(h, target)


def _t5_bucket_np(rel):
    n = np.maximum(rel, 0)
    max_exact = REL_BUCKETS // 2
    n_f = np.maximum(n, 1).astype(np.float32)
    large = max_exact + (np.log(n_f / np.float32(max_exact)) / np.float32(math.log(REL_MAX_DIST / max_exact))
                         * np.float32(REL_BUCKETS - max_exact)).astype(np.int32)
    large = np.minimum(large, REL_BUCKETS - 1)
    return np.where(n < max_exact, n, large).astype(np.int32)


def _swa_positions_np(n):
    i = np.arange(SWA_BLOCK)[:, None]
    j = np.arange(3 * SWA_BLOCK)[None, :]
    pos_q = n * SWA_BLOCK + i - PAD
    pos_k = np.where(j < SWA_BLOCK, j - PAD, (n - 1) * SWA_BLOCK + (j - SWA_BLOCK) - PAD)
    return pos_q, pos_k


def _swa_buckets():
    out = []
    for n in range(3):
        pos_q, pos_k = _swa_positions_np(n)
        out.append(_t5_bucket_np(pos_q - pos_k))
    return jnp.asarray(np.stack(out))


def swa_bias(table, buckets, *, name):
    nc, nq, nk = buckets.shape

    def body(tab_ref, bkt_ref, out_ref):
        for c in range(nc):
            bkt = bkt_ref[c]
            for h in range(SWA_Q_HEADS):
                acc = jnp.zeros((nq, nk), F32)
                for b in range(REL_BUCKETS):
                    acc = jnp.where(bkt == b, tab_ref[b, h], acc)
                out_ref[c, h] = acc

    return pl.pallas_call(
        body, name=name,
        in_specs=[pl.BlockSpec(memory_space=pltpu.SMEM), pl.BlockSpec(memory_space=pltpu.VMEM)],
        out_specs=pl.BlockSpec(memory_space=pltpu.VMEM),
        out_shape=jax.ShapeDtypeStruct((nc, SWA_Q_HEADS, nq, nk), F32),
    )(table, buckets)


def swa_bias_bwd(dbias, buckets, *, name):
    nc = buckets.shape[0]

    def body(db_ref, bkt_ref, out_ref):
        lane = lax.broadcasted_iota(jnp.int32, (1, LANE), 1)
        for b in range(REL_BUCKETS):
            row = jnp.zeros((1, LANE), F32)
            for c in range(nc):
                hit = bkt_ref[c] == b
                for h in range(SWA_Q_HEADS):
                    part = jnp.where(hit, db_ref[c, h], 0.0)
                    tot = jnp.sum(jnp.sum(part, axis=1, keepdims=True), axis=0, keepdims=True)
                    row = row + jnp.where(lane == h, tot, 0.0)
            out_ref[b:b + 1, :] = row

    return pl.pallas_call(
        body, name=name,
        in_specs=[pl.BlockSpec(memory_space=pltpu.VMEM), pl.BlockSpec(memory_space=pltpu.VMEM)],
        out_specs=pl.BlockSpec(memory_space=pltpu.VMEM),
        out_shape=jax.ShapeDtypeStruct((REL_BUCKETS, LANE), F32),
    )(dbias, buckets)


def _swa_block(q, kvm, kvp, kvc, bias, sinks, n, batched):
    blk = SWA_BLOCK
    i = lax.broadcasted_iota(jnp.int32, (blk, 3 * blk), 0)
    j = lax.broadcasted_iota(jnp.int32, (blk, 3 * blk), 1)
    pos_q = n * blk + i - PAD
    is_meta = j < blk
    pos_k = jnp.where(is_meta, j - PAD, (n - 1) * blk + (j - blk) - PAD)
    rel = pos_q - pos_k
    valid = ((is_meta & (pos_k >= 0) & (pos_k < N_META) & (rel >= 0))
             | (jnp.logical_not(is_meta) & (pos_k >= N_META) & (rel >= 0) & (rel < SWA_WINDOW)))
    valid_f = valid.astype(F32)
    kv =jnp.concatenate([kvm, kvp, kvc], axis=0)
    lane = lax.broadcasted_iota(jnp.int32, (1, LANE), 1)
    halves = ((lane < SWA_HEAD_DIM).astype(F32), (lane >= SWA_HEAD_DIM).astype(F32))
    nh, group = SWA_Q_HEADS, SWA_Q_HEADS // 2
    q_of = lambda h: q[:, (h // 2) * LANE:(h // 2 + 1) * LANE] * halves[h % 2]
    k_of = lambda h: kv[:, (h // group) * LANE:(h // group + 1) * LANE]
    v_of = lambda h: kv[:, (2 + h // group) * LANE:(3 + h // group) * LANE]
    sink_of = lambda h: jnp.sum(jnp.where(lane == h, sinks, 0.0), axis=1, keepdims=True)

    scale = SWA_HEAD_DIM ** -0.5

    def attend(logits, sink, pv):
        if batched:
            s = logits * valid_f + (valid_f - 1.0) * (-NEG_INF)
        else:
            s = jnp.where(valid, logits, NEG_INF)
        m =lax.stop_gradient(jnp.maximum(jnp.max(s, axis=-1, keepdims=True), sink))
        e = jnp.exp(s - m)
        return pv(e / (jnp.sum(e, axis=-1, keepdims=True) + jnp.exp(sink - m)))

    if batched:
        heads = range(nh)
        vh = _stack([v_of(h) for h in heads])
        qk = _bmm_nt(_stack([q_of(h) for h in heads]), _stack([k_of(h) for h in heads]))
        o = attend(qk * scale + bias, _stack([sink_of(h) for h in heads]), lambda p: _bmm(p, vh))
        head = lambda h: o[h]
    else:
        head = lambda h: attend(_mm_nt(q_of(h), k_of(h)) * scale + bias[h], sink_of(h), lambda p: _mm(p, v_of(h)))
    return jnp.concatenate([head(2 * p) * halves[0] + head(2 * p + 1) * halves[1] for p in range(nh // 2)], axis=1)


def _swa_in_specs(nb, rev):
    blk = SWA_BLOCK
    step = (lambda i: nb - 1 - i) if rev else (lambda i: i)
    return [
        pl.BlockSpec((blk, 4 * LANE), lambda i: (step(i), 0)),
        pl.BlockSpec((blk, 4 * LANE), lambda i: (0, 1)),
        pl.BlockSpec((blk, 4 * LANE), lambda i: (jnp.maximum(step(i) - 1, 0), 1)),
        pl.BlockSpec((blk, 4 * LANE), lambda i: (step(i), 1)),
        pl.BlockSpec((1, SWA_Q_HEADS, blk, 3 * blk), lambda i: (jnp.minimum(step(i), 2), 0, 0, 0)),
        _full((1, LANE)),
    ]


def swa_fwd(proj, bias, sinks, *, name):
    tp = proj.shape[0]
    nb = tp // SWA_BLOCK

    def body(q_ref, kvm_ref, kvp_ref, kvc_ref, bias_ref, sinks_ref, o_ref):
        n = pl.program_id(0)
        o_ref[...] = _swa_block(q_ref[...], kvm_ref[...], kvp_ref[...], kvc_ref[...], bias_ref[0], sinks_ref[...], n, True)

    return pl.pallas_call(
        body, name=name, grid=(nb,),
        in_specs=_swa_in_specs(nb, False),
        out_specs=pl.BlockSpec((SWA_BLOCK, 4 * LANE), lambda i: (i, 0)),
        out_shape=jax.ShapeDtypeStruct((tp, 4 * LANE), F32),
    )(proj, proj, proj, proj, bias, sinks)


def swa_bwd(proj, bias, sinks, do, *, name):
    tp = proj.shape[0]
    nb = tp // SWA_BLOCK
    blk = SWA_BLOCK

    def body(q_ref, kvm_ref, kvp_ref, kvc_ref, bias_ref, sinks_ref, do_ref, dqkv_ref, dbias_ref, dsinks_ref,
             carry, meta_acc):
        i = pl.program_id(0)
        n = nb - 1 - i

        @pl.when(i == 0)
        def _():
            carry[...] = jnp.zeros_like(carry)
            meta_acc[...] = jnp.zeros_like(meta_acc)
            dsinks_ref[...] = jnp.zeros_like(dsinks_ref)

        fn = lambda q, kvm, kvp, kvc, b, s: _swa_block(q, kvm, kvp, kvc, b, s, n, False)
        _, vjp = jax.vjp(fn, q_ref[...], kvm_ref[...], kvp_ref[...], kvc_ref[...], bias_ref[0], sinks_ref[...])
        dq, dkvm, dkvp, dkvc, dbias, dsinks = vjp(do_ref[...])
        dqkv_ref[:, :4 * LANE] = dq.astype(BF16)
        meta_acc[...] += dkvm
        dqkv_ref[:, 4 * LANE:] = (dkvc + carry[...] + jnp.where(n == 0, meta_acc[...], 0.0)).astype(BF16)
        carry[...] = dkvp
        first_visit = (n == nb - 1) | (n < 2)

        @pl.when(first_visit)
        def _():
            dbias_ref[0] = dbias

        @pl.when(jnp.logical_not(first_visit))
        def _():
            dbias_ref[0] += dbias

        dsinks_ref[...] += dsinks

    rev = lambda i: nb - 1 - i
    return pl.pallas_call(
        body, name=name, grid=(nb,),
        in_specs=_swa_in_specs(nb, True) + [pl.BlockSpec((blk, 4 * LANE), lambda i: (rev(i), 0))],
        out_specs=(pl.BlockSpec((blk, 8 * LANE), lambda i: (rev(i), 0)),
                   pl.BlockSpec((1, SWA_Q_HEADS, blk, 3 * blk), lambda i: (jnp.minimum(rev(i), 2), 0, 0, 0)),
                   _full((1, LANE))),
        out_shape=(jax.ShapeDtypeStruct((tp, PROJ_DIM), BF16),
                   jax.ShapeDtypeStruct((3, SWA_Q_HEADS, blk, 3 * blk), F32), jax.ShapeDtypeStruct((1, LANE), F32)),
        scratch_shapes=[pltpu.VMEM((blk, 4 * LANE), F32), pltpu.VMEM((blk, 4 * LANE), F32)],
    )(proj, proj, proj, proj, bias, sinks, do)


CONV_COL0 = 2
HALO = 8


def conv_fwd(proj, conv_w, *, name):
    tp = proj.shape[0]
    tm = _row_tile(tp, 384)
    cw = 4 * LANE
    ncol = conv_w.shape[1] // cw

    def body(x_ref, halo_ref, w_ref, y_ref, buf):
        i = pl.program_id(1)
        buf[0:HALO, :] = jnp.where(i > 0, halo_ref[...], 0.0)
        buf[HALO:, :] = x_ref[...]
        acc = None
        for j in range(DN_CONV):
            term = w_ref[j:j + 1, :] * buf[pl.ds(HALO - (DN_CONV - 1) + j, tm), :]
            acc = term if acc is None else acc + term
        y_ref[...] = acc

    return pl.pallas_call(
        body, name=name, grid=(ncol, tp // tm),
        in_specs=[pl.BlockSpec((tm, cw), lambda c, i: (i, CONV_COL0 + c)),
                  pl.BlockSpec((HALO, cw), lambda c, i: (jnp.maximum(i * (tm // HALO) - 1, 0), CONV_COL0 + c)),
                  pl.BlockSpec((DN_CONV, cw), lambda c, i: (0, c))],
        out_specs=pl.BlockSpec((tm, cw), lambda c, i: (i, c)),
        out_shape=jax.ShapeDtypeStruct((tp, ncol * cw), F32),
        scratch_shapes=[pltpu.VMEM((tm + HALO, cw), F32)],
    )(proj, proj, conv_w)


def conv_bwd(proj, conv_w, dy, dproj, *, name):
    tp = proj.shape[0]
    tm = _row_tile(tp, 384)
    cw = 4 * LANE
    ncol = conv_w.shape[1] // cw
    nt = tp // tm

    def body(x_ref, xhalo_ref, w_ref, dy_ref, dyhalo_ref, _, dx_ref, dw_ref, xbuf, dbuf):
        i = pl.program_id(1)
        xbuf[0:HALO, :] = jnp.where(i > 0, xhalo_ref[...], 0.0)
        xbuf[HALO:, :] = x_ref[...]
        dbuf[0:tm, :] = dy_ref[...]
        dbuf[tm:, :] = jnp.where(i < nt - 1, dyhalo_ref[...], 0.0)
        dy_t = dy_ref[...]
        acc = None
        rows = []
        for j in range(DN_CONV):
            term = w_ref[j:j + 1, :] * dbuf[pl.ds(DN_CONV - 1 - j, tm), :]
            acc = term if acc is None else acc + term
            rows.append(jnp.sum(dy_t * xbuf[pl.ds(HALO - (DN_CONV - 1) + j, tm), :], axis=0, keepdims=True))
        dx_ref[...] = acc.astype(BF16)

        @pl.when(i == 0)
        def _():
            dw_ref[...] = jnp.zeros_like(dw_ref)

        for j in range(DN_CONV):
            dw_ref[j:j + 1, :] += rows[j]

    return pl.pallas_call(
        body, name=name, grid=(ncol, nt),
        in_specs=[pl.BlockSpec((tm, cw), lambda c, i: (i, CONV_COL0 + c)),
                  pl.BlockSpec((HALO, cw), lambda c, i: (jnp.maximum(i * (tm // HALO) - 1, 0), CONV_COL0 + c)),
                  pl.BlockSpec((DN_CONV, cw), lambda c, i: (0, c)),
                  pl.BlockSpec((tm, cw), lambda c, i: (i, c)),
                  pl.BlockSpec((HALO, cw), lambda c, i: (jnp.minimum((i + 1) * (tm // HALO), tp // HALO - 1), c)),
                  pl.BlockSpec(memory_space=pl.ANY)],
        out_specs=(pl.BlockSpec((tm, cw), lambda c, i: (i, CONV_COL0 + c)), pl.BlockSpec((DN_CONV, cw), lambda c, i: (0, c))),
        out_shape=(jax.ShapeDtypeStruct(dproj.shape, dproj.dtype), jax.ShapeDtypeStruct((DN_CONV, ncol * cw), F32)),
        scratch_shapes=[pltpu.VMEM((tm + HALO, cw), F32), pltpu.VMEM((tm + HALO, cw), F32)],
        input_output_aliases={5: 0},
    )(proj, proj, conv_w, dy, dy, dproj)


def _stack(parts):
    return jnp.concatenate([p[None] for p in parts], axis=0)


def _chunk_masks():
    r = lax.broadcasted_iota(jnp.int32, (CHUNK, CHUNK), 0)
    c = lax.broadcasted_iota(jnp.int32, (CHUNK, CHUNK), 1)
    return (r >= c).astype(F32), (r > c).astype(F32), (r == c).astype(F32)


def _dn_chunk(y, z, small, s, a_log, dt_bias, norm_w, rows, known_inv=None):
    tri_incl, tri_strict, eye = _chunk_masks()
    lane = lax.broadcasted_iota(jnp.int32, (1, LANE), 1)
    dk = DN_HEAD_DIM
    nh = DN_HEADS
    heads = lambda t, first: _stack([t[:, (first + h) * dk:(first + h + 1) * dk] for h in range(nh)])
    pick = lambda t, l: jnp.sum(jnp.where(lane == l, t, 0.0), axis=1, keepdims=True)
    q = _l2n(_silu(heads(y, 0))) * dk ** -0.5
    k = _l2n(_silu(heads(y, nh)))
    v = _silu(heads(y, 2 * nh))
    g_all = jnp.where(lane < nh, -jnp.exp(a_log) * _softplus(small + dt_bias), 0.0) * rows
    beta_all = _sigmoid(small)
    gc_all = _cumsum_rows(g_all)
    g_sum = jnp.sum(g_all, axis=0, keepdims=True)
    gc = _stack([pick(gc_all, h) for h in range(nh)])
    beta = _stack([pick(beta_all, nh + h) for h in range(nh)])
    g_last = _stack([pick(g_sum, h) for h in range(nh)])
    gc_row = jnp.sum(eye * gc, axis=1, keepdims=True)
    gamma = jnp.exp((gc - gc_row) * tri_incl) * tri_incl
    k_beta = k * beta
    v_beta = v * beta
    a = _bmm_nt(k_beta, k) * gamma * tri_strict
    if known_inv is None:
        inv = eye - a
        power = a
        for _ in range(5):
            power = _bmm3(power, power)
            inv = inv + _bmm3(inv, power)
    else:
        inv = _known_inverse(a, known_inv)
    e_gc = jnp.exp(gc)
    uw = _bmm3(inv, jnp.concatenate([v_beta, k_beta * e_gc], axis=2))
    u, w = uw[:, :, :dk], uw[:, :, dk:]
    attn = _bmm_nt(q, k) * gamma
    q_dec = q * e_gc
    k_dec = k * jnp.exp(g_last - gc)
    v_new = u - _bmm(w, s)
    o = _bmm(q_dec, s) + _bmm(attn, v_new)
    s_new = s * jnp.exp(g_last) + _bmm_tn(k_dec, v_new)
    out = _rms(o, norm_w) * _silu(heads(z, 0))
    return jnp.concatenate([out[h] for h in range(nh)], axis=1), s_new, inv


Z_COL = 5
SMALL_COL = 24


def _chunk_rows(n):
    row = n * CHUNK + lax.broadcasted_iota(jnp.int32, (CHUNK, 1), 0)
    return (row >= PAD).astype(F32)


def dn_fwd(y, proj, a_log, dt_bias, norm_w, *, name):
    tp = y.shape[0]
    nc = tp // CHUNK
    dk = DN_HEAD_DIM
    per = CHUNKS_PER_STEP
    rows = per * CHUNK

    def body(y_ref, z_ref, small_ref, al_ref, dt_ref, nw_ref, o_ref, ssave_ref, isave_ref, state):
        n = pl.program_id(0)

        @pl.when(n == 0)
        def _():
            state[...] = jnp.zeros_like(state)

        s = state[...]
        for c in range(per):
            at = pl.ds(c * CHUNK, CHUNK)
            ssave_ref[c] = s
            out, s, inv = _dn_chunk(y_ref[at, :], z_ref[at, :], small_ref[at, :], s, al_ref[...], dt_ref[...],
                                    nw_ref[...], _chunk_rows(per * n + c))
            o_ref[at, :] = out
            isave_ref[c] = inv
        state[...] = s

    return pl.pallas_call(
        body, name=name, grid=(nc // per,),
        in_specs=[pl.BlockSpec((rows, y.shape[1]), lambda n: (n, 0)),
                  pl.BlockSpec((rows, 4 * LANE), lambda n: (n, Z_COL)),
                  pl.BlockSpec((rows, LANE), lambda n: (n, SMALL_COL)),
                  _full((1, LANE)), _full((1, LANE)), _full((1, LANE))],
        out_specs=(pl.BlockSpec((rows, 4 * LANE), lambda n: (n, 0)),
                   pl.BlockSpec((per, DN_HEADS, dk, dk), lambda n: (n, 0, 0, 0)),
                   pl.BlockSpec((per, DN_HEADS, CHUNK, CHUNK), lambda n: (n, 0, 0, 0))),
        out_shape=(jax.ShapeDtypeStruct((tp, 4 * LANE), F32), jax.ShapeDtypeStruct((nc, DN_HEADS, dk, dk), F32),
                   jax.ShapeDtypeStruct((nc, DN_HEADS, CHUNK, CHUNK), F32)),
        scratch_shapes=[pltpu.VMEM((DN_HEADS, dk, dk), F32)],
    )(y, proj, proj, a_log, dt_bias, norm_w)


def dn_bwd(y, proj, a_log, dt_bias, norm_w, ssave, isave, do, dproj, *, name):
    tp = y.shape[0]
    nc = tp // CHUNK
    dk = DN_HEAD_DIM
    per = CHUNKS_PER_STEP
    rev = lambda i: nc // per - 1 - i
    zs_width = 5 * LANE

    def body(y_ref, z_ref, small_ref, al_ref, dt_ref, nw_ref, ss_ref, is_ref, do_ref, _,
             dy_ref, dzs_ref, dal_ref, ddt_ref, dnw_ref, dstate):
        i = pl.program_id(0)
        n = nc // per - 1 - i

        @pl.when(i == 0)
        def _():
            dstate[...] = jnp.zeros_like(dstate)
            dal_ref[...] = jnp.zeros_like(dal_ref)
            ddt_ref[...] = jnp.zeros_like(ddt_ref)
            dnw_ref[...] = jnp.zeros_like(dnw_ref)

        ds = dstate[...]
        for c in reversed(range(per)):
            at = pl.ds(c * CHUNK, CHUNK)
            token_rows = _chunk_rows(per * n + c)
            known_inv = is_ref[c]
            fn = lambda *a: _dn_chunk(*a, token_rows, known_inv)[:2]
            _, vjp = jax.vjp(fn, y_ref[at, :], z_ref[at, :], small_ref[at, :], ss_ref[c], al_ref[...], dt_ref[...],
                             nw_ref[...])
            dy, dz, dsmall, ds, dal, ddt, dnw = vjp((do_ref[at, :], ds))
            dy_ref[at, :] = dy
            dzs_ref[at, :4 * LANE] = dz.astype(BF16)
            dzs_ref[at, 4 * LANE:] = dsmall.astype(BF16)
            dal_ref[...] += dal
            ddt_ref[...] += ddt
            dnw_ref[...] += dnw
        dstate[...] = ds

    rows = per * CHUNK
    return pl.pallas_call(
        body, name=name, grid=(nc // per,),
        in_specs=[pl.BlockSpec((rows, y.shape[1]), lambda i: (rev(i), 0)),
                  pl.BlockSpec((rows, 4 * LANE), lambda i: (rev(i), Z_COL)),
                  pl.BlockSpec((rows, LANE), lambda i: (rev(i), SMALL_COL)),
                  _full((1, LANE)), _full((1, LANE)), _full((1, LANE)),
                  pl.BlockSpec((per, DN_HEADS, dk, dk), lambda i: (rev(i), 0, 0, 0)),
                  pl.BlockSpec((per, DN_HEADS, CHUNK, CHUNK), lambda i: (rev(i), 0, 0, 0)),
                  pl.BlockSpec((rows, 4 * LANE), lambda i: (rev(i), 1)),
                  pl.BlockSpec(memory_space=pl.ANY)],
        out_specs=(pl.BlockSpec((rows, y.shape[1]), lambda i: (rev(i), 0)),
                   pl.BlockSpec((rows, zs_width), lambda i: (rev(i), Z_COL * 4 * LANE // zs_width)),
                   _full((1, LANE)), _full((1, LANE)), _full((1, LANE))),
        out_shape=(jax.ShapeDtypeStruct((tp, y.shape[1]), F32), jax.ShapeDtypeStruct(dproj.shape, dproj.dtype),
                   jax.ShapeDtypeStruct((1, LANE), F32), jax.ShapeDtypeStruct((1, LANE), F32),
                   jax.ShapeDtypeStruct((1, LANE), F32)),
        scratch_shapes=[pltpu.VMEM((DN_HEADS, dk, dk), F32)],
        input_output_aliases={9: 1},
    )(y, proj, proj, a_log, dt_bias, norm_w, ssave, isave, do, dproj)


def _gla_chunk(q, k, v, gate, low, s, w_gate_up, b_gate, norm_w, rows):
    tri_incl, _, _ = _chunk_masks()
    dk, dv, nh = GLA_DK, GLA_DV, GLA_HEADS
    heads = lambda t, width: _stack([t[:, h * width:(h + 1) * width] for h in range(nh)])
    logit = _mm3(low, w_gate_up) + b_gate
    glog_all = -_softplus(-logit) * (1.0 / GLA_GATE_NORM) * rows
    glog = heads(glog_all, dk)
    bcum = heads(_cumsum_rows(glog_all), dk)
    qh = heads(q, dk) * dk ** -0.5
    kh = heads(k, dk)
    vh = heads(v, dv)
    q_dec = qh * jnp.exp(bcum)
    attn = _bmm_nt(q_dec, kh * jnp.exp(-bcum)) * tri_incl
    b_last = jnp.sum(glog, axis=1, keepdims=True)
    k_dec = kh * jnp.exp(b_last - bcum)
    r = lax.broadcasted_iota(jnp.int32, (dk, dk), 0)
    c = lax.broadcasted_iota(jnp.int32, (dk, dk), 1)
    b_last_col = jnp.sum((r == c).astype(F32) * b_last, axis=2, keepdims=True)
    o = _bmm(attn, vh) + _bmm(q_dec, s)
    s_new = s * jnp.exp(b_last_col) + _bmm_tn(k_dec, vh)
    out = _rms(o, norm_w) * _silu(heads(gate, dv))
    return jnp.concatenate([out[h] for h in range(nh)], axis=1), s_new


LOW_COL = 24


def _gla_in_specs(step, rows=CHUNK):
    return [pl.BlockSpec((rows, 4 * LANE), lambda i: (step(i), 0)),
            pl.BlockSpec((rows, 4 * LANE), lambda i: (step(i), 1)),
            pl.BlockSpec((rows, 8 * LANE), lambda i: (step(i), 1)),
            pl.BlockSpec((rows, 8 * LANE), lambda i: (step(i), 2)),
            pl.BlockSpec((rows, LANE), lambda i: (step(i), LOW_COL)),
            _full((LANE, 4 * LANE)), _full((1, 4 * LANE)), _full((1, GLA_DV))]


def gla_fwd(proj, w_gate_up, b_gate, norm_w, *, name):
    tp = proj.shape[0]
    nc = tp // CHUNK
    per = CHUNKS_PER_STEP
    rows = per * CHUNK

    def body(q_ref, k_ref, v_ref, g_ref, low_ref, wgu_ref, bg_ref, nw_ref, o_ref, ssave_ref, state):
        n = pl.program_id(0)

        @pl.when(n == 0)
        def _():
            state[...] = jnp.zeros_like(state)

        s = state[...]
        for c in range(per):
            at = pl.ds(c * CHUNK, CHUNK)
            ssave_ref[c] = s
            out, s = _gla_chunk(q_ref[at, :], k_ref[at, :], v_ref[at, :], g_ref[at, :], low_ref[at, :], s, wgu_ref[...],
                                bg_ref[...], nw_ref[...], _chunk_rows(per * n + c))
            o_ref[at, :] = out
        state[...] = s

    return pl.pallas_call(
        body, name=name, grid=(nc // per,),
        in_specs=_gla_in_specs(lambda i: i, rows),
        out_specs=(pl.BlockSpec((rows, 8 * LANE), lambda n: (n, 0)),
                   pl.BlockSpec((per, GLA_HEADS, GLA_DK, GLA_DV), lambda n: (n, 0, 0, 0))),
        out_shape=(jax.ShapeDtypeStruct((tp, 8 * LANE), F32),
                   jax.ShapeDtypeStruct((nc, GLA_HEADS, GLA_DK, GLA_DV), F32)),
        scratch_shapes=[pltpu.VMEM((GLA_HEADS, GLA_DK, GLA_DV), F32)],
    )(proj, proj, proj, proj, proj, w_gate_up, b_gate, norm_w)


def gla_bwd(proj, w_gate_up, b_gate, norm_w, ssave, do, *, name):
    tp = proj.shape[0]
    nc = tp // CHUNK
    per = CHUNKS_PER_STEP
    rev = lambda i: nc // per - 1 - i

    def body(q_ref, k_ref, v_ref, g_ref, low_ref, wgu_ref, bg_ref, nw_ref, ss_ref, do_ref,
             dproj_ref, dwgu_ref, dbg_ref, dnw_ref, dstate):
        i = pl.program_id(0)
        n = nc // per - 1 - i

        @pl.when(i == 0)
        def _():
            dstate[...] = jnp.zeros_like(dstate)
            dwgu_ref[...] = jnp.zeros_like(dwgu_ref)
            dbg_ref[...] = jnp.zeros_like(dbg_ref)
            dnw_ref[...] = jnp.zeros_like(dnw_ref)

        ds = dstate[...]
        for c in reversed(range(per)):
            at = pl.ds(c * CHUNK, CHUNK)
            token_rows = _chunk_rows(per * n + c)
            fn = lambda *a: _gla_chunk(*a, token_rows)
            _, vjp = jax.vjp(fn, q_ref[at, :], k_ref[at, :], v_ref[at, :], g_ref[at, :], low_ref[at, :], ss_ref[c],
                             wgu_ref[...], bg_ref[...], nw_ref[...])
            dq, dk, dv, dg, dlow, ds, dwgu, dbg, dnw = vjp((do_ref[at, :], ds))
            off = 0
            for part in (dq, dk, dv, dg, dlow):
                dproj_ref[at, off:off + part.shape[1]] = part.astype(BF16)
                off += part.shape[1]
            dwgu_ref[...] += dwgu
            dbg_ref[...] += dbg
            dnw_ref[...] += dnw
        dstate[...] = ds

    chunk = lambda width: pl.BlockSpec((per * CHUNK, width), lambda i: (rev(i), 0))
    return pl.pallas_call(
        body, name=name, grid=(nc // per,),
        in_specs=_gla_in_specs(rev, per * CHUNK) + [pl.BlockSpec((per, GLA_HEADS, GLA_DK, GLA_DV), lambda i: (rev(i), 0, 0, 0)),
                                                    chunk(8 * LANE)],
        out_specs=(chunk(PROJ_DIM), _full((LANE, 4 * LANE)), _full((1, 4 * LANE)), _full((1, GLA_DV))),
        out_shape=(jax.ShapeDtypeStruct((tp, PROJ_DIM), BF16), jax.ShapeDtypeStruct((LANE, 4 * LANE), F32),
                   jax.ShapeDtypeStruct((1, 4 * LANE), F32), jax.ShapeDtypeStruct((1, GLA_DV), F32)),
        scratch_shapes=[pltpu.VMEM((GLA_HEADS, GLA_DK, GLA_DV), F32)],
    )(proj, proj, proj, proj, proj, w_gate_up, b_gate, norm_w, ssave, do)


def _even_proj_weight(w_t):
    hd = SWA_HEAD_DIM
    k0, k1 = w_t[512:512 + hd], w_t[512 + hd:640]
    v0, v1 = w_t[640:640 + hd], w_t[640 + hd:768]
    zeros = jnp.zeros((LANE - 2 * DN_HEADS, w_t.shape[1]), w_t.dtype)
    return jnp.concatenate([w_t[:512], k0, k0, k1, k1, v0, v0, v1, v1, w_t[768:2816], w_t[2820:2824], w_t[2816:2820],
                            zeros], axis=0)


def _even_proj_weight_grad(dw):
    hd = SWA_HEAD_DIM
    c = lambda i: dw[512 + i * hd:512 + (i + 1) * hd]
    return jnp.concatenate([dw[:512], c(0) + c(1), c(2) + c(3), c(4) + c(5), c(6) + c(7), dw[1024:3072],
                            dw[3076:3080], dw[3072:3076]], axis=0)


def _ffn_fwd(h, nw_in, nw_out, wts, idx, get_w):
    wts.update(get_w(f"ffn{idx}", h))
    w_gu = wts[f"w_gu{idx}"]
    hn, g, u, a = rms_mm(h, nw_in, w_gu[0], swiglu=True, name=f"ffn_up_{idx}", widx=w_gu[1])
    wts.update(get_w(f"down{idx}", a))
    w_down = wts[f"w_down{idx}"]
    f, h_out = mm_rms_res([a], w_down[0], h, nw_out, scale=0.5, name=f"ffn_down_{idx}", widx=w_down[1])
    return h_out, (h, hn, g, u, a, f)


def _ffn_bwd(dho, saved, nw_in, nw_out, w_gu, w_down, idx, on_grads):
    h, hn, g, u, a, f = saved
    df, dnw_out, dgu = mm_rms_res_bwd(dho, f, nw_out, w_down[0], (g, u), scale=0.5, name=f"ffn_down_bwd_{idx}",
                                      widx=w_down[1])
    g_down = mm_tn(a, df, name=f"ffn_dwd_{idx}", out_dtype=BF16)
    sent = on_grads("down", g_down)
    g_gu = mm_tn(dgu, hn, name=f"ffn_dwgu_{idx}", out_dtype=BF16, after=sent)
    sent = on_grads("gu", g_gu)
    dh, dnw_in = rms_mm_bwd([dgu], w_gu[0], h, nw_in + sent, dho, name=f"ffn_up_bwd_{idx}", widx=w_gu[1])
    return dh, dnw_in, dnw_out


def local_step(x, target, wts, get_w=None, put_g=None):
    seq, d = x.shape
    wts = dict(wts)
    get_w = get_w or (lambda stage, after: {})
    put_g = put_g or (lambda stage, grads: jnp.zeros((1, 1), F32))
    row = lambda v: v.reshape(1, -1)
    lane_row = lambda v: jnp.pad(v.reshape(1, -1), ((0, 0), (0, LANE - v.size)))
    nw = wts["norm_w"]
    h = jnp.concatenate([jnp.zeros((PAD, d), F32), wts["meta_tokens"], x], axis=0)
    buckets = _swa_buckets()
    bias = swa_bias(wts["rel_bias_table"], buckets, name="swa_bias")
    sinks = lane_row(wts["swa_sinks"])
    a_log, dt_bias = lane_row(wts["dn_a_log"]), lane_row(wts["dn_dt_bias"])
    dn_norm_w = row(wts["dn_norm_w"])
    conv_w = wts["even_conv_w"][0]
    w_gate_up = jnp.pad(wts["gla_w_gate_up"][0], ((0, LANE - GLA_GATE_RANK), (0, 0)))
    b_gate, gla_norm_w = row(wts["gla_b_gate"]), row(wts["gla_norm_w"])

    saved = []
    w_in, w_out = [None, None], [None, None]
    for l in range(2):
        h, s_a = _ffn_fwd(h, row(nw[l, 0]), row(nw[l, 1]), wts, 2 * l, get_w)
        if l == 0:
            wts.update(get_w("even", h))
            w_in[0], w_out[0] = _even_proj_weight(wts["even_w_in"]), wts["even_w_out"]
        else:
            wts.update(get_w("odd", h))
            w_in[1] = jnp.pad(wts["odd_w_in"], ((0, PROJ_DIM - wts["odd_w_in"].shape[0]), (0, 0)))
            w_out[1] = wts["odd_w_out"]
        h_mix = h
        hn, proj = rms_mm(h, row(nw[l, 2]), w_in[l], swiglu=False, name=f"mix_in_{l}")
        if l == 0:
            o_a = swa_fwd(proj, bias, sinks, name="swa_fwd")
            y = conv_fwd(proj, conv_w, name="conv_fwd")
            o_b, ssave, isave = dn_fwd(y, proj, a_log, dt_bias, dn_norm_w, name="dn_fwd")
            acts, extra = [o_a, o_b], (y, ssave, isave)
        else:
            o, ssave = gla_fwd(proj, w_gate_up, b_gate, gla_norm_w, name="gla_fwd")
            acts, extra = [o], (ssave,)
        mix, h = mm_rms_res(acts, w_out[l], h, row(nw[l, 3]), scale=1.0, name=f"mix_out_{l}")
        s_m = (h_mix, hn, proj, acts, extra, mix)
        h, s_b = _ffn_fwd(h, row(nw[l, 4]), row(nw[l, 5]), wts, 2 * l + 1, get_w)
        saved.append((s_a, s_m, s_b))

    dh, loss = loss_and_grad(h, target, name="loss")

    grads = {}
    dnw = [[None] * 6 for _ in range(2)]
    def on_grads(i):
        def put(which, g):
            grads[f"g_{which}{i}"] = g
            return put_g(f"{which}{i}", grads)
        return put

    for l in (1, 0):
        s_a, s_m, s_b = saved[l]
        i = 2 * l + 1
        dh, dnw[l][4], dnw[l][5] = _ffn_bwd(dh, s_b, row(nw[l, 4]), row(nw[l, 5]), wts[f"w_gu{i}"], wts[f"w_down{i}"],
                                            i, on_grads(i))
        h_mix, hn, proj, acts, extra, mix = s_m
        dmix, dnw[l][3], do = mm_rms_res_bwd(dh, mix, row(nw[l, 3]), w_out[l], None, scale=1.0, name=f"mix_out_bwd_{l}")
        dw_out = jnp.concatenate([mm_tn(a, dmix, name=f"mix_dwo_{l}_{i}") for i, a in enumerate(acts)], axis=0)
        sent = jnp.zeros((1, 1), F32)
        if l == 0:
            y, ssave, isave = extra
            dproj, dbias, dsinks = swa_bwd(proj, bias, sinks, do, name="swa_bwd")
            dy, dproj, da_log, ddt_bias, ddn_norm_w = dn_bwd(y, proj, a_log, dt_bias, dn_norm_w, ssave, isave, do, dproj,
                                                             name="dn_bwd")
            dproj, dconv_w = conv_bwd(proj, conv_w, dy, dproj, name="conv_bwd")
            grads["rel_bias_table"] = swa_bias_bwd(dbias, buckets, name="swa_bias_bwd")[:, :SWA_Q_HEADS]
            grads["swa_sinks"] = dsinks[:, :SWA_Q_HEADS]
            grads["dn_a_log"] = da_log[:, :DN_HEADS]
            grads["dn_dt_bias"] = ddt_bias[:, :DN_HEADS]
            grads["dn_norm_w"] = ddn_norm_w
            grads["even_conv_w"] = dconv_w[None]
            grads["even_w_out"] = dw_out
        else:
            (ssave,) = extra
            dproj, dwgu, dbg, dgnw = gla_bwd(proj, w_gate_up, b_gate, gla_norm_w, ssave, do, name="gla_bwd")
            grads["gla_w_gate_up"] = dwgu[None, :GLA_GATE_RANK]
            grads["gla_b_gate"] = dbg
            grads["gla_norm_w"] = dgnw
            grads["odd_w_out"] = dw_out
        dw_in = mm_tn(dproj, hn, name=f"mix_dwi_{l}")
        if l == 0:
            grads["even_w_in"] = _even_proj_weight_grad(dw_in)
            sent = put_g("even", grads)
        else:
            grads["odd_w_in"] = dw_in[:wts["odd_w_in"].shape[0]]
        dh, dnw[l][2] = rms_mm_bwd([dproj], w_in[l], h_mix, row(nw[l, 2]) + sent, dh, name=f"mix_in_bwd_{l}")
        i = 2 * l
        dh, dnw[l][0], dnw[l][1] = _ffn_bwd(dh, s_a, row(nw[l, 0]), row(nw[l, 1]), wts[f"w_gu{i}"], wts[f"w_down{i}"],
                                            i, on_grads(i))

    grads["norm_w"] = jnp.stack([jnp.concatenate(r, axis=0) for r in dnw])
    grads["meta_tokens"] = dh[PAD:PAD + N_META]
    return loss[0, 0], dh[PAD + N_META:], grads


def _peer(k):
    x, y, c = (lax.axis_index(a) for a in AXES)
    flip = lambda v, bit: 1 - v if bit else v
    return (flip(x, k & 4), flip(y, k & 2), flip(c, k & 1))


def _my_index():
    x, y, c = (lax.axis_index(a) for a in AXES)
    return 4 * x + 2 * y + c


_HBM = pl.BlockSpec(memory_space=pltpu.HBM)
_SEM = pl.BlockSpec(memory_space=pltpu.SEMAPHORE)
_EFFECT = pltpu.SideEffectType.DATAFLOW_SIDE_EFFECTING


def _remote_copies(items, src_refs, land_refs, send_sems, recv_sems):
    me = _my_index()
    copies = []
    for k in range(1, N_DEV):
        px, py, pc = _peer(k)
        pj = 4 * px + 2 * py + pc
        for a, (sn, send, ln, land, _) in enumerate(items):
            sem = (k - 1) * len(items) + a
            copies.append(pltpu.make_async_remote_copy(
                src_ref=send(src_refs[sn], pj), dst_ref=land(land_refs[ln], me), send_sem=send_sems.at[sem],
                recv_sem=recv_sems.at[sem], device_id=(px, py, pc), device_id_type=MESH))
    return copies


def exchange(srcs, lands, items, after, *, name):
    sn, ln = list(srcs), list(lands)

    def body(*refs):
        src_refs = dict(zip(sn, refs[:len(sn)]))
        land_refs = dict(zip(ln, refs[len(sn) + len(ln) + 1:len(sn) + 2 * len(ln) + 1]))
        send_sems, recv_sems = refs[len(sn) + 2 * len(ln) + 1:]
        copies = _remote_copies(items, src_refs, land_refs, send_sems, recv_sems)
        for cp in copies:
            cp.start()
        for cp in copies:
            cp.wait_recv()
        for cp in copies:
            cp.wait_send()

    n_remote = (N_DEV - 1) * len(items)
    outs = pl.pallas_call(
        body, name=name,
        in_specs=[pl.BlockSpec(memory_space=pl.ANY)] * (len(sn) + len(ln) + 1),
        out_specs=tuple(pl.BlockSpec(memory_space=pl.ANY) for _ in ln),
        out_shape=tuple(jax.ShapeDtypeStruct(lands[n].shape, lands[n].dtype) for n in ln),
        input_output_aliases={len(sn) + i: i for i in range(len(ln))},
        scratch_shapes=[pltpu.SemaphoreType.DMA((n_remote,)), pltpu.SemaphoreType.DMA((n_remote,))],
    )(*[srcs[n] for n in sn], *[lands[n] for n in ln], after)
    return dict(zip(ln, outs))


def start_copies(srcs, lands, items, *, name):
    sn, ln = list(srcs), list(lands)
    n_remote = (N_DEV - 1) * len(items)

    def body(*refs):
        src_refs = dict(zip(sn, refs[:len(sn)]))
        land_refs = dict(zip(ln, refs[len(sn):len(sn) + len(ln)]))
        send_sems, recv_sems = refs[len(sn) + len(ln):len(sn) + len(ln) + 2]
        token = refs[-1]
        for cp in _remote_copies(items, src_refs, land_refs, send_sems, recv_sems):
            cp.start()
        token[...] = jnp.zeros_like(token)

    hbm = lambda a: pltpu.with_memory_space_constraint(a, pltpu.HBM)
    outs = pl.pallas_call(
        body, name=name,
        in_specs=[_HBM] * (len(sn) + len(ln)),
        out_specs=(_SEM, _SEM) + (_HBM,) * len(ln) + (pl.BlockSpec(memory_space=pltpu.VMEM),),
        out_shape=(pltpu.SemaphoreType.DMA((n_remote,)), pltpu.SemaphoreType.DMA((n_remote,)))
        + tuple(pltpu.HBM(lands[n].shape, lands[n].dtype) for n in ln) + (jax.ShapeDtypeStruct((8, LANE), F32),),
        input_output_aliases={len(sn) + i: 2 + i for i in range(len(ln))},
        compiler_params=pltpu.CompilerParams(has_side_effects=_EFFECT),
    )(*[hbm(srcs[n]) for n in sn], *[hbm(lands[n]) for n in ln])
    return (outs[0], outs[1]), dict(zip(ln, outs[2:2 + len(ln)])), outs[-1][0:1, 0:1]


def wait_copies(sems, srcs, lands, items, after, *, name):
    sn, ln = list(srcs), list(lands)

    def body(*refs):
        src_refs = dict(zip(sn, refs[:len(sn)]))
        land_refs = dict(zip(ln, refs[len(sn):len(sn) + len(ln)]))
        send_sems, recv_sems = refs[len(sn) + len(ln):len(sn) + len(ln) + 2]
        copies = _remote_copies(items, src_refs, land_refs, send_sems, recv_sems)
        for cp in copies:
            cp.wait_send()
        for cp in copies:
            cp.wait_recv()

    outs = pl.pallas_call(
        body, name=name,
        in_specs=[_HBM] * (len(sn) + len(ln)) + [_SEM, _SEM, pl.BlockSpec(memory_space=pl.ANY)],
        out_specs=(_HBM,) * len(ln),
        out_shape=tuple(pltpu.HBM(lands[n].shape, lands[n].dtype) for n in ln),
        input_output_aliases={len(sn) + i: i for i in range(len(ln))},
        compiler_params=pltpu.CompilerParams(has_side_effects=_EFFECT),
    )(*[srcs[n] for n in sn], *[lands[n] for n in ln], sems[0], sems[1], after)
    return dict(zip(ln, outs))


def _block(index, size, base=0):
    return pl.ds(pl.multiple_of(base + index * size, ROW_TILE), size)


def _adam_tile(rows):
    for t in (256, 176, 128):
        if rows % t == 0:
            return t
    return rows


def sum_adamw(recvs, w, m, v, *, name, first_slab=0, into=None):
    _, r, c = w.shape
    b = len(recvs)
    rp = recvs[0].shape[1]
    whole = r % ROW_TILE != 0
    tr = r if whole else _adam_tile(r)
    c1 = 1.0 / (1.0 - ADAM_B1 ** ADAM_STEP)
    c2 = 1.0 / (1.0 - ADAM_B2 ** ADAM_STEP)

    def body(*refs):
        recv_refs = refs[:b]
        w_ref, m_ref, v_ref = refs[b:b + 3]
        g_ref, d_ref, nm_ref, nv_ref = refs[b + 3 + (0 if into is None else 4):][:4]
        for slab, recv_ref in enumerate(recv_refs):
            @pl.when(pl.program_id(0) == slab)
            def _():
                g = recv_ref[0].astype(F32)
                for i in range(1, N_DEV):
                    g = g + recv_ref[i].astype(F32)
                if whole:
                    sum_ref = refs[-1]
                    sum_ref[...] = g
                    g = sum_ref[0:r, :]
                nm = ADAM_B1 * m_ref[0] + (1.0 - ADAM_B1) * g
                nv = ADAM_B2 * v_ref[0] + (1.0 - ADAM_B2) * (g * g)
                g_ref[0] = g
                nm_ref[0] = nm
                nv_ref[0] = nv
                d_ref[0] = -ADAM_LR * ((nm * c1) / (jnp.sqrt(nv * c2) + ADAM_EPS) + ADAM_WD * w_ref[0])

    tile = pl.BlockSpec((1, tr, c), lambda bi, i: (first_slab + bi, i, 0))
    piece = lambda slab: pl.BlockSpec((N_DEV, rp if whole else tr, c), lambda bi, i: (0, jnp.where(bi == slab, i, 0), 0))
    earlier = [] if into is None else list(into)
    return pl.pallas_call(
        body, name=name, grid=(b, r // tr),
        in_specs=[piece(slab) for slab in range(b)] + [tile, tile, tile] + [pl.BlockSpec(memory_space=pl.ANY)] * len(earlier),
        out_specs=(tile,) * 4, out_shape=(jax.ShapeDtypeStruct(w.shape, F32),) * 4,
        input_output_aliases={b + 3 + i: i for i in range(len(earlier))},
        scratch_shapes=[pltpu.VMEM((rp, c), F32)] if whole else [],
    )(*recvs, w, m, v, *earlier)


def _flat_rows(n_elems, row_multiple):
    rows = -(-n_elems // FLAT_COLS)
    return -(-rows // row_multiple) * row_multiple


def _pack(arrays, row_multiple, dtype):
    flat = jnp.concatenate([a.reshape(-1).astype(dtype) for a in arrays])
    rows = _flat_rows(flat.size, row_multiple)
    return jnp.pad(flat, (0, rows * FLAT_COLS - flat.size)).reshape(rows, FLAT_COLS)


def _unpack(flat2d, shapes):
    lead = flat2d.shape[:-2]
    flat = flat2d.reshape(lead + (-1,))
    out, off = [], 0
    for shp in shapes:
        n = int(np.prod(shp))
        out.append(flat[..., off:off + n].reshape(lead + tuple(shp)))
        off += n
    return out


def _join_shards(stacked, axis):
    moved = jnp.moveaxis(stacked, 0, axis)
    shp = list(moved.shape)
    shp[axis:axis + 2] = [shp[axis] * shp[axis + 1]]
    return moved.reshape(shp)


def _split_shards(full, axis):
    shp = list(full.shape)
    shp[axis:axis + 1] = [N_DEV, shp[axis] // N_DEV]
    return jnp.moveaxis(full.reshape(shp), axis, 0)


def kernel(x, meta_tokens, norm_w, ffn_w_gate, ffn_w_up, ffn_w_down, rel_bias_table, even_w_in, even_conv_w, swa_sinks, dn_a_log, dn_dt_bias, dn_norm_w, even_w_out, odd_w_in, gla_w_gate_up, gla_b_gate, gla_norm_w, odd_w_out, loss_target, m_meta_tokens, m_norm_w, m_ffn_w_gate, m_ffn_w_up, m_ffn_w_down, m_rel_bias_table, m_even_w_in, m_even_conv_w, m_swa_sinks, m_dn_a_log, m_dn_dt_bias, m_dn_norm_w, m_even_w_out, m_odd_w_in, m_gla_w_gate_up, m_gla_b_gate, m_gla_norm_w, m_odd_w_out, v_meta_tokens, v_norm_w, v_ffn_w_gate, v_ffn_w_up, v_ffn_w_down, v_rel_bias_table, v_even_w_in, v_even_conv_w, v_swa_sinks, v_dn_a_log, v_dn_dt_bias, v_dn_norm_w, v_even_w_out, v_odd_w_in, v_gla_w_gate_up, v_gla_b_gate, v_gla_norm_w, v_odd_w_out):
    args = locals()
    w = {n: args[n] for n in WEIGHTS}
    m = {n: args["m_" + n] for n in WEIGHTS}
    v = {n: args["v_" + n] for n in WEIGHTS}

    d = D_MODEL
    me = _my_index()
    whole = lambda ref, j: ref
    rows = lambda size, base=0: (lambda ref, i: ref.at[_block(i, size, base), :])
    lead = lambda ref, i: ref.at[i]
    of_group = lambda items, g: [it for it in items if it[4] == g]
    names = lambda items, k: list(dict.fromkeys(it[k] for it in items))

    def placed(shape, dtype, parts):
        land = lax.empty(shape, dtype)
        for part, axis, start in parts:
            land = lax.dynamic_update_slice(land, part, tuple(start if a == axis else 0 for a in range(land.ndim)))
        return land

    as_rows = lambda a: jnp.swapaxes(a, -1, -2)
    pad_rows = lambda a, to: jnp.pad(a, [(0, 0)] * (a.ndim - 2) + [(0, to - a.shape[-2]), (0, 0)])
    gate_s = pad_rows(as_rows(w["ffn_w_gate"].reshape(N_FFN, d, FF_SHARD)), FF_SHARD_PAD).astype(BF16)
    up_s = pad_rows(as_rows(w["ffn_w_up"].reshape(N_FFN, d, FF_SHARD)), FF_SHARD_PAD).astype(BF16)
    down_s = pad_rows(w["ffn_w_down"].reshape(N_FFN, FF_SHARD, d), FF_SHARD_PAD).astype(BF16)
    small_s = _pack([w[n] for n in SMALL], 8, F32)
    srcs_w = {"ein": pad_rows(as_rows(w["even_w_in"][0]), EVEN_IN_SHARD_PAD).astype(BF16),
              "oin": pad_rows(as_rows(w["odd_w_in"][0]), ODD_IN_SHARD_PAD).astype(BF16),
              "eout": w["even_w_out"][0].astype(BF16), "oout": w["odd_w_out"][0].astype(BF16), "small": small_s}
    lands_w = {"ein": placed((N_DEV * EVEN_IN_SHARD_PAD, d), BF16, [(srcs_w["ein"], 0, me * EVEN_IN_SHARD_PAD)]),
               "oin": placed((N_DEV * ODD_IN_SHARD_PAD, d), BF16, [(srcs_w["oin"], 0, me * ODD_IN_SHARD_PAD)]),
               "eout": placed((d, d), BF16, [(srcs_w["eout"], 0, me * OUT_SHARD)]),
               "oout": placed((d, d), BF16, [(srcs_w["oout"], 0, me * OUT_SHARD)]),
               "small": placed((N_DEV,) + small_s.shape, F32, [(small_s[None], 0, me)])}
    items_w = [("small", whole, "small", lead, "first"), ("ein", whole, "ein", rows(EVEN_IN_SHARD_PAD), "even"),
               ("eout", whole, "eout", rows(OUT_SHARD), "even"), ("oin", whole, "oin", rows(ODD_IN_SHARD_PAD), "odd"),
               ("oout", whole, "oout", rows(OUT_SHARD), "odd")]
    for i, group, down_group in ((0, "first", "down0"), (1, "ffn1", "ffn1"), (2, "ffn2", "down2"), (3, "ffn3", "ffn3")):
        srcs_w.update({f"gate{i}": gate_s[i], f"up{i}": up_s[i], f"down{i}": down_s[i]})
        lands_w[f"w_gu{i}"] = placed((2 * FF_PAD, d), BF16, [(srcs_w[f"gate{i}"], 0, me * FF_SHARD_PAD),
                                                             (srcs_w[f"up{i}"], 0, FF_PAD + me * FF_SHARD_PAD)])
        lands_w[f"w_down{i}"] = placed((FF_PAD, d), BF16, [(srcs_w[f"down{i}"], 0, me * FF_SHARD_PAD)])
        items_w += [(f"gate{i}", whole, f"w_gu{i}", rows(FF_SHARD_PAD), group),
                    (f"up{i}", whole, f"w_gu{i}", rows(FF_SHARD_PAD, FF_PAD), group),
                    (f"down{i}", whole, f"w_down{i}", rows(FF_SHARD_PAD), down_group)]
    pending, started = {}, []
    for g in ("first", "down0", "even", "ffn1", "ffn2", "down2", "odd", "ffn3"):
        its = of_group(items_w, g)
        srcs = {n: srcs_w[n] for n in names(its, 0)}
        sems, lands, token = start_copies(srcs, {n: lands_w[n] for n in names(its, 2)}, its, name=f"gather_start_{g}")
        pending[g] = (sems, srcs, lands, its)
        started.append(token)

    unpad = lambda p, shard, shard_pad: p.reshape(N_DEV, shard_pad, d)[:, :shard].reshape(N_DEV * shard, d)

    def get_w(stage, after):
        if stage not in pending:
            return {}
        sems, srcs, lands, its = pending[stage]
        landed = wait_copies(sems, srcs, lands, its, after, name=f"gather_wait_{stage}")
        got = {}
        for n, arr in landed.items():
            if n == "small":
                for sn, stacked in zip(SMALL, _unpack(arr, [w[sn].shape for sn in SMALL])):
                    got[sn] = _join_shards(stacked, SHARD_AXIS[sn])
            elif n == "ein":
                got["even_w_in"] = unpad(arr, EVEN_IN_SHARD, EVEN_IN_SHARD_PAD)
            elif n == "oin":
                got["odd_w_in"] = unpad(arr, ODD_IN_SHARD, ODD_IN_SHARD_PAD)
            elif n in ("eout", "oout"):
                got["even_w_out" if n == "eout" else "odd_w_out"] = arr
            else:
                got[n] = (arr, None)
        return got

    full = {n: w[n] for n in REPL}
    full.update(get_w("first", sum(started)))

    repad = lambda g, shard, shard_pad: pad_rows(g.reshape(N_DEV, shard, d), shard_pad)
    pieces_g = {"r_oin": ("oin", None, "gu2"), "r_oout": ("oout", (OUT_SHARD, 0), "gu2"),
                "r_ein": ("ein", None, "even"), "r_eout": ("eout", (OUT_SHARD, 0), "even"), "r_small": ("small", None, "last")}
    for i in range(N_FFN):
        pieces_g.update({f"r_gate{i}": (f"g_gu{i}", (FF_SHARD_PAD, 0), f"gu{i}"),
                         f"r_up{i}": (f"g_gu{i}", (FF_SHARD_PAD, FF_PAD), f"gu{i}"),
                         f"r_down{i}": (f"g_down{i}", (FF_SHARD_PAD, 0), "down0" if i == 0 else f"gu{i}")})
    items_g = [(src, lead if blk is None else rows(*blk), land, lead, group) for land, (src, blk, group) in pieces_g.items()]
    last_groups = ("down0", "gu0")

    def grad_src(n, grads):
        if n == "oin":
            return repad(grads["odd_w_in"], ODD_IN_SHARD, ODD_IN_SHARD_PAD).astype(BF16)
        if n == "ein":
            return repad(grads["even_w_in"], EVEN_IN_SHARD, EVEN_IN_SHARD_PAD).astype(BF16)
        if n in ("oout", "eout"):
            return grads["odd_w_out" if n == "oout" else "even_w_out"].astype(BF16)
        return grads[n]

    def grad_land(n, srcs):
        src, blk, _ = pieces_g[n]
        if blk is None:
            own = lax.dynamic_index_in_dim(srcs[src], me, 0, keepdims=False)
        else:
            own = lax.dynamic_slice_in_dim(srcs[src], blk[1] + me * blk[0], blk[0], 0)
        return placed((N_DEV,) + own.shape, own.dtype, [(own[None], 0, me)])

    sent = {}

    def put_g(stage, grads):
        its = of_group(items_g, stage)
        if not its:
            return jnp.zeros((1, 1), F32)
        srcs = {n: grad_src(n, grads) for n in names(its, 0)}
        lands = {n: grad_land(n, srcs) for n in names(its, 2)}
        sems, lands, token = start_copies(srcs, lands, its, name=f"grads_start_{stage}")
        sent[stage] = (sems, srcs, lands, its)
        return token

    loss, grad_x, grads = local_step(x[0], loss_target[0], full, get_w, put_g)
    loss = lax.psum(loss, AXES)

    order = SMALL + REPL
    pieces = [_split_shards(grads[n].reshape(full[n].shape), SHARD_AXIS[n]) if n in SHARD_AXIS
              else jnp.broadcast_to(grads[n].reshape(w[n].shape)[None], (N_DEV,) + w[n].shape) for n in order]
    flat = jnp.concatenate([p.reshape(N_DEV, -1) for p in pieces], axis=1)
    srows = _flat_rows(flat.shape[1], 8)
    grads["small"] = jnp.pad(flat, ((0, 0), (0, srows * FLAT_COLS - flat.shape[1]))).reshape(N_DEV, srows, FLAT_COLS)
    recv = {}
    for stage, (sems, srcs, lands, its) in sent.items():
        if stage not in last_groups:
            recv.update(wait_copies(sems, srcs, lands, its, grad_x, name=f"grads_wait_{stage}"))
    result = [{} for _ in range(4)]

    views = {"ffn_w_gate": (lambda a: as_rows(a.reshape(N_FFN, d, FF_SHARD)), lambda o, n: as_rows(o).reshape(w[n].shape)),
             "ffn_w_up": (lambda a: as_rows(a.reshape(N_FFN, d, FF_SHARD)), lambda o, n: as_rows(o).reshape(w[n].shape)),
             "ffn_w_down": (lambda a: a.reshape(N_FFN, FF_SHARD, d), lambda o, n: o.reshape(w[n].shape)),
             "even_w_in": (as_rows, lambda o, n: as_rows(o)), "odd_w_in": (as_rows, lambda o, n: as_rows(o)),
             "even_w_out": (lambda a: a, lambda o, n: o), "odd_w_out": (lambda a: a, lambda o, n: o)}

    def adam(n, recvs, first_slab=0, into=None):
        view = views[n][0]
        return sum_adamw(recvs, view(w[n]), view(m[n]), view(v[n]), name=f"adamw_{n}_{first_slab}",
                         first_slab=first_slab, into=into)

    def finish(n, outs):
        for r, o in zip(result, outs):
            r[n] = views[n][1](o, n)

    ffn_recv = (("ffn_w_gate", "r_gate"), ("ffn_w_up", "r_up"), ("ffn_w_down", "r_down"))
    early = {n: adam(n, [recv[f"{r}{i}"] for i in (1, 2, 3)], first_slab=1) for n, r in ffn_recv}
    for n, r in (("even_w_in", "r_ein"), ("odd_w_in", "r_oin"), ("even_w_out", "r_eout"), ("odd_w_out", "r_oout")):
        finish(n, adam(n, [recv[r]]))
    srcs = {"small": grads["small"]}
    recv.update(exchange(srcs, {"r_small": grad_land("r_small", srcs)}, of_group(items_g, "last"),
                         early["ffn_w_down"][0], name="exchange_small"))
    for stage in last_groups:
        sems, srcs, lands, its = sent[stage]
        recv.update(wait_copies(sems, srcs, lands, its, recv["r_small"], name=f"grads_wait_{stage}"))
    for n, r in ffn_recv:
        finish(n, adam(n, [recv[f"{r}0"]], into=early[n]))
    pack_local = lambda t: _pack([t[n] for n in order], 8, F32)[None]
    small_outs = sum_adamw([recv["r_small"]], pack_local(w), pack_local(m), pack_local(v), name="adamw_small")
    for r, o in zip(result, small_outs):
        r.update(zip(order, _unpack(o[0], [w[n].shape for n in order])))
    return (loss, grad_x[None], *[r[n] for r in result for n in WEIGHTS])
```

```python
import functools
import math

import numpy as np
import jax
import jax.numpy as jnp
from jax import lax
from jax.experimental import pallas as pl
from jax.experimental.pallas import tpu as pltpu

F32 = jnp.float32
BF16 = jnp.bfloat16
MESH = pl.DeviceIdType.MESH
AXES = ("x", "y", "c")
N_DEV = 8

D_MODEL = 1024
N_META = 16
D_FF = 2816
NORM_EPS = 1e-6
NEG_INF = -1e30
SWA_Q_HEADS = 8
SWA_HEAD_DIM = 64
SWA_WINDOW = 128
SWA_BLOCK = 128
REL_BUCKETS = 32
REL_MAX_DIST = 128
DN_HEADS = 4
DN_HEAD_DIM = 128
DN_CONV = 4
GLA_HEADS = 4
GLA_DK = 128
GLA_DV = 256
GLA_GATE_RANK = 16
GLA_GATE_NORM = 16.0
CHUNK = 64
CHUNKS_PER_STEP = 6
PAD = SWA_BLOCK - N_META
LANE = 128
PROJ_DIM = 3200

ADAM_LR = 0.001
ADAM_B1 = 0.9
ADAM_B2 = 0.999
ADAM_EPS = 1e-08
ADAM_WD = 0.01
ADAM_STEP = 10

ROW_TILE = 16
FF_SHARD = D_FF // N_DEV
FF_SHARD_PAD = -(-FF_SHARD // ROW_TILE) * ROW_TILE
FF_PAD = N_DEV * FF_SHARD_PAD
N_FFN = 4
EVEN_IN_SHARD, EVEN_IN_SHARD_PAD = 353, 368
ODD_IN_SHARD, ODD_IN_SHARD_PAD = 386, 400
OUT_SHARD = D_MODEL // N_DEV

FLAT_COLS = 128
BIG = ("ffn_w_gate", "ffn_w_up", "ffn_w_down", "even_w_in", "even_w_out", "odd_w_in", "odd_w_out")
SMALL = ("meta_tokens", "norm_w", "even_conv_w", "gla_w_gate_up", "gla_b_gate", "gla_norm_w")
REPL = ("rel_bias_table", "swa_sinks", "dn_a_log", "dn_dt_bias", "dn_norm_w")
WEIGHTS = ("meta_tokens", "norm_w", "ffn_w_gate", "ffn_w_up", "ffn_w_down", "rel_bias_table", "even_w_in",
           "even_conv_w", "swa_sinks", "dn_a_log", "dn_dt_bias", "dn_norm_w", "even_w_out", "odd_w_in",
           "gla_w_gate_up", "gla_b_gate", "gla_norm_w", "odd_w_out")
SHARD_AXIS = {"ffn_w_gate": 3, "ffn_w_up": 3, "ffn_w_down": 2, "even_w_in": 2, "even_w_out": 1, "odd_w_in": 2,
              "odd_w_out": 1, "meta_tokens": 1, "norm_w": 2, "even_conv_w": 2, "gla_w_gate_up": 2,
              "gla_b_gate": 1, "gla_norm_w": 1}


def _rms(x, w):
    r = lax.rsqrt(jnp.mean(x * x, axis=-1, keepdims=True) + NORM_EPS)
    return x * r * w


def _sigmoid(x):
    return 0.5 * (jnp.tanh(0.5 * x) + 1.0)


def _silu(x):
    return x * _sigmoid(x)


def _softplus(x):
    pos = x > 0
    return jnp.where(pos, x, 0.0) + jnp.log(1.0 + jnp.exp(jnp.where(pos, -x, x)))


def _l2n(x):
    return x * lax.rsqrt(jnp.sum(x * x, axis=-1, keepdims=True) + 1e-6)


def _split_bf16(x):
    hi = x.astype(BF16)
    return hi, (x - hi.astype(F32)).astype(BF16)


def _make_mm(terms, batched):
    off = 1 if batched else 0
    bdims = ((0,), (0,)) if batched else ((), ())

    def dg(a, b, ca, cb):
        dot = lambda p, q: lax.dot_general(p, q, (((ca + off,), (cb + off,)), bdims), preferred_element_type=F32)
        a_hi, a_lo = _split_bf16(a)
        b_hi, b_lo = _split_bf16(b)
        if terms == 1:
            return dot(a_hi, b_hi)
        return dot(a_hi, b_hi) + (dot(a_hi, b_lo) + dot(a_lo, b_hi))

    @jax.custom_vjp
    def nn(a, b):
        return dg(a, b, 1, 0)

    @jax.custom_vjp
    def nt(a, b):
        return dg(a, b, 1, 1)

    @jax.custom_vjp
    def tn(a, b):
        return dg(a, b, 0, 0)

    nn.defvjp(lambda a, b: (nn(a, b), (a, b)), lambda r, g: (nt(g, r[1]), tn(r[0], g)))
    nt.defvjp(lambda a, b: (nt(a, b), (a, b)), lambda r, g: (nn(g, r[1]), tn(g, r[0])))
    tn.defvjp(lambda a, b: (tn(a, b), (a, b)), lambda r, g: (nt(r[1], g), nn(r[0], g)))
    return nn, nt, tn


_mm, _mm_nt, _mm_tn = _make_mm(1, False)
_mm3, _, _ = _make_mm(3, False)
_bmm, _bmm_nt, _bmm_tn = _make_mm(1, True)
_bmm3, _bmm3_nt, _bmm3_tn = _make_mm(3, True)


@jax.custom_vjp
def _known_inverse(a, inv):
    return inv


_known_inverse.defvjp(lambda a, inv: (inv, inv),
                      lambda inv, g: (-_bmm3_tn(inv, _bmm3_nt(g, inv)), jnp.zeros_like(inv)))


def _tri_ones_dot(x, lower):
    n = x.shape[0]
    r = lax.broadcasted_iota(jnp.int32, (n, n), 0)
    c = lax.broadcasted_iota(jnp.int32, (n, n), 1)
    t = ((r >= c) if lower else (r <= c)).astype(BF16)
    hi, lo = _split_bf16(x)
    return jnp.dot(t, hi, preferred_element_type=F32) + jnp.dot(t, lo, preferred_element_type=F32)


@jax.custom_vjp
def _cumsum_rows(x):
    return _tri_ones_dot(x, True)


_cumsum_rows.defvjp(lambda x: (_tri_ones_dot(x, True), None), lambda _, g: (_tri_ones_dot(g, False),))


def _row_tile(n_rows, cap):
    best = LANE
    for t in range(LANE, cap + 1, LANE):
        if n_rows % t == 0:
            best = t
    return best


def _real_rows(tile_index, tm):
    row = tile_index * tm + lax.broadcasted_iota(jnp.int32, (tm, 1), 0)
    return (row >= PAD).astype(F32)


def _full(shape):
    return pl.BlockSpec(shape, lambda *_: (0,) * len(shape))


def _resident(shape):
    return pl.BlockSpec(shape, lambda *_: (0,) * len(shape), pipeline_mode=pl.Buffered(1))


def _resident_w(wmat, widx):
    if wmat.ndim == 2:
        return _resident(wmat.shape)
    return pl.BlockSpec((None,) + wmat.shape[1:], lambda *_: (widx, 0, 0), pipeline_mode=pl.Buffered(1))


def rms_mm(h, w, wmat_t, *, swiglu, name, widx=None):
    tp, d = h.shape
    n = wmat_t.shape[-2]
    tm = _row_tile(tp, 384)
    half = n // 2
    wmat = wmat_t

    def body(h_ref, w_ref, wm_ref, hn_ref, *outs):
        hn = _rms(h_ref[...], w_ref[...]).astype(BF16)
        hn_ref[...] = hn
        p = lax.dot_general(hn, wm_ref[...], (((1,), (1,)), ((), ())), preferred_element_type=F32)
        if swiglu:
            g, u = p[:, :half], p[:, half:]
            outs[0][...] = g.astype(BF16)
            outs[1][...] = u.astype(BF16)
            outs[2][...] = (_silu(g) * u).astype(BF16)
        else:
            outs[0][...] = p

    row = lambda width: pl.BlockSpec((tm, width), lambda i: (i, 0))
    if swiglu:
        out_shape = (jax.ShapeDtypeStruct((tp, d), BF16),) + (jax.ShapeDtypeStruct((tp, half), BF16),) * 3
        out_specs = (row(d), row(half), row(half), row(half))
    else:
        out_shape = (jax.ShapeDtypeStruct((tp, d), BF16), jax.ShapeDtypeStruct((tp, n), F32))
        out_specs = (row(d), row(n))
    return pl.pallas_call(
        body, name=name, grid=(tp // tm,),
        in_specs=[row(d), _full((1, d)), _resident_w(wmat, widx)],
        out_specs=out_specs, out_shape=out_shape,
    )(h, w, wmat)


def mm_rms_res(acts, wmat, h, w, *, scale, name, widx=None):
    tp, d = h.shape
    tm = _row_tile(tp, 384)
    widths = [a.shape[1] for a in acts]
    offs = [sum(widths[:i]) for i in range(len(acts))]
    na = len(acts)

    def body(*refs):
        a_refs = refs[:na]
        wm_ref, h_ref, w_ref, f_ref, ho_ref = refs[na:]
        f = None
        for a_ref, off, width in zip(a_refs, offs, widths):
            part = jnp.dot(a_ref[...].astype(BF16), wm_ref[off:off + width, :], preferred_element_type=F32)
            f = part if f is None else f + part
        f_ref[...] = f
        ho_ref[...] = h_ref[...] + scale * _rms(f, w_ref[...])

    row = lambda width: pl.BlockSpec((tm, width), lambda i: (i, 0))
    return pl.pallas_call(
        body, name=name, grid=(tp // tm,),
        in_specs=[row(wd) for wd in widths] + [_resident_w(wmat, widx), row(d), _full((1, d))],
        out_specs=(row(d), row(d)),
        out_shape=(jax.ShapeDtypeStruct((tp, d), F32), jax.ShapeDtypeStruct((tp, d), F32)),
    )(*acts, wmat, h, w)


def mm_rms_res_bwd(dho, f, w, wmat, gu, *, scale, name, widx=None):
    tp, d = f.shape
    k = wmat.shape[-2]
    tm = _row_tile(tp, 384)
    swiglu = gu is not None

    def body(*refs):
        if swiglu:
            dho_ref, f_ref, w_ref, wm_ref, g_ref, u_ref, df_ref, dw_ref, dgu_ref = refs
        else:
            dho_ref, f_ref, w_ref, wm_ref, df_ref, dw_ref, da_ref = refs
        i = pl.program_id(0)
        _, vjp = jax.vjp(lambda ff, ww: scale * _rms(ff, ww), f_ref[...], w_ref[...])
        df, dw = vjp(dho_ref[...])
        dfb = (df * _real_rows(i, tm)).astype(BF16)
        df_ref[...] = dfb

        @pl.when(i == 0)
        def _():
            dw_ref[...] = jnp.zeros_like(dw_ref)

        dw_ref[...] += dw
        da = lax.dot_general(dfb, wm_ref[...], (((1,), (1,)), ((), ())), preferred_element_type=F32)
        if swiglu:
            g, u, dab = g_ref[...], u_ref[...], da.astype(BF16)
            s = _sigmoid(g)
            dgu_ref[:, :k] = dab * u * s * (1.0 + g * (1.0 - s))
            dgu_ref[:, k:] = dab * g * s
        else:
            da_ref[...] = da

    row = lambda width: pl.BlockSpec((tm, width), lambda i: (i, 0))
    in_specs = [row(d), row(d), _full((1, d)), _resident_w(wmat, widx)]
    args = [dho, f, w, wmat]
    out_shape = [jax.ShapeDtypeStruct((tp, d), BF16), jax.ShapeDtypeStruct((1, d), F32)]
    out_specs = [row(d), _full((1, d))]
    if swiglu:
        in_specs += [row(k), row(k)]
        args += list(gu)
        out_shape += [jax.ShapeDtypeStruct((tp, 2 * k), BF16)]
        out_specs += [row(2 * k)]
    else:
        out_shape += [jax.ShapeDtypeStruct((tp, k), F32)]
        out_specs += [row(k)]
    return pl.pallas_call(body, name=name, grid=(tp // tm,), in_specs=in_specs, out_specs=tuple(out_specs),
                          out_shape=tuple(out_shape))(*args)


def rms_mm_bwd(dps, wmat, h, w, dho, *, name, widx=None):
    tp, d = h.shape
    tm = _row_tile(tp, 384)
    widths = [p.shape[1] for p in dps]
    offs = [sum(widths[:i]) for i in range(len(dps))]
    ndp = len(dps)

    def body(*refs):
        dp_refs = refs[:ndp]
        wm_ref, h_ref, w_ref, dho_ref, dh_ref, dw_ref = refs[ndp:]
        i = pl.program_id(0)
        dhn = None
        for dp_ref, off, width in zip(dp_refs, offs, widths):
            part = jnp.dot(dp_ref[...].astype(BF16), wm_ref[off:off + width, :], preferred_element_type=F32)
            dhn = part if dhn is None else dhn + part
        _, vjp = jax.vjp(_rms, h_ref[...], w_ref[...])
        dx, dw = vjp(dhn)
        dh_ref[...] = (dho_ref[...] + dx) * _real_rows(i, tm)

        @pl.when(i == 0)
        def _():
            dw_ref[...] = jnp.zeros_like(dw_ref)

        dw_ref[...] += dw

    row = lambda width: pl.BlockSpec((tm, width), lambda i: (i, 0))
    return pl.pallas_call(
        body, name=name, grid=(tp // tm,),
        in_specs=[row(wd) for wd in widths] + [_resident_w(wmat, widx), row(d), _full((1, d)), row(d)],
        out_specs=(row(d), _full((1, d))),
        out_shape=(jax.ShapeDtypeStruct((tp, d), F32), jax.ShapeDtypeStruct((1, d), F32)),
    )(*dps, wmat, h, w, dho)


def mm_tn(a, b, *, name, out_dtype=F32, after=None):
    t, m = a.shape
    n = b.shape[1]
    bm = _row_tile(m, 1408 if n <= 1024 else 512)
    bn = _row_tile(n, 1536)
    bk = _row_tile(t, 1408)
    nk = t // bk
    ties = [] if after is None else [after]

    def body(a_ref, b_ref, *rest):
        o_ref, acc = rest[-2:]

        @pl.when(pl.program_id(2) == 0)
        def _():
            acc[...] = jnp.zeros_like(acc)

        acc[...] += lax.dot_general(a_ref[...].astype(BF16), b_ref[...].astype(BF16), (((0,), (0,)), ((), ())),
                                    preferred_element_type=F32)

        @pl.when(pl.program_id(2) == nk - 1)
        def _():
            o_ref[...] = acc[...].astype(o_ref.dtype)

    return pl.pallas_call(
        body, name=name, grid=(m // bm, n // bn, nk),
        in_specs=[pl.BlockSpec((bk, bm), lambda i, j, kk: (kk, i)), pl.BlockSpec((bk, bn), lambda i, j, kk: (kk, j))]
        + [pl.BlockSpec(memory_space=pl.ANY)] * len(ties),
        out_specs=pl.BlockSpec((bm, bn), lambda i, j, kk: (i, j)),
        out_shape=jax.ShapeDtypeStruct((m, n), out_dtype), scratch_shapes=[pltpu.VMEM((bm, bn), F32)],
    )(a, b, *ties)


def loss_and_grad(h, target, *, name):
    tp, d = h.shape
    tm = SWA_BLOCK

    def body(h_ref, t_ref, dh_ref, loss_ref):
        i = pl.program_id(0)

        @pl.when(i == 0)
        def _():
            loss_ref[...] = jnp.zeros_like(loss_ref)
            dh_ref[...] = jnp.zeros_like(dh_ref)

        @pl.when(i > 0)
        def _():
            err = h_ref[...] - t_ref[...]
            dh_ref[...] = err * (1.0 / d)
            loss_ref[...] += 0.5 * jnp.sum(jnp.sum(err * err, axis=1, keepdims=True), axis=0, keepdims=True) * (1.0 / d)

    return pl.pallas_call(
        body, name=name, grid=(tp // tm,),
        in_specs=[pl.BlockSpec((tm, d), lambda i: (i, 0)), pl.BlockSpec((tm, d), lambda i: (jnp.maximum(i - 1, 0), 0))],
        out_specs=(pl.BlockSpec((tm, d), lambda i: (i, 0)), _full((1, 1))),
        out_shape=(jax.ShapeDtypeStruct((tp, d), F32), jax.ShapeDtypeStruct((1, 1), F32)),
    )(h, target)


def _t5_bucket_np(rel):
    n = np.maximum(rel, 0)
    max_exact = REL_BUCKETS // 2
    n_f = np.maximum(n, 1).astype(np.float32)
    large = max_exact + (np.log(n_f / np.float32(max_exact)) / np.float32(math.log(REL_MAX_DIST / max_exact))
                         * np.float32(REL_BUCKETS - max_exact)).astype(np.int32)
    large = np.minimum(large, REL_BUCKETS - 1)
    return np.where(n < max_exact, n, large).astype(np.int32)


def _swa_positions_np(n):
    i = np.arange(SWA_BLOCK)[:, None]
    j = np.arange(3 * SWA_BLOCK)[None, :]
    pos_q = n * SWA_BLOCK + i - PAD
    pos_k = np.where(j < SWA_BLOCK, j - PAD, (n - 1) * SWA_BLOCK + (j - SWA_BLOCK) - PAD)
    return pos_q, pos_k


def _swa_buckets():
    out = []
    for n in range(3):
        pos_q, pos_k = _swa_positions_np(n)
        out.append(_t5_bucket_np(pos_q - pos_k))
    return jnp.asarray(np.stack(out))


def swa_bias(table, buckets, *, name):
    nc, nq, nk = buckets.shape

    def body(tab_ref, bkt_ref, out_ref):
        for c in range(nc):
            bkt = bkt_ref[c]
            for h in range(SWA_Q_HEADS):
                acc = jnp.zeros((nq, nk), F32)
                for b in range(REL_BUCKETS):
                    acc = jnp.where(bkt == b, tab_ref[b, h], acc)
                out_ref[c, h] = acc

    return pl.pallas_call(
        body, name=name,
        in_specs=[pl.BlockSpec(memory_space=pltpu.SMEM), pl.BlockSpec(memory_space=pltpu.VMEM)],
        out_specs=pl.BlockSpec(memory_space=pltpu.VMEM),
        out_shape=jax.ShapeDtypeStruct((nc, SWA_Q_HEADS, nq, nk), F32),
    )(table, buckets)


def swa_bias_bwd(dbias, buckets, *, name):
    nc = buckets.shape[0]

    def body(db_ref, bkt_ref, out_ref):
        lane = lax.broadcasted_iota(jnp.int32, (1, LANE), 1)
        for b in range(REL_BUCKETS):
            row = jnp.zeros((1, LANE), F32)
            for c in range(nc):
                hit = bkt_ref[c] == b
                for h in range(SWA_Q_HEADS):
                    part = jnp.where(hit, db_ref[c, h], 0.0)
                    tot = jnp.sum(jnp.sum(part, axis=1, keepdims=True), axis=0, keepdims=True)
                    row = row + jnp.where(lane == h, tot, 0.0)
            out_ref[b:b + 1, :] = row

    return pl.pallas_call(
        body, name=name,
        in_specs=[pl.BlockSpec(memory_space=pltpu.VMEM), pl.BlockSpec(memory_space=pltpu.VMEM)],
        out_specs=pl.BlockSpec(memory_space=pltpu.VMEM),
        out_shape=jax.ShapeDtypeStruct((REL_BUCKETS, LANE), F32),
    )(dbias, buckets)


def _swa_block(q, kvm, kvp, kvc, bias, sinks, n, batched):
    blk = SWA_BLOCK
    i = lax.broadcasted_iota(jnp.int32, (blk, 3 * blk), 0)
    j = lax.broadcasted_iota(jnp.int32, (blk, 3 * blk), 1)
    pos_q = n * blk + i - PAD
    is_meta = j < blk
    pos_k = jnp.where(is_meta, j - PAD, (n - 1) * blk + (j - blk) - PAD)
    rel = pos_q - pos_k
    valid = ((is_meta & (pos_k >= 0) & (pos_k < N_META) & (rel >= 0))
             | (jnp.logical_not(is_meta) & (pos_k >= N_META) & (rel >= 0) & (rel < SWA_WINDOW)))
    valid_f = valid.astype(F32)
    kv =jnp.concatenate([kvm, kvp, kvc], axis=0)
    lane = lax.broadcasted_iota(jnp.int32, (1, LANE), 1)
    halves = ((lane < SWA_HEAD_DIM).astype(F32), (lane >= SWA_HEAD_DIM).astype(F32))
    nh, group = SWA_Q_HEADS, SWA_Q_HEADS // 2
    q_of = lambda h: q[:, (h // 2) * LANE:(h // 2 + 1) * LANE] * halves[h % 2]
    k_of = lambda h: kv[:, (h // group) * LANE:(h // group + 1) * LANE]
    v_of = lambda h: kv[:, (2 + h // group) * LANE:(3 + h // group) * LANE]
    sink_of = lambda h: jnp.sum(jnp.where(lane == h, sinks, 0.0), axis=1, keepdims=True)

    scale = SWA_HEAD_DIM ** -0.5

    def attend(logits, sink, pv):
        if batched:
            s = logits * valid_f + (valid_f - 1.0) * (-NEG_INF)
        else:
            s = jnp.where(valid, logits, NEG_INF)
        m =lax.stop_gradient(jnp.maximum(jnp.max(s, axis=-1, keepdims=True), sink))
        e = jnp.exp(s - m)
        return pv(e / (jnp.sum(e, axis=-1, keepdims=True) + jnp.exp(sink - m)))

    if batched:
        heads = range(nh)
        vh = _stack([v_of(h) for h in heads])
        qk = _bmm_nt(_stack([q_of(h) for h in heads]), _stack([k_of(h) for h in heads]))
        o = attend(qk * scale + bias, _stack([sink_of(h) for h in heads]), lambda p: _bmm(p, vh))
        head = lambda h: o[h]
    else:
        head = lambda h: attend(_mm_nt(q_of(h), k_of(h)) * scale + bias[h], sink_of(h), lambda p: _mm(p, v_of(h)))
    return jnp.concatenate([head(2 * p) * halves[0] + head(2 * p + 1) * halves[1] for p in range(nh // 2)], axis=1)


def _swa_in_specs(nb, rev):
    blk = SWA_BLOCK
    step = (lambda i: nb - 1 - i) if rev else (lambda i: i)
    return [
        pl.BlockSpec((blk, 4 * LANE), lambda i: (step(i), 0)),
        pl.BlockSpec((blk, 4 * LANE), lambda i: (0, 1)),
        pl.BlockSpec((blk, 4 * LANE), lambda i: (jnp.maximum(step(i) - 1, 0), 1)),
        pl.BlockSpec((blk, 4 * LANE), lambda i: (step(i), 1)),
        pl.BlockSpec((1, SWA_Q_HEADS, blk, 3 * blk), lambda i: (jnp.minimum(step(i), 2), 0, 0, 0)),
        _full((1, LANE)),
    ]


def swa_fwd(proj, bias, sinks, *, name):
    tp = proj.shape[0]
    nb = tp // SWA_BLOCK

    def body(q_ref, kvm_ref, kvp_ref, kvc_ref, bias_ref, sinks_ref, o_ref):
        n = pl.program_id(0)
        o_ref[...] = _swa_block(q_ref[...], kvm_ref[...], kvp_ref[...], kvc_ref[...], bias_ref[0], sinks_ref[...], n, True)

    return pl.pallas_call(
        body, name=name, grid=(nb,),
        in_specs=_swa_in_specs(nb, False),
        out_specs=pl.BlockSpec((SWA_BLOCK, 4 * LANE), lambda i: (i, 0)),
        out_shape=jax.ShapeDtypeStruct((tp, 4 * LANE), F32),
    )(proj, proj, proj, proj, bias, sinks)


def swa_bwd(proj, bias, sinks, do, *, name):
    tp = proj.shape[0]
    nb = tp // SWA_BLOCK
    blk = SWA_BLOCK

    def body(q_ref, kvm_ref, kvp_ref, kvc_ref, bias_ref, sinks_ref, do_ref, dqkv_ref, dbias_ref, dsinks_ref,
             carry, meta_acc):
        i = pl.program_id(0)
        n = nb - 1 - i

        @pl.when(i == 0)
        def _():
            carry[...] = jnp.zeros_like(carry)
            meta_acc[...] = jnp.zeros_like(meta_acc)
            dsinks_ref[...] = jnp.zeros_like(dsinks_ref)

        fn = lambda q, kvm, kvp, kvc, b, s: _swa_block(q, kvm, kvp, kvc, b, s, n, False)
        _, vjp = jax.vjp(fn, q_ref[...], kvm_ref[...], kvp_ref[...], kvc_ref[...], bias_ref[0], sinks_ref[...])
        dq, dkvm, dkvp, dkvc, dbias, dsinks = vjp(do_ref[...])
        dqkv_ref[:, :4 * LANE] = dq.astype(BF16)
        meta_acc[...] += dkvm
        dqkv_ref[:, 4 * LANE:] = (dkvc + carry[...] + jnp.where(n == 0, meta_acc[...], 0.0)).astype(BF16)
        carry[...] = dkvp
        first_visit = (n == nb - 1) | (n < 2)

        @pl.when(first_visit)
        def _():
            dbias_ref[0] = dbias

        @pl.when(jnp.logical_not(first_visit))
        def _():
            dbias_ref[0] += dbias

        dsinks_ref[...] += dsinks

    rev = lambda i: nb - 1 - i
    return pl.pallas_call(
        body, name=name, grid=(nb,),
        in_specs=_swa_in_specs(nb, True) + [pl.BlockSpec((blk, 4 * LANE), lambda i: (rev(i), 0))],
        out_specs=(pl.BlockSpec((blk, 8 * LANE), lambda i: (rev(i), 0)),
                   pl.BlockSpec((1, SWA_Q_HEADS, blk, 3 * blk), lambda i: (jnp.minimum(rev(i), 2), 0, 0, 0)),
                   _full((1, LANE))),
        out_shape=(jax.ShapeDtypeStruct((tp, PROJ_DIM), BF16),
                   jax.ShapeDtypeStruct((3, SWA_Q_HEADS, blk, 3 * blk), F32), jax.ShapeDtypeStruct((1, LANE), F32)),
        scratch_shapes=[pltpu.VMEM((blk, 4 * LANE), F32), pltpu.VMEM((blk, 4 * LANE), F32)],
    )(proj, proj, proj, proj, bias, sinks, do)


CONV_COL0 = 2
HALO = 8


def conv_fwd(proj, conv_w, *, name):
    tp = proj.shape[0]
    tm = _row_tile(tp, 384)
    cw = 4 * LANE
    ncol = conv_w.shape[1] // cw

    def body(x_ref, halo_ref, w_ref, y_ref, buf):
        i = pl.program_id(1)
        buf[0:HALO, :] = jnp.where(i > 0, halo_ref[...], 0.0)
        buf[HALO:, :] = x_ref[...]
        acc = None
        for j in range(DN_CONV):
            term = w_ref[j:j + 1, :] * buf[pl.ds(HALO - (DN_CONV - 1) + j, tm), :]
            acc = term if acc is None else acc + term
        y_ref[...] = acc

    return pl.pallas_call(
        body, name=name, grid=(ncol, tp // tm),
        in_specs=[pl.BlockSpec((tm, cw), lambda c, i: (i, CONV_COL0 + c)),
                  pl.BlockSpec((HALO, cw), lambda c, i: (jnp.maximum(i * (tm // HALO) - 1, 0), CONV_COL0 + c)),
                  pl.BlockSpec((DN_CONV, cw), lambda c, i: (0, c))],
        out_specs=pl.BlockSpec((tm, cw), lambda c, i: (i, c)),
        out_shape=jax.ShapeDtypeStruct((tp, ncol * cw), F32),
        scratch_shapes=[pltpu.VMEM((tm + HALO, cw), F32)],
    )(proj, proj, conv_w)


def conv_bwd(proj, conv_w, dy, dproj, *, name):
    tp = proj.shape[0]
    tm = _row_tile(tp, 384)
    cw = 4 * LANE
    ncol = conv_w.shape[1] // cw
    nt = tp // tm

    def body(x_ref, xhalo_ref, w_ref, dy_ref, dyhalo_ref, _, dx_ref, dw_ref, xbuf, dbuf):
        i = pl.program_id(1)
        xbuf[0:HALO, :] = jnp.where(i > 0, xhalo_ref[...], 0.0)
        xbuf[HALO:, :] = x_ref[...]
        dbuf[0:tm, :] = dy_ref[...]
        dbuf[tm:, :] = jnp.where(i < nt - 1, dyhalo_ref[...], 0.0)
        dy_t = dy_ref[...]
        acc = None
        rows = []
        for j in range(DN_CONV):
            term = w_ref[j:j + 1, :] * dbuf[pl.ds(DN_CONV - 1 - j, tm), :]
            acc = term if acc is None else acc + term
            rows.append(jnp.sum(dy_t * xbuf[pl.ds(HALO - (DN_CONV - 1) + j, tm), :], axis=0, keepdims=True))
        dx_ref[...] = acc.astype(BF16)

        @pl.when(i == 0)
        def _():
            dw_ref[...] = jnp.zeros_like(dw_ref)

        for j in range(DN_CONV):
            dw_ref[j:j + 1, :] += rows[j]

    return pl.pallas_call(
        body, name=name, grid=(ncol, nt),
        in_specs=[pl.BlockSpec((tm, cw), lambda c, i: (i, CONV_COL0 + c)),
                  pl.BlockSpec((HALO, cw), lambda c, i: (jnp.maximum(i * (tm // HALO) - 1, 0), CONV_COL0 + c)),
                  pl.BlockSpec((DN_CONV, cw), lambda c, i: (0, c)),
                  pl.BlockSpec((tm, cw), lambda c, i: (i, c)),
                  pl.BlockSpec((HALO, cw), lambda c, i: (jnp.minimum((i + 1) * (tm // HALO), tp // HALO - 1), c)),
                  pl.BlockSpec(memory_space=pl.ANY)],
        out_specs=(pl.BlockSpec((tm, cw), lambda c, i: (i, CONV_COL0 + c)), pl.BlockSpec((DN_CONV, cw), lambda c, i: (0, c))),
        out_shape=(jax.ShapeDtypeStruct(dproj.shape, dproj.dtype), jax.ShapeDtypeStruct((DN_CONV, ncol * cw), F32)),
        scratch_shapes=[pltpu.VMEM((tm + HALO, cw), F32), pltpu.VMEM((tm + HALO, cw), F32)],
        input_output_aliases={5: 0},
    )(proj, proj, conv_w, dy, dy, dproj)


def _stack(parts):
    return jnp.concatenate([p[None] for p in parts], axis=0)


def _chunk_masks():
    r = lax.broadcasted_iota(jnp.int32, (CHUNK, CHUNK), 0)
    c = lax.broadcasted_iota(jnp.int32, (CHUNK, CHUNK), 1)
    return (r >= c).astype(F32), (r > c).astype(F32), (r == c).astype(F32)


def _dn_chunk(y, z, small, s, a_log, dt_bias, norm_w, rows, known_inv=None):
    tri_incl, tri_strict, eye = _chunk_masks()
    lane = lax.broadcasted_iota(jnp.int32, (1, LANE), 1)
    dk = DN_HEAD_DIM
    nh = DN_HEADS
    heads = lambda t, first: _stack([t[:, (first + h) * dk:(first + h + 1) * dk] for h in range(nh)])
    pick = lambda t, l: jnp.sum(jnp.where(lane == l, t, 0.0), axis=1, keepdims=True)
    q = _l2n(_silu(heads(y, 0))) * dk ** -0.5
    k = _l2n(_silu(heads(y, nh)))
    v = _silu(heads(y, 2 * nh))
    g_all = jnp.where(lane < nh, -jnp.exp(a_log) * _softplus(small + dt_bias), 0.0) * rows
    beta_all = _sigmoid(small)
    gc_all = _cumsum_rows(g_all)
    g_sum = jnp.sum(g_all, axis=0, keepdims=True)
    gc = _stack([pick(gc_all, h) for h in range(nh)])
    beta = _stack([pick(beta_all, nh + h) for h in range(nh)])
    g_last = _stack([pick(g_sum, h) for h in range(nh)])
    gc_row = jnp.sum(eye * gc, axis=1, keepdims=True)
    gamma = jnp.exp((gc - gc_row) * tri_incl) * tri_incl
    k_beta = k * beta
    v_beta = v * beta
    a = _bmm_nt(k_beta, k) * gamma * tri_strict
    if known_inv is None:
        inv = eye - a
        power = a
        for _ in range(5):
            power = _bmm3(power, power)
            inv = inv + _bmm3(inv, power)
    else:
        inv = _known_inverse(a, known_inv)
    e_gc = jnp.exp(gc)
    uw = _bmm3(inv, jnp.concatenate([v_beta, k_beta * e_gc], axis=2))
    u, w = uw[:, :, :dk], uw[:, :, dk:]
    attn = _bmm_nt(q, k) * gamma
    q_dec = q * e_gc
    k_dec = k * jnp.exp(g_last - gc)
    v_new = u - _bmm(w, s)
    o = _bmm(q_dec, s) + _bmm(attn, v_new)
    s_new = s * jnp.exp(g_last) + _bmm_tn(k_dec, v_new)
    out = _rms(o, norm_w) * _silu(heads(z, 0))
    return jnp.concatenate([out[h] for h in range(nh)], axis=1), s_new, inv


Z_COL = 5
SMALL_COL = 24


def _chunk_rows(n):
    row = n * CHUNK + lax.broadcasted_iota(jnp.int32, (CHUNK, 1), 0)
    return (row >= PAD).astype(F32)


def dn_fwd(y, proj, a_log, dt_bias, norm_w, *, name):
    tp = y.shape[0]
    nc = tp // CHUNK
    dk = DN_HEAD_DIM
    per = CHUNKS_PER_STEP
    rows = per * CHUNK

    def body(y_ref, z_ref, small_ref, al_ref, dt_ref, nw_ref, o_ref, ssave_ref, isave_ref, state):
        n = pl.program_id(0)

        @pl.when(n == 0)
        def _():
            state[...] = jnp.zeros_like(state)

        s = state[...]
        for c in range(per):
            at = pl.ds(c * CHUNK, CHUNK)
            ssave_ref[c] = s
            out, s, inv = _dn_chunk(y_ref[at, :], z_ref[at, :], small_ref[at, :], s, al_ref[...], dt_ref[...],
                                    nw_ref[...], _chunk_rows(per * n + c))
            o_ref[at, :] = out
            isave_ref[c] = inv
        state[...] = s

    return pl.pallas_call(
        body, name=name, grid=(nc // per,),
        in_specs=[pl.BlockSpec((rows, y.shape[1]), lambda n: (n, 0)),
                  pl.BlockSpec((rows, 4 * LANE), lambda n: (n, Z_COL)),
                  pl.BlockSpec((rows, LANE), lambda n: (n, SMALL_COL)),
                  _full((1, LANE)), _full((1, LANE)), _full((1, LANE))],
        out_specs=(pl.BlockSpec((rows, 4 * LANE), lambda n: (n, 0)),
                   pl.BlockSpec((per, DN_HEADS, dk, dk), lambda n: (n, 0, 0, 0)),
                   pl.BlockSpec((per, DN_HEADS, CHUNK, CHUNK), lambda n: (n, 0, 0, 0))),
        out_shape=(jax.ShapeDtypeStruct((tp, 4 * LANE), F32), jax.ShapeDtypeStruct((nc, DN_HEADS, dk, dk), F32),
                   jax.ShapeDtypeStruct((nc, DN_HEADS, CHUNK, CHUNK), F32)),
        scratch_shapes=[pltpu.VMEM((DN_HEADS, dk, dk), F32)],
    )(y, proj, proj, a_log, dt_bias, norm_w)


def dn_bwd(y, proj, a_log, dt_bias, norm_w, ssave, isave, do, dproj, *, name):
    tp = y.shape[0]
    nc = tp // CHUNK
    dk = DN_HEAD_DIM
    per = CHUNKS_PER_STEP
    rev = lambda i: nc // per - 1 - i
    zs_width = 5 * LANE

    def body(y_ref, z_ref, small_ref, al_ref, dt_ref, nw_ref, ss_ref, is_ref, do_ref, _,
             dy_ref, dzs_ref, dal_ref, ddt_ref, dnw_ref, dstate):
        i = pl.program_id(0)
        n = nc // per - 1 - i

        @pl.when(i == 0)
        def _():
            dstate[...] = jnp.zeros_like(dstate)
            dal_ref[...] = jnp.zeros_like(dal_ref)
            ddt_ref[...] = jnp.zeros_like(ddt_ref)
            dnw_ref[...] = jnp.zeros_like(dnw_ref)

        ds = dstate[...]
        for c in reversed(range(per)):
            at = pl.ds(c * CHUNK, CHUNK)
            token_rows = _chunk_rows(per * n + c)
            known_inv = is_ref[c]
            fn = lambda *a: _dn_chunk(*a, token_rows, known_inv)[:2]
            _, vjp = jax.vjp(fn, y_ref[at, :], z_ref[at, :], small_ref[at, :], ss_ref[c], al_ref[...], dt_ref[...],
                             nw_ref[...])
            dy, dz, dsmall, ds, dal, ddt, dnw = vjp((do_ref[at, :], ds))
            dy_ref[at, :] = dy
            dzs_ref[at, :4 * LANE] = dz.astype(BF16)
            dzs_ref[at, 4 * LANE:] = dsmall.astype(BF16)
            dal_ref[...] += dal
            ddt_ref[...] += ddt
            dnw_ref[...] += dnw
        dstate[...] = ds

    rows = per * CHUNK
    return pl.pallas_call(
        body, name=name, grid=(nc // per,),
        in_specs=[pl.BlockSpec((rows, y.shape[1]), lambda i: (rev(i), 0)),
                  pl.BlockSpec((rows, 4 * LANE), lambda i: (rev(i), Z_COL)),
                  pl.BlockSpec((rows, LANE), lambda i: (rev(i), SMALL_COL)),
                  _full((1, LANE)), _full((1, LANE)), _full((1, LANE)),
                  pl.BlockSpec((per, DN_HEADS, dk, dk), lambda i: (rev(i), 0, 0, 0)),
                  pl.BlockSpec((per, DN_HEADS, CHUNK, CHUNK), lambda i: (rev(i), 0, 0, 0)),
                  pl.BlockSpec((rows, 4 * LANE), lambda i: (rev(i), 1)),
                  pl.BlockSpec(memory_space=pl.ANY)],
        out_specs=(pl.BlockSpec((rows, y.shape[1]), lambda i: (rev(i), 0)),
                   pl.BlockSpec((rows, zs_width), lambda i: (rev(i), Z_COL * 4 * LANE // zs_width)),
                   _full((1, LANE)), _full((1, LANE)), _full((1, LANE))),
        out_shape=(jax.ShapeDtypeStruct((tp, y.shape[1]), F32), jax.ShapeDtypeStruct(dproj.shape, dproj.dtype),
                   jax.ShapeDtypeStruct((1, LANE), F32), jax.ShapeDtypeStruct((1, LANE), F32),
                   jax.ShapeDtypeStruct((1, LANE), F32)),
        scratch_shapes=[pltpu.VMEM((DN_HEADS, dk, dk), F32)],
        input_output_aliases={9: 1},
    )(y, proj, proj, a_log, dt_bias, norm_w, ssave, isave, do, dproj)


def _gla_chunk(q, k, v, gate, low, s, w_gate_up, b_gate, norm_w, rows):
    tri_incl, _, _ = _chunk_masks()
    dk, dv, nh = GLA_DK, GLA_DV, GLA_HEADS
    heads = lambda t, width: _stack([t[:, h * width:(h + 1) * width] for h in range(nh)])
    logit = _mm3(low, w_gate_up) + b_gate
    glog_all = -_softplus(-logit) * (1.0 / GLA_GATE_NORM) * rows
    glog = heads(glog_all, dk)
    bcum = heads(_cumsum_rows(glog_all), dk)
    qh = heads(q, dk) * dk ** -0.5
    kh = heads(k, dk)
    vh = heads(v, dv)
    q_dec = qh * jnp.exp(bcum)
    attn = _bmm_nt(q_dec, kh * jnp.exp(-bcum)) * tri_incl
    b_last = jnp.sum(glog, axis=1, keepdims=True)
    k_dec = kh * jnp.exp(b_last - bcum)
    r = lax.broadcasted_iota(jnp.int32, (dk, dk), 0)
    c = lax.broadcasted_iota(jnp.int32, (dk, dk), 1)
    b_last_col = jnp.sum((r == c).astype(F32) * b_last, axis=2, keepdims=True)
    o = _bmm(attn, vh) + _bmm(q_dec, s)
    s_new = s * jnp.exp(b_last_col) + _bmm_tn(k_dec, vh)
    out = _rms(o, norm_w) * _silu(heads(gate, dv))
    return jnp.concatenate([out[h] for h in range(nh)], axis=1), s_new


LOW_COL = 24


def _gla_in_specs(step, rows=CHUNK):
    return [pl.BlockSpec((rows, 4 * LANE), lambda i: (step(i), 0)),
            pl.BlockSpec((rows, 4 * LANE), lambda i: (step(i), 1)),
            pl.BlockSpec((rows, 8 * LANE), lambda i: (step(i), 1)),
            pl.BlockSpec((rows, 8 * LANE), lambda i: (step(i), 2)),
            pl.BlockSpec((rows, LANE), lambda i: (step(i), LOW_COL)),
            _full((LANE, 4 * LANE)), _full((1, 4 * LANE)), _full((1, GLA_DV))]


def gla_fwd(proj, w_gate_up, b_gate, norm_w, *, name):
    tp = proj.shape[0]
    nc = tp // CHUNK
    per = CHUNKS_PER_STEP
    rows = per * CHUNK

    def body(q_ref, k_ref, v_ref, g_ref, low_ref, wgu_ref, bg_ref, nw_ref, o_ref, ssave_ref, state):
        n = pl.program_id(0)

        @pl.when(n == 0)
        def _():
            state[...] = jnp.zeros_like(state)

        s = state[...]
        for c in range(per):
            at = pl.ds(c * CHUNK, CHUNK)
            ssave_ref[c] = s
            out, s = _gla_chunk(q_ref[at, :], k_ref[at, :], v_ref[at, :], g_ref[at, :], low_ref[at, :], s, wgu_ref[...],
                                bg_ref[...], nw_ref[...], _chunk_rows(per * n + c))
            o_ref[at, :] = out
        state[...] = s

    return pl.pallas_call(
        body, name=name, grid=(nc // per,),
        in_specs=_gla_in_specs(lambda i: i, rows),
        out_specs=(pl.BlockSpec((rows, 8 * LANE), lambda n: (n, 0)),
                   pl.BlockSpec((per, GLA_HEADS, GLA_DK, GLA_DV), lambda n: (n, 0, 0, 0))),
        out_shape=(jax.ShapeDtypeStruct((tp, 8 * LANE), F32),
                   jax.ShapeDtypeStruct((nc, GLA_HEADS, GLA_DK, GLA_DV), F32)),
        scratch_shapes=[pltpu.VMEM((GLA_HEADS, GLA_DK, GLA_DV), F32)],
    )(proj, proj, proj, proj, proj, w_gate_up, b_gate, norm_w)


def gla_bwd(proj, w_gate_up, b_gate, norm_w, ssave, do, *, name):
    tp = proj.shape[0]
    nc = tp // CHUNK
    per = CHUNKS_PER_STEP
    rev = lambda i: nc // per - 1 - i

    def body(q_ref, k_ref, v_ref, g_ref, low_ref, wgu_ref, bg_ref, nw_ref, ss_ref, do_ref,
             dproj_ref, dwgu_ref, dbg_ref, dnw_ref, dstate):
        i = pl.program_id(0)
        n = nc // per - 1 - i

        @pl.when(i == 0)
        def _():
            dstate[...] = jnp.zeros_like(dstate)
            dwgu_ref[...] = jnp.zeros_like(dwgu_ref)
            dbg_ref[...] = jnp.zeros_like(dbg_ref)
            dnw_ref[...] = jnp.zeros_like(dnw_ref)

        ds = dstate[...]
        for c in reversed(range(per)):
            at = pl.ds(c * CHUNK, CHUNK)
            token_rows = _chunk_rows(per * n + c)
            fn = lambda *a: _gla_chunk(*a, token_rows)
            _, vjp = jax.vjp(fn, q_ref[at, :], k_ref[at, :], v_ref[at, :], g_ref[at, :], low_ref[at, :], ss_ref[c],
                             wgu_ref[...], bg_ref[...], nw_ref[...])
            dq, dk, dv, dg, dlow, ds, dwgu, dbg, dnw = vjp((do_ref[at, :], ds))
            off = 0
            for part in (dq, dk, dv, dg, dlow):
                dproj_ref[at, off:off + part.shape[1]] = part.astype(BF16)
                off += part.shape[1]
            dwgu_ref[...] += dwgu
            dbg_ref[...] += dbg
            dnw_ref[...] += dnw
        dstate[...] = ds

    chunk = lambda width: pl.BlockSpec((per * CHUNK, width), lambda i: (rev(i), 0))
    return pl.pallas_call(
        body, name=name, grid=(nc // per,),
        in_specs=_gla_in_specs(rev, per * CHUNK) + [pl.BlockSpec((per, GLA_HEADS, GLA_DK, GLA_DV), lambda i: (rev(i), 0, 0, 0)),
                                                    chunk(8 * LANE)],
        out_specs=(chunk(PROJ_DIM), _full((LANE, 4 * LANE)), _full((1, 4 * LANE)), _full((1, GLA_DV))),
        out_shape=(jax.ShapeDtypeStruct((tp, PROJ_DIM), BF16), jax.ShapeDtypeStruct((LANE, 4 * LANE), F32),
                   jax.ShapeDtypeStruct((1, 4 * LANE), F32), jax.ShapeDtypeStruct((1, GLA_DV), F32)),
        scratch_shapes=[pltpu.VMEM((GLA_HEADS, GLA_DK, GLA_DV), F32)],
    )(proj, proj, proj, proj, proj, w_gate_up, b_gate, norm_w, ssave, do)


def _even_proj_weight(w_t):
    hd = SWA_HEAD_DIM
    k0, k1 = w_t[512:512 + hd], w_t[512 + hd:640]
    v0, v1 = w_t[640:640 + hd], w_t[640 + hd:768]
    zeros = jnp.zeros((LANE - 2 * DN_HEADS, w_t.shape[1]), w_t.dtype)
    return jnp.concatenate([w_t[:512], k0, k0, k1, k1, v0, v0, v1, v1, w_t[768:2816], w_t[2820:2824], w_t[2816:2820],
                            zeros], axis=0)


def _even_proj_weight_grad(dw):
    hd = SWA_HEAD_DIM
    c = lambda i: dw[512 + i * hd:512 + (i + 1) * hd]
    return jnp.concatenate([dw[:512], c(0) + c(1), c(2) + c(3), c(4) + c(5), c(6) + c(7), dw[1024:3072],
                            dw[3076:3080], dw[3072:3076]], axis=0)


def _ffn_fwd(h, nw_in, nw_out, wts, idx, get_w):
    wts.update(get_w(f"ffn{idx}", h))
    w_gu = wts[f"w_gu{idx}"]
    hn, g, u, a = rms_mm(h, nw_in, w_gu[0], swiglu=True, name=f"ffn_up_{idx}", widx=w_gu[1])
    wts.update(get_w(f"down{idx}", a))
    w_down = wts[f"w_down{idx}"]
    f, h_out = mm_rms_res([a], w_down[0], h, nw_out, scale=0.5, name=f"ffn_down_{idx}", widx=w_down[1])
    return h_out, (h, hn, g, u, a, f)


def _ffn_bwd(dho, saved, nw_in, nw_out, w_gu, w_down, idx, on_grads):
    h, hn, g, u, a, f = saved
    df, dnw_out, dgu = mm_rms_res_bwd(dho, f, nw_out, w_down[0], (g, u), scale=0.5, name=f"ffn_down_bwd_{idx}",
                                      widx=w_down[1])
    g_down = mm_tn(a, df, name=f"ffn_dwd_{idx}", out_dtype=BF16)
    sent = on_grads("down", g_down)
    g_gu = mm_tn(dgu, hn, name=f"ffn_dwgu_{idx}", out_dtype=BF16, after=sent)
    sent = on_grads("gu", g_gu)
    dh, dnw_in = rms_mm_bwd([dgu], w_gu[0], h, nw_in + sent, dho, name=f"ffn_up_bwd_{idx}", widx=w_gu[1])
    return dh, dnw_in, dnw_out


def local_step(x, target, wts, get_w=None, put_g=None):
    seq, d = x.shape
    wts = dict(wts)
    get_w = get_w or (lambda stage, after: {})
    put_g = put_g or (lambda stage, grads: jnp.zeros((1, 1), F32))
    row = lambda v: v.reshape(1, -1)
    lane_row = lambda v: jnp.pad(v.reshape(1, -1), ((0, 0), (0, LANE - v.size)))
    nw = wts["norm_w"]
    h = jnp.concatenate([jnp.zeros((PAD, d), F32), wts["meta_tokens"], x], axis=0)
    buckets = _swa_buckets()
    bias = swa_bias(wts["rel_bias_table"], buckets, name="swa_bias")
    sinks = lane_row(wts["swa_sinks"])
    a_log, dt_bias = lane_row(wts["dn_a_log"]), lane_row(wts["dn_dt_bias"])
    dn_norm_w = row(wts["dn_norm_w"])
    conv_w = wts["even_conv_w"][0]
    w_gate_up = jnp.pad(wts["gla_w_gate_up"][0], ((0, LANE - GLA_GATE_RANK), (0, 0)))
    b_gate, gla_norm_w = row(wts["gla_b_gate"]), row(wts["gla_norm_w"])

    saved = []
    w_in, w_out = [None, None], [None, None]
    for l in range(2):
        h, s_a = _ffn_fwd(h, row(nw[l, 0]), row(nw[l, 1]), wts, 2 * l, get_w)
        if l == 0:
            wts.update(get_w("even", h))
            w_in[0], w_out[0] = _even_proj_weight(wts["even_w_in"]), wts["even_w_out"]
        else:
            wts.update(get_w("odd", h))
            w_in[1] = jnp.pad(wts["odd_w_in"], ((0, PROJ_DIM - wts["odd_w_in"].shape[0]), (0, 0)))
            w_out[1] = wts["odd_w_out"]
        h_mix = h
        hn, proj = rms_mm(h, row(nw[l, 2]), w_in[l], swiglu=False, name=f"mix_in_{l}")
        if l == 0:
            o_a = swa_fwd(proj, bias, sinks, name="swa_fwd")
            y = conv_fwd(proj, conv_w, name="conv_fwd")
            o_b, ssave, isave = dn_fwd(y, proj, a_log, dt_bias, dn_norm_w, name="dn_fwd")
            acts, extra = [o_a, o_b], (y, ssave, isave)
        else:
            o, ssave = gla_fwd(proj, w_gate_up, b_gate, gla_norm_w, name="gla_fwd")
            acts, extra = [o], (ssave,)
        mix, h = mm_rms_res(acts, w_out[l], h, row(nw[l, 3]), scale=1.0, name=f"mix_out_{l}")
        s_m = (h_mix, hn, proj, acts, extra, mix)
        h, s_b = _ffn_fwd(h, row(nw[l, 4]), row(nw[l, 5]), wts, 2 * l + 1, get_w)
        saved.append((s_a, s_m, s_b))

    dh, loss = loss_and_grad(h, target, name="loss")

    grads = {}
    dnw = [[None] * 6 for _ in range(2)]
    def on_grads(i):
        def put(which, g):
            grads[f"g_{which}{i}"] = g
            return put_g(f"{which}{i}", grads)
        return put

    for l in (1, 0):
        s_a, s_m, s_b = saved[l]
        i = 2 * l + 1
        dh, dnw[l][4], dnw[l][5] = _ffn_bwd(dh, s_b, row(nw[l, 4]), row(nw[l, 5]), wts[f"w_gu{i}"], wts[f"w_down{i}"],
                                            i, on_grads(i))
        h_mix, hn, proj, acts, extra, mix = s_m
        dmix, dnw[l][3], do = mm_rms_res_bwd(dh, mix, row(nw[l, 3]), w_out[l], None, scale=1.0, name=f"mix_out_bwd_{l}")
        dw_out = jnp.concatenate([mm_tn(a, dmix, name=f"mix_dwo_{l}_{i}") for i, a in enumerate(acts)], axis=0)
        sent = jnp.zeros((1, 1), F32)
        if l == 0:
            y, ssave, isave = extra
            dproj, dbias, dsinks = swa_bwd(proj, bias, sinks, do, name="swa_bwd")
            dy, dproj, da_log, ddt_bias, ddn_norm_w = dn_bwd(y, proj, a_log, dt_bias, dn_norm_w, ssave, isave, do, dproj,
                                                             name="dn_bwd")
            dproj, dconv_w = conv_bwd(proj, conv_w, dy, dproj, name="conv_bwd")
            grads["rel_bias_table"] = swa_bias_bwd(dbias, buckets, name="swa_bias_bwd")[:, :SWA_Q_HEADS]
            grads["swa_sinks"] = dsinks[:, :SWA_Q_HEADS]
            grads["dn_a_log"] = da_log[:, :DN_HEADS]
            grads["dn_dt_bias"] = ddt_bias[:, :DN_HEADS]
            grads["dn_norm_w"] = ddn_norm_w
            grads["even_conv_w"] = dconv_w[None]
            grads["even_w_out"] = dw_out
        else:
            (ssave,) = extra
            dproj, dwgu, dbg, dgnw = gla_bwd(proj, w_gate_up, b_gate, gla_norm_w, ssave, do, name="gla_bwd")
            grads["gla_w_gate_up"] = dwgu[None, :GLA_GATE_RANK]
            grads["gla_b_gate"] = dbg
            grads["gla_norm_w"] = dgnw
            grads["odd_w_out"] = dw_out
        dw_in = mm_tn(dproj, hn, name=f"mix_dwi_{l}")
        if l == 0:
            grads["even_w_in"] = _even_proj_weight_grad(dw_in)
            sent = put_g("even", grads)
        else:
            grads["odd_w_in"] = dw_in[:wts["odd_w_in"].shape[0]]
        dh, dnw[l][2] = rms_mm_bwd([dproj], w_in[l], h_mix, row(nw[l, 2]) + sent, dh, name=f"mix_in_bwd_{l}")
        i = 2 * l
        dh, dnw[l][0], dnw[l][1] = _ffn_bwd(dh, s_a, row(nw[l, 0]), row(nw[l, 1]), wts[f"w_gu{i}"], wts[f"w_down{i}"],
                                            i, on_grads(i))

    grads["norm_w"] = jnp.stack([jnp.concatenate(r, axis=0) for r in dnw])
    grads["meta_tokens"] = dh[PAD:PAD + N_META]
    return loss[0, 0], dh[PAD + N_META:], grads


def _peer(k):
    x, y, c = (lax.axis_index(a) for a in AXES)
    flip = lambda v, bit: 1 - v if bit else v
    return (flip(x, k & 4), flip(y, k & 2), flip(c, k & 1))


def _my_index():
    x, y, c = (lax.axis_index(a) for a in AXES)
    return 4 * x + 2 * y + c


_HBM = pl.BlockSpec(memory_space=pltpu.HBM)
_SEM = pl.BlockSpec(memory_space=pltpu.SEMAPHORE)
_EFFECT = pltpu.SideEffectType.DATAFLOW_SIDE_EFFECTING


def _remote_copies(items, src_refs, land_refs, send_sems, recv_sems):
    me = _my_index()
    copies = []
    for k in range(1, N_DEV):
        px, py, pc = _peer(k)
        pj = 4 * px + 2 * py + pc
        for a, (sn, send, ln, land, _) in enumerate(items):
            sem = (k - 1) * len(items) + a
            copies.append(pltpu.make_async_remote_copy(
                src_ref=send(src_refs[sn], pj), dst_ref=land(land_refs[ln], me), send_sem=send_sems.at[sem],
                recv_sem=recv_sems.at[sem], device_id=(px, py, pc), device_id_type=MESH))
    return copies


def exchange(srcs, lands, items, after, *, name):
    sn, ln = list(srcs), list(lands)

    def body(*refs):
        src_refs = dict(zip(sn, refs[:len(sn)]))
        land_refs = dict(zip(ln, refs[len(sn) + len(ln) + 1:len(sn) + 2 * len(ln) + 1]))
        send_sems, recv_sems = refs[len(sn) + 2 * len(ln) + 1:]
        copies = _remote_copies(items, src_refs, land_refs, send_sems, recv_sems)
        for cp in copies:
            cp.start()
        for cp in copies:
            cp.wait_recv()
        for cp in copies:
            cp.wait_send()

    n_remote = (N_DEV - 1) * len(items)
    outs = pl.pallas_call(
        body, name=name,
        in_specs=[pl.BlockSpec(memory_space=pl.ANY)] * (len(sn) + len(ln) + 1),
        out_specs=tuple(pl.BlockSpec(memory_space=pl.ANY) for _ in ln),
        out_shape=tuple(jax.ShapeDtypeStruct(lands[n].shape, lands[n].dtype) for n in ln),
        input_output_aliases={len(sn) + i: i for i in range(len(ln))},
        scratch_shapes=[pltpu.SemaphoreType.DMA((n_remote,)), pltpu.SemaphoreType.DMA((n_remote,))],
    )(*[srcs[n] for n in sn], *[lands[n] for n in ln], after)
    return dict(zip(ln, outs))


def start_copies(srcs, lands, items, *, name):
    sn, ln = list(srcs), list(lands)
    n_remote = (N_DEV - 1) * len(items)

    def body(*refs):
        src_refs = dict(zip(sn, refs[:len(sn)]))
        land_refs = dict(zip(ln, refs[len(sn):len(sn) + len(ln)]))
        send_sems, recv_sems = refs[len(sn) + len(ln):len(sn) + len(ln) + 2]
        token = refs[-1]
        for cp in _remote_copies(items, src_refs, land_refs, send_sems, recv_sems):
            cp.start()
        token[...] = jnp.zeros_like(token)

    hbm = lambda a: pltpu.with_memory_space_constraint(a, pltpu.HBM)
    outs = pl.pallas_call(
        body, name=name,
        in_specs=[_HBM] * (len(sn) + len(ln)),
        out_specs=(_SEM, _SEM) + (_HBM,) * len(ln) + (pl.BlockSpec(memory_space=pltpu.VMEM),),
        out_shape=(pltpu.SemaphoreType.DMA((n_remote,)), pltpu.SemaphoreType.DMA((n_remote,)))
        + tuple(pltpu.HBM(lands[n].shape, lands[n].dtype) for n in ln) + (jax.ShapeDtypeStruct((8, LANE), F32),),
        input_output_aliases={len(sn) + i: 2 + i for i in range(len(ln))},
        compiler_params=pltpu.CompilerParams(has_side_effects=_EFFECT),
    )(*[hbm(srcs[n]) for n in sn], *[hbm(lands[n]) for n in ln])
    return (outs[0], outs[1]), dict(zip(ln, outs[2:2 + len(ln)])), outs[-1][0:1, 0:1]


def wait_copies(sems, srcs, lands, items, after, *, name):
    sn, ln = list(srcs), list(lands)

    def body(*refs):
        src_refs = dict(zip(sn, refs[:len(sn)]))
        land_refs = dict(zip(ln, refs[len(sn):len(sn) + len(ln)]))
        send_sems, recv_sems = refs[len(sn) + len(ln):len(sn) + len(ln) + 2]
        copies = _remote_copies(items, src_refs, land_refs, send_sems, recv_sems)
        for cp in copies:
            cp.wait_send()
        for cp in copies:
            cp.wait_recv()

    outs = pl.pallas_call(
        body, name=name,
        in_specs=[_HBM] * (len(sn) + len(ln)) + [_SEM, _SEM, pl.BlockSpec(memory_space=pl.ANY)],
        out_specs=(_HBM,) * len(ln),
        out_shape=tuple(pltpu.HBM(lands[n].shape, lands[n].dtype) for n in ln),
        input_output_aliases={len(sn) + i: i for i in range(len(ln))},
        compiler_params=pltpu.CompilerParams(has_side_effects=_EFFECT),
    )(*[srcs[n] for n in sn], *[lands[n] for n in ln], sems[0], sems[1], after)
    return dict(zip(ln, outs))


def _block(index, size, base=0):
    return pl.ds(pl.multiple_of(base + index * size, ROW_TILE), size)


def _adam_tile(rows):
    for t in (256, 176, 128):
        if rows % t == 0:
            return t
    return rows


def sum_adamw(recvs, w, m, v, *, name, first_slab=0, into=None):
    _, r, c = w.shape
    b = len(recvs)
    rp = recvs[0].shape[1]
    whole = r % ROW_TILE != 0
    tr = r if whole else _adam_tile(r)
    c1 = 1.0 / (1.0 - ADAM_B1 ** ADAM_STEP)
    c2 = 1.0 / (1.0 - ADAM_B2 ** ADAM_STEP)

    def body(*refs):
        recv_refs = refs[:b]
        w_ref, m_ref, v_ref = refs[b:b + 3]
        g_ref, d_ref, nm_ref, nv_ref = refs[b + 3 + (0 if into is None else 4):][:4]
        for slab, recv_ref in enumerate(recv_refs):
            @pl.when(pl.program_id(0) == slab)
            def _():
                g = recv_ref[0].astype(F32)
                for i in range(1, N_DEV):
                    g = g + recv_ref[i].astype(F32)
                if whole:
                    sum_ref = refs[-1]
                    sum_ref[...] = g
                    g = sum_ref[0:r, :]
                nm = ADAM_B1 * m_ref[0] + (1.0 - ADAM_B1) * g
                nv = ADAM_B2 * v_ref[0] + (1.0 - ADAM_B2) * (g * g)
                g_ref[0] = g
                nm_ref[0] = nm
                nv_ref[0] = nv
                d_ref[0] = -ADAM_LR * ((nm * c1) / (jnp.sqrt(nv * c2) + ADAM_EPS) + ADAM_WD * w_ref[0])

    tile = pl.BlockSpec((1, tr, c), lambda bi, i: (first_slab + bi, i, 0))
    piece = lambda slab: pl.BlockSpec((N_DEV, rp if whole else tr, c), lambda bi, i: (0, jnp.where(bi == slab, i, 0), 0))
    earlier = [] if into is None else list(into)
    return pl.pallas_call(
        body, name=name, grid=(b, r // tr),
        in_specs=[piece(slab) for slab in range(b)] + [tile, tile, tile] + [pl.BlockSpec(memory_space=pl.ANY)] * len(earlier),
        out_specs=(tile,) * 4, out_shape=(jax.ShapeDtypeStruct(w.shape, F32),) * 4,
        input_output_aliases={b + 3 + i: i for i in range(len(earlier))},
        scratch_shapes=[pltpu.VMEM((rp, c), F32)] if whole else [],
    )(*recvs, w, m, v, *earlier)


def _flat_rows(n_elems, row_multiple):
    rows = -(-n_elems // FLAT_COLS)
    return -(-rows // row_multiple) * row_multiple


def _pack(arrays, row_multiple, dtype):
    flat = jnp.concatenate([a.reshape(-1).astype(dtype) for a in arrays])
    rows = _flat_rows(flat.size, row_multiple)
    return jnp.pad(flat, (0, rows * FLAT_COLS - flat.size)).reshape(rows, FLAT_COLS)


def _unpack(flat2d, shapes):
    lead = flat2d.shape[:-2]
    flat = flat2d.reshape(lead + (-1,))
    out, off = [], 0
    for shp in shapes:
        n = int(np.prod(shp))
        out.append(flat[..., off:off + n].reshape(lead + tuple(shp)))
        off += n
    return out


def _join_shards(stacked, axis):
    moved = jnp.moveaxis(stacked, 0, axis)
    shp = list(moved.shape)
    shp[axis:axis + 2] = [shp[axis] * shp[axis + 1]]
    return moved.reshape(shp)


def _split_shards(full, axis):
    shp = list(full.shape)
    shp[axis:axis + 1] = [N_DEV, shp[axis] // N_DEV]
    return jnp.moveaxis(full.reshape(shp), axis, 0)


def kernel(x, meta_tokens, norm_w, ffn_w_gate, ffn_w_up, ffn_w_down, rel_bias_table, even_w_in, even_conv_w, swa_sinks, dn_a_log, dn_dt_bias, dn_norm_w, even_w_out, odd_w_in, gla_w_gate_up, gla_b_gate, gla_norm_w, odd_w_out, loss_target, m_meta_tokens, m_norm_w, m_ffn_w_gate, m_ffn_w_up, m_ffn_w_down, m_rel_bias_table, m_even_w_in, m_even_conv_w, m_swa_sinks, m_dn_a_log, m_dn_dt_bias, m_dn_norm_w, m_even_w_out, m_odd_w_in, m_gla_w_gate_up, m_gla_b_gate, m_gla_norm_w, m_odd_w_out, v_meta_tokens, v_norm_w, v_ffn_w_gate, v_ffn_w_up, v_ffn_w_down, v_rel_bias_table, v_even_w_in, v_even_conv_w, v_swa_sinks, v_dn_a_log, v_dn_dt_bias, v_dn_norm_w, v_even_w_out, v_odd_w_in, v_gla_w_gate_up, v_gla_b_gate, v_gla_norm_w, v_odd_w_out):
    args = locals()
    w = {n: args[n] for n in WEIGHTS}
    m = {n: args["m_" + n] for n in WEIGHTS}
    v = {n: args["v_" + n] for n in WEIGHTS}

    d = D_MODEL
    me = _my_index()
    whole = lambda ref, j: ref
    rows = lambda size, base=0: (lambda ref, i: ref.at[_block(i, size, base), :])
    lead = lambda ref, i: ref.at[i]
    of_group = lambda items, g: [it for it in items if it[4] == g]
    names = lambda items, k: list(dict.fromkeys(it[k] for it in items))

    def placed(shape, dtype, parts):
        land = lax.empty(shape, dtype)
        for part, axis, start in parts:
            land = lax.dynamic_update_slice(land, part, tuple(start if a == axis else 0 for a in range(land.ndim)))
        return land

    as_rows = lambda a: jnp.swapaxes(a, -1, -2)
    pad_rows = lambda a, to: jnp.pad(a, [(0, 0)] * (a.ndim - 2) + [(0, to - a.shape[-2]), (0, 0)])
    gate_s = pad_rows(as_rows(w["ffn_w_gate"].reshape(N_FFN, d, FF_SHARD)), FF_SHARD_PAD).astype(BF16)
    up_s = pad_rows(as_rows(w["ffn_w_up"].reshape(N_FFN, d, FF_SHARD)), FF_SHARD_PAD).astype(BF16)
    down_s = pad_rows(w["ffn_w_down"].reshape(N_FFN, FF_SHARD, d), FF_SHARD_PAD).astype(BF16)
    small_s = _pack([w[n] for n in SMALL], 8, F32)
    srcs_w = {"ein": pad_rows(as_rows(w["even_w_in"][0]), EVEN_IN_SHARD_PAD).astype(BF16),
              "oin": pad_rows(as_rows(w["odd_w_in"][0]), ODD_IN_SHARD_PAD).astype(BF16),
              "eout": w["even_w_out"][0].astype(BF16), "oout": w["odd_w_out"][0].astype(BF16), "small": small_s}
    lands_w = {"ein": placed((N_DEV * EVEN_IN_SHARD_PAD, d), BF16, [(srcs_w["ein"], 0, me * EVEN_IN_SHARD_PAD)]),
               "oin": placed((N_DEV * ODD_IN_SHARD_PAD, d), BF16, [(srcs_w["oin"], 0, me * ODD_IN_SHARD_PAD)]),
               "eout": placed((d, d), BF16, [(srcs_w["eout"], 0, me * OUT_SHARD)]),
               "oout": placed((d, d), BF16, [(srcs_w["oout"], 0, me * OUT_SHARD)]),
               "small": placed((N_DEV,) + small_s.shape, F32, [(small_s[None], 0, me)])}
    items_w = [("small", whole, "small", lead, "first"), ("ein", whole, "ein", rows(EVEN_IN_SHARD_PAD), "even"),
               ("eout", whole, "eout", rows(OUT_SHARD), "even"), ("oin", whole, "oin", rows(ODD_IN_SHARD_PAD), "odd"),
               ("oout", whole, "oout", rows(OUT_SHARD), "odd")]
    for i, group, down_group in ((0, "first", "down0"), (1, "ffn1", "ffn1"), (2, "ffn2", "down2"), (3, "ffn3", "ffn3")):
        srcs_w.update({f"gate{i}": gate_s[i], f"up{i}": up_s[i], f"down{i}": down_s[i]})
        lands_w[f"w_gu{i}"] = placed((2 * FF_PAD, d), BF16, [(srcs_w[f"gate{i}"], 0, me * FF_SHARD_PAD),
                                                             (srcs_w[f"up{i}"], 0, FF_PAD + me * FF_SHARD_PAD)])
        lands_w[f"w_down{i}"] = placed((FF_PAD, d), BF16, [(srcs_w[f"down{i}"], 0, me * FF_SHARD_PAD)])
        items_w += [(f"gate{i}", whole, f"w_gu{i}", rows(FF_SHARD_PAD), group),
                    (f"up{i}", whole, f"w_gu{i}", rows(FF_SHARD_PAD, FF_PAD), group),
                    (f"down{i}", whole, f"w_down{i}", rows(FF_SHARD_PAD), down_group)]
    pending, started = {}, []
    for g in ("first", "down0", "even", "ffn1", "ffn2", "down2", "odd", "ffn3"):
        its = of_group(items_w, g)
        srcs = {n: srcs_w[n] for n in names(its, 0)}
        sems, lands, token = start_copies(srcs, {n: lands_w[n] for n in names(its, 2)}, its, name=f"gather_start_{g}")
        pending[g] = (sems, srcs, lands, its)
        started.append(token)

    unpad = lambda p, shard, shard_pad: p.reshape(N_DEV, shard_pad, d)[:, :shard].reshape(N_DEV * shard, d)

    def get_w(stage, after):
        if stage not in pending:
            return {}
        sems, srcs, lands, its = pending[stage]
        landed = wait_copies(sems, srcs, lands, its, after, name=f"gather_wait_{stage}")
        got = {}
        for n, arr in landed.items():
            if n == "small":
                for sn, stacked in zip(SMALL, _unpack(arr, [w[sn].shape for sn in SMALL])):
                    got[sn] = _join_shards(stacked, SHARD_AXIS[sn])
            elif n == "ein":
                got["even_w_in"] = unpad(arr, EVEN_IN_SHARD, EVEN_IN_SHARD_PAD)
            elif n == "oin":
                got["odd_w_in"] = unpad(arr, ODD_IN_SHARD, ODD_IN_SHARD_PAD)
            elif n in ("eout", "oout"):
                got["even_w_out" if n == "eout" else "odd_w_out"] = arr
            else:
                got[n] = (arr, None)
        return got

    full = {n: w[n] for n in REPL}
    full.update(get_w("first", sum(started)))

    repad = lambda g, shard, shard_pad: pad_rows(g.reshape(N_DEV, shard, d), shard_pad)
    pieces_g = {"r_oin": ("oin", None, "gu2"), "r_oout": ("oout", (OUT_SHARD, 0), "gu2"),
                "r_ein": ("ein", None, "even"), "r_eout": ("eout", (OUT_SHARD, 0), "even"), "r_small": ("small", None, "last")}
    for i in range(N_FFN):
        pieces_g.update({f"r_gate{i}": (f"g_gu{i}", (FF_SHARD_PAD, 0), f"gu{i}"),
                         f"r_up{i}": (f"g_gu{i}", (FF_SHARD_PAD, FF_PAD), f"gu{i}"),
                         f"r_down{i}": (f"g_down{i}", (FF_SHARD_PAD, 0), "down0" if i == 0 else f"gu{i}")})
    items_g = [(src, lead if blk is None else rows(*blk), land, lead, group) for land, (src, blk, group) in pieces_g.items()]
    last_groups = ("down0", "gu0")

    def grad_src(n, grads):
        if n == "oin":
            return repad(grads["odd_w_in"], ODD_IN_SHARD, ODD_IN_SHARD_PAD).astype(BF16)
        if n == "ein":
            return repad(grads["even_w_in"], EVEN_IN_SHARD, EVEN_IN_SHARD_PAD).astype(BF16)
        if n in ("oout", "eout"):
            return grads["odd_w_out" if n == "oout" else "even_w_out"].astype(BF16)
        return grads[n]

    def grad_land(n, srcs):
        src, blk, _ = pieces_g[n]
        if blk is None:
            own = lax.dynamic_index_in_dim(srcs[src], me, 0, keepdims=False)
        else:
            own = lax.dynamic_slice_in_dim(srcs[src], blk[1] + me * blk[0], blk[0], 0)
        return placed((N_DEV,) + own.shape, own.dtype, [(own[None], 0, me)])

    sent = {}

    def put_g(stage, grads):
        its = of_group(items_g, stage)
        if not its:
            return jnp.zeros((1, 1), F32)
        srcs = {n: grad_src(n, grads) for n in names(its, 0)}
        lands = {n: grad_land(n, srcs) for n in names(its, 2)}
        sems, lands, token = start_copies(srcs, lands, its, name=f"grads_start_{stage}")
        sent[stage] = (sems, srcs, lands, its)
        return token

    loss, grad_x, grads = local_step(x[0], loss_target[0], full, get_w, put_g)
    loss = lax.psum(loss, AXES)

    order = SMALL + REPL
    pieces = [_split_shards(grads[n].reshape(full[n].shape), SHARD_AXIS[n]) if n in SHARD_AXIS
              else jnp.broadcast_to(grads[n].reshape(w[n].shape)[None], (N_DEV,) + w[n].shape) for n in order]
    flat = jnp.concatenate([p.reshape(N_DEV, -1) for p in pieces], axis=1)
    srows = _flat_rows(flat.shape[1], 8)
    grads["small"] = jnp.pad(flat, ((0, 0), (0, srows * FLAT_COLS - flat.shape[1]))).reshape(N_DEV, srows, FLAT_COLS)
    recv = {}
    for stage, (sems, srcs, lands, its) in sent.items():
        if stage not in last_groups:
            recv.update(wait_copies(sems, srcs, lands, its, grad_x, name=f"grads_wait_{stage}"))
    result = [{} for _ in range(4)]

    views = {"ffn_w_gate": (lambda a: as_rows(a.reshape(N_FFN, d, FF_SHARD)), lambda o, n: as_rows(o).reshape(w[n].shape)),
             "ffn_w_up": (lambda a: as_rows(a.reshape(N_FFN, d, FF_SHARD)), lambda o, n: as_rows(o).reshape(w[n].shape)),
             "ffn_w_down": (lambda a: a.reshape(N_FFN, FF_SHARD, d), lambda o, n: o.reshape(w[n].shape)),
             "even_w_in": (as_rows, lambda o, n: as_rows(o)), "odd_w_in": (as_rows, lambda o, n: as_rows(o)),
             "even_w_out": (lambda a: a, lambda o, n: o), "odd_w_out": (lambda a: a, lambda o, n: o)}

    def adam(n, recvs, first_slab=0, into=None):
        view = views[n][0]
        return sum_adamw(recvs, view(w[n]), view(m[n]), view(v[n]), name=f"adamw_{n}_{first_slab}",
                         first_slab=first_slab, into=into)

    def finish(n, outs):
        for r, o in zip(result, outs):
            r[n] = views[n][1](o, n)

    ffn_recv = (("ffn_w_gate", "r_gate"), ("ffn_w_up", "r_up"), ("ffn_w_down", "r_down"))
    early = {n: adam(n, [recv[f"{r}{i}"] for i in (1, 2, 3)], first_slab=1) for n, r in ffn_recv}
    for n, r in (("even_w_in", "r_ein"), ("odd_w_in", "r_oin"), ("even_w_out", "r_eout"), ("odd_w_out", "r_oout")):
        finish(n, adam(n, [recv[r]]))
    srcs = {"small": grads["small"]}
    recv.update(exchange(srcs, {"r_small": grad_land("r_small", srcs)}, of_group(items_g, "last"),
                         early["ffn_w_down"][0], name="exchange_small"))
    for stage in last_groups:
        sems, srcs, lands, its = sent[stage]
        recv.update(wait_copies(sems, srcs, lands, its, recv["r_small"], name=f"grads_wait_{stage}"))
    for n, r in ffn_recv:
        finish(n, adam(n, [recv[f"{r}0"]], into=early[n]))
    pack_local = lambda t: _pack([t[n] for n in order], 8, F32)[None]
    small_outs = sum_adamw([recv["r_small"]], pack_local(w), pack_local(m), pack_local(v), name="adamw_small")
    for r, o in zip(result, small_outs):
        r.update(zip(order, _unpack(o[0], [w[n].shape for n in order])))
    return (loss, grad_x[None], *[r[n] for r in result for n in WEIGHTS])
```

```python
import functools
import math

import numpy as np
import jax
import jax.numpy as jnp
from jax import lax
from jax.experimental import pallas as pl
from jax.experimental.pallas import tpu as pltpu

F32 = jnp.float32
BF16 = jnp.bfloat16
MESH = pl.DeviceIdType.MESH
AXES = ("x", "y", "c")
N_DEV = 8

D_MODEL = 1024
N_META = 16
D_FF = 2816
NORM_EPS = 1e-6
NEG_INF = -1e30
SWA_Q_HEADS = 8
SWA_HEAD_DIM = 64
SWA_WINDOW = 128
SWA_BLOCK = 128
REL_BUCKETS = 32
REL_MAX_DIST = 128
DN_HEADS = 4
DN_HEAD_DIM = 128
DN_CONV = 4
GLA_HEADS = 4
GLA_DK = 128
GLA_DV = 256
GLA_GATE_RANK = 16
GLA_GATE_NORM = 16.0
CHUNK = 64
CHUNKS_PER_STEP = 6
PAD = SWA_BLOCK - N_META
LANE = 128
PROJ_DIM = 3200

ADAM_LR = 0.001
ADAM_B1 = 0.9
ADAM_B2 = 0.999
ADAM_EPS = 1e-08
ADAM_WD = 0.01
ADAM_STEP = 10

ROW_TILE = 16
FF_SHARD = D_FF // N_DEV
FF_SHARD_PAD = -(-FF_SHARD // ROW_TILE) * ROW_TILE
FF_PAD = N_DEV * FF_SHARD_PAD
N_FFN = 4
EVEN_IN_SHARD, EVEN_IN_SHARD_PAD = 353, 368
ODD_IN_SHARD, ODD_IN_SHARD_PAD = 386, 400
OUT_SHARD = D_MODEL // N_DEV

FLAT_COLS = 128
BIG = ("ffn_w_gate", "ffn_w_up", "ffn_w_down", "even_w_in", "even_w_out", "odd_w_in", "odd_w_out")
SMALL = ("meta_tokens", "norm_w", "even_conv_w", "gla_w_gate_up", "gla_b_gate", "gla_norm_w")
REPL = ("rel_bias_table", "swa_sinks", "dn_a_log", "dn_dt_bias", "dn_norm_w")
WEIGHTS = ("meta_tokens", "norm_w", "ffn_w_gate", "ffn_w_up", "ffn_w_down", "rel_bias_table", "even_w_in",
           "even_conv_w", "swa_sinks", "dn_a_log", "dn_dt_bias", "dn_norm_w", "even_w_out", "odd_w_in",
           "gla_w_gate_up", "gla_b_gate", "gla_norm_w", "odd_w_out")
SHARD_AXIS = {"ffn_w_gate": 3, "ffn_w_up": 3, "ffn_w_down": 2, "even_w_in": 2, "even_w_out": 1, "odd_w_in": 2,
              "odd_w_out": 1, "meta_tokens": 1, "norm_w": 2, "even_conv_w": 2, "gla_w_gate_up": 2,
              "gla_b_gate": 1, "gla_norm_w": 1}


def _rms(x, w):
    r = lax.rsqrt(jnp.mean(x * x, axis=-1, keepdims=True) + NORM_EPS)
    return x * r * w


def _sigmoid(x):
    return 0.5 * (jnp.tanh(0.5 * x) + 1.0)


def _silu(x):
    return x * _sigmoid(x)


def _softplus(x):
    pos = x > 0
    return jnp.where(pos, x, 0.0) + jnp.log(1.0 + jnp.exp(jnp.where(pos, -x, x)))


def _l2n(x):
    return x * lax.rsqrt(jnp.sum(x * x, axis=-1, keepdims=True) + 1e-6)


def _split_bf16(x):
    hi = x.astype(BF16)
    return hi, (x - hi.astype(F32)).astype(BF16)


def _make_mm(terms, batched):
    off = 1 if batched else 0
    bdims = ((0,), (0,)) if batched else ((), ())

    def dg(a, b, ca, cb):
        dot = lambda p, q: lax.dot_general(p, q, (((ca + off,), (cb + off,)), bdims), preferred_element_type=F32)
        a_hi, a_lo = _split_bf16(a)
        b_hi, b_lo = _split_bf16(b)
        if terms == 1:
            return dot(a_hi, b_hi)
        return dot(a_hi, b_hi) + (dot(a_hi, b_lo) + dot(a_lo, b_hi))

    @jax.custom_vjp
    def nn(a, b):
        return dg(a, b, 1, 0)

    @jax.custom_vjp
    def nt(a, b):
        return dg(a, b, 1, 1)

    @jax.custom_vjp
    def tn(a, b):
        return dg(a, b, 0, 0)

    nn.defvjp(lambda a, b: (nn(a, b), (a, b)), lambda r, g: (nt(g, r[1]), tn(r[0], g)))
    nt.defvjp(lambda a, b: (nt(a, b), (a, b)), lambda r, g: (nn(g, r[1]), tn(g, r[0])))
    tn.defvjp(lambda a, b: (tn(a, b), (a, b)), lambda r, g: (nt(r[1], g), nn(r[0], g)))
    return nn, nt, tn


_mm, _mm_nt, _mm_tn = _make_mm(1, False)
_mm3, _, _ = _make_mm(3, False)
_bmm, _bmm_nt, _bmm_tn = _make_mm(1, True)
_bmm3, _bmm3_nt, _bmm3_tn = _make_mm(3, True)


@jax.custom_vjp
def _known_inverse(a, inv):
    return inv


_known_inverse.defvjp(lambda a, inv: (inv, inv),
                      lambda inv, g: (-_bmm3_tn(inv, _bmm3_nt(g, inv)), jnp.zeros_like(inv)))


def _tri_ones_dot(x, lower):
    n = x.shape[0]
    r = lax.broadcasted_iota(jnp.int32, (n, n), 0)
    c = lax.broadcasted_iota(jnp.int32, (n, n), 1)
    t = ((r >= c) if lower else (r <= c)).astype(BF16)
    hi, lo = _split_bf16(x)
    return jnp.dot(t, hi, preferred_element_type=F32) + jnp.dot(t, lo, preferred_element_type=F32)


@jax.custom_vjp
def _cumsum_rows(x):
    return _tri_ones_dot(x, True)


_cumsum_rows.defvjp(lambda x: (_tri_ones_dot(x, True), None), lambda _, g: (_tri_ones_dot(g, False),))


def _row_tile(n_rows, cap):
    best = LANE
    for t in range(LANE, cap + 1, LANE):
        if n_rows % t == 0:
            best = t
    return best


def _real_rows(tile_index, tm):
    row = tile_index * tm + lax.broadcasted_iota(jnp.int32, (tm, 1), 0)
    return (row >= PAD).astype(F32)


def _full(shape):
    return pl.BlockSpec(shape, lambda *_: (0,) * len(shape))


def _resident(shape):
    return pl.BlockSpec(shape, lambda *_: (0,) * len(shape), pipeline_mode=pl.Buffered(1))


def _resident_w(wmat, widx):
    if wmat.ndim == 2:
        return _resident(wmat.shape)
    return pl.BlockSpec((None,) + wmat.shape[1:], lambda *_: (widx, 0, 0), pipeline_mode=pl.Buffered(1))


def rms_mm(h, w, wmat_t, *, swiglu, name, widx=None):
    tp, d = h.shape
    n = wmat_t.shape[-2]
    tm = _row_tile(tp, 384)
    half = n // 2
    wmat = wmat_t

    def body(h_ref, w_ref, wm_ref, hn_ref, *outs):
        hn = _rms(h_ref[...], w_ref[...]).astype(BF16)
        hn_ref[...] = hn
        p = lax.dot_general(hn, wm_ref[...], (((1,), (1,)), ((), ())), preferred_element_type=F32)
        if swiglu:
            g, u = p[:, :half], p[:, half:]
            outs[0][...] = g.astype(BF16)
            outs[1][...] = u.astype(BF16)
            outs[2][...] = (_silu(g) * u).astype(BF16)
        else:
            outs[0][...] = p

    row = lambda width: pl.BlockSpec((tm, width), lambda i: (i, 0))
    if swiglu:
        out_shape = (jax.ShapeDtypeStruct((tp, d), BF16),) + (jax.ShapeDtypeStruct((tp, half), BF16),) * 3
        out_specs = (row(d), row(half), row(half), row(half))
    else:
        out_shape = (jax.ShapeDtypeStruct((tp, d), BF16), jax.ShapeDtypeStruct((tp, n), F32))
        out_specs = (row(d), row(n))
    return pl.pallas_call(
        body, name=name, grid=(tp // tm,),
        in_specs=[row(d), _full((1, d)), _resident_w(wmat, widx)],
        out_specs=out_specs, out_shape=out_shape,
    )(h, w, wmat)


def mm_rms_res(acts, wmat, h, w, *, scale, name, widx=None):
    tp, d = h.shape
    tm = _row_tile(tp, 384)
    widths = [a.shape[1] for a in acts]
    offs = [sum(widths[:i]) for i in range(len(acts))]
    na = len(acts)

    def body(*refs):
        a_refs = refs[:na]
        wm_ref, h_ref, w_ref, f_ref, ho_ref = refs[na:]
        f = None
        for a_ref, off, width in zip(a_refs, offs, widths):
            part = jnp.dot(a_ref[...].astype(BF16), wm_ref[off:off + width, :], preferred_element_type=F32)
            f = part if f is None else f + part
        f_ref[...] = f
        ho_ref[...] = h_ref[...] + scale * _rms(f, w_ref[...])

    row = lambda width: pl.BlockSpec((tm, width), lambda i: (i, 0))
    return pl.pallas_call(
        body, name=name, grid=(tp // tm,),
        in_specs=[row(wd) for wd in widths] + [_resident_w(wmat, widx), row(d), _full((1, d))],
        out_specs=(row(d), row(d)),
        out_shape=(jax.ShapeDtypeStruct((tp, d), F32), jax.ShapeDtypeStruct((tp, d), F32)),
    )(*acts, wmat, h, w)


def mm_rms_res_bwd(dho, f, w, wmat, gu, *, scale, name, widx=None):
    tp, d = f.shape
    k = wmat.shape[-2]
    tm = _row_tile(tp, 384)
    swiglu = gu is not None

    def body(*refs):
        if swiglu:
            dho_ref, f_ref, w_ref, wm_ref, g_ref, u_ref, df_ref, dw_ref, dgu_ref = refs
        else:
            dho_ref, f_ref, w_ref, wm_ref, df_ref, dw_ref, da_ref = refs
        i = pl.program_id(0)
        _, vjp = jax.vjp(lambda ff, ww: scale * _rms(ff, ww), f_ref[...], w_ref[...])
        df, dw = vjp(dho_ref[...])
        dfb = (df * _real_rows(i, tm)).astype(BF16)
        df_ref[...] = dfb

        @pl.when(i == 0)
        def _():
            dw_ref[...] = jnp.zeros_like(dw_ref)

        dw_ref[...] += dw
        da = lax.dot_general(dfb, wm_ref[...], (((1,), (1,)), ((), ())), preferred_element_type=F32)
        if swiglu:
            g, u, dab = g_ref[...], u_ref[...], da.astype(BF16)
            s = _sigmoid(g)
            dgu_ref[:, :k] = dab * u * s * (1.0 + g * (1.0 - s))
            dgu_ref[:, k:] = dab * g * s
        else:
            da_ref[...] = da

    row = lambda width: pl.BlockSpec((tm, width), lambda i: (i, 0))
    in_specs = [row(d), row(d), _full((1, d)), _resident_w(wmat, widx)]
    args = [dho, f, w, wmat]
    out_shape = [jax.ShapeDtypeStruct((tp, d), BF16), jax.ShapeDtypeStruct((1, d), F32)]
    out_specs = [row(d), _full((1, d))]
    if swiglu:
        in_specs += [row(k), row(k)]
        args += list(gu)
        out_shape += [jax.ShapeDtypeStruct((tp, 2 * k), BF16)]
        out_specs += [row(2 * k)]
    else:
        out_shape += [jax.ShapeDtypeStruct((tp, k), F32)]
        out_specs += [row(k)]
    return pl.pallas_call(body, name=name, grid=(tp // tm,), in_specs=in_specs, out_specs=tuple(out_specs),
                          out_shape=tuple(out_shape))(*args)


def rms_mm_bwd(dps, wmat, h, w, dho, *, name, widx=None):
    tp, d = h.shape
    tm = _row_tile(tp, 384)
    widths = [p.shape[1] for p in dps]
    offs = [sum(widths[:i]) for i in range(len(dps))]
    ndp = len(dps)

    def body(*refs):
        dp_refs = refs[:ndp]
        wm_ref, h_ref, w_ref, dho_ref, dh_ref, dw_ref = refs[ndp:]
        i = pl.program_id(0)
        dhn = None
        for dp_ref, off, width in zip(dp_refs, offs, widths):
            part = jnp.dot(dp_ref[...].astype(BF16), wm_ref[off:off + width, :], preferred_element_type=F32)
            dhn = part if dhn is None else dhn + part
        _, vjp = jax.vjp(_rms, h_ref[...], w_ref[...])
        dx, dw = vjp(dhn)
        dh_ref[...] = (dho_ref[...] + dx) * _real_rows(i, tm)

        @pl.when(i == 0)
        def _():
            dw_ref[...] = jnp.zeros_like(dw_ref)

        dw_ref[...] += dw

    row = lambda width: pl.BlockSpec((tm, width), lambda i: (i, 0))
    return pl.pallas_call(
        body, name=name, grid=(tp // tm,),
        in_specs=[row(wd) for wd in widths] + [_resident_w(wmat, widx), row(d), _full((1, d)), row(d)],
        out_specs=(row(d), _full((1, d))),
        out_shape=(jax.ShapeDtypeStruct((tp, d), F32), jax.ShapeDtypeStruct((1, d), F32)),
    )(*dps, wmat, h, w, dho)


def mm_tn(a, b, *, name, out_dtype=F32, after=None):
    t, m = a.shape
    n = b.shape[1]
    bm = _row_tile(m, 1408)
    ties = [] if after is None else [after]

    def body(a_ref, b_ref, *rest):
        rest[-1][...] = lax.dot_general(a_ref[...].astype(BF16), b_ref[...].astype(BF16), (((0,), (0,)), ((), ())),
                                        preferred_element_type=F32).astype(out_dtype)

    return pl.pallas_call(
        body, name=name, grid=(m // bm,),
        in_specs=[pl.BlockSpec((t, bm), lambda i: (0, i)), _resident((t, n))] + [pl.BlockSpec(memory_space=pl.ANY)] * len(ties),
        out_specs=pl.BlockSpec((bm, n), lambda i: (i, 0)),
        out_shape=jax.ShapeDtypeStruct((m, n), out_dtype),
    )(a, b, *ties)


def loss_and_grad(h, target, *, name):
    tp, d = h.shape
    tm = SWA_BLOCK

    def body(h_ref, t_ref, dh_ref, loss_ref):
        i = pl.program_id(0)

        @pl.when(i == 0)
        def _():
            loss_ref[...] = jnp.zeros_like(loss_ref)
            dh_ref[...] = jnp.zeros_like(dh_ref)

        @pl.when(i > 0)
        def _():
            err = h_ref[...] - t_ref[...]
            dh_ref[...] = err * (1.0 / d)
            loss_ref[...] += 0.5 * jnp.sum(jnp.sum(err * err, axis=1, keepdims=True), axis=0, keepdims=True) * (1.0 / d)

    return pl.pallas_call(
        body, name=name, grid=(tp // tm,),
        in_specs=[pl.BlockSpec((tm, d), lambda i: (i, 0)), pl.BlockSpec((tm, d), lambda i: (jnp.maximum(i - 1, 0), 0))],
        out_specs=(pl.BlockSpec((tm, d), lambda i: (i, 0)), _full((1, 1))),
        out_shape=(jax.ShapeDtypeStruct((tp, d), F32), jax.ShapeDtypeStruct((1, 1), F32)),
    )(h, target)


def _t5_bucket_np(rel):
    n = np.maximum(rel, 0)
    max_exact = REL_BUCKETS // 2
    n_f = np.maximum(n, 1).astype(np.float32)
    large = max_exact + (np.log(n_f / np.float32(max_exact)) / np.float32(math.log(REL_MAX_DIST / max_exact))
                         * np.float32(REL_BUCKETS - max_exact)).astype(np.int32)
    large = np.minimum(large, REL_BUCKETS - 1)
    return np.where(n < max_exact, n, large).astype(np.int32)


def _swa_positions_np(n):
    i = np.arange(SWA_BLOCK)[:, None]
    j = np.arange(3 * SWA_BLOCK)[None, :]
    pos_q = n * SWA_BLOCK + i - PAD
    pos_k = np.where(j < SWA_BLOCK, j - PAD, (n - 1) * SWA_BLOCK + (j - SWA_BLOCK) - PAD)
    return pos_q, pos_k


def _swa_buckets():
    out = []
    for n in range(3):
        pos_q, pos_k = _swa_positions_np(n)
        out.append(_t5_bucket_np(pos_q - pos_k))
    return jnp.asarray(np.stack(out))


def swa_bias(table, buckets, *, name):
    nc, nq, nk = buckets.shape

    def body(tab_ref, bkt_ref, out_ref):
        for c in range(nc):
            bkt = bkt_ref[c]
            for h in range(SWA_Q_HEADS):
                acc = jnp.zeros((nq, nk), F32)
                for b in range(REL_BUCKETS):
                    acc = jnp.where(bkt == b, tab_ref[b, h], acc)
                out_ref[c, h] = acc

    return pl.pallas_call(
        body, name=name,
        in_specs=[pl.BlockSpec(memory_space=pltpu.SMEM), pl.BlockSpec(memory_space=pltpu.VMEM)],
        out_specs=pl.BlockSpec(memory_space=pltpu.VMEM),
        out_shape=jax.ShapeDtypeStruct((nc, SWA_Q_HEADS, nq, nk), F32),
    )(table, buckets)


def swa_bias_bwd(dbias, buckets, *, name):
    nc = buckets.shape[0]

    def body(db_ref, bkt_ref, out_ref):
        lane = lax.broadcasted_iota(jnp.int32, (1, LANE), 1)
        for b in range(REL_BUCKETS):
            row = jnp.zeros((1, LANE), F32)
            for c in range(nc):
                hit = bkt_ref[c] == b
                for h in range(SWA_Q_HEADS):
                    part = jnp.where(hit, db_ref[c, h], 0.0)
                    tot = jnp.sum(jnp.sum(part, axis=1, keepdims=True), axis=0, keepdims=True)
                    row = row + jnp.where(lane == h, tot, 0.0)
            out_ref[b:b + 1, :] = row

    return pl.pallas_call(
        body, name=name,
        in_specs=[pl.BlockSpec(memory_space=pltpu.VMEM), pl.BlockSpec(memory_space=pltpu.VMEM)],
        out_specs=pl.BlockSpec(memory_space=pltpu.VMEM),
        out_shape=jax.ShapeDtypeStruct((REL_BUCKETS, LANE), F32),
    )(dbias, buckets)


def _swa_block(q, kvm, kvp, kvc, bias, sinks, n, batched):
    blk = SWA_BLOCK
    i = lax.broadcasted_iota(jnp.int32, (blk, 3 * blk), 0)
    j = lax.broadcasted_iota(jnp.int32, (blk, 3 * blk), 1)
    pos_q = n * blk + i - PAD
    is_meta = j < blk
    pos_k = jnp.where(is_meta, j - PAD, (n - 1) * blk + (j - blk) - PAD)
    rel = pos_q - pos_k
    valid = ((is_meta & (pos_k >= 0) & (pos_k < N_META) & (rel >= 0))
             | (jnp.logical_not(is_meta) & (pos_k >= N_META) & (rel >= 0) & (rel < SWA_WINDOW)))
    valid_f = valid.astype(F32)
    kv =jnp.concatenate([kvm, kvp, kvc], axis=0)
    lane = lax.broadcasted_iota(jnp.int32, (1, LANE), 1)
    halves = ((lane < SWA_HEAD_DIM).astype(F32), (lane >= SWA_HEAD_DIM).astype(F32))
    nh, group = SWA_Q_HEADS, SWA_Q_HEADS // 2
    q_of = lambda h: q[:, (h // 2) * LANE:(h // 2 + 1) * LANE] * halves[h % 2]
    k_of = lambda h: kv[:, (h // group) * LANE:(h // group + 1) * LANE]
    v_of = lambda h: kv[:, (2 + h // group) * LANE:(3 + h // group) * LANE]
    sink_of = lambda h: jnp.sum(jnp.where(lane == h, sinks, 0.0), axis=1, keepdims=True)

    scale = SWA_HEAD_DIM ** -0.5

    def attend(logits, sink, pv):
        if batched:
            s = logits * valid_f + (valid_f - 1.0) * (-NEG_INF)
        else:
            s = jnp.where(valid, logits, NEG_INF)
        m =lax.stop_gradient(jnp.maximum(jnp.max(s, axis=-1, keepdims=True), sink))
        e = jnp.exp(s - m)
        return pv(e / (jnp.sum(e, axis=-1, keepdims=True) + jnp.exp(sink - m)))

    if batched:
        heads = range(nh)
        vh = _stack([v_of(h) for h in heads])
        qk = _bmm_nt(_stack([q_of(h) for h in heads]), _stack([k_of(h) for h in heads]))
        o = attend(qk * scale + bias, _stack([sink_of(h) for h in heads]), lambda p: _bmm(p, vh))
        head = lambda h: o[h]
    else:
        head = lambda h: attend(_mm_nt(q_of(h), k_of(h)) * scale + bias[h], sink_of(h), lambda p: _mm(p, v_of(h)))
    return jnp.concatenate([head(2 * p) * halves[0] + head(2 * p + 1) * halves[1] for p in range(nh // 2)], axis=1)


def _swa_in_specs(nb, rev):
    blk = SWA_BLOCK
    step = (lambda i: nb - 1 - i) if rev else (lambda i: i)
    return [
        pl.BlockSpec((blk, 4 * LANE), lambda i: (step(i), 0)),
        pl.BlockSpec((blk, 4 * LANE), lambda i: (0, 1)),
        pl.BlockSpec((blk, 4 * LANE), lambda i: (jnp.maximum(step(i) - 1, 0), 1)),
        pl.BlockSpec((blk, 4 * LANE), lambda i: (step(i), 1)),
        pl.BlockSpec((1, SWA_Q_HEADS, blk, 3 * blk), lambda i: (jnp.minimum(step(i), 2), 0, 0, 0)),
        _full((1, LANE)),
    ]


def swa_fwd(proj, bias, sinks, *, name):
    tp = proj.shape[0]
    nb = tp // SWA_BLOCK

    def body(q_ref, kvm_ref, kvp_ref, kvc_ref, bias_ref, sinks_ref, o_ref):
        n = pl.program_id(0)
        o_ref[...] = _swa_block(q_ref[...], kvm_ref[...], kvp_ref[...], kvc_ref[...], bias_ref[0], sinks_ref[...], n, True)

    return pl.pallas_call(
        body, name=name, grid=(nb,),
        in_specs=_swa_in_specs(nb, False),
        out_specs=pl.BlockSpec((SWA_BLOCK, 4 * LANE), lambda i: (i, 0)),
        out_shape=jax.ShapeDtypeStruct((tp, 4 * LANE), F32),
    )(proj, proj, proj, proj, bias, sinks)


def swa_bwd(proj, bias, sinks, do, *, name):
    tp = proj.shape[0]
    nb = tp // SWA_BLOCK
    blk = SWA_BLOCK

    def body(q_ref, kvm_ref, kvp_ref, kvc_ref, bias_ref, sinks_ref, do_ref, dqkv_ref, dbias_ref, dsinks_ref,
             carry, meta_acc):
        i = pl.program_id(0)
        n = nb - 1 - i

        @pl.when(i == 0)
        def _():
            carry[...] = jnp.zeros_like(carry)
            meta_acc[...] = jnp.zeros_like(meta_acc)
            dsinks_ref[...] = jnp.zeros_like(dsinks_ref)

        fn = lambda q, kvm, kvp, kvc, b, s: _swa_block(q, kvm, kvp, kvc, b, s, n, False)
        _, vjp = jax.vjp(fn, q_ref[...], kvm_ref[...], kvp_ref[...], kvc_ref[...], bias_ref[0], sinks_ref[...])
        dq, dkvm, dkvp, dkvc, dbias, dsinks = vjp(do_ref[...])
        dqkv_ref[:, :4 * LANE] = dq.astype(BF16)
        meta_acc[...] += dkvm
        dqkv_ref[:, 4 * LANE:] = (dkvc + carry[...] + jnp.where(n == 0, meta_acc[...], 0.0)).astype(BF16)
        carry[...] = dkvp
        first_visit = (n == nb - 1) | (n < 2)

        @pl.when(first_visit)
        def _():
            dbias_ref[0] = dbias

        @pl.when(jnp.logical_not(first_visit))
        def _():
            dbias_ref[0] += dbias

        dsinks_ref[...] += dsinks

    rev = lambda i: nb - 1 - i
    return pl.pallas_call(
        body, name=name, grid=(nb,),
        in_specs=_swa_in_specs(nb, True) + [pl.BlockSpec((blk, 4 * LANE), lambda i: (rev(i), 0))],
        out_specs=(pl.BlockSpec((blk, 8 * LANE), lambda i: (rev(i), 0)),
                   pl.BlockSpec((1, SWA_Q_HEADS, blk, 3 * blk), lambda i: (jnp.minimum(rev(i), 2), 0, 0, 0)),
                   _full((1, LANE))),
        out_shape=(jax.ShapeDtypeStruct((tp, PROJ_DIM), BF16),
                   jax.ShapeDtypeStruct((3, SWA_Q_HEADS, blk, 3 * blk), F32), jax.ShapeDtypeStruct((1, LANE), F32)),
        scratch_shapes=[pltpu.VMEM((blk, 4 * LANE), F32), pltpu.VMEM((blk, 4 * LANE), F32)],
    )(proj, proj, proj, proj, bias, sinks, do)


CONV_COL0 = 2
HALO = 8


def conv_fwd(proj, conv_w, *, name):
    tp = proj.shape[0]
    tm = _row_tile(tp, 384)
    cw = 4 * LANE
    ncol = conv_w.shape[1] // cw

    def body(x_ref, halo_ref, w_ref, y_ref, buf):
        i = pl.program_id(1)
        buf[0:HALO, :] = jnp.where(i > 0, halo_ref[...], 0.0)
        buf[HALO:, :] = x_ref[...]
        acc = None
        for j in range(DN_CONV):
            term = w_ref[j:j + 1, :] * buf[pl.ds(HALO - (DN_CONV - 1) + j, tm), :]
            acc = term if acc is None else acc + term
        y_ref[...] = acc

    return pl.pallas_call(
        body, name=name, grid=(ncol, tp // tm),
        in_specs=[pl.BlockSpec((tm, cw), lambda c, i: (i, CONV_COL0 + c)),
                  pl.BlockSpec((HALO, cw), lambda c, i: (jnp.maximum(i * (tm // HALO) - 1, 0), CONV_COL0 + c)),
                  pl.BlockSpec((DN_CONV, cw), lambda c, i: (0, c))],
        out_specs=pl.BlockSpec((tm, cw), lambda c, i: (i, c)),
        out_shape=jax.ShapeDtypeStruct((tp, ncol * cw), F32),
        scratch_shapes=[pltpu.VMEM((tm + HALO, cw), F32)],
    )(proj, proj, conv_w)


def conv_bwd(proj, conv_w, dy, dproj, *, name):
    tp = proj.shape[0]
    tm = _row_tile(tp, 384)
    cw = 4 * LANE
    ncol = conv_w.shape[1] // cw
    nt = tp // tm

    def body(x_ref, xhalo_ref, w_ref, dy_ref, dyhalo_ref, _, dx_ref, dw_ref, xbuf, dbuf):
        i = pl.program_id(1)
        xbuf[0:HALO, :] = jnp.where(i > 0, xhalo_ref[...], 0.0)
        xbuf[HALO:, :] = x_ref[...]
        dbuf[0:tm, :] = dy_ref[...]
        dbuf[tm:, :] = jnp.where(i < nt - 1, dyhalo_ref[...], 0.0)
        dy_t = dy_ref[...]
        acc = None
        rows = []
        for j in range(DN_CONV):
            term = w_ref[j:j + 1, :] * dbuf[pl.ds(DN_CONV - 1 - j, tm), :]
            acc = term if acc is None else acc + term
            rows.append(jnp.sum(dy_t * xbuf[pl.ds(HALO - (DN_CONV - 1) + j, tm), :], axis=0, keepdims=True))
        dx_ref[...] = acc.astype(BF16)

        @pl.when(i == 0)
        def _():
            dw_ref[...] = jnp.zeros_like(dw_ref)

        for j in range(DN_CONV):
            dw_ref[j:j + 1, :] += rows[j]

    return pl.pallas_call(
        body, name=name, grid=(ncol, nt),
        in_specs=[pl.BlockSpec((tm, cw), lambda c, i: (i, CONV_COL0 + c)),
                  pl.BlockSpec((HALO, cw), lambda c, i: (jnp.maximum(i * (tm // HALO) - 1, 0), CONV_COL0 + c)),
                  pl.BlockSpec((DN_CONV, cw), lambda c, i: (0, c)),
                  pl.BlockSpec((tm, cw), lambda c, i: (i, c)),
                  pl.BlockSpec((HALO, cw), lambda c, i: (jnp.minimum((i + 1) * (tm // HALO), tp // HALO - 1), c)),
                  pl.BlockSpec(memory_space=pl.ANY)],
        out_specs=(pl.BlockSpec((tm, cw), lambda c, i: (i, CONV_COL0 + c)), pl.BlockSpec((DN_CONV, cw), lambda c, i: (0, c))),
        out_shape=(jax.ShapeDtypeStruct(dproj.shape, dproj.dtype), jax.ShapeDtypeStruct((DN_CONV, ncol * cw), F32)),
        scratch_shapes=[pltpu.VMEM((tm + HALO, cw), F32), pltpu.VMEM((tm + HALO, cw), F32)],
        input_output_aliases={5: 0},
    )(proj, proj, conv_w, dy, dy, dproj)


def _stack(parts):
    return jnp.concatenate([p[None] for p in parts], axis=0)


def _chunk_masks():
    r = lax.broadcasted_iota(jnp.int32, (CHUNK, CHUNK), 0)
    c = lax.broadcasted_iota(jnp.int32, (CHUNK, CHUNK), 1)
    return (r >= c).astype(F32), (r > c).astype(F32), (r == c).astype(F32)


def _dn_chunk(y, z, small, s, a_log, dt_bias, norm_w, rows, known_inv=None):
    tri_incl, tri_strict, eye = _chunk_masks()
    lane = lax.broadcasted_iota(jnp.int32, (1, LANE), 1)
    dk = DN_HEAD_DIM
    nh = DN_HEADS
    heads = lambda t, first: _stack([t[:, (first + h) * dk:(first + h + 1) * dk] for h in range(nh)])
    pick = lambda t, l: jnp.sum(jnp.where(lane == l, t, 0.0), axis=1, keepdims=True)
    q = _l2n(_silu(heads(y, 0))) * dk ** -0.5
    k = _l2n(_silu(heads(y, nh)))
    v = _silu(heads(y, 2 * nh))
    g_all = jnp.where(lane < nh, -jnp.exp(a_log) * _softplus(small + dt_bias), 0.0) * rows
    beta_all = _sigmoid(small)
    gc_all = _cumsum_rows(g_all)
    g_sum = jnp.sum(g_all, axis=0, keepdims=True)
    gc = _stack([pick(gc_all, h) for h in range(nh)])
    beta = _stack([pick(beta_all, nh + h) for h in range(nh)])
    g_last = _stack([pick(g_sum, h) for h in range(nh)])
    gc_row = jnp.sum(eye * gc, axis=1, keepdims=True)
    gamma = jnp.exp((gc - gc_row) * tri_incl) * tri_incl
    k_beta = k * beta
    v_beta = v * beta
    a = _bmm_nt(k_beta, k) * gamma * tri_strict
    if known_inv is None:
        inv = eye - a
        power = a
        for _ in range(5):
            power = _bmm3(power, power)
            inv = inv + _bmm3(inv, power)
    else:
        inv = _known_inverse(a, known_inv)
    e_gc = jnp.exp(gc)
    uw = _bmm3(inv, jnp.concatenate([v_beta, k_beta * e_gc], axis=2))
    u, w = uw[:, :, :dk], uw[:, :, dk:]
    attn = _bmm_nt(q, k) * gamma
    q_dec = q * e_gc
    k_dec = k * jnp.exp(g_last - gc)
    v_new = u - _bmm(w, s)
    o = _bmm(q_dec, s) + _bmm(attn, v_new)
    s_new = s * jnp.exp(g_last) + _bmm_tn(k_dec, v_new)
    out = _rms(o, norm_w) * _silu(heads(z, 0))
    return jnp.concatenate([out[h] for h in range(nh)], axis=1), s_new, inv


Z_COL = 5
SMALL_COL = 24


def _chunk_rows(n):
    row = n * CHUNK + lax.broadcasted_iota(jnp.int32, (CHUNK, 1), 0)
    return (row >= PAD).astype(F32)


def dn_fwd(y, proj, a_log, dt_bias, norm_w, *, name):
    tp = y.shape[0]
    nc = tp // CHUNK
    dk = DN_HEAD_DIM
    per = CHUNKS_PER_STEP
    rows = per * CHUNK

    def body(y_ref, z_ref, small_ref, al_ref, dt_ref, nw_ref, o_ref, ssave_ref, isave_ref, state):
        n = pl.program_id(0)

        @pl.when(n == 0)
        def _():
            state[...] = jnp.zeros_like(state)

        s = state[...]
        for c in range(per):
            at = pl.ds(c * CHUNK, CHUNK)
            ssave_ref[c] = s
            out, s, inv = _dn_chunk(y_ref[at, :], z_ref[at, :], small_ref[at, :], s, al_ref[...], dt_ref[...],
                                    nw_ref[...], _chunk_rows(per * n + c))
            o_ref[at, :] = out
            isave_ref[c] = inv
        state[...] = s

    return pl.pallas_call(
        body, name=name, grid=(nc // per,),
        in_specs=[pl.BlockSpec((rows, y.shape[1]), lambda n: (n, 0)),
                  pl.BlockSpec((rows, 4 * LANE), lambda n: (n, Z_COL)),
                  pl.BlockSpec((rows, LANE), lambda n: (n, SMALL_COL)),
                  _full((1, LANE)), _full((1, LANE)), _full((1, LANE))],
        out_specs=(pl.BlockSpec((rows, 4 * LANE), lambda n: (n, 0)),
                   pl.BlockSpec((per, DN_HEADS, dk, dk), lambda n: (n, 0, 0, 0)),
                   pl.BlockSpec((per, DN_HEADS, CHUNK, CHUNK), lambda n: (n, 0, 0, 0))),
        out_shape=(jax.ShapeDtypeStruct((tp, 4 * LANE), F32), jax.ShapeDtypeStruct((nc, DN_HEADS, dk, dk), F32),
                   jax.ShapeDtypeStruct((nc, DN_HEADS, CHUNK, CHUNK), F32)),
        scratch_shapes=[pltpu.VMEM((DN_HEADS, dk, dk), F32)],
    )(y, proj, proj, a_log, dt_bias, norm_w)


def dn_bwd(y, proj, a_log, dt_bias, norm_w, ssave, isave, do, dproj, *, name):
    tp = y.shape[0]
    nc = tp // CHUNK
    dk = DN_HEAD_DIM
    per = CHUNKS_PER_STEP
    rev = lambda i: nc // per - 1 - i
    zs_width = 5 * LANE

    def body(y_ref, z_ref, small_ref, al_ref, dt_ref, nw_ref, ss_ref, is_ref, do_ref, _,
             dy_ref, dzs_ref, dal_ref, ddt_ref, dnw_ref, dstate):
        i = pl.program_id(0)
        n = nc // per - 1 - i

        @pl.when(i == 0)
        def _():
            dstate[...] = jnp.zeros_like(dstate)
            dal_ref[...] = jnp.zeros_like(dal_ref)
            ddt_ref[...] = jnp.zeros_like(ddt_ref)
            dnw_ref[...] = jnp.zeros_like(dnw_ref)

        ds = dstate[...]
        for c in reversed(range(per)):
            at = pl.ds(c * CHUNK, CHUNK)
            token_rows = _chunk_rows(per * n + c)
            known_inv = is_ref[c]
            fn = lambda *a: _dn_chunk(*a, token_rows, known_inv)[:2]
            _, vjp = jax.vjp(fn, y_ref[at, :], z_ref[at, :], small_ref[at, :], ss_ref[c], al_ref[...], dt_ref[...],
                             nw_ref[...])
            dy, dz, dsmall, ds, dal, ddt, dnw = vjp((do_ref[at, :], ds))
            dy_ref[at, :] = dy
            dzs_ref[at, :4 * LANE] = dz.astype(BF16)
            dzs_ref[at, 4 * LANE:] = dsmall.astype(BF16)
            dal_ref[...] += dal
            ddt_ref[...] += ddt
            dnw_ref[...] += dnw
        dstate[...] = ds

    rows = per * CHUNK
    return pl.pallas_call(
        body, name=name, grid=(nc // per,),
        in_specs=[pl.BlockSpec((rows, y.shape[1]), lambda i: (rev(i), 0)),
                  pl.BlockSpec((rows, 4 * LANE), lambda i: (rev(i), Z_COL)),
                  pl.BlockSpec((rows, LANE), lambda i: (rev(i), SMALL_COL)),
                  _full((1, LANE)), _full((1, LANE)), _full((1, LANE)),
                  pl.BlockSpec((per, DN_HEADS, dk, dk), lambda i: (rev(i), 0, 0, 0)),
                  pl.BlockSpec((per, DN_HEADS, CHUNK, CHUNK), lambda i: (rev(i), 0, 0, 0)),
                  pl.BlockSpec((rows, 4 * LANE), lambda i: (rev(i), 1)),
                  pl.BlockSpec(memory_space=pl.ANY)],
        out_specs=(pl.BlockSpec((rows, y.shape[1]), lambda i: (rev(i), 0)),
                   pl.BlockSpec((rows, zs_width), lambda i: (rev(i), Z_COL * 4 * LANE // zs_width)),
                   _full((1, LANE)), _full((1, LANE)), _full((1, LANE))),
        out_shape=(jax.ShapeDtypeStruct((tp, y.shape[1]), F32), jax.ShapeDtypeStruct(dproj.shape, dproj.dtype),
                   jax.ShapeDtypeStruct((1, LANE), F32), jax.ShapeDtypeStruct((1, LANE), F32),
                   jax.ShapeDtypeStruct((1, LANE), F32)),
        scratch_shapes=[pltpu.VMEM((DN_HEADS, dk, dk), F32)],
        input_output_aliases={9: 1},
    )(y, proj, proj, a_log, dt_bias, norm_w, ssave, isave, do, dproj)


def _gla_chunk(q, k, v, gate, low, s, w_gate_up, b_gate, norm_w, rows):
    tri_incl, _, _ = _chunk_masks()
    dk, dv, nh = GLA_DK, GLA_DV, GLA_HEADS
    heads = lambda t, width: _stack([t[:, h * width:(h + 1) * width] for h in range(nh)])
    logit = _mm3(low, w_gate_up) + b_gate
    glog_all = -_softplus(-logit) * (1.0 / GLA_GATE_NORM) * rows
    glog = heads(glog_all, dk)
    bcum = heads(_cumsum_rows(glog_all), dk)
    qh = heads(q, dk) * dk ** -0.5
    kh = heads(k, dk)
    vh = heads(v, dv)
    q_dec = qh * jnp.exp(bcum)
    attn = _bmm_nt(q_dec, kh * jnp.exp(-bcum)) * tri_incl
    b_last = jnp.sum(glog, axis=1, keepdims=True)
    k_dec = kh * jnp.exp(b_last - bcum)
    r = lax.broadcasted_iota(jnp.int32, (dk, dk), 0)
    c = lax.broadcasted_iota(jnp.int32, (dk, dk), 1)
    b_last_col = jnp.sum((r == c).astype(F32) * b_last, axis=2, keepdims=True)
    o = _bmm(attn, vh) + _bmm(q_dec, s)
    s_new = s * jnp.exp(b_last_col) + _bmm_tn(k_dec, vh)
    out = _rms(o, norm_w) * _silu(heads(gate, dv))
    return jnp.concatenate([out[h] for h in range(nh)], axis=1), s_new


LOW_COL = 24


def _gla_in_specs(step, rows=CHUNK):
    return [pl.BlockSpec((rows, 4 * LANE), lambda i: (step(i), 0)),
            pl.BlockSpec((rows, 4 * LANE), lambda i: (step(i), 1)),
            pl.BlockSpec((rows, 8 * LANE), lambda i: (step(i), 1)),
            pl.BlockSpec((rows, 8 * LANE), lambda i: (step(i), 2)),
            pl.BlockSpec((rows, LANE), lambda i: (step(i), LOW_COL)),
            _full((LANE, 4 * LANE)), _full((1, 4 * LANE)), _full((1, GLA_DV))]


def gla_fwd(proj, w_gate_up, b_gate, norm_w, *, name):
    tp = proj.shape[0]
    nc = tp // CHUNK
    per = CHUNKS_PER_STEP
    rows = per * CHUNK

    def body(q_ref, k_ref, v_ref, g_ref, low_ref, wgu_ref, bg_ref, nw_ref, o_ref, ssave_ref, state):
        n = pl.program_id(0)

        @pl.when(n == 0)
        def _():
            state[...] = jnp.zeros_like(state)

        s = state[...]
        for c in range(per):
            at = pl.ds(c * CHUNK, CHUNK)
            ssave_ref[c] = s
            out, s = _gla_chunk(q_ref[at, :], k_ref[at, :], v_ref[at, :], g_ref[at, :], low_ref[at, :], s, wgu_ref[...],
                                bg_ref[...], nw_ref[...], _chunk_rows(per * n + c))
            o_ref[at, :] = out
        state[...] = s

    return pl.pallas_call(
        body, name=name, grid=(nc // per,),
        in_specs=_gla_in_specs(lambda i: i, rows),
        out_specs=(pl.BlockSpec((rows, 8 * LANE), lambda n: (n, 0)),
                   pl.BlockSpec((per, GLA_HEADS, GLA_DK, GLA_DV), lambda n: (n, 0, 0, 0))),
        out_shape=(jax.ShapeDtypeStruct((tp, 8 * LANE), F32),
                   jax.ShapeDtypeStruct((nc, GLA_HEADS, GLA_DK, GLA_DV), F32)),
        scratch_shapes=[pltpu.VMEM((GLA_HEADS, GLA_DK, GLA_DV), F32)],
    )(proj, proj, proj, proj, proj, w_gate_up, b_gate, norm_w)


def gla_bwd(proj, w_gate_up, b_gate, norm_w, ssave, do, *, name):
    tp = proj.shape[0]
    nc = tp // CHUNK
    per = CHUNKS_PER_STEP
    rev = lambda i: nc // per - 1 - i

    def body(q_ref, k_ref, v_ref, g_ref, low_ref, wgu_ref, bg_ref, nw_ref, ss_ref, do_ref,
             dproj_ref, dwgu_ref, dbg_ref, dnw_ref, dstate):
        i = pl.program_id(0)
        n = nc // per - 1 - i

        @pl.when(i == 0)
        def _():
            dstate[...] = jnp.zeros_like(dstate)
            dwgu_ref[...] = jnp.zeros_like(dwgu_ref)
            dbg_ref[...] = jnp.zeros_like(dbg_ref)
            dnw_ref[...] = jnp.zeros_like(dnw_ref)

        ds = dstate[...]
        for c in reversed(range(per)):
            at = pl.ds(c * CHUNK, CHUNK)
            token_rows = _chunk_rows(per * n + c)
            fn = lambda *a: _gla_chunk(*a, token_rows)
            _, vjp = jax.vjp(fn, q_ref[at, :], k_ref[at, :], v_ref[at, :], g_ref[at, :], low_ref[at, :], ss_ref[c],
                             wgu_ref[...], bg_ref[...], nw_ref[...])
            dq, dk, dv, dg, dlow, ds, dwgu, dbg, dnw = vjp((do_ref[at, :], ds))
            off = 0
            for part in (dq, dk, dv, dg, dlow):
                dproj_ref[at, off:off + part.shape[1]] = part.astype(BF16)
                off += part.shape[1]
            dwgu_ref[...] += dwgu
            dbg_ref[...] += dbg
            dnw_ref[...] += dnw
        dstate[...] = ds

    chunk = lambda width: pl.BlockSpec((per * CHUNK, width), lambda i: (rev(i), 0))
    return pl.pallas_call(
        body, name=name, grid=(nc // per,),
        in_specs=_gla_in_specs(rev, per * CHUNK) + [pl.BlockSpec((per, GLA_HEADS, GLA_DK, GLA_DV), lambda i: (rev(i), 0, 0, 0)),
                                                    chunk(8 * LANE)],
        out_specs=(chunk(PROJ_DIM), _full((LANE, 4 * LANE)), _full((1, 4 * LANE)), _full((1, GLA_DV))),
        out_shape=(jax.ShapeDtypeStruct((tp, PROJ_DIM), BF16), jax.ShapeDtypeStruct((LANE, 4 * LANE), F32),
                   jax.ShapeDtypeStruct((1, 4 * LANE), F32), jax.ShapeDtypeStruct((1, GLA_DV), F32)),
        scratch_shapes=[pltpu.VMEM((GLA_HEADS, GLA_DK, GLA_DV), F32)],
    )(proj, proj, proj, proj, proj, w_gate_up, b_gate, norm_w, ssave, do)


def _even_proj_weight(w_t):
    hd = SWA_HEAD_DIM
    k0, k1 = w_t[512:512 + hd], w_t[512 + hd:640]
    v0, v1 = w_t[640:640 + hd], w_t[640 + hd:768]
    zeros = jnp.zeros((LANE - 2 * DN_HEADS, w_t.shape[1]), w_t.dtype)
    return jnp.concatenate([w_t[:512], k0, k0, k1, k1, v0, v0, v1, v1, w_t[768:2816], w_t[2820:2824], w_t[2816:2820],
                            zeros], axis=0)


def _even_proj_weight_grad(dw):
    hd = SWA_HEAD_DIM
    c = lambda i: dw[512 + i * hd:512 + (i + 1) * hd]
    return jnp.concatenate([dw[:512], c(0) + c(1), c(2) + c(3), c(4) + c(5), c(6) + c(7), dw[1024:3072],
                            dw[3076:3080], dw[3072:3076]], axis=0)


def _ffn_fwd(h, nw_in, nw_out, wts, idx, get_w):
    wts.update(get_w(f"ffn{idx}", h))
    w_gu = wts[f"w_gu{idx}"]
    hn, g, u, a = rms_mm(h, nw_in, w_gu[0], swiglu=True, name=f"ffn_up_{idx}", widx=w_gu[1])
    wts.update(get_w(f"down{idx}", a))
    w_down = wts[f"w_down{idx}"]
    f, h_out = mm_rms_res([a], w_down[0], h, nw_out, scale=0.5, name=f"ffn_down_{idx}", widx=w_down[1])
    return h_out, (h, hn, g, u, a, f)


def _ffn_bwd(dho, saved, nw_in, nw_out, w_gu, w_down, idx, on_grads):
    h, hn, g, u, a, f = saved
    df, dnw_out, dgu = mm_rms_res_bwd(dho, f, nw_out, w_down[0], (g, u), scale=0.5, name=f"ffn_down_bwd_{idx}",
                                      widx=w_down[1])
    g_down = mm_tn(a, df, name=f"ffn_dwd_{idx}", out_dtype=BF16)
    sent = on_grads("down", g_down)
    g_gu = mm_tn(dgu, hn, name=f"ffn_dwgu_{idx}", out_dtype=BF16, after=sent)
    sent = on_grads("gu", g_gu)
    dh, dnw_in = rms_mm_bwd([dgu], w_gu[0], h, nw_in + sent, dho, name=f"ffn_up_bwd_{idx}", widx=w_gu[1])
    return dh, dnw_in, dnw_out


def local_step(x, target, wts, get_w=None, put_g=None):
    seq, d = x.shape
    wts = dict(wts)
    get_w = get_w or (lambda stage, after: {})
    put_g = put_g or (lambda stage, grads: jnp.zeros((1, 1), F32))
    row = lambda v: v.reshape(1, -1)
    lane_row = lambda v: jnp.pad(v.reshape(1, -1), ((0, 0), (0, LANE - v.size)))
    nw = wts["norm_w"]
    h = jnp.concatenate([jnp.zeros((PAD, d), F32), wts["meta_tokens"], x], axis=0)
    buckets = _swa_buckets()
    bias = swa_bias(wts["rel_bias_table"], buckets, name="swa_bias")
    sinks = lane_row(wts["swa_sinks"])
    a_log, dt_bias = lane_row(wts["dn_a_log"]), lane_row(wts["dn_dt_bias"])
    dn_norm_w = row(wts["dn_norm_w"])
    conv_w = wts["even_conv_w"][0]
    w_gate_up = jnp.pad(wts["gla_w_gate_up"][0], ((0, LANE - GLA_GATE_RANK), (0, 0)))
    b_gate, gla_norm_w = row(wts["gla_b_gate"]), row(wts["gla_norm_w"])

    saved = []
    w_in, w_out = [None, None], [None, None]
    for l in range(2):
        h, s_a = _ffn_fwd(h, row(nw[l, 0]), row(nw[l, 1]), wts, 2 * l, get_w)
        if l == 0:
            wts.update(get_w("even", h))
            w_in[0], w_out[0] = _even_proj_weight(wts["even_w_in"]), wts["even_w_out"]
        else:
            wts.update(get_w("odd", h))
            w_in[1] = jnp.pad(wts["odd_w_in"], ((0, PROJ_DIM - wts["odd_w_in"].shape[0]), (0, 0)))
            w_out[1] = wts["odd_w_out"]
        h_mix = h
        hn, proj = rms_mm(h, row(nw[l, 2]), w_in[l], swiglu=False, name=f"mix_in_{l}")
        if l == 0:
            o_a = swa_fwd(proj, bias, sinks, name="swa_fwd")
            y = conv_fwd(proj, conv_w, name="conv_fwd")
            o_b, ssave, isave = dn_fwd(y, proj, a_log, dt_bias, dn_norm_w, name="dn_fwd")
            acts, extra = [o_a, o_b], (y, ssave, isave)
        else:
            o, ssave = gla_fwd(proj, w_gate_up, b_gate, gla_norm_w, name="gla_fwd")
            acts, extra = [o], (ssave,)
        mix, h = mm_rms_res(acts, w_out[l], h, row(nw[l, 3]), scale=1.0, name=f"mix_out_{l}")
        s_m = (h_mix, hn, proj, acts, extra, mix)
        h, s_b = _ffn_fwd(h, row(nw[l, 4]), row(nw[l, 5]), wts, 2 * l + 1, get_w)
        saved.append((s_a, s_m, s_b))

    dh, loss = loss_and_grad(h, target, name="loss")

    grads = {}
    dnw = [[None] * 6 for _ in range(2)]
    def on_grads(i):
        def put(which, g):
            grads[f"g_{which}{i}"] = g
            return put_g(f"{which}{i}", grads)
        return put

    for l in (1, 0):
        s_a, s_m, s_b = saved[l]
        i = 2 * l + 1
        dh, dnw[l][4], dnw[l][5] = _ffn_bwd(dh, s_b, row(nw[l, 4]), row(nw[l, 5]), wts[f"w_gu{i}"], wts[f"w_down{i}"],
                                            i, on_grads(i))
        h_mix, hn, proj, acts, extra, mix = s_m
        dmix, dnw[l][3], do = mm_rms_res_bwd(dh, mix, row(nw[l, 3]), w_out[l], None, scale=1.0, name=f"mix_out_bwd_{l}")
        dw_out = jnp.concatenate([mm_tn(a, dmix, name=f"mix_dwo_{l}_{i}") for i, a in enumerate(acts)], axis=0)
        sent = jnp.zeros((1, 1), F32)
        if l == 0:
            y, ssave, isave = extra
            dproj, dbias, dsinks = swa_bwd(proj, bias, sinks, do, name="swa_bwd")
            dy, dproj, da_log, ddt_bias, ddn_norm_w = dn_bwd(y, proj, a_log, dt_bias, dn_norm_w, ssave, isave, do, dproj,
                                                             name="dn_bwd")
            dproj, dconv_w = conv_bwd(proj, conv_w, dy, dproj, name="conv_bwd")
            grads["rel_bias_table"] = swa_bias_bwd(dbias, buckets, name="swa_bias_bwd")[:, :SWA_Q_HEADS]
            grads["swa_sinks"] = dsinks[:, :SWA_Q_HEADS]
            grads["dn_a_log"] = da_log[:, :DN_HEADS]
            grads["dn_dt_bias"] = ddt_bias[:, :DN_HEADS]
            grads["dn_norm_w"] = ddn_norm_w
            grads["even_conv_w"] = dconv_w[None]
            grads["even_w_out"] = dw_out
        else:
            (ssave,) = extra
            dproj, dwgu, dbg, dgnw = gla_bwd(proj, w_gate_up, b_gate, gla_norm_w, ssave, do, name="gla_bwd")
            grads["gla_w_gate_up"] = dwgu[None, :GLA_GATE_RANK]
            grads["gla_b_gate"] = dbg
            grads["gla_norm_w"] = dgnw
            grads["odd_w_out"] = dw_out
        dw_in = mm_tn(dproj, hn, name=f"mix_dwi_{l}")
        if l == 0:
            grads["even_w_in"] = _even_proj_weight_grad(dw_in)
            sent = put_g("even", grads)
        else:
            grads["odd_w_in"] = dw_in[:wts["odd_w_in"].shape[0]]
        dh, dnw[l][2] = rms_mm_bwd([dproj], w_in[l], h_mix, row(nw[l, 2]) + sent, dh, name=f"mix_in_bwd_{l}")
        i = 2 * l
        dh, dnw[l][0], dnw[l][1] = _ffn_bwd(dh, s_a, row(nw[l, 0]), row(nw[l, 1]), wts[f"w_gu{i}"], wts[f"w_down{i}"],
                                            i, on_grads(i))

    grads["norm_w"] = jnp.stack([jnp.concatenate(r, axis=0) for r in dnw])
    grads["meta_tokens"] = dh[PAD:PAD + N_META]
    return loss[0, 0], dh[PAD + N_META:], grads


def _peer(k):
    x, y, c = (lax.axis_index(a) for a in AXES)
    flip = lambda v, bit: 1 - v if bit else v
    return (flip(x, k & 4), flip(y, k & 2), flip(c, k & 1))


def _my_index():
    x, y, c = (lax.axis_index(a) for a in AXES)
    return 4 * x + 2 * y + c


_HBM = pl.BlockSpec(memory_space=pltpu.HBM)
_SEM = pl.BlockSpec(memory_space=pltpu.SEMAPHORE)
_EFFECT = pltpu.SideEffectType.DATAFLOW_SIDE_EFFECTING


def _remote_copies(items, src_refs, land_refs, send_sems, recv_sems):
    me = _my_index()
    copies = []
    for k in range(1, N_DEV):
        px, py, pc = _peer(k)
        pj = 4 * px + 2 * py + pc
        for a, (sn, send, ln, land, _) in enumerate(items):
            sem = (k - 1) * len(items) + a
            copies.append(pltpu.make_async_remote_copy(
                src_ref=send(src_refs[sn], pj), dst_ref=land(land_refs[ln], me), send_sem=send_sems.at[sem],
                recv_sem=recv_sems.at[sem], device_id=(px, py, pc), device_id_type=MESH))
    return copies


def exchange(srcs, lands, items, after, *, name):
    sn, ln = list(srcs), list(lands)

    def body(*refs):
        src_refs = dict(zip(sn, refs[:len(sn)]))
        land_refs = dict(zip(ln, refs[len(sn) + len(ln) + 1:len(sn) + 2 * len(ln) + 1]))
        send_sems, recv_sems = refs[len(sn) + 2 * len(ln) + 1:]
        copies = _remote_copies(items, src_refs, land_refs, send_sems, recv_sems)
        for cp in copies:
            cp.start()
        for cp in copies:
            cp.wait_recv()
        for cp in copies:
            cp.wait_send()

    n_remote = (N_DEV - 1) * len(items)
    outs = pl.pallas_call(
        body, name=name,
        in_specs=[pl.BlockSpec(memory_space=pl.ANY)] * (len(sn) + len(ln) + 1),
        out_specs=tuple(pl.BlockSpec(memory_space=pl.ANY) for _ in ln),
        out_shape=tuple(jax.ShapeDtypeStruct(lands[n].shape, lands[n].dtype) for n in ln),
        input_output_aliases={len(sn) + i: i for i in range(len(ln))},
        scratch_shapes=[pltpu.SemaphoreType.DMA((n_remote,)), pltpu.SemaphoreType.DMA((n_remote,))],
    )(*[srcs[n] for n in sn], *[lands[n] for n in ln], after)
    return dict(zip(ln, outs))


def start_copies(srcs, lands, items, *, name):
    sn, ln = list(srcs), list(lands)
    n_remote = (N_DEV - 1) * len(items)

    def body(*refs):
        src_refs = dict(zip(sn, refs[:len(sn)]))
        land_refs = dict(zip(ln, refs[len(sn):len(sn) + len(ln)]))
        send_sems, recv_sems = refs[len(sn) + len(ln):len(sn) + len(ln) + 2]
        token = refs[-1]
        for cp in _remote_copies(items, src_refs, land_refs, send_sems, recv_sems):
            cp.start()
        token[...] = jnp.zeros_like(token)

    hbm = lambda a: pltpu.with_memory_space_constraint(a, pltpu.HBM)
    outs = pl.pallas_call(
        body, name=name,
        in_specs=[_HBM] * (len(sn) + len(ln)),
        out_specs=(_SEM, _SEM) + (_HBM,) * len(ln) + (pl.BlockSpec(memory_space=pltpu.VMEM),),
        out_shape=(pltpu.SemaphoreType.DMA((n_remote,)), pltpu.SemaphoreType.DMA((n_remote,)))
        + tuple(pltpu.HBM(lands[n].shape, lands[n].dtype) for n in ln) + (jax.ShapeDtypeStruct((8, LANE), F32),),
        input_output_aliases={len(sn) + i: 2 + i for i in range(len(ln))},
        compiler_params=pltpu.CompilerParams(has_side_effects=_EFFECT),
    )(*[hbm(srcs[n]) for n in sn], *[hbm(lands[n]) for n in ln])
    return (outs[0], outs[1]), dict(zip(ln, outs[2:2 + len(ln)])), outs[-1][0:1, 0:1]


def wait_copies(sems, srcs, lands, items, after, *, name):
    sn, ln = list(srcs), list(lands)

    def body(*refs):
        src_refs = dict(zip(sn, refs[:len(sn)]))
        land_refs = dict(zip(ln, refs[len(sn):len(sn) + len(ln)]))
        send_sems, recv_sems = refs[len(sn) + len(ln):len(sn) + len(ln) + 2]
        copies = _remote_copies(items, src_refs, land_refs, send_sems, recv_sems)
        for cp in copies:
            cp.wait_send()
        for cp in copies:
            cp.wait_recv()

    outs = pl.pallas_call(
        body, name=name,
        in_specs=[_HBM] * (len(sn) + len(ln)) + [_SEM, _SEM, pl.BlockSpec(memory_space=pl.ANY)],
        out_specs=(_HBM,) * len(ln),
        out_shape=tuple(pltpu.HBM(lands[n].shape, lands[n].dtype) for n in ln),
        input_output_aliases={len(sn) + i: i for i in range(len(ln))},
        compiler_params=pltpu.CompilerParams(has_side_effects=_EFFECT),
    )(*[srcs[n] for n in sn], *[lands[n] for n in ln], sems[0], sems[1], after)
    return dict(zip(ln, outs))


def _block(index, size, base=0):
    return pl.ds(pl.multiple_of(base + index * size, ROW_TILE), size)


def _adam_tile(rows):
    for t in (256, 176, 128):
        if rows % t == 0:
            return t
    return rows


def sum_adamw(recvs, w, m, v, *, name, first_slab=0, into=None):
    _, r, c = w.shape
    b = len(recvs)
    rp = recvs[0].shape[1]
    whole = r % ROW_TILE != 0
    tr = r if whole else _adam_tile(r)
    c1 = 1.0 / (1.0 - ADAM_B1 ** ADAM_STEP)
    c2 = 1.0 / (1.0 - ADAM_B2 ** ADAM_STEP)

    def body(*refs):
        recv_refs = refs[:b]
        w_ref, m_ref, v_ref = refs[b:b + 3]
        g_ref, d_ref, nm_ref, nv_ref = refs[b + 3 + (0 if into is None else 4):][:4]
        for slab, recv_ref in enumerate(recv_refs):
            @pl.when(pl.program_id(0) == slab)
            def _():
                g = recv_ref[0].astype(F32)
                for i in range(1, N_DEV):
                    g = g + recv_ref[i].astype(F32)
                if whole:
                    sum_ref = refs[-1]
                    sum_ref[...] = g
                    g = sum_ref[0:r, :]
                nm = ADAM_B1 * m_ref[0] + (1.0 - ADAM_B1) * g
                nv = ADAM_B2 * v_ref[0] + (1.0 - ADAM_B2) * (g * g)
                g_ref[0] = g
                nm_ref[0] = nm
                nv_ref[0] = nv
                d_ref[0] = -ADAM_LR * ((nm * c1) / (jnp.sqrt(nv * c2) + ADAM_EPS) + ADAM_WD * w_ref[0])

    tile = pl.BlockSpec((1, tr, c), lambda bi, i: (first_slab + bi, i, 0))
    piece = lambda slab: pl.BlockSpec((N_DEV, rp if whole else tr, c), lambda bi, i: (0, jnp.where(bi == slab, i, 0), 0))
    earlier = [] if into is None else list(into)
    return pl.pallas_call(
        body, name=name, grid=(b, r // tr),
        in_specs=[piece(slab) for slab in range(b)] + [tile, tile, tile] + [pl.BlockSpec(memory_space=pl.ANY)] * len(earlier),
        out_specs=(tile,) * 4, out_shape=(jax.ShapeDtypeStruct(w.shape, F32),) * 4,
        input_output_aliases={b + 3 + i: i for i in range(len(earlier))},
        scratch_shapes=[pltpu.VMEM((rp, c), F32)] if whole else [],
    )(*recvs, w, m, v, *earlier)


def _flat_rows(n_elems, row_multiple):
    rows = -(-n_elems // FLAT_COLS)
    return -(-rows // row_multiple) * row_multiple


def _pack(arrays, row_multiple, dtype):
    flat = jnp.concatenate([a.reshape(-1).astype(dtype) for a in arrays])
    rows = _flat_rows(flat.size, row_multiple)
    return jnp.pad(flat, (0, rows * FLAT_COLS - flat.size)).reshape(rows, FLAT_COLS)


def _unpack(flat2d, shapes):
    lead = flat2d.shape[:-2]
    flat = flat2d.reshape(lead + (-1,))
    out, off = [], 0
    for shp in shapes:
        n = int(np.prod(shp))
        out.append(flat[..., off:off + n].reshape(lead + tuple(shp)))
        off += n
    return out


def _join_shards(stacked, axis):
    moved = jnp.moveaxis(stacked, 0, axis)
    shp = list(moved.shape)
    shp[axis:axis + 2] = [shp[axis] * shp[axis + 1]]
    return moved.reshape(shp)


def _split_shards(full, axis):
    shp = list(full.shape)
    shp[axis:axis + 1] = [N_DEV, shp[axis] // N_DEV]
    return jnp.moveaxis(full.reshape(shp), axis, 0)


def kernel(x, meta_tokens, norm_w, ffn_w_gate, ffn_w_up, ffn_w_down, rel_bias_table, even_w_in, even_conv_w, swa_sinks, dn_a_log, dn_dt_bias, dn_norm_w, even_w_out, odd_w_in, gla_w_gate_up, gla_b_gate, gla_norm_w, odd_w_out, loss_target, m_meta_tokens, m_norm_w, m_ffn_w_gate, m_ffn_w_up, m_ffn_w_down, m_rel_bias_table, m_even_w_in, m_even_conv_w, m_swa_sinks, m_dn_a_log, m_dn_dt_bias, m_dn_norm_w, m_even_w_out, m_odd_w_in, m_gla_w_gate_up, m_gla_b_gate, m_gla_norm_w, m_odd_w_out, v_meta_tokens, v_norm_w, v_ffn_w_gate, v_ffn_w_up, v_ffn_w_down, v_rel_bias_table, v_even_w_in, v_even_conv_w, v_swa_sinks, v_dn_a_log, v_dn_dt_bias, v_dn_norm_w, v_even_w_out, v_odd_w_in, v_gla_w_gate_up, v_gla_b_gate, v_gla_norm_w, v_odd_w_out):
    args = locals()
    w = {n: args[n] for n in WEIGHTS}
    m = {n: args["m_" + n] for n in WEIGHTS}
    v = {n: args["v_" + n] for n in WEIGHTS}

    d = D_MODEL
    me = _my_index()
    whole = lambda ref, j: ref
    rows = lambda size, base=0: (lambda ref, i: ref.at[_block(i, size, base), :])
    lead = lambda ref, i: ref.at[i]
    of_group = lambda items, g: [it for it in items if it[4] == g]
    names = lambda items, k: list(dict.fromkeys(it[k] for it in items))

    def placed(shape, dtype, parts):
        land = lax.empty(shape, dtype)
        for part, axis, start in parts:
            land = lax.dynamic_update_slice(land, part, tuple(start if a == axis else 0 for a in range(land.ndim)))
        return land

    as_rows = lambda a: jnp.swapaxes(a, -1, -2)
    pad_rows = lambda a, to: jnp.pad(a, [(0, 0)] * (a.ndim - 2) + [(0, to - a.shape[-2]), (0, 0)])
    gate_s = pad_rows(as_rows(w["ffn_w_gate"].reshape(N_FFN, d, FF_SHARD)), FF_SHARD_PAD).astype(BF16)
    up_s = pad_rows(as_rows(w["ffn_w_up"].reshape(N_FFN, d, FF_SHARD)), FF_SHARD_PAD).astype(BF16)
    down_s = pad_rows(w["ffn_w_down"].reshape(N_FFN, FF_SHARD, d), FF_SHARD_PAD).astype(BF16)
    small_s = _pack([w[n] for n in SMALL], 8, F32)
    srcs_w = {"ein": pad_rows(as_rows(w["even_w_in"][0]), EVEN_IN_SHARD_PAD).astype(BF16),
              "oin": pad_rows(as_rows(w["odd_w_in"][0]), ODD_IN_SHARD_PAD).astype(BF16),
              "eout": w["even_w_out"][0].astype(BF16), "oout": w["odd_w_out"][0].astype(BF16), "small": small_s}
    lands_w = {"ein": placed((N_DEV * EVEN_IN_SHARD_PAD, d), BF16, [(srcs_w["ein"], 0, me * EVEN_IN_SHARD_PAD)]),
               "oin": placed((N_DEV * ODD_IN_SHARD_PAD, d), BF16, [(srcs_w["oin"], 0, me * ODD_IN_SHARD_PAD)]),
               "eout": placed((d, d), BF16, [(srcs_w["eout"], 0, me * OUT_SHARD)]),
               "oout": placed((d, d), BF16, [(srcs_w["oout"], 0, me * OUT_SHARD)]),
               "small": placed((N_DEV,) + small_s.shape, F32, [(small_s[None], 0, me)])}
    items_w = [("small", whole, "small", lead, "first"), ("ein", whole, "ein", rows(EVEN_IN_SHARD_PAD), "even"),
               ("eout", whole, "eout", rows(OUT_SHARD), "even"), ("oin", whole, "oin", rows(ODD_IN_SHARD_PAD), "odd"),
               ("oout", whole, "oout", rows(OUT_SHARD), "odd")]
    for i, group, down_group in ((0, "first", "down0"), (1, "ffn1", "ffn1"), (2, "ffn2", "down2"), (3, "ffn3", "ffn3")):
        srcs_w.update({f"gate{i}": gate_s[i], f"up{i}": up_s[i], f"down{i}": down_s[i]})
        lands_w[f"w_gu{i}"] = placed((2 * FF_PAD, d), BF16, [(srcs_w[f"gate{i}"], 0, me * FF_SHARD_PAD),
                                                             (srcs_w[f"up{i}"], 0, FF_PAD + me * FF_SHARD_PAD)])
        lands_w[f"w_down{i}"] = placed((FF_PAD, d), BF16, [(srcs_w[f"down{i}"], 0, me * FF_SHARD_PAD)])
        items_w += [(f"gate{i}", whole, f"w_gu{i}", rows(FF_SHARD_PAD), group),
                    (f"up{i}", whole, f"w_gu{i}", rows(FF_SHARD_PAD, FF_PAD), group),
                    (f"down{i}", whole, f"w_down{i}", rows(FF_SHARD_PAD), down_group)]
    pending, started = {}, []
    for g in ("first", "down0", "even", "ffn1", "ffn2", "down2", "odd", "ffn3"):
        its = of_group(items_w, g)
        srcs = {n: srcs_w[n] for n in names(its, 0)}
        sems, lands, token = start_copies(srcs, {n: lands_w[n] for n in names(its, 2)}, its, name=f"gather_start_{g}")
        pending[g] = (sems, srcs, lands, its)
        started.append(token)

    unpad = lambda p, shard, shard_pad: p.reshape(N_DEV, shard_pad, d)[:, :shard].reshape(N_DEV * shard, d)

    def get_w(stage, after):
        if stage not in pending:
            return {}
        sems, srcs, lands, its = pending[stage]
        landed = wait_copies(sems, srcs, lands, its, after, name=f"gather_wait_{stage}")
        got = {}
        for n, arr in landed.items():
            if n == "small":
                for sn, stacked in zip(SMALL, _unpack(arr, [w[sn].shape for sn in SMALL])):
                    got[sn] = _join_shards(stacked, SHARD_AXIS[sn])
            elif n == "ein":
                got["even_w_in"] = unpad(arr, EVEN_IN_SHARD, EVEN_IN_SHARD_PAD)
            elif n == "oin":
                got["odd_w_in"] = unpad(arr, ODD_IN_SHARD, ODD_IN_SHARD_PAD)
            elif n in ("eout", "oout"):
                got["even_w_out" if n == "eout" else "odd_w_out"] = arr
            else:
                got[n] = (arr, None)
        return got

    full = {n: w[n] for n in REPL}
    full.update(get_w("first", sum(started)))

    repad = lambda g, shard, shard_pad: pad_rows(g.reshape(N_DEV, shard, d), shard_pad)
    pieces_g = {"r_oin": ("oin", None, "gu2"), "r_oout": ("oout", (OUT_SHARD, 0), "gu2"),
                "r_ein": ("ein", None, "even"), "r_eout": ("eout", (OUT_SHARD, 0), "even"), "r_small": ("small", None, "last")}
    for i in range(N_FFN):
        pieces_g.update({f"r_gate{i}": (f"g_gu{i}", (FF_SHARD_PAD, 0), f"gu{i}"),
                         f"r_up{i}": (f"g_gu{i}", (FF_SHARD_PAD, FF_PAD), f"gu{i}"),
                         f"r_down{i}": (f"g_down{i}", (FF_SHARD_PAD, 0), "down0" if i == 0 else f"gu{i}")})
    items_g = [(src, lead if blk is None else rows(*blk), land, lead, group) for land, (src, blk, group) in pieces_g.items()]
    last_groups = ("down0", "gu0")

    def grad_src(n, grads):
        if n == "oin":
            return repad(grads["odd_w_in"], ODD_IN_SHARD, ODD_IN_SHARD_PAD).astype(BF16)
        if n == "ein":
            return repad(grads["even_w_in"], EVEN_IN_SHARD, EVEN_IN_SHARD_PAD).astype(BF16)
        if n in ("oout", "eout"):
            return grads["odd_w_out" if n == "oout" else "even_w_out"].astype(BF16)
        return grads[n]

    def grad_land(n, srcs):
        src, blk, _ = pieces_g[n]
        if blk is None:
            own = lax.dynamic_index_in_dim(srcs[src], me, 0, keepdims=False)
        else:
            own = lax.dynamic_slice_in_dim(srcs[src], blk[1] + me * blk[0], blk[0], 0)
        return placed((N_DEV,) + own.shape, own.dtype, [(own[None], 0, me)])

    sent = {}

    def put_g(stage, grads):
        its = of_group(items_g, stage)
        if not its:
            return jnp.zeros((1, 1), F32)
        srcs = {n: grad_src(n, grads) for n in names(its, 0)}
        lands = {n: grad_land(n, srcs) for n in names(its, 2)}
        sems, lands, token = start_copies(srcs, lands, its, name=f"grads_start_{stage}")
        sent[stage] = (sems, srcs, lands, its)
        return token

    loss, grad_x, grads = local_step(x[0], loss_target[0], full, get_w, put_g)
    loss = lax.psum(loss, AXES)

    order = SMALL + REPL
    pieces = [_split_shards(grads[n].reshape(full[n].shape), SHARD_AXIS[n]) if n in SHARD_AXIS
              else jnp.broadcast_to(grads[n].reshape(w[n].shape)[None], (N_DEV,) + w[n].shape) for n in order]
    flat = jnp.concatenate([p.reshape(N_DEV, -1) for p in pieces], axis=1)
    srows = _flat_rows(flat.shape[1], 8)
    grads["small"] = jnp.pad(flat, ((0, 0), (0, srows * FLAT_COLS - flat.shape[1]))).reshape(N_DEV, srows, FLAT_COLS)
    recv = {}
    for stage, (sems, srcs, lands, its) in sent.items():
        if stage not in last_groups:
            recv.update(wait_copies(sems, srcs, lands, its, grad_x, name=f"grads_wait_{stage}"))
    result = [{} for _ in range(4)]

    views = {"ffn_w_gate": (lambda a: as_rows(a.reshape(N_FFN, d, FF_SHARD)), lambda o, n: as_rows(o).reshape(w[n].shape)),
             "ffn_w_up": (lambda a: as_rows(a.reshape(N_FFN, d, FF_SHARD)), lambda o, n: as_rows(o).reshape(w[n].shape)),
             "ffn_w_down": (lambda a: a.reshape(N_FFN, FF_SHARD, d), lambda o, n: o.reshape(w[n].shape)),
             "even_w_in": (as_rows, lambda o, n: as_rows(o)), "odd_w_in": (as_rows, lambda o, n: as_rows(o)),
             "even_w_out": (lambda a: a, lambda o, n: o), "odd_w_out": (lambda a: a, lambda o, n: o)}

    def adam(n, recvs, first_slab=0, into=None):
        view = views[n][0]
        return sum_adamw(recvs, view(w[n]), view(m[n]), view(v[n]), name=f"adamw_{n}_{first_slab}",
                         first_slab=first_slab, into=into)

    def finish(n, outs):
        for r, o in zip(result, outs):
            r[n] = views[n][1](o, n)

    ffn_recv = (("ffn_w_gate", "r_gate"), ("ffn_w_up", "r_up"), ("ffn_w_down", "r_down"))
    early = {n: adam(n, [recv[f"{r}{i}"] for i in (1, 2, 3)], first_slab=1) for n, r in ffn_recv}
    for n, r in (("even_w_in", "r_ein"), ("odd_w_in", "r_oin"), ("even_w_out", "r_eout"), ("odd_w_out", "r_oout")):
        finish(n, adam(n, [recv[r]]))
    srcs = {"small": grads["small"]}
    recv.update(exchange(srcs, {"r_small": grad_land("r_small", srcs)}, of_group(items_g, "last"),
                         early["ffn_w_down"][0], name="exchange_small"))
    for stage in last_groups:
        sems, srcs, lands, its = sent[stage]
        recv.update(wait_copies(sems, srcs, lands, its, recv["r_small"], name=f"grads_wait_{stage}"))
    for n, r in ffn_recv:
        finish(n, adam(n, [recv[f"{r}0"]], into=early[n]))
    pack_local = lambda t: _pack([t[n] for n in order], 8, F32)[None]
    small_outs = sum_adamw([recv["r_small"]], pack_local(w), pack_local(m), pack_local(v), name="adamw_small")
    for r, o in zip(result, small_outs):
        r.update(zip(order, _unpack(o[0], [w[n].shape for n in order])))
    return (loss, grad_x[None], *[r[n] for r in result for n in WEIGHTS])
```

```python
import functools
import math

import numpy as np
import jax
import jax.numpy as jnp
from jax import lax
from jax.experimental import pallas as pl
from jax.experimental.pallas import tpu as pltpu

F32 = jnp.float32
BF16 = jnp.bfloat16
MESH = pl.DeviceIdType.MESH
AXES = ("x", "y", "c")
N_DEV = 8

D_MODEL = 1024
N_META = 16
D_FF = 2816
NORM_EPS = 1e-6
NEG_INF = -1e30
SWA_Q_HEADS = 8
SWA_HEAD_DIM = 64
SWA_WINDOW = 128
SWA_BLOCK = 128
REL_BUCKETS = 32
REL_MAX_DIST = 128
DN_HEADS = 4
DN_HEAD_DIM = 128
DN_CONV = 4
GLA_HEADS = 4
GLA_DK = 128
GLA_DV = 256
GLA_GATE_RANK = 16
GLA_GATE_NORM = 16.0
CHUNK = 64
CHUNKS_PER_STEP = 6
PAD = SWA_BLOCK - N_META
LANE = 128
PROJ_DIM = 3200

ADAM_LR = 0.001
ADAM_B1 = 0.9
ADAM_B2 = 0.999
ADAM_EPS = 1e-08
ADAM_WD = 0.01
ADAM_STEP = 10

ROW_TILE = 16
FF_SHARD = D_FF // N_DEV
FF_SHARD_PAD = -(-FF_SHARD // ROW_TILE) * ROW_TILE
FF_PAD = N_DEV * FF_SHARD_PAD
N_FFN = 4
EVEN_IN_SHARD, EVEN_IN_SHARD_PAD = 353, 368
ODD_IN_SHARD, ODD_IN_SHARD_PAD = 386, 400
OUT_SHARD = D_MODEL // N_DEV

FLAT_COLS = 128
BIG = ("ffn_w_gate", "ffn_w_up", "ffn_w_down", "even_w_in", "even_w_out", "odd_w_in", "odd_w_out")
SMALL = ("meta_tokens", "norm_w", "even_conv_w", "gla_w_gate_up", "gla_b_gate", "gla_norm_w")
REPL = ("rel_bias_table", "swa_sinks", "dn_a_log", "dn_dt_bias", "dn_norm_w")
WEIGHTS = ("meta_tokens", "norm_w", "ffn_w_gate", "ffn_w_up", "ffn_w_down", "rel_bias_table", "even_w_in",
           "even_conv_w", "swa_sinks", "dn_a_log", "dn_dt_bias", "dn_norm_w", "even_w_out", "odd_w_in",
           "gla_w_gate_up", "gla_b_gate", "gla_norm_w", "odd_w_out")
SHARD_AXIS = {"ffn_w_gate": 3, "ffn_w_up": 3, "ffn_w_down": 2, "even_w_in": 2, "even_w_out": 1, "odd_w_in": 2,
              "odd_w_out": 1, "meta_tokens": 1, "norm_w": 2, "even_conv_w": 2, "gla_w_gate_up": 2,
              "gla_b_gate": 1, "gla_norm_w": 1}


def _rms(x, w):
    r = lax.rsqrt(jnp.mean(x * x, axis=-1, keepdims=True) + NORM_EPS)
    return x * r * w


def _sigmoid(x):
    return 0.5 * (jnp.tanh(0.5 * x) + 1.0)


def _silu(x):
    return x * _sigmoid(x)


def _softplus(x):
    pos = x > 0
    return jnp.where(pos, x, 0.0) + jnp.log(1.0 + jnp.exp(jnp.where(pos, -x, x)))


def _l2n(x):
    return x * lax.rsqrt(jnp.sum(x * x, axis=-1, keepdims=True) + 1e-6)


def _split_bf16(x):
    hi = x.astype(BF16)
    return hi, (x - hi.astype(F32)).astype(BF16)


def _make_mm(terms, batched):
    off = 1 if batched else 0
    bdims = ((0,), (0,)) if batched else ((), ())

    def dg(a, b, ca, cb):
        dot = lambda p, q: lax.dot_general(p, q, (((ca + off,), (cb + off,)), bdims), preferred_element_type=F32)
        a_hi, a_lo = _split_bf16(a)
        b_hi, b_lo = _split_bf16(b)
        if terms == 1:
            return dot(a_hi, b_hi)
        return dot(a_hi, b_hi) + (dot(a_hi, b_lo) + dot(a_lo, b_hi))

    @jax.custom_vjp
    def nn(a, b):
        return dg(a, b, 1, 0)

    @jax.custom_vjp
    def nt(a, b):
        return dg(a, b, 1, 1)

    @jax.custom_vjp
    def tn(a, b):
        return dg(a, b, 0, 0)

    nn.defvjp(lambda a, b: (nn(a, b), (a, b)), lambda r, g: (nt(g, r[1]), tn(r[0], g)))
    nt.defvjp(lambda a, b: (nt(a, b), (a, b)), lambda r, g: (nn(g, r[1]), tn(g, r[0])))
    tn.defvjp(lambda a, b: (tn(a, b), (a, b)), lambda r, g: (nt(r[1], g), nn(r[0], g)))
    return nn, nt, tn


_mm, _mm_nt, _mm_tn = _make_mm(1, False)
_mm3, _, _ = _make_mm(3, False)
_bmm, _bmm_nt, _bmm_tn = _make_mm(1, True)
_bmm3, _bmm3_nt, _bmm3_tn = _make_mm(3, True)


@jax.custom_vjp
def _known_inverse(a, inv):
    return inv


_known_inverse.defvjp(lambda a, inv: (inv, inv),
                      lambda inv, g: (-_bmm3_tn(inv, _bmm3_nt(g, inv)), jnp.zeros_like(inv)))


def _tri_ones_dot(x, lower):
    n = x.shape[0]
    r = lax.broadcasted_iota(jnp.int32, (n, n), 0)
    c = lax.broadcasted_iota(jnp.int32, (n, n), 1)
    t = ((r >= c) if lower else (r <= c)).astype(BF16)
    hi, lo = _split_bf16(x)
    return jnp.dot(t, hi, preferred_element_type=F32) + jnp.dot(t, lo, preferred_element_type=F32)


@jax.custom_vjp
def _cumsum_rows(x):
    return _tri_ones_dot(x, True)


_cumsum_rows.defvjp(lambda x: (_tri_ones_dot(x, True), None), lambda _, g: (_tri_ones_dot(g, False),))


def _row_tile(n_rows, cap):
    best = LANE
    for t in range(LANE, cap + 1, LANE):
        if n_rows % t == 0:
            best = t
    return best


def _real_rows(tile_index, tm):
    row = tile_index * tm + lax.broadcasted_iota(jnp.int32, (tm, 1), 0)
    return (row >= PAD).astype(F32)


def _full(shape):
    return pl.BlockSpec(shape, lambda *_: (0,) * len(shape))


def _resident(shape):
    return pl.BlockSpec(shape, lambda *_: (0,) * len(shape), pipeline_mode=pl.Buffered(1))


def _resident_w(wmat, widx):
    if wmat.ndim == 2:
        return _resident(wmat.shape)
    return pl.BlockSpec((None,) + wmat.shape[1:], lambda *_: (widx, 0, 0), pipeline_mode=pl.Buffered(1))


def rms_mm(h, w, wmat_t, *, swiglu, name, widx=None):
    tp, d = h.shape
    n = wmat_t.shape[-2]
    tm = _row_tile(tp, 384)
    half = n // 2
    wmat = wmat_t

    def body(h_ref, w_ref, wm_ref, hn_ref, *outs):
        hn = _rms(h_ref[...], w_ref[...]).astype(BF16)
        hn_ref[...] = hn
        p = lax.dot_general(hn, wm_ref[...], (((1,), (1,)), ((), ())), preferred_element_type=F32)
        if swiglu:
            g, u = p[:, :half], p[:, half:]
            outs[0][...] = g.astype(BF16)
            outs[1][...] = u.astype(BF16)
            outs[2][...] = (_silu(g) * u).astype(BF16)
        else:
            outs[0][...] = p

    row = lambda width: pl.BlockSpec((tm, width), lambda i: (i, 0))
    if swiglu:
        out_shape = (jax.ShapeDtypeStruct((tp, d), BF16),) + (jax.ShapeDtypeStruct((tp, half), BF16),) * 3
        out_specs = (row(d), row(half), row(half), row(half))
    else:
        out_shape = (jax.ShapeDtypeStruct((tp, d), BF16), jax.ShapeDtypeStruct((tp, n), F32))
        out_specs = (row(d), row(n))
    return pl.pallas_call(
        body, name=name, grid=(tp // tm,),
        in_specs=[row(d), _full((1, d)), _resident_w(wmat, widx)],
        out_specs=out_specs, out_shape=out_shape,
    )(h, w, wmat)


def mm_rms_res(acts, wmat, h, w, *, scale, name, widx=None):
    tp, d = h.shape
    tm = _row_tile(tp, 384)
    widths = [a.shape[1] for a in acts]
    offs = [sum(widths[:i]) for i in range(len(acts))]
    na = len(acts)

    def body(*refs):
        a_refs = refs[:na]
        wm_ref, h_ref, w_ref, f_ref, ho_ref = refs[na:]
        f = None
        for a_ref, off, width in zip(a_refs, offs, widths):
            part = jnp.dot(a_ref[...].astype(BF16), wm_ref[off:off + width, :], preferred_element_type=F32)
            f = part if f is None else f + part
        f_ref[...] = f
        ho_ref[...] = h_ref[...] + scale * _rms(f, w_ref[...])

    row = lambda width: pl.BlockSpec((tm, width), lambda i: (i, 0))
    return pl.pallas_call(
        body, name=name, grid=(tp // tm,),
        in_specs=[row(wd) for wd in widths] + [_resident_w(wmat, widx), row(d), _full((1, d))],
        out_specs=(row(d), row(d)),
        out_shape=(jax.ShapeDtypeStruct((tp, d), F32), jax.ShapeDtypeStruct((tp, d), F32)),
    )(*acts, wmat, h, w)


def mm_rms_res_bwd(dho, f, w, wmat, gu, *, scale, name, widx=None):
    tp, d = f.shape
    k = wmat.shape[-2]
    tm = _row_tile(tp, 384)
    swiglu = gu is not None

    def body(*refs):
        if swiglu:
            dho_ref, f_ref, w_ref, wm_ref, g_ref, u_ref, df_ref, dw_ref, dgu_ref = refs
        else:
            dho_ref, f_ref, w_ref, wm_ref, df_ref, dw_ref, da_ref = refs
        i = pl.program_id(0)
        _, vjp = jax.vjp(lambda ff, ww: scale * _rms(ff, ww), f_ref[...], w_ref[...])
        df, dw = vjp(dho_ref[...])
        dfb = (df * _real_rows(i, tm)).astype(BF16)
        df_ref[...] = dfb

        @pl.when(i == 0)
        def _():
            dw_ref[...] = jnp.zeros_like(dw_ref)

        dw_ref[...] += dw
        da = lax.dot_general(dfb, wm_ref[...], (((1,), (1,)), ((), ())), preferred_element_type=F32)
        if swiglu:
            g, u, dab = g_ref[...], u_ref[...], da.astype(BF16)
            s = _sigmoid(g)
            dgu_ref[:, :k] = dab * u * s * (1.0 + g * (1.0 - s))
            dgu_ref[:, k:] = dab * g * s
        else:
            da_ref[...] = da

    row = lambda width: pl.BlockSpec((tm, width), lambda i: (i, 0))
    in_specs = [row(d), row(d), _full((1, d)), _resident_w(wmat, widx)]
    args = [dho, f, w, wmat]
    out_shape = [jax.ShapeDtypeStruct((tp, d), BF16), jax.ShapeDtypeStruct((1, d), F32)]
    out_specs = [row(d), _full((1, d))]
    if swiglu:
        in_specs += [row(k), row(k)]
        args += list(gu)
        out_shape += [jax.ShapeDtypeStruct((tp, 2 * k), BF16)]
        out_specs += [row(2 * k)]
    else:
        out_shape += [jax.ShapeDtypeStruct((tp, k), F32)]
        out_specs += [row(k)]
    return pl.pallas_call(body, name=name, grid=(tp // tm,), in_specs=in_specs, out_specs=tuple(out_specs),
                          out_shape=tuple(out_shape))(*args)


def rms_mm_bwd(dps, wmat, h, w, dho, *, name, widx=None):
    tp, d = h.shape
    tm = _row_tile(tp, 384)
    widths = [p.shape[1] for p in dps]
    offs = [sum(widths[:i]) for i in range(len(dps))]
    ndp = len(dps)

    def body(*refs):
        dp_refs = refs[:ndp]
        wm_ref, h_ref, w_ref, dho_ref, dh_ref, dw_ref = refs[ndp:]
        i = pl.program_id(0)
        dhn = None
        for dp_ref, off, width in zip(dp_refs, offs, widths):
            part = jnp.dot(dp_ref[...].astype(BF16), wm_ref[off:off + width, :], preferred_element_type=F32)
            dhn = part if dhn is None else dhn + part
        _, vjp = jax.vjp(_rms, h_ref[...], w_ref[...])
        dx, dw = vjp(dhn)
        dh_ref[...] = (dho_ref[...] + dx) * _real_rows(i, tm)

        @pl.when(i == 0)
        def _():
            dw_ref[...] = jnp.zeros_like(dw_ref)

        dw_ref[...] += dw

    row = lambda width: pl.BlockSpec((tm, width), lambda i: (i, 0))
    return pl.pallas_call(
        body, name=name, grid=(tp // tm,),
        in_specs=[row(wd) for wd in widths] + [_resident_w(wmat, widx), row(d), _full((1, d)), row(d)],
        out_specs=(row(d), _full((1, d))),
        out_shape=(jax.ShapeDtypeStruct((tp, d), F32), jax.ShapeDtypeStruct((1, d), F32)),
    )(*dps, wmat, h, w, dho)


def mm_tn(a, b, *, name, out_dtype=F32, after=None):
    t, m = a.shape
    n = b.shape[1]
    bm = _row_tile(m, 704)
    ties = [] if after is None else [after]

    def body(a_ref, b_ref, *rest):
        rest[-1][...] = lax.dot_general(a_ref[...].astype(BF16), b_ref[...].astype(BF16), (((0,), (0,)), ((), ())),
                                        preferred_element_type=F32).astype(out_dtype)

    return pl.pallas_call(
        body, name=name, grid=(m // bm,),
        in_specs=[pl.BlockSpec((t, bm), lambda i: (0, i)), _resident((t, n))] + [pl.BlockSpec(memory_space=pl.ANY)] * len(ties),
        out_specs=pl.BlockSpec((bm, n), lambda i: (i, 0)),
        out_shape=jax.ShapeDtypeStruct((m, n), out_dtype),
    )(a, b, *ties)


def loss_and_grad(h, target, *, name):
    tp, d = h.shape
    tm = SWA_BLOCK

    def body(h_ref, t_ref, dh_ref, loss_ref):
        i = pl.program_id(0)

        @pl.when(i == 0)
        def _():
            loss_ref[...] = jnp.zeros_like(loss_ref)
            dh_ref[...] = jnp.zeros_like(dh_ref)

        @pl.when(i > 0)
        def _():
            err = h_ref[...] - t_ref[...]
            dh_ref[...] = err * (1.0 / d)
            loss_ref[...] += 0.5 * jnp.sum(jnp.sum(err * err, axis=1, keepdims=True), axis=0, keepdims=True) * (1.0 / d)

    return pl.pallas_call(
        body, name=name, grid=(tp // tm,),
        in_specs=[pl.BlockSpec((tm, d), lambda i: (i, 0)), pl.BlockSpec((tm, d), lambda i: (jnp.maximum(i - 1, 0), 0))],
        out_specs=(pl.BlockSpec((tm, d), lambda i: (i, 0)), _full((1, 1))),
        out_shape=(jax.ShapeDtypeStruct((tp, d), F32), jax.ShapeDtypeStruct((1, 1), F32)),
    )(h, target)


def _t5_bucket_np(rel):
    n = np.maximum(rel, 0)
    max_exact = REL_BUCKETS // 2
    n_f = np.maximum(n, 1).astype(np.float32)
    large = max_exact + (np.log(n_f / np.float32(max_exact)) / np.float32(math.log(REL_MAX_DIST / max_exact))
                         * np.float32(REL_BUCKETS - max_exact)).astype(np.int32)
    large = np.minimum(large, REL_BUCKETS - 1)
    return np.where(n < max_exact, n, large).astype(np.int32)


def _swa_positions_np(n):
    i = np.arange(SWA_BLOCK)[:, None]
    j = np.arange(3 * SWA_BLOCK)[None, :]
    pos_q = n * SWA_BLOCK + i - PAD
    pos_k = np.where(j < SWA_BLOCK, j - PAD, (n - 1) * SWA_BLOCK + (j - SWA_BLOCK) - PAD)
    return pos_q, pos_k


def _swa_buckets():
    out = []
    for n in range(3):
        pos_q, pos_k = _swa_positions_np(n)
        out.append(_t5_bucket_np(pos_q - pos_k))
    return jnp.asarray(np.stack(out))


def swa_bias(table, buckets, *, name):
    nc, nq, nk = buckets.shape

    def body(tab_ref, bkt_ref, out_ref):
        for c in range(nc):
            bkt = bkt_ref[c]
            for h in range(SWA_Q_HEADS):
                acc = jnp.zeros((nq, nk), F32)
                for b in range(REL_BUCKETS):
                    acc = jnp.where(bkt == b, tab_ref[b, h], acc)
                out_ref[c, h] = acc

    return pl.pallas_call(
        body, name=name,
        in_specs=[pl.BlockSpec(memory_space=pltpu.SMEM), pl.BlockSpec(memory_space=pltpu.VMEM)],
        out_specs=pl.BlockSpec(memory_space=pltpu.VMEM),
        out_shape=jax.ShapeDtypeStruct((nc, SWA_Q_HEADS, nq, nk), F32),
    )(table, buckets)


def swa_bias_bwd(dbias, buckets, *, name):
    nc = buckets.shape[0]

    def body(db_ref, bkt_ref, out_ref):
        lane = lax.broadcasted_iota(jnp.int32, (1, LANE), 1)
        for b in range(REL_BUCKETS):
            row = jnp.zeros((1, LANE), F32)
            for c in range(nc):
                hit = bkt_ref[c] == b
                for h in range(SWA_Q_HEADS):
                    part = jnp.where(hit, db_ref[c, h], 0.0)
                    tot = jnp.sum(jnp.sum(part, axis=1, keepdims=True), axis=0, keepdims=True)
                    row = row + jnp.where(lane == h, tot, 0.0)
            out_ref[b:b + 1, :] = row

    return pl.pallas_call(
        body, name=name,
        in_specs=[pl.BlockSpec(memory_space=pltpu.VMEM), pl.BlockSpec(memory_space=pltpu.VMEM)],
        out_specs=pl.BlockSpec(memory_space=pltpu.VMEM),
        out_shape=jax.ShapeDtypeStruct((REL_BUCKETS, LANE), F32),
    )(dbias, buckets)


def _swa_block(q, kvm, kvp, kvc, bias, sinks, n, batched):
    blk = SWA_BLOCK
    i = lax.broadcasted_iota(jnp.int32, (blk, 3 * blk), 0)
    j = lax.broadcasted_iota(jnp.int32, (blk, 3 * blk), 1)
    pos_q = n * blk + i - PAD
    is_meta = j < blk
    pos_k = jnp.where(is_meta, j - PAD, (n - 1) * blk + (j - blk) - PAD)
    rel = pos_q - pos_k
    valid = ((is_meta & (pos_k >= 0) & (pos_k < N_META) & (rel >= 0))
             | (jnp.logical_not(is_meta) & (pos_k >= N_META) & (rel >= 0) & (rel < SWA_WINDOW)))
    valid_f = valid.astype(F32)
    kv =jnp.concatenate([kvm, kvp, kvc], axis=0)
    lane = lax.broadcasted_iota(jnp.int32, (1, LANE), 1)
    halves = ((lane < SWA_HEAD_DIM).astype(F32), (lane >= SWA_HEAD_DIM).astype(F32))
    nh, group = SWA_Q_HEADS, SWA_Q_HEADS // 2
    q_of = lambda h: q[:, (h // 2) * LANE:(h // 2 + 1) * LANE] * halves[h % 2]
    k_of = lambda h: kv[:, (h // group) * LANE:(h // group + 1) * LANE]
    v_of = lambda h: kv[:, (2 + h // group) * LANE:(3 + h // group) * LANE]
    sink_of = lambda h: jnp.sum(jnp.where(lane == h, sinks, 0.0), axis=1, keepdims=True)

    scale = SWA_HEAD_DIM ** -0.5

    def attend(logits, sink, pv):
        if batched:
            s = logits * valid_f + (valid_f - 1.0) * (-NEG_INF)
        else:
            s = jnp.where(valid, logits, NEG_INF)
        m =lax.stop_gradient(jnp.maximum(jnp.max(s, axis=-1, keepdims=True), sink))
        e = jnp.exp(s - m)
        return pv(e / (jnp.sum(e, axis=-1, keepdims=True) + jnp.exp(sink - m)))

    if batched:
        heads = range(nh)
        vh = _stack([v_of(h) for h in heads])
        qk = _bmm_nt(_stack([q_of(h) for h in heads]), _stack([k_of(h) for h in heads]))
        o = attend(qk * scale + bias, _stack([sink_of(h) for h in heads]), lambda p: _bmm(p, vh))
        head = lambda h: o[h]
    else:
        head = lambda h: attend(_mm_nt(q_of(h), k_of(h)) * scale + bias[h], sink_of(h), lambda p: _mm(p, v_of(h)))
    return jnp.concatenate([head(2 * p) * halves[0] + head(2 * p + 1) * halves[1] for p in range(nh // 2)], axis=1)


def _swa_in_specs(nb, rev):
    blk = SWA_BLOCK
    step = (lambda i: nb - 1 - i) if rev else (lambda i: i)
    return [
        pl.BlockSpec((blk, 4 * LANE), lambda i: (step(i), 0)),
        pl.BlockSpec((blk, 4 * LANE), lambda i: (0, 1)),
        pl.BlockSpec((blk, 4 * LANE), lambda i: (jnp.maximum(step(i) - 1, 0), 1)),
        pl.BlockSpec((blk, 4 * LANE), lambda i: (step(i), 1)),
        pl.BlockSpec((1, SWA_Q_HEADS, blk, 3 * blk), lambda i: (jnp.minimum(step(i), 2), 0, 0, 0)),
        _full((1, LANE)),
    ]


def swa_fwd(proj, bias, sinks, *, name):
    tp = proj.shape[0]
    nb = tp // SWA_BLOCK

    def body(q_ref, kvm_ref, kvp_ref, kvc_ref, bias_ref, sinks_ref, o_ref):
        n = pl.program_id(0)
        o_ref[...] = _swa_block(q_ref[...], kvm_ref[...], kvp_ref[...], kvc_ref[...], bias_ref[0], sinks_ref[...], n, True)

    return pl.pallas_call(
        body, name=name, grid=(nb,),
        in_specs=_swa_in_specs(nb, False),
        out_specs=pl.BlockSpec((SWA_BLOCK, 4 * LANE), lambda i: (i, 0)),
        out_shape=jax.ShapeDtypeStruct((tp, 4 * LANE), F32),
    )(proj, proj, proj, proj, bias, sinks)


def swa_bwd(proj, bias, sinks, do, *, name):
    tp = proj.shape[0]
    nb = tp // SWA_BLOCK
    blk = SWA_BLOCK

    def body(q_ref, kvm_ref, kvp_ref, kvc_ref, bias_ref, sinks_ref, do_ref, dqkv_ref, dbias_ref, dsinks_ref,
             carry, meta_acc):
        i = pl.program_id(0)
        n = nb - 1 - i

        @pl.when(i == 0)
        def _():
            carry[...] = jnp.zeros_like(carry)
            meta_acc[...] = jnp.zeros_like(meta_acc)
            dsinks_ref[...] = jnp.zeros_like(dsinks_ref)

        fn = lambda q, kvm, kvp, kvc, b, s: _swa_block(q, kvm, kvp, kvc, b, s, n, False)
        _, vjp = jax.vjp(fn, q_ref[...], kvm_ref[...], kvp_ref[...], kvc_ref[...], bias_ref[0], sinks_ref[...])
        dq, dkvm, dkvp, dkvc, dbias, dsinks = vjp(do_ref[...])
        dqkv_ref[:, :4 * LANE] = dq.astype(BF16)
        meta_acc[...] += dkvm
        dqkv_ref[:, 4 * LANE:] = (dkvc + carry[...] + jnp.where(n == 0, meta_acc[...], 0.0)).astype(BF16)
        carry[...] = dkvp
        first_visit = (n == nb - 1) | (n < 2)

        @pl.when(first_visit)
        def _():
            dbias_ref[0] = dbias

        @pl.when(jnp.logical_not(first_visit))
        def _():
            dbias_ref[0] += dbias

        dsinks_ref[...] += dsinks

    rev = lambda i: nb - 1 - i
    return pl.pallas_call(
        body, name=name, grid=(nb,),
        in_specs=_swa_in_specs(nb, True) + [pl.BlockSpec((blk, 4 * LANE), lambda i: (rev(i), 0))],
        out_specs=(pl.BlockSpec((blk, 8 * LANE), lambda i: (rev(i), 0)),
                   pl.BlockSpec((1, SWA_Q_HEADS, blk, 3 * blk), lambda i: (jnp.minimum(rev(i), 2), 0, 0, 0)),
                   _full((1, LANE))),
        out_shape=(jax.ShapeDtypeStruct((tp, PROJ_DIM), BF16),
                   jax.ShapeDtypeStruct((3, SWA_Q_HEADS, blk, 3 * blk), F32), jax.ShapeDtypeStruct((1, LANE), F32)),
        scratch_shapes=[pltpu.VMEM((blk, 4 * LANE), F32), pltpu.VMEM((blk, 4 * LANE), F32)],
    )(proj, proj, proj, proj, bias, sinks, do)


CONV_COL0 = 2
HALO = 8


def conv_fwd(proj, conv_w, *, name):
    tp = proj.shape[0]
    tm = _row_tile(tp, 384)
    cw = 4 * LANE
    ncol = conv_w.shape[1] // cw

    def body(x_ref, halo_ref, w_ref, y_ref, buf):
        i = pl.program_id(1)
        buf[0:HALO, :] = jnp.where(i > 0, halo_ref[...], 0.0)
        buf[HALO:, :] = x_ref[...]
        acc = None
        for j in range(DN_CONV):
            term = w_ref[j:j + 1, :] * buf[pl.ds(HALO - (DN_CONV - 1) + j, tm), :]
            acc = term if acc is None else acc + term
        y_ref[...] = acc

    return pl.pallas_call(
        body, name=name, grid=(ncol, tp // tm),
        in_specs=[pl.BlockSpec((tm, cw), lambda c, i: (i, CONV_COL0 + c)),
                  pl.BlockSpec((HALO, cw), lambda c, i: (jnp.maximum(i * (tm // HALO) - 1, 0), CONV_COL0 + c)),
                  pl.BlockSpec((DN_CONV, cw), lambda c, i: (0, c))],
        out_specs=pl.BlockSpec((tm, cw), lambda c, i: (i, c)),
        out_shape=jax.ShapeDtypeStruct((tp, ncol * cw), F32),
        scratch_shapes=[pltpu.VMEM((tm + HALO, cw), F32)],
    )(proj, proj, conv_w)


def conv_bwd(proj, conv_w, dy, dproj, *, name):
    tp = proj.shape[0]
    tm = _row_tile(tp, 384)
    cw = 4 * LANE
    ncol = conv_w.shape[1] // cw
    nt = tp // tm

    def body(x_ref, xhalo_ref, w_ref, dy_ref, dyhalo_ref, _, dx_ref, dw_ref, xbuf, dbuf):
        i = pl.program_id(1)
        xbuf[0:HALO, :] = jnp.where(i > 0, xhalo_ref[...], 0.0)
        xbuf[HALO:, :] = x_ref[...]
        dbuf[0:tm, :] = dy_ref[...]
        dbuf[tm:, :] = jnp.where(i < nt - 1, dyhalo_ref[...], 0.0)
        dy_t = dy_ref[...]
        acc = None
        rows = []
        for j in range(DN_CONV):
            term = w_ref[j:j + 1, :] * dbuf[pl.ds(DN_CONV - 1 - j, tm), :]
            acc = term if acc is None else acc + term
            rows.append(jnp.sum(dy_t * xbuf[pl.ds(HALO - (DN_CONV - 1) + j, tm), :], axis=0, keepdims=True))
        dx_ref[...] = acc.astype(BF16)

        @pl.when(i == 0)
        def _():
            dw_ref[...] = jnp.zeros_like(dw_ref)

        for j in range(DN_CONV):
            dw_ref[j:j + 1, :] += rows[j]

    return pl.pallas_call(
        body, name=name, grid=(ncol, nt),
        in_specs=[pl.BlockSpec((tm, cw), lambda c, i: (i, CONV_COL0 + c)),
                  pl.BlockSpec((HALO, cw), lambda c, i: (jnp.maximum(i * (tm // HALO) - 1, 0), CONV_COL0 + c)),
                  pl.BlockSpec((DN_CONV, cw), lambda c, i: (0, c)),
                  pl.BlockSpec((tm, cw), lambda c, i: (i, c)),
                  pl.BlockSpec((HALO, cw), lambda c, i: (jnp.minimum((i + 1) * (tm // HALO), tp // HALO - 1), c)),
                  pl.BlockSpec(memory_space=pl.ANY)],
        out_specs=(pl.BlockSpec((tm, cw), lambda c, i: (i, CONV_COL0 + c)), pl.BlockSpec((DN_CONV, cw), lambda c, i: (0, c))),
        out_shape=(jax.ShapeDtypeStruct(dproj.shape, dproj.dtype), jax.ShapeDtypeStruct((DN_CONV, ncol * cw), F32)),
        scratch_shapes=[pltpu.VMEM((tm + HALO, cw), F32), pltpu.VMEM((tm + HALO, cw), F32)],
        input_output_aliases={5: 0},
    )(proj, proj, conv_w, dy, dy, dproj)


def _stack(parts):
    return jnp.concatenate([p[None] for p in parts], axis=0)


def _chunk_masks():
    r = lax.broadcasted_iota(jnp.int32, (CHUNK, CHUNK), 0)
    c = lax.broadcasted_iota(jnp.int32, (CHUNK, CHUNK), 1)
    return (r >= c).astype(F32), (r > c).astype(F32), (r == c).astype(F32)


def _dn_chunk(y, z, small, s, a_log, dt_bias, norm_w, rows, known_inv=None):
    tri_incl, tri_strict, eye = _chunk_masks()
    lane = lax.broadcasted_iota(jnp.int32, (1, LANE), 1)
    dk = DN_HEAD_DIM
    nh = DN_HEADS
    heads = lambda t, first: _stack([t[:, (first + h) * dk:(first + h + 1) * dk] for h in range(nh)])
    pick = lambda t, l: jnp.sum(jnp.where(lane == l, t, 0.0), axis=1, keepdims=True)
    q = _l2n(_silu(heads(y, 0))) * dk ** -0.5
    k = _l2n(_silu(heads(y, nh)))
    v = _silu(heads(y, 2 * nh))
    g_all = jnp.where(lane < nh, -jnp.exp(a_log) * _softplus(small + dt_bias), 0.0) * rows
    beta_all = _sigmoid(small)
    gc_all = _cumsum_rows(g_all)
    g_sum = jnp.sum(g_all, axis=0, keepdims=True)
    gc = _stack([pick(gc_all, h) for h in range(nh)])
    beta = _stack([pick(beta_all, nh + h) for h in range(nh)])
    g_last = _stack([pick(g_sum, h) for h in range(nh)])
    gc_row = jnp.sum(eye * gc, axis=1, keepdims=True)
    gamma = jnp.exp((gc - gc_row) * tri_incl) * tri_incl
    k_beta = k * beta
    v_beta = v * beta
    a = _bmm_nt(k_beta, k) * gamma * tri_strict
    if known_inv is None:
        inv = eye - a
        power = a
        for _ in range(5):
            power = _bmm3(power, power)
            inv = inv + _bmm3(inv, power)
    else:
        inv = _known_inverse(a, known_inv)
    e_gc = jnp.exp(gc)
    uw = _bmm3(inv, jnp.concatenate([v_beta, k_beta * e_gc], axis=2))
    u, w = uw[:, :, :dk], uw[:, :, dk:]
    attn = _bmm_nt(q, k) * gamma
    q_dec = q * e_gc
    k_dec = k * jnp.exp(g_last - gc)
    v_new = u - _bmm(w, s)
    o = _bmm(q_dec, s) + _bmm(attn, v_new)
    s_new = s * jnp.exp(g_last) + _bmm_tn(k_dec, v_new)
    out = _rms(o, norm_w) * _silu(heads(z, 0))
    return jnp.concatenate([out[h] for h in range(nh)], axis=1), s_new, inv


Z_COL = 5
SMALL_COL = 24


def _chunk_rows(n):
    row = n * CHUNK + lax.broadcasted_iota(jnp.int32, (CHUNK, 1), 0)
    return (row >= PAD).astype(F32)


def dn_fwd(y, proj, a_log, dt_bias, norm_w, *, name):
    tp = y.shape[0]
    nc = tp // CHUNK
    dk = DN_HEAD_DIM
    per = CHUNKS_PER_STEP
    rows = per * CHUNK

    def body(y_ref, z_ref, small_ref, al_ref, dt_ref, nw_ref, o_ref, ssave_ref, isave_ref, state):
        n = pl.program_id(0)

        @pl.when(n == 0)
        def _():
            state[...] = jnp.zeros_like(state)

        s = state[...]
        for c in range(per):
            at = pl.ds(c * CHUNK, CHUNK)
            ssave_ref[c] = s
            out, s, inv = _dn_chunk(y_ref[at, :], z_ref[at, :], small_ref[at, :], s, al_ref[...], dt_ref[...],
                                    nw_ref[...], _chunk_rows(per * n + c))
            o_ref[at, :] = out
            isave_ref[c] = inv
        state[...] = s

    return pl.pallas_call(
        body, name=name, grid=(nc // per,),
        in_specs=[pl.BlockSpec((rows, y.shape[1]), lambda n: (n, 0)),
                  pl.BlockSpec((rows, 4 * LANE), lambda n: (n, Z_COL)),
                  pl.BlockSpec((rows, LANE), lambda n: (n, SMALL_COL)),
                  _full((1, LANE)), _full((1, LANE)), _full((1, LANE))],
        out_specs=(pl.BlockSpec((rows, 4 * LANE), lambda n: (n, 0)),
                   pl.BlockSpec((per, DN_HEADS, dk, dk), lambda n: (n, 0, 0, 0)),
                   pl.BlockSpec((per, DN_HEADS, CHUNK, CHUNK), lambda n: (n, 0, 0, 0))),
        out_shape=(jax.ShapeDtypeStruct((tp, 4 * LANE), F32), jax.ShapeDtypeStruct((nc, DN_HEADS, dk, dk), F32),
                   jax.ShapeDtypeStruct((nc, DN_HEADS, CHUNK, CHUNK), F32)),
        scratch_shapes=[pltpu.VMEM((DN_HEADS, dk, dk), F32)],
    )(y, proj, proj, a_log, dt_bias, norm_w)


def dn_bwd(y, proj, a_log, dt_bias, norm_w, ssave, isave, do, dproj, *, name):
    tp = y.shape[0]
    nc = tp // CHUNK
    dk = DN_HEAD_DIM
    per = CHUNKS_PER_STEP
    rev = lambda i: nc // per - 1 - i
    zs_width = 5 * LANE

    def body(y_ref, z_ref, small_ref, al_ref, dt_ref, nw_ref, ss_ref, is_ref, do_ref, _,
             dy_ref, dzs_ref, dal_ref, ddt_ref, dnw_ref, dstate):
        i = pl.program_id(0)
        n = nc // per - 1 - i

        @pl.when(i == 0)
        def _():
            dstate[...] = jnp.zeros_like(dstate)
            dal_ref[...] = jnp.zeros_like(dal_ref)
            ddt_ref[...] = jnp.zeros_like(ddt_ref)
            dnw_ref[...] = jnp.zeros_like(dnw_ref)

        ds = dstate[...]
        for c in reversed(range(per)):
            at = pl.ds(c * CHUNK, CHUNK)
            token_rows = _chunk_rows(per * n + c)
            known_inv = is_ref[c]
            fn = lambda *a: _dn_chunk(*a, token_rows, known_inv)[:2]
            _, vjp = jax.vjp(fn, y_ref[at, :], z_ref[at, :], small_ref[at, :], ss_ref[c], al_ref[...], dt_ref[...],
                             nw_ref[...])
            dy, dz, dsmall, ds, dal, ddt, dnw = vjp((do_ref[at, :], ds))
            dy_ref[at, :] = dy
            dzs_ref[at, :4 * LANE] = dz.astype(BF16)
            dzs_ref[at, 4 * LANE:] = dsmall.astype(BF16)
            dal_ref[...] += dal
            ddt_ref[...] += ddt
            dnw_ref[...] += dnw
        dstate[...] = ds

    rows = per * CHUNK
    return pl.pallas_call(
        body, name=name, grid=(nc // per,),
        in_specs=[pl.BlockSpec((rows, y.shape[1]), lambda i: (rev(i), 0)),
                  pl.BlockSpec((rows, 4 * LANE), lambda i: (rev(i), Z_COL)),
                  pl.BlockSpec((rows, LANE), lambda i: (rev(i), SMALL_COL)),
                  _full((1, LANE)), _full((1, LANE)), _full((1, LANE)),
                  pl.BlockSpec((per, DN_HEADS, dk, dk), lambda i: (rev(i), 0, 0, 0)),
                  pl.BlockSpec((per, DN_HEADS, CHUNK, CHUNK), lambda i: (rev(i), 0, 0, 0)),
                  pl.BlockSpec((rows, 4 * LANE), lambda i: (rev(i), 1)),
                  pl.BlockSpec(memory_space=pl.ANY)],
        out_specs=(pl.BlockSpec((rows, y.shape[1]), lambda i: (rev(i), 0)),
                   pl.BlockSpec((rows, zs_width), lambda i: (rev(i), Z_COL * 4 * LANE // zs_width)),
                   _full((1, LANE)), _full((1, LANE)), _full((1, LANE))),
        out_shape=(jax.ShapeDtypeStruct((tp, y.shape[1]), F32), jax.ShapeDtypeStruct(dproj.shape, dproj.dtype),
                   jax.ShapeDtypeStruct((1, LANE), F32), jax.ShapeDtypeStruct((1, LANE), F32),
                   jax.ShapeDtypeStruct((1, LANE), F32)),
        scratch_shapes=[pltpu.VMEM((DN_HEADS, dk, dk), F32)],
        input_output_aliases={9: 1},
    )(y, proj, proj, a_log, dt_bias, norm_w, ssave, isave, do, dproj)


def _gla_chunk(q, k, v, gate, low, s, w_gate_up, b_gate, norm_w, rows):
    tri_incl, _, _ = _chunk_masks()
    dk, dv, nh = GLA_DK, GLA_DV, GLA_HEADS
    heads = lambda t, width: _stack([t[:, h * width:(h + 1) * width] for h in range(nh)])
    logit = _mm3(low, w_gate_up) + b_gate
    glog_all = -_softplus(-logit) * (1.0 / GLA_GATE_NORM) * rows
    glog = heads(glog_all, dk)
    bcum = heads(_cumsum_rows(glog_all), dk)
    qh = heads(q, dk) * dk ** -0.5
    kh = heads(k, dk)
    vh = heads(v, dv)
    q_dec = qh * jnp.exp(bcum)
    attn = _bmm_nt(q_dec, kh * jnp.exp(-bcum)) * tri_incl
    b_last = jnp.sum(glog, axis=1, keepdims=True)
    k_dec = kh * jnp.exp(b_last - bcum)
    r = lax.broadcasted_iota(jnp.int32, (dk, dk), 0)
    c = lax.broadcasted_iota(jnp.int32, (dk, dk), 1)
    b_last_col = jnp.sum((r == c).astype(F32) * b_last, axis=2, keepdims=True)
    o = _bmm(attn, vh) + _bmm(q_dec, s)
    s_new = s * jnp.exp(b_last_col) + _bmm_tn(k_dec, vh)
    out = _rms(o, norm_w) * _silu(heads(gate, dv))
    return jnp.concatenate([out[h] for h in range(nh)], axis=1), s_new


LOW_COL = 24


def _gla_in_specs(step, rows=CHUNK):
    return [pl.BlockSpec((rows, 4 * LANE), lambda i: (step(i), 0)),
            pl.BlockSpec((rows, 4 * LANE), lambda i: (step(i), 1)),
            pl.BlockSpec((rows, 8 * LANE), lambda i: (step(i), 1)),
            pl.BlockSpec((rows, 8 * LANE), lambda i: (step(i), 2)),
            pl.BlockSpec((rows, LANE), lambda i: (step(i), LOW_COL)),
            _full((LANE, 4 * LANE)), _full((1, 4 * LANE)), _full((1, GLA_DV))]


def gla_fwd(proj, w_gate_up, b_gate, norm_w, *, name):
    tp = proj.shape[0]
    nc = tp // CHUNK
    per = CHUNKS_PER_STEP
    rows = per * CHUNK

    def body(q_ref, k_ref, v_ref, g_ref, low_ref, wgu_ref, bg_ref, nw_ref, o_ref, ssave_ref, state):
        n = pl.program_id(0)

        @pl.when(n == 0)
        def _():
            state[...] = jnp.zeros_like(state)

        s = state[...]
        for c in range(per):
            at = pl.ds(c * CHUNK, CHUNK)
            ssave_ref[c] = s
            out, s = _gla_chunk(q_ref[at, :], k_ref[at, :], v_ref[at, :], g_ref[at, :], low_ref[at, :], s, wgu_ref[...],
                                bg_ref[...], nw_ref[...], _chunk_rows(per * n + c))
            o_ref[at, :] = out
        state[...] = s

    return pl.pallas_call(
        body, name=name, grid=(nc // per,),
        in_specs=_gla_in_specs(lambda i: i, rows),
        out_specs=(pl.BlockSpec((rows, 8 * LANE), lambda n: (n, 0)),
                   pl.BlockSpec((per, GLA_HEADS, GLA_DK, GLA_DV), lambda n: (n, 0, 0, 0))),
        out_shape=(jax.ShapeDtypeStruct((tp, 8 * LANE), F32),
                   jax.ShapeDtypeStruct((nc, GLA_HEADS, GLA_DK, GLA_DV), F32)),
        scratch_shapes=[pltpu.VMEM((GLA_HEADS, GLA_DK, GLA_DV), F32)],
    )(proj, proj, proj, proj, proj, w_gate_up, b_gate, norm_w)


def gla_bwd(proj, w_gate_up, b_gate, norm_w, ssave, do, *, name):
    tp = proj.shape[0]
    nc = tp // CHUNK
    per = CHUNKS_PER_STEP
    rev = lambda i: nc // per - 1 - i

    def body(q_ref, k_ref, v_ref, g_ref, low_ref, wgu_ref, bg_ref, nw_ref, ss_ref, do_ref,
             dproj_ref, dwgu_ref, dbg_ref, dnw_ref, dstate):
        i = pl.program_id(0)
        n = nc // per - 1 - i

        @pl.when(i == 0)
        def _():
            dstate[...] = jnp.zeros_like(dstate)
            dwgu_ref[...] = jnp.zeros_like(dwgu_ref)
            dbg_ref[...] = jnp.zeros_like(dbg_ref)
            dnw_ref[...] = jnp.zeros_like(dnw_ref)

        ds = dstate[...]
        for c in reversed(range(per)):
            at = pl.ds(c * CHUNK, CHUNK)
            token_rows = _chunk_rows(per * n + c)
            fn = lambda *a: _gla_chunk(*a, token_rows)
            _, vjp = jax.vjp(fn, q_ref[at, :], k_ref[at, :], v_ref[at, :], g_ref[at, :], low_ref[at, :], ss_ref[c],
                             wgu_ref[...], bg_ref[...], nw_ref[...])
            dq, dk, dv, dg, dlow, ds, dwgu, dbg, dnw = vjp((do_ref[at, :], ds))
            off = 0
            for part in (dq, dk, dv, dg, dlow):
                dproj_ref[at, off:off + part.shape[1]] = part.astype(BF16)
                off += part.shape[1]
            dwgu_ref[...] += dwgu
            dbg_ref[...] += dbg
            dnw_ref[...] += dnw
        dstate[...] = ds

    chunk = lambda width: pl.BlockSpec((per * CHUNK, width), lambda i: (rev(i), 0))
    return pl.pallas_call(
        body, name=name, grid=(nc // per,),
        in_specs=_gla_in_specs(rev, per * CHUNK) + [pl.BlockSpec((per, GLA_HEADS, GLA_DK, GLA_DV), lambda i: (rev(i), 0, 0, 0)),
                                                    chunk(8 * LANE)],
        out_specs=(chunk(PROJ_DIM), _full((LANE, 4 * LANE)), _full((1, 4 * LANE)), _full((1, GLA_DV))),
        out_shape=(jax.ShapeDtypeStruct((tp, PROJ_DIM), BF16), jax.ShapeDtypeStruct((LANE, 4 * LANE), F32),
                   jax.ShapeDtypeStruct((1, 4 * LANE), F32), jax.ShapeDtypeStruct((1, GLA_DV), F32)),
        scratch_shapes=[pltpu.VMEM((GLA_HEADS, GLA_DK, GLA_DV), F32)],
    )(proj, proj, proj, proj, proj, w_gate_up, b_gate, norm_w, ssave, do)


def _even_proj_weight(w_t):
    hd = SWA_HEAD_DIM
    k0, k1 = w_t[512:512 + hd], w_t[512 + hd:640]
    v0, v1 = w_t[640:640 + hd], w_t[640 + hd:768]
    zeros = jnp.zeros((LANE - 2 * DN_HEADS, w_t.shape[1]), w_t.dtype)
    return jnp.concatenate([w_t[:512], k0, k0, k1, k1, v0, v0, v1, v1, w_t[768:2816], w_t[2820:2824], w_t[2816:2820],
                            zeros], axis=0)


def _even_proj_weight_grad(dw):
    hd = SWA_HEAD_DIM
    c = lambda i: dw[512 + i * hd:512 + (i + 1) * hd]
    return jnp.concatenate([dw[:512], c(0) + c(1), c(2) + c(3), c(4) + c(5), c(6) + c(7), dw[1024:3072],
                            dw[3076:3080], dw[3072:3076]], axis=0)


def _ffn_fwd(h, nw_in, nw_out, wts, idx, get_w):
    wts.update(get_w(f"ffn{idx}", h))
    w_gu = wts[f"w_gu{idx}"]
    hn, g, u, a = rms_mm(h, nw_in, w_gu[0], swiglu=True, name=f"ffn_up_{idx}", widx=w_gu[1])
    wts.update(get_w(f"down{idx}", a))
    w_down = wts[f"w_down{idx}"]
    f, h_out = mm_rms_res([a], w_down[0], h, nw_out, scale=0.5, name=f"ffn_down_{idx}", widx=w_down[1])
    return h_out, (h, hn, g, u, a, f)


def _ffn_bwd(dho, saved, nw_in, nw_out, w_gu, w_down, idx, on_grads):
    h, hn, g, u, a, f = saved
    df, dnw_out, dgu = mm_rms_res_bwd(dho, f, nw_out, w_down[0], (g, u), scale=0.5, name=f"ffn_down_bwd_{idx}",
                                      widx=w_down[1])
    g_down = mm_tn(a, df, name=f"ffn_dwd_{idx}", out_dtype=BF16)
    sent = on_grads("down", g_down)
    g_gu = mm_tn(dgu, hn, name=f"ffn_dwgu_{idx}", out_dtype=BF16, after=sent)
    sent = on_grads("gu", g_gu)
    dh, dnw_in = rms_mm_bwd([dgu], w_gu[0], h, nw_in + sent, dho, name=f"ffn_up_bwd_{idx}", widx=w_gu[1])
    return dh, dnw_in, dnw_out


def local_step(x, target, wts, get_w=None, put_g=None):
    seq, d = x.shape
    wts = dict(wts)
    get_w = get_w or (lambda stage, after: {})
    put_g = put_g or (lambda stage, grads: jnp.zeros((1, 1), F32))
    row = lambda v: v.reshape(1, -1)
    lane_row = lambda v: jnp.pad(v.reshape(1, -1), ((0, 0), (0, LANE - v.size)))
    nw = wts["norm_w"]
    h = lax.dynamic_update_slice(jnp.concatenate([jnp.zeros((PAD + N_META, d), F32), x], axis=0), wts["meta_tokens"], (PAD, 0))
    buckets = _swa_buckets()
    bias = swa_bias(wts["rel_bias_table"], buckets, name="swa_bias")
    sinks = lane_row(wts["swa_sinks"])
    a_log, dt_bias = lane_row(wts["dn_a_log"]), lane_row(wts["dn_dt_bias"])
    dn_norm_w = row(wts["dn_norm_w"])
    conv_w = wts["even_conv_w"][0]
    w_gate_up = jnp.pad(wts["gla_w_gate_up"][0], ((0, LANE - GLA_GATE_RANK), (0, 0)))
    b_gate, gla_norm_w = row(wts["gla_b_gate"]), row(wts["gla_norm_w"])

    saved = []
    w_in, w_out = [None, None], [None, None]
    for l in range(2):
        h, s_a = _ffn_fwd(h, row(nw[l, 0]), row(nw[l, 1]), wts, 2 * l, get_w)
        if l == 0:
            wts.update(get_w("even", h))
            w_in[0], w_out[0] = _even_proj_weight(wts["even_w_in"]), wts["even_w_out"]
        else:
            wts.update(get_w("odd", h))
            w_in[1] = jnp.pad(wts["odd_w_in"], ((0, PROJ_DIM - wts["odd_w_in"].shape[0]), (0, 0)))
            w_out[1] = wts["odd_w_out"]
        h_mix = h
        hn, proj = rms_mm(h, row(nw[l, 2]), w_in[l], swiglu=False, name=f"mix_in_{l}")
        if l == 0:
            o_a = swa_fwd(proj, bias, sinks, name="swa_fwd")
            y = conv_fwd(proj, conv_w, name="conv_fwd")
            o_b, ssave, isave = dn_fwd(y, proj, a_log, dt_bias, dn_norm_w, name="dn_fwd")
            acts, extra = [o_a, o_b], (y, ssave, isave)
        else:
            o, ssave = gla_fwd(proj, w_gate_up, b_gate, gla_norm_w, name="gla_fwd")
            acts, extra = [o], (ssave,)
        mix, h = mm_rms_res(acts, w_out[l], h, row(nw[l, 3]), scale=1.0, name=f"mix_out_{l}")
        s_m = (h_mix, hn, proj, acts, extra, mix)
        h, s_b = _ffn_fwd(h, row(nw[l, 4]), row(nw[l, 5]), wts, 2 * l + 1, get_w)
        saved.append((s_a, s_m, s_b))

    dh, loss = loss_and_grad(h, target, name="loss")

    grads = {}
    dnw = [[None] * 6 for _ in range(2)]
    def on_grads(i):
        def put(which, g):
            grads[f"g_{which}{i}"] = g
            return put_g(f"{which}{i}", grads)
        return put

    for l in (1, 0):
        s_a, s_m, s_b = saved[l]
        i = 2 * l + 1
        dh, dnw[l][4], dnw[l][5] = _ffn_bwd(dh, s_b, row(nw[l, 4]), row(nw[l, 5]), wts[f"w_gu{i}"], wts[f"w_down{i}"],
                                            i, on_grads(i))
        h_mix, hn, proj, acts, extra, mix = s_m
        dmix, dnw[l][3], do = mm_rms_res_bwd(dh, mix, row(nw[l, 3]), w_out[l], None, scale=1.0, name=f"mix_out_bwd_{l}")
        dw_out = jnp.concatenate([mm_tn(a, dmix, name=f"mix_dwo_{l}_{i}") for i, a in enumerate(acts)], axis=0)
        sent = jnp.zeros((1, 1), F32)
        if l == 0:
            y, ssave, isave = extra
            dproj, dbias, dsinks = swa_bwd(proj, bias, sinks, do, name="swa_bwd")
            dy, dproj, da_log, ddt_bias, ddn_norm_w = dn_bwd(y, proj, a_log, dt_bias, dn_norm_w, ssave, isave, do, dproj,
                                                             name="dn_bwd")
            dproj, dconv_w = conv_bwd(proj, conv_w, dy, dproj, name="conv_bwd")
            grads["rel_bias_table"] = swa_bias_bwd(dbias, buckets, name="swa_bias_bwd")[:, :SWA_Q_HEADS]
            grads["swa_sinks"] = dsinks[:, :SWA_Q_HEADS]
            grads["dn_a_log"] = da_log[:, :DN_HEADS]
            grads["dn_dt_bias"] = ddt_bias[:, :DN_HEADS]
            grads["dn_norm_w"] = ddn_norm_w
            grads["even_conv_w"] = dconv_w[None]
            grads["even_w_out"] = dw_out
        else:
            (ssave,) = extra
            dproj, dwgu, dbg, dgnw = gla_bwd(proj, w_gate_up, b_gate, gla_norm_w, ssave, do, name="gla_bwd")
            grads["gla_w_gate_up"] = dwgu[None, :GLA_GATE_RANK]
            grads["gla_b_gate"] = dbg
            grads["gla_norm_w"] = dgnw
            grads["odd_w_out"] = dw_out
        dw_in = mm_tn(dproj, hn, name=f"mix_dwi_{l}")
        if l == 0:
            grads["even_w_in"] = _even_proj_weight_grad(dw_in)
            sent = put_g("even", grads)
        else:
            grads["odd_w_in"] = dw_in[:wts["odd_w_in"].shape[0]]
        dh, dnw[l][2] = rms_mm_bwd([dproj], w_in[l], h_mix, row(nw[l, 2]) + sent, dh, name=f"mix_in_bwd_{l}")
        i = 2 * l
        dh, dnw[l][0], dnw[l][1] = _ffn_bwd(dh, s_a, row(nw[l, 0]), row(nw[l, 1]), wts[f"w_gu{i}"], wts[f"w_down{i}"],
                                            i, on_grads(i))

    grads["norm_w"] = jnp.stack([jnp.concatenate(r, axis=0) for r in dnw])
    grads["meta_tokens"] = dh[PAD:PAD + N_META]
    return loss[0, 0], dh[PAD + N_META:], grads


def _peer(k):
    x, y, c = (lax.axis_index(a) for a in AXES)
    flip = lambda v, bit: 1 - v if bit else v
    return (flip(x, k & 4), flip(y, k & 2), flip(c, k & 1))


def _my_index():
    x, y, c = (lax.axis_index(a) for a in AXES)
    return 4 * x + 2 * y + c


_HBM = pl.BlockSpec(memory_space=pltpu.HBM)
_SEM = pl.BlockSpec(memory_space=pltpu.SEMAPHORE)
_EFFECT = pltpu.SideEffectType.DATAFLOW_SIDE_EFFECTING


def _remote_copies(items, src_refs, land_refs, send_sems, recv_sems):
    me = _my_index()
    copies = []
    for k in range(1, N_DEV):
        px, py, pc = _peer(k)
        pj = 4 * px + 2 * py + pc
        for a, (sn, send, ln, land, _) in enumerate(items):
            sem = (k - 1) * len(items) + a
            copies.append(pltpu.make_async_remote_copy(
                src_ref=send(src_refs[sn], pj), dst_ref=land(land_refs[ln], me), send_sem=send_sems.at[sem],
                recv_sem=recv_sems.at[sem], device_id=(px, py, pc), device_id_type=MESH))
    return copies


def exchange(srcs, lands, items, after, *, name):
    sn, ln = list(srcs), list(lands)

    def body(*refs):
        src_refs = dict(zip(sn, refs[:len(sn)]))
        land_refs = dict(zip(ln, refs[len(sn) + len(ln) + 1:len(sn) + 2 * len(ln) + 1]))
        send_sems, recv_sems = refs[len(sn) + 2 * len(ln) + 1:]
        copies = _remote_copies(items, src_refs, land_refs, send_sems, recv_sems)
        for cp in copies:
            cp.start()
        for cp in copies:
            cp.wait_recv()
        for cp in copies:
            cp.wait_send()

    n_remote = (N_DEV - 1) * len(items)
    outs = pl.pallas_call(
        body, name=name,
        in_specs=[pl.BlockSpec(memory_space=pl.ANY)] * (len(sn) + len(ln) + 1),
        out_specs=tuple(pl.BlockSpec(memory_space=pl.ANY) for _ in ln),
        out_shape=tuple(jax.ShapeDtypeStruct(lands[n].shape, lands[n].dtype) for n in ln),
        input_output_aliases={len(sn) + i: i for i in range(len(ln))},
        scratch_shapes=[pltpu.SemaphoreType.DMA((n_remote,)), pltpu.SemaphoreType.DMA((n_remote,))],
    )(*[srcs[n] for n in sn], *[lands[n] for n in ln], after)
    return dict(zip(ln, outs))


def start_copies(srcs, lands, items, *, name):
    sn, ln = list(srcs), list(lands)
    n_remote = (N_DEV - 1) * len(items)

    def body(*refs):
        src_refs = dict(zip(sn, refs[:len(sn)]))
        land_refs = dict(zip(ln, refs[len(sn):len(sn) + len(ln)]))
        send_sems, recv_sems = refs[len(sn) + len(ln):len(sn) + len(ln) + 2]
        token = refs[-1]
        for cp in _remote_copies(items, src_refs, land_refs, send_sems, recv_sems):
            cp.start()
        token[...] = jnp.zeros_like(token)

    hbm = lambda a: pltpu.with_memory_space_constraint(a, pltpu.HBM)
    outs = pl.pallas_call(
        body, name=name,
        in_specs=[_HBM] * (len(sn) + len(ln)),
        out_specs=(_SEM, _SEM) + (_HBM,) * len(ln) + (pl.BlockSpec(memory_space=pltpu.VMEM),),
        out_shape=(pltpu.SemaphoreType.DMA((n_remote,)), pltpu.SemaphoreType.DMA((n_remote,)))
        + tuple(pltpu.HBM(lands[n].shape, lands[n].dtype) for n in ln) + (jax.ShapeDtypeStruct((8, LANE), F32),),
        input_output_aliases={len(sn) + i: 2 + i for i in range(len(ln))},
        compiler_params=pltpu.CompilerParams(has_side_effects=_EFFECT),
    )(*[hbm(srcs[n]) for n in sn], *[hbm(lands[n]) for n in ln])
    return (outs[0], outs[1]), dict(zip(ln, outs[2:2 + len(ln)])), outs[-1][0:1, 0:1]


def wait_copies(sems, srcs, lands, items, after, *, name):
    sn, ln = list(srcs), list(lands)

    def body(*refs):
        src_refs = dict(zip(sn, refs[:len(sn)]))
        land_refs = dict(zip(ln, refs[len(sn):len(sn) + len(ln)]))
        send_sems, recv_sems = refs[len(sn) + len(ln):len(sn) + len(ln) + 2]
        copies = _remote_copies(items, src_refs, land_refs, send_sems, recv_sems)
        for cp in copies:
            cp.wait_send()
        for cp in copies:
            cp.wait_recv()

    outs = pl.pallas_call(
        body, name=name,
        in_specs=[_HBM] * (len(sn) + len(ln)) + [_SEM, _SEM, pl.BlockSpec(memory_space=pl.ANY)],
        out_specs=(_HBM,) * len(ln),
        out_shape=tuple(pltpu.HBM(lands[n].shape, lands[n].dtype) for n in ln),
        input_output_aliases={len(sn) + i: i for i in range(len(ln))},
        compiler_params=pltpu.CompilerParams(has_side_effects=_EFFECT),
    )(*[srcs[n] for n in sn], *[lands[n] for n in ln], sems[0], sems[1], after)
    return dict(zip(ln, outs))


def _block(index, size, base=0):
    return pl.ds(pl.multiple_of(base + index * size, ROW_TILE), size)


def _adam_tile(rows):
    for t in (256, 176, 128):
        if rows % t == 0:
            return t
    return rows


def sum_adamw(recvs, w, m, v, *, name, first_slab=0, into=None):
    _, r, c = w.shape
    b = len(recvs)
    rp = recvs[0].shape[1]
    whole = r % ROW_TILE != 0
    tr = r if whole else _adam_tile(r)
    c1 = 1.0 / (1.0 - ADAM_B1 ** ADAM_STEP)
    c2 = 1.0 / (1.0 - ADAM_B2 ** ADAM_STEP)

    def body(*refs):
        recv_refs = refs[:b]
        w_ref, m_ref, v_ref = refs[b:b + 3]
        g_ref, d_ref, nm_ref, nv_ref = refs[b + 3 + (0 if into is None else 4):][:4]
        for slab, recv_ref in enumerate(recv_refs):
            @pl.when(pl.program_id(0) == slab)
            def _():
                g = recv_ref[0].astype(F32)
                for i in range(1, N_DEV):
                    g = g + recv_ref[i].astype(F32)
                if whole:
                    sum_ref = refs[-1]
                    sum_ref[...] = g
                    g = sum_ref[0:r, :]
                nm = ADAM_B1 * m_ref[0] + (1.0 - ADAM_B1) * g
                nv = ADAM_B2 * v_ref[0] + (1.0 - ADAM_B2) * (g * g)
                g_ref[0] = g
                nm_ref[0] = nm
                nv_ref[0] = nv
                d_ref[0] = -ADAM_LR * ((nm * c1) / (jnp.sqrt(nv * c2) + ADAM_EPS) + ADAM_WD * w_ref[0])

    tile = pl.BlockSpec((1, tr, c), lambda bi, i: (first_slab + bi, i, 0))
    piece = lambda slab: pl.BlockSpec((N_DEV, rp if whole else tr, c), lambda bi, i: (0, jnp.where(bi == slab, i, 0), 0))
    earlier = [] if into is None else list(into)
    return pl.pallas_call(
        body, name=name, grid=(b, r // tr),
        in_specs=[piece(slab) for slab in range(b)] + [tile, tile, tile] + [pl.BlockSpec(memory_space=pl.ANY)] * len(earlier),
        out_specs=(tile,) * 4, out_shape=(jax.ShapeDtypeStruct(w.shape, F32),) * 4,
        input_output_aliases={b + 3 + i: i for i in range(len(earlier))},
        scratch_shapes=[pltpu.VMEM((rp, c), F32)] if whole else [],
    )(*recvs, w, m, v, *earlier)


def _flat_rows(n_elems, row_multiple):
    rows = -(-n_elems // FLAT_COLS)
    return -(-rows // row_multiple) * row_multiple


def _pack(arrays, row_multiple, dtype):
    flat = jnp.concatenate([a.reshape(-1).astype(dtype) for a in arrays])
    rows = _flat_rows(flat.size, row_multiple)
    return jnp.pad(flat, (0, rows * FLAT_COLS - flat.size)).reshape(rows, FLAT_COLS)


def _unpack(flat2d, shapes):
    lead = flat2d.shape[:-2]
    flat = flat2d.reshape(lead + (-1,))
    out, off = [], 0
    for shp in shapes:
        n = int(np.prod(shp))
        out.append(flat[..., off:off + n].reshape(lead + tuple(shp)))
        off += n
    return out


def _join_shards(stacked, axis):
    moved = jnp.moveaxis(stacked, 0, axis)
    shp = list(moved.shape)
    shp[axis:axis + 2] = [shp[axis] * shp[axis + 1]]
    return moved.reshape(shp)


def _split_shards(full, axis):
    shp = list(full.shape)
    shp[axis:axis + 1] = [N_DEV, shp[axis] // N_DEV]
    return jnp.moveaxis(full.reshape(shp), axis, 0)


def kernel(x, meta_tokens, norm_w, ffn_w_gate, ffn_w_up, ffn_w_down, rel_bias_table, even_w_in, even_conv_w, swa_sinks, dn_a_log, dn_dt_bias, dn_norm_w, even_w_out, odd_w_in, gla_w_gate_up, gla_b_gate, gla_norm_w, odd_w_out, loss_target, m_meta_tokens, m_norm_w, m_ffn_w_gate, m_ffn_w_up, m_ffn_w_down, m_rel_bias_table, m_even_w_in, m_even_conv_w, m_swa_sinks, m_dn_a_log, m_dn_dt_bias, m_dn_norm_w, m_even_w_out, m_odd_w_in, m_gla_w_gate_up, m_gla_b_gate, m_gla_norm_w, m_odd_w_out, v_meta_tokens, v_norm_w, v_ffn_w_gate, v_ffn_w_up, v_ffn_w_down, v_rel_bias_table, v_even_w_in, v_even_conv_w, v_swa_sinks, v_dn_a_log, v_dn_dt_bias, v_dn_norm_w, v_even_w_out, v_odd_w_in, v_gla_w_gate_up, v_gla_b_gate, v_gla_norm_w, v_odd_w_out):
    args = locals()
    w = {n: args[n] for n in WEIGHTS}
    m = {n: args["m_" + n] for n in WEIGHTS}
    v = {n: args["v_" + n] for n in WEIGHTS}

    d = D_MODEL
    me = _my_index()
    whole = lambda ref, j: ref
    rows = lambda size, base=0: (lambda ref, i: ref.at[_block(i, size, base), :])
    lead = lambda ref, i: ref.at[i]
    of_group = lambda items, g: [it for it in items if it[4] == g]
    names = lambda items, k: list(dict.fromkeys(it[k] for it in items))

    def placed(shape, dtype, parts):
        land = lax.empty(shape, dtype)
        for part, axis, start in parts:
            land = lax.dynamic_update_slice(land, part, tuple(start if a == axis else 0 for a in range(land.ndim)))
        return land

    as_rows = lambda a: jnp.swapaxes(a, -1, -2)
    pad_rows = lambda a, to: jnp.pad(a, [(0, 0)] * (a.ndim - 2) + [(0, to - a.shape[-2]), (0, 0)])
    gate_s = pad_rows(as_rows(w["ffn_w_gate"].reshape(N_FFN, d, FF_SHARD)), FF_SHARD_PAD).astype(BF16)
    up_s = pad_rows(as_rows(w["ffn_w_up"].reshape(N_FFN, d, FF_SHARD)), FF_SHARD_PAD).astype(BF16)
    down_s = pad_rows(w["ffn_w_down"].reshape(N_FFN, FF_SHARD, d), FF_SHARD_PAD).astype(BF16)
    small_s = _pack([w[n] for n in SMALL], 8, F32)
    srcs_w = {"ein": pad_rows(as_rows(w["even_w_in"][0]), EVEN_IN_SHARD_PAD).astype(BF16),
              "oin": pad_rows(as_rows(w["odd_w_in"][0]), ODD_IN_SHARD_PAD).astype(BF16),
              "eout": w["even_w_out"][0].astype(BF16), "oout": w["odd_w_out"][0].astype(BF16), "small": small_s}
    lands_w = {"ein": placed((N_DEV * EVEN_IN_SHARD_PAD, d), BF16, [(srcs_w["ein"], 0, me * EVEN_IN_SHARD_PAD)]),
               "oin": placed((N_DEV * ODD_IN_SHARD_PAD, d), BF16, [(srcs_w["oin"], 0, me * ODD_IN_SHARD_PAD)]),
               "eout": placed((d, d), BF16, [(srcs_w["eout"], 0, me * OUT_SHARD)]),
               "oout": placed((d, d), BF16, [(srcs_w["oout"], 0, me * OUT_SHARD)]),
               "small": placed((N_DEV,) + small_s.shape, F32, [(small_s[None], 0, me)])}
    items_w = [("small", whole, "small", lead, "first"), ("ein", whole, "ein", rows(EVEN_IN_SHARD_PAD), "even"),
               ("eout", whole, "eout", rows(OUT_SHARD), "even"), ("oin", whole, "oin", rows(ODD_IN_SHARD_PAD), "odd"),
               ("oout", whole, "oout", rows(OUT_SHARD), "odd")]
    for i, group, down_group in ((0, "first", "down0"), (1, "ffn1", "ffn1"), (2, "ffn2", "down2"), (3, "ffn3", "ffn3")):
        srcs_w.update({f"gate{i}": gate_s[i], f"up{i}": up_s[i], f"down{i}": down_s[i]})
        lands_w[f"w_gu{i}"] = placed((2 * FF_PAD, d), BF16, [(srcs_w[f"gate{i}"], 0, me * FF_SHARD_PAD),
                                                             (srcs_w[f"up{i}"], 0, FF_PAD + me * FF_SHARD_PAD)])
        lands_w[f"w_down{i}"] = placed((FF_PAD, d), BF16, [(srcs_w[f"down{i}"], 0, me * FF_SHARD_PAD)])
        items_w += [(f"gate{i}", whole, f"w_gu{i}", rows(FF_SHARD_PAD), group),
                    (f"up{i}", whole, f"w_gu{i}", rows(FF_SHARD_PAD, FF_PAD), group),
                    (f"down{i}", whole, f"w_down{i}", rows(FF_SHARD_PAD), down_group)]
    pending, started = {}, []
    for g in ("first", "down0", "even", "ffn1", "ffn2", "down2", "odd", "ffn3"):
        its = of_group(items_w, g)
        srcs = {n: srcs_w[n] for n in names(its, 0)}
        sems, lands, token = start_copies(srcs, {n: lands_w[n] for n in names(its, 2)}, its, name=f"gather_start_{g}")
        pending[g] = (sems, srcs, lands, its)
        started.append(token)

    unpad = lambda p, shard, shard_pad: p.reshape(N_DEV, shard_pad, d)[:, :shard].reshape(N_DEV * shard, d)

    def get_w(stage, after):
        if stage not in pending:
            return {}
        sems, srcs, lands, its = pending[stage]
        landed = wait_copies(sems, srcs, lands, its, after, name=f"gather_wait_{stage}")
        got = {}
        for n, arr in landed.items():
            if n == "small":
                for sn, stacked in zip(SMALL, _unpack(arr, [w[sn].shape for sn in SMALL])):
                    got[sn] = _join_shards(stacked, SHARD_AXIS[sn])
            elif n == "ein":
                got["even_w_in"] = unpad(arr, EVEN_IN_SHARD, EVEN_IN_SHARD_PAD)
            elif n == "oin":
                got["odd_w_in"] = unpad(arr, ODD_IN_SHARD, ODD_IN_SHARD_PAD)
            elif n in ("eout", "oout"):
                got["even_w_out" if n == "eout" else "odd_w_out"] = arr
            else:
                got[n] = (arr, None)
        return got

    full = {n: w[n] for n in REPL}
    full.update(get_w("first", sum(started)))

    repad = lambda g, shard, shard_pad: pad_rows(g.reshape(N_DEV, shard, d), shard_pad)
    pieces_g = {"r_oin": ("oin", None, "gu2"), "r_oout": ("oout", (OUT_SHARD, 0), "gu2"),
                "r_ein": ("ein", None, "even"), "r_eout": ("eout", (OUT_SHARD, 0), "even"), "r_small": ("small", None, "last")}
    for i in range(N_FFN):
        pieces_g.update({f"r_gate{i}": (f"g_gu{i}", (FF_SHARD_PAD, 0), f"gu{i}"),
                         f"r_up{i}": (f"g_gu{i}", (FF_SHARD_PAD, FF_PAD), f"gu{i}"),
                         f"r_down{i}": (f"g_down{i}", (FF_SHARD_PAD, 0), "down0" if i == 0 else f"gu{i}")})
    items_g = [(src, lead if blk is None else rows(*blk), land, lead, group) for land, (src, blk, group) in pieces_g.items()]
    last_groups = ("down0", "gu0")

    def grad_src(n, grads):
        if n == "oin":
            return repad(grads["odd_w_in"], ODD_IN_SHARD, ODD_IN_SHARD_PAD).astype(BF16)
        if n == "ein":
            return repad(grads["even_w_in"], EVEN_IN_SHARD, EVEN_IN_SHARD_PAD).astype(BF16)
        if n in ("oout", "eout"):
            return grads["odd_w_out" if n == "oout" else "even_w_out"].astype(BF16)
        return grads[n]

    def grad_land(n, srcs):
        src, blk, _ = pieces_g[n]
        if blk is None:
            own = lax.dynamic_index_in_dim(srcs[src], me, 0, keepdims=False)
        else:
            own = lax.dynamic_slice_in_dim(srcs[src], blk[1] + me * blk[0], blk[0], 0)
        return placed((N_DEV,) + own.shape, own.dtype, [(own[None], 0, me)])

    sent = {}

    def put_g(stage, grads):
        its = of_group(items_g, stage)
        if not its:
            return jnp.zeros((1, 1), F32)
        srcs = {n: grad_src(n, grads) for n in names(its, 0)}
        lands = {n: grad_land(n, srcs) for n in names(its, 2)}
        sems, lands, token = start_copies(srcs, lands, its, name=f"grads_start_{stage}")
        sent[stage] = (sems, srcs, lands, its)
        return token

    loss, grad_x, grads = local_step(x[0], loss_target[0], full, get_w, put_g)
    loss = lax.psum(loss, AXES)

    order = SMALL + REPL
    pieces = [_split_shards(grads[n].reshape(full[n].shape), SHARD_AXIS[n]) if n in SHARD_AXIS
              else jnp.broadcast_to(grads[n].reshape(w[n].shape)[None], (N_DEV,) + w[n].shape) for n in order]
    flat = jnp.concatenate([p.reshape(N_DEV, -1) for p in pieces], axis=1)
    srows = _flat_rows(flat.shape[1], 8)
    grads["small"] = jnp.pad(flat, ((0, 0), (0, srows * FLAT_COLS - flat.shape[1]))).reshape(N_DEV, srows, FLAT_COLS)
    recv = {}
    for stage, (sems, srcs, lands, its) in sent.items():
        if stage not in last_groups:
            recv.update(wait_copies(sems, srcs, lands, its, grad_x, name=f"grads_wait_{stage}"))
    result = [{} for _ in range(4)]

    views = {"ffn_w_gate": (lambda a: as_rows(a.reshape(N_FFN, d, FF_SHARD)), lambda o, n: as_rows(o).reshape(w[n].shape)),
             "ffn_w_up": (lambda a: as_rows(a.reshape(N_FFN, d, FF_SHARD)), lambda o, n: as_rows(o).reshape(w[n].shape)),
             "ffn_w_down": (lambda a: a.reshape(N_FFN, FF_SHARD, d), lambda o, n: o.reshape(w[n].shape)),
             "even_w_in": (as_rows, lambda o, n: as_rows(o)), "odd_w_in": (as_rows, lambda o, n: as_rows(o)),
             "even_w_out": (lambda a: a, lambda o, n: o), "odd_w_out": (lambda a: a, lambda o, n: o)}

    def adam(n, recvs, first_slab=0, into=None):
        view = views[n][0]
        return sum_adamw(recvs, view(w[n]), view(m[n]), view(v[n]), name=f"adamw_{n}_{first_slab}",
                         first_slab=first_slab, into=into)

    def finish(n, outs):
        for r, o in zip(result, outs):
            r[n] = views[n][1](o, n)

    ffn_recv = (("ffn_w_gate", "r_gate"), ("ffn_w_up", "r_up"), ("ffn_w_down", "r_down"))
    early = {n: adam(n, [recv[f"{r}{i}"] for i in (1, 2, 3)], first_slab=1) for n, r in ffn_recv}
    for n, r in (("even_w_in", "r_ein"), ("odd_w_in", "r_oin"), ("even_w_out", "r_eout"), ("odd_w_out", "r_oout")):
        finish(n, adam(n, [recv[r]]))
    srcs = {"small": grads["small"]}
    recv.update(exchange(srcs, {"r_small": grad_land("r_small", srcs)}, of_group(items_g, "last"),
                         early["ffn_w_down"][0], name="exchange_small"))
    for stage in last_groups:
        sems, srcs, lands, its = sent[stage]
        recv.update(wait_copies(sems, srcs, lands, its, recv["r_small"], name=f"grads_wait_{stage}"))
    for n, r in ffn_recv:
        finish(n, adam(n, [recv[f"{r}0"]], into=early[n]))
    pack_local = lambda t: _pack([t[n] for n in order], 8, F32)[None]
    small_outs = sum_adamw([recv["r_small"]], pack_local(w), pack_local(m), pack_local(v), name="adamw_small")
    for r, o in zip(result, small_outs):
        r.update(zip(order, _unpack(o[0], [w[n].shape for n in order])))
    return (loss, grad_x[None], *[r[n] for r in result for n in WEIGHTS])
```

```python
import functools
import math

import numpy as np
import jax
import jax.numpy as jnp
from jax import lax
from jax.experimental import pallas as pl
from jax.experimental.pallas import tpu as pltpu

F32 = jnp.float32
BF16 = jnp.bfloat16
MESH = pl.DeviceIdType.MESH
AXES = ("x", "y", "c")
N_DEV = 8

D_MODEL = 1024
N_META = 16
D_FF = 2816
NORM_EPS = 1e-6
NEG_INF = -1e30
SWA_Q_HEADS = 8
SWA_HEAD_DIM = 64
SWA_WINDOW = 128
SWA_BLOCK = 128
REL_BUCKETS = 32
REL_MAX_DIST = 128
DN_HEADS = 4
DN_HEAD_DIM = 128
DN_CONV = 4
GLA_HEADS = 4
GLA_DK = 128
GLA_DV = 256
GLA_GATE_RANK = 16
GLA_GATE_NORM = 16.0
CHUNK = 64
CHUNKS_PER_STEP = 6
PAD = SWA_BLOCK - N_META
LANE = 128
PROJ_DIM = 3200

ADAM_LR = 0.001
ADAM_B1 = 0.9
ADAM_B2 = 0.999
ADAM_EPS = 1e-08
ADAM_WD = 0.01
ADAM_STEP = 10

ROW_TILE = 16
FF_SHARD = D_FF // N_DEV
FF_SHARD_PAD = -(-FF_SHARD // ROW_TILE) * ROW_TILE
FF_PAD = N_DEV * FF_SHARD_PAD
N_FFN = 4
EVEN_IN_SHARD, EVEN_IN_SHARD_PAD = 353, 368
ODD_IN_SHARD, ODD_IN_SHARD_PAD = 386, 400
OUT_SHARD = D_MODEL // N_DEV

FLAT_COLS = 128
BIG = ("ffn_w_gate", "ffn_w_up", "ffn_w_down", "even_w_in", "even_w_out", "odd_w_in", "odd_w_out")
SMALL = ("meta_tokens", "norm_w", "even_conv_w", "gla_w_gate_up", "gla_b_gate", "gla_norm_w")
REPL = ("rel_bias_table", "swa_sinks", "dn_a_log", "dn_dt_bias", "dn_norm_w")
WEIGHTS = ("meta_tokens", "norm_w", "ffn_w_gate", "ffn_w_up", "ffn_w_down", "rel_bias_table", "even_w_in",
           "even_conv_w", "swa_sinks", "dn_a_log", "dn_dt_bias", "dn_norm_w", "even_w_out", "odd_w_in",
           "gla_w_gate_up", "gla_b_gate", "gla_norm_w", "odd_w_out")
SHARD_AXIS = {"ffn_w_gate": 3, "ffn_w_up": 3, "ffn_w_down": 2, "even_w_in": 2, "even_w_out": 1, "odd_w_in": 2,
              "odd_w_out": 1, "meta_tokens": 1, "norm_w": 2, "even_conv_w": 2, "gla_w_gate_up": 2,
              "gla_b_gate": 1, "gla_norm_w": 1}


def _rms(x, w):
    r = lax.rsqrt(jnp.mean(x * x, axis=-1, keepdims=True) + NORM_EPS)
    return x * r * w


def _sigmoid(x):
    return 0.5 * (jnp.tanh(0.5 * x) + 1.0)


def _silu(x):
    return x * _sigmoid(x)


def _softplus(x):
    pos = x > 0
    return jnp.where(pos, x, 0.0) + jnp.log(1.0 + jnp.exp(jnp.where(pos, -x, x)))


def _l2n(x):
    return x * lax.rsqrt(jnp.sum(x * x, axis=-1, keepdims=True) + 1e-6)


def _split_bf16(x):
    hi = x.astype(BF16)
    return hi, (x - hi.astype(F32)).astype(BF16)


def _make_mm(terms, batched):
    off = 1 if batched else 0
    bdims = ((0,), (0,)) if batched else ((), ())

    def dg(a, b, ca, cb):
        dot = lambda p, q: lax.dot_general(p, q, (((ca + off,), (cb + off,)), bdims), preferred_element_type=F32)
        a_hi, a_lo = _split_bf16(a)
        b_hi, b_lo = _split_bf16(b)
        if terms == 1:
            return dot(a_hi, b_hi)
        return dot(a_hi, b_hi) + (dot(a_hi, b_lo) + dot(a_lo, b_hi))

    @jax.custom_vjp
    def nn(a, b):
        return dg(a, b, 1, 0)

    @jax.custom_vjp
    def nt(a, b):
        return dg(a, b, 1, 1)

    @jax.custom_vjp
    def tn(a, b):
        return dg(a, b, 0, 0)

    nn.defvjp(lambda a, b: (nn(a, b), (a, b)), lambda r, g: (nt(g, r[1]), tn(r[0], g)))
    nt.defvjp(lambda a, b: (nt(a, b), (a, b)), lambda r, g: (nn(g, r[1]), tn(g, r[0])))
    tn.defvjp(lambda a, b: (tn(a, b), (a, b)), lambda r, g: (nt(r[1], g), nn(r[0], g)))
    return nn, nt, tn


_mm, _mm_nt, _mm_tn = _make_mm(1, False)
_mm3, _, _ = _make_mm(3, False)
_bmm, _bmm_nt, _bmm_tn = _make_mm(1, True)
_bmm3, _bmm3_nt, _bmm3_tn = _make_mm(3, True)


@jax.custom_vjp
def _known_inverse(a, inv):
    return inv


_known_inverse.defvjp(lambda a, inv: (inv, inv),
                      lambda inv, g: (-_bmm3_tn(inv, _bmm3_nt(g, inv)), jnp.zeros_like(inv)))


def _tri_ones_dot(x, lower):
    n = x.shape[0]
    r = lax.broadcasted_iota(jnp.int32, (n, n), 0)
    c = lax.broadcasted_iota(jnp.int32, (n, n), 1)
    t = ((r >= c) if lower else (r <= c)).astype(BF16)
    hi, lo = _split_bf16(x)
    return jnp.dot(t, hi, preferred_element_type=F32) + jnp.dot(t, lo, preferred_element_type=F32)


@jax.custom_vjp
def _cumsum_rows(x):
    return _tri_ones_dot(x, True)


_cumsum_rows.defvjp(lambda x: (_tri_ones_dot(x, True), None), lambda _, g: (_tri_ones_dot(g, False),))


def _row_tile(n_rows, cap):
    best = LANE
    for t in range(LANE, cap + 1, LANE):
        if n_rows % t == 0:
            best = t
    return best


def _real_rows(tile_index, tm):
    row = tile_index * tm + lax.broadcasted_iota(jnp.int32, (tm, 1), 0)
    return (row >= PAD).astype(F32)


def _full(shape):
    return pl.BlockSpec(shape, lambda *_: (0,) * len(shape))


def _resident(shape):
    return pl.BlockSpec(shape, lambda *_: (0,) * len(shape), pipeline_mode=pl.Buffered(1))


def _resident_w(wmat, widx):
    if wmat.ndim == 2:
        return _resident(wmat.shape)
    return pl.BlockSpec((None,) + wmat.shape[1:], lambda *_: (widx, 0, 0), pipeline_mode=pl.Buffered(1))


def rms_mm(h, w, wmat_t, *, swiglu, name, widx=None):
    tp, d = h.shape
    n = wmat_t.shape[-2]
    tm = _row_tile(tp, 384)
    half = n // 2
    wmat = wmat_t

    def body(h_ref, w_ref, wm_ref, hn_ref, *outs):
        hn = _rms(h_ref[...], w_ref[...]).astype(BF16)
        hn_ref[...] = hn
        p = lax.dot_general(hn, wm_ref[...], (((1,), (1,)), ((), ())), preferred_element_type=F32)
        if swiglu:
            g, u = p[:, :half], p[:, half:]
            outs[0][...] = g.astype(BF16)
            outs[1][...] = u.astype(BF16)
            outs[2][...] = (_silu(g) * u).astype(BF16)
        else:
            outs[0][...] = p

    row = lambda width: pl.BlockSpec((tm, width), lambda i: (i, 0))
    if swiglu:
        out_shape = (jax.ShapeDtypeStruct((tp, d), BF16),) + (jax.ShapeDtypeStruct((tp, half), BF16),) * 3
        out_specs = (row(d), row(half), row(half), row(half))
    else:
        out_shape = (jax.ShapeDtypeStruct((tp, d), BF16), jax.ShapeDtypeStruct((tp, n), F32))
        out_specs = (row(d), row(n))
    return pl.pallas_call(
        body, name=name, grid=(tp // tm,),
        in_specs=[row(d), _full((1, d)), _resident_w(wmat, widx)],
        out_specs=out_specs, out_shape=out_shape,
    )(h, w, wmat)


def mm_rms_res(acts, wmat, h, w, *, scale, name, widx=None):
    tp, d = h.shape
    tm = _row_tile(tp, 384)
    widths = [a.shape[1] for a in acts]
    offs = [sum(widths[:i]) for i in range(len(acts))]
    na = len(acts)

    def body(*refs):
        a_refs = refs[:na]
        wm_ref, h_ref, w_ref, f_ref, ho_ref = refs[na:]
        f = None
        for a_ref, off, width in zip(a_refs, offs, widths):
            part = jnp.dot(a_ref[...].astype(BF16), wm_ref[off:off + width, :], preferred_element_type=F32)
            f = part if f is None else f + part
        f_ref[...] = f
        ho_ref[...] = h_ref[...] + scale * _rms(f, w_ref[...])

    row = lambda width: pl.BlockSpec((tm, width), lambda i: (i, 0))
    return pl.pallas_call(
        body, name=name, grid=(tp // tm,),
        in_specs=[row(wd) for wd in widths] + [_resident_w(wmat, widx), row(d), _full((1, d))],
        out_specs=(row(d), row(d)),
        out_shape=(jax.ShapeDtypeStruct((tp, d), F32), jax.ShapeDtypeStruct((tp, d), F32)),
    )(*acts, wmat, h, w)


def mm_rms_res_bwd(dho, f, w, wmat, gu, *, scale, name, widx=None):
    tp, d = f.shape
    k = wmat.shape[-2]
    tm = _row_tile(tp, 384)
    swiglu = gu is not None

    def body(*refs):
        if swiglu:
            dho_ref, f_ref, w_ref, wm_ref, g_ref, u_ref, df_ref, dw_ref, dgu_ref = refs
        else:
            dho_ref, f_ref, w_ref, wm_ref, df_ref, dw_ref, da_ref = refs
        i = pl.program_id(0)
        _, vjp = jax.vjp(lambda ff, ww: scale * _rms(ff, ww), f_ref[...], w_ref[...])
        df, dw = vjp(dho_ref[...])
        dfb = (df * _real_rows(i, tm)).astype(BF16)
        df_ref[...] = dfb

        @pl.when(i == 0)
        def _():
            dw_ref[...] = jnp.zeros_like(dw_ref)

        dw_ref[...] += dw
        da = lax.dot_general(dfb, wm_ref[...], (((1,), (1,)), ((), ())), preferred_element_type=F32)
        if swiglu:
            g, u, dab = g_ref[...], u_ref[...], da.astype(BF16)
            s = _sigmoid(g)
            dgu_ref[:, :k] = dab * u * s * (1.0 + g * (1.0 - s))
            dgu_ref[:, k:] = dab * g * s
        else:
            da_ref[...] = da

    row = lambda width: pl.BlockSpec((tm, width), lambda i: (i, 0))
    in_specs = [row(d), row(d), _full((1, d)), _resident_w(wmat, widx)]
    args = [dho, f, w, wmat]
    out_shape = [jax.ShapeDtypeStruct((tp, d), BF16), jax.ShapeDtypeStruct((1, d), F32)]
    out_specs = [row(d), _full((1, d))]
    if swiglu:
        in_specs += [row(k), row(k)]
        args += list(gu)
        out_shape += [jax.ShapeDtypeStruct((tp, 2 * k), BF16)]
        out_specs += [row(2 * k)]
    else:
        out_shape += [jax.ShapeDtypeStruct((tp, k), F32)]
        out_specs += [row(k)]
    return pl.pallas_call(body, name=name, grid=(tp // tm,), in_specs=in_specs, out_specs=tuple(out_specs),
                          out_shape=tuple(out_shape))(*args)


def rms_mm_bwd(dps, wmat, h, w, dho, *, name, widx=None):
    tp, d = h.shape
    tm = _row_tile(tp, 384)
    widths = [p.shape[1] for p in dps]
    offs = [sum(widths[:i]) for i in range(len(dps))]
    ndp = len(dps)

    def body(*refs):
        dp_refs = refs[:ndp]
        wm_ref, h_ref, w_ref, dho_ref, dh_ref, dw_ref = refs[ndp:]
        i = pl.program_id(0)
        dhn = None
        for dp_ref, off, width in zip(dp_refs, offs, widths):
            part = jnp.dot(dp_ref[...].astype(BF16), wm_ref[off:off + width, :], preferred_element_type=F32)
            dhn = part if dhn is None else dhn + part
        _, vjp = jax.vjp(_rms, h_ref[...], w_ref[...])
        dx, dw = vjp(dhn)
        dh_ref[...] = (dho_ref[...] + dx) * _real_rows(i, tm)

        @pl.when(i == 0)
        def _():
            dw_ref[...] = jnp.zeros_like(dw_ref)

        dw_ref[...] += dw

    row = lambda width: pl.BlockSpec((tm, width), lambda i: (i, 0))
    return pl.pallas_call(
        body, name=name, grid=(tp // tm,),
        in_specs=[row(wd) for wd in widths] + [_resident_w(wmat, widx), row(d), _full((1, d)), row(d)],
        out_specs=(row(d), _full((1, d))),
        out_shape=(jax.ShapeDtypeStruct((tp, d), F32), jax.ShapeDtypeStruct((1, d), F32)),
    )(*dps, wmat, h, w, dho)


def mm_tn(a, b, *, name, out_dtype=F32, after=None):
    t, m = a.shape
    n = b.shape[1]
    bm = _row_tile(m, 704)
    ties = [] if after is None else [after]

    def body(a_ref, b_ref, *rest):
        rest[-1][...] = lax.dot_general(a_ref[...].astype(BF16), b_ref[...].astype(BF16), (((0,), (0,)), ((), ())),
                                        preferred_element_type=F32).astype(out_dtype)

    return pl.pallas_call(
        body, name=name, grid=(m // bm,),
        in_specs=[pl.BlockSpec((t, bm), lambda i: (0, i)), _resident((t, n))] + [pl.BlockSpec(memory_space=pl.ANY)] * len(ties),
        out_specs=pl.BlockSpec((bm, n), lambda i: (i, 0)),
        out_shape=jax.ShapeDtypeStruct((m, n), out_dtype),
    )(a, b, *ties)


def loss_and_grad(h, target, *, name):
    tp, d = h.shape
    tm = SWA_BLOCK

    def body(h_ref, t_ref, dh_ref, loss_ref):
        i = pl.program_id(0)

        @pl.when(i == 0)
        def _():
            loss_ref[...] = jnp.zeros_like(loss_ref)
            dh_ref[...] = jnp.zeros_like(dh_ref)

        @pl.when(i > 0)
        def _():
            err = h_ref[...] - t_ref[...]
            dh_ref[...] = err * (1.0 / d)
            loss_ref[...] += 0.5 * jnp.sum(jnp.sum(err * err, axis=1, keepdims=True), axis=0, keepdims=True) * (1.0 / d)

    return pl.pallas_call(
        body, name=name, grid=(tp // tm,),
        in_specs=[pl.BlockSpec((tm, d), lambda i: (i, 0)), pl.BlockSpec((tm, d), lambda i: (jnp.maximum(i - 1, 0), 0))],
        out_specs=(pl.BlockSpec((tm, d), lambda i: (i, 0)), _full((1, 1))),
        out_shape=(jax.ShapeDtypeStruct((tp, d), F32), jax.ShapeDtypeStruct((1, 1), F32)),
    )(h, target)


def _t5_bucket_np(rel):
    n = np.maximum(rel, 0)
    max_exact = REL_BUCKETS // 2
    n_f = np.maximum(n, 1).astype(np.float32)
    large = max_exact + (np.log(n_f / np.float32(max_exact)) / np.float32(math.log(REL_MAX_DIST / max_exact))
                         * np.float32(REL_BUCKETS - max_exact)).astype(np.int32)
    large = np.minimum(large, REL_BUCKETS - 1)
    return np.where(n < max_exact, n, large).astype(np.int32)


def _swa_positions_np(n):
    i = np.arange(SWA_BLOCK)[:, None]
    j = np.arange(3 * SWA_BLOCK)[None, :]
    pos_q = n * SWA_BLOCK + i - PAD
    pos_k = np.where(j < SWA_BLOCK, j - PAD, (n - 1) * SWA_BLOCK + (j - SWA_BLOCK) - PAD)
    return pos_q, pos_k


def _swa_buckets():
    out = []
    for n in range(3):
        pos_q, pos_k = _swa_positions_np(n)
        out.append(_t5_bucket_np(pos_q - pos_k))
    return jnp.asarray(np.stack(out))


def swa_bias(table, buckets, *, name):
    nc, nq, nk = buckets.shape

    def body(tab_ref, bkt_ref, out_ref):
        for c in range(nc):
            bkt = bkt_ref[c]
            for h in range(SWA_Q_HEADS):
                acc = jnp.zeros((nq, nk), F32)
                for b in range(REL_BUCKETS):
                    acc = jnp.where(bkt == b, tab_ref[b, h], acc)
                out_ref[c, h] = acc

    return pl.pallas_call(
        body, name=name,
        in_specs=[pl.BlockSpec(memory_space=pltpu.SMEM), pl.BlockSpec(memory_space=pltpu.VMEM)],
        out_specs=pl.BlockSpec(memory_space=pltpu.VMEM),
        out_shape=jax.ShapeDtypeStruct((nc, SWA_Q_HEADS, nq, nk), F32),
    )(table, buckets)


def swa_bias_bwd(dbias, buckets, *, name):
    nc = buckets.shape[0]

    def body(db_ref, bkt_ref, out_ref):
        lane = lax.broadcasted_iota(jnp.int32, (1, LANE), 1)
        for b in range(REL_BUCKETS):
            row = jnp.zeros((1, LANE), F32)
            for c in range(nc):
                hit = bkt_ref[c] == b
                for h in range(SWA_Q_HEADS):
                    part = jnp.where(hit, db_ref[c, h], 0.0)
                    tot = jnp.sum(jnp.sum(part, axis=1, keepdims=True), axis=0, keepdims=True)
                    row = row + jnp.where(lane == h, tot, 0.0)
            out_ref[b:b + 1, :] = row

    return pl.pallas_call(
        body, name=name,
        in_specs=[pl.BlockSpec(memory_space=pltpu.VMEM), pl.BlockSpec(memory_space=pltpu.VMEM)],
        out_specs=pl.BlockSpec(memory_space=pltpu.VMEM),
        out_shape=jax.ShapeDtypeStruct((REL_BUCKETS, LANE), F32),
    )(dbias, buckets)


def _swa_block(q, kvm, kvp, kvc, bias, sinks, n, batched, row0=0):
    blk = SWA_BLOCK
    i = lax.broadcasted_iota(jnp.int32, (q.shape[0], 3 * blk), 0)
    j = lax.broadcasted_iota(jnp.int32, (q.shape[0], 3 * blk), 1)
    pos_q = n * blk + row0 + i - PAD
    is_meta = j < blk
    pos_k = jnp.where(is_meta, j - PAD, (n - 1) * blk + (j - blk) - PAD)
    rel = pos_q - pos_k
    valid = ((is_meta & (pos_k >= 0) & (pos_k < N_META) & (rel >= 0))
             | (jnp.logical_not(is_meta) & (pos_k >= N_META) & (rel >= 0) & (rel < SWA_WINDOW)))
    valid_f = valid.astype(F32)
    kv =jnp.concatenate([kvm, kvp, kvc], axis=0)
    lane = lax.broadcasted_iota(jnp.int32, (1, LANE), 1)
    halves = ((lane < SWA_HEAD_DIM).astype(F32), (lane >= SWA_HEAD_DIM).astype(F32))
    nh, group = SWA_Q_HEADS, SWA_Q_HEADS // 2
    q_of = lambda h: q[:, (h // 2) * LANE:(h // 2 + 1) * LANE] * halves[h % 2]
    k_of = lambda h: kv[:, (h // group) * LANE:(h // group + 1) * LANE]
    v_of = lambda h: kv[:, (2 + h // group) * LANE:(3 + h // group) * LANE]
    sink_of = lambda h: jnp.sum(jnp.where(lane == h, sinks, 0.0), axis=1, keepdims=True)

    scale = SWA_HEAD_DIM ** -0.5

    def attend(logits, sink, pv):
        if batched:
            s = logits * valid_f + (valid_f - 1.0) * (-NEG_INF)
        else:
            s = jnp.where(valid, logits, NEG_INF)
        m =lax.stop_gradient(jnp.maximum(jnp.max(s, axis=-1, keepdims=True), sink))
        e = jnp.exp(s - m)
        return pv(e / (jnp.sum(e, axis=-1, keepdims=True) + jnp.exp(sink - m)))

    if batched:
        heads = range(nh)
        vh = _stack([v_of(h) for h in heads])
        qk = _bmm_nt(_stack([q_of(h) for h in heads]), _stack([k_of(h) for h in heads]))
        o = attend(qk * scale + bias, _stack([sink_of(h) for h in heads]), lambda p: _bmm(p, vh))
        head = lambda h: o[h]
    else:
        head = lambda h: attend(_mm_nt(q_of(h), k_of(h)) * scale + bias[h], sink_of(h), lambda p: _mm(p, v_of(h)))
    return jnp.concatenate([head(2 * p) * halves[0] + head(2 * p + 1) * halves[1] for p in range(nh // 2)], axis=1)


def _swa_in_specs(nb, rev):
    blk = SWA_BLOCK
    step = (lambda i: nb - 1 - i) if rev else (lambda i: i)
    return [
        pl.BlockSpec((blk, 4 * LANE), lambda i: (step(i), 0)),
        pl.BlockSpec((blk, 4 * LANE), lambda i: (0, 1)),
        pl.BlockSpec((blk, 4 * LANE), lambda i: (jnp.maximum(step(i) - 1, 0), 1)),
        pl.BlockSpec((blk, 4 * LANE), lambda i: (step(i), 1)),
        pl.BlockSpec((1, SWA_Q_HEADS, blk, 3 * blk), lambda i: (jnp.minimum(step(i), 2), 0, 0, 0)),
        _full((1, LANE)),
    ]


def swa_fwd(proj, bias, sinks, *, name):
    tp = proj.shape[0]
    nb = tp // SWA_BLOCK

    def body(q_ref, kvm_ref, kvp_ref, kvc_ref, bias_ref, sinks_ref, o_ref):
        n = pl.program_id(0)
        o_ref[...] = _swa_block(q_ref[...], kvm_ref[...], kvp_ref[...], kvc_ref[...], bias_ref[0], sinks_ref[...], n, True)

    return pl.pallas_call(
        body, name=name, grid=(nb,),
        in_specs=_swa_in_specs(nb, False),
        out_specs=pl.BlockSpec((SWA_BLOCK, 4 * LANE), lambda i: (i, 0)),
        out_shape=jax.ShapeDtypeStruct((tp, 4 * LANE), F32),
    )(proj, proj, proj, proj, bias, sinks)


def swa_bwd(proj, bias, sinks, do, *, name):
    tp = proj.shape[0]
    nb = tp // SWA_BLOCK
    blk = SWA_BLOCK

    def body(q_ref, kvm_ref, kvp_ref, kvc_ref, bias_ref, sinks_ref, do_ref, dqkv_ref, dbias_ref, dsinks_ref,
             carry, meta_acc):
        i = pl.program_id(0)
        n = nb - 1 - i

        @pl.when(i == 0)
        def _():
            carry[...] = jnp.zeros_like(carry)
            meta_acc[...] = jnp.zeros_like(meta_acc)
            dsinks_ref[...] = jnp.zeros_like(dsinks_ref)

        half = blk // 2
        dkvm = dkvp = dkvc = dsinks = None
        dbias_halves = []
        for r0 in (0, half):
            at = pl.ds(r0, half)
            fn = lambda q, kvm, kvp, kvc, b, s: _swa_block(q, kvm, kvp, kvc, b, s, n, False, r0)
            _, vjp = jax.vjp(fn, q_ref[at, :], kvm_ref[...], kvp_ref[...], kvc_ref[...], bias_ref[0, :, at, :],
                             sinks_ref[...])
            dq, gm, gp, gc, gb, gs = vjp(do_ref[at, :])
            dqkv_ref[at, :4 * LANE] = dq.astype(BF16)
            dbias_halves.append(gb)
            if r0 == 0:
                dkvm, dkvp, dkvc, dsinks = gm, gp, gc, gs
            else:
                dkvm, dkvp, dkvc, dsinks = dkvm + gm, dkvp + gp, dkvc + gc, dsinks + gs
        dbias = jnp.concatenate(dbias_halves, axis=1)
        meta_acc[...] += dkvm
        dqkv_ref[:, 4 * LANE:] = (dkvc + carry[...] + jnp.where(n == 0, meta_acc[...], 0.0)).astype(BF16)
        carry[...] = dkvp
        first_visit = (n == nb - 1) | (n < 2)

        @pl.when(first_visit)
        def _():
            dbias_ref[0] = dbias

        @pl.when(jnp.logical_not(first_visit))
        def _():
            dbias_ref[0] += dbias

        dsinks_ref[...] += dsinks

    rev = lambda i: nb - 1 - i
    return pl.pallas_call(
        body, name=name, grid=(nb,),
        in_specs=_swa_in_specs(nb, True) + [pl.BlockSpec((blk, 4 * LANE), lambda i: (rev(i), 0))],
        out_specs=(pl.BlockSpec((blk, 8 * LANE), lambda i: (rev(i), 0)),
                   pl.BlockSpec((1, SWA_Q_HEADS, blk, 3 * blk), lambda i: (jnp.minimum(rev(i), 2), 0, 0, 0)),
                   _full((1, LANE))),
        out_shape=(jax.ShapeDtypeStruct((tp, PROJ_DIM), BF16),
                   jax.ShapeDtypeStruct((3, SWA_Q_HEADS, blk, 3 * blk), F32), jax.ShapeDtypeStruct((1, LANE), F32)),
        scratch_shapes=[pltpu.VMEM((blk, 4 * LANE), F32), pltpu.VMEM((blk, 4 * LANE), F32)],
    )(proj, proj, proj, proj, bias, sinks, do)


CONV_COL0 = 2
HALO = 8


def conv_fwd(proj, conv_w, *, name):
    tp = proj.shape[0]
    tm = _row_tile(tp, 384)
    cw = 4 * LANE
    ncol = conv_w.shape[1] // cw

    def body(x_ref, halo_ref, w_ref, y_ref, buf):
        i = pl.program_id(1)
        buf[0:HALO, :] = jnp.where(i > 0, halo_ref[...], 0.0)
        buf[HALO:, :] = x_ref[...]
        acc = None
        for j in range(DN_CONV):
            term = w_ref[j:j + 1, :] * buf[pl.ds(HALO - (DN_CONV - 1) + j, tm), :]
            acc = term if acc is None else acc + term
        y_ref[...] = acc

    return pl.pallas_call(
        body, name=name, grid=(ncol, tp // tm),
        in_specs=[pl.BlockSpec((tm, cw), lambda c, i: (i, CONV_COL0 + c)),
                  pl.BlockSpec((HALO, cw), lambda c, i: (jnp.maximum(i * (tm // HALO) - 1, 0), CONV_COL0 + c)),
                  pl.BlockSpec((DN_CONV, cw), lambda c, i: (0, c))],
        out_specs=pl.BlockSpec((tm, cw), lambda c, i: (i, c)),
        out_shape=jax.ShapeDtypeStruct((tp, ncol * cw), F32),
        scratch_shapes=[pltpu.VMEM((tm + HALO, cw), F32)],
    )(proj, proj, conv_w)


def conv_bwd(proj, conv_w, dy, dproj, *, name):
    tp = proj.shape[0]
    tm = _row_tile(tp, 384)
    cw = 4 * LANE
    ncol = conv_w.shape[1] // cw
    nt = tp // tm

    def body(x_ref, xhalo_ref, w_ref, dy_ref, dyhalo_ref, _, dx_ref, dw_ref, xbuf, dbuf):
        i = pl.program_id(1)
        xbuf[0:HALO, :] = jnp.where(i > 0, xhalo_ref[...], 0.0)
        xbuf[HALO:, :] = x_ref[...]
        dbuf[0:tm, :] = dy_ref[...]
        dbuf[tm:, :] = jnp.where(i < nt - 1, dyhalo_ref[...], 0.0)
        dy_t = dy_ref[...]
        acc = None
        rows = []
        for j in range(DN_CONV):
            term = w_ref[j:j + 1, :] * dbuf[pl.ds(DN_CONV - 1 - j, tm), :]
            acc = term if acc is None else acc + term
            rows.append(jnp.sum(dy_t * xbuf[pl.ds(HALO - (DN_CONV - 1) + j, tm), :], axis=0, keepdims=True))
        dx_ref[...] = acc.astype(BF16)

        @pl.when(i == 0)
        def _():
            dw_ref[...] = jnp.zeros_like(dw_ref)

        for j in range(DN_CONV):
            dw_ref[j:j + 1, :] += rows[j]

    return pl.pallas_call(
        body, name=name, grid=(ncol, nt),
        in_specs=[pl.BlockSpec((tm, cw), lambda c, i: (i, CONV_COL0 + c)),
                  pl.BlockSpec((HALO, cw), lambda c, i: (jnp.maximum(i * (tm // HALO) - 1, 0), CONV_COL0 + c)),
                  pl.BlockSpec((DN_CONV, cw), lambda c, i: (0, c)),
                  pl.BlockSpec((tm, cw), lambda c, i: (i, c)),
                  pl.BlockSpec((HALO, cw), lambda c, i: (jnp.minimum((i + 1) * (tm // HALO), tp // HALO - 1), c)),
                  pl.BlockSpec(memory_space=pl.ANY)],
        out_specs=(pl.BlockSpec((tm, cw), lambda c, i: (i, CONV_COL0 + c)), pl.BlockSpec((DN_CONV, cw), lambda c, i: (0, c))),
        out_shape=(jax.ShapeDtypeStruct(dproj.shape, dproj.dtype), jax.ShapeDtypeStruct((DN_CONV, ncol * cw), F32)),
        scratch_shapes=[pltpu.VMEM((tm + HALO, cw), F32), pltpu.VMEM((tm + HALO, cw), F32)],
        input_output_aliases={5: 0},
    )(proj, proj, conv_w, dy, dy, dproj)


def _stack(parts):
    return jnp.concatenate([p[None] for p in parts], axis=0)


def _chunk_masks():
    r = lax.broadcasted_iota(jnp.int32, (CHUNK, CHUNK), 0)
    c = lax.broadcasted_iota(jnp.int32, (CHUNK, CHUNK), 1)
    return (r >= c).astype(F32), (r > c).astype(F32), (r == c).astype(F32)


def _dn_chunk(y, z, small, s, a_log, dt_bias, norm_w, rows, known_inv=None):
    tri_incl, tri_strict, eye = _chunk_masks()
    lane = lax.broadcasted_iota(jnp.int32, (1, LANE), 1)
    dk = DN_HEAD_DIM
    nh = DN_HEADS
    heads = lambda t, first: _stack([t[:, (first + h) * dk:(first + h + 1) * dk] for h in range(nh)])
    pick = lambda t, l: jnp.sum(jnp.where(lane == l, t, 0.0), axis=1, keepdims=True)
    q = _l2n(_silu(heads(y, 0))) * dk ** -0.5
    k = _l2n(_silu(heads(y, nh)))
    v = _silu(heads(y, 2 * nh))
    g_all = jnp.where(lane < nh, -jnp.exp(a_log) * _softplus(small + dt_bias), 0.0) * rows
    beta_all = _sigmoid(small)
    gc_all = _cumsum_rows(g_all)
    g_sum = jnp.sum(g_all, axis=0, keepdims=True)
    gc = _stack([pick(gc_all, h) for h in range(nh)])
    beta = _stack([pick(beta_all, nh + h) for h in range(nh)])
    g_last = _stack([pick(g_sum, h) for h in range(nh)])
    gc_row = jnp.sum(eye * gc, axis=1, keepdims=True)
    gamma = jnp.exp((gc - gc_row) * tri_incl) * tri_incl
    k_beta = k * beta
    v_beta = v * beta
    a = _bmm_nt(k_beta, k) * gamma * tri_strict
    if known_inv is None:
        inv = eye - a
        power = a
        for _ in range(5):
            power = _bmm3(power, power)
            inv = inv + _bmm3(inv, power)
    else:
        inv = _known_inverse(a, known_inv)
    e_gc = jnp.exp(gc)
    uw = _bmm3(inv, jnp.concatenate([v_beta, k_beta * e_gc], axis=2))
    u, w = uw[:, :, :dk], uw[:, :, dk:]
    attn = _bmm_nt(q, k) * gamma
    q_dec = q * e_gc
    k_dec = k * jnp.exp(g_last - gc)
    v_new = u - _bmm(w, s)
    o = _bmm(q_dec, s) + _bmm(attn, v_new)
    s_new = s * jnp.exp(g_last) + _bmm_tn(k_dec, v_new)
    out = _rms(o, norm_w) * _silu(heads(z, 0))
    return jnp.concatenate([out[h] for h in range(nh)], axis=1), s_new, inv


Z_COL = 5
SMALL_COL = 24


def _chunk_rows(n):
    row = n * CHUNK + lax.broadcasted_iota(jnp.int32, (CHUNK, 1), 0)
    return (row >= PAD).astype(F32)


def dn_fwd(y, proj, a_log, dt_bias, norm_w, *, name):
    tp = y.shape[0]
    nc = tp // CHUNK
    dk = DN_HEAD_DIM
    per = CHUNKS_PER_STEP
    rows = per * CHUNK

    def body(y_ref, z_ref, small_ref, al_ref, dt_ref, nw_ref, o_ref, ssave_ref, isave_ref, state):
        n = pl.program_id(0)

        @pl.when(n == 0)
        def _():
            state[...] = jnp.zeros_like(state)

        s = state[...]
        for c in range(per):
            at = pl.ds(c * CHUNK, CHUNK)
            ssave_ref[c] = s
            out, s, inv = _dn_chunk(y_ref[at, :], z_ref[at, :], small_ref[at, :], s, al_ref[...], dt_ref[...],
                                    nw_ref[...], _chunk_rows(per * n + c))
            o_ref[at, :] = out
            isave_ref[c] = inv
        state[...] = s

    return pl.pallas_call(
        body, name=name, grid=(nc // per,),
        in_specs=[pl.BlockSpec((rows, y.shape[1]), lambda n: (n, 0)),
                  pl.BlockSpec((rows, 4 * LANE), lambda n: (n, Z_COL)),
                  pl.BlockSpec((rows, LANE), lambda n: (n, SMALL_COL)),
                  _full((1, LANE)), _full((1, LANE)), _full((1, LANE))],
        out_specs=(pl.BlockSpec((rows, 4 * LANE), lambda n: (n, 0)),
                   pl.BlockSpec((per, DN_HEADS, dk, dk), lambda n: (n, 0, 0, 0)),
                   pl.BlockSpec((per, DN_HEADS, CHUNK, CHUNK), lambda n: (n, 0, 0, 0))),
        out_shape=(jax.ShapeDtypeStruct((tp, 4 * LANE), F32), jax.ShapeDtypeStruct((nc, DN_HEADS, dk, dk), F32),
                   jax.ShapeDtypeStruct((nc, DN_HEADS, CHUNK, CHUNK), F32)),
        scratch_shapes=[pltpu.VMEM((DN_HEADS, dk, dk), F32)],
    )(y, proj, proj, a_log, dt_bias, norm_w)


def dn_bwd(y, proj, a_log, dt_bias, norm_w, ssave, isave, do, dproj, *, name):
    tp = y.shape[0]
    nc = tp // CHUNK
    dk = DN_HEAD_DIM
    per = CHUNKS_PER_STEP
    rev = lambda i: nc // per - 1 - i
    zs_width = 5 * LANE

    def body(y_ref, z_ref, small_ref, al_ref, dt_ref, nw_ref, ss_ref, is_ref, do_ref, _,
             dy_ref, dzs_ref, dal_ref, ddt_ref, dnw_ref, dstate):
        i = pl.program_id(0)
        n = nc // per - 1 - i

        @pl.when(i == 0)
        def _():
            dstate[...] = jnp.zeros_like(dstate)
            dal_ref[...] = jnp.zeros_like(dal_ref)
            ddt_ref[...] = jnp.zeros_like(ddt_ref)
            dnw_ref[...] = jnp.zeros_like(dnw_ref)

        ds = dstate[...]
        for c in reversed(range(per)):
            at = pl.ds(c * CHUNK, CHUNK)
            token_rows = _chunk_rows(per * n + c)
            known_inv = is_ref[c]
            fn = lambda *a: _dn_chunk(*a, token_rows, known_inv)[:2]
            _, vjp = jax.vjp(fn, y_ref[at, :], z_ref[at, :], small_ref[at, :], ss_ref[c], al_ref[...], dt_ref[...],
                             nw_ref[...])
            dy, dz, dsmall, ds, dal, ddt, dnw = vjp((do_ref[at, :], ds))
            dy_ref[at, :] = dy
            dzs_ref[at, :4 * LANE] = dz.astype(BF16)
            dzs_ref[at, 4 * LANE:] = dsmall.astype(BF16)
            dal_ref[...] += dal
            ddt_ref[...] += ddt
            dnw_ref[...] += dnw
        dstate[...] = ds

    rows = per * CHUNK
    return pl.pallas_call(
        body, name=name, grid=(nc // per,),
        in_specs=[pl.BlockSpec((rows, y.shape[1]), lambda i: (rev(i), 0)),
                  pl.BlockSpec((rows, 4 * LANE), lambda i: (rev(i), Z_COL)),
                  pl.BlockSpec((rows, LANE), lambda i: (rev(i), SMALL_COL)),
                  _full((1, LANE)), _full((1, LANE)), _full((1, LANE)),
                  pl.BlockSpec((per, DN_HEADS, dk, dk), lambda i: (rev(i), 0, 0, 0)),
                  pl.BlockSpec((per, DN_HEADS, CHUNK, CHUNK), lambda i: (rev(i), 0, 0, 0)),
                  pl.BlockSpec((rows, 4 * LANE), lambda i: (rev(i), 1)),
                  pl.BlockSpec(memory_space=pl.ANY)],
        out_specs=(pl.BlockSpec((rows, y.shape[1]), lambda i: (rev(i), 0)),
                   pl.BlockSpec((rows, zs_width), lambda i: (rev(i), Z_COL * 4 * LANE // zs_width)),
                   _full((1, LANE)), _full((1, LANE)), _full((1, LANE))),
        out_shape=(jax.ShapeDtypeStruct((tp, y.shape[1]), F32), jax.ShapeDtypeStruct(dproj.shape, dproj.dtype),
                   jax.ShapeDtypeStruct((1, LANE), F32), jax.ShapeDtypeStruct((1, LANE), F32),
                   jax.ShapeDtypeStruct((1, LANE), F32)),
        scratch_shapes=[pltpu.VMEM((DN_HEADS, dk, dk), F32)],
        input_output_aliases={9: 1},
    )(y, proj, proj, a_log, dt_bias, norm_w, ssave, isave, do, dproj)


def _gla_chunk(q, k, v, gate, low, s, w_gate_up, b_gate, norm_w, rows):
    tri_incl, _, _ = _chunk_masks()
    dk, dv, nh = GLA_DK, GLA_DV, GLA_HEADS
    heads = lambda t, width: _stack([t[:, h * width:(h + 1) * width] for h in range(nh)])
    logit = _mm3(low, w_gate_up) + b_gate
    glog_all = -_softplus(-logit) * (1.0 / GLA_GATE_NORM) * rows
    glog = heads(glog_all, dk)
    bcum = heads(_cumsum_rows(glog_all), dk)
    qh = heads(q, dk) * dk ** -0.5
    kh = heads(k, dk)
    vh = heads(v, dv)
    q_dec = qh * jnp.exp(bcum)
    attn = _bmm_nt(q_dec, kh * jnp.exp(-bcum)) * tri_incl
    b_last = jnp.sum(glog, axis=1, keepdims=True)
    k_dec = kh * jnp.exp(b_last - bcum)
    r = lax.broadcasted_iota(jnp.int32, (dk, dk), 0)
    c = lax.broadcasted_iota(jnp.int32, (dk, dk), 1)
    b_last_col = jnp.sum((r == c).astype(F32) * b_last, axis=2, keepdims=True)
    o = _bmm(attn, vh) + _bmm(q_dec, s)
    s_new = s * jnp.exp(b_last_col) + _bmm_tn(k_dec, vh)
    out = _rms(o, norm_w) * _silu(heads(gate, dv))
    return jnp.concatenate([out[h] for h in range(nh)], axis=1), s_new


LOW_COL = 24


def _gla_in_specs(step, rows=CHUNK):
    return [pl.BlockSpec((rows, 4 * LANE), lambda i: (step(i), 0)),
            pl.BlockSpec((rows, 4 * LANE), lambda i: (step(i), 1)),
            pl.BlockSpec((rows, 8 * LANE), lambda i: (step(i), 1)),
            pl.BlockSpec((rows, 8 * LANE), lambda i: (step(i), 2)),
            pl.BlockSpec((rows, LANE), lambda i: (step(i), LOW_COL)),
            _full((LANE, 4 * LANE)), _full((1, 4 * LANE)), _full((1, GLA_DV))]


def gla_fwd(proj, w_gate_up, b_gate, norm_w, *, name):
    tp = proj.shape[0]
    nc = tp // CHUNK
    per = CHUNKS_PER_STEP
    rows = per * CHUNK

    def body(q_ref, k_ref, v_ref, g_ref, low_ref, wgu_ref, bg_ref, nw_ref, o_ref, ssave_ref, state):
        n = pl.program_id(0)

        @pl.when(n == 0)
        def _():
            state[...] = jnp.zeros_like(state)

        s = state[...]
        for c in range(per):
            at = pl.ds(c * CHUNK, CHUNK)
            ssave_ref[c] = s
            out, s = _gla_chunk(q_ref[at, :], k_ref[at, :], v_ref[at, :], g_ref[at, :], low_ref[at, :], s, wgu_ref[...],
                                bg_ref[...], nw_ref[...], _chunk_rows(per * n + c))
            o_ref[at, :] = out
        state[...] = s

    return pl.pallas_call(
        body, name=name, grid=(nc // per,),
        in_specs=_gla_in_specs(lambda i: i, rows),
        out_specs=(pl.BlockSpec((rows, 8 * LANE), lambda n: (n, 0)),
                   pl.BlockSpec((per, GLA_HEADS, GLA_DK, GLA_DV), lambda n: (n, 0, 0, 0))),
        out_shape=(jax.ShapeDtypeStruct((tp, 8 * LANE), F32),
                   jax.ShapeDtypeStruct((nc, GLA_HEADS, GLA_DK, GLA_DV), F32)),
        scratch_shapes=[pltpu.VMEM((GLA_HEADS, GLA_DK, GLA_DV), F32)],
    )(proj, proj, proj, proj, proj, w_gate_up, b_gate, norm_w)


def gla_bwd(proj, w_gate_up, b_gate, norm_w, ssave, do, *, name):
    tp = proj.shape[0]
    nc = tp // CHUNK
    per = CHUNKS_PER_STEP
    rev = lambda i: nc // per - 1 - i

    def body(q_ref, k_ref, v_ref, g_ref, low_ref, wgu_ref, bg_ref, nw_ref, ss_ref, do_ref,
             dproj_ref, dwgu_ref, dbg_ref, dnw_ref, dstate):
        i = pl.program_id(0)
        n = nc // per - 1 - i

        @pl.when(i == 0)
        def _():
            dstate[...] = jnp.zeros_like(dstate)
            dwgu_ref[...] = jnp.zeros_like(dwgu_ref)
            dbg_ref[...] = jnp.zeros_like(dbg_ref)
            dnw_ref[...] = jnp.zeros_like(dnw_ref)

        ds = dstate[...]
        for c in reversed(range(per)):
            at = pl.ds(c * CHUNK, CHUNK)
            token_rows = _chunk_rows(per * n + c)
            fn = lambda *a: _gla_chunk(*a, token_rows)
            _, vjp = jax.vjp(fn, q_ref[at, :], k_ref[at, :], v_ref[at, :], g_ref[at, :], low_ref[at, :], ss_ref[c],
                             wgu_ref[...], bg_ref[...], nw_ref[...])
            dq, dk, dv, dg, dlow, ds, dwgu, dbg, dnw = vjp((do_ref[at, :], ds))
            off = 0
            for part in (dq, dk, dv, dg, dlow):
                dproj_ref[at, off:off + part.shape[1]] = part.astype(BF16)
                off += part.shape[1]
            dwgu_ref[...] += dwgu
            dbg_ref[...] += dbg
            dnw_ref[...] += dnw
        dstate[...] = ds

    chunk = lambda width: pl.BlockSpec((per * CHUNK, width), lambda i: (rev(i), 0))
    return pl.pallas_call(
        body, name=name, grid=(nc // per,),
        in_specs=_gla_in_specs(rev, per * CHUNK) + [pl.BlockSpec((per, GLA_HEADS, GLA_DK, GLA_DV), lambda i: (rev(i), 0, 0, 0)),
                                                    chunk(8 * LANE)],
        out_specs=(chunk(PROJ_DIM), _full((LANE, 4 * LANE)), _full((1, 4 * LANE)), _full((1, GLA_DV))),
        out_shape=(jax.ShapeDtypeStruct((tp, PROJ_DIM), BF16), jax.ShapeDtypeStruct((LANE, 4 * LANE), F32),
                   jax.ShapeDtypeStruct((1, 4 * LANE), F32), jax.ShapeDtypeStruct((1, GLA_DV), F32)),
        scratch_shapes=[pltpu.VMEM((GLA_HEADS, GLA_DK, GLA_DV), F32)],
    )(proj, proj, proj, proj, proj, w_gate_up, b_gate, norm_w, ssave, do)


def _even_proj_weight(w_t):
    hd = SWA_HEAD_DIM
    k0, k1 = w_t[512:512 + hd], w_t[512 + hd:640]
    v0, v1 = w_t[640:640 + hd], w_t[640 + hd:768]
    zeros = jnp.zeros((LANE - 2 * DN_HEADS, w_t.shape[1]), w_t.dtype)
    return jnp.concatenate([w_t[:512], k0, k0, k1, k1, v0, v0, v1, v1, w_t[768:2816], w_t[2820:2824], w_t[2816:2820],
                            zeros], axis=0)


def _even_proj_weight_grad(dw):
    hd = SWA_HEAD_DIM
    c = lambda i: dw[512 + i * hd:512 + (i + 1) * hd]
    return jnp.concatenate([dw[:512], c(0) + c(1), c(2) + c(3), c(4) + c(5), c(6) + c(7), dw[1024:3072],
                            dw[3076:3080], dw[3072:3076]], axis=0)


def _ffn_fwd(h, nw_in, nw_out, wts, idx, get_w):
    wts.update(get_w(f"ffn{idx}", h))
    w_gu = wts[f"w_gu{idx}"]
    hn, g, u, a = rms_mm(h, nw_in, w_gu[0], swiglu=True, name=f"ffn_up_{idx}", widx=w_gu[1])
    wts.update(get_w(f"down{idx}", a))
    w_down = wts[f"w_down{idx}"]
    f, h_out = mm_rms_res([a], w_down[0], h, nw_out, scale=0.5, name=f"ffn_down_{idx}", widx=w_down[1])
    return h_out, (h, hn, g, u, a, f)


def _ffn_bwd(dho, saved, nw_in, nw_out, w_gu, w_down, idx, on_grads):
    h, hn, g, u, a, f = saved
    df, dnw_out, dgu = mm_rms_res_bwd(dho, f, nw_out, w_down[0], (g, u), scale=0.5, name=f"ffn_down_bwd_{idx}",
                                      widx=w_down[1])
    g_down = mm_tn(a, df, name=f"ffn_dwd_{idx}", out_dtype=BF16)
    sent = on_grads("down", g_down)
    g_gu = mm_tn(dgu, hn, name=f"ffn_dwgu_{idx}", out_dtype=BF16, after=sent)
    sent = on_grads("gu", g_gu)
    dh, dnw_in = rms_mm_bwd([dgu], w_gu[0], h, nw_in + sent, dho, name=f"ffn_up_bwd_{idx}", widx=w_gu[1])
    return dh, dnw_in, dnw_out


def local_step(x, target, wts, get_w=None, put_g=None):
    seq, d = x.shape
    wts = dict(wts)
    get_w = get_w or (lambda stage, after: {})
    put_g = put_g or (lambda stage, grads: jnp.zeros((1, 1), F32))
    row = lambda v: v.reshape(1, -1)
    lane_row = lambda v: jnp.pad(v.reshape(1, -1), ((0, 0), (0, LANE - v.size)))
    nw = wts["norm_w"]
    h = lax.dynamic_update_slice(jnp.concatenate([jnp.zeros((PAD + N_META, d), F32), x], axis=0), wts["meta_tokens"], (PAD, 0))
    buckets = _swa_buckets()
    bias = swa_bias(wts["rel_bias_table"], buckets, name="swa_bias")
    sinks = lane_row(wts["swa_sinks"])
    a_log, dt_bias = lane_row(wts["dn_a_log"]), lane_row(wts["dn_dt_bias"])
    dn_norm_w = row(wts["dn_norm_w"])
    conv_w = wts["even_conv_w"][0]
    w_gate_up = jnp.pad(wts["gla_w_gate_up"][0], ((0, LANE - GLA_GATE_RANK), (0, 0)))
    b_gate, gla_norm_w = row(wts["gla_b_gate"]), row(wts["gla_norm_w"])

    saved = []
    w_in, w_out = [None, None], [None, None]
    for l in range(2):
        h, s_a = _ffn_fwd(h, row(nw[l, 0]), row(nw[l, 1]), wts, 2 * l, get_w)
        if l == 0:
            wts.update(get_w("even", h))
            w_in[0], w_out[0] = _even_proj_weight(wts["even_w_in"]), wts["even_w_out"]
        else:
            wts.update(get_w("odd", h))
            w_in[1] = jnp.pad(wts["odd_w_in"], ((0, PROJ_DIM - wts["odd_w_in"].shape[0]), (0, 0)))
            w_out[1] = wts["odd_w_out"]
        h_mix = h
        hn, proj = rms_mm(h, row(nw[l, 2]), w_in[l], swiglu=False, name=f"mix_in_{l}")
        if l == 0:
            o_a = swa_fwd(proj, bias, sinks, name="swa_fwd")
            y = conv_fwd(proj, conv_w, name="conv_fwd")
            o_b, ssave, isave = dn_fwd(y, proj, a_log, dt_bias, dn_norm_w, name="dn_fwd")
            acts, extra = [o_a, o_b], (y, ssave, isave)
        else:
            o, ssave = gla_fwd(proj, w_gate_up, b_gate, gla_norm_w, name="gla_fwd")
            acts, extra = [o], (ssave,)
        mix, h = mm_rms_res(acts, w_out[l], h, row(nw[l, 3]), scale=1.0, name=f"mix_out_{l}")
        s_m = (h_mix, hn, proj, acts, extra, mix)
        h, s_b = _ffn_fwd(h, row(nw[l, 4]), row(nw[l, 5]), wts, 2 * l + 1, get_w)
        saved.append((s_a, s_m, s_b))

    dh, loss = loss_and_grad(h, target, name="loss")

    grads = {}
    dnw = [[None] * 6 for _ in range(2)]
    def on_grads(i):
        def put(which, g):
            grads[f"g_{which}{i}"] = g
            return put_g(f"{which}{i}", grads)
        return put

    for l in (1, 0):
        s_a, s_m, s_b = saved[l]
        i = 2 * l + 1
        dh, dnw[l][4], dnw[l][5] = _ffn_bwd(dh, s_b, row(nw[l, 4]), row(nw[l, 5]), wts[f"w_gu{i}"], wts[f"w_down{i}"],
                                            i, on_grads(i))
        h_mix, hn, proj, acts, extra, mix = s_m
        dmix, dnw[l][3], do = mm_rms_res_bwd(dh, mix, row(nw[l, 3]), w_out[l], None, scale=1.0, name=f"mix_out_bwd_{l}")
        dw_out = jnp.concatenate([mm_tn(a, dmix, name=f"mix_dwo_{l}_{i}") for i, a in enumerate(acts)], axis=0)
        sent = jnp.zeros((1, 1), F32)
        if l == 0:
            y, ssave, isave = extra
            dproj, dbias, dsinks = swa_bwd(proj, bias, sinks, do, name="swa_bwd")
            dy, dproj, da_log, ddt_bias, ddn_norm_w = dn_bwd(y, proj, a_log, dt_bias, dn_norm_w, ssave, isave, do, dproj,
                                                             name="dn_bwd")
            dproj, dconv_w = conv_bwd(proj, conv_w, dy, dproj, name="conv_bwd")
            grads["rel_bias_table"] = swa_bias_bwd(dbias, buckets, name="swa_bias_bwd")[:, :SWA_Q_HEADS]
            grads["swa_sinks"] = dsinks[:, :SWA_Q_HEADS]
            grads["dn_a_log"] = da_log[:, :DN_HEADS]
            grads["dn_dt_bias"] = ddt_bias[:, :DN_HEADS]
            grads["dn_norm_w"] = ddn_norm_w
            grads["even_conv_w"] = dconv_w[None]
            grads["even_w_out"] = dw_out
        else:
            (ssave,) = extra
            dproj, dwgu, dbg, dgnw = gla_bwd(proj, w_gate_up, b_gate, gla_norm_w, ssave, do, name="gla_bwd")
            grads["gla_w_gate_up"] = dwgu[None, :GLA_GATE_RANK]
            grads["gla_b_gate"] = dbg
            grads["gla_norm_w"] = dgnw
            grads["odd_w_out"] = dw_out
        dw_in = mm_tn(dproj, hn, name=f"mix_dwi_{l}")
        if l == 0:
            grads["even_w_in"] = _even_proj_weight_grad(dw_in)
            sent = put_g("even", grads)
        else:
            grads["odd_w_in"] = dw_in[:wts["odd_w_in"].shape[0]]
        dh, dnw[l][2] = rms_mm_bwd([dproj], w_in[l], h_mix, row(nw[l, 2]) + sent, dh, name=f"mix_in_bwd_{l}")
        i = 2 * l
        dh, dnw[l][0], dnw[l][1] = _ffn_bwd(dh, s_a, row(nw[l, 0]), row(nw[l, 1]), wts[f"w_gu{i}"], wts[f"w_down{i}"],
                                            i, on_grads(i))

    grads["norm_w"] = jnp.stack([jnp.concatenate(r, axis=0) for r in dnw])
    grads["meta_tokens"] = dh[PAD:PAD + N_META]
    return loss[0, 0], dh[PAD + N_META:], grads


def _peer(k):
    x, y, c = (lax.axis_index(a) for a in AXES)
    flip = lambda v, bit: 1 - v if bit else v
    return (flip(x, k & 4), flip(y, k & 2), flip(c, k & 1))


def _my_index():
    x, y, c = (lax.axis_index(a) for a in AXES)
    return 4 * x + 2 * y + c


_HBM = pl.BlockSpec(memory_space=pltpu.HBM)
_SEM = pl.BlockSpec(memory_space=pltpu.SEMAPHORE)
_EFFECT = pltpu.SideEffectType.DATAFLOW_SIDE_EFFECTING


def _remote_copies(items, src_refs, land_refs, send_sems, recv_sems):
    me = _my_index()
    copies = []
    for k in range(1, N_DEV):
        px, py, pc = _peer(k)
        pj = 4 * px + 2 * py + pc
        for a, (sn, send, ln, land, _) in enumerate(items):
            sem = (k - 1) * len(items) + a
            copies.append(pltpu.make_async_remote_copy(
                src_ref=send(src_refs[sn], pj), dst_ref=land(land_refs[ln], me), send_sem=send_sems.at[sem],
                recv_sem=recv_sems.at[sem], device_id=(px, py, pc), device_id_type=MESH))
    return copies


def exchange(srcs, lands, items, after, *, name):
    sn, ln = list(srcs), list(lands)

    def body(*refs):
        src_refs = dict(zip(sn, refs[:len(sn)]))
        land_refs = dict(zip(ln, refs[len(sn) + len(ln) + 1:len(sn) + 2 * len(ln) + 1]))
        send_sems, recv_sems = refs[len(sn) + 2 * len(ln) + 1:]
        copies = _remote_copies(items, src_refs, land_refs, send_sems, recv_sems)
        for cp in copies:
            cp.start()
        for cp in copies:
            cp.wait_recv()
        for cp in copies:
            cp.wait_send()

    n_remote = (N_DEV - 1) * len(items)
    outs = pl.pallas_call(
        body, name=name,
        in_specs=[pl.BlockSpec(memory_space=pl.ANY)] * (len(sn) + len(ln) + 1),
        out_specs=tuple(pl.BlockSpec(memory_space=pl.ANY) for _ in ln),
        out_shape=tuple(jax.ShapeDtypeStruct(lands[n].shape, lands[n].dtype) for n in ln),
        input_output_aliases={len(sn) + i: i for i in range(len(ln))},
        scratch_shapes=[pltpu.SemaphoreType.DMA((n_remote,)), pltpu.SemaphoreType.DMA((n_remote,))],
    )(*[srcs[n] for n in sn], *[lands[n] for n in ln], after)
    return dict(zip(ln, outs))


def start_copies(srcs, lands, items, *, name):
    sn, ln = list(srcs), list(lands)
    n_remote = (N_DEV - 1) * len(items)

    def body(*refs):
        src_refs = dict(zip(sn, refs[:len(sn)]))
        land_refs = dict(zip(ln, refs[len(sn):len(sn) + len(ln)]))
        send_sems, recv_sems = refs[len(sn) + len(ln):len(sn) + len(ln) + 2]
        token = refs[-1]
        for cp in _remote_copies(items, src_refs, land_refs, send_sems, recv_sems):
            cp.start()
        token[...] = jnp.zeros_like(token)

    hbm = lambda a: pltpu.with_memory_space_constraint(a, pltpu.HBM)
    outs = pl.pallas_call(
        body, name=name,
        in_specs=[_HBM] * (len(sn) + len(ln)),
        out_specs=(_SEM, _SEM) + (_HBM,) * len(ln) + (pl.BlockSpec(memory_space=pltpu.VMEM),),
        out_shape=(pltpu.SemaphoreType.DMA((n_remote,)), pltpu.SemaphoreType.DMA((n_remote,)))
        + tuple(pltpu.HBM(lands[n].shape, lands[n].dtype) for n in ln) + (jax.ShapeDtypeStruct((8, LANE), F32),),
        input_output_aliases={len(sn) + i: 2 + i for i in range(len(ln))},
        compiler_params=pltpu.CompilerParams(has_side_effects=_EFFECT),
    )(*[hbm(srcs[n]) for n in sn], *[hbm(lands[n]) for n in ln])
    return (outs[0], outs[1]), dict(zip(ln, outs[2:2 + len(ln)])), outs[-1][0:1, 0:1]


def wait_copies(sems, srcs, lands, items, after, *, name):
    sn, ln = list(srcs), list(lands)

    def body(*refs):
        src_refs = dict(zip(sn, refs[:len(sn)]))
        land_refs = dict(zip(ln, refs[len(sn):len(sn) + len(ln)]))
        send_sems, recv_sems = refs[len(sn) + len(ln):len(sn) + len(ln) + 2]
        copies = _remote_copies(items, src_refs, land_refs, send_sems, recv_sems)
        for cp in copies:
            cp.wait_send()
        for cp in copies:
            cp.wait_recv()

    outs = pl.pallas_call(
        body, name=name,
        in_specs=[_HBM] * (len(sn) + len(ln)) + [_SEM, _SEM, pl.BlockSpec(memory_space=pl.ANY)],
        out_specs=(_HBM,) * len(ln),
        out_shape=tuple(pltpu.HBM(lands[n].shape, lands[n].dtype) for n in ln),
        input_output_aliases={len(sn) + i: i for i in range(len(ln))},
        compiler_params=pltpu.CompilerParams(has_side_effects=_EFFECT),
    )(*[srcs[n] for n in sn], *[lands[n] for n in ln], sems[0], sems[1], after)
    return dict(zip(ln, outs))


def _block(index, size, base=0):
    return pl.ds(pl.multiple_of(base + index * size, ROW_TILE), size)


def _adam_tile(rows):
    for t in (256, 176, 128):
        if rows % t == 0:
            return t
    return rows


def sum_adamw(recvs, w, m, v, *, name, first_slab=0, into=None):
    _, r, c = w.shape
    b = len(recvs)
    rp = recvs[0].shape[1]
    whole = r % ROW_TILE != 0
    tr = r if whole else _adam_tile(r)
    c1 = 1.0 / (1.0 - ADAM_B1 ** ADAM_STEP)
    c2 = 1.0 / (1.0 - ADAM_B2 ** ADAM_STEP)

    def body(*refs):
        recv_refs = refs[:b]
        w_ref, m_ref, v_ref = refs[b:b + 3]
        g_ref, d_ref, nm_ref, nv_ref = refs[b + 3 + (0 if into is None else 4):][:4]
        for slab, recv_ref in enumerate(recv_refs):
            @pl.when(pl.program_id(0) == slab)
            def _():
                g = recv_ref[0].astype(F32)
                for i in range(1, N_DEV):
                    g = g + recv_ref[i].astype(F32)
                if whole:
                    sum_ref = refs[-1]
                    sum_ref[...] = g
                    g = sum_ref[0:r, :]
                nm = ADAM_B1 * m_ref[0] + (1.0 - ADAM_B1) * g
                nv = ADAM_B2 * v_ref[0] + (1.0 - ADAM_B2) * (g * g)
                g_ref[0] = g
                nm_ref[0] = nm
                nv_ref[0] = nv
                d_ref[0] = -ADAM_LR * ((nm * c1) / (jnp.sqrt(nv * c2) + ADAM_EPS) + ADAM_WD * w_ref[0])

    tile = pl.BlockSpec((1, tr, c), lambda bi, i: (first_slab + bi, i, 0))
    piece = lambda slab: pl.BlockSpec((N_DEV, rp if whole else tr, c), lambda bi, i: (0, jnp.where(bi == slab, i, 0), 0))
    earlier = [] if into is None else list(into)
    return pl.pallas_call(
        body, name=name, grid=(b, r // tr),
        in_specs=[piece(slab) for slab in range(b)] + [tile, tile, tile] + [pl.BlockSpec(memory_space=pl.ANY)] * len(earlier),
        out_specs=(tile,) * 4, out_shape=(jax.ShapeDtypeStruct(w.shape, F32),) * 4,
        input_output_aliases={b + 3 + i: i for i in range(len(earlier))},
        scratch_shapes=[pltpu.VMEM((rp, c), F32)] if whole else [],
    )(*recvs, w, m, v, *earlier)


def _flat_rows(n_elems, row_multiple):
    rows = -(-n_elems // FLAT_COLS)
    return -(-rows // row_multiple) * row_multiple


def _pack(arrays, row_multiple, dtype):
    flat = jnp.concatenate([a.reshape(-1).astype(dtype) for a in arrays])
    rows = _flat_rows(flat.size, row_multiple)
    return jnp.pad(flat, (0, rows * FLAT_COLS - flat.size)).reshape(rows, FLAT_COLS)


def _unpack(flat2d, shapes):
    lead = flat2d.shape[:-2]
    flat = flat2d.reshape(lead + (-1,))
    out, off = [], 0
    for shp in shapes:
        n = int(np.prod(shp))
        out.append(flat[..., off:off + n].reshape(lead + tuple(shp)))
        off += n
    return out


def _join_shards(stacked, axis):
    moved = jnp.moveaxis(stacked, 0, axis)
    shp = list(moved.shape)
    shp[axis:axis + 2] = [shp[axis] * shp[axis + 1]]
    return moved.reshape(shp)


def _split_shards(full, axis):
    shp = list(full.shape)
    shp[axis:axis + 1] = [N_DEV, shp[axis] // N_DEV]
    return jnp.moveaxis(full.reshape(shp), axis, 0)


def kernel(x, meta_tokens, norm_w, ffn_w_gate, ffn_w_up, ffn_w_down, rel_bias_table, even_w_in, even_conv_w, swa_sinks, dn_a_log, dn_dt_bias, dn_norm_w, even_w_out, odd_w_in, gla_w_gate_up, gla_b_gate, gla_norm_w, odd_w_out, loss_target, m_meta_tokens, m_norm_w, m_ffn_w_gate, m_ffn_w_up, m_ffn_w_down, m_rel_bias_table, m_even_w_in, m_even_conv_w, m_swa_sinks, m_dn_a_log, m_dn_dt_bias, m_dn_norm_w, m_even_w_out, m_odd_w_in, m_gla_w_gate_up, m_gla_b_gate, m_gla_norm_w, m_odd_w_out, v_meta_tokens, v_norm_w, v_ffn_w_gate, v_ffn_w_up, v_ffn_w_down, v_rel_bias_table, v_even_w_in, v_even_conv_w, v_swa_sinks, v_dn_a_log, v_dn_dt_bias, v_dn_norm_w, v_even_w_out, v_odd_w_in, v_gla_w_gate_up, v_gla_b_gate, v_gla_norm_w, v_odd_w_out):
    args = locals()
    w = {n: args[n] for n in WEIGHTS}
    m = {n: args["m_" + n] for n in WEIGHTS}
    v = {n: args["v_" + n] for n in WEIGHTS}

    d = D_MODEL
    me = _my_index()
    whole = lambda ref, j: ref
    rows = lambda size, base=0: (lambda ref, i: ref.at[_block(i, size, base), :])
    lead = lambda ref, i: ref.at[i]
    of_group = lambda items, g: [it for it in items if it[4] == g]
    names = lambda items, k: list(dict.fromkeys(it[k] for it in items))

    def placed(shape, dtype, parts):
        land = lax.empty(shape, dtype)
        for part, axis, start in parts:
            land = lax.dynamic_update_slice(land, part, tuple(start if a == axis else 0 for a in range(land.ndim)))
        return land

    as_rows = lambda a: jnp.swapaxes(a, -1, -2)
    pad_rows = lambda a, to: jnp.pad(a, [(0, 0)] * (a.ndim - 2) + [(0, to - a.shape[-2]), (0, 0)])
    gate_s = pad_rows(as_rows(w["ffn_w_gate"].reshape(N_FFN, d, FF_SHARD)), FF_SHARD_PAD).astype(BF16)
    up_s = pad_rows(as_rows(w["ffn_w_up"].reshape(N_FFN, d, FF_SHARD)), FF_SHARD_PAD).astype(BF16)
    down_s = pad_rows(w["ffn_w_down"].reshape(N_FFN, FF_SHARD, d), FF_SHARD_PAD).astype(BF16)
    small_s = _pack([w[n] for n in SMALL], 8, F32)
    srcs_w = {"ein": pad_rows(as_rows(w["even_w_in"][0]), EVEN_IN_SHARD_PAD).astype(BF16),
              "oin": pad_rows(as_rows(w["odd_w_in"][0]), ODD_IN_SHARD_PAD).astype(BF16),
              "eout": w["even_w_out"][0].astype(BF16), "oout": w["odd_w_out"][0].astype(BF16), "small": small_s}
    lands_w = {"ein": placed((N_DEV * EVEN_IN_SHARD_PAD, d), BF16, [(srcs_w["ein"], 0, me * EVEN_IN_SHARD_PAD)]),
               "oin": placed((N_DEV * ODD_IN_SHARD_PAD, d), BF16, [(srcs_w["oin"], 0, me * ODD_IN_SHARD_PAD)]),
               "eout": placed((d, d), BF16, [(srcs_w["eout"], 0, me * OUT_SHARD)]),
               "oout": placed((d, d), BF16, [(srcs_w["oout"], 0, me * OUT_SHARD)]),
               "small": placed((N_DEV,) + small_s.shape, F32, [(small_s[None], 0, me)])}
    items_w = [("small", whole, "small", lead, "first"), ("ein", whole, "ein", rows(EVEN_IN_SHARD_PAD), "even"),
               ("eout", whole, "eout", rows(OUT_SHARD), "even"), ("oin", whole, "oin", rows(ODD_IN_SHARD_PAD), "odd"),
               ("oout", whole, "oout", rows(OUT_SHARD), "odd")]
    for i, group, down_group in ((0, "first", "down0"), (1, "ffn1", "ffn1"), (2, "ffn2", "down2"), (3, "ffn3", "ffn3")):
        srcs_w.update({f"gate{i}": gate_s[i], f"up{i}": up_s[i], f"down{i}": down_s[i]})
        lands_w[f"w_gu{i}"] = placed((2 * FF_PAD, d), BF16, [(srcs_w[f"gate{i}"], 0, me * FF_SHARD_PAD),
                                                             (srcs_w[f"up{i}"], 0, FF_PAD + me * FF_SHARD_PAD)])
        lands_w[f"w_down{i}"] = placed((FF_PAD, d), BF16, [(srcs_w[f"down{i}"], 0, me * FF_SHARD_PAD)])
        items_w += [(f"gate{i}", whole, f"w_gu{i}", rows(FF_SHARD_PAD), group),
                    (f"up{i}", whole, f"w_gu{i}", rows(FF_SHARD_PAD, FF_PAD), group),
                    (f"down{i}", whole, f"w_down{i}", rows(FF_SHARD_PAD), down_group)]
    pending, started = {}, []
    for g in ("first", "down0", "even", "ffn1", "ffn2", "down2", "odd", "ffn3"):
        its = of_group(items_w, g)
        srcs = {n: srcs_w[n] for n in names(its, 0)}
        sems, lands, token = start_copies(srcs, {n: lands_w[n] for n in names(its, 2)}, its, name=f"gather_start_{g}")
        pending[g] = (sems, srcs, lands, its)
        started.append(token)

    unpad = lambda p, shard, shard_pad: p.reshape(N_DEV, shard_pad, d)[:, :shard].reshape(N_DEV * shard, d)

    def get_w(stage, after):
        if stage not in pending:
            return {}
        sems, srcs, lands, its = pending[stage]
        landed = wait_copies(sems, srcs, lands, its, after, name=f"gather_wait_{stage}")
        got = {}
        for n, arr in landed.items():
            if n == "small":
                for sn, stacked in zip(SMALL, _unpack(arr, [w[sn].shape for sn in SMALL])):
                    got[sn] = _join_shards(stacked, SHARD_AXIS[sn])
            elif n == "ein":
                got["even_w_in"] = unpad(arr, EVEN_IN_SHARD, EVEN_IN_SHARD_PAD)
            elif n == "oin":
                got["odd_w_in"] = unpad(arr, ODD_IN_SHARD, ODD_IN_SHARD_PAD)
            elif n in ("eout", "oout"):
                got["even_w_out" if n == "eout" else "odd_w_out"] = arr
            else:
                got[n] = (arr, None)
        return got

    full = {n: w[n] for n in REPL}
    full.update(get_w("first", sum(started)))

    repad = lambda g, shard, shard_pad: pad_rows(g.reshape(N_DEV, shard, d), shard_pad)
    pieces_g = {"r_oin": ("oin", None, "gu2"), "r_oout": ("oout", (OUT_SHARD, 0), "gu2"),
                "r_ein": ("ein", None, "even"), "r_eout": ("eout", (OUT_SHARD, 0), "even"), "r_small": ("small", None, "last")}
    for i in range(N_FFN):
        pieces_g.update({f"r_gate{i}": (f"g_gu{i}", (FF_SHARD_PAD, 0), f"gu{i}"),
                         f"r_up{i}": (f"g_gu{i}", (FF_SHARD_PAD, FF_PAD), f"gu{i}"),
                         f"r_down{i}": (f"g_down{i}", (FF_SHARD_PAD, 0), "down0" if i == 0 else f"gu{i}")})
    items_g = [(src, lead if blk is None else rows(*blk), land, lead, group) for land, (src, blk, group) in pieces_g.items()]
    last_groups = ("down0", "gu0")

    def grad_src(n, grads):
        if n == "oin":
            return repad(grads["odd_w_in"], ODD_IN_SHARD, ODD_IN_SHARD_PAD).astype(BF16)
        if n == "ein":
            return repad(grads["even_w_in"], EVEN_IN_SHARD, EVEN_IN_SHARD_PAD).astype(BF16)
        if n in ("oout", "eout"):
            return grads["odd_w_out" if n == "oout" else "even_w_out"].astype(BF16)
        return grads[n]

    def grad_land(n, srcs):
        src, blk, _ = pieces_g[n]
        if blk is None:
            own = lax.dynamic_index_in_dim(srcs[src], me, 0, keepdims=False)
        else:
            own = lax.dynamic_slice_in_dim(srcs[src], blk[1] + me * blk[0], blk[0], 0)
        return placed((N_DEV,) + own.shape, own.dtype, [(own[None], 0, me)])

    sent = {}

    def put_g(stage, grads):
        its = of_group(items_g, stage)
        if not its:
            return jnp.zeros((1, 1), F32)
        srcs = {n: grad_src(n, grads) for n in names(its, 0)}
        lands = {n: grad_land(n, srcs) for n in names(its, 2)}
        sems, lands, token = start_copies(srcs, lands, its, name=f"grads_start_{stage}")
        sent[stage] = (sems, srcs, lands, its)
        return token

    loss, grad_x, grads = local_step(x[0], loss_target[0], full, get_w, put_g)
    loss = lax.psum(loss, AXES)

    order = SMALL + REPL
    pieces = [_split_shards(grads[n].reshape(full[n].shape), SHARD_AXIS[n]) if n in SHARD_AXIS
              else jnp.broadcast_to(grads[n].reshape(w[n].shape)[None], (N_DEV,) + w[n].shape) for n in order]
    flat = jnp.concatenate([p.reshape(N_DEV, -1) for p in pieces], axis=1)
    srows = _flat_rows(flat.shape[1], 8)
    grads["small"] = jnp.pad(flat, ((0, 0), (0, srows * FLAT_COLS - flat.shape[1]))).reshape(N_DEV, srows, FLAT_COLS)
    recv = {}
    for stage, (sems, srcs, lands, its) in sent.items():
        if stage not in last_groups:
            recv.update(wait_copies(sems, srcs, lands, its, grad_x, name=f"grads_wait_{stage}"))
    result = [{} for _ in range(4)]

    views = {"ffn_w_gate": (lambda a: as_rows(a.reshape(N_FFN, d, FF_SHARD)), lambda o, n: as_rows(o).reshape(w[n].shape)),
             "ffn_w_up": (lambda a: as_rows(a.reshape(N_FFN, d, FF_SHARD)), lambda o, n: as_rows(o).reshape(w[n].shape)),
             "ffn_w_down": (lambda a: a.reshape(N_FFN, FF_SHARD, d), lambda o, n: o.reshape(w[n].shape)),
             "even_w_in": (as_rows, lambda o, n: as_rows(o)), "odd_w_in": (as_rows, lambda o, n: as_rows(o)),
             "even_w_out": (lambda a: a, lambda o, n: o), "odd_w_out": (lambda a: a, lambda o, n: o)}

    def adam(n, recvs, first_slab=0, into=None):
        view = views[n][0]
        return sum_adamw(recvs, view(w[n]), view(m[n]), view(v[n]), name=f"adamw_{n}_{first_slab}",
                         first_slab=first_slab, into=into)

    def finish(n, outs):
        for r, o in zip(result, outs):
            r[n] = views[n][1](o, n)

    ffn_recv = (("ffn_w_gate", "r_gate"), ("ffn_w_up", "r_up"), ("ffn_w_down", "r_down"))
    early = {n: adam(n, [recv[f"{r}{i}"] for i in (1, 2, 3)], first_slab=1) for n, r in ffn_recv}
    for n, r in (("even_w_in", "r_ein"), ("odd_w_in", "r_oin"), ("even_w_out", "r_eout"), ("odd_w_out", "r_oout")):
        finish(n, adam(n, [recv[r]]))
    srcs = {"small": grads["small"]}
    recv.update(exchange(srcs, {"r_small": grad_land("r_small", srcs)}, of_group(items_g, "last"),
                         early["ffn_w_down"][0], name="exchange_small"))
    for stage in last_groups:
        sems, srcs, lands, its = sent[stage]
        recv.update(wait_copies(sems, srcs, lands, its, recv["r_small"], name=f"grads_wait_{stage}"))
    for n, r in ffn_recv:
        finish(n, adam(n, [recv[f"{r}0"]], into=early[n]))
    pack_local = lambda t: _pack([t[n] for n in order], 8, F32)[None]
    small_outs = sum_adamw([recv["r_small"]], pack_local(w), pack_local(m), pack_local(v), name="adamw_small")
    for r, o in zip(result, small_outs):
        r.update(zip(order, _unpack(o[0], [w[n].shape for n in order])))
    return (loss, grad_x[None], *[r[n] for r in result for n in WEIGHTS])
```
